```python
import math
import jax, jax.numpy as jnp
from jax import lax
import numpy as np

D_MODEL = 1024
BATCH = 8
SEQ = 4096
DEPTH = 1

D_MIX = D_MODEL
DN_WIDTH = D_MIX // 2
DN_HEADS = 4
DN_HEAD_DIM = DN_WIDTH // DN_HEADS
CONV_K = 4
DN_CHUNK = 64
SGU_WIDTH = D_MIX - DN_WIDTH
SGU_GROUPS = 4
SGU_GROUP_DIM = SGU_WIDTH // SGU_GROUPS
SGU_CHUNK = 128
IN_COLS = 4 * DN_WIDTH + 2 * DN_HEADS + 2 * SGU_WIDTH
MOE_GROUPS = 8
EXPERTS_PER_GROUP = 8
N_EXPERTS = MOE_GROUPS * EXPERTS_PER_GROUP
TOP_K = 2
D_EXPERT = D_MODEL // 2
MOE_BLOCK = 128
DEEPNORM_ALPHA = (2.0 * DEPTH) ** 0.25
DEEPNORM_BETA = (8.0 * DEPTH) ** -0.25
LN_EPS = 1e-5
RMS_EPS = 1e-6

kernel_name = "hybrid_deltanet_sgu_hmoe_deepnorm"


def _layernorm(x, g, b):
    xf = x.astype(jnp.float32)
    mu = jnp.mean(xf, axis=-1, keepdims=True)
    xc = xf - mu
    var = jnp.mean(xc * xc, axis=-1, keepdims=True)
    return (xc * lax.rsqrt(var + LN_EPS) * g + b).astype(x.dtype)


def _l2norm(x):
    return x * lax.rsqrt(jnp.sum(x * x, axis=-1, keepdims=True) + RMS_EPS)


def _causal_conv_silu(x, w):
    y = lax.conv_general_dilated(
        x, w[:, None, :], window_strides=(1,), padding=[(CONV_K - 1, 0)],
        dimension_numbers=("NWC", "WIO", "NWC"), feature_group_count=x.shape[-1])
    return jax.nn.silu(y)


def _chunked_gated_delta_rule(q, k, v, g, beta):
    B, T, H, dk = q.shape
    dv = v.shape[-1]
    C = DN_CHUNK
    NC = T // C

    def to_chunks(t):
        return t.reshape(B, NC, C, H, t.shape[-1]).transpose(0, 3, 1, 2, 4)

    q, k, v = to_chunks(q), to_chunks(k), to_chunks(v)
    g = g.reshape(B, NC, C, H).transpose(0, 3, 1, 2)
    beta = beta.reshape(B, NC, C, H).transpose(0, 3, 1, 2)
    gc = jnp.cumsum(g, axis=-1)

    pos = jnp.arange(C)
    causal = pos[:, None] >= pos[None, :]
    strict = pos[:, None] > pos[None, :]
    decay = jnp.exp(jnp.where(causal, gc[..., :, None] - gc[..., None, :], -jnp.inf))

    kb = k * beta[..., None]
    kkt = jnp.einsum("bhnid,bhnjd->bhnij", kb, k) * decay
    a_mat = jnp.eye(C, dtype=jnp.float32) + jnp.where(strict, kkt, 0.0)
    rhs = jnp.concatenate([v * beta[..., None], kb * jnp.exp(gc)[..., None]], axis=-1)
    sol = lax.linalg.triangular_solve(a_mat, rhs, left_side=True, lower=True, unit_diagonal=True)
    u_vals, w_dec = sol[..., :dv], sol[..., dv:]

    qk_intra = jnp.einsum("bhnid,bhnjd->bhnij", q, k) * decay
    q_dec = q * jnp.exp(gc)[..., None]
    k_dec = k * jnp.exp(gc[..., -1:] - gc)[..., None]
    g_last = jnp.exp(gc[..., -1])

    xs = tuple(jnp.moveaxis(t, 2, 0) for t in (q_dec, k_dec, u_vals, w_dec, qk_intra, g_last))

    def step(S, inp):
        qd, kd, u, w, a_intra, gl = inp
        v_new = u - jnp.einsum("bhck,bhkv->bhcv", w, S)
        o = jnp.einsum("bhck,bhkv->bhcv", qd, S) + jnp.einsum("bhij,bhjv->bhiv", a_intra, v_new)
        S = S * gl[..., None, None] + jnp.einsum("bhck,bhcv->bhkv", kd, v_new)
        return S, o

    S0 = jnp.zeros((B, H, dk, dv), jnp.float32)
    _, o = lax.scan(step, S0, xs)
    return o.transpose(1, 0, 3, 2, 4).reshape(B, T, H, dv)


def _gated_deltanet(qkv, z, b_raw, a_raw, conv_w, a_log, dt_bias, norm_w):
    B, T, _ = qkv.shape
    qkv = _causal_conv_silu(qkv.astype(jnp.float32), conv_w.astype(jnp.float32))
    q, k, v = jnp.split(qkv, 3, axis=-1)
    q = _l2norm(q.reshape(B, T, DN_HEADS, DN_HEAD_DIM)) * (DN_HEAD_DIM ** -0.5)
    k = _l2norm(k.reshape(B, T, DN_HEADS, DN_HEAD_DIM))
    v = v.reshape(B, T, DN_HEADS, DN_HEAD_DIM)
    beta = jax.nn.sigmoid(b_raw.astype(jnp.float32))
    g = -jnp.exp(a_log.astype(jnp.float32)) * jax.nn.softplus(a_raw.astype(jnp.float32) + dt_bias.astype(jnp.float32))
    o = _chunked_gated_delta_rule(q, k, v, g, beta)
    zf = z.astype(jnp.float32).reshape(B, T, DN_HEADS, DN_HEAD_DIM)
    o = o * lax.rsqrt(jnp.mean(o * o, axis=-1, keepdims=True) + RMS_EPS) * norm_w * jax.nn.silu(zf)
    return o.reshape(B, T, DN_WIDTH)


def _spatial_gating(u, v, ln_g, ln_b, w_spatial, b_spatial):
    B, T, _ = u.shape
    n_chunks = T // SGU_CHUNK
    u = jax.nn.gelu(u)
    v = jax.nn.gelu(v).reshape(B, n_chunks, SGU_CHUNK, SGU_GROUPS, SGU_GROUP_DIM)
    v = _layernorm(v, ln_g.reshape(SGU_GROUPS, SGU_GROUP_DIM), ln_b.reshape(SGU_GROUPS, SGU_GROUP_DIM))
    mask = jnp.tril(jnp.ones((SGU_CHUNK, SGU_CHUNK), dtype=bool))
    ws = jnp.where(mask, w_spatial, 0.0).astype(v.dtype)
    mixed = jnp.einsum("gts,bnsgd->bntgd", ws, v) + b_spatial.T[None, None, :, :, None].astype(v.dtype)
    return u * mixed.reshape(B, T, SGU_WIDTH)


def _mixer(h, w_in, conv_w, a_log, dt_bias, dn_norm_w, sgu_ln_g, sgu_ln_b, w_spatial, b_spatial, w_out):
    proj = h @ w_in
    splits = [3 * DN_WIDTH, 4 * DN_WIDTH, 4 * DN_WIDTH + DN_HEADS,
              4 * DN_WIDTH + 2 * DN_HEADS, 4 * DN_WIDTH + 2 * DN_HEADS + SGU_WIDTH]
    qkv, z, b_raw, a_raw, u, v = jnp.split(proj, splits, axis=-1)
    y_dn = _gated_deltanet(qkv, z, b_raw, a_raw, conv_w, a_log, dt_bias, dn_norm_w).astype(h.dtype)
    y_sgu = _spatial_gating(u, v, sgu_ln_g, sgu_ln_b, w_spatial, b_spatial)
    return jnp.concatenate([y_dn, y_sgu], axis=-1) @ w_out


def _hierarchical_moe(h, w_rg, b_rg, w_re, b_re, w_gate, w_up, w_down):
    B, T, D = h.shape
    xf = h.reshape(B * T, D)
    N = xf.shape[0]
    group_logits = (xf @ w_rg).astype(jnp.float32) + b_rg.astype(jnp.float32)
    g_idx = jnp.argmax(group_logits, axis=-1)
    p_group = jnp.take_along_axis(jax.nn.softmax(group_logits, axis=-1), g_idx[:, None], axis=-1)
    exp_logits = ((xf @ w_re).astype(jnp.float32) + b_re.astype(jnp.float32)).reshape(N, MOE_GROUPS, EXPERTS_PER_GROUP)
    within = jnp.take_along_axis(exp_logits, g_idx[:, None, None], axis=1)[:, 0]
    top_vals, top_idx = lax.top_k(within, TOP_K)
    weights = p_group * jax.nn.softmax(top_vals, axis=-1)
    expert_ids = g_idx[:, None] * EXPERTS_PER_GROUP + top_idx

    A = N * TOP_K
    e_flat = expert_ids.reshape(A).astype(jnp.int32)
    tok_flat = jnp.repeat(jnp.arange(N, dtype=jnp.int32), TOP_K)
    w_flat = weights.reshape(A)
    order = jnp.argsort(e_flat)
    e_sorted, tok_sorted, w_sorted = e_flat[order], tok_flat[order], w_flat[order]
    counts = jnp.zeros((N_EXPERTS,), jnp.int32).at[e_flat].add(1)
    padded = ((counts + MOE_BLOCK - 1) // MOE_BLOCK) * MOE_BLOCK
    starts = jnp.cumsum(counts) - counts
    pends = jnp.cumsum(padded)
    pstarts = pends - padded
    dest = pstarts[e_sorted] + (jnp.arange(A, dtype=jnp.int32) - starts[e_sorted])
    P = (-(-A // MOE_BLOCK)) * MOE_BLOCK + N_EXPERTS * MOE_BLOCK
    NB = P // MOE_BLOCK
    row_tok = jnp.full((P,), N, jnp.int32).at[dest].set(tok_sorted)
    row_w = jnp.zeros((P,), h.dtype).at[dest].set(w_sorted.astype(h.dtype))
    block_start = jnp.arange(NB, dtype=jnp.int32) * MOE_BLOCK
    block_e = jnp.minimum(jnp.searchsorted(pends, block_start, side="right"), N_EXPERTS - 1)

    x_pad = jnp.concatenate([xf, jnp.zeros((1, D), xf.dtype)], axis=0)
    xs = x_pad[row_tok].reshape(NB, MOE_BLOCK, D)

    def expert_block(args):
        xb, e = args
        hid = jax.nn.silu(xb @ w_gate[e]) * (xb @ w_up[e])
        return hid @ w_down[e]

    ys = lax.map(expert_block, (xs, block_e)).reshape(P, D) * row_w[:, None]
    out = jnp.zeros((N + 1, D), h.dtype).at[row_tok].add(ys)[:N]
    return out.reshape(B, T, D)


def setup_inputs(seed: int = 0) -> dict:
    key = jax.random.key(seed)
    ks = jax.random.split(key, 24)
    f32 = jnp.float32
    L = DEPTH

    def nrm(k, shape, scale):
        return jax.random.normal(k, shape, f32) * scale

    x = jax.random.normal(ks[0], (BATCH, SEQ, D_MODEL), f32)
    w_in = nrm(ks[1], (L, D_MODEL, IN_COLS), D_MODEL ** -0.5)
    conv_w = nrm(ks[2], (L, CONV_K, 3 * DN_WIDTH), CONV_K ** -0.5)
    a_log = jnp.log(jax.random.uniform(ks[3], (L, DN_HEADS), f32, 1.0, 16.0))
    dt = jnp.exp(jax.random.uniform(ks[4], (L, DN_HEADS), f32, math.log(1e-3), math.log(1e-1)))
    dt_bias = dt + jnp.log(-jnp.expm1(-dt))
    dn_norm_w = 1.0 + nrm(ks[5], (L, DN_HEAD_DIM), 0.02)
    sgu_ln_g = 1.0 + nrm(ks[6], (L, SGU_WIDTH), 0.02)
    sgu_ln_b = nrm(ks[7], (L, SGU_WIDTH), 0.02)
    tri = jnp.tril(jnp.ones((SGU_CHUNK, SGU_CHUNK), f32))
    w_spatial = nrm(ks[8], (L, SGU_GROUPS, SGU_CHUNK, SGU_CHUNK), SGU_CHUNK ** -0.5) * tri
    b_spatial = 1.0 + nrm(ks[9], (L, SGU_GROUPS, SGU_CHUNK), 0.02)
    w_out = nrm(ks[10], (L, D_MIX, D_MODEL), DEEPNORM_BETA * D_MIX ** -0.5)
    ln1_g = 1.0 + nrm(ks[11], (L, D_MODEL), 0.02)
    ln1_b = nrm(ks[12], (L, D_MODEL), 0.02)
    w_router_group = nrm(ks[13], (L, D_MODEL, MOE_GROUPS), D_MODEL ** -0.5)
    b_router_group = nrm(ks[14], (L, MOE_GROUPS), 0.01)
    w_router_expert = nrm(ks[15], (L, D_MODEL, N_EXPERTS), D_MODEL ** -0.5)
    b_router_expert = nrm(ks[16], (L, N_EXPERTS), 0.01)
    w_gate = nrm(ks[17], (L, N_EXPERTS, D_MODEL, D_EXPERT), D_MODEL ** -0.5)
    w_up = nrm(ks[18], (L, N_EXPERTS, D_MODEL, D_EXPERT), D_MODEL ** -0.5)
    w_down = nrm(ks[19], (L, N_EXPERTS, D_EXPERT, D_MODEL), DEEPNORM_BETA * D_EXPERT ** -0.5)
    ln2_g = 1.0 + nrm(ks[20], (L, D_MODEL), 0.02)
    ln2_b = nrm(ks[21], (L, D_MODEL), 0.02)
    return {"x": x, "w_in": w_in, "conv_w": conv_w, "a_log": a_log, "dt_bias": dt_bias,
            "dn_norm_w": dn_norm_w, "sgu_ln_g": sgu_ln_g, "sgu_ln_b": sgu_ln_b,
            "w_spatial": w_spatial, "b_spatial": b_spatial, "w_out": w_out,
            "ln1_g": ln1_g, "ln1_b": ln1_b,
            "w_router_group": w_router_group, "b_router_group": b_router_group,
            "w_router_expert": w_router_expert, "b_router_expert": b_router_expert,
            "w_gate": w_gate, "w_up": w_up, "w_down": w_down,
            "ln2_g": ln2_g, "ln2_b": ln2_b}


def reference(x, w_in, conv_w, a_log, dt_bias, dn_norm_w, sgu_ln_g, sgu_ln_b, w_spatial, b_spatial,
              w_out, ln1_g, ln1_b, w_router_group, b_router_group, w_router_expert, b_router_expert,
              w_gate, w_up, w_down, ln2_g, ln2_b):
    h = x
    for l in range(DEPTH):
        mix = _mixer(h, w_in[l], conv_w[l], a_log[l], dt_bias[l], dn_norm_w[l], sgu_ln_g[l], sgu_ln_b[l],
                     w_spatial[l], b_spatial[l], w_out[l])
        h = _layernorm(DEEPNORM_ALPHA * h + mix, ln1_g[l], ln1_b[l])
        ffn = _hierarchical_moe(h, w_router_group[l], b_router_group[l], w_router_expert[l], b_router_expert[l],
                                w_gate[l], w_up[l], w_down[l])
        h = _layernorm(DEEPNORM_ALPHA * h + ffn, ln2_g[l], ln2_b[l])
    return h
```

```python
import functools

import jax
import jax.numpy as jnp
from jax import lax
from jax.experimental import pallas as pl
from jax.experimental.pallas import tpu as pltpu

f32 = jnp.float32
bf16 = jnp.bfloat16
i32 = jnp.int32

D_MODEL = 1024
DN_WIDTH = 512
DN_HEADS = 4
HEAD_DIM = 128
CONV_K = 4
SGU_WIDTH = 512
SGU_GROUPS = 4
SGU_CHUNK = 128
DN_CHUNK = 128
MOE_GROUPS = 8
EXPERTS_PER_GROUP = 8
N_EXPERTS = 64
D_EXPERT = 512
MOE_BLOCK = 128
DEEPNORM_ALPHA = 2.0 ** 0.25
LN_EPS = 1e-5
RMS_EPS = 1e-6
HIGHEST = lax.Precision.HIGHEST
VMEM_LIMIT_BYTES = 56 * 1024 * 1024

NT_DIMS = (((1,), (1,)), ((), ()))


def _cparams(sem):
    return pltpu.CompilerParams(dimension_semantics=sem, vmem_limit_bytes=VMEM_LIMIT_BYTES)


def _sigmoid(x):
    return 1.0 / (1.0 + jnp.exp(-x))


def _softplus(x):
    return jnp.maximum(x, 0.0) + jnp.log1p(jnp.exp(-jnp.abs(x)))


def _gelu_tanh(x):
    c = 0.7978845608028654
    return x * (0.5 * (1.0 + jnp.tanh(c * (x + 0.044715 * (x * x * x)))))


def _iota2(shape, axis):
    return lax.broadcasted_iota(i32, shape, axis)


def _inproj_body(x_ref, w_ref, wbat_ref, convw_ref, pcol_ref, prow_ref, lng_ref, lnb_ref,
                 q_ref, k_ref, v_ref, z_ref, u_ref, vln_ref, gcol_ref, grow_ref, ext_ref, *, tm):
    @pl.when(pl.program_id(1) == 0)
    def _():
        ext_ref[0:8, :] = jnp.zeros((8, 3 * DN_WIDTH), f32)

    xb = x_ref[0].astype(bf16)

    for part, out_ref in enumerate((q_ref, k_ref, v_ref)):
        c0 = part * DN_WIDTH
        pre = jnp.dot(xb, w_ref[:, c0:c0 + DN_WIDTH], preferred_element_type=f32)
        ext_ref[8:8 + tm, c0:c0 + DN_WIDTH] = pre
        y = convw_ref[3:4, c0:c0 + DN_WIDTH] * pre
        for j in range(CONV_K - 1):
            y = y + convw_ref[j:j + 1, c0:c0 + DN_WIDTH] * ext_ref[5 + j:5 + j + tm, c0:c0 + DN_WIDTH]
        y = y * _sigmoid(y)
        if part < 2:
            scale = HEAD_DIM ** -0.5 if part == 0 else 1.0
            for h in range(DN_HEADS):
                yh = y[:, h * HEAD_DIM:(h + 1) * HEAD_DIM]
                ss = jnp.sum(yh * yh, axis=-1, keepdims=True)
                out_ref[0, :, h * HEAD_DIM:(h + 1) * HEAD_DIM] = (yh * lax.rsqrt(ss + RMS_EPS) * scale).astype(bf16)
        else:
            out_ref[0] = y.astype(bf16)
    ext_ref[0:8, :] = ext_ref[tm:tm + 8, :]

    zc = 3 * DN_WIDTH
    pz = jnp.dot(xb, w_ref[:, zc:zc + DN_WIDTH], preferred_element_type=f32)
    z_ref[0] = (pz * _sigmoid(pz)).astype(bf16)

    uc = zc + DN_WIDTH
    pu = jnp.dot(xb, w_ref[:, uc:uc + SGU_WIDTH], preferred_element_type=f32)
    u_ref[0] = _gelu_tanh(pu).astype(bf16)
    vc = uc + SGU_WIDTH
    pv = _gelu_tanh(jnp.dot(xb, w_ref[:, vc:vc + SGU_WIDTH], preferred_element_type=f32))
    for g in range(SGU_GROUPS):
        sl = slice(g * SGU_CHUNK, (g + 1) * SGU_CHUNK)
        vg = pv[:, sl]
        mu = jnp.mean(vg, axis=-1, keepdims=True)
        vcn = vg - mu
        var = jnp.mean(vcn * vcn, axis=-1, keepdims=True)
        vln_ref[0, :, sl] = (vcn * lax.rsqrt(var + LN_EPS) * lng_ref[:, sl] + lnb_ref[:, sl]).astype(bf16)

    bc = vc + SGU_WIDTH
    pba = jnp.dot(xb, w_ref[:, bc:bc + 128], preferred_element_type=f32)
    lane = _iota2((DN_CHUNK, 128), 1)
    is_g = (lane >= DN_HEADS) & (lane < 2 * DN_HEADS)
    beta = _sigmoid(pba)
    gval = -jnp.exp(pcol_ref[0:1, :]) * _softplus(pba + pcol_ref[1:2, :])
    ci = _iota2((DN_CHUNK, DN_CHUNK), 0)
    cj = _iota2((DN_CHUNK, DN_CHUNK), 1)
    ltri = (ci >= cj).astype(f32)
    utri = (ci <= cj).astype(f32)
    pbat = lax.dot_general(wbat_ref[...], xb, NT_DIMS, preferred_element_type=f32)
    sub = _iota2((8, DN_CHUNK), 0)
    for c in range(tm // DN_CHUNK):
        rs = slice(c * DN_CHUNK, (c + 1) * DN_CHUNK)
        gc = jnp.dot(ltri, jnp.where(is_g, gval[rs], 0.0), precision=HIGHEST, preferred_element_type=f32)
        gcol_ref[0, rs, :] = jnp.where(lane < DN_HEADS, beta[rs], gc)
        gt = -jnp.exp(prow_ref[0]) * _softplus(pbat[:, rs] + prow_ref[1])
        gt = jnp.where(sub >= DN_HEADS, gt, 0.0)
        grow_ref[0, :, rs] = jnp.dot(gt, utri, precision=HIGHEST, preferred_element_type=f32)


def _stage_inproj(x, w_re, wbat, conv_w, pcol, prow, lng, lnb, *, tm):
    B, T, _ = x.shape
    wcols = w_re.shape[1]
    grid = (B, T // tm)
    act = lambda: jax.ShapeDtypeStruct((B, T, DN_WIDTH), bf16)
    act_spec = lambda: pl.BlockSpec((1, tm, DN_WIDTH), lambda b, t: (b, t, 0))
    const2 = lambda shp: pl.BlockSpec(shp, lambda b, t: (0, 0))
    return pl.pallas_call(
        functools.partial(_inproj_body, tm=tm),
        grid=grid,
        in_specs=[
            pl.BlockSpec((1, tm, D_MODEL), lambda b, t: (b, t, 0)),
            const2((D_MODEL, wcols)),
            const2((8, D_MODEL)),
            const2((CONV_K, 3 * DN_WIDTH)),
            const2((8, 128)),
            pl.BlockSpec((2, 8, 128), lambda b, t: (0, 0, 0)),
            const2((1, SGU_WIDTH)),
            const2((1, SGU_WIDTH)),
        ],
        out_specs=[act_spec() for _ in range(6)] + [
            pl.BlockSpec((1, tm, 128), lambda b, t: (b, t, 0)),
            pl.BlockSpec((1, 8, tm), lambda b, t: (b, 0, t)),
        ],
        out_shape=[act() for _ in range(6)] + [
            jax.ShapeDtypeStruct((B, T, 128), f32),
            jax.ShapeDtypeStruct((B, 8, T), f32),
        ],
        scratch_shapes=[pltpu.VMEM((tm + 8, 3 * DN_WIDTH), f32)],
        compiler_params=_cparams(("arbitrary", "arbitrary")),
        name="inproj",
    )(x, w_re, wbat, conv_w, pcol, prow, lng, lnb)


def _unit_lower_inverse(nmat, ii, jj):
    n = nmat.shape[0]
    eye = (ii == jj).astype(f32)
    dblk = jnp.where(jnp.right_shift(ii, 3) == jnp.right_shift(jj, 3), nmat, 0.0)
    d16 = dblk.astype(bf16)
    s1 = jnp.dot(d16, d16, preferred_element_type=f32)
    r1 = eye - dblk
    s1b = s1.astype(bf16)
    both = jnp.dot(s1b, jnp.concatenate([s1b, r1.astype(bf16)], axis=1), preferred_element_type=f32)
    s2 = both[:, :n]
    r2 = r1 + both[:, n:]
    x = r2 + jnp.dot(s2.astype(bf16), r2.astype(bf16), preferred_element_type=f32)
    shift = 3
    while (1 << shift) < n:
        same_parent = jnp.right_shift(ii, shift + 1) == jnp.right_shift(jj, shift + 1)
        mlev = jnp.where(same_parent & (jnp.right_shift(ii, shift) != jnp.right_shift(jj, shift)), nmat, 0.0)
        xb = x.astype(bf16)
        y = jnp.dot(mlev.astype(bf16), xb, preferred_element_type=f32)
        x = x - jnp.dot(xb, y.astype(bf16), preferred_element_type=f32)
        shift += 1
    return x


def _deltanet_body(q_ref, k_ref, v_ref, z_ref, gcol_ref, grow_ref, nw_ref, y_ref, s_ref):
    C = DN_CHUNK

    @pl.when(pl.program_id(1) == 0)
    def _():
        s_ref[...] = jnp.zeros(s_ref.shape, f32)

    ii = _iota2((C, C), 0)
    jj = _iota2((C, C), 1)
    causal = ii >= jj
    gcol = gcol_ref[0]
    for h in range(DN_HEADS):
        hs = slice(h * HEAD_DIM, (h + 1) * HEAD_DIM)
        qh = q_ref[0, :, hs]
        kh = k_ref[0, :, hs]
        vh = v_ref[0, :, hs]
        gc_b = jnp.broadcast_to(gcol[:, DN_HEADS + h:DN_HEADS + h + 1], (C, HEAD_DIM))
        beta_b = jnp.broadcast_to(gcol[:, h:h + 1], (C, HEAD_DIM))
        gc_r = jnp.broadcast_to(grow_ref[0, DN_HEADS + h:DN_HEADS + h + 1, :], (C, C))
        decay = jnp.exp(jnp.where(causal, gc_b - gc_r, -1e30))

        kf = kh.astype(f32)
        kb = kf * beta_b
        kb16 = kb.astype(bf16)
        kk = lax.dot_general(kb16, kh, NT_DIMS, preferred_element_type=f32)
        nmat = jnp.where(ii > jj, kk * decay, 0.0)
        tinv = _unit_lower_inverse(nmat, ii, jj).astype(bf16)

        eg = jnp.exp(gc_b)
        rhs = jnp.concatenate([vh.astype(f32) * beta_b, kb * eg], axis=1).astype(bf16)
        sol = jnp.dot(tinv, rhs, preferred_element_type=f32)
        u_vals = sol[:, :HEAD_DIM]
        w_dec = sol[:, HEAD_DIM:]

        a_intra = lax.dot_general(qh, kh, NT_DIMS, preferred_element_type=f32) * decay
        q_dec = qh.astype(f32) * eg
        g_last = gc_b[C - 1:C, :]
        k_dec = kf * jnp.exp(g_last - gc_b)
        kdt = k_dec.T
        gl = jnp.exp(g_last)

        s_old = s_ref[h]
        m1 = jnp.dot(jnp.concatenate([w_dec, q_dec], axis=0).astype(bf16), s_old.astype(bf16),
                     preferred_element_type=f32)
        v_new = u_vals - m1[:C]
        m2 = jnp.dot(jnp.concatenate([a_intra, kdt], axis=0).astype(bf16), v_new.astype(bf16),
                     preferred_element_type=f32)
        o = m1[C:] + m2[:C]
        s_ref[h] = s_old * gl + m2[C:]

        rms = lax.rsqrt(jnp.mean(o * o, axis=-1, keepdims=True) + RMS_EPS)
        y_ref[0, :, hs] = (o * rms * nw_ref[...] * z_ref[0, :, hs].astype(f32)).astype(bf16)


def _stage_deltanet(q, k, v, z, gcol, grow, norm_w):
    B, T, _ = q.shape
    C = DN_CHUNK
    act_spec = lambda: pl.BlockSpec((1, C, DN_WIDTH), lambda b, t: (b, t, 0))
    return pl.pallas_call(
        _deltanet_body,
        grid=(B, T // C),
        in_specs=[act_spec(), act_spec(), act_spec(), act_spec(),
                  pl.BlockSpec((1, C, 128), lambda b, t: (b, t, 0)),
                  pl.BlockSpec((1, 8, C), lambda b, t: (b, 0, t)),
                  pl.BlockSpec((1, HEAD_DIM), lambda b, t: (0, 0))],
        out_specs=act_spec(),
        out_shape=jax.ShapeDtypeStruct((B, T, DN_WIDTH), bf16),
        scratch_shapes=[pltpu.VMEM((DN_HEADS, HEAD_DIM, HEAD_DIM), f32)],
        compiler_params=_cparams(("arbitrary", "arbitrary")),
        name="deltanet",
    )(q, k, v, z, gcol, grow, norm_w)


def _mixout_body(ydn_ref, u_ref, vln_ref, x_ref, ws_ref, bsp_ref, wout_ref, g1_ref, b1_ref, wrt_ref, brt_ref,
                 h_ref, ids_ref, wts_ref, ycat_ref, *, tm):
    C = SGU_CHUNK
    ii = _iota2((C, C), 0)
    jj = _iota2((C, C), 1)
    ycat_ref[:, 0:DN_WIDTH] = ydn_ref[0]
    for g in range(SGU_GROUPS):
        gs = slice(g * C, (g + 1) * C)
        wsg = jnp.where(ii >= jj, ws_ref[g], 0.0).astype(bf16)
        for c in range(tm // C):
            rs = slice(c * C, (c + 1) * C)
            mixed = jnp.dot(wsg, vln_ref[0, rs, gs], preferred_element_type=f32) + bsp_ref[:, gs]
            ycat_ref[rs, DN_WIDTH + g * C:DN_WIDTH + (g + 1) * C] = (u_ref[0, rs, gs].astype(f32) * mixed).astype(bf16)

    mix = jnp.dot(ycat_ref[...], wout_ref[...], preferred_element_type=f32)
    hp = DEEPNORM_ALPHA * x_ref[0] + mix
    mu = jnp.mean(hp, axis=-1, keepdims=True)
    hc = hp - mu
    var = jnp.mean(hc * hc, axis=-1, keepdims=True)
    h1 = hc * lax.rsqrt(var + LN_EPS) * g1_ref[...] + b1_ref[...]
    h_ref[0] = h1

    logits = lax.dot_general(wrt_ref[...], h1.astype(bf16), NT_DIMS, preferred_element_type=f32)
    logits = logits + jnp.concatenate([brt_ref[...]] * (tm // 128), axis=1)
    sub = _iota2((8, tm), 0)
    subf = sub.astype(f32)
    gl = logits[0:8]
    gmax = jnp.max(gl, axis=0, keepdims=True)
    g_idx = jnp.min(jnp.where(gl == gmax, subf, float(MOE_GROUPS)), axis=0, keepdims=True)
    p_group = 1.0 / jnp.sum(jnp.exp(gl - gmax), axis=0, keepdims=True)
    within = jnp.zeros((8, tm), f32)
    for g in range(MOE_GROUPS):
        within = within + jnp.where(g_idx == float(g), logits[8 + 8 * g:16 + 8 * g], 0.0)
    m1 = jnp.max(within, axis=0, keepdims=True)
    i1 = jnp.min(jnp.where(within == m1, subf, float(EXPERTS_PER_GROUP)), axis=0, keepdims=True)
    rest = jnp.where(subf == i1, -jnp.inf, within)
    m2 = jnp.max(rest, axis=0, keepdims=True)
    i2 = jnp.min(jnp.where(rest == m2, subf, float(EXPERTS_PER_GROUP)), axis=0, keepdims=True)
    e = jnp.exp(m2 - m1)
    w1 = p_group / (1.0 + e)
    w2 = p_group * e / (1.0 + e)
    e1 = g_idx * float(EXPERTS_PER_GROUP) + i1
    e2 = g_idx * float(EXPERTS_PER_GROUP) + i2
    ids_ref[...] = jnp.where(sub == 0, e1, jnp.where(sub == 1, e2, 0.0)).astype(i32)
    wts_ref[...] = jnp.where(sub == 0, w1, jnp.where(sub == 1, w2, 0.0))


def _stage_mixout(ydn, u, vln, x, ws, bsp, wout, g1, b1, wrt, brt, *, tm):
    B, T, _ = x.shape
    nt = T // tm
    act_spec = lambda: pl.BlockSpec((1, tm, DN_WIDTH), lambda b, t: (b, t, 0))
    const2 = lambda shp: pl.BlockSpec(shp, lambda b, t: (0, 0))
    tok_spec = lambda: pl.BlockSpec((8, tm), lambda b, t: (0, b * nt + t))
    return pl.pallas_call(
        functools.partial(_mixout_body, tm=tm),
        grid=(B, nt),
        in_specs=[act_spec(), act_spec(), act_spec(),
                  pl.BlockSpec((1, tm, D_MODEL), lambda b, t: (b, t, 0)),
                  pl.BlockSpec((SGU_GROUPS, SGU_CHUNK, SGU_CHUNK), lambda b, t: (0, 0, 0)),
                  const2((SGU_CHUNK, SGU_WIDTH)),
                  const2((D_MODEL, D_MODEL)),
                  const2((1, D_MODEL)), const2((1, D_MODEL)),
                  const2((128, D_MODEL)), const2((128, 128))],
        out_specs=[pl.BlockSpec((1, tm, D_MODEL), lambda b, t: (b, t, 0)), tok_spec(), tok_spec()],
        out_shape=[jax.ShapeDtypeStruct((B, T, D_MODEL), f32),
                   jax.ShapeDtypeStruct((8, B * T), i32),
                   jax.ShapeDtypeStruct((8, B * T), f32)],
        scratch_shapes=[pltpu.VMEM((tm, D_MODEL), bf16)],
        compiler_params=_cparams(("arbitrary", "arbitrary")),
        name="mixout",
    )(ydn, u, vln, x, ws, bsp, wout, g1, b1, wrt, brt)


def _rank_body(ids_ref, rank_ref, cnt_ref, base_ref, *, tm):
    @pl.when(pl.program_id(0) == 0)
    def _():
        base_ref[...] = jnp.zeros(base_ref.shape, f32)

    sub = _iota2((N_EXPERTS, tm), 0)
    oh1 = (sub == ids_ref[0:1, :]).astype(f32)
    oh2 = (sub == ids_ref[1:2, :]).astype(f32)
    oh = (oh1 + oh2).astype(bf16)
    ti = _iota2((tm, tm), 0)
    tj = _iota2((tm, tm), 1)
    before = (ti < tj).astype(bf16)
    prefix = jnp.dot(oh, before, preferred_element_type=f32)
    tot = prefix + jnp.concatenate([base_ref[...]] * (tm // 128), axis=1)
    r1 = jnp.sum(oh1 * tot, axis=0, keepdims=True)
    r2 = jnp.sum(oh2 * tot, axis=0, keepdims=True)
    sub8 = _iota2((8, tm), 0)
    rank_ref[...] = jnp.where(sub8 == 0, r1, jnp.where(sub8 == 1, r2, 0.0)).astype(i32)
    base_ref[...] = base_ref[...] + jnp.dot(oh, jnp.ones((tm, 128), bf16), preferred_element_type=f32)
    cnt_ref[...] = base_ref[...]


def _stage_rank(ids, *, tm):
    n = ids.shape[1]
    return pl.pallas_call(
        functools.partial(_rank_body, tm=tm),
        grid=(n // tm,),
        in_specs=[pl.BlockSpec((8, tm), lambda i: (0, i))],
        out_specs=[pl.BlockSpec((8, tm), lambda i: (0, i)),
                   pl.BlockSpec((N_EXPERTS, 128), lambda i: (0, 0))],
        out_shape=[jax.ShapeDtypeStruct((8, n), i32), jax.ShapeDtypeStruct((N_EXPERTS, 128), f32)],
        scratch_shapes=[pltpu.VMEM((N_EXPERTS, 128), f32)],
        compiler_params=_cparams(("arbitrary",)),
        name="moe_rank",
    )(ids)


def _dest_body(cnt_ref, ids_ref, rank_ref, dest_ref, meta_ref, blk_ref, *, tm, nb_pad):
    cnt = cnt_ref[...]
    padded = jnp.floor((cnt + (MOE_BLOCK - 1)) * (1.0 / MOE_BLOCK)) * MOE_BLOCK
    ei = _iota2((N_EXPERTS, N_EXPERTS), 0)
    ej = _iota2((N_EXPERTS, N_EXPERTS), 1)
    pends = jnp.dot((ei >= ej).astype(f32), padded, precision=HIGHEST, preferred_element_type=f32)
    pstart = pends - padded
    sub = _iota2((N_EXPERTS, tm), 0)
    pst = jnp.concatenate([pstart] * (tm // 128), axis=1)
    d1 = jnp.sum(jnp.where(sub == ids_ref[0:1, :], pst, 0.0), axis=0, keepdims=True)
    d2 = jnp.sum(jnp.where(sub == ids_ref[1:2, :], pst, 0.0), axis=0, keepdims=True)
    sub8 = _iota2((8, tm), 0)
    dest_ref[...] = jnp.where(sub8 == 0, d1, jnp.where(sub8 == 1, d2, 0.0)).astype(i32) + rank_ref[...]

    @pl.when(pl.program_id(0) == 0)
    def _():
        s64 = _iota2((N_EXPERTS, 128), 0)
        l64 = _iota2((N_EXPERTS, 128), 1)
        diag = s64 == l64
        fill_off = jnp.sum(jnp.where(diag, pstart + cnt, 0.0), axis=0, keepdims=True)
        fill_n = jnp.sum(jnp.where(diag, padded - cnt, 0.0), axis=0, keepdims=True)
        nused = pends[N_EXPERTS - 1:N_EXPERTS, :] * (1.0 / MOE_BLOCK)
        m8 = _iota2((8, 128), 0)
        meta_ref[...] = jnp.where(m8 == 0, fill_off, jnp.where(m8 == 1, fill_n, jnp.where(m8 == 2, nused, 0.0))).astype(i32)
        bstart = (_iota2((N_EXPERTS, nb_pad), 1) * MOE_BLOCK).astype(f32)
        pe = jnp.concatenate([pends] * (nb_pad // 128), axis=1)
        be = jnp.sum((pe <= bstart).astype(f32), axis=0, keepdims=True)
        be = jnp.minimum(be, float(N_EXPERTS - 1))
        blk_ref[...] = jnp.broadcast_to(be, (8, nb_pad)).astype(i32)


def _stage_dest(cnt, ids, rank, *, tm, nb_pad):
    n = ids.shape[1]
    tok = lambda: pl.BlockSpec((8, tm), lambda i: (0, i))
    return pl.pallas_call(
        functools.partial(_dest_body, tm=tm, nb_pad=nb_pad),
        grid=(n // tm,),
        in_specs=[pl.BlockSpec((N_EXPERTS, 128), lambda i: (0, 0)), tok(), tok()],
        out_specs=[tok(), pl.BlockSpec((8, 128), lambda i: (0, 0)), pl.BlockSpec((8, nb_pad), lambda i: (0, 0))],
        out_shape=[jax.ShapeDtypeStruct((8, n), i32), jax.ShapeDtypeStruct((8, 128), i32),
                   jax.ShapeDtypeStruct((8, nb_pad), i32)],
        compiler_params=_cparams(("arbitrary",)),
        name="moe_dest",
    )(cnt, ids, rank)


def _dispatch_body(fill_off_ref, fill_n_ref, nused_ref, dest_ref, h_ref, xs_ref, zero_ref, sem, zsem, *, tm):
    def row_copy(t, d):
        return pltpu.make_async_copy(h_ref.at[pl.ds(t, 1)], xs_ref.at[pl.ds(d, 1)], sem)

    def issue(t, carry):
        row_copy(t, dest_ref[0, 0, t]).start()
        row_copy(t, dest_ref[0, 1, t]).start()
        return carry

    lax.fori_loop(0, tm, issue, 0)

    @pl.when(pl.program_id(0) == 0)
    def _():
        zero_ref[...] = jnp.zeros(zero_ref.shape, f32)

        def zero_copy(d):
            return pltpu.make_async_copy(zero_ref.at[pl.ds(0, 1)], xs_ref.at[pl.ds(d, 1)], zsem)

        def fill(e, carry):
            off = fill_off_ref[e]
            lax.fori_loop(0, fill_n_ref[e], lambda r, c: (zero_copy(off + r).start(), c)[1], 0)
            return carry

        def fill_wait(e, carry):
            lax.fori_loop(0, fill_n_ref[e], lambda r, c: (zero_copy(0).wait(), c)[1], 0)
            return carry

        lax.fori_loop(0, N_EXPERTS, fill, 0)
        lax.fori_loop(0, N_EXPERTS, fill_wait, 0)

        def tail_copy(b):
            row = pl.multiple_of(b * MOE_BLOCK, MOE_BLOCK)
            return pltpu.make_async_copy(zero_ref, xs_ref.at[pl.ds(row, MOE_BLOCK)], zsem)

        nblocks = xs_ref.shape[0] // MOE_BLOCK
        lax.fori_loop(nused_ref[0], nblocks, lambda b, c: (tail_copy(b).start(), c)[1], 0)
        lax.fori_loop(nused_ref[0], nblocks, lambda b, c: (tail_copy(0).wait(), c)[1], 0)

    def drain(t, carry):
        row_copy(0, 0).wait()
        row_copy(0, 0).wait()
        return carry

    lax.fori_loop(0, tm, drain, 0)


def _stage_dispatch(fill_off, fill_n, nused, dest3, h2, p_rows, *, tm):
    n = h2.shape[0]
    return pl.pallas_call(
        functools.partial(_dispatch_body, tm=tm),
        grid_spec=pltpu.PrefetchScalarGridSpec(
            num_scalar_prefetch=3,
            grid=(n // tm,),
            in_specs=[pl.BlockSpec((1, 2, tm), lambda i, fo, fn, nu: (i, 0, 0), memory_space=pltpu.SMEM),
                      pl.BlockSpec((tm, D_MODEL), lambda i, fo, fn, nu: (i, 0))],
            out_specs=pl.BlockSpec(memory_space=pl.ANY),
            scratch_shapes=[pltpu.VMEM((MOE_BLOCK, D_MODEL), f32), pltpu.SemaphoreType.DMA, pltpu.SemaphoreType.DMA],
        ),
        out_shape=jax.ShapeDtypeStruct((p_rows, D_MODEL), f32),
        compiler_params=_cparams(("arbitrary",)),
        name="moe_dispatch",
    )(fill_off, fill_n, nused, dest3, h2)


def _experts_body(blk_ref, nused_ref, xs_ref, wg_ref, wu_ref, wd_ref, ys_ref):
    used = pl.program_id(0) < nused_ref[0]

    @pl.when(used)
    def _():
        xb = xs_ref[...].astype(bf16)
        gate = jnp.dot(xb, wg_ref[0].astype(bf16), preferred_element_type=f32)
        up = jnp.dot(xb, wu_ref[0].astype(bf16), preferred_element_type=f32)
        hid = (gate * _sigmoid(gate) * up).astype(bf16)
        ys_ref[...] = jnp.dot(hid, wd_ref[0].astype(bf16), preferred_element_type=f32)

    @pl.when(jnp.logical_not(used))
    def _():
        ys_ref[...] = jnp.zeros(ys_ref.shape, f32)


def _stage_experts(blk_e, nused, xs, w_gate, w_up, w_down):
    p_rows = xs.shape[0]
    nb = p_rows // MOE_BLOCK

    def x_map(i, blk, nu):
        return (jnp.minimum(i, nu[0] - 1), 0)

    def row_map(i, blk, nu):
        return (i, 0)

    def w_map(i, blk, nu):
        return (blk[jnp.minimum(i, nu[0] - 1)], 0, 0)

    return pl.pallas_call(
        _experts_body,
        grid_spec=pltpu.PrefetchScalarGridSpec(
            num_scalar_prefetch=2,
            grid=(nb,),
            in_specs=[pl.BlockSpec((MOE_BLOCK, D_MODEL), x_map),
                      pl.BlockSpec((1, D_MODEL, D_EXPERT), w_map),
                      pl.BlockSpec((1, D_MODEL, D_EXPERT), w_map),
                      pl.BlockSpec((1, D_EXPERT, D_MODEL), w_map)],
            out_specs=pl.BlockSpec((MOE_BLOCK, D_MODEL), row_map),
        ),
        out_shape=jax.ShapeDtypeStruct((p_rows, D_MODEL), f32),
        compiler_params=_cparams(("arbitrary",)),
        name="moe_experts",
    )(blk_e, nused, xs, w_gate, w_up, w_down)


def _combine_body(dest_ref, h_ref, wts_ref, g2_ref, b2_ref, ys_ref, o_ref, y1_ref, y2_ref, sem, *, tm):
    def row_copy(buf_ref, t, d):
        return pltpu.make_async_copy(ys_ref.at[pl.ds(d, 1)], buf_ref.at[pl.ds(t, 1)], sem)

    def issue(t, carry):
        row_copy(y1_ref, t, dest_ref[0, 0, t]).start()
        row_copy(y2_ref, t, dest_ref[0, 1, t]).start()
        return carry

    lax.fori_loop(0, tm, issue, 0)

    def drain(t, carry):
        row_copy(y1_ref, 0, 0).wait()
        row_copy(y2_ref, 0, 0).wait()
        return carry

    lax.fori_loop(0, tm, drain, 0)

    pieces = []
    for c in range(tm // 128):
        ls = slice(c * 128, (c + 1) * 128)
        w1c = jnp.broadcast_to(wts_ref[0:1, ls], (128, 128)).T
        w2c = jnp.broadcast_to(wts_ref[1:2, ls], (128, 128)).T
        w1f = jnp.concatenate([w1c] * (D_MODEL // 128), axis=1)
        w2f = jnp.concatenate([w2c] * (D_MODEL // 128), axis=1)
        pieces.append(w1f * y1_ref[ls, :] + w2f * y2_ref[ls, :])
    ffn = jnp.concatenate(pieces, axis=0)
    hp = DEEPNORM_ALPHA * h_ref[...] + ffn
    mu = jnp.mean(hp, axis=-1, keepdims=True)
    hc = hp - mu
    var = jnp.mean(hc * hc, axis=-1, keepdims=True)
    o_ref[...] = hc * lax.rsqrt(var + LN_EPS) * g2_ref[...] + b2_ref[...]


def _stage_combine(dest3, h2, wts, g2, b2, ys, *, tm):
    n = h2.shape[0]
    return pl.pallas_call(
        functools.partial(_combine_body, tm=tm),
        grid=(n // tm,),
        in_specs=[pl.BlockSpec((1, 2, tm), lambda i: (i, 0, 0), memory_space=pltpu.SMEM),
                  pl.BlockSpec((tm, D_MODEL), lambda i: (i, 0)),
                  pl.BlockSpec((8, tm), lambda i: (0, i)),
                  pl.BlockSpec((1, D_MODEL), lambda i: (0, 0)),
                  pl.BlockSpec((1, D_MODEL), lambda i: (0, 0)),
                  pl.BlockSpec(memory_space=pl.ANY)],
        out_specs=pl.BlockSpec((tm, D_MODEL), lambda i: (i, 0)),
        out_shape=jax.ShapeDtypeStruct((n, D_MODEL), f32),
        scratch_shapes=[pltpu.VMEM((tm, D_MODEL), f32), pltpu.VMEM((tm, D_MODEL), f32), pltpu.SemaphoreType.DMA],
        compiler_params=_cparams(("arbitrary",)),
        name="moe_combine",
    )(dest3, h2, wts, g2, b2, ys)


def _layer(h, w_in, conv_w, a_log, dt_bias, dn_norm_w, sgu_ln_g, sgu_ln_b, w_spatial, b_spatial, w_out,
           ln1_g, ln1_b, w_rg, b_rg, w_re, b_re, w_gate, w_up, w_down, ln2_g, ln2_b,
           *, tm_in, tm_mix, tm_rank, tm_disp, tm_comb):
    B, T, _ = h.shape
    n = B * T
    qkvz = 4 * DN_WIDTH
    ba0 = qkvz
    uv0 = qkvz + 2 * DN_HEADS
    w_cols = jnp.concatenate(
        [w_in[:, :qkvz], w_in[:, uv0:], w_in[:, ba0:uv0], jnp.zeros((D_MODEL, 128 - 2 * DN_HEADS), f32)], axis=1).astype(bf16)
    wbat = w_in[:, ba0:uv0].T.astype(bf16)
    pcol = jnp.zeros((8, 128), f32).at[0, DN_HEADS:2 * DN_HEADS].set(a_log).at[1, DN_HEADS:2 * DN_HEADS].set(dt_bias)
    prow = jnp.zeros((2, 8, 128), f32)
    prow = prow.at[0, DN_HEADS:2 * DN_HEADS, :].set(jnp.broadcast_to(a_log[:, None], (DN_HEADS, 128)))
    prow = prow.at[1, DN_HEADS:2 * DN_HEADS, :].set(jnp.broadcast_to(dt_bias[:, None], (DN_HEADS, 128)))

    q, k, v, z, u, vln, gcol, grow = _stage_inproj(
        h, w_cols, wbat, conv_w, pcol, prow, sgu_ln_g[None, :], sgu_ln_b[None, :], tm=tm_in)
    ydn = _stage_deltanet(q, k, v, z, gcol, grow, dn_norm_w[None, :])

    bsp = jnp.broadcast_to(b_spatial.T[:, :, None], (SGU_CHUNK, SGU_GROUPS, SGU_CHUNK)).reshape(SGU_CHUNK, SGU_WIDTH)
    wrt = jnp.zeros((128, D_MODEL), f32).at[0:MOE_GROUPS].set(w_rg.T).at[MOE_GROUPS:MOE_GROUPS + N_EXPERTS].set(w_re.T).astype(bf16)
    brt = jnp.zeros((128,), f32).at[0:MOE_GROUPS].set(b_rg).at[MOE_GROUPS:MOE_GROUPS + N_EXPERTS].set(b_re)
    brt = jnp.broadcast_to(brt[:, None], (128, 128))
    h1, ids, wts = _stage_mixout(ydn, u, vln, h, w_spatial, bsp, w_out.astype(bf16), ln1_g[None, :], ln1_b[None, :],
                                 wrt, brt, tm=tm_mix)

    p_rows = (-(-(n * 2) // MOE_BLOCK)) * MOE_BLOCK + N_EXPERTS * MOE_BLOCK
    nb = p_rows // MOE_BLOCK
    nb_pad = (-(-nb // 128)) * 128
    rank, cnt = _stage_rank(ids, tm=tm_rank)
    dest, meta, blk = _stage_dest(cnt, ids, rank, tm=tm_rank, nb_pad=nb_pad)

    h2 = h1.reshape(n, D_MODEL)
    dest_d = dest[0:2].reshape(2, n // tm_disp, tm_disp).transpose(1, 0, 2)
    xs = _stage_dispatch(meta[0, :N_EXPERTS], meta[1, :N_EXPERTS], meta[2, 0:1], dest_d, h2, p_rows, tm=tm_disp)
    ys = _stage_experts(blk[0, :nb], meta[2, 0:1], xs, w_gate, w_up, w_down)
    dest_c = dest[0:2].reshape(2, n // tm_comb, tm_comb).transpose(1, 0, 2)
    out = _stage_combine(dest_c, h2, wts, ln2_g[None, :], ln2_b[None, :], ys, tm=tm_comb)
    return out.reshape(B, T, D_MODEL)


def kernel(x, w_in, conv_w, a_log, dt_bias, dn_norm_w, sgu_ln_g, sgu_ln_b, w_spatial, b_spatial, w_out, ln1_g, ln1_b, w_router_group, b_router_group, w_router_expert, b_router_expert, w_gate, w_up, w_down, ln2_g, ln2_b):
    h = x
    for l in range(w_in.shape[0]):
        h = _layer(h, w_in[l], conv_w[l], a_log[l], dt_bias[l], dn_norm_w[l], sgu_ln_g[l], sgu_ln_b[l],
                   w_spatial[l], b_spatial[l], w_out[l], ln1_g[l], ln1_b[l],
                   w_router_group[l], b_router_group[l], w_router_expert[l], b_router_expert[l],
                   w_gate[l], w_up[l], w_down[l], ln2_g[l], ln2_b[l],
                   tm_in=512, tm_mix=512, tm_rank=512, tm_disp=256, tm_comb=256)
    return h
```

```python
import functools

import jax
import jax.numpy as jnp
from jax import lax
from jax.experimental import pallas as pl
from jax.experimental.pallas import tpu as pltpu

f32 = jnp.float32
bf16 = jnp.bfloat16
i32 = jnp.int32

D_MODEL = 1024
DN_WIDTH = 512
DN_HEADS = 4
HEAD_DIM = 128
CONV_K = 4
SGU_WIDTH = 512
SGU_GROUPS = 4
SGU_CHUNK = 128
DN_CHUNK = 128
MOE_GROUPS = 8
EXPERTS_PER_GROUP = 8
N_EXPERTS = 64
D_EXPERT = 512
MOE_BLOCK = 128
DEEPNORM_ALPHA = 2.0 ** 0.25
LN_EPS = 1e-5
RMS_EPS = 1e-6
HIGHEST = lax.Precision.HIGHEST
VMEM_LIMIT_BYTES = 56 * 1024 * 1024

NT_DIMS = (((1,), (1,)), ((), ()))


def _cparams(sem, flags=None):
    return pltpu.CompilerParams(dimension_semantics=sem, vmem_limit_bytes=VMEM_LIMIT_BYTES, flags=flags)


def _sigmoid(x):
    return 1.0 / (1.0 + jnp.exp(-x))


def _softplus(x):
    return jnp.maximum(x, 0.0) + jnp.log1p(jnp.exp(-jnp.abs(x)))


def _gelu_tanh(x):
    c = 0.7978845608028654
    return x * (0.5 * (1.0 + jnp.tanh(c * (x + 0.044715 * (x * x * x)))))


def _iota2(shape, axis):
    return lax.broadcasted_iota(i32, shape, axis)


def _inproj_body(x_ref, w_ref, wbat_ref, convw_ref, pcol_ref, prow_ref, lng_ref, lnb_ref,
                 q_ref, k_ref, v_ref, z_ref, u_ref, vln_ref, gcol_ref, grow_ref, ext_ref, *, tm):
    @pl.when(pl.program_id(1) == 0)
    def _():
        ext_ref[0:8, :] = jnp.zeros((8, 3 * DN_WIDTH), f32)

    xb = x_ref[0].astype(bf16)

    for part, out_ref in enumerate((q_ref, k_ref, v_ref)):
        c0 = part * DN_WIDTH
        pre = jnp.dot(xb, w_ref[:, c0:c0 + DN_WIDTH], preferred_element_type=f32)
        ext_ref[8:8 + tm, c0:c0 + DN_WIDTH] = pre
        y = convw_ref[3:4, c0:c0 + DN_WIDTH] * pre
        for j in range(CONV_K - 1):
            y = y + convw_ref[j:j + 1, c0:c0 + DN_WIDTH] * ext_ref[5 + j:5 + j + tm, c0:c0 + DN_WIDTH]
        y = y * _sigmoid(y)
        if part < 2:
            scale = HEAD_DIM ** -0.5 if part == 0 else 1.0
            for h in range(DN_HEADS):
                yh = y[:, h * HEAD_DIM:(h + 1) * HEAD_DIM]
                ss = jnp.sum(yh * yh, axis=-1, keepdims=True)
                out_ref[0, :, h * HEAD_DIM:(h + 1) * HEAD_DIM] = (yh * lax.rsqrt(ss + RMS_EPS) * scale).astype(bf16)
        else:
            out_ref[0] = y.astype(bf16)
    ext_ref[0:8, :] = ext_ref[tm:tm + 8, :]

    zc = 3 * DN_WIDTH
    pz = jnp.dot(xb, w_ref[:, zc:zc + DN_WIDTH], preferred_element_type=f32)
    z_ref[0] = (pz * _sigmoid(pz)).astype(bf16)

    uc = zc + DN_WIDTH
    pu = jnp.dot(xb, w_ref[:, uc:uc + SGU_WIDTH], preferred_element_type=f32)
    u_ref[0] = _gelu_tanh(pu).astype(bf16)
    vc = uc + SGU_WIDTH
    pv = _gelu_tanh(jnp.dot(xb, w_ref[:, vc:vc + SGU_WIDTH], preferred_element_type=f32))
    for g in range(SGU_GROUPS):
        sl = slice(g * SGU_CHUNK, (g + 1) * SGU_CHUNK)
        vg = pv[:, sl]
        mu = jnp.mean(vg, axis=-1, keepdims=True)
        vcn = vg - mu
        var = jnp.mean(vcn * vcn, axis=-1, keepdims=True)
        vln_ref[0, :, sl] = (vcn * lax.rsqrt(var + LN_EPS) * lng_ref[:, sl] + lnb_ref[:, sl]).astype(bf16)

    bc = vc + SGU_WIDTH
    pba = jnp.dot(xb, w_ref[:, bc:bc + 128], preferred_element_type=f32)
    lane = _iota2((DN_CHUNK, 128), 1)
    is_g = (lane >= DN_HEADS) & (lane < 2 * DN_HEADS)
    beta = _sigmoid(pba)
    gval = -jnp.exp(pcol_ref[0:1, :]) * _softplus(pba + pcol_ref[1:2, :])
    ci = _iota2((DN_CHUNK, DN_CHUNK), 0)
    cj = _iota2((DN_CHUNK, DN_CHUNK), 1)
    ltri = (ci >= cj).astype(f32)
    utri = (ci <= cj).astype(f32)
    pbat = lax.dot_general(wbat_ref[...], xb, NT_DIMS, preferred_element_type=f32)
    sub = _iota2((8, DN_CHUNK), 0)
    for c in range(tm // DN_CHUNK):
        rs = slice(c * DN_CHUNK, (c + 1) * DN_CHUNK)
        gc = jnp.dot(ltri, jnp.where(is_g, gval[rs], 0.0), precision=HIGHEST, preferred_element_type=f32)
        gcol_ref[0, rs, :] = jnp.where(lane < DN_HEADS, beta[rs], gc)
        gt = -jnp.exp(prow_ref[0]) * _softplus(pbat[:, rs] + prow_ref[1])
        gt = jnp.where(sub >= DN_HEADS, gt, 0.0)
        grow_ref[0, :, rs] = jnp.dot(gt, utri, precision=HIGHEST, preferred_element_type=f32)


def _stage_inproj(x, w_re, wbat, conv_w, pcol, prow, lng, lnb, *, tm):
    B, T, _ = x.shape
    wcols = w_re.shape[1]
    grid = (B, T // tm)
    act = lambda: jax.ShapeDtypeStruct((B, T, DN_WIDTH), bf16)
    act_spec = lambda: pl.BlockSpec((1, tm, DN_WIDTH), lambda b, t: (b, t, 0))
    const2 = lambda shp: pl.BlockSpec(shp, lambda b, t: (0, 0))
    return pl.pallas_call(
        functools.partial(_inproj_body, tm=tm),
        grid=grid,
        in_specs=[
            pl.BlockSpec((1, tm, D_MODEL), lambda b, t: (b, t, 0)),
            const2((D_MODEL, wcols)),
            const2((8, D_MODEL)),
            const2((CONV_K, 3 * DN_WIDTH)),
            const2((8, 128)),
            pl.BlockSpec((2, 8, 128), lambda b, t: (0, 0, 0)),
            const2((1, SGU_WIDTH)),
            const2((1, SGU_WIDTH)),
        ],
        out_specs=[act_spec() for _ in range(6)] + [
            pl.BlockSpec((1, tm, 128), lambda b, t: (b, t, 0)),
            pl.BlockSpec((1, 8, tm), lambda b, t: (b, 0, t)),
        ],
        out_shape=[act() for _ in range(6)] + [
            jax.ShapeDtypeStruct((B, T, 128), f32),
            jax.ShapeDtypeStruct((B, 8, T), f32),
        ],
        scratch_shapes=[pltpu.VMEM((tm + 8, 3 * DN_WIDTH), f32)],
        compiler_params=_cparams(("arbitrary", "arbitrary")),
        name="inproj",
    )(x, w_re, wbat, conv_w, pcol, prow, lng, lnb)


def _mm(a, b):
    return jnp.dot(a.astype(bf16), b.astype(bf16), preferred_element_type=f32)


def _unit_lower_inverse(nmats, ii, jj):
    n = nmats[0].shape[0]
    eye = (ii == jj).astype(f32)
    leaf = jnp.right_shift(ii, 3) == jnp.right_shift(jj, 3)
    dblk = [jnp.where(leaf, m, 0.0) for m in nmats]
    s1 = [_mm(d, d) for d in dblk]
    r1 = [eye - d for d in dblk]
    both = [_mm(s, jnp.concatenate([s, r], axis=1)) for s, r in zip(s1, r1)]
    r2 = [r + bo[:, n:] for r, bo in zip(r1, both)]
    xs = [r + _mm(bo[:, :n], r) for r, bo in zip(r2, both)]
    shift = 3
    while (1 << shift) < n:
        same_parent = jnp.right_shift(ii, shift + 1) == jnp.right_shift(jj, shift + 1)
        level = same_parent & (jnp.right_shift(ii, shift) != jnp.right_shift(jj, shift))
        ys = [_mm(jnp.where(level, m, 0.0), x) for m, x in zip(nmats, xs)]
        xs = [x - _mm(x, y) for x, y in zip(xs, ys)]
        shift += 1
    return xs


def _deltanet_body(q_ref, k_ref, v_ref, z_ref, gcol_ref, grow_ref, nw_ref, y_ref, s_ref, *, nch):
    C = DN_CHUNK
    H = range(DN_HEADS)
    P = [(c, h) for c in range(nch) for h in H]

    @pl.when(pl.program_id(1) == 0)
    def _():
        s_ref[...] = jnp.zeros(s_ref.shape, f32)

    ii = _iota2((C, C), 0)
    jj = _iota2((C, C), 1)
    causal = ii >= jj
    rs = [slice(c * C, (c + 1) * C) for c in range(nch)]
    hs = [slice(h * HEAD_DIM, (h + 1) * HEAD_DIM) for h in H]
    gcol = [gcol_ref[0, rs[c], :] for c in range(nch)]
    qh = {p: q_ref[0, rs[p[0]], hs[p[1]]] for p in P}
    kh = {p: k_ref[0, rs[p[0]], hs[p[1]]] for p in P}
    vh = {p: v_ref[0, rs[p[0]], hs[p[1]]] for p in P}
    gc_b = {(c, h): jnp.broadcast_to(gcol[c][:, DN_HEADS + h:DN_HEADS + h + 1], (C, HEAD_DIM)) for c, h in P}
    beta_b = {(c, h): jnp.broadcast_to(gcol[c][:, h:h + 1], (C, HEAD_DIM)) for c, h in P}
    gc_r = {(c, h): jnp.broadcast_to(grow_ref[0, DN_HEADS + h:DN_HEADS + h + 1, rs[c]], (C, C)) for c, h in P}
    decay = {p: jnp.exp(jnp.where(causal, gc_b[p] - gc_r[p], -1e30)) for p in P}

    kf = {p: kh[p].astype(f32) for p in P}
    kb = {p: kf[p] * beta_b[p] for p in P}
    kk = {p: lax.dot_general(kb[p].astype(bf16), kh[p], NT_DIMS, preferred_element_type=f32) for p in P}
    a_intra = {p: lax.dot_general(qh[p], kh[p], NT_DIMS, preferred_element_type=f32) * decay[p] for p in P}
    nmat = [jnp.where(ii > jj, kk[p] * decay[p], 0.0) for p in P]
    tinv = dict(zip(P, _unit_lower_inverse(nmat, ii, jj)))

    eg = {p: jnp.exp(gc_b[p]) for p in P}
    rhs = {p: jnp.concatenate([vh[p].astype(f32) * beta_b[p], kb[p] * eg[p]], axis=1) for p in P}
    sol = {p: _mm(tinv[p], rhs[p]) for p in P}

    q_dec = {p: qh[p].astype(f32) * eg[p] for p in P}
    g_last = {p: gc_b[p][C - 1:C, :] for p in P}
    kdt = {p: (kf[p] * jnp.exp(g_last[p] - gc_b[p])).T for p in P}

    state = [s_ref[h] for h in H]
    for c in range(nch):
        m1 = [_mm(jnp.concatenate([sol[c, h][:, HEAD_DIM:], q_dec[c, h]], axis=0), state[h]) for h in H]
        v_new = [sol[c, h][:, :HEAD_DIM] - m1[h][:C] for h in H]
        m2 = [_mm(jnp.concatenate([a_intra[c, h], kdt[c, h]], axis=0), v_new[h]) for h in H]
        state = [state[h] * jnp.exp(g_last[c, h]) + m2[h][C:] for h in H]
        for h in H:
            o = m1[h][C:] + m2[h][:C]
            rms = lax.rsqrt(jnp.mean(o * o, axis=-1, keepdims=True) + RMS_EPS)
            y_ref[0, rs[c], hs[h]] = (o * rms * nw_ref[...] * z_ref[0, rs[c], hs[h]].astype(f32)).astype(bf16)
    for h in H:
        s_ref[h] = state[h]


def _stage_deltanet(q, k, v, z, gcol, grow, norm_w, *, nch):
    B, T, _ = q.shape
    tt = nch * DN_CHUNK
    act_spec = lambda: pl.BlockSpec((1, tt, DN_WIDTH), lambda b, t: (b, t, 0))
    return pl.pallas_call(
        functools.partial(_deltanet_body, nch=nch),
        grid=(B, T // tt),
        in_specs=[act_spec(), act_spec(), act_spec(), act_spec(),
                  pl.BlockSpec((1, tt, 128), lambda b, t: (b, t, 0)),
                  pl.BlockSpec((1, 8, tt), lambda b, t: (b, 0, t)),
                  pl.BlockSpec((1, HEAD_DIM), lambda b, t: (0, 0))],
        out_specs=act_spec(),
        out_shape=jax.ShapeDtypeStruct((B, T, DN_WIDTH), bf16),
        scratch_shapes=[pltpu.VMEM((DN_HEADS, HEAD_DIM, HEAD_DIM), f32)],
        compiler_params=_cparams(("arbitrary", "arbitrary")),
        name="deltanet",
    )(q, k, v, z, gcol, grow, norm_w)


def _mixout_body(ydn_ref, u_ref, vln_ref, x_ref, ws_ref, bsp_ref, wout_ref, g1_ref, b1_ref, wrt_ref, brt_ref,
                 h_ref, ids_ref, wts_ref, ycat_ref, *, tm):
    C = SGU_CHUNK
    ii = _iota2((C, C), 0)
    jj = _iota2((C, C), 1)
    ycat_ref[:, 0:DN_WIDTH] = ydn_ref[0]
    for g in range(SGU_GROUPS):
        gs = slice(g * C, (g + 1) * C)
        wsg = jnp.where(ii >= jj, ws_ref[g], 0.0).astype(bf16)
        for c in range(tm // C):
            rs = slice(c * C, (c + 1) * C)
            mixed = jnp.dot(wsg, vln_ref[0, rs, gs], preferred_element_type=f32) + bsp_ref[:, gs]
            ycat_ref[rs, DN_WIDTH + g * C:DN_WIDTH + (g + 1) * C] = (u_ref[0, rs, gs].astype(f32) * mixed).astype(bf16)

    mix = jnp.dot(ycat_ref[...], wout_ref[...], preferred_element_type=f32)
    hp = DEEPNORM_ALPHA * x_ref[0] + mix
    mu = jnp.mean(hp, axis=-1, keepdims=True)
    hc = hp - mu
    var = jnp.mean(hc * hc, axis=-1, keepdims=True)
    h1 = hc * lax.rsqrt(var + LN_EPS) * g1_ref[...] + b1_ref[...]
    h_ref[0] = h1

    logits = lax.dot_general(wrt_ref[...], h1.astype(bf16), NT_DIMS, preferred_element_type=f32)
    logits = logits + jnp.concatenate([brt_ref[...]] * (tm // 128), axis=1)
    sub = _iota2((8, tm), 0)
    subf = sub.astype(f32)
    gl = logits[0:8]
    gmax = jnp.max(gl, axis=0, keepdims=True)
    g_idx = jnp.min(jnp.where(gl == gmax, subf, float(MOE_GROUPS)), axis=0, keepdims=True)
    p_group = 1.0 / jnp.sum(jnp.exp(gl - gmax), axis=0, keepdims=True)
    within = jnp.zeros((8, tm), f32)
    for g in range(MOE_GROUPS):
        within = within + jnp.where(g_idx == float(g), logits[8 + 8 * g:16 + 8 * g], 0.0)
    m1 = jnp.max(within, axis=0, keepdims=True)
    i1 = jnp.min(jnp.where(within == m1, subf, float(EXPERTS_PER_GROUP)), axis=0, keepdims=True)
    rest = jnp.where(subf == i1, -jnp.inf, within)
    m2 = jnp.max(rest, axis=0, keepdims=True)
    i2 = jnp.min(jnp.where(rest == m2, subf, float(EXPERTS_PER_GROUP)), axis=0, keepdims=True)
    e = jnp.exp(m2 - m1)
    w1 = p_group / (1.0 + e)
    w2 = p_group * e / (1.0 + e)
    e1 = g_idx * float(EXPERTS_PER_GROUP) + i1
    e2 = g_idx * float(EXPERTS_PER_GROUP) + i2
    ids_ref[...] = jnp.where(sub == 0, e1, jnp.where(sub == 1, e2, 0.0)).astype(i32)
    wts_ref[...] = jnp.where(sub == 0, w1, jnp.where(sub == 1, w2, 0.0))


def _stage_mixout(ydn, u, vln, x, ws, bsp, wout, g1, b1, wrt, brt, *, tm):
    B, T, _ = x.shape
    nt = T // tm
    act_spec = lambda: pl.BlockSpec((1, tm, DN_WIDTH), lambda b, t: (b, t, 0))
    const2 = lambda shp: pl.BlockSpec(shp, lambda b, t: (0, 0))
    tok_spec = lambda: pl.BlockSpec((8, tm), lambda b, t: (0, b * nt + t))
    return pl.pallas_call(
        functools.partial(_mixout_body, tm=tm),
        grid=(B, nt),
        in_specs=[act_spec(), act_spec(), act_spec(),
                  pl.BlockSpec((1, tm, D_MODEL), lambda b, t: (b, t, 0)),
                  pl.BlockSpec((SGU_GROUPS, SGU_CHUNK, SGU_CHUNK), lambda b, t: (0, 0, 0)),
                  const2((SGU_CHUNK, SGU_WIDTH)),
                  const2((D_MODEL, D_MODEL)),
                  const2((1, D_MODEL)), const2((1, D_MODEL)),
                  const2((128, D_MODEL)), const2((128, 128))],
        out_specs=[pl.BlockSpec((1, tm, D_MODEL), lambda b, t: (b, t, 0)), tok_spec(), tok_spec()],
        out_shape=[jax.ShapeDtypeStruct((B, T, D_MODEL), f32),
                   jax.ShapeDtypeStruct((8, B * T), i32),
                   jax.ShapeDtypeStruct((8, B * T), f32)],
        scratch_shapes=[pltpu.VMEM((tm, D_MODEL), bf16)],
        compiler_params=_cparams(("arbitrary", "arbitrary")),
        name="mixout",
    )(ydn, u, vln, x, ws, bsp, wout, g1, b1, wrt, brt)


def _rank_body(ids_ref, rank_ref, cnt_ref, base_ref, *, tm):
    @pl.when(pl.program_id(0) == 0)
    def _():
        base_ref[...] = jnp.zeros(base_ref.shape, f32)

    sub = _iota2((N_EXPERTS, tm), 0)
    oh1 = (sub == ids_ref[0:1, :]).astype(f32)
    oh2 = (sub == ids_ref[1:2, :]).astype(f32)
    oh = (oh1 + oh2).astype(bf16)
    ti = _iota2((tm, tm), 0)
    tj = _iota2((tm, tm), 1)
    before = (ti < tj).astype(bf16)
    prefix = jnp.dot(oh, before, preferred_element_type=f32)
    tot = prefix + jnp.concatenate([base_ref[...]] * (tm // 128), axis=1)
    r1 = jnp.sum(oh1 * tot, axis=0, keepdims=True)
    r2 = jnp.sum(oh2 * tot, axis=0, keepdims=True)
    sub8 = _iota2((8, tm), 0)
    rank_ref[...] = jnp.where(sub8 == 0, r1, jnp.where(sub8 == 1, r2, 0.0)).astype(i32)
    base_ref[...] = base_ref[...] + jnp.dot(oh, jnp.ones((tm, 128), bf16), preferred_element_type=f32)
    cnt_ref[...] = base_ref[...]


def _stage_rank(ids, *, tm):
    n = ids.shape[1]
    return pl.pallas_call(
        functools.partial(_rank_body, tm=tm),
        grid=(n // tm,),
        in_specs=[pl.BlockSpec((8, tm), lambda i: (0, i))],
        out_specs=[pl.BlockSpec((8, tm), lambda i: (0, i)),
                   pl.BlockSpec((N_EXPERTS, 128), lambda i: (0, 0))],
        out_shape=[jax.ShapeDtypeStruct((8, n), i32), jax.ShapeDtypeStruct((N_EXPERTS, 128), f32)],
        scratch_shapes=[pltpu.VMEM((N_EXPERTS, 128), f32)],
        compiler_params=_cparams(("arbitrary",)),
        name="moe_rank",
    )(ids)


def _dest_body(cnt_ref, ids_ref, rank_ref, dest_ref, meta_ref, blk_ref, *, tm, nb_pad):
    cnt = cnt_ref[...]
    padded = jnp.floor((cnt + (MOE_BLOCK - 1)) * (1.0 / MOE_BLOCK)) * MOE_BLOCK
    ei = _iota2((N_EXPERTS, N_EXPERTS), 0)
    ej = _iota2((N_EXPERTS, N_EXPERTS), 1)
    pends = jnp.dot((ei >= ej).astype(f32), padded, precision=HIGHEST, preferred_element_type=f32)
    pstart = pends - padded
    sub = _iota2((N_EXPERTS, tm), 0)
    pst = jnp.concatenate([pstart] * (tm // 128), axis=1)
    d1 = jnp.sum(jnp.where(sub == ids_ref[0:1, :], pst, 0.0), axis=0, keepdims=True)
    d2 = jnp.sum(jnp.where(sub == ids_ref[1:2, :], pst, 0.0), axis=0, keepdims=True)
    sub8 = _iota2((8, tm), 0)
    dest_ref[...] = jnp.where(sub8 == 0, d1, jnp.where(sub8 == 1, d2, 0.0)).astype(i32) + rank_ref[...]

    @pl.when(pl.program_id(0) == 0)
    def _():
        s64 = _iota2((N_EXPERTS, 128), 0)
        l64 = _iota2((N_EXPERTS, 128), 1)
        diag = s64 == l64
        fill_off = jnp.sum(jnp.where(diag, pstart + cnt, 0.0), axis=0, keepdims=True)
        fill_n = jnp.sum(jnp.where(diag, padded - cnt, 0.0), axis=0, keepdims=True)
        nused = pends[N_EXPERTS - 1:N_EXPERTS, :] * (1.0 / MOE_BLOCK)
        m8 = _iota2((8, 128), 0)
        meta_ref[...] = jnp.where(m8 == 0, fill_off, jnp.where(m8 == 1, fill_n, jnp.where(m8 == 2, nused, 0.0))).astype(i32)
        bstart = (_iota2((N_EXPERTS, nb_pad), 1) * MOE_BLOCK).astype(f32)
        pe = jnp.concatenate([pends] * (nb_pad // 128), axis=1)
        be = jnp.sum((pe <= bstart).astype(f32), axis=0, keepdims=True)
        be = jnp.minimum(be, float(N_EXPERTS - 1))
        blk_ref[...] = jnp.broadcast_to(be, (8, nb_pad)).astype(i32)


def _stage_dest(cnt, ids, rank, *, tm, nb_pad):
    n = ids.shape[1]
    tok = lambda: pl.BlockSpec((8, tm), lambda i: (0, i))
    return pl.pallas_call(
        functools.partial(_dest_body, tm=tm, nb_pad=nb_pad),
        grid=(n // tm,),
        in_specs=[pl.BlockSpec((N_EXPERTS, 128), lambda i: (0, 0)), tok(), tok()],
        out_specs=[tok(), pl.BlockSpec((8, 128), lambda i: (0, 0)), pl.BlockSpec((8, nb_pad), lambda i: (0, 0))],
        out_shape=[jax.ShapeDtypeStruct((8, n), i32), jax.ShapeDtypeStruct((8, 128), i32),
                   jax.ShapeDtypeStruct((8, nb_pad), i32)],
        compiler_params=_cparams(("arbitrary",)),
        name="moe_dest",
    )(cnt, ids, rank)


def _dispatch_body(fill_off_ref, fill_n_ref, nused_ref, dest_ref, h_ref, xs_ref, zero_ref, sem, zsem, *, tm):
    def row_copy(t, d):
        return pltpu.make_async_copy(h_ref.at[pl.ds(t, 1)], xs_ref.at[pl.ds(d, 1)], sem)

    def issue(t, carry):
        row_copy(t, dest_ref[0, 0, t]).start()
        row_copy(t, dest_ref[0, 1, t]).start()
        return carry

    lax.fori_loop(0, tm, issue, 0)

    @pl.when(pl.program_id(0) == 0)
    def _():
        zero_ref[...] = jnp.zeros(zero_ref.shape, f32)

        def zero_copy(d):
            return pltpu.make_async_copy(zero_ref.at[pl.ds(0, 1)], xs_ref.at[pl.ds(d, 1)], zsem)

        def fill(e, carry):
            off = fill_off_ref[e]
            lax.fori_loop(0, fill_n_ref[e], lambda r, c: (zero_copy(off + r).start(), c)[1], 0)
            return carry

        def fill_wait(e, carry):
            lax.fori_loop(0, fill_n_ref[e], lambda r, c: (zero_copy(0).wait(), c)[1], 0)
            return carry

        lax.fori_loop(0, N_EXPERTS, fill, 0)
        lax.fori_loop(0, N_EXPERTS, fill_wait, 0)

        def tail_copy(b):
            row = pl.multiple_of(b * MOE_BLOCK, MOE_BLOCK)
            return pltpu.make_async_copy(zero_ref, xs_ref.at[pl.ds(row, MOE_BLOCK)], zsem)

        nblocks = xs_ref.shape[0] // MOE_BLOCK
        lax.fori_loop(nused_ref[0], nblocks, lambda b, c: (tail_copy(b).start(), c)[1], 0)
        lax.fori_loop(nused_ref[0], nblocks, lambda b, c: (tail_copy(0).wait(), c)[1], 0)

    def drain(t, carry):
        row_copy(0, 0).wait()
        row_copy(0, 0).wait()
        return carry

    lax.fori_loop(0, tm, drain, 0)


def _stage_dispatch(fill_off, fill_n, nused, dest3, h2, p_rows, *, tm):
    n = h2.shape[0]
    return pl.pallas_call(
        functools.partial(_dispatch_body, tm=tm),
        grid_spec=pltpu.PrefetchScalarGridSpec(
            num_scalar_prefetch=3,
            grid=(n // tm,),
            in_specs=[pl.BlockSpec((1, 2, tm), lambda i, fo, fn, nu: (i, 0, 0), memory_space=pltpu.SMEM),
                      pl.BlockSpec((tm, D_MODEL), lambda i, fo, fn, nu: (i, 0))],
            out_specs=pl.BlockSpec(memory_space=pl.ANY),
            scratch_shapes=[pltpu.VMEM((MOE_BLOCK, D_MODEL), f32), pltpu.SemaphoreType.DMA, pltpu.SemaphoreType.DMA],
        ),
        out_shape=jax.ShapeDtypeStruct((p_rows, D_MODEL), f32),
        compiler_params=_cparams(("arbitrary",)),
        name="moe_dispatch",
    )(fill_off, fill_n, nused, dest3, h2)


def _experts_body(blk_ref, nused_ref, xs_ref, wg_ref, wu_ref, wd_ref, ys_ref):
    used = pl.program_id(0) < nused_ref[0]

    @pl.when(used)
    def _():
        xb = xs_ref[...].astype(bf16)
        gate = jnp.dot(xb, wg_ref[0].astype(bf16), preferred_element_type=f32)
        up = jnp.dot(xb, wu_ref[0].astype(bf16), preferred_element_type=f32)
        hid = (gate * _sigmoid(gate) * up).astype(bf16)
        ys_ref[...] = jnp.dot(hid, wd_ref[0].astype(bf16), preferred_element_type=f32)

    @pl.when(jnp.logical_not(used))
    def _():
        ys_ref[...] = jnp.zeros(ys_ref.shape, f32)


def _stage_experts(blk_e, nused, xs, w_gate, w_up, w_down):
    p_rows = xs.shape[0]
    nb = p_rows // MOE_BLOCK

    def x_map(i, blk, nu):
        return (jnp.minimum(i, nu[0] - 1), 0)

    def row_map(i, blk, nu):
        return (i, 0)

    def w_map(i, blk, nu):
        return (blk[jnp.minimum(i, nu[0] - 1)], 0, 0)

    return pl.pallas_call(
        _experts_body,
        grid_spec=pltpu.PrefetchScalarGridSpec(
            num_scalar_prefetch=2,
            grid=(nb,),
            in_specs=[pl.BlockSpec((MOE_BLOCK, D_MODEL), x_map),
                      pl.BlockSpec((1, D_MODEL, D_EXPERT), w_map),
                      pl.BlockSpec((1, D_MODEL, D_EXPERT), w_map),
                      pl.BlockSpec((1, D_EXPERT, D_MODEL), w_map)],
            out_specs=pl.BlockSpec((MOE_BLOCK, D_MODEL), row_map),
        ),
        out_shape=jax.ShapeDtypeStruct((p_rows, D_MODEL), f32),
        compiler_params=_cparams(("arbitrary",)),
        name="moe_experts",
    )(blk_e, nused, xs, w_gate, w_up, w_down)


def _combine_body(dest_ref, h_ref, wts_ref, g2_ref, b2_ref, ys_ref, o_ref, y1_ref, y2_ref, sem, *, tm):
    def row_copy(buf_ref, t, d):
        return pltpu.make_async_copy(ys_ref.at[pl.ds(d, 1)], buf_ref.at[pl.ds(t, 1)], sem)

    def issue(t, carry):
        row_copy(y1_ref, t, dest_ref[0, 0, t]).start()
        row_copy(y2_ref, t, dest_ref[0, 1, t]).start()
        return carry

    lax.fori_loop(0, tm, issue, 0)

    def drain(t, carry):
        row_copy(y1_ref, 0, 0).wait()
        row_copy(y2_ref, 0, 0).wait()
        return carry

    lax.fori_loop(0, tm, drain, 0)

    pieces = []
    for c in range(tm // 128):
        ls = slice(c * 128, (c + 1) * 128)
        w1c = jnp.broadcast_to(wts_ref[0:1, ls], (128, 128)).T
        w2c = jnp.broadcast_to(wts_ref[1:2, ls], (128, 128)).T
        w1f = jnp.concatenate([w1c] * (D_MODEL // 128), axis=1)
        w2f = jnp.concatenate([w2c] * (D_MODEL // 128), axis=1)
        pieces.append(w1f * y1_ref[ls, :] + w2f * y2_ref[ls, :])
    ffn = jnp.concatenate(pieces, axis=0)
    hp = DEEPNORM_ALPHA * h_ref[...] + ffn
    mu = jnp.mean(hp, axis=-1, keepdims=True)
    hc = hp - mu
    var = jnp.mean(hc * hc, axis=-1, keepdims=True)
    o_ref[...] = hc * lax.rsqrt(var + LN_EPS) * g2_ref[...] + b2_ref[...]


def _stage_combine(dest3, h2, wts, g2, b2, ys, *, tm):
    n = h2.shape[0]
    return pl.pallas_call(
        functools.partial(_combine_body, tm=tm),
        grid=(n // tm,),
        in_specs=[pl.BlockSpec((1, 2, tm), lambda i: (i, 0, 0), memory_space=pltpu.SMEM),
                  pl.BlockSpec((tm, D_MODEL), lambda i: (i, 0)),
                  pl.BlockSpec((8, tm), lambda i: (0, i)),
                  pl.BlockSpec((1, D_MODEL), lambda i: (0, 0)),
                  pl.BlockSpec((1, D_MODEL), lambda i: (0, 0)),
                  pl.BlockSpec(memory_space=pl.ANY)],
        out_specs=pl.BlockSpec((tm, D_MODEL), lambda i: (i, 0)),
        out_shape=jax.ShapeDtypeStruct((n, D_MODEL), f32),
        scratch_shapes=[pltpu.VMEM((tm, D_MODEL), f32), pltpu.VMEM((tm, D_MODEL), f32), pltpu.SemaphoreType.DMA],
        compiler_params=_cparams(("arbitrary",)),
        name="moe_combine",
    )(dest3, h2, wts, g2, b2, ys)


def _layer(h, w_in, conv_w, a_log, dt_bias, dn_norm_w, sgu_ln_g, sgu_ln_b, w_spatial, b_spatial, w_out,
           ln1_g, ln1_b, w_rg, b_rg, w_re, b_re, w_gate, w_up, w_down, ln2_g, ln2_b,
           *, tm_in, dn_chunks, tm_mix, tm_rank, tm_disp, tm_comb):
    B, T, _ = h.shape
    n = B * T
    qkvz = 4 * DN_WIDTH
    ba0 = qkvz
    uv0 = qkvz + 2 * DN_HEADS
    w_cols = jnp.concatenate(
        [w_in[:, :qkvz], w_in[:, uv0:], w_in[:, ba0:uv0], jnp.zeros((D_MODEL, 128 - 2 * DN_HEADS), f32)], axis=1).astype(bf16)
    wbat = w_in[:, ba0:uv0].T.astype(bf16)
    pcol = jnp.zeros((8, 128), f32).at[0, DN_HEADS:2 * DN_HEADS].set(a_log).at[1, DN_HEADS:2 * DN_HEADS].set(dt_bias)
    prow = jnp.zeros((2, 8, 128), f32)
    prow = prow.at[0, DN_HEADS:2 * DN_HEADS, :].set(jnp.broadcast_to(a_log[:, None], (DN_HEADS, 128)))
    prow = prow.at[1, DN_HEADS:2 * DN_HEADS, :].set(jnp.broadcast_to(dt_bias[:, None], (DN_HEADS, 128)))

    q, k, v, z, u, vln, gcol, grow = _stage_inproj(
        h, w_cols, wbat, conv_w, pcol, prow, sgu_ln_g[None, :], sgu_ln_b[None, :], tm=tm_in)
    ydn = _stage_deltanet(q, k, v, z, gcol, grow, dn_norm_w[None, :], nch=dn_chunks)

    bsp = jnp.broadcast_to(b_spatial.T[:, :, None], (SGU_CHUNK, SGU_GROUPS, SGU_CHUNK)).reshape(SGU_CHUNK, SGU_WIDTH)
    wrt = jnp.zeros((128, D_MODEL), f32).at[0:MOE_GROUPS].set(w_rg.T).at[MOE_GROUPS:MOE_GROUPS + N_EXPERTS].set(w_re.T).astype(bf16)
    brt = jnp.zeros((128,), f32).at[0:MOE_GROUPS].set(b_rg).at[MOE_GROUPS:MOE_GROUPS + N_EXPERTS].set(b_re)
    brt = jnp.broadcast_to(brt[:, None], (128, 128))
    h1, ids, wts = _stage_mixout(ydn, u, vln, h, w_spatial, bsp, w_out.astype(bf16), ln1_g[None, :], ln1_b[None, :],
                                 wrt, brt, tm=tm_mix)

    p_rows = (-(-(n * 2) // MOE_BLOCK)) * MOE_BLOCK + N_EXPERTS * MOE_BLOCK
    nb = p_rows // MOE_BLOCK
    nb_pad = (-(-nb // 128)) * 128
    rank, cnt = _stage_rank(ids, tm=tm_rank)
    dest, meta, blk = _stage_dest(cnt, ids, rank, tm=tm_rank, nb_pad=nb_pad)

    h2 = h1.reshape(n, D_MODEL)
    dest_d = dest[0:2].reshape(2, n // tm_disp, tm_disp).transpose(1, 0, 2)
    xs = _stage_dispatch(meta[0, :N_EXPERTS], meta[1, :N_EXPERTS], meta[2, 0:1], dest_d, h2, p_rows, tm=tm_disp)
    ys = _stage_experts(blk[0, :nb], meta[2, 0:1], xs, w_gate, w_up, w_down)
    dest_c = dest[0:2].reshape(2, n // tm_comb, tm_comb).transpose(1, 0, 2)
    out = _stage_combine(dest_c, h2, wts, ln2_g[None, :], ln2_b[None, :], ys, tm=tm_comb)
    return out.reshape(B, T, D_MODEL)


def kernel(x, w_in, conv_w, a_log, dt_bias, dn_norm_w, sgu_ln_g, sgu_ln_b, w_spatial, b_spatial, w_out, ln1_g, ln1_b, w_router_group, b_router_group, w_router_expert, b_router_expert, w_gate, w_up, w_down, ln2_g, ln2_b):
    h = x
    for l in range(w_in.shape[0]):
        h = _layer(h, w_in[l], conv_w[l], a_log[l], dt_bias[l], dn_norm_w[l], sgu_ln_g[l], sgu_ln_b[l],
                   w_spatial[l], b_spatial[l], w_out[l], ln1_g[l], ln1_b[l],
                   w_router_group[l], b_router_group[l], w_router_expert[l], b_router_expert[l],
                   w_gate[l], w_up[l], w_down[l], ln2_g[l], ln2_b[l],
                   tm_in=512, dn_chunks=4, tm_mix=512, tm_rank=512, tm_disp=256, tm_comb=256)
    return h
```

```python
import functools

import jax
import jax.numpy as jnp
from jax import lax
from jax.experimental import pallas as pl
from jax.experimental.pallas import tpu as pltpu

f32 = jnp.float32
bf16 = jnp.bfloat16
i32 = jnp.int32

D_MODEL = 1024
DN_WIDTH = 512
DN_HEADS = 4
HEAD_DIM = 128
CONV_K = 4
SGU_WIDTH = 512
SGU_GROUPS = 4
SGU_CHUNK = 128
DN_CHUNK = 128
MOE_GROUPS = 8
EXPERTS_PER_GROUP = 8
N_EXPERTS = 64
D_EXPERT = 512
MOE_BLOCK = 256
ROW_TILE = D_MODEL // 128
DEEPNORM_ALPHA = 2.0 ** 0.25
LN_EPS = 1e-5
RMS_EPS = 1e-6
HIGHEST = lax.Precision.HIGHEST
VMEM_LIMIT_BYTES = 56 * 1024 * 1024

NT_DIMS = (((1,), (1,)), ((), ()))


def _cparams(sem, flags=None):
    return pltpu.CompilerParams(dimension_semantics=sem, vmem_limit_bytes=VMEM_LIMIT_BYTES, flags=flags)


def _sigmoid(x):
    return 1.0 / (1.0 + jnp.exp(-x))


def _softplus(x):
    return jnp.maximum(x, 0.0) + jnp.log1p(jnp.exp(-jnp.abs(x)))


def _gelu_tanh(x):
    c = 0.7978845608028654
    return x * (0.5 * (1.0 + jnp.tanh(c * (x + 0.044715 * (x * x * x)))))


def _iota2(shape, axis):
    return lax.broadcasted_iota(i32, shape, axis)


def _inproj_body(x_ref, w_ref, wbat_ref, convw_ref, pcol_ref, prow_ref, lng_ref, lnb_ref,
                 q_ref, k_ref, v_ref, z_ref, u_ref, vln_ref, gcol_ref, grow_ref, ext_ref, *, tm):
    @pl.when(pl.program_id(1) == 0)
    def _():
        ext_ref[0:8, :] = jnp.zeros((8, 3 * DN_WIDTH), f32)

    xb = x_ref[0].astype(bf16)

    for part, out_ref in enumerate((q_ref, k_ref, v_ref)):
        c0 = part * DN_WIDTH
        pre = jnp.dot(xb, w_ref[:, c0:c0 + DN_WIDTH], preferred_element_type=f32)
        ext_ref[8:8 + tm, c0:c0 + DN_WIDTH] = pre
        y = convw_ref[3:4, c0:c0 + DN_WIDTH] * pre
        for j in range(CONV_K - 1):
            y = y + convw_ref[j:j + 1, c0:c0 + DN_WIDTH] * ext_ref[5 + j:5 + j + tm, c0:c0 + DN_WIDTH]
        y = y * _sigmoid(y)
        if part < 2:
            scale = HEAD_DIM ** -0.5 if part == 0 else 1.0
            for h in range(DN_HEADS):
                yh = y[:, h * HEAD_DIM:(h + 1) * HEAD_DIM]
                ss = jnp.sum(yh * yh, axis=-1, keepdims=True)
                out_ref[0, :, h * HEAD_DIM:(h + 1) * HEAD_DIM] = (yh * lax.rsqrt(ss + RMS_EPS) * scale).astype(bf16)
        else:
            out_ref[0] = y.astype(bf16)
    ext_ref[0:8, :] = ext_ref[tm:tm + 8, :]

    zc = 3 * DN_WIDTH
    pz = jnp.dot(xb, w_ref[:, zc:zc + DN_WIDTH], preferred_element_type=f32)
    z_ref[0] = (pz * _sigmoid(pz)).astype(bf16)

    uc = zc + DN_WIDTH
    pu = jnp.dot(xb, w_ref[:, uc:uc + SGU_WIDTH], preferred_element_type=f32)
    u_ref[0] = _gelu_tanh(pu).astype(bf16)
    vc = uc + SGU_WIDTH
    pv = _gelu_tanh(jnp.dot(xb, w_ref[:, vc:vc + SGU_WIDTH], preferred_element_type=f32))
    for g in range(SGU_GROUPS):
        sl = slice(g * SGU_CHUNK, (g + 1) * SGU_CHUNK)
        vg = pv[:, sl]
        mu = jnp.mean(vg, axis=-1, keepdims=True)
        vcn = vg - mu
        var = jnp.mean(vcn * vcn, axis=-1, keepdims=True)
        vln_ref[0, :, sl] = (vcn * lax.rsqrt(var + LN_EPS) * lng_ref[:, sl] + lnb_ref[:, sl]).astype(bf16)

    bc = vc + SGU_WIDTH
    pba = jnp.dot(xb, w_ref[:, bc:bc + 128], preferred_element_type=f32)
    lane = _iota2((DN_CHUNK, 128), 1)
    is_g = (lane >= DN_HEADS) & (lane < 2 * DN_HEADS)
    beta = _sigmoid(pba)
    gval = -jnp.exp(pcol_ref[0:1, :]) * _softplus(pba + pcol_ref[1:2, :])
    ci = _iota2((DN_CHUNK, DN_CHUNK), 0)
    cj = _iota2((DN_CHUNK, DN_CHUNK), 1)
    ltri = (ci >= cj).astype(f32)
    utri = (ci <= cj).astype(f32)
    pbat = lax.dot_general(wbat_ref[...], xb, NT_DIMS, preferred_element_type=f32)
    sub = _iota2((8, DN_CHUNK), 0)
    for c in range(tm // DN_CHUNK):
        rs = slice(c * DN_CHUNK, (c + 1) * DN_CHUNK)
        gc = jnp.dot(ltri, jnp.where(is_g, gval[rs], 0.0), precision=HIGHEST, preferred_element_type=f32)
        gcol_ref[0, rs, :] = jnp.where(lane < DN_HEADS, beta[rs], gc)
        gt = -jnp.exp(prow_ref[0]) * _softplus(pbat[:, rs] + prow_ref[1])
        gt = jnp.where(sub >= DN_HEADS, gt, 0.0)
        grow_ref[0, :, rs] = jnp.dot(gt, utri, precision=HIGHEST, preferred_element_type=f32)


def _stage_inproj(x, w_re, wbat, conv_w, pcol, prow, lng, lnb, *, tm):
    B, T, _ = x.shape
    wcols = w_re.shape[1]
    grid = (B, T // tm)
    act = lambda: jax.ShapeDtypeStruct((B, T, DN_WIDTH), bf16)
    act_spec = lambda: pl.BlockSpec((1, tm, DN_WIDTH), lambda b, t: (b, t, 0))
    const2 = lambda shp: pl.BlockSpec(shp, lambda b, t: (0, 0))
    return pl.pallas_call(
        functools.partial(_inproj_body, tm=tm),
        grid=grid,
        in_specs=[
            pl.BlockSpec((1, tm, D_MODEL), lambda b, t: (b, t, 0)),
            const2((D_MODEL, wcols)),
            const2((8, D_MODEL)),
            const2((CONV_K, 3 * DN_WIDTH)),
            const2((8, 128)),
            pl.BlockSpec((2, 8, 128), lambda b, t: (0, 0, 0)),
            const2((1, SGU_WIDTH)),
            const2((1, SGU_WIDTH)),
        ],
        out_specs=[act_spec() for _ in range(6)] + [
            pl.BlockSpec((1, tm, 128), lambda b, t: (b, t, 0)),
            pl.BlockSpec((1, 8, tm), lambda b, t: (b, 0, t)),
        ],
        out_shape=[act() for _ in range(6)] + [
            jax.ShapeDtypeStruct((B, T, 128), f32),
            jax.ShapeDtypeStruct((B, 8, T), f32),
        ],
        scratch_shapes=[pltpu.VMEM((tm + 8, 3 * DN_WIDTH), f32)],
        compiler_params=_cparams(("arbitrary", "arbitrary")),
        name="inproj",
    )(x, w_re, wbat, conv_w, pcol, prow, lng, lnb)


def _mm(a, b):
    return jnp.dot(a.astype(bf16), b.astype(bf16), preferred_element_type=f32)


def _unit_lower_inverse(nmats, ii, jj):
    n = nmats[0].shape[0]
    eye = (ii == jj).astype(f32)
    leaf = jnp.right_shift(ii, 3) == jnp.right_shift(jj, 3)
    dblk = [jnp.where(leaf, m, 0.0) for m in nmats]
    s1 = [_mm(d, d) for d in dblk]
    r1 = [eye - d for d in dblk]
    both = [_mm(s, jnp.concatenate([s, r], axis=1)) for s, r in zip(s1, r1)]
    r2 = [r + bo[:, n:] for r, bo in zip(r1, both)]
    xs = [r + _mm(bo[:, :n], r) for r, bo in zip(r2, both)]
    shift = 3
    while (1 << shift) < n:
        same_parent = jnp.right_shift(ii, shift + 1) == jnp.right_shift(jj, shift + 1)
        level = same_parent & (jnp.right_shift(ii, shift) != jnp.right_shift(jj, shift))
        ys = [_mm(jnp.where(level, m, 0.0), x) for m, x in zip(nmats, xs)]
        xs = [x - _mm(x, y) for x, y in zip(xs, ys)]
        shift += 1
    return xs


def _deltanet_body(q_ref, k_ref, v_ref, z_ref, gcol_ref, grow_ref, nw_ref, y_ref, s_ref, *, nch):
    C = DN_CHUNK
    H = range(DN_HEADS)
    P = [(c, h) for c in range(nch) for h in H]

    @pl.when(pl.program_id(1) == 0)
    def _():
        s_ref[...] = jnp.zeros(s_ref.shape, f32)

    ii = _iota2((C, C), 0)
    jj = _iota2((C, C), 1)
    causal = ii >= jj
    rs = [slice(c * C, (c + 1) * C) for c in range(nch)]
    hs = [slice(h * HEAD_DIM, (h + 1) * HEAD_DIM) for h in H]
    gcol = [gcol_ref[0, rs[c], :] for c in range(nch)]
    qh = {p: q_ref[0, rs[p[0]], hs[p[1]]] for p in P}
    kh = {p: k_ref[0, rs[p[0]], hs[p[1]]] for p in P}
    vh = {p: v_ref[0, rs[p[0]], hs[p[1]]] for p in P}
    gc_b = {(c, h): jnp.broadcast_to(gcol[c][:, DN_HEADS + h:DN_HEADS + h + 1], (C, HEAD_DIM)) for c, h in P}
    beta_b = {(c, h): jnp.broadcast_to(gcol[c][:, h:h + 1], (C, HEAD_DIM)) for c, h in P}
    gc_r = {(c, h): jnp.broadcast_to(grow_ref[0, DN_HEADS + h:DN_HEADS + h + 1, rs[c]], (C, C)) for c, h in P}
    decay = {p: jnp.exp(jnp.where(causal, gc_b[p] - gc_r[p], -1e30)) for p in P}

    kf = {p: kh[p].astype(f32) for p in P}
    kb = {p: kf[p] * beta_b[p] for p in P}
    kk = {p: lax.dot_general(kb[p].astype(bf16), kh[p], NT_DIMS, preferred_element_type=f32) for p in P}
    a_intra = {p: lax.dot_general(qh[p], kh[p], NT_DIMS, preferred_element_type=f32) * decay[p] for p in P}
    nmat = [jnp.where(ii > jj, kk[p] * decay[p], 0.0) for p in P]
    tinv = dict(zip(P, _unit_lower_inverse(nmat, ii, jj)))

    eg = {p: jnp.exp(gc_b[p]) for p in P}
    rhs = {p: jnp.concatenate([vh[p].astype(f32) * beta_b[p], kb[p] * eg[p]], axis=1) for p in P}
    sol = {p: _mm(tinv[p], rhs[p]) for p in P}

    q_dec = {p: qh[p].astype(f32) * eg[p] for p in P}
    g_last = {p: gc_b[p][C - 1:C, :] for p in P}
    kdt = {p: (kf[p] * jnp.exp(g_last[p] - gc_b[p])).T for p in P}

    state = [s_ref[h] for h in H]
    for c in range(nch):
        m1 = [_mm(jnp.concatenate([sol[c, h][:, HEAD_DIM:], q_dec[c, h]], axis=0), state[h]) for h in H]
        v_new = [sol[c, h][:, :HEAD_DIM] - m1[h][:C] for h in H]
        m2 = [_mm(jnp.concatenate([a_intra[c, h], kdt[c, h]], axis=0), v_new[h]) for h in H]
        state = [state[h] * jnp.exp(g_last[c, h]) + m2[h][C:] for h in H]
        for h in H:
            o = m1[h][C:] + m2[h][:C]
            rms = lax.rsqrt(jnp.mean(o * o, axis=-1, keepdims=True) + RMS_EPS)
            y_ref[0, rs[c], hs[h]] = (o * rms * nw_ref[...] * z_ref[0, rs[c], hs[h]].astype(f32)).astype(bf16)
    for h in H:
        s_ref[h] = state[h]


def _stage_deltanet(q, k, v, z, gcol, grow, norm_w, *, nch):
    B, T, _ = q.shape
    tt = nch * DN_CHUNK
    act_spec = lambda: pl.BlockSpec((1, tt, DN_WIDTH), lambda b, t: (b, t, 0))
    return pl.pallas_call(
        functools.partial(_deltanet_body, nch=nch),
        grid=(B, T // tt),
        in_specs=[act_spec(), act_spec(), act_spec(), act_spec(),
                  pl.BlockSpec((1, tt, 128), lambda b, t: (b, t, 0)),
                  pl.BlockSpec((1, 8, tt), lambda b, t: (b, 0, t)),
                  pl.BlockSpec((1, HEAD_DIM), lambda b, t: (0, 0))],
        out_specs=act_spec(),
        out_shape=jax.ShapeDtypeStruct((B, T, DN_WIDTH), bf16),
        scratch_shapes=[pltpu.VMEM((DN_HEADS, HEAD_DIM, HEAD_DIM), f32)],
        compiler_params=_cparams(("arbitrary", "arbitrary")),
        name="deltanet",
    )(q, k, v, z, gcol, grow, norm_w)


def _mixout_body(ydn_ref, u_ref, vln_ref, x_ref, ws_ref, bsp_ref, wout_ref, g1_ref, b1_ref, wrt_ref, brt_ref,
                 h_ref, ids_ref, wts_ref, ycat_ref, *, tm):
    C = SGU_CHUNK
    ii = _iota2((C, C), 0)
    jj = _iota2((C, C), 1)
    ycat_ref[:, 0:DN_WIDTH] = ydn_ref[0]
    for g in range(SGU_GROUPS):
        gs = slice(g * C, (g + 1) * C)
        wsg = jnp.where(ii >= jj, ws_ref[g], 0.0).astype(bf16)
        for c in range(tm // C):
            rs = slice(c * C, (c + 1) * C)
            mixed = jnp.dot(wsg, vln_ref[0, rs, gs], preferred_element_type=f32) + bsp_ref[:, gs]
            ycat_ref[rs, DN_WIDTH + g * C:DN_WIDTH + (g + 1) * C] = (u_ref[0, rs, gs].astype(f32) * mixed).astype(bf16)

    mix = jnp.dot(ycat_ref[...], wout_ref[...], preferred_element_type=f32)
    hp = DEEPNORM_ALPHA * x_ref[0] + mix
    mu = jnp.mean(hp, axis=-1, keepdims=True)
    hc = hp - mu
    var = jnp.mean(hc * hc, axis=-1, keepdims=True)
    h1 = hc * lax.rsqrt(var + LN_EPS) * g1_ref[...] + b1_ref[...]
    h_ref[0] = h1

    logits = lax.dot_general(wrt_ref[...], h1.astype(bf16), NT_DIMS, preferred_element_type=f32)
    logits = logits + jnp.concatenate([brt_ref[...]] * (tm // 128), axis=1)
    sub = _iota2((8, tm), 0)
    subf = sub.astype(f32)
    gl = logits[0:8]
    gmax = jnp.max(gl, axis=0, keepdims=True)
    g_idx = jnp.min(jnp.where(gl == gmax, subf, float(MOE_GROUPS)), axis=0, keepdims=True)
    p_group = 1.0 / jnp.sum(jnp.exp(gl - gmax), axis=0, keepdims=True)
    within = jnp.zeros((8, tm), f32)
    for g in range(MOE_GROUPS):
        within = within + jnp.where(g_idx == float(g), logits[8 + 8 * g:16 + 8 * g], 0.0)
    m1 = jnp.max(within, axis=0, keepdims=True)
    i1 = jnp.min(jnp.where(within == m1, subf, float(EXPERTS_PER_GROUP)), axis=0, keepdims=True)
    rest = jnp.where(subf == i1, -jnp.inf, within)
    m2 = jnp.max(rest, axis=0, keepdims=True)
    i2 = jnp.min(jnp.where(rest == m2, subf, float(EXPERTS_PER_GROUP)), axis=0, keepdims=True)
    e = jnp.exp(m2 - m1)
    w1 = p_group / (1.0 + e)
    w2 = p_group * e / (1.0 + e)
    e1 = g_idx * float(EXPERTS_PER_GROUP) + i1
    e2 = g_idx * float(EXPERTS_PER_GROUP) + i2
    ids_ref[...] = jnp.where(sub == 0, e1, jnp.where(sub == 1, e2, 0.0)).astype(i32)
    wts_ref[...] = jnp.where(sub == 0, w1, jnp.where(sub == 1, w2, 0.0))


def _stage_mixout(ydn, u, vln, x, ws, bsp, wout, g1, b1, wrt, brt, *, tm):
    B, T, _ = x.shape
    nt = T // tm
    act_spec = lambda: pl.BlockSpec((1, tm, DN_WIDTH), lambda b, t: (b, t, 0))
    const2 = lambda shp: pl.BlockSpec(shp, lambda b, t: (0, 0))
    tok_spec = lambda: pl.BlockSpec((8, tm), lambda b, t: (0, b * nt + t))
    return pl.pallas_call(
        functools.partial(_mixout_body, tm=tm),
        grid=(B, nt),
        in_specs=[act_spec(), act_spec(), act_spec(),
                  pl.BlockSpec((1, tm, D_MODEL), lambda b, t: (b, t, 0)),
                  pl.BlockSpec((SGU_GROUPS, SGU_CHUNK, SGU_CHUNK), lambda b, t: (0, 0, 0)),
                  const2((SGU_CHUNK, SGU_WIDTH)),
                  const2((D_MODEL, D_MODEL)),
                  const2((1, D_MODEL)), const2((1, D_MODEL)),
                  const2((128, D_MODEL)), const2((128, 128))],
        out_specs=[pl.BlockSpec((1, tm, D_MODEL), lambda b, t: (b, t, 0)), tok_spec(), tok_spec()],
        out_shape=[jax.ShapeDtypeStruct((B, T, D_MODEL), f32),
                   jax.ShapeDtypeStruct((8, B * T), i32),
                   jax.ShapeDtypeStruct((8, B * T), f32)],
        scratch_shapes=[pltpu.VMEM((tm, D_MODEL), bf16)],
        compiler_params=_cparams(("arbitrary", "arbitrary")),
        name="mixout",
    )(ydn, u, vln, x, ws, bsp, wout, g1, b1, wrt, brt)


def _rank_body(ids_ref, rank_ref, cnt_ref, base_ref, *, tm):
    @pl.when(pl.program_id(0) == 0)
    def _():
        base_ref[...] = jnp.zeros(base_ref.shape, f32)

    sub = _iota2((N_EXPERTS, tm), 0)
    oh1 = (sub == ids_ref[0:1, :]).astype(f32)
    oh2 = (sub == ids_ref[1:2, :]).astype(f32)
    oh = (oh1 + oh2).astype(bf16)
    ti = _iota2((tm, tm), 0)
    tj = _iota2((tm, tm), 1)
    before = (ti < tj).astype(bf16)
    prefix = jnp.dot(oh, before, preferred_element_type=f32)
    tot = prefix + jnp.concatenate([base_ref[...]] * (tm // 128), axis=1)
    r1 = jnp.sum(oh1 * tot, axis=0, keepdims=True)
    r2 = jnp.sum(oh2 * tot, axis=0, keepdims=True)
    sub8 = _iota2((8, tm), 0)
    rank_ref[...] = jnp.where(sub8 == 0, r1, jnp.where(sub8 == 1, r2, 0.0)).astype(i32)
    base_ref[...] = base_ref[...] + jnp.dot(oh, jnp.ones((tm, 128), bf16), preferred_element_type=f32)
    cnt_ref[...] = base_ref[...]


def _stage_rank(ids, *, tm):
    n = ids.shape[1]
    return pl.pallas_call(
        functools.partial(_rank_body, tm=tm),
        grid=(n // tm,),
        in_specs=[pl.BlockSpec((8, tm), lambda i: (0, i))],
        out_specs=[pl.BlockSpec((8, tm), lambda i: (0, i)),
                   pl.BlockSpec((N_EXPERTS, 128), lambda i: (0, 0))],
        out_shape=[jax.ShapeDtypeStruct((8, n), i32), jax.ShapeDtypeStruct((N_EXPERTS, 128), f32)],
        scratch_shapes=[pltpu.VMEM((N_EXPERTS, 128), f32)],
        compiler_params=_cparams(("arbitrary",)),
        name="moe_rank",
    )(ids)


def _dest_body(cnt_ref, ids_ref, rank_ref, dest_ref, meta_ref, blk_ref, *, tm, nb_pad):
    cnt = cnt_ref[...]
    padded = jnp.floor((cnt + (MOE_BLOCK - 1)) * (1.0 / MOE_BLOCK)) * MOE_BLOCK
    ei = _iota2((N_EXPERTS, N_EXPERTS), 0)
    ej = _iota2((N_EXPERTS, N_EXPERTS), 1)
    pends = jnp.dot((ei >= ej).astype(f32), padded, precision=HIGHEST, preferred_element_type=f32)
    pstart = pends - padded
    sub = _iota2((N_EXPERTS, tm), 0)
    pst = jnp.concatenate([pstart] * (tm // 128), axis=1)
    d1 = jnp.sum(jnp.where(sub == ids_ref[0:1, :], pst, 0.0), axis=0, keepdims=True)
    d2 = jnp.sum(jnp.where(sub == ids_ref[1:2, :], pst, 0.0), axis=0, keepdims=True)
    sub8 = _iota2((8, tm), 0)
    dest_ref[...] = jnp.where(sub8 == 0, d1, jnp.where(sub8 == 1, d2, 0.0)).astype(i32) + rank_ref[...]

    @pl.when(pl.program_id(0) == 0)
    def _():
        s64 = _iota2((N_EXPERTS, 128), 0)
        l64 = _iota2((N_EXPERTS, 128), 1)
        diag = s64 == l64
        fill_off = jnp.sum(jnp.where(diag, pstart + cnt, 0.0), axis=0, keepdims=True)
        fill_n = jnp.sum(jnp.where(diag, padded - cnt, 0.0), axis=0, keepdims=True)
        nused = pends[N_EXPERTS - 1:N_EXPERTS, :] * (1.0 / MOE_BLOCK)
        m8 = _iota2((8, 128), 0)
        meta_ref[...] = jnp.where(m8 == 0, fill_off, jnp.where(m8 == 1, fill_n, jnp.where(m8 == 2, nused, 0.0))).astype(i32)
        bstart = (_iota2((N_EXPERTS, nb_pad), 1) * MOE_BLOCK).astype(f32)
        pe = jnp.concatenate([pends] * (nb_pad // 128), axis=1)
        be = jnp.sum((pe <= bstart).astype(f32), axis=0, keepdims=True)
        be = jnp.minimum(be, float(N_EXPERTS - 1))
        blk_ref[...] = jnp.broadcast_to(be, (8, nb_pad)).astype(i32)


def _stage_dest(cnt, ids, rank, *, tm, nb_pad):
    n = ids.shape[1]
    tok = lambda: pl.BlockSpec((8, tm), lambda i: (0, i))
    return pl.pallas_call(
        functools.partial(_dest_body, tm=tm, nb_pad=nb_pad),
        grid=(n // tm,),
        in_specs=[pl.BlockSpec((N_EXPERTS, 128), lambda i: (0, 0)), tok(), tok()],
        out_specs=[tok(), pl.BlockSpec((8, 128), lambda i: (0, 0)), pl.BlockSpec((8, nb_pad), lambda i: (0, 0))],
        out_shape=[jax.ShapeDtypeStruct((8, n), i32), jax.ShapeDtypeStruct((8, 128), i32),
                   jax.ShapeDtypeStruct((8, nb_pad), i32)],
        compiler_params=_cparams(("arbitrary",)),
        name="moe_dest",
    )(cnt, ids, rank)


def _dispatch_body(fill_off_ref, fill_n_ref, nused_ref, dest_ref, h_ref, xs_ref, h3_ref, zero_ref, sem, zsem, *, tm):
    h3_ref[...] = h_ref[...].reshape(tm, ROW_TILE, 128)

    def row_copy(t, d):
        return pltpu.make_async_copy(h3_ref.at[t], xs_ref.at[d], sem)

    def issue(t, carry):
        row_copy(t, dest_ref[0, 0, t]).start()
        row_copy(t, dest_ref[0, 1, t]).start()
        return carry

    lax.fori_loop(0, tm, issue, 0, unroll=8)

    @pl.when(pl.program_id(0) == 0)
    def _():
        zero_ref[...] = jnp.zeros(zero_ref.shape, f32)

        def fill(start):
            def body(e, carry):
                off = fill_off_ref[e]
                npad = fill_n_ref[e]
                bit = MOE_BLOCK // 2
                while bit:
                    @pl.when((npad & bit) != 0)
                    def _(off=off, bit=bit):
                        cp = pltpu.make_async_copy(zero_ref.at[pl.ds(0, bit)], xs_ref.at[pl.ds(off, bit)], zsem)
                        cp.start() if start else cp.wait()
                    off = off + (npad & bit)
                    bit //= 2
                return carry
            return body

        lax.fori_loop(0, N_EXPERTS, fill(True), 0)
        lax.fori_loop(0, N_EXPERTS, fill(False), 0)

        def tail_copy(b):
            return pltpu.make_async_copy(zero_ref, xs_ref.at[pl.ds(b * MOE_BLOCK, MOE_BLOCK)], zsem)

        nblocks = xs_ref.shape[0] // MOE_BLOCK
        lax.fori_loop(nused_ref[0], nblocks, lambda b, c: (tail_copy(b).start(), c)[1], 0)
        lax.fori_loop(nused_ref[0], nblocks, lambda b, c: (tail_copy(0).wait(), c)[1], 0)

    for _ in range(2):
        pltpu.make_async_copy(h3_ref, xs_ref.at[pl.ds(0, tm)], sem).wait()


def _stage_dispatch(fill_off, fill_n, nused, dest3, h2, p_rows, *, tm):
    n = h2.shape[0]
    return pl.pallas_call(
        functools.partial(_dispatch_body, tm=tm),
        grid_spec=pltpu.PrefetchScalarGridSpec(
            num_scalar_prefetch=3,
            grid=(n // tm,),
            in_specs=[pl.BlockSpec((1, 2, tm), lambda i, fo, fn, nu: (i, 0, 0), memory_space=pltpu.SMEM),
                      pl.BlockSpec((tm, D_MODEL), lambda i, fo, fn, nu: (i, 0))],
            out_specs=pl.BlockSpec(memory_space=pl.ANY),
            scratch_shapes=[pltpu.VMEM((tm, ROW_TILE, 128), f32), pltpu.VMEM((MOE_BLOCK, ROW_TILE, 128), f32),
                            pltpu.SemaphoreType.DMA, pltpu.SemaphoreType.DMA],
        ),
        out_shape=jax.ShapeDtypeStruct((p_rows, ROW_TILE, 128), f32),
        compiler_params=_cparams(("arbitrary",)),
        name="moe_dispatch",
    )(fill_off, fill_n, nused, dest3, h2)


def _experts_body(blk_ref, nused_ref, xs_ref, wg_ref, wu_ref, wd_ref, ys_ref, wgu16_ref, wd16_ref):
    i = pl.program_id(0)
    used = i < nused_ref[0]

    @pl.when(used & ((i == 0) | (blk_ref[i] != blk_ref[jnp.maximum(i - 1, 0)])))
    def _():
        wgu16_ref[:, 0:D_EXPERT] = wg_ref[0].astype(bf16)
        wgu16_ref[:, D_EXPERT:2 * D_EXPERT] = wu_ref[0].astype(bf16)
        wd16_ref[...] = wd_ref[0].astype(bf16)

    @pl.when(used)
    def _():
        xb = xs_ref[...].reshape(MOE_BLOCK, D_MODEL).astype(bf16)
        gu = jnp.dot(xb, wgu16_ref[...], preferred_element_type=f32)
        gate = gu[:, :D_EXPERT]
        hid = (gate * _sigmoid(gate) * gu[:, D_EXPERT:]).astype(bf16)
        y = jnp.dot(hid, wd16_ref[...], preferred_element_type=f32)
        ys_ref[...] = y.reshape(MOE_BLOCK, ROW_TILE, 128)

    @pl.when(jnp.logical_not(used))
    def _():
        ys_ref[...] = jnp.zeros(ys_ref.shape, f32)


def _stage_experts(blk_e, nused, xs, w_gate, w_up, w_down):
    p_rows = xs.shape[0]
    nb = p_rows // MOE_BLOCK

    def last_used(i, nu):
        return jnp.maximum(jnp.minimum(i, nu[0] - 1), 0)

    def x_map(i, blk, nu):
        return (last_used(i, nu), 0, 0)

    def row_map(i, blk, nu):
        return (i, 0, 0)

    def w_map(i, blk, nu):
        return (blk[last_used(i, nu)], 0, 0)

    return pl.pallas_call(
        _experts_body,
        grid_spec=pltpu.PrefetchScalarGridSpec(
            num_scalar_prefetch=2,
            grid=(nb,),
            in_specs=[pl.BlockSpec((MOE_BLOCK, ROW_TILE, 128), x_map),
                      pl.BlockSpec((1, D_MODEL, D_EXPERT), w_map),
                      pl.BlockSpec((1, D_MODEL, D_EXPERT), w_map),
                      pl.BlockSpec((1, D_EXPERT, D_MODEL), w_map)],
            out_specs=pl.BlockSpec((MOE_BLOCK, ROW_TILE, 128), row_map),
            scratch_shapes=[pltpu.VMEM((D_MODEL, 2 * D_EXPERT), bf16), pltpu.VMEM((D_EXPERT, D_MODEL), bf16)],
        ),
        out_shape=jax.ShapeDtypeStruct((p_rows, ROW_TILE, 128), f32),
        compiler_params=_cparams(("arbitrary",)),
        name="moe_experts",
    )(blk_e, nused, xs, w_gate, w_up, w_down)


def _combine_body(dcur_ref, dnext_ref, h_ref, wts_ref, g2_ref, b2_ref, ys_ref, o_ref, ybuf_ref, sem, *, tm, nsteps):
    i = pl.program_id(0)
    slot = lax.rem(i, 2)

    def issue_tile(d_ref, s):
        def body(t, carry):
            for k in range(2):
                pltpu.make_async_copy(ys_ref.at[d_ref[0, k, t]], ybuf_ref.at[s, k, t], sem.at[s]).start()
            return carry

        lax.fori_loop(0, tm, body, 0, unroll=8)

    @pl.when(i == 0)
    def _():
        issue_tile(dcur_ref, 0)

    @pl.when(i + 1 < nsteps)
    def _():
        issue_tile(dnext_ref, 1 - slot)

    for k in range(2):
        pltpu.make_async_copy(ys_ref.at[pl.ds(0, tm)], ybuf_ref.at[slot, k], sem.at[slot]).wait()

    pieces = []
    for c in range(tm // 128):
        ls = slice(c * 128, (c + 1) * 128)
        w1c = jnp.broadcast_to(wts_ref[0:1, ls], (128, 128)).T
        w2c = jnp.broadcast_to(wts_ref[1:2, ls], (128, 128)).T
        w1f = jnp.concatenate([w1c] * (D_MODEL // 128), axis=1)
        w2f = jnp.concatenate([w2c] * (D_MODEL // 128), axis=1)
        y1 = ybuf_ref[slot, 0, ls].reshape(128, D_MODEL)
        y2 = ybuf_ref[slot, 1, ls].reshape(128, D_MODEL)
        pieces.append(w1f * y1 + w2f * y2)
    ffn = jnp.concatenate(pieces, axis=0)
    hp = DEEPNORM_ALPHA * h_ref[...] + ffn
    mu = jnp.mean(hp, axis=-1, keepdims=True)
    hc = hp - mu
    var = jnp.mean(hc * hc, axis=-1, keepdims=True)
    o_ref[...] = hc * lax.rsqrt(var + LN_EPS) * g2_ref[...] + b2_ref[...]


def _stage_combine(dest3, h2, wts, g2, b2, ys, *, tm):
    n = h2.shape[0]
    nsteps = n // tm
    return pl.pallas_call(
        functools.partial(_combine_body, tm=tm, nsteps=nsteps),
        grid=(nsteps,),
        in_specs=[pl.BlockSpec((1, 2, tm), lambda i: (i, 0, 0), memory_space=pltpu.SMEM),
                  pl.BlockSpec((1, 2, tm), lambda i: (jnp.minimum(i + 1, nsteps - 1), 0, 0), memory_space=pltpu.SMEM),
                  pl.BlockSpec((tm, D_MODEL), lambda i: (i, 0)),
                  pl.BlockSpec((8, tm), lambda i: (0, i)),
                  pl.BlockSpec((1, D_MODEL), lambda i: (0, 0)),
                  pl.BlockSpec((1, D_MODEL), lambda i: (0, 0)),
                  pl.BlockSpec(memory_space=pl.ANY)],
        out_specs=pl.BlockSpec((tm, D_MODEL), lambda i: (i, 0)),
        out_shape=jax.ShapeDtypeStruct((n, D_MODEL), f32),
        scratch_shapes=[pltpu.VMEM((2, 2, tm, ROW_TILE, 128), f32), pltpu.SemaphoreType.DMA((2,))],
        compiler_params=_cparams(("arbitrary",)),
        name="moe_combine",
    )(dest3, dest3, h2, wts, g2, b2, ys)


def _layer(h, w_in, conv_w, a_log, dt_bias, dn_norm_w, sgu_ln_g, sgu_ln_b, w_spatial, b_spatial, w_out,
           ln1_g, ln1_b, w_rg, b_rg, w_re, b_re, w_gate, w_up, w_down, ln2_g, ln2_b,
           *, tm_in, dn_chunks, tm_mix, tm_rank, tm_disp, tm_comb):
    B, T, _ = h.shape
    n = B * T
    qkvz = 4 * DN_WIDTH
    ba0 = qkvz
    uv0 = qkvz + 2 * DN_HEADS
    w_cols = jnp.concatenate(
        [w_in[:, :qkvz], w_in[:, uv0:], w_in[:, ba0:uv0], jnp.zeros((D_MODEL, 128 - 2 * DN_HEADS), f32)], axis=1).astype(bf16)
    wbat = w_in[:, ba0:uv0].T.astype(bf16)
    pcol = jnp.zeros((8, 128), f32).at[0, DN_HEADS:2 * DN_HEADS].set(a_log).at[1, DN_HEADS:2 * DN_HEADS].set(dt_bias)
    prow = jnp.zeros((2, 8, 128), f32)
    prow = prow.at[0, DN_HEADS:2 * DN_HEADS, :].set(jnp.broadcast_to(a_log[:, None], (DN_HEADS, 128)))
    prow = prow.at[1, DN_HEADS:2 * DN_HEADS, :].set(jnp.broadcast_to(dt_bias[:, None], (DN_HEADS, 128)))

    q, k, v, z, u, vln, gcol, grow = _stage_inproj(
        h, w_cols, wbat, conv_w, pcol, prow, sgu_ln_g[None, :], sgu_ln_b[None, :], tm=tm_in)
    ydn = _stage_deltanet(q, k, v, z, gcol, grow, dn_norm_w[None, :], nch=dn_chunks)

    bsp = jnp.broadcast_to(b_spatial.T[:, :, None], (SGU_CHUNK, SGU_GROUPS, SGU_CHUNK)).reshape(SGU_CHUNK, SGU_WIDTH)
    wrt = jnp.zeros((128, D_MODEL), f32).at[0:MOE_GROUPS].set(w_rg.T).at[MOE_GROUPS:MOE_GROUPS + N_EXPERTS].set(w_re.T).astype(bf16)
    brt = jnp.zeros((128,), f32).at[0:MOE_GROUPS].set(b_rg).at[MOE_GROUPS:MOE_GROUPS + N_EXPERTS].set(b_re)
    brt = jnp.broadcast_to(brt[:, None], (128, 128))
    h1, ids, wts = _stage_mixout(ydn, u, vln, h, w_spatial, bsp, w_out.astype(bf16), ln1_g[None, :], ln1_b[None, :],
                                 wrt, brt, tm=tm_mix)

    p_rows = (-(-(n * 2) // MOE_BLOCK)) * MOE_BLOCK + N_EXPERTS * MOE_BLOCK
    nb = p_rows // MOE_BLOCK
    nb_pad = (-(-nb // 128)) * 128
    rank, cnt = _stage_rank(ids, tm=tm_rank)
    dest, meta, blk = _stage_dest(cnt, ids, rank, tm=tm_rank, nb_pad=nb_pad)

    h2 = h1.reshape(n, D_MODEL)
    dest_d = dest[0:2].reshape(2, n // tm_disp, tm_disp).transpose(1, 0, 2)
    xs = _stage_dispatch(meta[0, :N_EXPERTS], meta[1, :N_EXPERTS], meta[2, 0:1], dest_d, h2, p_rows, tm=tm_disp)
    ys = _stage_experts(blk[0, :nb], meta[2, 0:1], xs, w_gate, w_up, w_down)
    dest_c = dest[0:2].reshape(2, n // tm_comb, tm_comb).transpose(1, 0, 2)
    out = _stage_combine(dest_c, h2, wts, ln2_g[None, :], ln2_b[None, :], ys, tm=tm_comb)
    return out.reshape(B, T, D_MODEL)


def kernel(x, w_in, conv_w, a_log, dt_bias, dn_norm_w, sgu_ln_g, sgu_ln_b, w_spatial, b_spatial, w_out, ln1_g, ln1_b, w_router_group, b_router_group, w_router_expert, b_router_expert, w_gate, w_up, w_down, ln2_g, ln2_b):
    h = x
    for l in range(w_in.shape[0]):
        h = _layer(h, w_in[l], conv_w[l], a_log[l], dt_bias[l], dn_norm_w[l], sgu_ln_g[l], sgu_ln_b[l],
                   w_spatial[l], b_spatial[l], w_out[l], ln1_g[l], ln1_b[l],
                   w_router_group[l], b_router_group[l], w_router_expert[l], b_router_expert[l],
                   w_gate[l], w_up[l], w_down[l], ln2_g[l], ln2_b[l],
                   tm_in=512, dn_chunks=4, tm_mix=512, tm_rank=512, tm_disp=256, tm_comb=256)
    return h
```

```python
import functools

import jax
import jax.numpy as jnp
from jax import lax
from jax.experimental import pallas as pl
from jax.experimental.pallas import tpu as pltpu

f32 = jnp.float32
bf16 = jnp.bfloat16
i32 = jnp.int32

D_MODEL = 1024
DN_WIDTH = 512
DN_HEADS = 4
HEAD_DIM = 128
CONV_K = 4
SGU_WIDTH = 512
SGU_GROUPS = 4
SGU_CHUNK = 128
DN_CHUNK = 128
MOE_GROUPS = 8
EXPERTS_PER_GROUP = 8
N_EXPERTS = 64
D_EXPERT = 512
MOE_BLOCK = 256
ROW_TILE = D_MODEL // 128
DEEPNORM_ALPHA = 2.0 ** 0.25
LN_EPS = 1e-5
RMS_EPS = 1e-6
HIGHEST = lax.Precision.HIGHEST
VMEM_LIMIT_BYTES = 56 * 1024 * 1024

NT_DIMS = (((1,), (1,)), ((), ()))


def _cparams(sem, flags=None):
    return pltpu.CompilerParams(dimension_semantics=sem, vmem_limit_bytes=VMEM_LIMIT_BYTES, flags=flags)


def _sigmoid(x):
    return 1.0 / (1.0 + jnp.exp(-x))


def _silu(x):
    h = 0.5 * x
    return h + h * jnp.tanh(h)


def _softplus(x):
    return jnp.maximum(x, 0.0) + jnp.log1p(jnp.exp(-jnp.abs(x)))


def _gelu_tanh(x):
    c = 0.7978845608028654
    return x * (0.5 * (1.0 + jnp.tanh(c * (x + 0.044715 * (x * x * x)))))


def _iota2(shape, axis):
    return lax.broadcasted_iota(i32, shape, axis)


def _inproj_body(x_ref, w_ref, wbat_ref, convw_ref, pcol_ref, prow_ref, lng_ref, lnb_ref,
                 q_ref, k_ref, v_ref, z_ref, u_ref, vln_ref, gcol_ref, grow_ref, ext_ref, *, tm):
    @pl.when(pl.program_id(1) == 0)
    def _():
        ext_ref[0:8, :] = jnp.zeros((8, 3 * DN_WIDTH), f32)

    xb = x_ref[0].astype(bf16)

    for part, out_ref in enumerate((q_ref, k_ref, v_ref)):
        c0 = part * DN_WIDTH
        pre = jnp.dot(xb, w_ref[:, c0:c0 + DN_WIDTH], preferred_element_type=f32)
        ext_ref[8:8 + tm, c0:c0 + DN_WIDTH] = pre
        y = convw_ref[3:4, c0:c0 + DN_WIDTH] * pre
        for j in range(CONV_K - 1):
            y = y + convw_ref[j:j + 1, c0:c0 + DN_WIDTH] * ext_ref[5 + j:5 + j + tm, c0:c0 + DN_WIDTH]
        y = _silu(y)
        if part < 2:
            scale = HEAD_DIM ** -0.5 if part == 0 else 1.0
            for h in range(DN_HEADS):
                yh = y[:, h * HEAD_DIM:(h + 1) * HEAD_DIM]
                ss = jnp.sum(yh * yh, axis=-1, keepdims=True)
                out_ref[0, :, h * HEAD_DIM:(h + 1) * HEAD_DIM] = (yh * lax.rsqrt(ss + RMS_EPS) * scale).astype(bf16)
        else:
            out_ref[0] = y.astype(bf16)
    ext_ref[0:8, :] = ext_ref[tm:tm + 8, :]

    zc = 3 * DN_WIDTH
    pz = jnp.dot(xb, w_ref[:, zc:zc + DN_WIDTH], preferred_element_type=f32)
    z_ref[0] = _silu(pz).astype(bf16)

    uc = zc + DN_WIDTH
    pu = jnp.dot(xb, w_ref[:, uc:uc + SGU_WIDTH], preferred_element_type=f32)
    u_ref[0] = _gelu_tanh(pu).astype(bf16)
    vc = uc + SGU_WIDTH
    pv = _gelu_tanh(jnp.dot(xb, w_ref[:, vc:vc + SGU_WIDTH], preferred_element_type=f32))
    for g in range(SGU_GROUPS):
        sl = slice(g * SGU_CHUNK, (g + 1) * SGU_CHUNK)
        vg = pv[:, sl]
        mu = jnp.mean(vg, axis=-1, keepdims=True)
        vcn = vg - mu
        var = jnp.mean(vcn * vcn, axis=-1, keepdims=True)
        vln_ref[0, :, sl] = (vcn * lax.rsqrt(var + LN_EPS) * lng_ref[:, sl] + lnb_ref[:, sl]).astype(bf16)

    bc = vc + SGU_WIDTH
    pba = jnp.dot(xb, w_ref[:, bc:bc + 128], preferred_element_type=f32)
    lane = _iota2((DN_CHUNK, 128), 1)
    is_g = (lane >= DN_HEADS) & (lane < 2 * DN_HEADS)
    beta = _sigmoid(pba)
    gval = -jnp.exp(pcol_ref[0:1, :]) * _softplus(pba + pcol_ref[1:2, :])
    ci = _iota2((DN_CHUNK, DN_CHUNK), 0)
    cj = _iota2((DN_CHUNK, DN_CHUNK), 1)
    ltri = (ci >= cj).astype(f32)
    utri = (ci <= cj).astype(f32)
    pbat = lax.dot_general(wbat_ref[...], xb, NT_DIMS, preferred_element_type=f32)
    sub = _iota2((8, DN_CHUNK), 0)
    for c in range(tm // DN_CHUNK):
        rs = slice(c * DN_CHUNK, (c + 1) * DN_CHUNK)
        gc = jnp.dot(ltri, jnp.where(is_g, gval[rs], 0.0), precision=HIGHEST, preferred_element_type=f32)
        gcol_ref[0, rs, :] = jnp.where(lane < DN_HEADS, beta[rs], gc)
        gt = -jnp.exp(prow_ref[0]) * _softplus(pbat[:, rs] + prow_ref[1])
        gt = jnp.where(sub >= DN_HEADS, gt, 0.0)
        grow_ref[0, :, rs] = jnp.dot(gt, utri, precision=HIGHEST, preferred_element_type=f32)


def _stage_inproj(x, w_re, wbat, conv_w, pcol, prow, lng, lnb, *, tm):
    B, T, _ = x.shape
    wcols = w_re.shape[1]
    grid = (B, T // tm)
    act = lambda: jax.ShapeDtypeStruct((B, T, DN_WIDTH), bf16)
    act_spec = lambda: pl.BlockSpec((1, tm, DN_WIDTH), lambda b, t: (b, t, 0))
    const2 = lambda shp: pl.BlockSpec(shp, lambda b, t: (0, 0))
    return pl.pallas_call(
        functools.partial(_inproj_body, tm=tm),
        grid=grid,
        in_specs=[
            pl.BlockSpec((1, tm, D_MODEL), lambda b, t: (b, t, 0)),
            const2((D_MODEL, wcols)),
            const2((8, D_MODEL)),
            const2((CONV_K, 3 * DN_WIDTH)),
            const2((8, 128)),
            pl.BlockSpec((2, 8, 128), lambda b, t: (0, 0, 0)),
            const2((1, SGU_WIDTH)),
            const2((1, SGU_WIDTH)),
        ],
        out_specs=[act_spec() for _ in range(6)] + [
            pl.BlockSpec((1, tm, 128), lambda b, t: (b, t, 0)),
            pl.BlockSpec((1, 8, tm), lambda b, t: (b, 0, t)),
        ],
        out_shape=[act() for _ in range(6)] + [
            jax.ShapeDtypeStruct((B, T, 128), f32),
            jax.ShapeDtypeStruct((B, 8, T), f32),
        ],
        scratch_shapes=[pltpu.VMEM((tm + 8, 3 * DN_WIDTH), f32)],
        compiler_params=_cparams(("arbitrary", "arbitrary")),
        name="inproj",
    )(x, w_re, wbat, conv_w, pcol, prow, lng, lnb)


def _mm(a, b):
    return jnp.dot(a.astype(bf16), b.astype(bf16), preferred_element_type=f32)


def _unit_lower_inverse(nmats, ii, jj):
    n = nmats[0].shape[0]
    eye = (ii == jj).astype(f32)
    leaf = jnp.right_shift(ii, 3) == jnp.right_shift(jj, 3)
    dblk = [jnp.where(leaf, m, 0.0) for m in nmats]
    s1 = [_mm(d, d) for d in dblk]
    r1 = [eye - d for d in dblk]
    both = [_mm(s, jnp.concatenate([s, r], axis=1)) for s, r in zip(s1, r1)]
    r2 = [r + bo[:, n:] for r, bo in zip(r1, both)]
    xs = [r + _mm(bo[:, :n], r) for r, bo in zip(r2, both)]
    shift = 3
    while (1 << shift) < n:
        same_parent = jnp.right_shift(ii, shift + 1) == jnp.right_shift(jj, shift + 1)
        level = same_parent & (jnp.right_shift(ii, shift) != jnp.right_shift(jj, shift))
        ys = [_mm(jnp.where(level, m, 0.0), x) for m, x in zip(nmats, xs)]
        xs = [x - _mm(x, y) for x, y in zip(xs, ys)]
        shift += 1
    return xs


def _deltanet_body(q_ref, k_ref, v_ref, z_ref, gcol_ref, grow_ref, nw_ref, y_ref, s_ref, *, nch):
    C = DN_CHUNK
    H = range(DN_HEADS)
    P = [(c, h) for c in range(nch) for h in H]

    @pl.when(pl.program_id(1) == 0)
    def _():
        s_ref[...] = jnp.zeros(s_ref.shape, f32)

    ii = _iota2((C, C), 0)
    jj = _iota2((C, C), 1)
    causal = ii >= jj
    rs = [slice(c * C, (c + 1) * C) for c in range(nch)]
    hs = [slice(h * HEAD_DIM, (h + 1) * HEAD_DIM) for h in H]
    gcol = [gcol_ref[0, rs[c], :] for c in range(nch)]
    qh = {p: q_ref[0, rs[p[0]], hs[p[1]]] for p in P}
    kh = {p: k_ref[0, rs[p[0]], hs[p[1]]] for p in P}
    vh = {p: v_ref[0, rs[p[0]], hs[p[1]]] for p in P}
    gc_b = {(c, h): jnp.broadcast_to(gcol[c][:, DN_HEADS + h:DN_HEADS + h + 1], (C, HEAD_DIM)) for c, h in P}
    beta_b = {(c, h): jnp.broadcast_to(gcol[c][:, h:h + 1], (C, HEAD_DIM)) for c, h in P}
    gc_r = {(c, h): jnp.broadcast_to(grow_ref[0, DN_HEADS + h:DN_HEADS + h + 1, rs[c]], (C, C)) for c, h in P}
    decay = {p: jnp.exp(jnp.where(causal, gc_b[p] - gc_r[p], -1e30)) for p in P}

    kf = {p: kh[p].astype(f32) for p in P}
    kb = {p: kf[p] * beta_b[p] for p in P}
    kk = {p: lax.dot_general(kb[p].astype(bf16), kh[p], NT_DIMS, preferred_element_type=f32) for p in P}
    a_intra = {p: lax.dot_general(qh[p], kh[p], NT_DIMS, preferred_element_type=f32) * decay[p] for p in P}
    nmat = [jnp.where(ii > jj, kk[p] * decay[p], 0.0) for p in P]
    tinv = dict(zip(P, _unit_lower_inverse(nmat, ii, jj)))

    eg = {p: jnp.exp(gc_b[p]) for p in P}
    rhs = {p: jnp.concatenate([vh[p].astype(f32) * beta_b[p], kb[p] * eg[p]], axis=1) for p in P}
    sol = {p: _mm(tinv[p], rhs[p]) for p in P}

    q_dec = {p: qh[p].astype(f32) * eg[p] for p in P}
    g_last = {p: gc_b[p][C - 1:C, :] for p in P}
    kdt = {p: (kf[p] * jnp.exp(g_last[p] - gc_b[p])).T for p in P}

    state = [s_ref[h] for h in H]
    for c in range(nch):
        m1 = [_mm(jnp.concatenate([sol[c, h][:, HEAD_DIM:], q_dec[c, h]], axis=0), state[h]) for h in H]
        v_new = [sol[c, h][:, :HEAD_DIM] - m1[h][:C] for h in H]
        m2 = [_mm(jnp.concatenate([a_intra[c, h], kdt[c, h]], axis=0), v_new[h]) for h in H]
        state = [state[h] * jnp.exp(g_last[c, h]) + m2[h][C:] for h in H]
        for h in H:
            o = m1[h][C:] + m2[h][:C]
            rms = lax.rsqrt(jnp.mean(o * o, axis=-1, keepdims=True) + RMS_EPS)
            y_ref[0, rs[c], hs[h]] = (o * rms * nw_ref[...] * z_ref[0, rs[c], hs[h]].astype(f32)).astype(bf16)
    for h in H:
        s_ref[h] = state[h]


def _stage_deltanet(q, k, v, z, gcol, grow, norm_w, *, nch):
    B, T, _ = q.shape
    tt = nch * DN_CHUNK
    act_spec = lambda: pl.BlockSpec((1, tt, DN_WIDTH), lambda b, t: (b, t, 0))
    return pl.pallas_call(
        functools.partial(_deltanet_body, nch=nch),
        grid=(B, T // tt),
        in_specs=[act_spec(), act_spec(), act_spec(), act_spec(),
                  pl.BlockSpec((1, tt, 128), lambda b, t: (b, t, 0)),
                  pl.BlockSpec((1, 8, tt), lambda b, t: (b, 0, t)),
                  pl.BlockSpec((1, HEAD_DIM), lambda b, t: (0, 0))],
        out_specs=act_spec(),
        out_shape=jax.ShapeDtypeStruct((B, T, DN_WIDTH), bf16),
        scratch_shapes=[pltpu.VMEM((DN_HEADS, HEAD_DIM, HEAD_DIM), f32)],
        compiler_params=_cparams(("arbitrary", "arbitrary")),
        name="deltanet",
    )(q, k, v, z, gcol, grow, norm_w)


def _mixout_body(ydn_ref, u_ref, vln_ref, x_ref, ws_ref, bsp_ref, wout_ref, g1_ref, b1_ref, wrt_ref, brt_ref,
                 h_ref, hrow_ref, ids_ref, wts_ref, ycat_ref, *, tm):
    C = SGU_CHUNK
    ii = _iota2((C, C), 0)
    jj = _iota2((C, C), 1)
    ycat_ref[:, 0:DN_WIDTH] = ydn_ref[0]
    for g in range(SGU_GROUPS):
        gs = slice(g * C, (g + 1) * C)
        wsg = jnp.where(ii >= jj, ws_ref[g], 0.0).astype(bf16)
        for c in range(tm // C):
            rs = slice(c * C, (c + 1) * C)
            mixed = jnp.dot(wsg, vln_ref[0, rs, gs], preferred_element_type=f32) + bsp_ref[:, gs]
            ycat_ref[rs, DN_WIDTH + g * C:DN_WIDTH + (g + 1) * C] = (u_ref[0, rs, gs].astype(f32) * mixed).astype(bf16)

    mix = jnp.dot(ycat_ref[...], wout_ref[...], preferred_element_type=f32)
    hp = DEEPNORM_ALPHA * x_ref[0] + mix
    mu = jnp.mean(hp, axis=-1, keepdims=True)
    hc = hp - mu
    var = jnp.mean(hc * hc, axis=-1, keepdims=True)
    h1 = hc * lax.rsqrt(var + LN_EPS) * g1_ref[...] + b1_ref[...]
    h_ref[0] = h1
    h1b = h1.astype(bf16)
    hrow_ref[...] = h1b.reshape(tm, ROW_TILE, 128)

    logits = lax.dot_general(wrt_ref[...], h1b, NT_DIMS, preferred_element_type=f32)
    logits = logits + jnp.concatenate([brt_ref[...]] * (tm // 128), axis=1)
    sub = _iota2((8, tm), 0)
    subf = sub.astype(f32)
    gl = logits[0:8]
    gmax = jnp.max(gl, axis=0, keepdims=True)
    g_idx = jnp.min(jnp.where(gl == gmax, subf, float(MOE_GROUPS)), axis=0, keepdims=True)
    p_group = 1.0 / jnp.sum(jnp.exp(gl - gmax), axis=0, keepdims=True)
    within = jnp.zeros((8, tm), f32)
    for g in range(MOE_GROUPS):
        within = within + jnp.where(g_idx == float(g), logits[8 + 8 * g:16 + 8 * g], 0.0)
    m1 = jnp.max(within, axis=0, keepdims=True)
    i1 = jnp.min(jnp.where(within == m1, subf, float(EXPERTS_PER_GROUP)), axis=0, keepdims=True)
    rest = jnp.where(subf == i1, -jnp.inf, within)
    m2 = jnp.max(rest, axis=0, keepdims=True)
    i2 = jnp.min(jnp.where(rest == m2, subf, float(EXPERTS_PER_GROUP)), axis=0, keepdims=True)
    e = jnp.exp(m2 - m1)
    w1 = p_group / (1.0 + e)
    w2 = p_group * e / (1.0 + e)
    e1 = g_idx * float(EXPERTS_PER_GROUP) + i1
    e2 = g_idx * float(EXPERTS_PER_GROUP) + i2
    ids_ref[...] = jnp.where(sub == 0, e1, jnp.where(sub == 1, e2, 0.0)).astype(i32)
    wts_ref[...] = jnp.where(sub == 0, w1, jnp.where(sub == 1, w2, 0.0))


def _stage_mixout(ydn, u, vln, x, ws, bsp, wout, g1, b1, wrt, brt, *, tm):
    B, T, _ = x.shape
    nt = T // tm
    act_spec = lambda: pl.BlockSpec((1, tm, DN_WIDTH), lambda b, t: (b, t, 0))
    const2 = lambda shp: pl.BlockSpec(shp, lambda b, t: (0, 0))
    tok_spec = lambda: pl.BlockSpec((8, tm), lambda b, t: (0, b * nt + t))
    return pl.pallas_call(
        functools.partial(_mixout_body, tm=tm),
        grid=(B, nt),
        in_specs=[act_spec(), act_spec(), act_spec(),
                  pl.BlockSpec((1, tm, D_MODEL), lambda b, t: (b, t, 0)),
                  pl.BlockSpec((SGU_GROUPS, SGU_CHUNK, SGU_CHUNK), lambda b, t: (0, 0, 0)),
                  const2((SGU_CHUNK, SGU_WIDTH)),
                  const2((D_MODEL, D_MODEL)),
                  const2((1, D_MODEL)), const2((1, D_MODEL)),
                  const2((128, D_MODEL)), const2((128, 128))],
        out_specs=[pl.BlockSpec((1, tm, D_MODEL), lambda b, t: (b, t, 0)),
                   pl.BlockSpec((tm, ROW_TILE, 128), lambda b, t: (b * nt + t, 0, 0)), tok_spec(), tok_spec()],
        out_shape=[jax.ShapeDtypeStruct((B, T, D_MODEL), f32),
                   jax.ShapeDtypeStruct((B * T, ROW_TILE, 128), bf16),
                   jax.ShapeDtypeStruct((8, B * T), i32),
                   jax.ShapeDtypeStruct((8, B * T), f32)],
        scratch_shapes=[pltpu.VMEM((tm, D_MODEL), bf16)],
        compiler_params=_cparams(("arbitrary", "arbitrary")),
        name="mixout",
    )(ydn, u, vln, x, ws, bsp, wout, g1, b1, wrt, brt)


def _rank_body(ids_ref, rank_ref, cnt_ref, base_ref, *, tm):
    @pl.when(pl.program_id(0) == 0)
    def _():
        base_ref[...] = jnp.zeros(base_ref.shape, f32)

    sub = _iota2((N_EXPERTS, tm), 0)
    oh1 = (sub == ids_ref[0:1, :]).astype(f32)
    oh2 = (sub == ids_ref[1:2, :]).astype(f32)
    oh = (oh1 + oh2).astype(bf16)
    ti = _iota2((tm, tm), 0)
    tj = _iota2((tm, tm), 1)
    before = (ti < tj).astype(bf16)
    prefix = jnp.dot(oh, before, preferred_element_type=f32)
    tot = prefix + jnp.concatenate([base_ref[...]] * (tm // 128), axis=1)
    r1 = jnp.sum(oh1 * tot, axis=0, keepdims=True)
    r2 = jnp.sum(oh2 * tot, axis=0, keepdims=True)
    sub8 = _iota2((8, tm), 0)
    rank_ref[...] = jnp.where(sub8 == 0, r1, jnp.where(sub8 == 1, r2, 0.0)).astype(i32)
    base_ref[...] = base_ref[...] + jnp.dot(oh, jnp.ones((tm, 128), bf16), preferred_element_type=f32)
    cnt_ref[...] = base_ref[...]


def _stage_rank(ids, *, tm):
    n = ids.shape[1]
    return pl.pallas_call(
        functools.partial(_rank_body, tm=tm),
        grid=(n // tm,),
        in_specs=[pl.BlockSpec((8, tm), lambda i: (0, i))],
        out_specs=[pl.BlockSpec((8, tm), lambda i: (0, i)),
                   pl.BlockSpec((N_EXPERTS, 128), lambda i: (0, 0))],
        out_shape=[jax.ShapeDtypeStruct((8, n), i32), jax.ShapeDtypeStruct((N_EXPERTS, 128), f32)],
        scratch_shapes=[pltpu.VMEM((N_EXPERTS, 128), f32)],
        compiler_params=_cparams(("arbitrary",)),
        name="moe_rank",
    )(ids)


def _dest_body(cnt_ref, ids_ref, rank_ref, dest_ref, meta_ref, blk_ref, *, tm, nb_pad):
    cnt = cnt_ref[...]
    padded = jnp.floor((cnt + (MOE_BLOCK - 1)) * (1.0 / MOE_BLOCK)) * MOE_BLOCK
    ei = _iota2((N_EXPERTS, N_EXPERTS), 0)
    ej = _iota2((N_EXPERTS, N_EXPERTS), 1)
    pends = jnp.dot((ei >= ej).astype(f32), padded, precision=HIGHEST, preferred_element_type=f32)
    pstart = pends - padded
    sub = _iota2((N_EXPERTS, tm), 0)
    pst = jnp.concatenate([pstart] * (tm // 128), axis=1)
    d1 = jnp.sum(jnp.where(sub == ids_ref[0:1, :], pst, 0.0), axis=0, keepdims=True)
    d2 = jnp.sum(jnp.where(sub == ids_ref[1:2, :], pst, 0.0), axis=0, keepdims=True)
    sub8 = _iota2((8, tm), 0)
    dest_ref[...] = jnp.where(sub8 == 0, d1, jnp.where(sub8 == 1, d2, 0.0)).astype(i32) + rank_ref[...]

    @pl.when(pl.program_id(0) == 0)
    def _():
        s64 = _iota2((N_EXPERTS, 128), 0)
        l64 = _iota2((N_EXPERTS, 128), 1)
        diag = s64 == l64
        fill_off = jnp.sum(jnp.where(diag, pstart + cnt, 0.0), axis=0, keepdims=True)
        fill_n = jnp.sum(jnp.where(diag, padded - cnt, 0.0), axis=0, keepdims=True)
        nused = pends[N_EXPERTS - 1:N_EXPERTS, :] * (1.0 / MOE_BLOCK)
        m8 = _iota2((8, 128), 0)
        meta_ref[...] = jnp.where(m8 == 0, fill_off, jnp.where(m8 == 1, fill_n, jnp.where(m8 == 2, nused, 0.0))).astype(i32)
        bstart = (_iota2((N_EXPERTS, nb_pad), 1) * MOE_BLOCK).astype(f32)
        pe = jnp.concatenate([pends] * (nb_pad // 128), axis=1)
        be = jnp.sum((pe <= bstart).astype(f32), axis=0, keepdims=True)
        be = jnp.minimum(be, float(N_EXPERTS - 1))
        blk_ref[...] = jnp.broadcast_to(be, (8, nb_pad)).astype(i32)


def _stage_dest(cnt, ids, rank, *, tm, nb_pad):
    n = ids.shape[1]
    tok = lambda: pl.BlockSpec((8, tm), lambda i: (0, i))
    return pl.pallas_call(
        functools.partial(_dest_body, tm=tm, nb_pad=nb_pad),
        grid=(n // tm,),
        in_specs=[pl.BlockSpec((N_EXPERTS, 128), lambda i: (0, 0)), tok(), tok()],
        out_specs=[tok(), pl.BlockSpec((8, 128), lambda i: (0, 0)), pl.BlockSpec((8, nb_pad), lambda i: (0, 0))],
        out_shape=[jax.ShapeDtypeStruct((8, n), i32), jax.ShapeDtypeStruct((8, 128), i32),
                   jax.ShapeDtypeStruct((8, nb_pad), i32)],
        compiler_params=_cparams(("arbitrary",)),
        name="moe_dest",
    )(cnt, ids, rank)


def _dispatch_body(fill_off_ref, fill_n_ref, nused_ref, dest_ref, h3_ref, xs_ref, zero_ref, sem, zsem, *, tm):
    def row_copy(t, d):
        return pltpu.make_async_copy(h3_ref.at[t], xs_ref.at[d], sem)

    def issue(t, carry):
        row_copy(t, dest_ref[0, 0, t]).start(priority=0)
        row_copy(t, dest_ref[0, 1, t]).start(priority=1)
        return carry

    lax.fori_loop(0, tm, issue, 0, unroll=8)

    @pl.when(pl.program_id(0) == 0)
    def _():
        zero_ref[...] = jnp.zeros(zero_ref.shape, bf16)

        def fill(start):
            def body(e, carry):
                off = fill_off_ref[e]
                npad = fill_n_ref[e]
                bit = MOE_BLOCK // 2
                while bit:
                    @pl.when((npad & bit) != 0)
                    def _(off=off, bit=bit):
                        cp = pltpu.make_async_copy(zero_ref.at[pl.ds(0, bit)], xs_ref.at[pl.ds(off, bit)], zsem)
                        cp.start() if start else cp.wait()
                    off = off + (npad & bit)
                    bit //= 2
                return carry
            return body

        lax.fori_loop(0, N_EXPERTS, fill(True), 0)
        lax.fori_loop(0, N_EXPERTS, fill(False), 0)

        def tail_copy(b):
            return pltpu.make_async_copy(zero_ref, xs_ref.at[pl.ds(b * MOE_BLOCK, MOE_BLOCK)], zsem)

        nblocks = xs_ref.shape[0] // MOE_BLOCK
        lax.fori_loop(nused_ref[0], nblocks, lambda b, c: (tail_copy(b).start(), c)[1], 0)
        lax.fori_loop(nused_ref[0], nblocks, lambda b, c: (tail_copy(0).wait(), c)[1], 0)

    for _ in range(2):
        pltpu.make_async_copy(h3_ref, xs_ref.at[pl.ds(0, tm)], sem).wait()


def _stage_dispatch(fill_off, fill_n, nused, dest3, hrow, p_rows, *, tm):
    n = hrow.shape[0]
    return pl.pallas_call(
        functools.partial(_dispatch_body, tm=tm),
        grid_spec=pltpu.PrefetchScalarGridSpec(
            num_scalar_prefetch=3,
            grid=(n // tm,),
            in_specs=[pl.BlockSpec((1, 2, tm), lambda i, fo, fn, nu: (i, 0, 0), memory_space=pltpu.SMEM),
                      pl.BlockSpec((tm, ROW_TILE, 128), lambda i, fo, fn, nu: (i, 0, 0))],
            out_specs=pl.BlockSpec(memory_space=pl.ANY),
            scratch_shapes=[pltpu.VMEM((MOE_BLOCK, ROW_TILE, 128), bf16),
                            pltpu.SemaphoreType.DMA, pltpu.SemaphoreType.DMA],
        ),
        out_shape=jax.ShapeDtypeStruct((p_rows, ROW_TILE, 128), bf16),
        compiler_params=_cparams(("arbitrary",)),
        name="moe_dispatch",
    )(fill_off, fill_n, nused, dest3, hrow)


def _experts_body(blk_ref, nused_ref, xs_ref, wg_ref, wu_ref, wd_ref, ys_ref, wgu16_ref, wd16_ref):
    i = pl.program_id(0)
    used = i < nused_ref[0]

    @pl.when(used & ((i == 0) | (blk_ref[i] != blk_ref[jnp.maximum(i - 1, 0)])))
    def _():
        wgu16_ref[:, 0:D_EXPERT] = wg_ref[0].astype(bf16)
        wgu16_ref[:, D_EXPERT:2 * D_EXPERT] = wu_ref[0].astype(bf16)
        wd16_ref[...] = wd_ref[0].astype(bf16)

    @pl.when(used)
    def _():
        xb = xs_ref[...].reshape(MOE_BLOCK, D_MODEL)
        gu = jnp.dot(xb, wgu16_ref[...], preferred_element_type=f32)
        gate = gu[:, :D_EXPERT]
        hid = (_silu(gate) * gu[:, D_EXPERT:]).astype(bf16)
        y = jnp.dot(hid, wd16_ref[...], preferred_element_type=f32)
        ys_ref[...] = y.astype(bf16).reshape(MOE_BLOCK, ROW_TILE, 128)

    @pl.when(jnp.logical_not(used))
    def _():
        ys_ref[...] = jnp.zeros(ys_ref.shape, bf16)


def _stage_experts(blk_e, nused, xs, w_gate, w_up, w_down):
    p_rows = xs.shape[0]
    nb = p_rows // MOE_BLOCK

    def last_used(i, nu):
        return jnp.maximum(jnp.minimum(i, nu[0] - 1), 0)

    def x_map(i, blk, nu):
        return (last_used(i, nu), 0, 0)

    def row_map(i, blk, nu):
        return (i, 0, 0)

    def w_map(i, blk, nu):
        return (blk[last_used(i, nu)], 0, 0)

    return pl.pallas_call(
        _experts_body,
        grid_spec=pltpu.PrefetchScalarGridSpec(
            num_scalar_prefetch=2,
            grid=(nb,),
            in_specs=[pl.BlockSpec((MOE_BLOCK, ROW_TILE, 128), x_map),
                      pl.BlockSpec((1, D_MODEL, D_EXPERT), w_map),
                      pl.BlockSpec((1, D_MODEL, D_EXPERT), w_map),
                      pl.BlockSpec((1, D_EXPERT, D_MODEL), w_map)],
            out_specs=pl.BlockSpec((MOE_BLOCK, ROW_TILE, 128), row_map),
            scratch_shapes=[pltpu.VMEM((D_MODEL, 2 * D_EXPERT), bf16), pltpu.VMEM((D_EXPERT, D_MODEL), bf16)],
        ),
        out_shape=jax.ShapeDtypeStruct((p_rows, ROW_TILE, 128), bf16),
        compiler_params=_cparams(("arbitrary",)),
        name="moe_experts",
    )(blk_e, nused, xs, w_gate, w_up, w_down)


def _combine_body(dcur_ref, dnext_ref, h_ref, wts_ref, g2_ref, b2_ref, ys_ref, o_ref, ybuf_ref, sem, *, tm, nsteps):
    i = pl.program_id(0)
    slot = lax.rem(i, 2)

    def issue_tile(d_ref, s):
        def body(t, carry):
            for k in range(2):
                pltpu.make_async_copy(ys_ref.at[d_ref[0, k, t]], ybuf_ref.at[s, k, t], sem.at[s]).start(priority=k)
            return carry

        lax.fori_loop(0, tm, body, 0, unroll=8)

    @pl.when(i == 0)
    def _():
        issue_tile(dcur_ref, 0)

    @pl.when(i + 1 < nsteps)
    def _():
        issue_tile(dnext_ref, 1 - slot)

    for k in range(2):
        pltpu.make_async_copy(ys_ref.at[pl.ds(0, tm)], ybuf_ref.at[slot, k], sem.at[slot]).wait()

    pieces = []
    for c in range(tm // 128):
        ls = slice(c * 128, (c + 1) * 128)
        w1c = jnp.broadcast_to(wts_ref[0:1, ls], (128, 128)).T
        w2c = jnp.broadcast_to(wts_ref[1:2, ls], (128, 128)).T
        w1f = jnp.concatenate([w1c] * (D_MODEL // 128), axis=1)
        w2f = jnp.concatenate([w2c] * (D_MODEL // 128), axis=1)
        y1 = ybuf_ref[slot, 0, ls].reshape(128, D_MODEL).astype(f32)
        y2 = ybuf_ref[slot, 1, ls].reshape(128, D_MODEL).astype(f32)
        pieces.append(w1f * y1 + w2f * y2)
    ffn = jnp.concatenate(pieces, axis=0)
    hp = DEEPNORM_ALPHA * h_ref[...] + ffn
    mu = jnp.mean(hp, axis=-1, keepdims=True)
    hc = hp - mu
    var = jnp.mean(hc * hc, axis=-1, keepdims=True)
    o_ref[...] = hc * lax.rsqrt(var + LN_EPS) * g2_ref[...] + b2_ref[...]


def _stage_combine(dest3, h2, wts, g2, b2, ys, *, tm):
    n = h2.shape[0]
    nsteps = n // tm
    return pl.pallas_call(
        functools.partial(_combine_body, tm=tm, nsteps=nsteps),
        grid=(nsteps,),
        in_specs=[pl.BlockSpec((1, 2, tm), lambda i: (i, 0, 0), memory_space=pltpu.SMEM),
                  pl.BlockSpec((1, 2, tm), lambda i: (jnp.minimum(i + 1, nsteps - 1), 0, 0), memory_space=pltpu.SMEM),
                  pl.BlockSpec((tm, D_MODEL), lambda i: (i, 0)),
                  pl.BlockSpec((8, tm), lambda i: (0, i)),
                  pl.BlockSpec((1, D_MODEL), lambda i: (0, 0)),
                  pl.BlockSpec((1, D_MODEL), lambda i: (0, 0)),
                  pl.BlockSpec(memory_space=pl.ANY)],
        out_specs=pl.BlockSpec((tm, D_MODEL), lambda i: (i, 0)),
        out_shape=jax.ShapeDtypeStruct((n, D_MODEL), f32),
        scratch_shapes=[pltpu.VMEM((2, 2, tm, ROW_TILE, 128), bf16), pltpu.SemaphoreType.DMA((2,))],
        compiler_params=_cparams(("arbitrary",)),
        name="moe_combine",
    )(dest3, dest3, h2, wts, g2, b2, ys)


def _layer(h, w_in, conv_w, a_log, dt_bias, dn_norm_w, sgu_ln_g, sgu_ln_b, w_spatial, b_spatial, w_out,
           ln1_g, ln1_b, w_rg, b_rg, w_re, b_re, w_gate, w_up, w_down, ln2_g, ln2_b,
           *, tm_in, dn_chunks, tm_mix, tm_rank, tm_disp, tm_comb):
    B, T, _ = h.shape
    n = B * T
    qkvz = 4 * DN_WIDTH
    ba0 = qkvz
    uv0 = qkvz + 2 * DN_HEADS
    w_cols = jnp.concatenate(
        [w_in[:, :qkvz], w_in[:, uv0:], w_in[:, ba0:uv0], jnp.zeros((D_MODEL, 128 - 2 * DN_HEADS), f32)], axis=1).astype(bf16)
    wbat = w_in[:, ba0:uv0].T.astype(bf16)
    pcol = jnp.zeros((8, 128), f32).at[0, DN_HEADS:2 * DN_HEADS].set(a_log).at[1, DN_HEADS:2 * DN_HEADS].set(dt_bias)
    prow = jnp.zeros((2, 8, 128), f32)
    prow = prow.at[0, DN_HEADS:2 * DN_HEADS, :].set(jnp.broadcast_to(a_log[:, None], (DN_HEADS, 128)))
    prow = prow.at[1, DN_HEADS:2 * DN_HEADS, :].set(jnp.broadcast_to(dt_bias[:, None], (DN_HEADS, 128)))

    q, k, v, z, u, vln, gcol, grow = _stage_inproj(
        h, w_cols, wbat, conv_w, pcol, prow, sgu_ln_g[None, :], sgu_ln_b[None, :], tm=tm_in)
    ydn = _stage_deltanet(q, k, v, z, gcol, grow, dn_norm_w[None, :], nch=dn_chunks)

    bsp = jnp.broadcast_to(b_spatial.T[:, :, None], (SGU_CHUNK, SGU_GROUPS, SGU_CHUNK)).reshape(SGU_CHUNK, SGU_WIDTH)
    wrt = jnp.zeros((128, D_MODEL), f32).at[0:MOE_GROUPS].set(w_rg.T).at[MOE_GROUPS:MOE_GROUPS + N_EXPERTS].set(w_re.T).astype(bf16)
    brt = jnp.zeros((128,), f32).at[0:MOE_GROUPS].set(b_rg).at[MOE_GROUPS:MOE_GROUPS + N_EXPERTS].set(b_re)
    brt = jnp.broadcast_to(brt[:, None], (128, 128))
    h1, hrow, ids, wts = _stage_mixout(ydn, u, vln, h, w_spatial, bsp, w_out.astype(bf16), ln1_g[None, :],
                                       ln1_b[None, :], wrt, brt, tm=tm_mix)

    p_rows = (-(-(n * 2) // MOE_BLOCK)) * MOE_BLOCK + N_EXPERTS * MOE_BLOCK
    nb = p_rows // MOE_BLOCK
    nb_pad = (-(-nb // 128)) * 128
    rank, cnt = _stage_rank(ids, tm=tm_rank)
    dest, meta, blk = _stage_dest(cnt, ids, rank, tm=tm_rank, nb_pad=nb_pad)

    h2 = h1.reshape(n, D_MODEL)
    dest_d = dest[0:2].reshape(2, n // tm_disp, tm_disp).transpose(1, 0, 2)
    xs = _stage_dispatch(meta[0, :N_EXPERTS], meta[1, :N_EXPERTS], meta[2, 0:1], dest_d, hrow, p_rows, tm=tm_disp)
    ys = _stage_experts(blk[0, :nb], meta[2, 0:1], xs, w_gate, w_up, w_down)
    dest_c = dest[0:2].reshape(2, n // tm_comb, tm_comb).transpose(1, 0, 2)
    out = _stage_combine(dest_c, h2, wts, ln2_g[None, :], ln2_b[None, :], ys, tm=tm_comb)
    return out.reshape(B, T, D_MODEL)


def kernel(x, w_in, conv_w, a_log, dt_bias, dn_norm_w, sgu_ln_g, sgu_ln_b, w_spatial, b_spatial, w_out, ln1_g, ln1_b, w_router_group, b_router_group, w_router_expert, b_router_expert, w_gate, w_up, w_down, ln2_g, ln2_b):
    h = x
    for l in range(w_in.shape[0]):
        h = _layer(h, w_in[l], conv_w[l], a_log[l], dt_bias[l], dn_norm_w[l], sgu_ln_g[l], sgu_ln_b[l],
                   w_spatial[l], b_spatial[l], w_out[l], ln1_g[l], ln1_b[l],
                   w_router_group[l], b_router_group[l], w_router_expert[l], b_router_expert[l],
                   w_gate[l], w_up[l], w_down[l], ln2_g[l], ln2_b[l],
                   tm_in=512, dn_chunks=4, tm_mix=512, tm_rank=512, tm_disp=256, tm_comb=256)
    return h
```

```python
import functools

import jax
import jax.numpy as jnp
from jax import lax
from jax.experimental import pallas as pl
from jax.experimental.pallas import tpu as pltpu

f32 = jnp.float32
bf16 = jnp.bfloat16
i32 = jnp.int32

D_MODEL = 1024
DN_WIDTH = 512
DN_HEADS = 4
HEAD_DIM = 128
CONV_K = 4
SGU_WIDTH = 512
SGU_GROUPS = 4
SGU_CHUNK = 128
DN_CHUNK = 128
MOE_GROUPS = 8
EXPERTS_PER_GROUP = 8
N_EXPERTS = 64
D_EXPERT = 512
MOE_BLOCK = 256
ROW_TILE = D_MODEL // 128
DEEPNORM_ALPHA = 2.0 ** 0.25
LN_EPS = 1e-5
RMS_EPS = 1e-6
HIGHEST = lax.Precision.HIGHEST
VMEM_LIMIT_BYTES = 56 * 1024 * 1024

NT_DIMS = (((1,), (1,)), ((), ()))


def _cparams(sem, flags=None):
    return pltpu.CompilerParams(dimension_semantics=sem, vmem_limit_bytes=VMEM_LIMIT_BYTES, flags=flags)


def _sigmoid(x):
    return 1.0 / (1.0 + jnp.exp(-x))


def _silu(x):
    h = 0.5 * x
    return h + h * jnp.tanh(h)


def _softplus(x):
    return jnp.maximum(x, 0.0) + jnp.log1p(jnp.exp(-jnp.abs(x)))


def _gelu_tanh(x):
    c = 0.7978845608028654
    return x * (0.5 * (1.0 + jnp.tanh(c * (x + 0.044715 * (x * x * x)))))


def _iota2(shape, axis):
    return lax.broadcasted_iota(i32, shape, axis)


def _inproj_body(x_ref, w_ref, wbat_ref, convw_ref, pcol_ref, prow_ref, lng_ref, lnb_ref, ones_ref,
                 q_ref, k_ref, v_ref, z_ref, u_ref, vln_ref, gcol_ref, grow_ref, eq_ref, ek_ref, ev_ref, *, tm):
    W = DN_WIDTH
    ext_refs = (eq_ref, ek_ref, ev_ref)

    @pl.when(pl.program_id(1) == 0)
    def _():
        for e_ref in ext_refs:
            e_ref[0:8, :] = jnp.zeros((8, W), f32)

    xb = x_ref[0].astype(bf16)
    for part, e_ref in enumerate(ext_refs):
        e_ref[8:8 + tm, :] = jnp.dot(xb, w_ref[:, part * W:(part + 1) * W], preferred_element_type=f32)
    zc = 3 * W
    uc = zc + W
    vc = uc + SGU_WIDTH
    bc = vc + SGU_WIDTH
    pz = jnp.dot(xb, w_ref[:, zc:zc + W], preferred_element_type=f32)
    pu = jnp.dot(xb, w_ref[:, uc:uc + SGU_WIDTH], preferred_element_type=f32)
    pv = jnp.dot(xb, w_ref[:, vc:vc + SGU_WIDTH], preferred_element_type=f32)
    pba = jnp.dot(xb, w_ref[:, bc:bc + 128], preferred_element_type=f32)
    pbat = lax.dot_general(wbat_ref[...], xb, NT_DIMS, preferred_element_type=f32)

    def group_sums(a):
        return jnp.dot(a.astype(bf16), ones_ref[...], preferred_element_type=f32)

    for part, (e_ref, out_ref) in enumerate(zip(ext_refs, (q_ref, k_ref, v_ref))):
        cs = slice(part * W, (part + 1) * W)
        y = convw_ref[3:4, cs] * e_ref[8:8 + tm, :]
        for j in range(CONV_K - 1):
            y = y + convw_ref[j:j + 1, cs] * e_ref[5 + j:5 + j + tm, :]
        y = _silu(y)
        if part < 2:
            scale = HEAD_DIM ** -0.5 if part == 0 else 1.0
            y = y * (lax.rsqrt(group_sums(y * y) + RMS_EPS) * scale)
        out_ref[0] = y.astype(bf16)
        e_ref[0:8, :] = e_ref[tm:tm + 8, :]

    z_ref[0] = _silu(pz).astype(bf16)

    u_ref[0] = _gelu_tanh(pu).astype(bf16)
    pv = _gelu_tanh(pv)
    for g in range(SGU_GROUPS):
        sl = slice(g * SGU_CHUNK, (g + 1) * SGU_CHUNK)
        vg = pv[:, sl]
        mu = jnp.mean(vg, axis=-1, keepdims=True)
        vcn = vg - mu
        var = jnp.mean(vcn * vcn, axis=-1, keepdims=True)
        vln_ref[0, :, sl] = (vcn * lax.rsqrt(var + LN_EPS) * lng_ref[:, sl] + lnb_ref[:, sl]).astype(bf16)

    lane = _iota2((DN_CHUNK, 128), 1)
    is_g = (lane >= DN_HEADS) & (lane < 2 * DN_HEADS)
    beta = _sigmoid(pba)
    gval = -jnp.exp(pcol_ref[0:1, :]) * _softplus(pba + pcol_ref[1:2, :])
    ci = _iota2((DN_CHUNK, DN_CHUNK), 0)
    cj = _iota2((DN_CHUNK, DN_CHUNK), 1)
    ltri = (ci >= cj).astype(f32)
    utri = (ci <= cj).astype(f32)
    sub = _iota2((8, DN_CHUNK), 0)
    for c in range(tm // DN_CHUNK):
        rs = slice(c * DN_CHUNK, (c + 1) * DN_CHUNK)
        gc = jnp.dot(ltri, jnp.where(is_g, gval[rs], 0.0), precision=HIGHEST, preferred_element_type=f32)
        gcol_ref[0, rs, :] = jnp.where(lane < DN_HEADS, beta[rs], gc)
        gt = -jnp.exp(prow_ref[0]) * _softplus(pbat[:, rs] + prow_ref[1])
        gt = jnp.where(sub >= DN_HEADS, gt, 0.0)
        grow_ref[0, :, rs] = jnp.dot(gt, utri, precision=HIGHEST, preferred_element_type=f32)


def _stage_inproj(x, w_re, wbat, conv_w, pcol, prow, lng, lnb, *, tm):
    B, T, _ = x.shape
    wcols = w_re.shape[1]
    grid = (B, T // tm)
    gi = lax.broadcasted_iota(i32, (DN_WIDTH, DN_WIDTH), 0) // 128
    gj = lax.broadcasted_iota(i32, (DN_WIDTH, DN_WIDTH), 1) // 128
    group_ones = (gi == gj).astype(bf16)
    act = lambda: jax.ShapeDtypeStruct((B, T, DN_WIDTH), bf16)
    act_spec = lambda: pl.BlockSpec((1, tm, DN_WIDTH), lambda b, t: (b, t, 0))
    const2 = lambda shp: pl.BlockSpec(shp, lambda b, t: (0, 0))
    return pl.pallas_call(
        functools.partial(_inproj_body, tm=tm),
        grid=grid,
        in_specs=[
            pl.BlockSpec((1, tm, D_MODEL), lambda b, t: (b, t, 0)),
            const2((D_MODEL, wcols)),
            const2((8, D_MODEL)),
            const2((CONV_K, 3 * DN_WIDTH)),
            const2((8, 128)),
            pl.BlockSpec((2, 8, 128), lambda b, t: (0, 0, 0)),
            const2((1, SGU_WIDTH)),
            const2((1, SGU_WIDTH)),
            const2((DN_WIDTH, DN_WIDTH)),
        ],
        out_specs=[act_spec() for _ in range(6)] + [
            pl.BlockSpec((1, tm, 128), lambda b, t: (b, t, 0)),
            pl.BlockSpec((1, 8, tm), lambda b, t: (b, 0, t)),
        ],
        out_shape=[act() for _ in range(6)] + [
            jax.ShapeDtypeStruct((B, T, 128), f32),
            jax.ShapeDtypeStruct((B, 8, T), f32),
        ],
        scratch_shapes=[pltpu.VMEM((tm + 8, DN_WIDTH), f32) for _ in range(3)],
        compiler_params=_cparams(("arbitrary", "arbitrary")),
        name="inproj",
    )(x, w_re, wbat, conv_w, pcol, prow, lng, lnb, group_ones)


def _mm(a, b):
    return jnp.dot(a.astype(bf16), b.astype(bf16), preferred_element_type=f32)


def _unit_lower_inverse(nmats, ii, jj):
    n = nmats[0].shape[0]
    eye = (ii == jj).astype(f32)
    leaf = jnp.right_shift(ii, 3) == jnp.right_shift(jj, 3)
    dblk = [jnp.where(leaf, m, 0.0) for m in nmats]
    s1 = [_mm(d, d) for d in dblk]
    r1 = [eye - d for d in dblk]
    both = [_mm(s, jnp.concatenate([s, r], axis=1)) for s, r in zip(s1, r1)]
    r2 = [r + bo[:, n:] for r, bo in zip(r1, both)]
    xs = [r + _mm(bo[:, :n], r) for r, bo in zip(r2, both)]
    shift = 3
    while (1 << shift) < n:
        same_parent = jnp.right_shift(ii, shift + 1) == jnp.right_shift(jj, shift + 1)
        level = same_parent & (jnp.right_shift(ii, shift) != jnp.right_shift(jj, shift))
        ys = [_mm(jnp.where(level, m, 0.0), x) for m, x in zip(nmats, xs)]
        xs = [x - _mm(x, y) for x, y in zip(xs, ys)]
        shift += 1
    return xs


def _deltanet_body(q_ref, k_ref, v_ref, z_ref, gcol_ref, grow_ref, nw_ref, y_ref, s_ref, *, nch):
    C = DN_CHUNK
    H = range(DN_HEADS)
    P = [(c, h) for c in range(nch) for h in H]

    @pl.when(pl.program_id(1) == 0)
    def _():
        s_ref[...] = jnp.zeros(s_ref.shape, f32)

    ii = _iota2((C, C), 0)
    jj = _iota2((C, C), 1)
    causal = ii >= jj
    rs = [slice(c * C, (c + 1) * C) for c in range(nch)]
    hs = [slice(h * HEAD_DIM, (h + 1) * HEAD_DIM) for h in H]
    gcol = [gcol_ref[0, rs[c], :] for c in range(nch)]
    qh = {p: q_ref[0, rs[p[0]], hs[p[1]]] for p in P}
    kh = {p: k_ref[0, rs[p[0]], hs[p[1]]] for p in P}
    vh = {p: v_ref[0, rs[p[0]], hs[p[1]]] for p in P}
    gc_b = {(c, h): jnp.broadcast_to(gcol[c][:, DN_HEADS + h:DN_HEADS + h + 1], (C, HEAD_DIM)) for c, h in P}
    beta_b = {(c, h): jnp.broadcast_to(gcol[c][:, h:h + 1], (C, HEAD_DIM)) for c, h in P}
    gc_r = {(c, h): jnp.broadcast_to(grow_ref[0, DN_HEADS + h:DN_HEADS + h + 1, rs[c]], (C, C)) for c, h in P}
    decay = {p: jnp.exp(jnp.where(causal, gc_b[p] - gc_r[p], -1e30)) for p in P}

    kf = {p: kh[p].astype(f32) for p in P}
    kb = {p: kf[p] * beta_b[p] for p in P}
    kk = {p: lax.dot_general(kb[p].astype(bf16), kh[p], NT_DIMS, preferred_element_type=f32) for p in P}
    a_intra = {p: lax.dot_general(qh[p], kh[p], NT_DIMS, preferred_element_type=f32) * decay[p] for p in P}
    nmat = [jnp.where(ii > jj, kk[p] * decay[p], 0.0) for p in P]
    tinv = dict(zip(P, _unit_lower_inverse(nmat, ii, jj)))

    eg = {p: jnp.exp(gc_b[p]) for p in P}
    rhs = {p: jnp.concatenate([vh[p].astype(f32) * beta_b[p], kb[p] * eg[p]], axis=1) for p in P}
    sol = {p: _mm(tinv[p], rhs[p]) for p in P}

    q_dec = {p: qh[p].astype(f32) * eg[p] for p in P}
    g_last = {p: gc_b[p][C - 1:C, :] for p in P}
    kdt = {p: (kf[p] * jnp.exp(g_last[p] - gc_b[p])).T for p in P}

    state = [s_ref[h] for h in H]
    for c in range(nch):
        m1 = [_mm(jnp.concatenate([sol[c, h][:, HEAD_DIM:], q_dec[c, h]], axis=0), state[h]) for h in H]
        v_new = [sol[c, h][:, :HEAD_DIM] - m1[h][:C] for h in H]
        m2 = [_mm(jnp.concatenate([a_intra[c, h], kdt[c, h]], axis=0), v_new[h]) for h in H]
        state = [state[h] * jnp.exp(g_last[c, h]) + m2[h][C:] for h in H]
        for h in H:
            o = m1[h][C:] + m2[h][:C]
            rms = lax.rsqrt(jnp.mean(o * o, axis=-1, keepdims=True) + RMS_EPS)
            y_ref[0, rs[c], hs[h]] = (o * rms * nw_ref[...] * z_ref[0, rs[c], hs[h]].astype(f32)).astype(bf16)
    for h in H:
        s_ref[h] = state[h]


def _stage_deltanet(q, k, v, z, gcol, grow, norm_w, *, nch):
    B, T, _ = q.shape
    tt = nch * DN_CHUNK
    act_spec = lambda: pl.BlockSpec((1, tt, DN_WIDTH), lambda b, t: (b, t, 0))
    return pl.pallas_call(
        functools.partial(_deltanet_body, nch=nch),
        grid=(B, T // tt),
        in_specs=[act_spec(), act_spec(), act_spec(), act_spec(),
                  pl.BlockSpec((1, tt, 128), lambda b, t: (b, t, 0)),
                  pl.BlockSpec((1, 8, tt), lambda b, t: (b, 0, t)),
                  pl.BlockSpec((1, HEAD_DIM), lambda b, t: (0, 0))],
        out_specs=act_spec(),
        out_shape=jax.ShapeDtypeStruct((B, T, DN_WIDTH), bf16),
        scratch_shapes=[pltpu.VMEM((DN_HEADS, HEAD_DIM, HEAD_DIM), f32)],
        compiler_params=_cparams(("arbitrary", "arbitrary")),
        name="deltanet",
    )(q, k, v, z, gcol, grow, norm_w)


def _mixout_body(ydn_ref, u_ref, vln_ref, x_ref, ws_ref, bsp_ref, wout_ref, g1_ref, b1_ref, wrt_ref, brt_ref,
                 h_ref, hrow_ref, ids_ref, wts_ref, ycat_ref, *, tm):
    C = SGU_CHUNK
    ii = _iota2((C, C), 0)
    jj = _iota2((C, C), 1)
    ycat_ref[:, 0:DN_WIDTH] = ydn_ref[0]
    for g in range(SGU_GROUPS):
        gs = slice(g * C, (g + 1) * C)
        wsg = jnp.where(ii >= jj, ws_ref[g], 0.0).astype(bf16)
        for c in range(tm // C):
            rs = slice(c * C, (c + 1) * C)
            mixed = jnp.dot(wsg, vln_ref[0, rs, gs], preferred_element_type=f32) + bsp_ref[:, gs]
            ycat_ref[rs, DN_WIDTH + g * C:DN_WIDTH + (g + 1) * C] = (u_ref[0, rs, gs].astype(f32) * mixed).astype(bf16)

    mix = jnp.dot(ycat_ref[...], wout_ref[...], preferred_element_type=f32)
    hp = DEEPNORM_ALPHA * x_ref[0] + mix
    mu = jnp.mean(hp, axis=-1, keepdims=True)
    hc = hp - mu
    var = jnp.mean(hc * hc, axis=-1, keepdims=True)
    h1 = hc * lax.rsqrt(var + LN_EPS) * g1_ref[...] + b1_ref[...]
    h_ref[0] = h1
    h1b = h1.astype(bf16)
    hrow_ref[...] = h1b.reshape(tm, ROW_TILE, 128)

    logits = lax.dot_general(wrt_ref[...], h1b, NT_DIMS, preferred_element_type=f32)
    logits = logits + jnp.concatenate([brt_ref[...]] * (tm // 128), axis=1)
    sub = _iota2((8, tm), 0)
    subf = sub.astype(f32)
    gl = logits[0:8]
    gmax = jnp.max(gl, axis=0, keepdims=True)
    g_idx = jnp.min(jnp.where(gl == gmax, subf, float(MOE_GROUPS)), axis=0, keepdims=True)
    p_group = 1.0 / jnp.sum(jnp.exp(gl - gmax), axis=0, keepdims=True)
    within = jnp.zeros((8, tm), f32)
    for g in range(MOE_GROUPS):
        within = within + jnp.where(g_idx == float(g), logits[8 + 8 * g:16 + 8 * g], 0.0)
    m1 = jnp.max(within, axis=0, keepdims=True)
    i1 = jnp.min(jnp.where(within == m1, subf, float(EXPERTS_PER_GROUP)), axis=0, keepdims=True)
    rest = jnp.where(subf == i1, -jnp.inf, within)
    m2 = jnp.max(rest, axis=0, keepdims=True)
    i2 = jnp.min(jnp.where(rest == m2, subf, float(EXPERTS_PER_GROUP)), axis=0, keepdims=True)
    e = jnp.exp(m2 - m1)
    w1 = p_group / (1.0 + e)
    w2 = p_group * e / (1.0 + e)
    e1 = g_idx * float(EXPERTS_PER_GROUP) + i1
    e2 = g_idx * float(EXPERTS_PER_GROUP) + i2
    ids_ref[...] = jnp.where(sub == 0, e1, jnp.where(sub == 1, e2, 0.0)).astype(i32)
    wts_ref[...] = jnp.where(sub == 0, w1, jnp.where(sub == 1, w2, 0.0))


def _stage_mixout(ydn, u, vln, x, ws, bsp, wout, g1, b1, wrt, brt, *, tm):
    B, T, _ = x.shape
    nt = T // tm
    act_spec = lambda: pl.BlockSpec((1, tm, DN_WIDTH), lambda b, t: (b, t, 0))
    const2 = lambda shp: pl.BlockSpec(shp, lambda b, t: (0, 0))
    tok_spec = lambda: pl.BlockSpec((8, tm), lambda b, t: (0, b * nt + t))
    return pl.pallas_call(
        functools.partial(_mixout_body, tm=tm),
        grid=(B, nt),
        in_specs=[act_spec(), act_spec(), act_spec(),
                  pl.BlockSpec((1, tm, D_MODEL), lambda b, t: (b, t, 0)),
                  pl.BlockSpec((SGU_GROUPS, SGU_CHUNK, SGU_CHUNK), lambda b, t: (0, 0, 0)),
                  const2((SGU_CHUNK, SGU_WIDTH)),
                  const2((D_MODEL, D_MODEL)),
                  const2((1, D_MODEL)), const2((1, D_MODEL)),
                  const2((128, D_MODEL)), const2((128, 128))],
        out_specs=[pl.BlockSpec((1, tm, D_MODEL), lambda b, t: (b, t, 0)),
                   pl.BlockSpec((tm, ROW_TILE, 128), lambda b, t: (b * nt + t, 0, 0)), tok_spec(), tok_spec()],
        out_shape=[jax.ShapeDtypeStruct((B, T, D_MODEL), f32),
                   jax.ShapeDtypeStruct((B * T, ROW_TILE, 128), bf16),
                   jax.ShapeDtypeStruct((8, B * T), i32),
                   jax.ShapeDtypeStruct((8, B * T), f32)],
        scratch_shapes=[pltpu.VMEM((tm, D_MODEL), bf16)],
        compiler_params=_cparams(("arbitrary", "arbitrary")),
        name="mixout",
    )(ydn, u, vln, x, ws, bsp, wout, g1, b1, wrt, brt)


def _rank_body(ids_ref, rank_ref, cnt_ref, base_ref, *, tm):
    @pl.when(pl.program_id(0) == 0)
    def _():
        base_ref[...] = jnp.zeros(base_ref.shape, f32)

    sub = _iota2((N_EXPERTS, tm), 0)
    oh1 = (sub == ids_ref[0:1, :]).astype(f32)
    oh2 = (sub == ids_ref[1:2, :]).astype(f32)
    oh = (oh1 + oh2).astype(bf16)
    ti = _iota2((tm, tm), 0)
    tj = _iota2((tm, tm), 1)
    before = (ti < tj).astype(bf16)
    prefix = jnp.dot(oh, before, preferred_element_type=f32)
    tot = prefix + jnp.concatenate([base_ref[...]] * (tm // 128), axis=1)
    r1 = jnp.sum(oh1 * tot, axis=0, keepdims=True)
    r2 = jnp.sum(oh2 * tot, axis=0, keepdims=True)
    sub8 = _iota2((8, tm), 0)
    rank_ref[...] = jnp.where(sub8 == 0, r1, jnp.where(sub8 == 1, r2, 0.0)).astype(i32)
    base_ref[...] = base_ref[...] + jnp.dot(oh, jnp.ones((tm, 128), bf16), preferred_element_type=f32)
    cnt_ref[...] = base_ref[...]


def _stage_rank(ids, *, tm):
    n = ids.shape[1]
    return pl.pallas_call(
        functools.partial(_rank_body, tm=tm),
        grid=(n // tm,),
        in_specs=[pl.BlockSpec((8, tm), lambda i: (0, i))],
        out_specs=[pl.BlockSpec((8, tm), lambda i: (0, i)),
                   pl.BlockSpec((N_EXPERTS, 128), lambda i: (0, 0))],
        out_shape=[jax.ShapeDtypeStruct((8, n), i32), jax.ShapeDtypeStruct((N_EXPERTS, 128), f32)],
        scratch_shapes=[pltpu.VMEM((N_EXPERTS, 128), f32)],
        compiler_params=_cparams(("arbitrary",)),
        name="moe_rank",
    )(ids)


def _dest_body(cnt_ref, ids_ref, rank_ref, dest_ref, meta_ref, blk_ref, *, tm, nb_pad):
    cnt = cnt_ref[...]
    padded = jnp.floor((cnt + (MOE_BLOCK - 1)) * (1.0 / MOE_BLOCK)) * MOE_BLOCK
    ei = _iota2((N_EXPERTS, N_EXPERTS), 0)
    ej = _iota2((N_EXPERTS, N_EXPERTS), 1)
    pends = jnp.dot((ei >= ej).astype(f32), padded, precision=HIGHEST, preferred_element_type=f32)
    pstart = pends - padded
    sub = _iota2((N_EXPERTS, tm), 0)
    pst = jnp.concatenate([pstart] * (tm // 128), axis=1)
    d1 = jnp.sum(jnp.where(sub == ids_ref[0:1, :], pst, 0.0), axis=0, keepdims=True)
    d2 = jnp.sum(jnp.where(sub == ids_ref[1:2, :], pst, 0.0), axis=0, keepdims=True)
    sub8 = _iota2((8, tm), 0)
    dest_ref[...] = jnp.where(sub8 == 0, d1, jnp.where(sub8 == 1, d2, 0.0)).astype(i32) + rank_ref[...]

    @pl.when(pl.program_id(0) == 0)
    def _():
        s64 = _iota2((N_EXPERTS, 128), 0)
        l64 = _iota2((N_EXPERTS, 128), 1)
        diag = s64 == l64
        fill_off = jnp.sum(jnp.where(diag, pstart + cnt, 0.0), axis=0, keepdims=True)
        fill_n = jnp.sum(jnp.where(diag, padded - cnt, 0.0), axis=0, keepdims=True)
        nused = pends[N_EXPERTS - 1:N_EXPERTS, :] * (1.0 / MOE_BLOCK)
        m8 = _iota2((8, 128), 0)
        meta_ref[...] = jnp.where(m8 == 0, fill_off, jnp.where(m8 == 1, fill_n, jnp.where(m8 == 2, nused, 0.0))).astype(i32)
        bstart = (_iota2((N_EXPERTS, nb_pad), 1) * MOE_BLOCK).astype(f32)
        pe = jnp.concatenate([pends] * (nb_pad // 128), axis=1)
        be = jnp.sum((pe <= bstart).astype(f32), axis=0, keepdims=True)
        be = jnp.minimum(be, float(N_EXPERTS - 1))
        blk_ref[...] = jnp.broadcast_to(be, (8, nb_pad)).astype(i32)


def _stage_dest(cnt, ids, rank, *, tm, nb_pad):
    n = ids.shape[1]
    tok = lambda: pl.BlockSpec((8, tm), lambda i: (0, i))
    return pl.pallas_call(
        functools.partial(_dest_body, tm=tm, nb_pad=nb_pad),
        grid=(n // tm,),
        in_specs=[pl.BlockSpec((N_EXPERTS, 128), lambda i: (0, 0)), tok(), tok()],
        out_specs=[tok(), pl.BlockSpec((8, 128), lambda i: (0, 0)), pl.BlockSpec((8, nb_pad), lambda i: (0, 0))],
        out_shape=[jax.ShapeDtypeStruct((8, n), i32), jax.ShapeDtypeStruct((8, 128), i32),
                   jax.ShapeDtypeStruct((8, nb_pad), i32)],
        compiler_params=_cparams(("arbitrary",)),
        name="moe_dest",
    )(cnt, ids, rank)


def _dispatch_body(fill_off_ref, fill_n_ref, nused_ref, dest_ref, h3_ref, xs_ref, zero_ref, sem, zsem, *, tm):
    def row_copy(t, d):
        return pltpu.make_async_copy(h3_ref.at[t], xs_ref.at[d], sem)

    def issue(t, carry):
        row_copy(t, dest_ref[0, 0, t]).start(priority=0)
        row_copy(t, dest_ref[0, 1, t]).start(priority=1)
        return carry

    lax.fori_loop(0, tm, issue, 0, unroll=8)

    @pl.when(pl.program_id(0) == 0)
    def _():
        zero_ref[...] = jnp.zeros(zero_ref.shape, bf16)

        def fill(start):
            def body(e, carry):
                off = fill_off_ref[e]
                npad = fill_n_ref[e]
                bit = MOE_BLOCK // 2
                while bit:
                    @pl.when((npad & bit) != 0)
                    def _(off=off, bit=bit):
                        cp = pltpu.make_async_copy(zero_ref.at[pl.ds(0, bit)], xs_ref.at[pl.ds(off, bit)], zsem)
                        cp.start() if start else cp.wait()
                    off = off + (npad & bit)
                    bit //= 2
                return carry
            return body

        lax.fori_loop(0, N_EXPERTS, fill(True), 0)
        lax.fori_loop(0, N_EXPERTS, fill(False), 0)

        def tail_copy(b):
            return pltpu.make_async_copy(zero_ref, xs_ref.at[pl.ds(b * MOE_BLOCK, MOE_BLOCK)], zsem)

        nblocks = xs_ref.shape[0] // MOE_BLOCK
        lax.fori_loop(nused_ref[0], nblocks, lambda b, c: (tail_copy(b).start(), c)[1], 0)
        lax.fori_loop(nused_ref[0], nblocks, lambda b, c: (tail_copy(0).wait(), c)[1], 0)

    for _ in range(2):
        pltpu.make_async_copy(h3_ref, xs_ref.at[pl.ds(0, tm)], sem).wait()


def _stage_dispatch(fill_off, fill_n, nused, dest3, hrow, p_rows, *, tm):
    n = hrow.shape[0]
    return pl.pallas_call(
        functools.partial(_dispatch_body, tm=tm),
        grid_spec=pltpu.PrefetchScalarGridSpec(
            num_scalar_prefetch=3,
            grid=(n // tm,),
            in_specs=[pl.BlockSpec((1, 2, tm), lambda i, fo, fn, nu: (i, 0, 0), memory_space=pltpu.SMEM),
                      pl.BlockSpec((tm, ROW_TILE, 128), lambda i, fo, fn, nu: (i, 0, 0))],
            out_specs=pl.BlockSpec(memory_space=pl.ANY),
            scratch_shapes=[pltpu.VMEM((MOE_BLOCK, ROW_TILE, 128), bf16),
                            pltpu.SemaphoreType.DMA, pltpu.SemaphoreType.DMA],
        ),
        out_shape=jax.ShapeDtypeStruct((p_rows, ROW_TILE, 128), bf16),
        compiler_params=_cparams(("arbitrary",)),
        name="moe_dispatch",
    )(fill_off, fill_n, nused, dest3, hrow)


def _experts_body(blk_ref, nused_ref, xs_ref, wg_hbm, wu_hbm, wd_hbm, ys_ref,
                  wg32_ref, wu32_ref, wd32_ref, wgu16_ref, wd16_ref, wsem):
    i = pl.program_id(0)
    nused = nused_ref[0]
    used = i < nused
    e = blk_ref[i]

    def weight_copies(ex):
        return (pltpu.make_async_copy(wg_hbm.at[ex], wg32_ref, wsem.at[0]),
                pltpu.make_async_copy(wu_hbm.at[ex], wu32_ref, wsem.at[1]),
                pltpu.make_async_copy(wd_hbm.at[ex], wd32_ref, wsem.at[2]))

    @pl.when((i == 0) & used)
    def _():
        for cp in weight_copies(e):
            cp.start()

    @pl.when(used & ((i == 0) | (e != blk_ref[jnp.maximum(i - 1, 0)])))
    def _():
        for cp in weight_copies(e):
            cp.wait()
        wgu16_ref[:, 0:D_EXPERT] = wg32_ref[...].astype(bf16)
        wgu16_ref[:, D_EXPERT:2 * D_EXPERT] = wu32_ref[...].astype(bf16)
        wd16_ref[...] = wd32_ref[...].astype(bf16)
        nxt = lax.while_loop(lambda j: (j < nused) & (blk_ref[jnp.minimum(j, nused - 1)] == e), lambda j: j + 1, i + 1)

        @pl.when(nxt < nused)
        def _():
            for cp in weight_copies(blk_ref[jnp.minimum(nxt, nused - 1)]):
                cp.start()

    @pl.when(used)
    def _():
        xb = xs_ref[...].reshape(MOE_BLOCK, D_MODEL)
        gu = jnp.dot(xb, wgu16_ref[...], preferred_element_type=f32)
        gate = gu[:, :D_EXPERT]
        hid = (_silu(gate) * gu[:, D_EXPERT:]).astype(bf16)
        y = jnp.dot(hid, wd16_ref[...], preferred_element_type=f32)
        ys_ref[...] = y.astype(bf16).reshape(MOE_BLOCK, ROW_TILE, 128)

    @pl.when(jnp.logical_not(used))
    def _():
        ys_ref[...] = jnp.zeros(ys_ref.shape, bf16)


def _stage_experts(blk_e, nused, xs, w_gate, w_up, w_down):
    p_rows = xs.shape[0]
    nb = p_rows // MOE_BLOCK

    def last_used(i, nu):
        return jnp.maximum(jnp.minimum(i, nu[0] - 1), 0)

    def x_map(i, blk, nu):
        return (last_used(i, nu), 0, 0)

    def row_map(i, blk, nu):
        return (i, 0, 0)

    return pl.pallas_call(
        _experts_body,
        grid_spec=pltpu.PrefetchScalarGridSpec(
            num_scalar_prefetch=2,
            grid=(nb,),
            in_specs=[pl.BlockSpec((MOE_BLOCK, ROW_TILE, 128), x_map),
                      pl.BlockSpec(memory_space=pl.ANY),
                      pl.BlockSpec(memory_space=pl.ANY),
                      pl.BlockSpec(memory_space=pl.ANY)],
            out_specs=pl.BlockSpec((MOE_BLOCK, ROW_TILE, 128), row_map),
            scratch_shapes=[pltpu.VMEM((D_MODEL, D_EXPERT), f32), pltpu.VMEM((D_MODEL, D_EXPERT), f32),
                            pltpu.VMEM((D_EXPERT, D_MODEL), f32),
                            pltpu.VMEM((D_MODEL, 2 * D_EXPERT), bf16), pltpu.VMEM((D_EXPERT, D_MODEL), bf16),
                            pltpu.SemaphoreType.DMA((3,))],
        ),
        out_shape=jax.ShapeDtypeStruct((p_rows, ROW_TILE, 128), bf16),
        compiler_params=_cparams(("arbitrary",)),
        name="moe_experts",
    )(blk_e, nused, xs, w_gate, w_up, w_down)


def _combine_body(dcur_ref, dnext_ref, h_ref, wts_ref, g2_ref, b2_ref, ys_ref, o_ref, ybuf_ref, sem, *, tm, nsteps):
    i = pl.program_id(0)
    slot = lax.rem(i, 2)

    def issue_tile(d_ref, s):
        def body(t, carry):
            for k in range(2):
                pltpu.make_async_copy(ys_ref.at[d_ref[0, k, t]], ybuf_ref.at[s, k, t], sem.at[s]).start(priority=k)
            return carry

        lax.fori_loop(0, tm, body, 0, unroll=8)

    @pl.when(i == 0)
    def _():
        issue_tile(dcur_ref, 0)

    @pl.when(i + 1 < nsteps)
    def _():
        issue_tile(dnext_ref, 1 - slot)

    for k in range(2):
        pltpu.make_async_copy(ys_ref.at[pl.ds(0, tm)], ybuf_ref.at[slot, k], sem.at[slot]).wait()

    pieces = []
    for c in range(tm // 128):
        ls = slice(c * 128, (c + 1) * 128)
        w1c = jnp.broadcast_to(wts_ref[0:1, ls], (128, 128)).T
        w2c = jnp.broadcast_to(wts_ref[1:2, ls], (128, 128)).T
        w1f = jnp.concatenate([w1c] * (D_MODEL // 128), axis=1)
        w2f = jnp.concatenate([w2c] * (D_MODEL // 128), axis=1)
        y1 = ybuf_ref[slot, 0, ls].reshape(128, D_MODEL).astype(f32)
        y2 = ybuf_ref[slot, 1, ls].reshape(128, D_MODEL).astype(f32)
        pieces.append(w1f * y1 + w2f * y2)
    ffn = jnp.concatenate(pieces, axis=0)
    hp = DEEPNORM_ALPHA * h_ref[...] + ffn
    mu = jnp.mean(hp, axis=-1, keepdims=True)
    hc = hp - mu
    var = jnp.mean(hc * hc, axis=-1, keepdims=True)
    o_ref[...] = hc * lax.rsqrt(var + LN_EPS) * g2_ref[...] + b2_ref[...]


def _stage_combine(dest3, h2, wts, g2, b2, ys, *, tm):
    n = h2.shape[0]
    nsteps = n // tm
    return pl.pallas_call(
        functools.partial(_combine_body, tm=tm, nsteps=nsteps),
        grid=(nsteps,),
        in_specs=[pl.BlockSpec((1, 2, tm), lambda i: (i, 0, 0), memory_space=pltpu.SMEM),
                  pl.BlockSpec((1, 2, tm), lambda i: (jnp.minimum(i + 1, nsteps - 1), 0, 0), memory_space=pltpu.SMEM),
                  pl.BlockSpec((tm, D_MODEL), lambda i: (i, 0)),
                  pl.BlockSpec((8, tm), lambda i: (0, i)),
                  pl.BlockSpec((1, D_MODEL), lambda i: (0, 0)),
                  pl.BlockSpec((1, D_MODEL), lambda i: (0, 0)),
                  pl.BlockSpec(memory_space=pl.ANY)],
        out_specs=pl.BlockSpec((tm, D_MODEL), lambda i: (i, 0)),
        out_shape=jax.ShapeDtypeStruct((n, D_MODEL), f32),
        scratch_shapes=[pltpu.VMEM((2, 2, tm, ROW_TILE, 128), bf16), pltpu.SemaphoreType.DMA((2,))],
        compiler_params=_cparams(("arbitrary",)),
        name="moe_combine",
    )(dest3, dest3, h2, wts, g2, b2, ys)


def _layer(h, w_in, conv_w, a_log, dt_bias, dn_norm_w, sgu_ln_g, sgu_ln_b, w_spatial, b_spatial, w_out,
           ln1_g, ln1_b, w_rg, b_rg, w_re, b_re, w_gate, w_up, w_down, ln2_g, ln2_b,
           *, tm_in, dn_chunks, tm_mix, tm_rank, tm_disp, tm_comb):
    B, T, _ = h.shape
    n = B * T
    qkvz = 4 * DN_WIDTH
    ba0 = qkvz
    uv0 = qkvz + 2 * DN_HEADS
    w_cols = jnp.concatenate(
        [w_in[:, :qkvz], w_in[:, uv0:], w_in[:, ba0:uv0], jnp.zeros((D_MODEL, 128 - 2 * DN_HEADS), f32)], axis=1).astype(bf16)
    wbat = w_in[:, ba0:uv0].T.astype(bf16)
    pcol = jnp.zeros((8, 128), f32).at[0, DN_HEADS:2 * DN_HEADS].set(a_log).at[1, DN_HEADS:2 * DN_HEADS].set(dt_bias)
    prow = jnp.zeros((2, 8, 128), f32)
    prow = prow.at[0, DN_HEADS:2 * DN_HEADS, :].set(jnp.broadcast_to(a_log[:, None], (DN_HEADS, 128)))
    prow = prow.at[1, DN_HEADS:2 * DN_HEADS, :].set(jnp.broadcast_to(dt_bias[:, None], (DN_HEADS, 128)))

    q, k, v, z, u, vln, gcol, grow = _stage_inproj(
        h, w_cols, wbat, conv_w, pcol, prow, sgu_ln_g[None, :], sgu_ln_b[None, :], tm=tm_in)
    ydn = _stage_deltanet(q, k, v, z, gcol, grow, dn_norm_w[None, :], nch=dn_chunks)

    bsp = jnp.broadcast_to(b_spatial.T[:, :, None], (SGU_CHUNK, SGU_GROUPS, SGU_CHUNK)).reshape(SGU_CHUNK, SGU_WIDTH)
    wrt = jnp.zeros((128, D_MODEL), f32).at[0:MOE_GROUPS].set(w_rg.T).at[MOE_GROUPS:MOE_GROUPS + N_EXPERTS].set(w_re.T).astype(bf16)
    brt = jnp.zeros((128,), f32).at[0:MOE_GROUPS].set(b_rg).at[MOE_GROUPS:MOE_GROUPS + N_EXPERTS].set(b_re)
    brt = jnp.broadcast_to(brt[:, None], (128, 128))
    h1, hrow, ids, wts = _stage_mixout(ydn, u, vln, h, w_spatial, bsp, w_out.astype(bf16), ln1_g[None, :],
                                       ln1_b[None, :], wrt, brt, tm=tm_mix)

    p_rows = (-(-(n * 2) // MOE_BLOCK)) * MOE_BLOCK + N_EXPERTS * MOE_BLOCK
    nb = p_rows // MOE_BLOCK
    nb_pad = (-(-nb // 128)) * 128
    rank, cnt = _stage_rank(ids, tm=tm_rank)
    dest, meta, blk = _stage_dest(cnt, ids, rank, tm=tm_rank, nb_pad=nb_pad)

    h2 = h1.reshape(n, D_MODEL)
    dest_d = dest[0:2].reshape(2, n // tm_disp, tm_disp).transpose(1, 0, 2)
    xs = _stage_dispatch(meta[0, :N_EXPERTS], meta[1, :N_EXPERTS], meta[2, 0:1], dest_d, hrow, p_rows, tm=tm_disp)
    ys = _stage_experts(blk[0, :nb], meta[2, 0:1], xs, w_gate, w_up, w_down)
    dest_c = dest[0:2].reshape(2, n // tm_comb, tm_comb).transpose(1, 0, 2)
    out = _stage_combine(dest_c, h2, wts, ln2_g[None, :], ln2_b[None, :], ys, tm=tm_comb)
    return out.reshape(B, T, D_MODEL)


def kernel(x, w_in, conv_w, a_log, dt_bias, dn_norm_w, sgu_ln_g, sgu_ln_b, w_spatial, b_spatial, w_out, ln1_g, ln1_b, w_router_group, b_router_group, w_router_expert, b_router_expert, w_gate, w_up, w_down, ln2_g, ln2_b):
    h = x
    for l in range(w_in.shape[0]):
        h = _layer(h, w_in[l], conv_w[l], a_log[l], dt_bias[l], dn_norm_w[l], sgu_ln_g[l], sgu_ln_b[l],
                   w_spatial[l], b_spatial[l], w_out[l], ln1_g[l], ln1_b[l],
                   w_router_group[l], b_router_group[l], w_router_expert[l], b_router_expert[l],
                   w_gate[l], w_up[l], w_down[l], ln2_g[l], ln2_b[l],
                   tm_in=512, dn_chunks=4, tm_mix=512, tm_rank=512, tm_disp=256, tm_comb=256)
    return h
```

```python
import functools

import jax
import jax.numpy as jnp
from jax import lax
from jax.experimental import pallas as pl
from jax.experimental.pallas import tpu as pltpu

f32 = jnp.float32
bf16 = jnp.bfloat16
i32 = jnp.int32

D_MODEL = 1024
DN_WIDTH = 512
DN_HEADS = 4
HEAD_DIM = 128
CONV_K = 4
SGU_WIDTH = 512
SGU_GROUPS = 4
SGU_CHUNK = 128
DN_CHUNK = 128
MOE_GROUPS = 8
EXPERTS_PER_GROUP = 8
N_EXPERTS = 64
D_EXPERT = 512
MOE_BLOCK = 256
ROW_TILE = D_MODEL // 128
DEEPNORM_ALPHA = 2.0 ** 0.25
LN_EPS = 1e-5
RMS_EPS = 1e-6
HIGHEST = lax.Precision.HIGHEST
VMEM_LIMIT_BYTES = 56 * 1024 * 1024

NT_DIMS = (((1,), (1,)), ((), ()))


def _cparams(sem, flags=None):
    return pltpu.CompilerParams(dimension_semantics=sem, vmem_limit_bytes=VMEM_LIMIT_BYTES, flags=flags)


def _sigmoid(x):
    return 1.0 / (1.0 + jnp.exp(-x))


def _silu(x):
    h = 0.5 * x
    return h + h * jnp.tanh(h)


def _softplus(x):
    return jnp.maximum(x, 0.0) + jnp.log1p(jnp.exp(-jnp.abs(x)))


def _gelu_tanh(x):
    c = 0.7978845608028654
    return x * (0.5 * (1.0 + jnp.tanh(c * (x + 0.044715 * (x * x * x)))))


def _iota2(shape, axis):
    return lax.broadcasted_iota(i32, shape, axis)


def _inproj_body(x_ref, w_ref, wbat_ref, convw_ref, pcol_ref, prow_ref, lng_ref, lnb_ref, ones_ref,
                 q_ref, k_ref, v_ref, z_ref, u_ref, vln_ref, gcol_ref, grow_ref, eq_ref, ek_ref, ev_ref, *, tm):
    W = DN_WIDTH
    ext_refs = (eq_ref, ek_ref, ev_ref)

    @pl.when(pl.program_id(1) == 0)
    def _():
        for e_ref in ext_refs:
            e_ref[0:8, :] = jnp.zeros((8, W), f32)

    xb = x_ref[0].astype(bf16)
    for part, e_ref in enumerate(ext_refs):
        e_ref[8:8 + tm, :] = jnp.dot(xb, w_ref[:, part * W:(part + 1) * W], preferred_element_type=f32)
    zc = 3 * W
    uc = zc + W
    vc = uc + SGU_WIDTH
    bc = vc + SGU_WIDTH
    pz = jnp.dot(xb, w_ref[:, zc:zc + W], preferred_element_type=f32)
    pu = jnp.dot(xb, w_ref[:, uc:uc + SGU_WIDTH], preferred_element_type=f32)
    pv = jnp.dot(xb, w_ref[:, vc:vc + SGU_WIDTH], preferred_element_type=f32)
    pba = jnp.dot(xb, w_ref[:, bc:bc + 128], preferred_element_type=f32)
    pbat = lax.dot_general(wbat_ref[...], xb, NT_DIMS, preferred_element_type=f32)

    def group_sums(a):
        return jnp.dot(a.astype(bf16), ones_ref[...], preferred_element_type=f32)

    for part, (e_ref, out_ref) in enumerate(zip(ext_refs, (q_ref, k_ref, v_ref))):
        cs = slice(part * W, (part + 1) * W)
        y = convw_ref[3:4, cs] * e_ref[8:8 + tm, :]
        for j in range(CONV_K - 1):
            y = y + convw_ref[j:j + 1, cs] * e_ref[5 + j:5 + j + tm, :]
        y = _silu(y)
        if part < 2:
            scale = HEAD_DIM ** -0.5 if part == 0 else 1.0
            y = y * (lax.rsqrt(group_sums(y * y) + RMS_EPS) * scale)
        out_ref[0] = y.astype(bf16)
        e_ref[0:8, :] = e_ref[tm:tm + 8, :]

    z_ref[0] = _silu(pz).astype(bf16)

    u_ref[0] = _gelu_tanh(pu).astype(bf16)
    pv = _gelu_tanh(pv)
    for g in range(SGU_GROUPS):
        sl = slice(g * SGU_CHUNK, (g + 1) * SGU_CHUNK)
        vg = pv[:, sl]
        mu = jnp.mean(vg, axis=-1, keepdims=True)
        vcn = vg - mu
        var = jnp.mean(vcn * vcn, axis=-1, keepdims=True)
        vln_ref[0, :, sl] = (vcn * lax.rsqrt(var + LN_EPS) * lng_ref[:, sl] + lnb_ref[:, sl]).astype(bf16)

    lane = _iota2((DN_CHUNK, 128), 1)
    is_g = (lane >= DN_HEADS) & (lane < 2 * DN_HEADS)
    beta = _sigmoid(pba)
    gval = -jnp.exp(pcol_ref[0:1, :]) * _softplus(pba + pcol_ref[1:2, :])
    ci = _iota2((DN_CHUNK, DN_CHUNK), 0)
    cj = _iota2((DN_CHUNK, DN_CHUNK), 1)
    ltri = (ci >= cj).astype(f32)
    utri = (ci <= cj).astype(f32)
    sub = _iota2((8, DN_CHUNK), 0)
    for c in range(tm // DN_CHUNK):
        rs = slice(c * DN_CHUNK, (c + 1) * DN_CHUNK)
        gc = jnp.dot(ltri, jnp.where(is_g, gval[rs], 0.0), precision=HIGHEST, preferred_element_type=f32)
        gcol_ref[0, rs, :] = jnp.where(lane < DN_HEADS, beta[rs], gc)
        gt = -jnp.exp(prow_ref[0]) * _softplus(pbat[:, rs] + prow_ref[1])
        gt = jnp.where(sub >= DN_HEADS, gt, 0.0)
        grow_ref[0, :, rs] = jnp.dot(gt, utri, precision=HIGHEST, preferred_element_type=f32)


def _stage_inproj(x, w_re, wbat, conv_w, pcol, prow, lng, lnb, *, tm):
    B, T, _ = x.shape
    wcols = w_re.shape[1]
    grid = (B, T // tm)
    gi = lax.broadcasted_iota(i32, (DN_WIDTH, DN_WIDTH), 0) // 128
    gj = lax.broadcasted_iota(i32, (DN_WIDTH, DN_WIDTH), 1) // 128
    group_ones = (gi == gj).astype(bf16)
    act = lambda: jax.ShapeDtypeStruct((B, T, DN_WIDTH), bf16)
    act_spec = lambda: pl.BlockSpec((1, tm, DN_WIDTH), lambda b, t: (b, t, 0))
    const2 = lambda shp: pl.BlockSpec(shp, lambda b, t: (0, 0))
    return pl.pallas_call(
        functools.partial(_inproj_body, tm=tm),
        grid=grid,
        in_specs=[
            pl.BlockSpec((1, tm, D_MODEL), lambda b, t: (b, t, 0)),
            const2((D_MODEL, wcols)),
            const2((8, D_MODEL)),
            const2((CONV_K, 3 * DN_WIDTH)),
            const2((8, 128)),
            pl.BlockSpec((2, 8, 128), lambda b, t: (0, 0, 0)),
            const2((1, SGU_WIDTH)),
            const2((1, SGU_WIDTH)),
            const2((DN_WIDTH, DN_WIDTH)),
        ],
        out_specs=[act_spec() for _ in range(6)] + [
            pl.BlockSpec((1, tm, 128), lambda b, t: (b, t, 0)),
            pl.BlockSpec((1, 8, tm), lambda b, t: (b, 0, t)),
        ],
        out_shape=[act() for _ in range(6)] + [
            jax.ShapeDtypeStruct((B, T, 128), f32),
            jax.ShapeDtypeStruct((B, 8, T), f32),
        ],
        scratch_shapes=[pltpu.VMEM((tm + 8, DN_WIDTH), f32) for _ in range(3)],
        compiler_params=_cparams(("arbitrary", "arbitrary")),
        name="inproj",
    )(x, w_re, wbat, conv_w, pcol, prow, lng, lnb, group_ones)


def _mm(a, b):
    return jnp.dot(a.astype(bf16), b.astype(bf16), preferred_element_type=f32)


def _unit_lower_inverse(nmats, ii, jj):
    n = nmats[0].shape[0]
    eye = (ii == jj).astype(f32)
    leaf = jnp.right_shift(ii, 3) == jnp.right_shift(jj, 3)
    dblk = [jnp.where(leaf, m, 0.0) for m in nmats]
    s1 = [_mm(d, d) for d in dblk]
    r1 = [eye - d for d in dblk]
    both = [_mm(s, jnp.concatenate([s, r], axis=1)) for s, r in zip(s1, r1)]
    r2 = [r + bo[:, n:] for r, bo in zip(r1, both)]
    xs = [r + _mm(bo[:, :n], r) for r, bo in zip(r2, both)]
    shift = 3
    while (1 << shift) < n:
        same_parent = jnp.right_shift(ii, shift + 1) == jnp.right_shift(jj, shift + 1)
        level = same_parent & (jnp.right_shift(ii, shift) != jnp.right_shift(jj, shift))
        ys = [_mm(jnp.where(level, m, 0.0), x) for m, x in zip(nmats, xs)]
        xs = [x - _mm(x, y) for x, y in zip(xs, ys)]
        shift += 1
    return xs


def _deltanet_body(q_ref, k_ref, v_ref, z_ref, gcol_ref, grow_ref, nw_ref, y_ref, s_ref, *, nch):
    C = DN_CHUNK
    H = range(DN_HEADS)
    P = [(c, h) for c in range(nch) for h in H]

    @pl.when(pl.program_id(1) == 0)
    def _():
        s_ref[...] = jnp.zeros(s_ref.shape, f32)

    ii = _iota2((C, C), 0)
    jj = _iota2((C, C), 1)
    causal = ii >= jj
    rs = [slice(c * C, (c + 1) * C) for c in range(nch)]
    hs = [slice(h * HEAD_DIM, (h + 1) * HEAD_DIM) for h in H]
    gcol = [gcol_ref[0, rs[c], :] for c in range(nch)]
    qh = {p: q_ref[0, rs[p[0]], hs[p[1]]] for p in P}
    kh = {p: k_ref[0, rs[p[0]], hs[p[1]]] for p in P}
    vh = {p: v_ref[0, rs[p[0]], hs[p[1]]] for p in P}
    gc_b = {(c, h): jnp.broadcast_to(gcol[c][:, DN_HEADS + h:DN_HEADS + h + 1], (C, HEAD_DIM)) for c, h in P}
    beta_b = {(c, h): jnp.broadcast_to(gcol[c][:, h:h + 1], (C, HEAD_DIM)) for c, h in P}
    gc_r = {(c, h): jnp.broadcast_to(grow_ref[0, DN_HEADS + h:DN_HEADS + h + 1, rs[c]], (C, C)) for c, h in P}
    decay = {p: jnp.exp(jnp.where(causal, gc_b[p] - gc_r[p], -1e30)) for p in P}

    kf = {p: kh[p].astype(f32) for p in P}
    kb = {p: kf[p] * beta_b[p] for p in P}
    kk = {p: lax.dot_general(kb[p].astype(bf16), kh[p], NT_DIMS, preferred_element_type=f32) for p in P}
    a_intra = {p: lax.dot_general(qh[p], kh[p], NT_DIMS, preferred_element_type=f32) * decay[p] for p in P}
    nmat = [jnp.where(ii > jj, kk[p] * decay[p], 0.0) for p in P]
    tinv = dict(zip(P, _unit_lower_inverse(nmat, ii, jj)))

    eg = {p: jnp.exp(gc_b[p]) for p in P}
    rhs = {p: jnp.concatenate([vh[p].astype(f32) * beta_b[p], kb[p] * eg[p]], axis=1) for p in P}
    sol = {p: _mm(tinv[p], rhs[p]) for p in P}

    q_dec = {p: qh[p].astype(f32) * eg[p] for p in P}
    g_last = {p: gc_b[p][C - 1:C, :] for p in P}
    kdt = {p: (kf[p] * jnp.exp(g_last[p] - gc_b[p])).T for p in P}

    state = [s_ref[h] for h in H]
    for c in range(nch):
        m1 = [_mm(jnp.concatenate([sol[c, h][:, HEAD_DIM:], q_dec[c, h]], axis=0), state[h]) for h in H]
        v_new = [sol[c, h][:, :HEAD_DIM] - m1[h][:C] for h in H]
        m2 = [_mm(jnp.concatenate([a_intra[c, h], kdt[c, h]], axis=0), v_new[h]) for h in H]
        state = [state[h] * jnp.exp(g_last[c, h]) + m2[h][C:] for h in H]
        for h in H:
            o = m1[h][C:] + m2[h][:C]
            rms = lax.rsqrt(jnp.mean(o * o, axis=-1, keepdims=True) + RMS_EPS)
            y_ref[0, rs[c], hs[h]] = (o * rms * nw_ref[...] * z_ref[0, rs[c], hs[h]].astype(f32)).astype(bf16)
    for h in H:
        s_ref[h] = state[h]


def _stage_deltanet(q, k, v, z, gcol, grow, norm_w, *, nch):
    B, T, _ = q.shape
    tt = nch * DN_CHUNK
    act_spec = lambda: pl.BlockSpec((1, tt, DN_WIDTH), lambda b, t: (b, t, 0))
    return pl.pallas_call(
        functools.partial(_deltanet_body, nch=nch),
        grid=(B, T // tt),
        in_specs=[act_spec(), act_spec(), act_spec(), act_spec(),
                  pl.BlockSpec((1, tt, 128), lambda b, t: (b, t, 0)),
                  pl.BlockSpec((1, 8, tt), lambda b, t: (b, 0, t)),
                  pl.BlockSpec((1, HEAD_DIM), lambda b, t: (0, 0))],
        out_specs=act_spec(),
        out_shape=jax.ShapeDtypeStruct((B, T, DN_WIDTH), bf16),
        scratch_shapes=[pltpu.VMEM((DN_HEADS, HEAD_DIM, HEAD_DIM), f32)],
        compiler_params=_cparams(("arbitrary", "arbitrary")),
        name="deltanet",
    )(q, k, v, z, gcol, grow, norm_w)


def _mixout_body(ydn_ref, u_ref, vln_ref, x_ref, ws_ref, bsp_ref, wout_ref, g1_ref, b1_ref, wrt_ref, brt_ref,
                 h_ref, hrow_ref, ids_ref, wts_ref, ycat_ref, *, tm):
    C = SGU_CHUNK
    ii = _iota2((C, C), 0)
    jj = _iota2((C, C), 1)
    ycat_ref[:, 0:DN_WIDTH] = ydn_ref[0]
    for g in range(SGU_GROUPS):
        gs = slice(g * C, (g + 1) * C)
        wsg = jnp.where(ii >= jj, ws_ref[g], 0.0).astype(bf16)
        for c in range(tm // C):
            rs = slice(c * C, (c + 1) * C)
            mixed = jnp.dot(wsg, vln_ref[0, rs, gs], preferred_element_type=f32) + bsp_ref[:, gs]
            ycat_ref[rs, DN_WIDTH + g * C:DN_WIDTH + (g + 1) * C] = (u_ref[0, rs, gs].astype(f32) * mixed).astype(bf16)

    mix = jnp.dot(ycat_ref[...], wout_ref[...], preferred_element_type=f32)
    hp = DEEPNORM_ALPHA * x_ref[0] + mix
    mu = jnp.mean(hp, axis=-1, keepdims=True)
    hc = hp - mu
    var = jnp.mean(hc * hc, axis=-1, keepdims=True)
    h1 = hc * lax.rsqrt(var + LN_EPS) * g1_ref[...] + b1_ref[...]
    h_ref[0] = h1
    h1b = h1.astype(bf16)
    hrow_ref[...] = h1b.reshape(tm, ROW_TILE, 128)

    logits = lax.dot_general(wrt_ref[...], h1b, NT_DIMS, preferred_element_type=f32)
    logits = logits + jnp.concatenate([brt_ref[...]] * (tm // 128), axis=1)
    sub = _iota2((8, tm), 0)
    subf = sub.astype(f32)
    gl = logits[0:8]
    gmax = jnp.max(gl, axis=0, keepdims=True)
    g_idx = jnp.min(jnp.where(gl == gmax, subf, float(MOE_GROUPS)), axis=0, keepdims=True)
    p_group = 1.0 / jnp.sum(jnp.exp(gl - gmax), axis=0, keepdims=True)
    within = jnp.zeros((8, tm), f32)
    for g in range(MOE_GROUPS):
        within = within + jnp.where(g_idx == float(g), logits[8 + 8 * g:16 + 8 * g], 0.0)
    m1 = jnp.max(within, axis=0, keepdims=True)
    i1 = jnp.min(jnp.where(within == m1, subf, float(EXPERTS_PER_GROUP)), axis=0, keepdims=True)
    rest = jnp.where(subf == i1, -jnp.inf, within)
    m2 = jnp.max(rest, axis=0, keepdims=True)
    i2 = jnp.min(jnp.where(rest == m2, subf, float(EXPERTS_PER_GROUP)), axis=0, keepdims=True)
    e = jnp.exp(m2 - m1)
    w1 = p_group / (1.0 + e)
    w2 = p_group * e / (1.0 + e)
    e1 = g_idx * float(EXPERTS_PER_GROUP) + i1
    e2 = g_idx * float(EXPERTS_PER_GROUP) + i2
    ids_ref[...] = jnp.where(sub == 0, e1, jnp.where(sub == 1, e2, 0.0)).astype(i32)
    wts_ref[...] = jnp.where(sub == 0, w1, jnp.where(sub == 1, w2, 0.0))


def _stage_mixout(ydn, u, vln, x, ws, bsp, wout, g1, b1, wrt, brt, *, tm):
    B, T, _ = x.shape
    nt = T // tm
    act_spec = lambda: pl.BlockSpec((1, tm, DN_WIDTH), lambda b, t: (b, t, 0))
    const2 = lambda shp: pl.BlockSpec(shp, lambda b, t: (0, 0))
    tok_spec = lambda: pl.BlockSpec((8, tm), lambda b, t: (0, b * nt + t))
    return pl.pallas_call(
        functools.partial(_mixout_body, tm=tm),
        grid=(B, nt),
        in_specs=[act_spec(), act_spec(), act_spec(),
                  pl.BlockSpec((1, tm, D_MODEL), lambda b, t: (b, t, 0)),
                  pl.BlockSpec((SGU_GROUPS, SGU_CHUNK, SGU_CHUNK), lambda b, t: (0, 0, 0)),
                  const2((SGU_CHUNK, SGU_WIDTH)),
                  const2((D_MODEL, D_MODEL)),
                  const2((1, D_MODEL)), const2((1, D_MODEL)),
                  const2((128, D_MODEL)), const2((128, 128))],
        out_specs=[pl.BlockSpec((1, tm, D_MODEL), lambda b, t: (b, t, 0)),
                   pl.BlockSpec((tm, ROW_TILE, 128), lambda b, t: (b * nt + t, 0, 0)), tok_spec(), tok_spec()],
        out_shape=[jax.ShapeDtypeStruct((B, T, D_MODEL), f32),
                   jax.ShapeDtypeStruct((B * T, ROW_TILE, 128), bf16),
                   jax.ShapeDtypeStruct((8, B * T), i32),
                   jax.ShapeDtypeStruct((8, B * T), f32)],
        scratch_shapes=[pltpu.VMEM((tm, D_MODEL), bf16)],
        compiler_params=_cparams(("arbitrary", "arbitrary")),
        name="mixout",
    )(ydn, u, vln, x, ws, bsp, wout, g1, b1, wrt, brt)


def _rank_body(ids_ref, rank_ref, cnt_ref, base_ref, *, tm):
    @pl.when(pl.program_id(0) == 0)
    def _():
        base_ref[...] = jnp.zeros(base_ref.shape, f32)

    sub = _iota2((N_EXPERTS, tm), 0)
    oh1 = (sub == ids_ref[0:1, :]).astype(f32)
    oh2 = (sub == ids_ref[1:2, :]).astype(f32)
    oh = (oh1 + oh2).astype(bf16)
    ti = _iota2((tm, tm), 0)
    tj = _iota2((tm, tm), 1)
    before = (ti < tj).astype(bf16)
    prefix = jnp.dot(oh, before, preferred_element_type=f32)
    tot = prefix + jnp.concatenate([base_ref[...]] * (tm // 128), axis=1)
    r1 = jnp.sum(oh1 * tot, axis=0, keepdims=True)
    r2 = jnp.sum(oh2 * tot, axis=0, keepdims=True)
    sub8 = _iota2((8, tm), 0)
    rank_ref[...] = jnp.where(sub8 == 0, r1, jnp.where(sub8 == 1, r2, 0.0)).astype(i32)
    base_ref[...] = base_ref[...] + jnp.dot(oh, jnp.ones((tm, 128), bf16), preferred_element_type=f32)
    cnt_ref[...] = base_ref[...]


def _stage_rank(ids, *, tm):
    n = ids.shape[1]
    return pl.pallas_call(
        functools.partial(_rank_body, tm=tm),
        grid=(n // tm,),
        in_specs=[pl.BlockSpec((8, tm), lambda i: (0, i))],
        out_specs=[pl.BlockSpec((8, tm), lambda i: (0, i)),
                   pl.BlockSpec((N_EXPERTS, 128), lambda i: (0, 0))],
        out_shape=[jax.ShapeDtypeStruct((8, n), i32), jax.ShapeDtypeStruct((N_EXPERTS, 128), f32)],
        scratch_shapes=[pltpu.VMEM((N_EXPERTS, 128), f32)],
        compiler_params=_cparams(("arbitrary",)),
        name="moe_rank",
    )(ids)


def _dest_body(cnt_ref, ids_ref, rank_ref, dest_ref, meta_ref, blk_ref, *, tm, nb_pad):
    cnt = cnt_ref[...]
    padded = jnp.floor((cnt + (MOE_BLOCK - 1)) * (1.0 / MOE_BLOCK)) * MOE_BLOCK
    ei = _iota2((N_EXPERTS, N_EXPERTS), 0)
    ej = _iota2((N_EXPERTS, N_EXPERTS), 1)
    pends = jnp.dot((ei >= ej).astype(f32), padded, precision=HIGHEST, preferred_element_type=f32)
    pstart = pends - padded
    sub = _iota2((N_EXPERTS, tm), 0)
    pst = jnp.concatenate([pstart] * (tm // 128), axis=1)
    d1 = jnp.sum(jnp.where(sub == ids_ref[0:1, :], pst, 0.0), axis=0, keepdims=True)
    d2 = jnp.sum(jnp.where(sub == ids_ref[1:2, :], pst, 0.0), axis=0, keepdims=True)
    sub8 = _iota2((8, tm), 0)
    dest_ref[...] = jnp.where(sub8 == 0, d1, jnp.where(sub8 == 1, d2, 0.0)).astype(i32) + rank_ref[...]

    @pl.when(pl.program_id(0) == 0)
    def _():
        s64 = _iota2((N_EXPERTS, 128), 0)
        l64 = _iota2((N_EXPERTS, 128), 1)
        diag = s64 == l64
        fill_off = jnp.sum(jnp.where(diag, pstart + cnt, 0.0), axis=0, keepdims=True)
        fill_n = jnp.sum(jnp.where(diag, padded - cnt, 0.0), axis=0, keepdims=True)
        nused = pends[N_EXPERTS - 1:N_EXPERTS, :] * (1.0 / MOE_BLOCK)
        m8 = _iota2((8, 128), 0)
        meta_ref[...] = jnp.where(m8 == 0, fill_off, jnp.where(m8 == 1, fill_n, jnp.where(m8 == 2, nused, 0.0))).astype(i32)
        bstart = (_iota2((N_EXPERTS, nb_pad), 1) * MOE_BLOCK).astype(f32)
        pe = jnp.concatenate([pends] * (nb_pad // 128), axis=1)
        be = jnp.sum((pe <= bstart).astype(f32), axis=0, keepdims=True)
        be = jnp.minimum(be, float(N_EXPERTS - 1))
        blk_ref[...] = jnp.broadcast_to(be, (8, nb_pad)).astype(i32)


def _stage_dest(cnt, ids, rank, *, tm, nb_pad):
    n = ids.shape[1]
    tok = lambda: pl.BlockSpec((8, tm), lambda i: (0, i))
    return pl.pallas_call(
        functools.partial(_dest_body, tm=tm, nb_pad=nb_pad),
        grid=(n // tm,),
        in_specs=[pl.BlockSpec((N_EXPERTS, 128), lambda i: (0, 0)), tok(), tok()],
        out_specs=[tok(), pl.BlockSpec((8, 128), lambda i: (0, 0)), pl.BlockSpec((8, nb_pad), lambda i: (0, 0))],
        out_shape=[jax.ShapeDtypeStruct((8, n), i32), jax.ShapeDtypeStruct((8, 128), i32),
                   jax.ShapeDtypeStruct((8, nb_pad), i32)],
        compiler_params=_cparams(("arbitrary",)),
        name="moe_dest",
    )(cnt, ids, rank)


def _dispatch_body(fill_off_ref, fill_n_ref, nused_ref, dest_ref, h3_ref, xs_ref, zero_ref, sem, zsem, *, tm):
    def row_copy(t, d):
        return pltpu.make_async_copy(h3_ref.at[t], xs_ref.at[d], sem)

    def issue(t, carry):
        row_copy(t, dest_ref[0, 0, t]).start(priority=0)
        row_copy(t, dest_ref[0, 1, t]).start(priority=1)
        return carry

    lax.fori_loop(0, tm, issue, 0, unroll=8)

    @pl.when(pl.program_id(0) == 0)
    def _():
        zero_ref[...] = jnp.zeros(zero_ref.shape, bf16)

        def fill(start):
            def body(e, carry):
                off = fill_off_ref[e]
                npad = fill_n_ref[e]
                bit = MOE_BLOCK // 2
                while bit:
                    @pl.when((npad & bit) != 0)
                    def _(off=off, bit=bit):
                        cp = pltpu.make_async_copy(zero_ref.at[pl.ds(0, bit)], xs_ref.at[pl.ds(off, bit)], zsem)
                        cp.start() if start else cp.wait()
                    off = off + (npad & bit)
                    bit //= 2
                return carry
            return body

        lax.fori_loop(0, N_EXPERTS, fill(True), 0)
        lax.fori_loop(0, N_EXPERTS, fill(False), 0)

        def tail_copy(b):
            return pltpu.make_async_copy(zero_ref, xs_ref.at[pl.ds(b * MOE_BLOCK, MOE_BLOCK)], zsem)

        nblocks = xs_ref.shape[0] // MOE_BLOCK
        lax.fori_loop(nused_ref[0], nblocks, lambda b, c: (tail_copy(b).start(), c)[1], 0)
        lax.fori_loop(nused_ref[0], nblocks, lambda b, c: (tail_copy(0).wait(), c)[1], 0)

    for _ in range(2):
        pltpu.make_async_copy(h3_ref, xs_ref.at[pl.ds(0, tm)], sem).wait()


def _stage_dispatch(fill_off, fill_n, nused, dest3, hrow, p_rows, *, tm):
    n = hrow.shape[0]
    return pl.pallas_call(
        functools.partial(_dispatch_body, tm=tm),
        grid_spec=pltpu.PrefetchScalarGridSpec(
            num_scalar_prefetch=3,
            grid=(n // tm,),
            in_specs=[pl.BlockSpec((1, 2, tm), lambda i, fo, fn, nu: (i, 0, 0), memory_space=pltpu.SMEM),
                      pl.BlockSpec((tm, ROW_TILE, 128), lambda i, fo, fn, nu: (i, 0, 0))],
            out_specs=pl.BlockSpec(memory_space=pl.ANY),
            scratch_shapes=[pltpu.VMEM((MOE_BLOCK, ROW_TILE, 128), bf16),
                            pltpu.SemaphoreType.DMA, pltpu.SemaphoreType.DMA],
        ),
        out_shape=jax.ShapeDtypeStruct((p_rows, ROW_TILE, 128), bf16),
        compiler_params=_cparams(("arbitrary",)),
        name="moe_dispatch",
    )(fill_off, fill_n, nused, dest3, hrow)


def _experts_body(blk_ref, nused_ref, xs_ref, wg_hbm, wu_hbm, wd_hbm, ys_ref,
                  wg32_ref, wu32_ref, wd32_ref, wgu16_ref, wd16_ref, wsem):
    i = pl.program_id(0)
    nused = nused_ref[0]
    used = i < nused
    e = blk_ref[i]

    def weight_copies(ex):
        return (pltpu.make_async_copy(wg_hbm.at[ex], wg32_ref, wsem.at[0]),
                pltpu.make_async_copy(wu_hbm.at[ex], wu32_ref, wsem.at[1]),
                pltpu.make_async_copy(wd_hbm.at[ex], wd32_ref, wsem.at[2]))

    @pl.when((i == 0) & used)
    def _():
        for cp in weight_copies(e):
            cp.start()

    @pl.when(used & ((i == 0) | (e != blk_ref[jnp.maximum(i - 1, 0)])))
    def _():
        for cp in weight_copies(e):
            cp.wait()
        wgu16_ref[:, 0:D_EXPERT] = wg32_ref[...].astype(bf16)
        wgu16_ref[:, D_EXPERT:2 * D_EXPERT] = wu32_ref[...].astype(bf16)
        wd16_ref[...] = wd32_ref[...].astype(bf16)
        nxt = lax.while_loop(lambda j: (j < nused) & (blk_ref[jnp.minimum(j, nused - 1)] == e), lambda j: j + 1, i + 1)

        @pl.when(nxt < nused)
        def _():
            for cp in weight_copies(blk_ref[jnp.minimum(nxt, nused - 1)]):
                cp.start()

    @pl.when(used)
    def _():
        half = MOE_BLOCK // 2
        rows = [slice(p * half, (p + 1) * half) for p in range(2)]
        gu = [jnp.dot(xs_ref[r].reshape(half, D_MODEL), wgu16_ref[...], preferred_element_type=f32) for r in rows]
        hid = [(_silu(g[:, :D_EXPERT]) * g[:, D_EXPERT:]).astype(bf16) for g in gu]
        y = [jnp.dot(hd, wd16_ref[...], preferred_element_type=f32) for hd in hid]
        for r, yp in zip(rows, y):
            ys_ref[r] = yp.astype(bf16).reshape(half, ROW_TILE, 128)

    @pl.when(jnp.logical_not(used))
    def _():
        ys_ref[...] = jnp.zeros(ys_ref.shape, bf16)


def _stage_experts(blk_e, nused, xs, w_gate, w_up, w_down):
    p_rows = xs.shape[0]
    nb = p_rows // MOE_BLOCK

    def last_used(i, nu):
        return jnp.maximum(jnp.minimum(i, nu[0] - 1), 0)

    def x_map(i, blk, nu):
        return (last_used(i, nu), 0, 0)

    def row_map(i, blk, nu):
        return (i, 0, 0)

    return pl.pallas_call(
        _experts_body,
        grid_spec=pltpu.PrefetchScalarGridSpec(
            num_scalar_prefetch=2,
            grid=(nb,),
            in_specs=[pl.BlockSpec((MOE_BLOCK, ROW_TILE, 128), x_map),
                      pl.BlockSpec(memory_space=pl.ANY),
                      pl.BlockSpec(memory_space=pl.ANY),
                      pl.BlockSpec(memory_space=pl.ANY)],
            out_specs=pl.BlockSpec((MOE_BLOCK, ROW_TILE, 128), row_map),
            scratch_shapes=[pltpu.VMEM((D_MODEL, D_EXPERT), f32), pltpu.VMEM((D_MODEL, D_EXPERT), f32),
                            pltpu.VMEM((D_EXPERT, D_MODEL), f32),
                            pltpu.VMEM((D_MODEL, 2 * D_EXPERT), bf16), pltpu.VMEM((D_EXPERT, D_MODEL), bf16),
                            pltpu.SemaphoreType.DMA((3,))],
        ),
        out_shape=jax.ShapeDtypeStruct((p_rows, ROW_TILE, 128), bf16),
        compiler_params=_cparams(("arbitrary",)),
        name="moe_experts",
    )(blk_e, nused, xs, w_gate, w_up, w_down)


def _combine_body(dcur_ref, dnext_ref, h_ref, wts_ref, g2_ref, b2_ref, ys_ref, o_ref, ybuf_ref, sem, *, tm, nsteps):
    i = pl.program_id(0)
    slot = lax.rem(i, 2)

    def issue_tile(d_ref, s):
        def body(t, carry):
            for k in range(2):
                pltpu.make_async_copy(ys_ref.at[d_ref[0, k, t]], ybuf_ref.at[s, k, t], sem.at[s]).start(priority=k)
            return carry

        lax.fori_loop(0, tm, body, 0, unroll=8)

    @pl.when(i == 0)
    def _():
        issue_tile(dcur_ref, 0)

    @pl.when(i + 1 < nsteps)
    def _():
        issue_tile(dnext_ref, 1 - slot)

    for k in range(2):
        pltpu.make_async_copy(ys_ref.at[pl.ds(0, tm)], ybuf_ref.at[slot, k], sem.at[slot]).wait()

    pieces = []
    for c in range(tm // 128):
        ls = slice(c * 128, (c + 1) * 128)
        w1c = jnp.broadcast_to(wts_ref[0:1, ls], (128, 128)).T
        w2c = jnp.broadcast_to(wts_ref[1:2, ls], (128, 128)).T
        w1f = jnp.concatenate([w1c] * (D_MODEL // 128), axis=1)
        w2f = jnp.concatenate([w2c] * (D_MODEL // 128), axis=1)
        y1 = ybuf_ref[slot, 0, ls].reshape(128, D_MODEL).astype(f32)
        y2 = ybuf_ref[slot, 1, ls].reshape(128, D_MODEL).astype(f32)
        pieces.append(w1f * y1 + w2f * y2)
    ffn = jnp.concatenate(pieces, axis=0)
    hp = DEEPNORM_ALPHA * h_ref[...] + ffn
    mu = jnp.mean(hp, axis=-1, keepdims=True)
    hc = hp - mu
    var = jnp.mean(hc * hc, axis=-1, keepdims=True)
    o_ref[...] = hc * lax.rsqrt(var + LN_EPS) * g2_ref[...] + b2_ref[...]


def _stage_combine(dest3, h2, wts, g2, b2, ys, *, tm):
    n = h2.shape[0]
    nsteps = n // tm
    return pl.pallas_call(
        functools.partial(_combine_body, tm=tm, nsteps=nsteps),
        grid=(nsteps,),
        in_specs=[pl.BlockSpec((1, 2, tm), lambda i: (i, 0, 0), memory_space=pltpu.SMEM),
                  pl.BlockSpec((1, 2, tm), lambda i: (jnp.minimum(i + 1, nsteps - 1), 0, 0), memory_space=pltpu.SMEM),
                  pl.BlockSpec((tm, D_MODEL), lambda i: (i, 0)),
                  pl.BlockSpec((8, tm), lambda i: (0, i)),
                  pl.BlockSpec((1, D_MODEL), lambda i: (0, 0)),
                  pl.BlockSpec((1, D_MODEL), lambda i: (0, 0)),
                  pl.BlockSpec(memory_space=pl.ANY)],
        out_specs=pl.BlockSpec((tm, D_MODEL), lambda i: (i, 0)),
        out_shape=jax.ShapeDtypeStruct((n, D_MODEL), f32),
        scratch_shapes=[pltpu.VMEM((2, 2, tm, ROW_TILE, 128), bf16), pltpu.SemaphoreType.DMA((2,))],
        compiler_params=_cparams(("arbitrary",)),
        name="moe_combine",
    )(dest3, dest3, h2, wts, g2, b2, ys)


def _layer(h, w_in, conv_w, a_log, dt_bias, dn_norm_w, sgu_ln_g, sgu_ln_b, w_spatial, b_spatial, w_out,
           ln1_g, ln1_b, w_rg, b_rg, w_re, b_re, w_gate, w_up, w_down, ln2_g, ln2_b,
           *, tm_in, dn_chunks, tm_mix, tm_rank, tm_disp, tm_comb):
    B, T, _ = h.shape
    n = B * T
    qkvz = 4 * DN_WIDTH
    ba0 = qkvz
    uv0 = qkvz + 2 * DN_HEADS
    w_cols = jnp.concatenate(
        [w_in[:, :qkvz], w_in[:, uv0:], w_in[:, ba0:uv0], jnp.zeros((D_MODEL, 128 - 2 * DN_HEADS), f32)], axis=1).astype(bf16)
    wbat = w_in[:, ba0:uv0].T.astype(bf16)
    pcol = jnp.zeros((8, 128), f32).at[0, DN_HEADS:2 * DN_HEADS].set(a_log).at[1, DN_HEADS:2 * DN_HEADS].set(dt_bias)
    prow = jnp.zeros((2, 8, 128), f32)
    prow = prow.at[0, DN_HEADS:2 * DN_HEADS, :].set(jnp.broadcast_to(a_log[:, None], (DN_HEADS, 128)))
    prow = prow.at[1, DN_HEADS:2 * DN_HEADS, :].set(jnp.broadcast_to(dt_bias[:, None], (DN_HEADS, 128)))

    q, k, v, z, u, vln, gcol, grow = _stage_inproj(
        h, w_cols, wbat, conv_w, pcol, prow, sgu_ln_g[None, :], sgu_ln_b[None, :], tm=tm_in)
    ydn = _stage_deltanet(q, k, v, z, gcol, grow, dn_norm_w[None, :], nch=dn_chunks)

    bsp = jnp.broadcast_to(b_spatial.T[:, :, None], (SGU_CHUNK, SGU_GROUPS, SGU_CHUNK)).reshape(SGU_CHUNK, SGU_WIDTH)
    wrt = jnp.zeros((128, D_MODEL), f32).at[0:MOE_GROUPS].set(w_rg.T).at[MOE_GROUPS:MOE_GROUPS + N_EXPERTS].set(w_re.T).astype(bf16)
    brt = jnp.zeros((128,), f32).at[0:MOE_GROUPS].set(b_rg).at[MOE_GROUPS:MOE_GROUPS + N_EXPERTS].set(b_re)
    brt = jnp.broadcast_to(brt[:, None], (128, 128))
    h1, hrow, ids, wts = _stage_mixout(ydn, u, vln, h, w_spatial, bsp, w_out.astype(bf16), ln1_g[None, :],
                                       ln1_b[None, :], wrt, brt, tm=tm_mix)

    p_rows = (-(-(n * 2) // MOE_BLOCK)) * MOE_BLOCK + N_EXPERTS * MOE_BLOCK
    nb = p_rows // MOE_BLOCK
    nb_pad = (-(-nb // 128)) * 128
    rank, cnt = _stage_rank(ids, tm=tm_rank)
    dest, meta, blk = _stage_dest(cnt, ids, rank, tm=tm_rank, nb_pad=nb_pad)

    h2 = h1.reshape(n, D_MODEL)
    dest_d = dest[0:2].reshape(2, n // tm_disp, tm_disp).transpose(1, 0, 2)
    xs = _stage_dispatch(meta[0, :N_EXPERTS], meta[1, :N_EXPERTS], meta[2, 0:1], dest_d, hrow, p_rows, tm=tm_disp)
    ys = _stage_experts(blk[0, :nb], meta[2, 0:1], xs, w_gate, w_up, w_down)
    dest_c = dest[0:2].reshape(2, n // tm_comb, tm_comb).transpose(1, 0, 2)
    out = _stage_combine(dest_c, h2, wts, ln2_g[None, :], ln2_b[None, :], ys, tm=tm_comb)
    return out.reshape(B, T, D_MODEL)


def kernel(x, w_in, conv_w, a_log, dt_bias, dn_norm_w, sgu_ln_g, sgu_ln_b, w_spatial, b_spatial, w_out, ln1_g, ln1_b, w_router_group, b_router_group, w_router_expert, b_router_expert, w_gate, w_up, w_down, ln2_g, ln2_b):
    h = x
    for l in range(w_in.shape[0]):
        h = _layer(h, w_in[l], conv_w[l], a_log[l], dt_bias[l], dn_norm_w[l], sgu_ln_g[l], sgu_ln_b[l],
                   w_spatial[l], b_spatial[l], w_out[l], ln1_g[l], ln1_b[l],
                   w_router_group[l], b_router_group[l], w_router_expert[l], b_router_expert[l],
                   w_gate[l], w_up[l], w_down[l], ln2_g[l], ln2_b[l],
                   tm_in=512, dn_chunks=4, tm_mix=512, tm_rank=1024, tm_disp=256, tm_comb=256)
    return h
```

```python
import functools

import jax
import jax.numpy as jnp
from jax import lax
from jax.experimental import pallas as pl
from jax.experimental.pallas import tpu as pltpu

f32 = jnp.float32
bf16 = jnp.bfloat16
i32 = jnp.int32

D_MODEL = 1024
DN_WIDTH = 512
DN_HEADS = 4
HEAD_DIM = 128
CONV_K = 4
SGU_WIDTH = 512
SGU_GROUPS = 4
SGU_CHUNK = 128
DN_CHUNK = 128
MOE_GROUPS = 8
EXPERTS_PER_GROUP = 8
N_EXPERTS = 64
D_EXPERT = 512
MOE_BLOCK = 256
ROW_TILE = D_MODEL // 128
DEEPNORM_ALPHA = 2.0 ** 0.25
LN_EPS = 1e-5
RMS_EPS = 1e-6
HIGHEST = lax.Precision.HIGHEST
VMEM_LIMIT_BYTES = 56 * 1024 * 1024

NT_DIMS = (((1,), (1,)), ((), ()))


def _cparams(sem, flags=None):
    return pltpu.CompilerParams(dimension_semantics=sem, vmem_limit_bytes=VMEM_LIMIT_BYTES, flags=flags)


def _sigmoid(x):
    return 1.0 / (1.0 + jnp.exp(-x))


def _silu(x):
    h = 0.5 * x
    return h + h * jnp.tanh(h)


def _softplus(x):
    return jnp.maximum(x, 0.0) + jnp.log1p(jnp.exp(-jnp.abs(x)))


def _gelu_tanh(x):
    c = 0.7978845608028654
    return x * (0.5 * (1.0 + jnp.tanh(c * (x + 0.044715 * (x * x * x)))))


def _iota2(shape, axis):
    return lax.broadcasted_iota(i32, shape, axis)


def _inproj_body(x_ref, w_ref, wbat_ref, convw_ref, pcol_ref, prow_ref, lng_ref, lnb_ref, ones_ref,
                 q_ref, k_ref, v_ref, z_ref, u_ref, vln_ref, gcol_ref, grow_ref, eq_ref, ek_ref, ev_ref, *, tm):
    W = DN_WIDTH
    ext_refs = (eq_ref, ek_ref, ev_ref)

    @pl.when(pl.program_id(1) == 0)
    def _():
        for e_ref in ext_refs:
            e_ref[0:8, :] = jnp.zeros((8, W), f32)

    xb = x_ref[0].astype(bf16)
    for part, e_ref in enumerate(ext_refs):
        e_ref[8:8 + tm, :] = jnp.dot(xb, w_ref[:, part * W:(part + 1) * W], preferred_element_type=f32)
    zc = 3 * W
    uc = zc + W
    vc = uc + SGU_WIDTH
    bc = vc + SGU_WIDTH
    pz = jnp.dot(xb, w_ref[:, zc:zc + W], preferred_element_type=f32)
    pu = jnp.dot(xb, w_ref[:, uc:uc + SGU_WIDTH], preferred_element_type=f32)
    pv = jnp.dot(xb, w_ref[:, vc:vc + SGU_WIDTH], preferred_element_type=f32)
    pba = jnp.dot(xb, w_ref[:, bc:bc + 128], preferred_element_type=f32)
    pbat = lax.dot_general(wbat_ref[...], xb, NT_DIMS, preferred_element_type=f32)

    def group_sums(a):
        return jnp.dot(a.astype(bf16), ones_ref[...], preferred_element_type=f32)

    for part, (e_ref, out_ref) in enumerate(zip(ext_refs, (q_ref, k_ref, v_ref))):
        cs = slice(part * W, (part + 1) * W)
        y = convw_ref[3:4, cs] * e_ref[8:8 + tm, :]
        for j in range(CONV_K - 1):
            y = y + convw_ref[j:j + 1, cs] * e_ref[5 + j:5 + j + tm, :]
        y = _silu(y)
        if part < 2:
            scale = HEAD_DIM ** -0.5 if part == 0 else 1.0
            y = y * (lax.rsqrt(group_sums(y * y) + RMS_EPS) * scale)
        out_ref[0] = y.astype(bf16)
        e_ref[0:8, :] = e_ref[tm:tm + 8, :]

    z_ref[0] = _silu(pz).astype(bf16)

    u_ref[0] = _gelu_tanh(pu).astype(bf16)
    pv = _gelu_tanh(pv)
    for g in range(SGU_GROUPS):
        sl = slice(g * SGU_CHUNK, (g + 1) * SGU_CHUNK)
        vg = pv[:, sl]
        mu = jnp.mean(vg, axis=-1, keepdims=True)
        vcn = vg - mu
        var = jnp.mean(vcn * vcn, axis=-1, keepdims=True)
        vln_ref[0, :, sl] = (vcn * lax.rsqrt(var + LN_EPS) * lng_ref[:, sl] + lnb_ref[:, sl]).astype(bf16)

    lane = _iota2((DN_CHUNK, 128), 1)
    is_g = (lane >= DN_HEADS) & (lane < 2 * DN_HEADS)
    beta = _sigmoid(pba)
    gval = -jnp.exp(pcol_ref[0:1, :]) * _softplus(pba + pcol_ref[1:2, :])
    ci = _iota2((DN_CHUNK, DN_CHUNK), 0)
    cj = _iota2((DN_CHUNK, DN_CHUNK), 1)
    ltri = (ci >= cj).astype(f32)
    utri = (ci <= cj).astype(f32)
    sub = _iota2((8, DN_CHUNK), 0)
    for c in range(tm // DN_CHUNK):
        rs = slice(c * DN_CHUNK, (c + 1) * DN_CHUNK)
        gc = jnp.dot(ltri, jnp.where(is_g, gval[rs], 0.0), precision=HIGHEST, preferred_element_type=f32)
        gcol_ref[0, rs, :] = jnp.where(lane < DN_HEADS, beta[rs], gc)
        gt = -jnp.exp(prow_ref[0]) * _softplus(pbat[:, rs] + prow_ref[1])
        gt = jnp.where(sub >= DN_HEADS, gt, 0.0)
        grow_ref[0, :, rs] = jnp.dot(gt, utri, precision=HIGHEST, preferred_element_type=f32)


def _stage_inproj(x, w_re, wbat, conv_w, pcol, prow, lng, lnb, *, tm):
    B, T, _ = x.shape
    wcols = w_re.shape[1]
    grid = (B, T // tm)
    gi = lax.broadcasted_iota(i32, (DN_WIDTH, DN_WIDTH), 0) // 128
    gj = lax.broadcasted_iota(i32, (DN_WIDTH, DN_WIDTH), 1) // 128
    group_ones = (gi == gj).astype(bf16)
    act = lambda: jax.ShapeDtypeStruct((B, T, DN_WIDTH), bf16)
    act_spec = lambda: pl.BlockSpec((1, tm, DN_WIDTH), lambda b, t: (b, t, 0))
    const2 = lambda shp: pl.BlockSpec(shp, lambda b, t: (0, 0))
    return pl.pallas_call(
        functools.partial(_inproj_body, tm=tm),
        grid=grid,
        in_specs=[
            pl.BlockSpec((1, tm, D_MODEL), lambda b, t: (b, t, 0)),
            const2((D_MODEL, wcols)),
            const2((8, D_MODEL)),
            const2((CONV_K, 3 * DN_WIDTH)),
            const2((8, 128)),
            pl.BlockSpec((2, 8, 128), lambda b, t: (0, 0, 0)),
            const2((1, SGU_WIDTH)),
            const2((1, SGU_WIDTH)),
            const2((DN_WIDTH, DN_WIDTH)),
        ],
        out_specs=[act_spec() for _ in range(6)] + [
            pl.BlockSpec((1, tm, 128), lambda b, t: (b, t, 0)),
            pl.BlockSpec((1, 8, tm), lambda b, t: (b, 0, t)),
        ],
        out_shape=[act() for _ in range(6)] + [
            jax.ShapeDtypeStruct((B, T, 128), f32),
            jax.ShapeDtypeStruct((B, 8, T), f32),
        ],
        scratch_shapes=[pltpu.VMEM((tm + 8, DN_WIDTH), f32) for _ in range(3)],
        compiler_params=_cparams(("arbitrary", "arbitrary")),
        name="inproj",
    )(x, w_re, wbat, conv_w, pcol, prow, lng, lnb, group_ones)


def _mm(a, b):
    return jnp.dot(a.astype(bf16), b.astype(bf16), preferred_element_type=f32)


def _unit_lower_inverse(nmats, ii, jj):
    n = nmats[0].shape[0]
    eye = (ii == jj).astype(f32)
    leaf = jnp.right_shift(ii, 3) == jnp.right_shift(jj, 3)
    dblk = [jnp.where(leaf, m, 0.0) for m in nmats]
    s1 = [_mm(d, d) for d in dblk]
    r1 = [eye - d for d in dblk]
    both = [_mm(s, jnp.concatenate([s, r], axis=1)) for s, r in zip(s1, r1)]
    r2 = [r + bo[:, n:] for r, bo in zip(r1, both)]
    xs = [r + _mm(bo[:, :n], r) for r, bo in zip(r2, both)]
    size = 8
    while size < n:
        lows = [slice(r + size, r + 2 * size) for r in range(0, n, 2 * size)]
        ups = [slice(r, r + size) for r in range(0, n, 2 * size)]
        rsel = _iota2((n // 2, n), 0)
        ilow = rsel + size * (jnp.right_shift(rsel, size.bit_length() - 1) + 1)
        jlow = _iota2((n // 2, n), 1)
        in_pair_upper = (jlow >= ilow - (ilow & (2 * size - 1))) & (jlow < ilow - (ilow & (size - 1)))
        zeros = jnp.zeros((size, n), f32)
        new_xs = []
        ylows = [_mm(jnp.where(in_pair_upper, jnp.concatenate([m[s] for s in lows], axis=0), 0.0), x)
                 for m, x in zip(nmats, xs)]
        yfull = [jnp.concatenate([piece for k in range(len(lows)) for piece in (zeros, y[k * size:(k + 1) * size])], axis=0)
                 for y in ylows]
        corr = [_mm(jnp.concatenate([x[s] for s in lows], axis=0), yf) for x, yf in zip(xs, yfull)]
        for x, c in zip(xs, corr):
            pieces = []
            for k, (u, l) in enumerate(zip(ups, lows)):
                pieces += [x[u], x[l] - c[k * size:(k + 1) * size]]
            new_xs.append(jnp.concatenate(pieces, axis=0))
        xs = new_xs
        size *= 2
    return xs


def _deltanet_body(q_ref, k_ref, v_ref, z_ref, gcol_ref, grow_ref, nw_ref, y_ref, s_ref, *, nbr, nch, ngroups):
    @pl.when(pl.program_id(1) == 0)
    def _():
        s_ref[...] = jnp.zeros(s_ref.shape, f32)

    per = nbr // ngroups
    for g in range(ngroups):
        _deltanet_rows(q_ref, k_ref, v_ref, z_ref, gcol_ref, grow_ref, nw_ref, y_ref, s_ref,
                       rows=range(g * per, (g + 1) * per), nch=nch)


def _deltanet_rows(q_ref, k_ref, v_ref, z_ref, gcol_ref, grow_ref, nw_ref, y_ref, s_ref, *, rows, nch):
    C = DN_CHUNK
    S = [(b, h) for b in rows for h in range(DN_HEADS)]
    P = [(b, c, h) for c in range(nch) for b, h in S]

    ii = _iota2((C, C), 0)
    jj = _iota2((C, C), 1)
    causal = ii >= jj
    rs = [slice(c * C, (c + 1) * C) for c in range(nch)]
    hs = [slice(h * HEAD_DIM, (h + 1) * HEAD_DIM) for h in range(DN_HEADS)]
    gcol = {(b, c): gcol_ref[b, rs[c], :] for b in rows for c in range(nch)}
    qh = {(b, c, h): q_ref[b, rs[c], hs[h]] for b, c, h in P}
    kh = {(b, c, h): k_ref[b, rs[c], hs[h]] for b, c, h in P}
    vh = {(b, c, h): v_ref[b, rs[c], hs[h]] for b, c, h in P}
    gc_b = {(b, c, h): jnp.broadcast_to(gcol[b, c][:, DN_HEADS + h:DN_HEADS + h + 1], (C, HEAD_DIM)) for b, c, h in P}
    beta_b = {(b, c, h): jnp.broadcast_to(gcol[b, c][:, h:h + 1], (C, HEAD_DIM)) for b, c, h in P}
    gc_r = {(b, c, h): jnp.broadcast_to(grow_ref[b, DN_HEADS + h:DN_HEADS + h + 1, rs[c]], (C, C)) for b, c, h in P}
    decay = {p: jnp.exp(jnp.where(causal, gc_b[p] - gc_r[p], -1e30)) for p in P}

    kf = {p: kh[p].astype(f32) for p in P}
    kb = {p: kf[p] * beta_b[p] for p in P}
    kk = {p: lax.dot_general(kb[p].astype(bf16), kh[p], NT_DIMS, preferred_element_type=f32) for p in P}
    a_intra = {p: lax.dot_general(qh[p], kh[p], NT_DIMS, preferred_element_type=f32) * decay[p] for p in P}
    nmat = [jnp.where(ii > jj, kk[p] * decay[p], 0.0) for p in P]
    tinv = dict(zip(P, _unit_lower_inverse(nmat, ii, jj)))

    eg = {p: jnp.exp(gc_b[p]) for p in P}
    rhs = {p: jnp.concatenate([vh[p].astype(f32) * beta_b[p], kb[p] * eg[p]], axis=1) for p in P}
    sol = {p: _mm(tinv[p], rhs[p]) for p in P}

    q_dec = {p: qh[p].astype(f32) * eg[p] for p in P}
    g_last = {p: gc_b[p][C - 1:C, :] for p in P}
    kdt = {p: (kf[p] * jnp.exp(g_last[p] - gc_b[p])).T for p in P}

    state = {(b, h): s_ref[b * DN_HEADS + h] for b, h in S}
    for c in range(nch):
        m1 = {(b, h): _mm(jnp.concatenate([sol[b, c, h][:, HEAD_DIM:], q_dec[b, c, h]], axis=0), state[b, h]) for b, h in S}
        v_new = {(b, h): sol[b, c, h][:, :HEAD_DIM] - m1[b, h][:C] for b, h in S}
        m2 = {(b, h): _mm(jnp.concatenate([a_intra[b, c, h], kdt[b, c, h]], axis=0), v_new[b, h]) for b, h in S}
        state = {(b, h): state[b, h] * jnp.exp(g_last[b, c, h]) + m2[b, h][C:] for b, h in S}
        for b, h in S:
            o = m1[b, h][C:] + m2[b, h][:C]
            rms = lax.rsqrt(jnp.mean(o * o, axis=-1, keepdims=True) + RMS_EPS)
            y_ref[b, rs[c], hs[h]] = (o * rms * nw_ref[...] * z_ref[b, rs[c], hs[h]].astype(f32)).astype(bf16)
    for b, h in S:
        s_ref[b * DN_HEADS + h] = state[b, h]


def _stage_deltanet(q, k, v, z, gcol, grow, norm_w, *, nbr, nch, ngroups):
    B, T, _ = q.shape
    tt = nch * DN_CHUNK
    act_spec = lambda: pl.BlockSpec((nbr, tt, DN_WIDTH), lambda b, t: (b, t, 0))
    return pl.pallas_call(
        functools.partial(_deltanet_body, nbr=nbr, nch=nch, ngroups=ngroups),
        grid=(B // nbr, T // tt),
        in_specs=[act_spec(), act_spec(), act_spec(), act_spec(),
                  pl.BlockSpec((nbr, tt, 128), lambda b, t: (b, t, 0)),
                  pl.BlockSpec((nbr, 8, tt), lambda b, t: (b, 0, t)),
                  pl.BlockSpec((1, HEAD_DIM), lambda b, t: (0, 0))],
        out_specs=act_spec(),
        out_shape=jax.ShapeDtypeStruct((B, T, DN_WIDTH), bf16),
        scratch_shapes=[pltpu.VMEM((nbr * DN_HEADS, HEAD_DIM, HEAD_DIM), f32)],
        compiler_params=_cparams(("arbitrary", "arbitrary")),
        name="deltanet",
    )(q, k, v, z, gcol, grow, norm_w)


def _mixout_body(ydn_ref, u_ref, vln_ref, x_ref, ws_ref, bsp_ref, wout_ref, g1_ref, b1_ref, wrt_ref, brt_ref,
                 h_ref, hrow_ref, ids_ref, wts_ref, ycat_ref, *, tm):
    C = SGU_CHUNK
    ii = _iota2((C, C), 0)
    jj = _iota2((C, C), 1)
    ycat_ref[:, 0:DN_WIDTH] = ydn_ref[0]
    for g in range(SGU_GROUPS):
        gs = slice(g * C, (g + 1) * C)
        wsg = jnp.where(ii >= jj, ws_ref[g], 0.0).astype(bf16)
        for c in range(tm // C):
            rs = slice(c * C, (c + 1) * C)
            mixed = jnp.dot(wsg, vln_ref[0, rs, gs], preferred_element_type=f32) + bsp_ref[:, gs]
            ycat_ref[rs, DN_WIDTH + g * C:DN_WIDTH + (g + 1) * C] = (u_ref[0, rs, gs].astype(f32) * mixed).astype(bf16)

    RB = 128
    blocks = [slice(r, r + RB) for r in range(0, tm, RB)]
    mix = [jnp.dot(ycat_ref[rb, :], wout_ref[...], preferred_element_type=f32) for rb in blocks]
    h1s = []
    for rb, m in zip(blocks, mix):
        hp = DEEPNORM_ALPHA * x_ref[0, rb, :] + m
        mu = jnp.mean(hp, axis=-1, keepdims=True)
        hc = hp - mu
        var = jnp.mean(hc * hc, axis=-1, keepdims=True)
        h1 = hc * lax.rsqrt(var + LN_EPS) * g1_ref[...] + b1_ref[...]
        h_ref[0, rb, :] = h1
        h1b = h1.astype(bf16)
        hrow_ref[rb] = h1b.reshape(RB, ROW_TILE, 128)
        h1s.append(h1b)

    logit_blocks = [lax.dot_general(wrt_ref[...], hb, NT_DIMS, preferred_element_type=f32) + brt_ref[...] for hb in h1s]
    sub = _iota2((8, RB), 0)
    subf = sub.astype(f32)
    for rb, logits in zip(blocks, logit_blocks):
        gl = logits[0:8]
        gmax = jnp.max(gl, axis=0, keepdims=True)
        g_idx = jnp.min(jnp.where(gl == gmax, subf, float(MOE_GROUPS)), axis=0, keepdims=True)
        p_group = 1.0 / jnp.sum(jnp.exp(gl - gmax), axis=0, keepdims=True)
        within = jnp.zeros((8, RB), f32)
        for g in range(MOE_GROUPS):
            within = within + jnp.where(g_idx == float(g), logits[8 + 8 * g:16 + 8 * g], 0.0)
        m1 = jnp.max(within, axis=0, keepdims=True)
        i1 = jnp.min(jnp.where(within == m1, subf, float(EXPERTS_PER_GROUP)), axis=0, keepdims=True)
        rest = jnp.where(subf == i1, -jnp.inf, within)
        m2 = jnp.max(rest, axis=0, keepdims=True)
        i2 = jnp.min(jnp.where(rest == m2, subf, float(EXPERTS_PER_GROUP)), axis=0, keepdims=True)
        e = jnp.exp(m2 - m1)
        w1 = p_group / (1.0 + e)
        w2 = p_group * e / (1.0 + e)
        e1 = g_idx * float(EXPERTS_PER_GROUP) + i1
        e2 = g_idx * float(EXPERTS_PER_GROUP) + i2
        ids_ref[:, rb] = jnp.where(sub == 0, e1, jnp.where(sub == 1, e2, 0.0)).astype(i32)
        wts_ref[:, rb] = jnp.where(sub == 0, w1, jnp.where(sub == 1, w2, 0.0))


def _stage_mixout(ydn, u, vln, x, ws, bsp, wout, g1, b1, wrt, brt, *, tm):
    B, T, _ = x.shape
    nt = T // tm
    act_spec = lambda: pl.BlockSpec((1, tm, DN_WIDTH), lambda b, t: (b, t, 0))
    const2 = lambda shp: pl.BlockSpec(shp, lambda b, t: (0, 0))
    tok_spec = lambda: pl.BlockSpec((8, tm), lambda b, t: (0, b * nt + t))
    return pl.pallas_call(
        functools.partial(_mixout_body, tm=tm),
        grid=(B, nt),
        in_specs=[act_spec(), act_spec(), act_spec(),
                  pl.BlockSpec((1, tm, D_MODEL), lambda b, t: (b, t, 0)),
                  pl.BlockSpec((SGU_GROUPS, SGU_CHUNK, SGU_CHUNK), lambda b, t: (0, 0, 0)),
                  const2((SGU_CHUNK, SGU_WIDTH)),
                  const2((D_MODEL, D_MODEL)),
                  const2((1, D_MODEL)), const2((1, D_MODEL)),
                  const2((128, D_MODEL)), const2((128, 128))],
        out_specs=[pl.BlockSpec((1, tm, D_MODEL), lambda b, t: (b, t, 0)),
                   pl.BlockSpec((tm, ROW_TILE, 128), lambda b, t: (b * nt + t, 0, 0)), tok_spec(), tok_spec()],
        out_shape=[jax.ShapeDtypeStruct((B, T, D_MODEL), f32),
                   jax.ShapeDtypeStruct((B * T, ROW_TILE, 128), bf16),
                   jax.ShapeDtypeStruct((8, B * T), i32),
                   jax.ShapeDtypeStruct((8, B * T), f32)],
        scratch_shapes=[pltpu.VMEM((tm, D_MODEL), bf16)],
        compiler_params=_cparams(("arbitrary", "arbitrary")),
        name="mixout",
    )(ydn, u, vln, x, ws, bsp, wout, g1, b1, wrt, brt)


def _rank_body(ids_ref, rank_ref, cnt_ref, base_ref, *, tm):
    @pl.when(pl.program_id(0) == 0)
    def _():
        base_ref[...] = jnp.zeros(base_ref.shape, f32)

    sub = _iota2((N_EXPERTS, tm), 0)
    oh1 = (sub == ids_ref[0:1, :]).astype(f32)
    oh2 = (sub == ids_ref[1:2, :]).astype(f32)
    oh = (oh1 + oh2).astype(bf16)
    ti = _iota2((tm, tm), 0)
    tj = _iota2((tm, tm), 1)
    before = (ti < tj).astype(bf16)
    prefix = jnp.dot(oh, before, preferred_element_type=f32)
    tot = prefix + jnp.concatenate([base_ref[...]] * (tm // 128), axis=1)
    r1 = jnp.sum(oh1 * tot, axis=0, keepdims=True)
    r2 = jnp.sum(oh2 * tot, axis=0, keepdims=True)
    sub8 = _iota2((8, tm), 0)
    rank_ref[...] = jnp.where(sub8 == 0, r1, jnp.where(sub8 == 1, r2, 0.0)).astype(i32)
    base_ref[...] = base_ref[...] + jnp.dot(oh, jnp.ones((tm, 128), bf16), preferred_element_type=f32)
    cnt_ref[...] = base_ref[...]


def _stage_rank(ids, *, tm):
    n = ids.shape[1]
    return pl.pallas_call(
        functools.partial(_rank_body, tm=tm),
        grid=(n // tm,),
        in_specs=[pl.BlockSpec((8, tm), lambda i: (0, i))],
        out_specs=[pl.BlockSpec((8, tm), lambda i: (0, i)),
                   pl.BlockSpec((N_EXPERTS, 128), lambda i: (0, 0))],
        out_shape=[jax.ShapeDtypeStruct((8, n), i32), jax.ShapeDtypeStruct((N_EXPERTS, 128), f32)],
        scratch_shapes=[pltpu.VMEM((N_EXPERTS, 128), f32)],
        compiler_params=_cparams(("arbitrary",)),
        name="moe_rank",
    )(ids)


def _dest_body(cnt_ref, ids_ref, rank_ref, dest_ref, meta_ref, blk_ref, *, tm, nb_pad):
    cnt = cnt_ref[...]
    padded = jnp.floor((cnt + (MOE_BLOCK - 1)) * (1.0 / MOE_BLOCK)) * MOE_BLOCK
    ei = _iota2((N_EXPERTS, N_EXPERTS), 0)
    ej = _iota2((N_EXPERTS, N_EXPERTS), 1)
    pends = jnp.dot((ei >= ej).astype(f32), padded, precision=HIGHEST, preferred_element_type=f32)
    pstart = pends - padded
    sub = _iota2((N_EXPERTS, tm), 0)
    pst = jnp.concatenate([pstart] * (tm // 128), axis=1)
    d1 = jnp.sum(jnp.where(sub == ids_ref[0:1, :], pst, 0.0), axis=0, keepdims=True)
    d2 = jnp.sum(jnp.where(sub == ids_ref[1:2, :], pst, 0.0), axis=0, keepdims=True)
    sub8 = _iota2((8, tm), 0)
    dest_ref[...] = jnp.where(sub8 == 0, d1, jnp.where(sub8 == 1, d2, 0.0)).astype(i32) + rank_ref[...]

    @pl.when(pl.program_id(0) == 0)
    def _():
        s64 = _iota2((N_EXPERTS, 128), 0)
        l64 = _iota2((N_EXPERTS, 128), 1)
        diag = s64 == l64
        fill_off = jnp.sum(jnp.where(diag, pstart + cnt, 0.0), axis=0, keepdims=True)
        fill_n = jnp.sum(jnp.where(diag, padded - cnt, 0.0), axis=0, keepdims=True)
        nused = pends[N_EXPERTS - 1:N_EXPERTS, :] * (1.0 / MOE_BLOCK)
        m8 = _iota2((8, 128), 0)
        meta_ref[...] = jnp.where(m8 == 0, fill_off, jnp.where(m8 == 1, fill_n, jnp.where(m8 == 2, nused, 0.0))).astype(i32)
        bstart = (_iota2((N_EXPERTS, nb_pad), 1) * MOE_BLOCK).astype(f32)
        pe = jnp.concatenate([pends] * (nb_pad // 128), axis=1)
        be = jnp.sum((pe <= bstart).astype(f32), axis=0, keepdims=True)
        be = jnp.minimum(be, float(N_EXPERTS - 1))
        blk_ref[...] = jnp.broadcast_to(be, (8, nb_pad)).astype(i32)


def _stage_dest(cnt, ids, rank, *, tm, nb_pad):
    n = ids.shape[1]
    tok = lambda: pl.BlockSpec((8, tm), lambda i: (0, i))
    return pl.pallas_call(
        functools.partial(_dest_body, tm=tm, nb_pad=nb_pad),
        grid=(n // tm,),
        in_specs=[pl.BlockSpec((N_EXPERTS, 128), lambda i: (0, 0)), tok(), tok()],
        out_specs=[tok(), pl.BlockSpec((8, 128), lambda i: (0, 0)), pl.BlockSpec((8, nb_pad), lambda i: (0, 0))],
        out_shape=[jax.ShapeDtypeStruct((8, n), i32), jax.ShapeDtypeStruct((8, 128), i32),
                   jax.ShapeDtypeStruct((8, nb_pad), i32)],
        compiler_params=_cparams(("arbitrary",)),
        name="moe_dest",
    )(cnt, ids, rank)


def _dispatch_body(fill_off_ref, fill_n_ref, nused_ref, dest_ref, h3_ref, xs_ref, zero_ref, sem, zsem, *, tm):
    def row_copy(t, d):
        return pltpu.make_async_copy(h3_ref.at[t], xs_ref.at[d], sem)

    def issue(t, carry):
        row_copy(t, dest_ref[0, 0, t]).start(priority=0)
        row_copy(t, dest_ref[0, 1, t]).start(priority=1)
        return carry

    lax.fori_loop(0, tm, issue, 0, unroll=8)

    @pl.when(pl.program_id(0) == 0)
    def _():
        zero_ref[...] = jnp.zeros(zero_ref.shape, bf16)

        def fill(start):
            def body(e, carry):
                off = fill_off_ref[e]
                npad = fill_n_ref[e]
                bit = MOE_BLOCK // 2
                while bit:
                    @pl.when((npad & bit) != 0)
                    def _(off=off, bit=bit):
                        cp = pltpu.make_async_copy(zero_ref.at[pl.ds(0, bit)], xs_ref.at[pl.ds(off, bit)], zsem)
                        cp.start() if start else cp.wait()
                    off = off + (npad & bit)
                    bit //= 2
                return carry
            return body

        lax.fori_loop(0, N_EXPERTS, fill(True), 0)
        lax.fori_loop(0, N_EXPERTS, fill(False), 0)

        def tail_copy(b):
            return pltpu.make_async_copy(zero_ref, xs_ref.at[pl.ds(b * MOE_BLOCK, MOE_BLOCK)], zsem)

        nblocks = xs_ref.shape[0] // MOE_BLOCK
        lax.fori_loop(nused_ref[0], nblocks, lambda b, c: (tail_copy(b).start(), c)[1], 0)
        lax.fori_loop(nused_ref[0], nblocks, lambda b, c: (tail_copy(0).wait(), c)[1], 0)

    for _ in range(2):
        pltpu.make_async_copy(h3_ref, xs_ref.at[pl.ds(0, tm)], sem).wait()


def _stage_dispatch(fill_off, fill_n, nused, dest3, hrow, p_rows, *, tm):
    n = hrow.shape[0]
    return pl.pallas_call(
        functools.partial(_dispatch_body, tm=tm),
        grid_spec=pltpu.PrefetchScalarGridSpec(
            num_scalar_prefetch=3,
            grid=(n // tm,),
            in_specs=[pl.BlockSpec((1, 2, tm), lambda i, fo, fn, nu: (i, 0, 0), memory_space=pltpu.SMEM),
                      pl.BlockSpec((tm, ROW_TILE, 128), lambda i, fo, fn, nu: (i, 0, 0))],
            out_specs=pl.BlockSpec(memory_space=pl.ANY),
            scratch_shapes=[pltpu.VMEM((MOE_BLOCK, ROW_TILE, 128), bf16),
                            pltpu.SemaphoreType.DMA, pltpu.SemaphoreType.DMA],
        ),
        out_shape=jax.ShapeDtypeStruct((p_rows, ROW_TILE, 128), bf16),
        compiler_params=_cparams(("arbitrary",)),
        name="moe_dispatch",
    )(fill_off, fill_n, nused, dest3, hrow)


def _experts_body(blk_ref, nused_ref, xs_ref, wg_hbm, wu_hbm, wd_hbm, ys_ref,
                  wg32_ref, wu32_ref, wd32_ref, wgu16_ref, wd16_ref, wsem):
    i = pl.program_id(0)
    nused = nused_ref[0]
    used = i < nused
    e = blk_ref[i]

    def weight_copies(ex):
        return (pltpu.make_async_copy(wg_hbm.at[ex], wg32_ref, wsem.at[0]),
                pltpu.make_async_copy(wu_hbm.at[ex], wu32_ref, wsem.at[1]),
                pltpu.make_async_copy(wd_hbm.at[ex], wd32_ref, wsem.at[2]))

    @pl.when((i == 0) & used)
    def _():
        for cp in weight_copies(e):
            cp.start()

    @pl.when(used & ((i == 0) | (e != blk_ref[jnp.maximum(i - 1, 0)])))
    def _():
        for cp in weight_copies(e):
            cp.wait()
        wgu16_ref[:, 0:D_EXPERT] = wg32_ref[...].astype(bf16)
        wgu16_ref[:, D_EXPERT:2 * D_EXPERT] = wu32_ref[...].astype(bf16)
        wd16_ref[...] = wd32_ref[...].astype(bf16)
        nxt = lax.while_loop(lambda j: (j < nused) & (blk_ref[jnp.minimum(j, nused - 1)] == e), lambda j: j + 1, i + 1)

        @pl.when(nxt < nused)
        def _():
            for cp in weight_copies(blk_ref[jnp.minimum(nxt, nused - 1)]):
                cp.start()

    @pl.when(used)
    def _():
        half = MOE_BLOCK // 2
        rows = [slice(p * half, (p + 1) * half) for p in range(2)]
        gu = [jnp.dot(xs_ref[r].reshape(half, D_MODEL), wgu16_ref[...], preferred_element_type=f32) for r in rows]
        hid = [(_silu(g[:, :D_EXPERT]) * g[:, D_EXPERT:]).astype(bf16) for g in gu]
        y = [jnp.dot(hd, wd16_ref[...], preferred_element_type=f32) for hd in hid]
        for r, yp in zip(rows, y):
            ys_ref[r] = yp.astype(bf16).reshape(half, ROW_TILE, 128)

    @pl.when(jnp.logical_not(used))
    def _():
        ys_ref[...] = jnp.zeros(ys_ref.shape, bf16)


def _stage_experts(blk_e, nused, xs, w_gate, w_up, w_down):
    p_rows = xs.shape[0]
    nb = p_rows // MOE_BLOCK

    def last_used(i, nu):
        return jnp.maximum(jnp.minimum(i, nu[0] - 1), 0)

    def x_map(i, blk, nu):
        return (last_used(i, nu), 0, 0)

    def row_map(i, blk, nu):
        return (i, 0, 0)

    return pl.pallas_call(
        _experts_body,
        grid_spec=pltpu.PrefetchScalarGridSpec(
            num_scalar_prefetch=2,
            grid=(nb,),
            in_specs=[pl.BlockSpec((MOE_BLOCK, ROW_TILE, 128), x_map),
                      pl.BlockSpec(memory_space=pl.ANY),
                      pl.BlockSpec(memory_space=pl.ANY),
                      pl.BlockSpec(memory_space=pl.ANY)],
            out_specs=pl.BlockSpec((MOE_BLOCK, ROW_TILE, 128), row_map),
            scratch_shapes=[pltpu.VMEM((D_MODEL, D_EXPERT), f32), pltpu.VMEM((D_MODEL, D_EXPERT), f32),
                            pltpu.VMEM((D_EXPERT, D_MODEL), f32),
                            pltpu.VMEM((D_MODEL, 2 * D_EXPERT), bf16), pltpu.VMEM((D_EXPERT, D_MODEL), bf16),
                            pltpu.SemaphoreType.DMA((3,))],
        ),
        out_shape=jax.ShapeDtypeStruct((p_rows, ROW_TILE, 128), bf16),
        compiler_params=_cparams(("arbitrary",)),
        name="moe_experts",
    )(blk_e, nused, xs, w_gate, w_up, w_down)


def _combine_body(dcur_ref, dnext_ref, h_ref, wts_ref, g2_ref, b2_ref, ys_ref, o_ref, ybuf_ref, sem, *, tm, nsteps):
    i = pl.program_id(0)
    slot = lax.rem(i, 2)

    def issue_tile(d_ref, s):
        def body(t, carry):
            for k in range(2):
                pltpu.make_async_copy(ys_ref.at[d_ref[0, k, t]], ybuf_ref.at[s, k, t], sem.at[s]).start(priority=k)
            return carry

        lax.fori_loop(0, tm, body, 0, unroll=8)

    @pl.when(i == 0)
    def _():
        issue_tile(dcur_ref, 0)

    @pl.when(i + 1 < nsteps)
    def _():
        issue_tile(dnext_ref, 1 - slot)

    for k in range(2):
        pltpu.make_async_copy(ys_ref.at[pl.ds(0, tm)], ybuf_ref.at[slot, k], sem.at[slot]).wait()

    pieces = []
    for c in range(tm // 128):
        ls = slice(c * 128, (c + 1) * 128)
        w1c = jnp.broadcast_to(wts_ref[0:1, ls], (128, 128)).T
        w2c = jnp.broadcast_to(wts_ref[1:2, ls], (128, 128)).T
        w1f = jnp.concatenate([w1c] * (D_MODEL // 128), axis=1)
        w2f = jnp.concatenate([w2c] * (D_MODEL // 128), axis=1)
        y1 = ybuf_ref[slot, 0, ls].reshape(128, D_MODEL).astype(f32)
        y2 = ybuf_ref[slot, 1, ls].reshape(128, D_MODEL).astype(f32)
        pieces.append(w1f * y1 + w2f * y2)
    ffn = jnp.concatenate(pieces, axis=0)
    hp = DEEPNORM_ALPHA * h_ref[...] + ffn
    mu = jnp.mean(hp, axis=-1, keepdims=True)
    hc = hp - mu
    var = jnp.mean(hc * hc, axis=-1, keepdims=True)
    o_ref[...] = hc * lax.rsqrt(var + LN_EPS) * g2_ref[...] + b2_ref[...]


def _stage_combine(dest3, h2, wts, g2, b2, ys, *, tm):
    n = h2.shape[0]
    nsteps = n // tm
    return pl.pallas_call(
        functools.partial(_combine_body, tm=tm, nsteps=nsteps),
        grid=(nsteps,),
        in_specs=[pl.BlockSpec((1, 2, tm), lambda i: (i, 0, 0), memory_space=pltpu.SMEM),
                  pl.BlockSpec((1, 2, tm), lambda i: (jnp.minimum(i + 1, nsteps - 1), 0, 0), memory_space=pltpu.SMEM),
                  pl.BlockSpec((tm, D_MODEL), lambda i: (i, 0)),
                  pl.BlockSpec((8, tm), lambda i: (0, i)),
                  pl.BlockSpec((1, D_MODEL), lambda i: (0, 0)),
                  pl.BlockSpec((1, D_MODEL), lambda i: (0, 0)),
                  pl.BlockSpec(memory_space=pl.ANY)],
        out_specs=pl.BlockSpec((tm, D_MODEL), lambda i: (i, 0)),
        out_shape=jax.ShapeDtypeStruct((n, D_MODEL), f32),
        scratch_shapes=[pltpu.VMEM((2, 2, tm, ROW_TILE, 128), bf16), pltpu.SemaphoreType.DMA((2,))],
        compiler_params=_cparams(("arbitrary",)),
        name="moe_combine",
    )(dest3, dest3, h2, wts, g2, b2, ys)


def _layer(h, w_in, conv_w, a_log, dt_bias, dn_norm_w, sgu_ln_g, sgu_ln_b, w_spatial, b_spatial, w_out,
           ln1_g, ln1_b, w_rg, b_rg, w_re, b_re, w_gate, w_up, w_down, ln2_g, ln2_b,
           *, tm_in, dn_rows, dn_chunks, dn_groups, tm_mix, tm_rank, tm_disp, tm_comb):
    B, T, _ = h.shape
    n = B * T
    qkvz = 4 * DN_WIDTH
    ba0 = qkvz
    uv0 = qkvz + 2 * DN_HEADS
    w_cols = jnp.concatenate(
        [w_in[:, :qkvz], w_in[:, uv0:], w_in[:, ba0:uv0], jnp.zeros((D_MODEL, 128 - 2 * DN_HEADS), f32)], axis=1).astype(bf16)
    wbat = w_in[:, ba0:uv0].T.astype(bf16)
    pcol = jnp.zeros((8, 128), f32).at[0, DN_HEADS:2 * DN_HEADS].set(a_log).at[1, DN_HEADS:2 * DN_HEADS].set(dt_bias)
    prow = jnp.zeros((2, 8, 128), f32)
    prow = prow.at[0, DN_HEADS:2 * DN_HEADS, :].set(jnp.broadcast_to(a_log[:, None], (DN_HEADS, 128)))
    prow = prow.at[1, DN_HEADS:2 * DN_HEADS, :].set(jnp.broadcast_to(dt_bias[:, None], (DN_HEADS, 128)))

    q, k, v, z, u, vln, gcol, grow = _stage_inproj(
        h, w_cols, wbat, conv_w, pcol, prow, sgu_ln_g[None, :], sgu_ln_b[None, :], tm=tm_in)
    ydn = _stage_deltanet(q, k, v, z, gcol, grow, dn_norm_w[None, :], nbr=dn_rows, nch=dn_chunks, ngroups=dn_groups)

    bsp = jnp.broadcast_to(b_spatial.T[:, :, None], (SGU_CHUNK, SGU_GROUPS, SGU_CHUNK)).reshape(SGU_CHUNK, SGU_WIDTH)
    wrt = jnp.zeros((128, D_MODEL), f32).at[0:MOE_GROUPS].set(w_rg.T).at[MOE_GROUPS:MOE_GROUPS + N_EXPERTS].set(w_re.T).astype(bf16)
    brt = jnp.zeros((128,), f32).at[0:MOE_GROUPS].set(b_rg).at[MOE_GROUPS:MOE_GROUPS + N_EXPERTS].set(b_re)
    brt = jnp.broadcast_to(brt[:, None], (128, 128))
    h1, hrow, ids, wts = _stage_mixout(ydn, u, vln, h, w_spatial, bsp, w_out.astype(bf16), ln1_g[None, :],
                                       ln1_b[None, :], wrt, brt, tm=tm_mix)

    p_rows = (-(-(n * 2) // MOE_BLOCK)) * MOE_BLOCK + N_EXPERTS * MOE_BLOCK
    nb = p_rows // MOE_BLOCK
    nb_pad = (-(-nb // 128)) * 128
    rank, cnt = _stage_rank(ids, tm=tm_rank)
    dest, meta, blk = _stage_dest(cnt, ids, rank, tm=tm_rank, nb_pad=nb_pad)

    h2 = h1.reshape(n, D_MODEL)
    dest_d = dest[0:2].reshape(2, n // tm_disp, tm_disp).transpose(1, 0, 2)
    xs = _stage_dispatch(meta[0, :N_EXPERTS], meta[1, :N_EXPERTS], meta[2, 0:1], dest_d, hrow, p_rows, tm=tm_disp)
    ys = _stage_experts(blk[0, :nb], meta[2, 0:1], xs, w_gate, w_up, w_down)
    dest_c = dest[0:2].reshape(2, n // tm_comb, tm_comb).transpose(1, 0, 2)
    out = _stage_combine(dest_c, h2, wts, ln2_g[None, :], ln2_b[None, :], ys, tm=tm_comb)
    return out.reshape(B, T, D_MODEL)


def kernel(x, w_in, conv_w, a_log, dt_bias, dn_norm_w, sgu_ln_g, sgu_ln_b, w_spatial, b_spatial, w_out, ln1_g, ln1_b, w_router_group, b_router_group, w_router_expert, b_router_expert, w_gate, w_up, w_down, ln2_g, ln2_b):
    h = x
    for l in range(w_in.shape[0]):
        h = _layer(h, w_in[l], conv_w[l], a_log[l], dt_bias[l], dn_norm_w[l], sgu_ln_g[l], sgu_ln_b[l],
                   w_spatial[l], b_spatial[l], w_out[l], ln1_g[l], ln1_b[l],
                   w_router_group[l], b_router_group[l], w_router_expert[l], b_router_expert[l],
                   w_gate[l], w_up[l], w_down[l], ln2_g[l], ln2_b[l],
                   tm_in=512, dn_rows=4, dn_chunks=2, dn_groups=2, tm_mix=512, tm_rank=1024, tm_disp=256, tm_comb=256)
    return h
```

```python
import functools

import jax
import jax.numpy as jnp
from jax import lax
from jax.experimental import pallas as pl
from jax.experimental.pallas import tpu as pltpu

f32 = jnp.float32
bf16 = jnp.bfloat16
i32 = jnp.int32

D_MODEL = 1024
DN_WIDTH = 512
DN_HEADS = 4
HEAD_DIM = 128
CONV_K = 4
SGU_WIDTH = 512
SGU_GROUPS = 4
SGU_CHUNK = 128
DN_CHUNK = 128
MOE_GROUPS = 8
EXPERTS_PER_GROUP = 8
N_EXPERTS = 64
D_EXPERT = 512
MOE_BLOCK = 256
ROW_TILE = D_MODEL // 128
DEEPNORM_ALPHA = 2.0 ** 0.25
LN_EPS = 1e-5
RMS_EPS = 1e-6
HIGHEST = lax.Precision.HIGHEST
VMEM_LIMIT_BYTES = 56 * 1024 * 1024

NT_DIMS = (((1,), (1,)), ((), ()))


def _cparams(sem, flags=None):
    return pltpu.CompilerParams(dimension_semantics=sem, vmem_limit_bytes=VMEM_LIMIT_BYTES, flags=flags)


def _sigmoid(x):
    return 1.0 / (1.0 + jnp.exp(-x))


def _silu(x):
    h = 0.5 * x
    return h + h * jnp.tanh(h)


def _softplus(x):
    return jnp.maximum(x, 0.0) + jnp.log1p(jnp.exp(-jnp.abs(x)))


def _gelu_tanh(x):
    c = 0.7978845608028654
    return x * (0.5 * (1.0 + jnp.tanh(c * (x + 0.044715 * (x * x * x)))))


def _iota2(shape, axis):
    return lax.broadcasted_iota(i32, shape, axis)


def _inproj_body(x_ref, w_ref, wbat_ref, convw_ref, pcol_ref, prow_ref, lng_ref, lnb_ref, ones_ref,
                 q_ref, k_ref, v_ref, z_ref, u_ref, vln_ref, gcol_ref, grow_ref, eq_ref, ek_ref, ev_ref, *, tm):
    W = DN_WIDTH
    ext_refs = (eq_ref, ek_ref, ev_ref)

    @pl.when(pl.program_id(1) == 0)
    def _():
        for e_ref in ext_refs:
            e_ref[0:8, :] = jnp.zeros((8, W), f32)

    xb = x_ref[0].astype(bf16)
    for part, e_ref in enumerate(ext_refs):
        e_ref[8:8 + tm, :] = jnp.dot(xb, w_ref[:, part * W:(part + 1) * W], preferred_element_type=f32)
    zc = 3 * W
    uc = zc + W
    vc = uc + SGU_WIDTH
    bc = vc + SGU_WIDTH
    pz = jnp.dot(xb, w_ref[:, zc:zc + W], preferred_element_type=f32)
    pu = jnp.dot(xb, w_ref[:, uc:uc + SGU_WIDTH], preferred_element_type=f32)
    pv = jnp.dot(xb, w_ref[:, vc:vc + SGU_WIDTH], preferred_element_type=f32)
    pba = jnp.dot(xb, w_ref[:, bc:bc + 128], preferred_element_type=f32)
    pbat = lax.dot_general(wbat_ref[...], xb, NT_DIMS, preferred_element_type=f32)

    def group_sums(a):
        return jnp.dot(a.astype(bf16), ones_ref[...], preferred_element_type=f32)

    for part, (e_ref, out_ref) in enumerate(zip(ext_refs, (q_ref, k_ref, v_ref))):
        cs = slice(part * W, (part + 1) * W)
        y = convw_ref[3:4, cs] * e_ref[8:8 + tm, :]
        for j in range(CONV_K - 1):
            y = y + convw_ref[j:j + 1, cs] * e_ref[5 + j:5 + j + tm, :]
        y = _silu(y)
        if part < 2:
            scale = HEAD_DIM ** -0.5 if part == 0 else 1.0
            y = y * (lax.rsqrt(group_sums(y * y) + RMS_EPS) * scale)
        out_ref[0] = y.astype(bf16)
        e_ref[0:8, :] = e_ref[tm:tm + 8, :]

    z_ref[0] = _silu(pz).astype(bf16)

    u_ref[0] = _gelu_tanh(pu).astype(bf16)
    pv = _gelu_tanh(pv)
    for g in range(SGU_GROUPS):
        sl = slice(g * SGU_CHUNK, (g + 1) * SGU_CHUNK)
        vg = pv[:, sl]
        mu = jnp.mean(vg, axis=-1, keepdims=True)
        vcn = vg - mu
        var = jnp.mean(vcn * vcn, axis=-1, keepdims=True)
        vln_ref[0, :, sl] = (vcn * lax.rsqrt(var + LN_EPS) * lng_ref[:, sl] + lnb_ref[:, sl]).astype(bf16)

    lane = _iota2((DN_CHUNK, 128), 1)
    is_g = (lane >= DN_HEADS) & (lane < 2 * DN_HEADS)
    beta = _sigmoid(pba)
    gval = -jnp.exp(pcol_ref[0:1, :]) * _softplus(pba + pcol_ref[1:2, :])
    ci = _iota2((DN_CHUNK, DN_CHUNK), 0)
    cj = _iota2((DN_CHUNK, DN_CHUNK), 1)
    ltri = (ci >= cj).astype(f32)
    utri = (ci <= cj).astype(f32)
    sub = _iota2((8, DN_CHUNK), 0)
    for c in range(tm // DN_CHUNK):
        rs = slice(c * DN_CHUNK, (c + 1) * DN_CHUNK)
        gc = jnp.dot(ltri, jnp.where(is_g, gval[rs], 0.0), precision=HIGHEST, preferred_element_type=f32)
        gcol_ref[0, rs, :] = jnp.where(lane < DN_HEADS, beta[rs], gc)
        gt = -jnp.exp(prow_ref[0]) * _softplus(pbat[:, rs] + prow_ref[1])
        gt = jnp.where(sub >= DN_HEADS, gt, 0.0)
        grow_ref[0, :, rs] = jnp.dot(gt, utri, precision=HIGHEST, preferred_element_type=f32)


def _stage_inproj(x, w_re, wbat, conv_w, pcol, prow, lng, lnb, *, tm):
    B, T, _ = x.shape
    wcols = w_re.shape[1]
    grid = (B, T // tm)
    gi = lax.broadcasted_iota(i32, (DN_WIDTH, DN_WIDTH), 0) // 128
    gj = lax.broadcasted_iota(i32, (DN_WIDTH, DN_WIDTH), 1) // 128
    group_ones = (gi == gj).astype(bf16)
    act = lambda: jax.ShapeDtypeStruct((B, T, DN_WIDTH), bf16)
    act_spec = lambda: pl.BlockSpec((1, tm, DN_WIDTH), lambda b, t: (b, t, 0))
    const2 = lambda shp: pl.BlockSpec(shp, lambda b, t: (0, 0))
    return pl.pallas_call(
        functools.partial(_inproj_body, tm=tm),
        grid=grid,
        in_specs=[
            pl.BlockSpec((1, tm, D_MODEL), lambda b, t: (b, t, 0)),
            const2((D_MODEL, wcols)),
            const2((8, D_MODEL)),
            const2((CONV_K, 3 * DN_WIDTH)),
            const2((8, 128)),
            pl.BlockSpec((2, 8, 128), lambda b, t: (0, 0, 0)),
            const2((1, SGU_WIDTH)),
            const2((1, SGU_WIDTH)),
            const2((DN_WIDTH, DN_WIDTH)),
        ],
        out_specs=[act_spec() for _ in range(6)] + [
            pl.BlockSpec((1, tm, 128), lambda b, t: (b, t, 0)),
            pl.BlockSpec((1, 8, tm), lambda b, t: (b, 0, t)),
        ],
        out_shape=[act() for _ in range(6)] + [
            jax.ShapeDtypeStruct((B, T, 128), f32),
            jax.ShapeDtypeStruct((B, 8, T), f32),
        ],
        scratch_shapes=[pltpu.VMEM((tm + 8, DN_WIDTH), f32) for _ in range(3)],
        compiler_params=_cparams(("arbitrary", "arbitrary")),
        name="inproj",
    )(x, w_re, wbat, conv_w, pcol, prow, lng, lnb, group_ones)


def _mm(a, b):
    return jnp.dot(a.astype(bf16), b.astype(bf16), preferred_element_type=f32)


def _unit_lower_inverse(nmats, ii, jj):
    n = nmats[0].shape[0]
    eye = (ii == jj).astype(f32)
    leaf = jnp.right_shift(ii, 3) == jnp.right_shift(jj, 3)
    dblk = [jnp.where(leaf, m, 0.0) for m in nmats]
    s1 = [_mm(d, d) for d in dblk]
    r1 = [eye - d for d in dblk]
    both = [_mm(s, jnp.concatenate([s, r], axis=1)) for s, r in zip(s1, r1)]
    r2 = [r + bo[:, n:] for r, bo in zip(r1, both)]
    xs = [r + _mm(bo[:, :n], r) for r, bo in zip(r2, both)]
    size = 8
    while size < n:
        lows = [slice(r + size, r + 2 * size) for r in range(0, n, 2 * size)]
        ups = [slice(r, r + size) for r in range(0, n, 2 * size)]
        rsel = _iota2((n // 2, n), 0)
        ilow = rsel + size * (jnp.right_shift(rsel, size.bit_length() - 1) + 1)
        jlow = _iota2((n // 2, n), 1)
        in_pair_upper = (jlow >= ilow - (ilow & (2 * size - 1))) & (jlow < ilow - (ilow & (size - 1)))
        zeros = jnp.zeros((size, n), f32)
        new_xs = []
        ylows = [_mm(jnp.where(in_pair_upper, jnp.concatenate([m[s] for s in lows], axis=0), 0.0), x)
                 for m, x in zip(nmats, xs)]
        yfull = [jnp.concatenate([piece for k in range(len(lows)) for piece in (zeros, y[k * size:(k + 1) * size])], axis=0)
                 for y in ylows]
        corr = [_mm(jnp.concatenate([x[s] for s in lows], axis=0), yf) for x, yf in zip(xs, yfull)]
        for x, c in zip(xs, corr):
            pieces = []
            for k, (u, l) in enumerate(zip(ups, lows)):
                pieces += [x[u], x[l] - c[k * size:(k + 1) * size]]
            new_xs.append(jnp.concatenate(pieces, axis=0))
        xs = new_xs
        size *= 2
    return xs


def _deltanet_body(q_ref, k_ref, v_ref, z_ref, gcol_ref, grow_ref, nw_ref, y_ref, s_ref, *, nbr, nch, ngroups):
    @pl.when(pl.program_id(1) == 0)
    def _():
        s_ref[...] = jnp.zeros(s_ref.shape, f32)

    per = nbr // ngroups
    for g in range(ngroups):
        _deltanet_rows(q_ref, k_ref, v_ref, z_ref, gcol_ref, grow_ref, nw_ref, y_ref, s_ref,
                       rows=range(g * per, (g + 1) * per), nch=nch)


def _deltanet_rows(q_ref, k_ref, v_ref, z_ref, gcol_ref, grow_ref, nw_ref, y_ref, s_ref, *, rows, nch):
    C = DN_CHUNK
    S = [(b, h) for b in rows for h in range(DN_HEADS)]
    P = [(b, c, h) for c in range(nch) for b, h in S]

    ii = _iota2((C, C), 0)
    jj = _iota2((C, C), 1)
    causal = ii >= jj
    rs = [slice(c * C, (c + 1) * C) for c in range(nch)]
    hs = [slice(h * HEAD_DIM, (h + 1) * HEAD_DIM) for h in range(DN_HEADS)]
    gcol = {(b, c): gcol_ref[b, rs[c], :] for b in rows for c in range(nch)}
    qh = {(b, c, h): q_ref[b, rs[c], hs[h]] for b, c, h in P}
    kh = {(b, c, h): k_ref[b, rs[c], hs[h]] for b, c, h in P}
    vh = {(b, c, h): v_ref[b, rs[c], hs[h]] for b, c, h in P}
    gc_b = {(b, c, h): jnp.broadcast_to(gcol[b, c][:, DN_HEADS + h:DN_HEADS + h + 1], (C, HEAD_DIM)) for b, c, h in P}
    beta_b = {(b, c, h): jnp.broadcast_to(gcol[b, c][:, h:h + 1], (C, HEAD_DIM)) for b, c, h in P}
    gc_r = {(b, c, h): jnp.broadcast_to(grow_ref[b, DN_HEADS + h:DN_HEADS + h + 1, rs[c]], (C, C)) for b, c, h in P}
    decay = {p: jnp.exp(jnp.where(causal, gc_b[p] - gc_r[p], -1e30)) for p in P}

    kf = {p: kh[p].astype(f32) for p in P}
    kb = {p: kf[p] * beta_b[p] for p in P}
    kk = {p: lax.dot_general(kb[p].astype(bf16), kh[p], NT_DIMS, preferred_element_type=f32) for p in P}
    a_intra = {p: lax.dot_general(qh[p], kh[p], NT_DIMS, preferred_element_type=f32) * decay[p] for p in P}
    nmat = [jnp.where(ii > jj, kk[p] * decay[p], 0.0) for p in P]
    tinv = dict(zip(P, _unit_lower_inverse(nmat, ii, jj)))

    eg = {p: jnp.exp(gc_b[p]) for p in P}
    rhs = {p: jnp.concatenate([vh[p].astype(f32) * beta_b[p], kb[p] * eg[p]], axis=1) for p in P}
    sol = {p: _mm(tinv[p], rhs[p]) for p in P}

    q_dec = {p: qh[p].astype(f32) * eg[p] for p in P}
    g_last = {p: gc_b[p][C - 1:C, :] for p in P}
    kdt = {p: (kf[p] * jnp.exp(g_last[p] - gc_b[p])).T for p in P}

    state = {(b, h): s_ref[b * DN_HEADS + h] for b, h in S}
    for c in range(nch):
        m1 = {(b, h): _mm(jnp.concatenate([sol[b, c, h][:, HEAD_DIM:], q_dec[b, c, h]], axis=0), state[b, h]) for b, h in S}
        v_new = {(b, h): sol[b, c, h][:, :HEAD_DIM] - m1[b, h][:C] for b, h in S}
        m2 = {(b, h): _mm(jnp.concatenate([a_intra[b, c, h], kdt[b, c, h]], axis=0), v_new[b, h]) for b, h in S}
        state = {(b, h): state[b, h] * jnp.exp(g_last[b, c, h]) + m2[b, h][C:] for b, h in S}
        for b, h in S:
            o = m1[b, h][C:] + m2[b, h][:C]
            rms = lax.rsqrt(jnp.mean(o * o, axis=-1, keepdims=True) + RMS_EPS)
            y_ref[b, rs[c], hs[h]] = (o * rms * nw_ref[...] * z_ref[b, rs[c], hs[h]].astype(f32)).astype(bf16)
    for b, h in S:
        s_ref[b * DN_HEADS + h] = state[b, h]


def _stage_deltanet(q, k, v, z, gcol, grow, norm_w, *, nbr, nch, ngroups):
    B, T, _ = q.shape
    tt = nch * DN_CHUNK
    act_spec = lambda: pl.BlockSpec((nbr, tt, DN_WIDTH), lambda b, t: (b, t, 0))
    return pl.pallas_call(
        functools.partial(_deltanet_body, nbr=nbr, nch=nch, ngroups=ngroups),
        grid=(B // nbr, T // tt),
        in_specs=[act_spec(), act_spec(), act_spec(), act_spec(),
                  pl.BlockSpec((nbr, tt, 128), lambda b, t: (b, t, 0)),
                  pl.BlockSpec((nbr, 8, tt), lambda b, t: (b, 0, t)),
                  pl.BlockSpec((1, HEAD_DIM), lambda b, t: (0, 0))],
        out_specs=act_spec(),
        out_shape=jax.ShapeDtypeStruct((B, T, DN_WIDTH), bf16),
        scratch_shapes=[pltpu.VMEM((nbr * DN_HEADS, HEAD_DIM, HEAD_DIM), f32)],
        compiler_params=_cparams(("arbitrary", "arbitrary")),
        name="deltanet",
    )(q, k, v, z, gcol, grow, norm_w)


def _mixout_body(ydn_ref, u_ref, vln_ref, x_ref, ws_ref, bsp_ref, wout_ref, g1_ref, b1_ref, wrt_ref, brt_ref,
                 h_ref, hrow_ref, ids_ref, wts_ref, ycat_ref, *, tm):
    C = SGU_CHUNK
    ii = _iota2((C, C), 0)
    jj = _iota2((C, C), 1)
    ycat_ref[:, 0:DN_WIDTH] = ydn_ref[0]
    for g in range(SGU_GROUPS):
        gs = slice(g * C, (g + 1) * C)
        wsg = jnp.where(ii >= jj, ws_ref[g], 0.0).astype(bf16)
        for c in range(tm // C):
            rs = slice(c * C, (c + 1) * C)
            mixed = jnp.dot(wsg, vln_ref[0, rs, gs], preferred_element_type=f32) + bsp_ref[:, gs]
            ycat_ref[rs, DN_WIDTH + g * C:DN_WIDTH + (g + 1) * C] = (u_ref[0, rs, gs].astype(f32) * mixed).astype(bf16)

    RB = 128
    blocks = [slice(r, r + RB) for r in range(0, tm, RB)]
    mix = [jnp.dot(ycat_ref[rb, :], wout_ref[...], preferred_element_type=f32) for rb in blocks]
    h1s = []
    for rb, m in zip(blocks, mix):
        hp = DEEPNORM_ALPHA * x_ref[0, rb, :] + m
        mu = jnp.mean(hp, axis=-1, keepdims=True)
        hc = hp - mu
        var = jnp.mean(hc * hc, axis=-1, keepdims=True)
        h1 = hc * lax.rsqrt(var + LN_EPS) * g1_ref[...] + b1_ref[...]
        h_ref[0, rb, :] = h1
        h1b = h1.astype(bf16)
        hrow_ref[rb] = h1b.reshape(RB, ROW_TILE, 128)
        h1s.append(h1b)

    logit_blocks = [lax.dot_general(wrt_ref[...], hb, NT_DIMS, preferred_element_type=f32) + brt_ref[...] for hb in h1s]
    sub = _iota2((8, RB), 0)
    subf = sub.astype(f32)
    for rb, logits in zip(blocks, logit_blocks):
        gl = logits[0:8]
        gmax = jnp.max(gl, axis=0, keepdims=True)
        g_idx = jnp.min(jnp.where(gl == gmax, subf, float(MOE_GROUPS)), axis=0, keepdims=True)
        p_group = 1.0 / jnp.sum(jnp.exp(gl - gmax), axis=0, keepdims=True)
        within = jnp.zeros((8, RB), f32)
        for g in range(MOE_GROUPS):
            within = within + jnp.where(g_idx == float(g), logits[8 + 8 * g:16 + 8 * g], 0.0)
        m1 = jnp.max(within, axis=0, keepdims=True)
        i1 = jnp.min(jnp.where(within == m1, subf, float(EXPERTS_PER_GROUP)), axis=0, keepdims=True)
        rest = jnp.where(subf == i1, -jnp.inf, within)
        m2 = jnp.max(rest, axis=0, keepdims=True)
        i2 = jnp.min(jnp.where(rest == m2, subf, float(EXPERTS_PER_GROUP)), axis=0, keepdims=True)
        e = jnp.exp(m2 - m1)
        w1 = p_group / (1.0 + e)
        w2 = p_group * e / (1.0 + e)
        e1 = g_idx * float(EXPERTS_PER_GROUP) + i1
        e2 = g_idx * float(EXPERTS_PER_GROUP) + i2
        ids_ref[:, rb] = jnp.where(sub == 0, e1, jnp.where(sub == 1, e2, 0.0)).astype(i32)
        wts_ref[:, rb] = jnp.where(sub == 0, w1, jnp.where(sub == 1, w2, 0.0))


def _stage_mixout(ydn, u, vln, x, ws, bsp, wout, g1, b1, wrt, brt, *, tm):
    B, T, _ = x.shape
    nt = T // tm
    act_spec = lambda: pl.BlockSpec((1, tm, DN_WIDTH), lambda b, t: (b, t, 0))
    const2 = lambda shp: pl.BlockSpec(shp, lambda b, t: (0, 0))
    tok_spec = lambda: pl.BlockSpec((8, tm), lambda b, t: (0, b * nt + t))
    return pl.pallas_call(
        functools.partial(_mixout_body, tm=tm),
        grid=(B, nt),
        in_specs=[act_spec(), act_spec(), act_spec(),
                  pl.BlockSpec((1, tm, D_MODEL), lambda b, t: (b, t, 0)),
                  pl.BlockSpec((SGU_GROUPS, SGU_CHUNK, SGU_CHUNK), lambda b, t: (0, 0, 0)),
                  const2((SGU_CHUNK, SGU_WIDTH)),
                  const2((D_MODEL, D_MODEL)),
                  const2((1, D_MODEL)), const2((1, D_MODEL)),
                  const2((128, D_MODEL)), const2((128, 128))],
        out_specs=[pl.BlockSpec((1, tm, D_MODEL), lambda b, t: (b, t, 0)),
                   pl.BlockSpec((tm, ROW_TILE, 128), lambda b, t: (b * nt + t, 0, 0)), tok_spec(), tok_spec()],
        out_shape=[jax.ShapeDtypeStruct((B, T, D_MODEL), f32),
                   jax.ShapeDtypeStruct((B * T, ROW_TILE, 128), bf16),
                   jax.ShapeDtypeStruct((8, B * T), i32),
                   jax.ShapeDtypeStruct((8, B * T), f32)],
        scratch_shapes=[pltpu.VMEM((tm, D_MODEL), bf16)],
        compiler_params=_cparams(("arbitrary", "arbitrary")),
        name="mixout",
    )(ydn, u, vln, x, ws, bsp, wout, g1, b1, wrt, brt)


def _rank_body(ids_ref, rank_ref, cnt_ref, base_ref, *, tm):
    @pl.when(pl.program_id(0) == 0)
    def _():
        base_ref[...] = jnp.zeros(base_ref.shape, f32)

    sub = _iota2((N_EXPERTS, tm), 0)
    oh1 = (sub == ids_ref[0:1, :]).astype(f32)
    oh2 = (sub == ids_ref[1:2, :]).astype(f32)
    oh = (oh1 + oh2).astype(bf16)
    ti = _iota2((tm, tm), 0)
    tj = _iota2((tm, tm), 1)
    before = (ti < tj).astype(bf16)
    prefix = jnp.dot(oh, before, preferred_element_type=f32)
    tot = prefix + jnp.concatenate([base_ref[...]] * (tm // 128), axis=1)
    r1 = jnp.sum(oh1 * tot, axis=0, keepdims=True)
    r2 = jnp.sum(oh2 * tot, axis=0, keepdims=True)
    sub8 = _iota2((8, tm), 0)
    rank_ref[...] = jnp.where(sub8 == 0, r1, jnp.where(sub8 == 1, r2, 0.0)).astype(i32)
    base_ref[...] = base_ref[...] + jnp.dot(oh, jnp.ones((tm, 128), bf16), preferred_element_type=f32)
    cnt_ref[...] = base_ref[...]


def _stage_rank(ids, *, tm):
    n = ids.shape[1]
    return pl.pallas_call(
        functools.partial(_rank_body, tm=tm),
        grid=(n // tm,),
        in_specs=[pl.BlockSpec((8, tm), lambda i: (0, i))],
        out_specs=[pl.BlockSpec((8, tm), lambda i: (0, i)),
                   pl.BlockSpec((N_EXPERTS, 128), lambda i: (0, 0))],
        out_shape=[jax.ShapeDtypeStruct((8, n), i32), jax.ShapeDtypeStruct((N_EXPERTS, 128), f32)],
        scratch_shapes=[pltpu.VMEM((N_EXPERTS, 128), f32)],
        compiler_params=_cparams(("arbitrary",)),
        name="moe_rank",
    )(ids)


def _dest_body(cnt_ref, ids_ref, rank_ref, dest_ref, meta_ref, blk_ref, *, tm, nb_pad):
    cnt = cnt_ref[...]
    padded = jnp.floor((cnt + (MOE_BLOCK - 1)) * (1.0 / MOE_BLOCK)) * MOE_BLOCK
    ei = _iota2((N_EXPERTS, N_EXPERTS), 0)
    ej = _iota2((N_EXPERTS, N_EXPERTS), 1)
    pends = jnp.dot((ei >= ej).astype(f32), padded, precision=HIGHEST, preferred_element_type=f32)
    pstart = pends - padded
    sub = _iota2((N_EXPERTS, tm), 0)
    pst = jnp.concatenate([pstart] * (tm // 128), axis=1)
    d1 = jnp.sum(jnp.where(sub == ids_ref[0:1, :], pst, 0.0), axis=0, keepdims=True)
    d2 = jnp.sum(jnp.where(sub == ids_ref[1:2, :], pst, 0.0), axis=0, keepdims=True)
    sub8 = _iota2((8, tm), 0)
    dest_ref[...] = jnp.where(sub8 == 0, d1, jnp.where(sub8 == 1, d2, 0.0)).astype(i32) + rank_ref[...]

    @pl.when(pl.program_id(0) == 0)
    def _():
        s64 = _iota2((N_EXPERTS, 128), 0)
        l64 = _iota2((N_EXPERTS, 128), 1)
        diag = s64 == l64
        fill_off = jnp.sum(jnp.where(diag, pstart + cnt, 0.0), axis=0, keepdims=True)
        fill_n = jnp.sum(jnp.where(diag, padded - cnt, 0.0), axis=0, keepdims=True)
        nused = pends[N_EXPERTS - 1:N_EXPERTS, :] * (1.0 / MOE_BLOCK)
        m8 = _iota2((8, 128), 0)
        meta_ref[...] = jnp.where(m8 == 0, fill_off, jnp.where(m8 == 1, fill_n, jnp.where(m8 == 2, nused, 0.0))).astype(i32)
        bstart = (_iota2((N_EXPERTS, nb_pad), 1) * MOE_BLOCK).astype(f32)
        pe = jnp.concatenate([pends] * (nb_pad // 128), axis=1)
        be = jnp.sum((pe <= bstart).astype(f32), axis=0, keepdims=True)
        be = jnp.minimum(be, float(N_EXPERTS - 1))
        blk_ref[...] = jnp.broadcast_to(be, (8, nb_pad)).astype(i32)


def _stage_dest(cnt, ids, rank, *, tm, nb_pad):
    n = ids.shape[1]
    tok = lambda: pl.BlockSpec((8, tm), lambda i: (0, i))
    return pl.pallas_call(
        functools.partial(_dest_body, tm=tm, nb_pad=nb_pad),
        grid=(n // tm,),
        in_specs=[pl.BlockSpec((N_EXPERTS, 128), lambda i: (0, 0)), tok(), tok()],
        out_specs=[tok(), pl.BlockSpec((8, 128), lambda i: (0, 0)), pl.BlockSpec((8, nb_pad), lambda i: (0, 0))],
        out_shape=[jax.ShapeDtypeStruct((8, n), i32), jax.ShapeDtypeStruct((8, 128), i32),
                   jax.ShapeDtypeStruct((8, nb_pad), i32)],
        compiler_params=_cparams(("arbitrary",)),
        name="moe_dest",
    )(cnt, ids, rank)


def _dispatch_body(fill_off_ref, fill_n_ref, nused_ref, dest_ref, h3_ref, xs_ref, zero_ref, sem, zsem, *, tm):
    def row_copy(t, d):
        return pltpu.make_async_copy(h3_ref.at[t], xs_ref.at[d], sem)

    def issue(t, carry):
        row_copy(t, dest_ref[0, 0, t]).start(priority=0)
        row_copy(t, dest_ref[0, 1, t]).start(priority=1)
        return carry

    lax.fori_loop(0, tm, issue, 0, unroll=8)

    @pl.when(pl.program_id(0) == 0)
    def _():
        zero_ref[...] = jnp.zeros(zero_ref.shape, bf16)

        def fill(start):
            def body(e, carry):
                off = fill_off_ref[e]
                npad = fill_n_ref[e]
                bit = MOE_BLOCK // 2
                while bit:
                    @pl.when((npad & bit) != 0)
                    def _(off=off, bit=bit):
                        cp = pltpu.make_async_copy(zero_ref.at[pl.ds(0, bit)], xs_ref.at[pl.ds(off, bit)], zsem)
                        cp.start() if start else cp.wait()
                    off = off + (npad & bit)
                    bit //= 2
                return carry
            return body

        lax.fori_loop(0, N_EXPERTS, fill(True), 0)
        lax.fori_loop(0, N_EXPERTS, fill(False), 0)

        def tail_copy(b):
            return pltpu.make_async_copy(zero_ref, xs_ref.at[pl.ds(b * MOE_BLOCK, MOE_BLOCK)], zsem)

        nblocks = xs_ref.shape[0] // MOE_BLOCK
        lax.fori_loop(nused_ref[0], nblocks, lambda b, c: (tail_copy(b).start(), c)[1], 0)
        lax.fori_loop(nused_ref[0], nblocks, lambda b, c: (tail_copy(0).wait(), c)[1], 0)

    for _ in range(2):
        pltpu.make_async_copy(h3_ref, xs_ref.at[pl.ds(0, tm)], sem).wait()


def _stage_dispatch(fill_off, fill_n, nused, dest3, hrow, p_rows, *, tm):
    n = hrow.shape[0]
    return pl.pallas_call(
        functools.partial(_dispatch_body, tm=tm),
        grid_spec=pltpu.PrefetchScalarGridSpec(
            num_scalar_prefetch=3,
            grid=(n // tm,),
            in_specs=[pl.BlockSpec((1, 2, tm), lambda i, fo, fn, nu: (i, 0, 0), memory_space=pltpu.SMEM),
                      pl.BlockSpec((tm, ROW_TILE, 128), lambda i, fo, fn, nu: (i, 0, 0))],
            out_specs=pl.BlockSpec(memory_space=pl.ANY),
            scratch_shapes=[pltpu.VMEM((MOE_BLOCK, ROW_TILE, 128), bf16),
                            pltpu.SemaphoreType.DMA, pltpu.SemaphoreType.DMA],
        ),
        out_shape=jax.ShapeDtypeStruct((p_rows, ROW_TILE, 128), bf16),
        compiler_params=_cparams(("arbitrary",)),
        name="moe_dispatch",
    )(fill_off, fill_n, nused, dest3, hrow)


def _experts_body(blk_ref, nused_ref, xs_ref, wg_hbm, wu_hbm, wd_hbm, ys_ref,
                  wg32_ref, wu32_ref, wd32_ref, wgu16_ref, wd16_ref, wsem):
    i = pl.program_id(0)
    nused = nused_ref[0]
    used = i < nused
    e = blk_ref[i]

    def weight_copies(ex):
        return (pltpu.make_async_copy(wg_hbm.at[ex], wg32_ref, wsem.at[0]),
                pltpu.make_async_copy(wu_hbm.at[ex], wu32_ref, wsem.at[1]),
                pltpu.make_async_copy(wd_hbm.at[ex], wd32_ref, wsem.at[2]))

    @pl.when((i == 0) & used)
    def _():
        for cp in weight_copies(e):
            cp.start()

    @pl.when(used & ((i == 0) | (e != blk_ref[jnp.maximum(i - 1, 0)])))
    def _():
        for cp in weight_copies(e):
            cp.wait()
        wgu16_ref[:, 0:D_EXPERT] = wg32_ref[...].astype(bf16)
        wgu16_ref[:, D_EXPERT:2 * D_EXPERT] = wu32_ref[...].astype(bf16)
        wd16_ref[...] = wd32_ref[...].astype(bf16)
        nxt = lax.while_loop(lambda j: (j < nused) & (blk_ref[jnp.minimum(j, nused - 1)] == e), lambda j: j + 1, i + 1)

        @pl.when(nxt < nused)
        def _():
            for cp in weight_copies(blk_ref[jnp.minimum(nxt, nused - 1)]):
                cp.start()

    @pl.when(used)
    def _():
        half = MOE_BLOCK // 2
        rows = [slice(p * half, (p + 1) * half) for p in range(2)]
        gu = [jnp.dot(xs_ref[r].reshape(half, D_MODEL), wgu16_ref[...], preferred_element_type=f32) for r in rows]
        hid = [(_silu(g[:, :D_EXPERT]) * g[:, D_EXPERT:]).astype(bf16) for g in gu]
        y = [jnp.dot(hd, wd16_ref[...], preferred_element_type=f32) for hd in hid]
        for r, yp in zip(rows, y):
            ys_ref[r] = yp.astype(bf16).reshape(half, ROW_TILE, 128)

    @pl.when(jnp.logical_not(used))
    def _():
        ys_ref[...] = jnp.zeros(ys_ref.shape, bf16)


def _stage_experts(blk_e, nused, xs, w_gate, w_up, w_down):
    p_rows = xs.shape[0]
    nb = p_rows // MOE_BLOCK

    def last_used(i, nu):
        return jnp.maximum(jnp.minimum(i, nu[0] - 1), 0)

    def x_map(i, blk, nu):
        return (last_used(i, nu), 0, 0)

    def row_map(i, blk, nu):
        return (i, 0, 0)

    return pl.pallas_call(
        _experts_body,
        grid_spec=pltpu.PrefetchScalarGridSpec(
            num_scalar_prefetch=2,
            grid=(nb,),
            in_specs=[pl.BlockSpec((MOE_BLOCK, ROW_TILE, 128), x_map),
                      pl.BlockSpec(memory_space=pl.ANY),
                      pl.BlockSpec(memory_space=pl.ANY),
                      pl.BlockSpec(memory_space=pl.ANY)],
            out_specs=pl.BlockSpec((MOE_BLOCK, ROW_TILE, 128), row_map),
            scratch_shapes=[pltpu.VMEM((D_MODEL, D_EXPERT), f32), pltpu.VMEM((D_MODEL, D_EXPERT), f32),
                            pltpu.VMEM((D_EXPERT, D_MODEL), f32),
                            pltpu.VMEM((D_MODEL, 2 * D_EXPERT), bf16), pltpu.VMEM((D_EXPERT, D_MODEL), bf16),
                            pltpu.SemaphoreType.DMA((3,))],
        ),
        out_shape=jax.ShapeDtypeStruct((p_rows, ROW_TILE, 128), bf16),
        compiler_params=_cparams(("arbitrary",)),
        name="moe_experts",
    )(blk_e, nused, xs, w_gate, w_up, w_down)


def _combine_body(dcur_ref, dnext_ref, h_ref, wts_ref, g2_ref, b2_ref, ys_ref, o_ref, ybuf_ref, sem, *, tm, nsteps):
    i = pl.program_id(0)
    slot = lax.rem(i, 2)

    def issue_tile(d_ref, s):
        def body(t, carry):
            for k in range(2):
                pltpu.make_async_copy(ys_ref.at[d_ref[0, k, t]], ybuf_ref.at[s, k, t], sem.at[s]).start(priority=k)
            return carry

        lax.fori_loop(0, tm, body, 0, unroll=8)

    @pl.when(i == 0)
    def _():
        issue_tile(dcur_ref, 0)

    @pl.when(i + 1 < nsteps)
    def _():
        issue_tile(dnext_ref, 1 - slot)

    for k in range(2):
        pltpu.make_async_copy(ys_ref.at[pl.ds(0, tm)], ybuf_ref.at[slot, k], sem.at[slot]).wait()

    pieces = []
    for c in range(tm // 128):
        ls = slice(c * 128, (c + 1) * 128)
        w1c = jnp.broadcast_to(wts_ref[0:1, ls], (128, 128)).T
        w2c = jnp.broadcast_to(wts_ref[1:2, ls], (128, 128)).T
        w1f = jnp.concatenate([w1c] * (D_MODEL // 128), axis=1)
        w2f = jnp.concatenate([w2c] * (D_MODEL // 128), axis=1)
        y1 = ybuf_ref[slot, 0, ls].reshape(128, D_MODEL).astype(f32)
        y2 = ybuf_ref[slot, 1, ls].reshape(128, D_MODEL).astype(f32)
        pieces.append(w1f * y1 + w2f * y2)
    ffn = jnp.concatenate(pieces, axis=0)
    hp = DEEPNORM_ALPHA * h_ref[...] + ffn
    mu = jnp.mean(hp, axis=-1, keepdims=True)
    hc = hp - mu
    var = jnp.mean(hc * hc, axis=-1, keepdims=True)
    o_ref[...] = hc * lax.rsqrt(var + LN_EPS) * g2_ref[...] + b2_ref[...]


def _stage_combine(dest3, h2, wts, g2, b2, ys, *, tm):
    n = h2.shape[0]
    nsteps = n // tm
    return pl.pallas_call(
        functools.partial(_combine_body, tm=tm, nsteps=nsteps),
        grid=(nsteps,),
        in_specs=[pl.BlockSpec((1, 2, tm), lambda i: (i, 0, 0), memory_space=pltpu.SMEM),
                  pl.BlockSpec((1, 2, tm), lambda i: (jnp.minimum(i + 1, nsteps - 1), 0, 0), memory_space=pltpu.SMEM),
                  pl.BlockSpec((tm, D_MODEL), lambda i: (i, 0)),
                  pl.BlockSpec((8, tm), lambda i: (0, i)),
                  pl.BlockSpec((1, D_MODEL), lambda i: (0, 0)),
                  pl.BlockSpec((1, D_MODEL), lambda i: (0, 0)),
                  pl.BlockSpec(memory_space=pl.ANY)],
        out_specs=pl.BlockSpec((tm, D_MODEL), lambda i: (i, 0)),
        out_shape=jax.ShapeDtypeStruct((n, D_MODEL), f32),
        scratch_shapes=[pltpu.VMEM((2, 2, tm, ROW_TILE, 128), bf16), pltpu.SemaphoreType.DMA((2,))],
        compiler_params=_cparams(("arbitrary",)),
        name="moe_combine",
    )(dest3, dest3, h2, wts, g2, b2, ys)


def _layer(h, w_in, conv_w, a_log, dt_bias, dn_norm_w, sgu_ln_g, sgu_ln_b, w_spatial, b_spatial, w_out,
           ln1_g, ln1_b, w_rg, b_rg, w_re, b_re, w_gate, w_up, w_down, ln2_g, ln2_b,
           *, tm_in, dn_rows, dn_chunks, dn_groups, tm_mix, tm_rank, tm_disp, tm_comb):
    B, T, _ = h.shape
    n = B * T
    qkvz = 4 * DN_WIDTH
    ba0 = qkvz
    uv0 = qkvz + 2 * DN_HEADS
    w_cols = jnp.concatenate(
        [w_in[:, :qkvz], w_in[:, uv0:], w_in[:, ba0:uv0], jnp.zeros((D_MODEL, 128 - 2 * DN_HEADS), f32)], axis=1).astype(bf16)
    wbat = w_in[:, ba0:uv0].T.astype(bf16)
    pcol = jnp.zeros((8, 128), f32).at[0, DN_HEADS:2 * DN_HEADS].set(a_log).at[1, DN_HEADS:2 * DN_HEADS].set(dt_bias)
    prow = jnp.zeros((2, 8, 128), f32)
    prow = prow.at[0, DN_HEADS:2 * DN_HEADS, :].set(jnp.broadcast_to(a_log[:, None], (DN_HEADS, 128)))
    prow = prow.at[1, DN_HEADS:2 * DN_HEADS, :].set(jnp.broadcast_to(dt_bias[:, None], (DN_HEADS, 128)))

    q, k, v, z, u, vln, gcol, grow = _stage_inproj(
        h, w_cols, wbat, conv_w, pcol, prow, sgu_ln_g[None, :], sgu_ln_b[None, :], tm=tm_in)
    ydn = _stage_deltanet(q, k, v, z, gcol, grow, dn_norm_w[None, :], nbr=dn_rows, nch=dn_chunks, ngroups=dn_groups)

    bsp = jnp.broadcast_to(b_spatial.T[:, :, None], (SGU_CHUNK, SGU_GROUPS, SGU_CHUNK)).reshape(SGU_CHUNK, SGU_WIDTH)
    wrt = jnp.zeros((128, D_MODEL), f32).at[0:MOE_GROUPS].set(w_rg.T).at[MOE_GROUPS:MOE_GROUPS + N_EXPERTS].set(w_re.T).astype(bf16)
    brt = jnp.zeros((128,), f32).at[0:MOE_GROUPS].set(b_rg).at[MOE_GROUPS:MOE_GROUPS + N_EXPERTS].set(b_re)
    brt = jnp.broadcast_to(brt[:, None], (128, 128))
    h1, hrow, ids, wts = _stage_mixout(ydn, u, vln, h, w_spatial, bsp, w_out.astype(bf16), ln1_g[None, :],
                                       ln1_b[None, :], wrt, brt, tm=tm_mix)

    p_rows = (-(-(n * 2) // MOE_BLOCK)) * MOE_BLOCK + N_EXPERTS * MOE_BLOCK
    nb = p_rows // MOE_BLOCK
    nb_pad = (-(-nb // 128)) * 128
    rank, cnt = _stage_rank(ids, tm=tm_rank)
    dest, meta, blk = _stage_dest(cnt, ids, rank, tm=tm_rank, nb_pad=nb_pad)

    h2 = h1.reshape(n, D_MODEL)
    dest_d = dest[0:2].reshape(2, n // tm_disp, tm_disp).transpose(1, 0, 2)
    xs = _stage_dispatch(meta[0, :N_EXPERTS], meta[1, :N_EXPERTS], meta[2, 0:1], dest_d, hrow, p_rows, tm=tm_disp)
    ys = _stage_experts(blk[0, :nb], meta[2, 0:1], xs, w_gate, w_up, w_down)
    dest_c = dest[0:2].reshape(2, n // tm_comb, tm_comb).transpose(1, 0, 2)
    out = _stage_combine(dest_c, h2, wts, ln2_g[None, :], ln2_b[None, :], ys, tm=tm_comb)
    return out.reshape(B, T, D_MODEL)


def kernel(x, w_in, conv_w, a_log, dt_bias, dn_norm_w, sgu_ln_g, sgu_ln_b, w_spatial, b_spatial, w_out, ln1_g, ln1_b, w_router_group, b_router_group, w_router_expert, b_router_expert, w_gate, w_up, w_down, ln2_g, ln2_b):
    h = x
    for l in range(w_in.shape[0]):
        h = _layer(h, w_in[l], conv_w[l], a_log[l], dt_bias[l], dn_norm_w[l], sgu_ln_g[l], sgu_ln_b[l],
                   w_spatial[l], b_spatial[l], w_out[l], ln1_g[l], ln1_b[l],
                   w_router_group[l], b_router_group[l], w_router_expert[l], b_router_expert[l],
                   w_gate[l], w_up[l], w_down[l], ln2_g[l], ln2_b[l],
                   tm_in=512, dn_rows=4, dn_chunks=2, dn_groups=2, tm_mix=512, tm_rank=1024, tm_disp=2048, tm_comb=256)
    return h
```

```python
import functools

import jax
import jax.numpy as jnp
from jax import lax
from jax.experimental import pallas as pl
from jax.experimental.pallas import tpu as pltpu

f32 = jnp.float32
bf16 = jnp.bfloat16
i32 = jnp.int32

D_MODEL = 1024
DN_WIDTH = 512
DN_HEADS = 4
HEAD_DIM = 128
CONV_K = 4
SGU_WIDTH = 512
SGU_GROUPS = 4
SGU_CHUNK = 128
DN_CHUNK = 128
MOE_GROUPS = 8
EXPERTS_PER_GROUP = 8
N_EXPERTS = 64
D_EXPERT = 512
MOE_BLOCK = 256
IN_COLS_ALIGNED = 4 * DN_WIDTH + 2 * SGU_WIDTH + 128
ROW_TILE = D_MODEL // 128
DEEPNORM_ALPHA = 2.0 ** 0.25
LN_EPS = 1e-5
RMS_EPS = 1e-6
HIGHEST = lax.Precision.HIGHEST
VMEM_LIMIT_BYTES = 56 * 1024 * 1024

NT_DIMS = (((1,), (1,)), ((), ()))


def _cparams(sem, flags=None):
    return pltpu.CompilerParams(dimension_semantics=sem, vmem_limit_bytes=VMEM_LIMIT_BYTES, flags=flags)


def _sigmoid(x):
    return 1.0 / (1.0 + jnp.exp(-x))


def _silu(x):
    h = 0.5 * x
    return h + h * jnp.tanh(h)


def _softplus(x):
    return jnp.maximum(x, 0.0) + jnp.log1p(jnp.exp(-jnp.abs(x)))


def _gelu_tanh(x):
    c = 0.7978845608028654
    return x * (0.5 * (1.0 + jnp.tanh(c * (x + 0.044715 * (x * x * x)))))


def _iota2(shape, axis):
    return lax.broadcasted_iota(i32, shape, axis)


def _inproj_body(x_ref, w_ref, wbat_ref, convw_ref, pcol_ref, prow_ref, lng_ref, lnb_ref, ones_ref,
                 q_ref, k_ref, v_ref, z_ref, u_ref, vln_ref, gcol_ref, grow_ref, *ext_refs, tm, ngroups):
    W = DN_WIDTH
    gm = tm // ngroups
    ext = [ext_refs[3 * g:3 * g + 3] for g in range(ngroups)]

    @pl.when(pl.program_id(1) == 0)
    def _():
        for e_ref in ext[0]:
            e_ref[0:8, :] = jnp.zeros((8, W), f32)

    for g in range(ngroups):
        _inproj_rows(x_ref, w_ref, wbat_ref, convw_ref, pcol_ref, prow_ref, lng_ref, lnb_ref, ones_ref,
                     q_ref, k_ref, v_ref, z_ref, u_ref, vln_ref, gcol_ref, grow_ref, ext[g],
                     ext[(g + 1) % ngroups], r0=g * gm, gm=gm)


def _inproj_rows(x_ref, w_ref, wbat_ref, convw_ref, pcol_ref, prow_ref, lng_ref, lnb_ref, ones_ref,
                 q_ref, k_ref, v_ref, z_ref, u_ref, vln_ref, gcol_ref, grow_ref, ext, ext_next, *, r0, gm):
    W = DN_WIDTH
    rows = slice(r0, r0 + gm)
    xb = x_ref[0, rows, :].astype(bf16)
    for part, e_ref in enumerate(ext):
        e_ref[8:8 + gm, :] = jnp.dot(xb, w_ref[:, part * W:(part + 1) * W], preferred_element_type=f32)
    zc = 3 * W
    uc = zc + W
    vc = uc + SGU_WIDTH
    bc = vc + SGU_WIDTH
    pz = jnp.dot(xb, w_ref[:, zc:zc + W], preferred_element_type=f32)
    pu = jnp.dot(xb, w_ref[:, uc:uc + SGU_WIDTH], preferred_element_type=f32)
    pv = jnp.dot(xb, w_ref[:, vc:vc + SGU_WIDTH], preferred_element_type=f32)
    pba = jnp.dot(xb, w_ref[:, bc:bc + 128], preferred_element_type=f32)
    pbat = lax.dot_general(wbat_ref[...], xb, NT_DIMS, preferred_element_type=f32)

    def group_sums(a):
        return jnp.dot(a.astype(bf16), ones_ref[...], preferred_element_type=f32)

    for part, (e_ref, n_ref, out_ref) in enumerate(zip(ext, ext_next, (q_ref, k_ref, v_ref))):
        cs = slice(part * W, (part + 1) * W)
        y = convw_ref[3:4, cs] * e_ref[8:8 + gm, :]
        for j in range(CONV_K - 1):
            y = y + convw_ref[j:j + 1, cs] * e_ref[5 + j:5 + j + gm, :]
        y = _silu(y)
        if part < 2:
            scale = HEAD_DIM ** -0.5 if part == 0 else 1.0
            y = y * (lax.rsqrt(group_sums(y * y) + RMS_EPS) * scale)
        out_ref[0, rows, :] = y.astype(bf16)
        n_ref[0:8, :] = e_ref[gm:gm + 8, :]

    z_ref[0, rows, :] = _silu(pz).astype(bf16)

    u_ref[0, rows, :] = _gelu_tanh(pu).astype(bf16)
    pv = _gelu_tanh(pv)
    for g in range(SGU_GROUPS):
        sl = slice(g * SGU_CHUNK, (g + 1) * SGU_CHUNK)
        vg = pv[:, sl]
        mu = jnp.mean(vg, axis=-1, keepdims=True)
        vcn = vg - mu
        var = jnp.mean(vcn * vcn, axis=-1, keepdims=True)
        vln_ref[0, rows, sl] = (vcn * lax.rsqrt(var + LN_EPS) * lng_ref[:, sl] + lnb_ref[:, sl]).astype(bf16)

    lane = _iota2((DN_CHUNK, 128), 1)
    is_g = (lane >= DN_HEADS) & (lane < 2 * DN_HEADS)
    beta = _sigmoid(pba)
    gval = -jnp.exp(pcol_ref[0:1, :]) * _softplus(pba + pcol_ref[1:2, :])
    ci = _iota2((DN_CHUNK, DN_CHUNK), 0)
    cj = _iota2((DN_CHUNK, DN_CHUNK), 1)
    ltri = (ci >= cj).astype(f32)
    utri = (ci <= cj).astype(f32)
    sub = _iota2((8, DN_CHUNK), 0)
    for c in range(gm // DN_CHUNK):
        rs = slice(c * DN_CHUNK, (c + 1) * DN_CHUNK)
        os_ = slice(r0 + c * DN_CHUNK, r0 + (c + 1) * DN_CHUNK)
        gc = jnp.dot(ltri, jnp.where(is_g, gval[rs], 0.0), precision=HIGHEST, preferred_element_type=f32)
        gcol_ref[0, os_, :] = jnp.where(lane < DN_HEADS, beta[rs], gc)
        gt = -jnp.exp(prow_ref[0]) * _softplus(pbat[:, rs] + prow_ref[1])
        gt = jnp.where(sub >= DN_HEADS, gt, 0.0)
        grow_ref[0, :, os_] = jnp.dot(gt, utri, precision=HIGHEST, preferred_element_type=f32)


def _stage_inproj(x, w_re, wbat, conv_w, pcol, prow, lng, lnb, *, tm, ngroups):
    B, T, _ = x.shape
    wcols = w_re.shape[1]
    grid = (B, T // tm)
    gi = lax.broadcasted_iota(i32, (DN_WIDTH, DN_WIDTH), 0) // 128
    gj = lax.broadcasted_iota(i32, (DN_WIDTH, DN_WIDTH), 1) // 128
    group_ones = (gi == gj).astype(bf16)
    act = lambda: jax.ShapeDtypeStruct((B, T, DN_WIDTH), bf16)
    act_spec = lambda: pl.BlockSpec((1, tm, DN_WIDTH), lambda b, t: (b, t, 0))
    const2 = lambda shp: pl.BlockSpec(shp, lambda b, t: (0, 0))
    return pl.pallas_call(
        functools.partial(_inproj_body, tm=tm, ngroups=ngroups),
        grid=grid,
        in_specs=[
            pl.BlockSpec((1, tm, D_MODEL), lambda b, t: (b, t, 0)),
            const2((D_MODEL, wcols)),
            const2((8, D_MODEL)),
            const2((CONV_K, 3 * DN_WIDTH)),
            const2((8, 128)),
            pl.BlockSpec((2, 8, 128), lambda b, t: (0, 0, 0)),
            const2((1, SGU_WIDTH)),
            const2((1, SGU_WIDTH)),
            const2((DN_WIDTH, DN_WIDTH)),
        ],
        out_specs=[act_spec() for _ in range(6)] + [
            pl.BlockSpec((1, tm, 128), lambda b, t: (b, t, 0)),
            pl.BlockSpec((1, 8, tm), lambda b, t: (b, 0, t)),
        ],
        out_shape=[act() for _ in range(6)] + [
            jax.ShapeDtypeStruct((B, T, 128), f32),
            jax.ShapeDtypeStruct((B, 8, T), f32),
        ],
        scratch_shapes=[pltpu.VMEM((tm // ngroups + 8, DN_WIDTH), f32) for _ in range(3 * ngroups)],
        compiler_params=_cparams(("arbitrary", "arbitrary")),
        name="inproj",
    )(x, w_re, wbat, conv_w, pcol, prow, lng, lnb, group_ones)


def _mm(a, b):
    return jnp.dot(a.astype(bf16), b.astype(bf16), preferred_element_type=f32)


def _unit_lower_inverse(nmats, ii, jj):
    n = nmats[0].shape[0]
    eye = (ii == jj).astype(f32)
    leaf = jnp.right_shift(ii, 3) == jnp.right_shift(jj, 3)
    dblk = [jnp.where(leaf, m, 0.0) for m in nmats]
    s1 = [_mm(d, d) for d in dblk]
    r1 = [eye - d for d in dblk]
    both = [_mm(s, jnp.concatenate([s, r], axis=1)) for s, r in zip(s1, r1)]
    r2 = [r + bo[:, n:] for r, bo in zip(r1, both)]
    xs = [r + _mm(bo[:, :n], r) for r, bo in zip(r2, both)]
    size = 8
    while size < n:
        lows = [slice(r + size, r + 2 * size) for r in range(0, n, 2 * size)]
        ups = [slice(r, r + size) for r in range(0, n, 2 * size)]
        rsel = _iota2((n // 2, n), 0)
        ilow = rsel + size * (jnp.right_shift(rsel, size.bit_length() - 1) + 1)
        jlow = _iota2((n // 2, n), 1)
        in_pair_upper = (jlow >= ilow - (ilow & (2 * size - 1))) & (jlow < ilow - (ilow & (size - 1)))
        zeros = jnp.zeros((size, n), f32)
        new_xs = []
        ylows = [_mm(jnp.where(in_pair_upper, jnp.concatenate([m[s] for s in lows], axis=0), 0.0), x)
                 for m, x in zip(nmats, xs)]
        yfull = [jnp.concatenate([piece for k in range(len(lows)) for piece in (zeros, y[k * size:(k + 1) * size])], axis=0)
                 for y in ylows]
        corr = [_mm(jnp.concatenate([x[s] for s in lows], axis=0), yf) for x, yf in zip(xs, yfull)]
        for x, c in zip(xs, corr):
            pieces = []
            for k, (u, l) in enumerate(zip(ups, lows)):
                pieces += [x[u], x[l] - c[k * size:(k + 1) * size]]
            new_xs.append(jnp.concatenate(pieces, axis=0))
        xs = new_xs
        size *= 2
    return xs


def _deltanet_body(q_ref, k_ref, v_ref, z_ref, gcol_ref, grow_ref, nw_ref, y_ref, s_ref, *, nbr, nch, ngroups):
    @pl.when(pl.program_id(1) == 0)
    def _():
        s_ref[...] = jnp.zeros(s_ref.shape, f32)

    per = nbr // ngroups
    for g in range(ngroups):
        _deltanet_rows(q_ref, k_ref, v_ref, z_ref, gcol_ref, grow_ref, nw_ref, y_ref, s_ref,
                       rows=range(g * per, (g + 1) * per), nch=nch)


def _deltanet_rows(q_ref, k_ref, v_ref, z_ref, gcol_ref, grow_ref, nw_ref, y_ref, s_ref, *, rows, nch):
    C = DN_CHUNK
    S = [(b, h) for b in rows for h in range(DN_HEADS)]
    P = [(b, c, h) for c in range(nch) for b, h in S]

    ii = _iota2((C, C), 0)
    jj = _iota2((C, C), 1)
    causal = ii >= jj
    rs = [slice(c * C, (c + 1) * C) for c in range(nch)]
    hs = [slice(h * HEAD_DIM, (h + 1) * HEAD_DIM) for h in range(DN_HEADS)]
    gcol = {(b, c): gcol_ref[b, rs[c], :] for b in rows for c in range(nch)}
    qh = {(b, c, h): q_ref[b, rs[c], hs[h]] for b, c, h in P}
    kh = {(b, c, h): k_ref[b, rs[c], hs[h]] for b, c, h in P}
    vh = {(b, c, h): v_ref[b, rs[c], hs[h]] for b, c, h in P}
    gc_b = {(b, c, h): jnp.broadcast_to(gcol[b, c][:, DN_HEADS + h:DN_HEADS + h + 1], (C, HEAD_DIM)) for b, c, h in P}
    beta_b = {(b, c, h): jnp.broadcast_to(gcol[b, c][:, h:h + 1], (C, HEAD_DIM)) for b, c, h in P}
    gc_r = {(b, c, h): jnp.broadcast_to(grow_ref[b, DN_HEADS + h:DN_HEADS + h + 1, rs[c]], (C, C)) for b, c, h in P}
    decay = {p: jnp.exp(jnp.where(causal, gc_b[p] - gc_r[p], -1e30)) for p in P}

    kf = {p: kh[p].astype(f32) for p in P}
    kb = {p: kf[p] * beta_b[p] for p in P}
    kk = {p: lax.dot_general(kb[p].astype(bf16), kh[p], NT_DIMS, preferred_element_type=f32) for p in P}
    a_intra = {p: lax.dot_general(qh[p], kh[p], NT_DIMS, preferred_element_type=f32) * decay[p] for p in P}
    nmat = [jnp.where(ii > jj, kk[p] * decay[p], 0.0) for p in P]
    tinv = dict(zip(P, _unit_lower_inverse(nmat, ii, jj)))

    eg = {p: jnp.exp(gc_b[p]) for p in P}
    rhs = {p: jnp.concatenate([vh[p].astype(f32) * beta_b[p], kb[p] * eg[p]], axis=1) for p in P}
    sol = {p: _mm(tinv[p], rhs[p]) for p in P}

    q_dec = {p: qh[p].astype(f32) * eg[p] for p in P}
    g_last = {p: gc_b[p][C - 1:C, :] for p in P}
    kdt = {p: (kf[p] * jnp.exp(g_last[p] - gc_b[p])).T for p in P}

    state = {(b, h): s_ref[b * DN_HEADS + h] for b, h in S}
    for c in range(nch):
        m1 = {(b, h): _mm(jnp.concatenate([sol[b, c, h][:, HEAD_DIM:], q_dec[b, c, h]], axis=0), state[b, h]) for b, h in S}
        v_new = {(b, h): sol[b, c, h][:, :HEAD_DIM] - m1[b, h][:C] for b, h in S}
        m2 = {(b, h): _mm(jnp.concatenate([a_intra[b, c, h], kdt[b, c, h]], axis=0), v_new[b, h]) for b, h in S}
        state = {(b, h): state[b, h] * jnp.exp(g_last[b, c, h]) + m2[b, h][C:] for b, h in S}
        for b, h in S:
            o = m1[b, h][C:] + m2[b, h][:C]
            rms = lax.rsqrt(jnp.mean(o * o, axis=-1, keepdims=True) + RMS_EPS)
            y_ref[b, rs[c], hs[h]] = (o * rms * nw_ref[...] * z_ref[b, rs[c], hs[h]].astype(f32)).astype(bf16)
    for b, h in S:
        s_ref[b * DN_HEADS + h] = state[b, h]


def _stage_deltanet(q, k, v, z, gcol, grow, norm_w, *, nbr, nch, ngroups):
    B, T, _ = q.shape
    tt = nch * DN_CHUNK
    act_spec = lambda: pl.BlockSpec((nbr, tt, DN_WIDTH), lambda b, t: (b, t, 0))
    return pl.pallas_call(
        functools.partial(_deltanet_body, nbr=nbr, nch=nch, ngroups=ngroups),
        grid=(B // nbr, T // tt),
        in_specs=[act_spec(), act_spec(), act_spec(), act_spec(),
                  pl.BlockSpec((nbr, tt, 128), lambda b, t: (b, t, 0)),
                  pl.BlockSpec((nbr, 8, tt), lambda b, t: (b, 0, t)),
                  pl.BlockSpec((1, HEAD_DIM), lambda b, t: (0, 0))],
        out_specs=act_spec(),
        out_shape=jax.ShapeDtypeStruct((B, T, DN_WIDTH), bf16),
        scratch_shapes=[pltpu.VMEM((nbr * DN_HEADS, HEAD_DIM, HEAD_DIM), f32)],
        compiler_params=_cparams(("arbitrary", "arbitrary")),
        name="deltanet",
    )(q, k, v, z, gcol, grow, norm_w)


def _mixout_body(ydn_ref, u_ref, vln_ref, x_ref, ws_ref, bsp_ref, wout_ref, g1_ref, b1_ref, wrt_ref, brt_ref,
                 h_ref, hrow_ref, ids_ref, wts_ref, ycat_ref, *, tm):
    C = SGU_CHUNK
    ii = _iota2((C, C), 0)
    jj = _iota2((C, C), 1)
    ycat_ref[:, 0:DN_WIDTH] = ydn_ref[0]
    for g in range(SGU_GROUPS):
        gs = slice(g * C, (g + 1) * C)
        wsg = jnp.where(ii >= jj, ws_ref[g], 0.0).astype(bf16)
        for c in range(tm // C):
            rs = slice(c * C, (c + 1) * C)
            mixed = jnp.dot(wsg, vln_ref[0, rs, gs], preferred_element_type=f32) + bsp_ref[:, gs]
            ycat_ref[rs, DN_WIDTH + g * C:DN_WIDTH + (g + 1) * C] = (u_ref[0, rs, gs].astype(f32) * mixed).astype(bf16)

    RB = 128
    blocks = [slice(r, r + RB) for r in range(0, tm, RB)]
    mix = [jnp.dot(ycat_ref[rb, :], wout_ref[...], preferred_element_type=f32) for rb in blocks]
    h1s = []
    for rb, m in zip(blocks, mix):
        hp = DEEPNORM_ALPHA * x_ref[0, rb, :] + m
        mu = jnp.mean(hp, axis=-1, keepdims=True)
        hc = hp - mu
        var = jnp.mean(hc * hc, axis=-1, keepdims=True)
        h1 = hc * lax.rsqrt(var + LN_EPS) * g1_ref[...] + b1_ref[...]
        h_ref[0, rb, :] = h1
        h1b = h1.astype(bf16)
        hrow_ref[rb] = h1b.reshape(RB, ROW_TILE, 128)
        h1s.append(h1b)

    logit_blocks = [lax.dot_general(wrt_ref[...], hb, NT_DIMS, preferred_element_type=f32) + brt_ref[...] for hb in h1s]
    sub = _iota2((8, RB), 0)
    subf = sub.astype(f32)
    for rb, logits in zip(blocks, logit_blocks):
        gl = logits[0:8]
        gmax = jnp.max(gl, axis=0, keepdims=True)
        g_idx = jnp.min(jnp.where(gl == gmax, subf, float(MOE_GROUPS)), axis=0, keepdims=True)
        p_group = 1.0 / jnp.sum(jnp.exp(gl - gmax), axis=0, keepdims=True)
        within = jnp.zeros((8, RB), f32)
        for g in range(MOE_GROUPS):
            within = within + jnp.where(g_idx == float(g), logits[8 + 8 * g:16 + 8 * g], 0.0)
        m1 = jnp.max(within, axis=0, keepdims=True)
        i1 = jnp.min(jnp.where(within == m1, subf, float(EXPERTS_PER_GROUP)), axis=0, keepdims=True)
        rest = jnp.where(subf == i1, -jnp.inf, within)
        m2 = jnp.max(rest, axis=0, keepdims=True)
        i2 = jnp.min(jnp.where(rest == m2, subf, float(EXPERTS_PER_GROUP)), axis=0, keepdims=True)
        e = jnp.exp(m2 - m1)
        w1 = p_group / (1.0 + e)
        w2 = p_group * e / (1.0 + e)
        e1 = g_idx * float(EXPERTS_PER_GROUP) + i1
        e2 = g_idx * float(EXPERTS_PER_GROUP) + i2
        ids_ref[:, rb] = jnp.where(sub == 0, e1, jnp.where(sub == 1, e2, 0.0)).astype(i32)
        wts_ref[:, rb] = jnp.where(sub == 0, w1, jnp.where(sub == 1, w2, 0.0))


def _stage_mixout(ydn, u, vln, x, ws, bsp, wout, g1, b1, wrt, brt, *, tm):
    B, T, _ = x.shape
    nt = T // tm
    act_spec = lambda: pl.BlockSpec((1, tm, DN_WIDTH), lambda b, t: (b, t, 0))
    const2 = lambda shp: pl.BlockSpec(shp, lambda b, t: (0, 0))
    tok_spec = lambda: pl.BlockSpec((8, tm), lambda b, t: (0, b * nt + t))
    return pl.pallas_call(
        functools.partial(_mixout_body, tm=tm),
        grid=(B, nt),
        in_specs=[act_spec(), act_spec(), act_spec(),
                  pl.BlockSpec((1, tm, D_MODEL), lambda b, t: (b, t, 0)),
                  pl.BlockSpec((SGU_GROUPS, SGU_CHUNK, SGU_CHUNK), lambda b, t: (0, 0, 0)),
                  const2((SGU_CHUNK, SGU_WIDTH)),
                  const2((D_MODEL, D_MODEL)),
                  const2((1, D_MODEL)), const2((1, D_MODEL)),
                  const2((128, D_MODEL)), const2((128, 128))],
        out_specs=[pl.BlockSpec((1, tm, D_MODEL), lambda b, t: (b, t, 0)),
                   pl.BlockSpec((tm, ROW_TILE, 128), lambda b, t: (b * nt + t, 0, 0)), tok_spec(), tok_spec()],
        out_shape=[jax.ShapeDtypeStruct((B, T, D_MODEL), f32),
                   jax.ShapeDtypeStruct((B * T, ROW_TILE, 128), bf16),
                   jax.ShapeDtypeStruct((8, B * T), i32),
                   jax.ShapeDtypeStruct((8, B * T), f32)],
        scratch_shapes=[pltpu.VMEM((tm, D_MODEL), bf16)],
        compiler_params=_cparams(("arbitrary", "arbitrary")),
        name="mixout",
    )(ydn, u, vln, x, ws, bsp, wout, g1, b1, wrt, brt)


def _rank_body(ids_ref, rank_ref, cnt_ref, base_ref, *, tm):
    @pl.when(pl.program_id(0) == 0)
    def _():
        base_ref[...] = jnp.zeros(base_ref.shape, f32)

    sub = _iota2((N_EXPERTS, tm), 0)
    oh1 = (sub == ids_ref[0:1, :]).astype(f32)
    oh2 = (sub == ids_ref[1:2, :]).astype(f32)
    oh = (oh1 + oh2).astype(bf16)
    ti = _iota2((tm, tm), 0)
    tj = _iota2((tm, tm), 1)
    before = (ti < tj).astype(bf16)
    prefix = jnp.dot(oh, before, preferred_element_type=f32)
    tot = prefix + jnp.concatenate([base_ref[...]] * (tm // 128), axis=1)
    r1 = jnp.sum(oh1 * tot, axis=0, keepdims=True)
    r2 = jnp.sum(oh2 * tot, axis=0, keepdims=True)
    sub8 = _iota2((8, tm), 0)
    rank_ref[...] = jnp.where(sub8 == 0, r1, jnp.where(sub8 == 1, r2, 0.0)).astype(i32)
    base_ref[...] = base_ref[...] + jnp.dot(oh, jnp.ones((tm, 128), bf16), preferred_element_type=f32)
    cnt_ref[...] = base_ref[...]


def _stage_rank(ids, *, tm):
    n = ids.shape[1]
    return pl.pallas_call(
        functools.partial(_rank_body, tm=tm),
        grid=(n // tm,),
        in_specs=[pl.BlockSpec((8, tm), lambda i: (0, i))],
        out_specs=[pl.BlockSpec((8, tm), lambda i: (0, i)),
                   pl.BlockSpec((N_EXPERTS, 128), lambda i: (0, 0))],
        out_shape=[jax.ShapeDtypeStruct((8, n), i32), jax.ShapeDtypeStruct((N_EXPERTS, 128), f32)],
        scratch_shapes=[pltpu.VMEM((N_EXPERTS, 128), f32)],
        compiler_params=_cparams(("arbitrary",)),
        name="moe_rank",
    )(ids)


def _dest_body(cnt_ref, ids_ref, rank_ref, dest_ref, meta_ref, blk_ref, *, tm, nb_pad):
    cnt = cnt_ref[...]
    padded = jnp.floor((cnt + (MOE_BLOCK - 1)) * (1.0 / MOE_BLOCK)) * MOE_BLOCK
    ei = _iota2((N_EXPERTS, N_EXPERTS), 0)
    ej = _iota2((N_EXPERTS, N_EXPERTS), 1)
    pends = jnp.dot((ei >= ej).astype(f32), padded, precision=HIGHEST, preferred_element_type=f32)
    pstart = pends - padded
    sub = _iota2((N_EXPERTS, tm), 0)
    pst = jnp.concatenate([pstart] * (tm // 128), axis=1)
    d1 = jnp.sum(jnp.where(sub == ids_ref[0:1, :], pst, 0.0), axis=0, keepdims=True)
    d2 = jnp.sum(jnp.where(sub == ids_ref[1:2, :], pst, 0.0), axis=0, keepdims=True)
    sub8 = _iota2((8, tm), 0)
    dest_ref[...] = jnp.where(sub8 == 0, d1, jnp.where(sub8 == 1, d2, 0.0)).astype(i32) + rank_ref[...]

    @pl.when(pl.program_id(0) == 0)
    def _():
        s64 = _iota2((N_EXPERTS, 128), 0)
        l64 = _iota2((N_EXPERTS, 128), 1)
        diag = s64 == l64
        fill_off = jnp.sum(jnp.where(diag, pstart + cnt, 0.0), axis=0, keepdims=True)
        fill_n = jnp.sum(jnp.where(diag, padded - cnt, 0.0), axis=0, keepdims=True)
        nused = pends[N_EXPERTS - 1:N_EXPERTS, :] * (1.0 / MOE_BLOCK)
        m8 = _iota2((8, 128), 0)
        meta_ref[...] = jnp.where(m8 == 0, fill_off, jnp.where(m8 == 1, fill_n, jnp.where(m8 == 2, nused, 0.0))).astype(i32)
        bstart = (_iota2((N_EXPERTS, nb_pad), 1) * MOE_BLOCK).astype(f32)
        pe = jnp.concatenate([pends] * (nb_pad // 128), axis=1)
        be = jnp.sum((pe <= bstart).astype(f32), axis=0, keepdims=True)
        be = jnp.minimum(be, float(N_EXPERTS - 1))
        blk_ref[...] = jnp.broadcast_to(be, (8, nb_pad)).astype(i32)


def _stage_dest(cnt, ids, rank, *, tm, nb_pad):
    n = ids.shape[1]
    tok = lambda: pl.BlockSpec((8, tm), lambda i: (0, i))
    return pl.pallas_call(
        functools.partial(_dest_body, tm=tm, nb_pad=nb_pad),
        grid=(n // tm,),
        in_specs=[pl.BlockSpec((N_EXPERTS, 128), lambda i: (0, 0)), tok(), tok()],
        out_specs=[tok(), pl.BlockSpec((8, 128), lambda i: (0, 0)), pl.BlockSpec((8, nb_pad), lambda i: (0, 0))],
        out_shape=[jax.ShapeDtypeStruct((8, n), i32), jax.ShapeDtypeStruct((8, 128), i32),
                   jax.ShapeDtypeStruct((8, nb_pad), i32)],
        compiler_params=_cparams(("arbitrary",)),
        name="moe_dest",
    )(cnt, ids, rank)


def _dispatch_body(fill_off_ref, fill_n_ref, nused_ref, dest_ref, h3_ref, xs_ref, zero_ref, sem, zsem, *, tm):
    def row_copy(t, d):
        return pltpu.make_async_copy(h3_ref.at[t], xs_ref.at[d], sem)

    def issue(t, carry):
        row_copy(t, dest_ref[0, 0, t]).start(priority=0)
        row_copy(t, dest_ref[0, 1, t]).start(priority=1)
        return carry

    lax.fori_loop(0, tm, issue, 0, unroll=8)

    @pl.when(pl.program_id(0) == 0)
    def _():
        zero_ref[...] = jnp.zeros(zero_ref.shape, bf16)

        def fill(start):
            def body(e, carry):
                off = fill_off_ref[e]
                npad = fill_n_ref[e]
                bit = MOE_BLOCK // 2
                while bit:
                    @pl.when((npad & bit) != 0)
                    def _(off=off, bit=bit):
                        cp = pltpu.make_async_copy(zero_ref.at[pl.ds(0, bit)], xs_ref.at[pl.ds(off, bit)], zsem)
                        cp.start() if start else cp.wait()
                    off = off + (npad & bit)
                    bit //= 2
                return carry
            return body

        lax.fori_loop(0, N_EXPERTS, fill(True), 0)
        lax.fori_loop(0, N_EXPERTS, fill(False), 0)

        def tail_copy(b):
            return pltpu.make_async_copy(zero_ref, xs_ref.at[pl.ds(b * MOE_BLOCK, MOE_BLOCK)], zsem)

        nblocks = xs_ref.shape[0] // MOE_BLOCK
        lax.fori_loop(nused_ref[0], nblocks, lambda b, c: (tail_copy(b).start(), c)[1], 0)
        lax.fori_loop(nused_ref[0], nblocks, lambda b, c: (tail_copy(0).wait(), c)[1], 0)

    for _ in range(2):
        pltpu.make_async_copy(h3_ref, xs_ref.at[pl.ds(0, tm)], sem).wait()


def _stage_dispatch(fill_off, fill_n, nused, dest3, hrow, p_rows, *, tm):
    n = hrow.shape[0]
    return pl.pallas_call(
        functools.partial(_dispatch_body, tm=tm),
        grid_spec=pltpu.PrefetchScalarGridSpec(
            num_scalar_prefetch=3,
            grid=(n // tm,),
            in_specs=[pl.BlockSpec((1, 2, tm), lambda i, fo, fn, nu: (i, 0, 0), memory_space=pltpu.SMEM),
                      pl.BlockSpec((tm, ROW_TILE, 128), lambda i, fo, fn, nu: (i, 0, 0))],
            out_specs=pl.BlockSpec(memory_space=pl.ANY),
            scratch_shapes=[pltpu.VMEM((MOE_BLOCK, ROW_TILE, 128), bf16),
                            pltpu.SemaphoreType.DMA, pltpu.SemaphoreType.DMA],
        ),
        out_shape=jax.ShapeDtypeStruct((p_rows, ROW_TILE, 128), bf16),
        compiler_params=_cparams(("arbitrary",)),
        name="moe_dispatch",
    )(fill_off, fill_n, nused, dest3, hrow)


def _experts_body(blk_ref, nused_ref, xs_ref, wg_hbm, wu_hbm, wd_hbm, ys_ref,
                  wg32_ref, wu32_ref, wd32_ref, wgu16_ref, wd16_ref, wsem):
    i = pl.program_id(0)
    nused = nused_ref[0]
    used = i < nused
    e = blk_ref[i]

    def weight_copies(ex):
        return (pltpu.make_async_copy(wg_hbm.at[ex], wg32_ref, wsem.at[0]),
                pltpu.make_async_copy(wu_hbm.at[ex], wu32_ref, wsem.at[1]),
                pltpu.make_async_copy(wd_hbm.at[ex], wd32_ref, wsem.at[2]))

    @pl.when((i == 0) & used)
    def _():
        for cp in weight_copies(e):
            cp.start()

    @pl.when(used & ((i == 0) | (e != blk_ref[jnp.maximum(i - 1, 0)])))
    def _():
        for cp in weight_copies(e):
            cp.wait()
        wgu16_ref[:, 0:D_EXPERT] = wg32_ref[...].astype(bf16)
        wgu16_ref[:, D_EXPERT:2 * D_EXPERT] = wu32_ref[...].astype(bf16)
        wd16_ref[...] = wd32_ref[...].astype(bf16)
        nxt = lax.while_loop(lambda j: (j < nused) & (blk_ref[jnp.minimum(j, nused - 1)] == e), lambda j: j + 1, i + 1)

        @pl.when(nxt < nused)
        def _():
            for cp in weight_copies(blk_ref[jnp.minimum(nxt, nused - 1)]):
                cp.start()

    @pl.when(used)
    def _():
        half = MOE_BLOCK // 2
        rows = [slice(p * half, (p + 1) * half) for p in range(2)]
        gu = [jnp.dot(xs_ref[r].reshape(half, D_MODEL), wgu16_ref[...], preferred_element_type=f32) for r in rows]
        hid = [(_silu(g[:, :D_EXPERT]) * g[:, D_EXPERT:]).astype(bf16) for g in gu]
        y = [jnp.dot(hd, wd16_ref[...], preferred_element_type=f32) for hd in hid]
        for r, yp in zip(rows, y):
            ys_ref[r] = yp.astype(bf16).reshape(half, ROW_TILE, 128)

    @pl.when(jnp.logical_not(used))
    def _():
        ys_ref[...] = jnp.zeros(ys_ref.shape, bf16)


def _stage_experts(blk_e, nused, xs, w_gate, w_up, w_down):
    p_rows = xs.shape[0]
    nb = p_rows // MOE_BLOCK

    def last_used(i, nu):
        return jnp.maximum(jnp.minimum(i, nu[0] - 1), 0)

    def x_map(i, blk, nu):
        return (last_used(i, nu), 0, 0)

    def row_map(i, blk, nu):
        return (i, 0, 0)

    return pl.pallas_call(
        _experts_body,
        grid_spec=pltpu.PrefetchScalarGridSpec(
            num_scalar_prefetch=2,
            grid=(nb,),
            in_specs=[pl.BlockSpec((MOE_BLOCK, ROW_TILE, 128), x_map),
                      pl.BlockSpec(memory_space=pl.ANY),
                      pl.BlockSpec(memory_space=pl.ANY),
                      pl.BlockSpec(memory_space=pl.ANY)],
            out_specs=pl.BlockSpec((MOE_BLOCK, ROW_TILE, 128), row_map),
            scratch_shapes=[pltpu.VMEM((D_MODEL, D_EXPERT), f32), pltpu.VMEM((D_MODEL, D_EXPERT), f32),
                            pltpu.VMEM((D_EXPERT, D_MODEL), f32),
                            pltpu.VMEM((D_MODEL, 2 * D_EXPERT), bf16), pltpu.VMEM((D_EXPERT, D_MODEL), bf16),
                            pltpu.SemaphoreType.DMA((3,))],
        ),
        out_shape=jax.ShapeDtypeStruct((p_rows, ROW_TILE, 128), bf16),
        compiler_params=_cparams(("arbitrary",)),
        name="moe_experts",
    )(blk_e, nused, xs, w_gate, w_up, w_down)


def _combine_body(dcur_ref, dnext_ref, h_ref, wts_ref, g2_ref, b2_ref, ys_ref, o_ref, ybuf_ref, sem, *, tm, nsteps):
    i = pl.program_id(0)
    slot = lax.rem(i, 2)

    def issue_tile(d_ref, s):
        def body(t, carry):
            for k in range(2):
                pltpu.make_async_copy(ys_ref.at[d_ref[0, k, t]], ybuf_ref.at[s, k, t], sem.at[s]).start(priority=k)
            return carry

        lax.fori_loop(0, tm, body, 0, unroll=8)

    @pl.when(i == 0)
    def _():
        issue_tile(dcur_ref, 0)

    @pl.when(i + 1 < nsteps)
    def _():
        issue_tile(dnext_ref, 1 - slot)

    for k in range(2):
        pltpu.make_async_copy(ys_ref.at[pl.ds(0, tm)], ybuf_ref.at[slot, k], sem.at[slot]).wait()

    pieces = []
    for c in range(tm // 128):
        ls = slice(c * 128, (c + 1) * 128)
        w1c = jnp.broadcast_to(wts_ref[0:1, ls], (128, 128)).T
        w2c = jnp.broadcast_to(wts_ref[1:2, ls], (128, 128)).T
        w1f = jnp.concatenate([w1c] * (D_MODEL // 128), axis=1)
        w2f = jnp.concatenate([w2c] * (D_MODEL // 128), axis=1)
        y1 = ybuf_ref[slot, 0, ls].reshape(128, D_MODEL).astype(f32)
        y2 = ybuf_ref[slot, 1, ls].reshape(128, D_MODEL).astype(f32)
        pieces.append(w1f * y1 + w2f * y2)
    ffn = jnp.concatenate(pieces, axis=0)
    hp = DEEPNORM_ALPHA * h_ref[...] + ffn
    mu = jnp.mean(hp, axis=-1, keepdims=True)
    hc = hp - mu
    var = jnp.mean(hc * hc, axis=-1, keepdims=True)
    o_ref[...] = hc * lax.rsqrt(var + LN_EPS) * g2_ref[...] + b2_ref[...]


def _stage_combine(dest3, h2, wts, g2, b2, ys, *, tm):
    n = h2.shape[0]
    nsteps = n // tm
    return pl.pallas_call(
        functools.partial(_combine_body, tm=tm, nsteps=nsteps),
        grid=(nsteps,),
        in_specs=[pl.BlockSpec((1, 2, tm), lambda i: (i, 0, 0), memory_space=pltpu.SMEM),
                  pl.BlockSpec((1, 2, tm), lambda i: (jnp.minimum(i + 1, nsteps - 1), 0, 0), memory_space=pltpu.SMEM),
                  pl.BlockSpec((tm, D_MODEL), lambda i: (i, 0)),
                  pl.BlockSpec((8, tm), lambda i: (0, i)),
                  pl.BlockSpec((1, D_MODEL), lambda i: (0, 0)),
                  pl.BlockSpec((1, D_MODEL), lambda i: (0, 0)),
                  pl.BlockSpec(memory_space=pl.ANY)],
        out_specs=pl.BlockSpec((tm, D_MODEL), lambda i: (i, 0)),
        out_shape=jax.ShapeDtypeStruct((n, D_MODEL), f32),
        scratch_shapes=[pltpu.VMEM((2, 2, tm, ROW_TILE, 128), bf16), pltpu.SemaphoreType.DMA((2,))],
        compiler_params=_cparams(("arbitrary",)),
        name="moe_combine",
    )(dest3, dest3, h2, wts, g2, b2, ys)


def _layer(h, w_in, conv_w, a_log, dt_bias, dn_norm_w, sgu_ln_g, sgu_ln_b, w_spatial, b_spatial, w_out,
           ln1_g, ln1_b, w_rg, b_rg, w_re, b_re, w_gate, w_up, w_down, ln2_g, ln2_b,
           *, tm_in, in_groups, dn_rows, dn_chunks, dn_groups, tm_mix, tm_rank, tm_disp, tm_comb):
    B, T, _ = h.shape
    n = B * T
    qkvz = 4 * DN_WIDTH
    ba0 = qkvz
    uv0 = qkvz + 2 * DN_HEADS
    w_cols = jnp.zeros((D_MODEL, IN_COLS_ALIGNED), bf16)
    w_cols = lax.dynamic_update_slice(w_cols, w_in[:, :qkvz].astype(bf16), (0, 0))
    w_cols = lax.dynamic_update_slice(w_cols, w_in[:, uv0:].astype(bf16), (0, qkvz))
    w_cols = lax.dynamic_update_slice(w_cols, w_in[:, ba0:uv0].astype(bf16), (0, qkvz + 2 * SGU_WIDTH))
    wbat = w_in[:, ba0:uv0].T.astype(bf16)
    decay_prm = jnp.stack([a_log, dt_bias])
    pcol = jnp.pad(decay_prm, ((0, 6), (DN_HEADS, 128 - 2 * DN_HEADS)))
    prow = jnp.broadcast_to(jnp.pad(decay_prm, ((0, 0), (DN_HEADS, 8 - 2 * DN_HEADS)))[:, :, None], (2, 8, 128))

    q, k, v, z, u, vln, gcol, grow = _stage_inproj(
        h, w_cols, wbat, conv_w, pcol, prow, sgu_ln_g[None, :], sgu_ln_b[None, :], tm=tm_in, ngroups=in_groups)
    ydn = _stage_deltanet(q, k, v, z, gcol, grow, dn_norm_w[None, :], nbr=dn_rows, nch=dn_chunks, ngroups=dn_groups)

    bsp = jnp.broadcast_to(b_spatial.T[:, :, None], (SGU_CHUNK, SGU_GROUPS, SGU_CHUNK)).reshape(SGU_CHUNK, SGU_WIDTH)
    n_logit = MOE_GROUPS + N_EXPERTS
    wrt = jnp.pad(jnp.concatenate([w_rg, w_re], axis=1).T, ((0, 128 - n_logit), (0, 0))).astype(bf16)
    brt = jnp.broadcast_to(jnp.pad(jnp.concatenate([b_rg, b_re]), (0, 128 - n_logit))[:, None], (128, 128))
    h1, hrow, ids, wts = _stage_mixout(ydn, u, vln, h, w_spatial, bsp, w_out.astype(bf16), ln1_g[None, :],
                                       ln1_b[None, :], wrt, brt, tm=tm_mix)

    p_rows = (-(-(n * 2) // MOE_BLOCK)) * MOE_BLOCK + N_EXPERTS * MOE_BLOCK
    nb = p_rows // MOE_BLOCK
    nb_pad = (-(-nb // 128)) * 128
    rank, cnt = _stage_rank(ids, tm=tm_rank)
    dest, meta, blk = _stage_dest(cnt, ids, rank, tm=tm_rank, nb_pad=nb_pad)

    h2 = h1.reshape(n, D_MODEL)
    dest_d = dest[0:2].reshape(2, n // tm_disp, tm_disp).transpose(1, 0, 2)
    xs = _stage_dispatch(meta[0, :N_EXPERTS], meta[1, :N_EXPERTS], meta[2, 0:1], dest_d, hrow, p_rows, tm=tm_disp)
    ys = _stage_experts(blk[0, :nb], meta[2, 0:1], xs, w_gate, w_up, w_down)
    dest_c = dest[0:2].reshape(2, n // tm_comb, tm_comb).transpose(1, 0, 2)
    out = _stage_combine(dest_c, h2, wts, ln2_g[None, :], ln2_b[None, :], ys, tm=tm_comb)
    return out.reshape(B, T, D_MODEL)


def kernel(x, w_in, conv_w, a_log, dt_bias, dn_norm_w, sgu_ln_g, sgu_ln_b, w_spatial, b_spatial, w_out, ln1_g, ln1_b, w_router_group, b_router_group, w_router_expert, b_router_expert, w_gate, w_up, w_down, ln2_g, ln2_b):
    h = x
    for l in range(w_in.shape[0]):
        h = _layer(h, w_in[l], conv_w[l], a_log[l], dt_bias[l], dn_norm_w[l], sgu_ln_g[l], sgu_ln_b[l],
                   w_spatial[l], b_spatial[l], w_out[l], ln1_g[l], ln1_b[l],
                   w_router_group[l], b_router_group[l], w_router_expert[l], b_router_expert[l],
                   w_gate[l], w_up[l], w_down[l], ln2_g[l], ln2_b[l],
                   tm_in=512, in_groups=1, dn_rows=4, dn_chunks=2, dn_groups=2, tm_mix=512, tm_rank=1024, tm_disp=2048, tm_comb=512)
    return h
```

```python
import functools

import jax
import jax.numpy as jnp
from jax import lax
from jax.experimental import pallas as pl
from jax.experimental.pallas import tpu as pltpu

f32 = jnp.float32
bf16 = jnp.bfloat16
i32 = jnp.int32

D_MODEL = 1024
DN_WIDTH = 512
DN_HEADS = 4
HEAD_DIM = 128
CONV_K = 4
SGU_WIDTH = 512
SGU_GROUPS = 4
SGU_CHUNK = 128
DN_CHUNK = 128
MOE_GROUPS = 8
EXPERTS_PER_GROUP = 8
N_EXPERTS = 64
D_EXPERT = 512
MOE_BLOCK = 256
IN_COLS_ALIGNED = 4 * DN_WIDTH + 2 * SGU_WIDTH + 128
ROW_TILE = D_MODEL // 128
DEEPNORM_ALPHA = 2.0 ** 0.25
LN_EPS = 1e-5
RMS_EPS = 1e-6
HIGHEST = lax.Precision.HIGHEST
VMEM_LIMIT_BYTES = 56 * 1024 * 1024

NT_DIMS = (((1,), (1,)), ((), ()))


def _cparams(sem, flags=None):
    return pltpu.CompilerParams(dimension_semantics=sem, vmem_limit_bytes=VMEM_LIMIT_BYTES, flags=flags)


def _sigmoid(x):
    return 1.0 / (1.0 + jnp.exp(-x))


def _silu(x):
    h = 0.5 * x
    return h + h * jnp.tanh(h)


def _softplus(x):
    return jnp.maximum(x, 0.0) + jnp.log1p(jnp.exp(-jnp.abs(x)))


def _gelu_tanh(x):
    c = 0.7978845608028654
    return x * (0.5 * (1.0 + jnp.tanh(c * (x + 0.044715 * (x * x * x)))))


def _iota2(shape, axis):
    return lax.broadcasted_iota(i32, shape, axis)


def _inproj_body(x_ref, w_ref, convw_ref, pcol_ref, prow_ref, lng_ref, lnb_ref, ones_ref,
                 q_ref, k_ref, v_ref, z_ref, u_ref, vln_ref, gcol_ref, grow_ref, *ext_refs, tm, ngroups):
    W = DN_WIDTH
    gm = tm // ngroups
    ext = [ext_refs[3 * g:3 * g + 3] for g in range(ngroups)]

    @pl.when(pl.program_id(1) == 0)
    def _():
        for e_ref in ext[0]:
            e_ref[0:8, :] = jnp.zeros((8, W), f32)

    for g in range(ngroups):
        _inproj_rows(x_ref, w_ref, convw_ref, pcol_ref, prow_ref, lng_ref, lnb_ref, ones_ref,
                     q_ref, k_ref, v_ref, z_ref, u_ref, vln_ref, gcol_ref, grow_ref, ext[g],
                     ext[(g + 1) % ngroups], r0=g * gm, gm=gm)


def _inproj_rows(x_ref, w_ref, convw_ref, pcol_ref, prow_ref, lng_ref, lnb_ref, ones_ref,
                 q_ref, k_ref, v_ref, z_ref, u_ref, vln_ref, gcol_ref, grow_ref, ext, ext_next, *, r0, gm):
    W = DN_WIDTH
    rows = slice(r0, r0 + gm)
    xb = x_ref[0, rows, :].astype(bf16)
    for part, e_ref in enumerate(ext):
        e_ref[8:8 + gm, :] = jnp.dot(xb, w_ref[:, part * W:(part + 1) * W], preferred_element_type=f32)
    zc = 3 * W
    uc = zc + W
    vc = uc + SGU_WIDTH
    bc = vc + SGU_WIDTH
    pz = jnp.dot(xb, w_ref[:, zc:zc + W], preferred_element_type=f32)
    pu = jnp.dot(xb, w_ref[:, uc:uc + SGU_WIDTH], preferred_element_type=f32)
    pv = jnp.dot(xb, w_ref[:, vc:vc + SGU_WIDTH], preferred_element_type=f32)
    pba = jnp.dot(xb, w_ref[:, bc:bc + 128], preferred_element_type=f32)

    def group_sums(a):
        return jnp.dot(a.astype(bf16), ones_ref[...], preferred_element_type=f32)

    for part, (e_ref, n_ref, out_ref) in enumerate(zip(ext, ext_next, (q_ref, k_ref, v_ref))):
        cs = slice(part * W, (part + 1) * W)
        y = convw_ref[3:4, cs] * e_ref[8:8 + gm, :]
        for j in range(CONV_K - 1):
            y = y + convw_ref[j:j + 1, cs] * e_ref[5 + j:5 + j + gm, :]
        y = _silu(y)
        if part < 2:
            scale = HEAD_DIM ** -0.5 if part == 0 else 1.0
            y = y * (lax.rsqrt(group_sums(y * y) + RMS_EPS) * scale)
        out_ref[0, rows, :] = y.astype(bf16)
        n_ref[0:8, :] = e_ref[gm:gm + 8, :]

    z_ref[0, rows, :] = _silu(pz).astype(bf16)

    u_ref[0, rows, :] = _gelu_tanh(pu).astype(bf16)
    pv = _gelu_tanh(pv)
    for g in range(SGU_GROUPS):
        sl = slice(g * SGU_CHUNK, (g + 1) * SGU_CHUNK)
        vg = pv[:, sl]
        mu = jnp.mean(vg, axis=-1, keepdims=True)
        vcn = vg - mu
        var = jnp.mean(vcn * vcn, axis=-1, keepdims=True)
        vln_ref[0, rows, sl] = (vcn * lax.rsqrt(var + LN_EPS) * lng_ref[:, sl] + lnb_ref[:, sl]).astype(bf16)

    lane = _iota2((DN_CHUNK, 128), 1)
    is_g = (lane >= DN_HEADS) & (lane < 2 * DN_HEADS)
    beta = _sigmoid(pba)
    gval = -jnp.exp(pcol_ref[0:1, :]) * _softplus(pba + pcol_ref[1:2, :])
    ci = _iota2((DN_CHUNK, DN_CHUNK), 0)
    cj = _iota2((DN_CHUNK, DN_CHUNK), 1)
    ltri = (ci >= cj).astype(f32)
    utri = (ci <= cj).astype(f32)
    sub = _iota2((8, DN_CHUNK), 0)
    for c in range(gm // DN_CHUNK):
        rs = slice(c * DN_CHUNK, (c + 1) * DN_CHUNK)
        os_ = slice(r0 + c * DN_CHUNK, r0 + (c + 1) * DN_CHUNK)
        gc = jnp.dot(ltri, jnp.where(is_g, gval[rs], 0.0), precision=HIGHEST, preferred_element_type=f32)
        gcol_ref[0, os_, :] = jnp.where(lane < DN_HEADS, beta[rs], gc)
        pbat = pba[rs].T[0:8, :]
        gt = -jnp.exp(prow_ref[0]) * _softplus(pbat + prow_ref[1])
        gt = jnp.where(sub >= DN_HEADS, gt, 0.0)
        grow_ref[0, :, os_] = jnp.dot(gt, utri, precision=HIGHEST, preferred_element_type=f32)


def _stage_inproj(x, w_re, conv_w, pcol, prow, lng, lnb, *, tm, ngroups):
    B, T, _ = x.shape
    wcols = w_re.shape[1]
    grid = (B, T // tm)
    gi = lax.broadcasted_iota(i32, (DN_WIDTH, DN_WIDTH), 0) // 128
    gj = lax.broadcasted_iota(i32, (DN_WIDTH, DN_WIDTH), 1) // 128
    group_ones = (gi == gj).astype(bf16)
    act = lambda: jax.ShapeDtypeStruct((B, T, DN_WIDTH), bf16)
    act_spec = lambda: pl.BlockSpec((1, tm, DN_WIDTH), lambda b, t: (b, t, 0))
    const2 = lambda shp: pl.BlockSpec(shp, lambda b, t: (0, 0))
    return pl.pallas_call(
        functools.partial(_inproj_body, tm=tm, ngroups=ngroups),
        grid=grid,
        in_specs=[
            pl.BlockSpec((1, tm, D_MODEL), lambda b, t: (b, t, 0)),
            const2((D_MODEL, wcols)),
            const2((CONV_K, 3 * DN_WIDTH)),
            const2((8, 128)),
            pl.BlockSpec((2, 8, 128), lambda b, t: (0, 0, 0)),
            const2((1, SGU_WIDTH)),
            const2((1, SGU_WIDTH)),
            const2((DN_WIDTH, DN_WIDTH)),
        ],
        out_specs=[act_spec() for _ in range(6)] + [
            pl.BlockSpec((1, tm, 128), lambda b, t: (b, t, 0)),
            pl.BlockSpec((1, 8, tm), lambda b, t: (b, 0, t)),
        ],
        out_shape=[act() for _ in range(6)] + [
            jax.ShapeDtypeStruct((B, T, 128), f32),
            jax.ShapeDtypeStruct((B, 8, T), f32),
        ],
        scratch_shapes=[pltpu.VMEM((tm // ngroups + 8, DN_WIDTH), f32) for _ in range(3 * ngroups)],
        compiler_params=_cparams(("arbitrary", "arbitrary")),
        name="inproj",
    )(x, w_re, conv_w, pcol, prow, lng, lnb, group_ones)


def _mm(a, b):
    return jnp.dot(a.astype(bf16), b.astype(bf16), preferred_element_type=f32)


def _unit_lower_inverse(nmats, ii, jj):
    n = nmats[0].shape[0]
    eye = (ii == jj).astype(f32)
    leaf = jnp.right_shift(ii, 3) == jnp.right_shift(jj, 3)
    dblk = [jnp.where(leaf, m, 0.0) for m in nmats]
    s1 = [_mm(d, d) for d in dblk]
    r1 = [eye - d for d in dblk]
    both = [_mm(s, jnp.concatenate([s, r], axis=1)) for s, r in zip(s1, r1)]
    r2 = [r + bo[:, n:] for r, bo in zip(r1, both)]
    xs = [r + _mm(bo[:, :n], r) for r, bo in zip(r2, both)]
    size = 8
    while size < n:
        lows = [slice(r + size, r + 2 * size) for r in range(0, n, 2 * size)]
        ups = [slice(r, r + size) for r in range(0, n, 2 * size)]
        rsel = _iota2((n // 2, n), 0)
        ilow = rsel + size * (jnp.right_shift(rsel, size.bit_length() - 1) + 1)
        jlow = _iota2((n // 2, n), 1)
        in_pair_upper = (jlow >= ilow - (ilow & (2 * size - 1))) & (jlow < ilow - (ilow & (size - 1)))
        zeros = jnp.zeros((size, n), f32)
        new_xs = []
        ylows = [_mm(jnp.where(in_pair_upper, jnp.concatenate([m[s] for s in lows], axis=0), 0.0), x)
                 for m, x in zip(nmats, xs)]
        yfull = [jnp.concatenate([piece for k in range(len(lows)) for piece in (zeros, y[k * size:(k + 1) * size])], axis=0)
                 for y in ylows]
        corr = [_mm(jnp.concatenate([x[s] for s in lows], axis=0), yf) for x, yf in zip(xs, yfull)]
        for x, c in zip(xs, corr):
            pieces = []
            for k, (u, l) in enumerate(zip(ups, lows)):
                pieces += [x[u], x[l] - c[k * size:(k + 1) * size]]
            new_xs.append(jnp.concatenate(pieces, axis=0))
        xs = new_xs
        size *= 2
    return xs


def _deltanet_body(q_ref, k_ref, v_ref, z_ref, gcol_ref, grow_ref, nw_ref, y_ref, s_ref, *, nbr, nch, ngroups):
    @pl.when(pl.program_id(1) == 0)
    def _():
        s_ref[...] = jnp.zeros(s_ref.shape, f32)

    per = nbr // ngroups
    for g in range(ngroups):
        _deltanet_rows(q_ref, k_ref, v_ref, z_ref, gcol_ref, grow_ref, nw_ref, y_ref, s_ref,
                       rows=range(g * per, (g + 1) * per), nch=nch)


def _deltanet_rows(q_ref, k_ref, v_ref, z_ref, gcol_ref, grow_ref, nw_ref, y_ref, s_ref, *, rows, nch):
    C = DN_CHUNK
    S = [(b, h) for b in rows for h in range(DN_HEADS)]
    P = [(b, c, h) for c in range(nch) for b, h in S]

    ii = _iota2((C, C), 0)
    jj = _iota2((C, C), 1)
    causal = ii >= jj
    rs = [slice(c * C, (c + 1) * C) for c in range(nch)]
    hs = [slice(h * HEAD_DIM, (h + 1) * HEAD_DIM) for h in range(DN_HEADS)]
    gcol = {(b, c): gcol_ref[b, rs[c], :] for b in rows for c in range(nch)}
    qh = {(b, c, h): q_ref[b, rs[c], hs[h]] for b, c, h in P}
    kh = {(b, c, h): k_ref[b, rs[c], hs[h]] for b, c, h in P}
    vh = {(b, c, h): v_ref[b, rs[c], hs[h]] for b, c, h in P}
    gc_b = {(b, c, h): jnp.broadcast_to(gcol[b, c][:, DN_HEADS + h:DN_HEADS + h + 1], (C, HEAD_DIM)) for b, c, h in P}
    beta_b = {(b, c, h): jnp.broadcast_to(gcol[b, c][:, h:h + 1], (C, HEAD_DIM)) for b, c, h in P}
    gc_r = {(b, c, h): jnp.broadcast_to(grow_ref[b, DN_HEADS + h:DN_HEADS + h + 1, rs[c]], (C, C)) for b, c, h in P}
    decay = {p: jnp.exp(jnp.where(causal, gc_b[p] - gc_r[p], -1e30)) for p in P}

    kf = {p: kh[p].astype(f32) for p in P}
    kb = {p: kf[p] * beta_b[p] for p in P}
    kk = {p: lax.dot_general(kb[p].astype(bf16), kh[p], NT_DIMS, preferred_element_type=f32) for p in P}
    a_intra = {p: lax.dot_general(qh[p], kh[p], NT_DIMS, preferred_element_type=f32) * decay[p] for p in P}
    nmat = [jnp.where(ii > jj, kk[p] * decay[p], 0.0) for p in P]
    tinv = dict(zip(P, _unit_lower_inverse(nmat, ii, jj)))

    eg = {p: jnp.exp(gc_b[p]) for p in P}
    rhs = {p: jnp.concatenate([vh[p].astype(f32) * beta_b[p], kb[p] * eg[p]], axis=1) for p in P}
    sol = {p: _mm(tinv[p], rhs[p]) for p in P}

    q_dec = {p: qh[p].astype(f32) * eg[p] for p in P}
    g_last = {p: gc_b[p][C - 1:C, :] for p in P}
    kdt = {p: (kf[p] * jnp.exp(g_last[p] - gc_b[p])).T for p in P}

    state = {(b, h): s_ref[b * DN_HEADS + h] for b, h in S}
    for c in range(nch):
        m1 = {(b, h): _mm(jnp.concatenate([sol[b, c, h][:, HEAD_DIM:], q_dec[b, c, h]], axis=0), state[b, h]) for b, h in S}
        v_new = {(b, h): sol[b, c, h][:, :HEAD_DIM] - m1[b, h][:C] for b, h in S}
        m2 = {(b, h): _mm(jnp.concatenate([a_intra[b, c, h], kdt[b, c, h]], axis=0), v_new[b, h]) for b, h in S}
        state = {(b, h): state[b, h] * jnp.exp(g_last[b, c, h]) + m2[b, h][C:] for b, h in S}
        for b, h in S:
            o = m1[b, h][C:] + m2[b, h][:C]
            rms = lax.rsqrt(jnp.mean(o * o, axis=-1, keepdims=True) + RMS_EPS)
            y_ref[b, rs[c], hs[h]] = (o * rms * nw_ref[...] * z_ref[b, rs[c], hs[h]].astype(f32)).astype(bf16)
    for b, h in S:
        s_ref[b * DN_HEADS + h] = state[b, h]


def _stage_deltanet(q, k, v, z, gcol, grow, norm_w, *, nbr, nch, ngroups):
    B, T, _ = q.shape
    tt = nch * DN_CHUNK
    act_spec = lambda: pl.BlockSpec((nbr, tt, DN_WIDTH), lambda b, t: (b, t, 0))
    return pl.pallas_call(
        functools.partial(_deltanet_body, nbr=nbr, nch=nch, ngroups=ngroups),
        grid=(B // nbr, T // tt),
        in_specs=[act_spec(), act_spec(), act_spec(), act_spec(),
                  pl.BlockSpec((nbr, tt, 128), lambda b, t: (b, t, 0)),
                  pl.BlockSpec((nbr, 8, tt), lambda b, t: (b, 0, t)),
                  pl.BlockSpec((1, HEAD_DIM), lambda b, t: (0, 0))],
        out_specs=act_spec(),
        out_shape=jax.ShapeDtypeStruct((B, T, DN_WIDTH), bf16),
        scratch_shapes=[pltpu.VMEM((nbr * DN_HEADS, HEAD_DIM, HEAD_DIM), f32)],
        compiler_params=_cparams(("arbitrary", "arbitrary")),
        name="deltanet",
    )(q, k, v, z, gcol, grow, norm_w)


def _mixout_body(ydn_ref, u_ref, vln_ref, x_ref, ws_ref, bsp_ref, wout_ref, g1_ref, b1_ref, wrt_ref, brt_ref,
                 h_ref, hrow_ref, ids_ref, wts_ref, ycat_ref, *, tm):
    C = SGU_CHUNK
    ii = _iota2((C, C), 0)
    jj = _iota2((C, C), 1)
    ycat_ref[:, 0:DN_WIDTH] = ydn_ref[0]
    for g in range(SGU_GROUPS):
        gs = slice(g * C, (g + 1) * C)
        wsg = jnp.where(ii >= jj, ws_ref[g], 0.0).astype(bf16)
        for c in range(tm // C):
            rs = slice(c * C, (c + 1) * C)
            mixed = jnp.dot(wsg, vln_ref[0, rs, gs], preferred_element_type=f32) + bsp_ref[:, gs]
            ycat_ref[rs, DN_WIDTH + g * C:DN_WIDTH + (g + 1) * C] = (u_ref[0, rs, gs].astype(f32) * mixed).astype(bf16)

    RB = 128
    blocks = [slice(r, r + RB) for r in range(0, tm, RB)]
    mix = [jnp.dot(ycat_ref[rb, :], wout_ref[...], preferred_element_type=f32) for rb in blocks]
    h1s = []
    for rb, m in zip(blocks, mix):
        hp = DEEPNORM_ALPHA * x_ref[0, rb, :] + m
        mu = jnp.mean(hp, axis=-1, keepdims=True)
        hc = hp - mu
        var = jnp.mean(hc * hc, axis=-1, keepdims=True)
        h1 = hc * lax.rsqrt(var + LN_EPS) * g1_ref[...] + b1_ref[...]
        h_ref[0, rb, :] = h1
        h1b = h1.astype(bf16)
        hrow_ref[rb] = h1b.reshape(RB, ROW_TILE, 128)
        h1s.append(h1b)

    logit_blocks = [lax.dot_general(wrt_ref[...], hb, NT_DIMS, preferred_element_type=f32) + brt_ref[...] for hb in h1s]
    sub = _iota2((8, RB), 0)
    subf = sub.astype(f32)
    for rb, logits in zip(blocks, logit_blocks):
        gl = logits[0:8]
        gmax = jnp.max(gl, axis=0, keepdims=True)
        g_idx = jnp.min(jnp.where(gl == gmax, subf, float(MOE_GROUPS)), axis=0, keepdims=True)
        p_group = 1.0 / jnp.sum(jnp.exp(gl - gmax), axis=0, keepdims=True)
        within = jnp.zeros((8, RB), f32)
        for g in range(MOE_GROUPS):
            within = within + jnp.where(g_idx == float(g), logits[8 + 8 * g:16 + 8 * g], 0.0)
        m1 = jnp.max(within, axis=0, keepdims=True)
        i1 = jnp.min(jnp.where(within == m1, subf, float(EXPERTS_PER_GROUP)), axis=0, keepdims=True)
        rest = jnp.where(subf == i1, -jnp.inf, within)
        m2 = jnp.max(rest, axis=0, keepdims=True)
        i2 = jnp.min(jnp.where(rest == m2, subf, float(EXPERTS_PER_GROUP)), axis=0, keepdims=True)
        e = jnp.exp(m2 - m1)
        w1 = p_group / (1.0 + e)
        w2 = p_group * e / (1.0 + e)
        e1 = g_idx * float(EXPERTS_PER_GROUP) + i1
        e2 = g_idx * float(EXPERTS_PER_GROUP) + i2
        ids_ref[:, rb] = jnp.where(sub == 0, e1, jnp.where(sub == 1, e2, 0.0)).astype(i32)
        wts_ref[:, rb] = jnp.where(sub == 0, w1, jnp.where(sub == 1, w2, 0.0))


def _stage_mixout(ydn, u, vln, x, ws, bsp, wout, g1, b1, wrt, brt, *, tm):
    B, T, _ = x.shape
    nt = T // tm
    act_spec = lambda: pl.BlockSpec((1, tm, DN_WIDTH), lambda b, t: (b, t, 0))
    const2 = lambda shp: pl.BlockSpec(shp, lambda b, t: (0, 0))
    tok_spec = lambda: pl.BlockSpec((8, tm), lambda b, t: (0, b * nt + t))
    return pl.pallas_call(
        functools.partial(_mixout_body, tm=tm),
        grid=(B, nt),
        in_specs=[act_spec(), act_spec(), act_spec(),
                  pl.BlockSpec((1, tm, D_MODEL), lambda b, t: (b, t, 0)),
                  pl.BlockSpec((SGU_GROUPS, SGU_CHUNK, SGU_CHUNK), lambda b, t: (0, 0, 0)),
                  const2((SGU_CHUNK, SGU_WIDTH)),
                  const2((D_MODEL, D_MODEL)),
                  const2((1, D_MODEL)), const2((1, D_MODEL)),
                  const2((128, D_MODEL)), const2((128, 128))],
        out_specs=[pl.BlockSpec((1, tm, D_MODEL), lambda b, t: (b, t, 0)),
                   pl.BlockSpec((tm, ROW_TILE, 128), lambda b, t: (b * nt + t, 0, 0)), tok_spec(), tok_spec()],
        out_shape=[jax.ShapeDtypeStruct((B, T, D_MODEL), f32),
                   jax.ShapeDtypeStruct((B * T, ROW_TILE, 128), bf16),
                   jax.ShapeDtypeStruct((8, B * T), i32),
                   jax.ShapeDtypeStruct((8, B * T), f32)],
        scratch_shapes=[pltpu.VMEM((tm, D_MODEL), bf16)],
        compiler_params=_cparams(("arbitrary", "arbitrary")),
        name="mixout",
    )(ydn, u, vln, x, ws, bsp, wout, g1, b1, wrt, brt)


def _route_body(ids_ref, dest_ref, meta_ref, blk_ref, base_ref, pstart_ref, *, tm, nb_pad):
    phase = pl.program_id(0)
    i = pl.program_id(1)
    sub = _iota2((N_EXPERTS, tm), 0)
    is1 = sub == ids_ref[0:1, :]
    is2 = sub == ids_ref[1:2, :]
    oh = (is1.astype(f32) + is2.astype(f32)).astype(bf16)
    counts = jnp.dot(oh, jnp.ones((tm, 128), bf16), preferred_element_type=f32)

    @pl.when((phase == 0) & (i == 0))
    def _():
        base_ref[...] = jnp.zeros(base_ref.shape, f32)

    @pl.when(phase == 0)
    def _():
        base_ref[...] = base_ref[...] + counts

    @pl.when((phase == 1) & (i == 0))
    def _():
        cnt = base_ref[...]
        padded = jnp.floor((cnt + (MOE_BLOCK - 1)) * (1.0 / MOE_BLOCK)) * MOE_BLOCK
        ei = _iota2((N_EXPERTS, N_EXPERTS), 0)
        ej = _iota2((N_EXPERTS, N_EXPERTS), 1)
        pends = jnp.dot((ei >= ej).astype(f32), padded, precision=HIGHEST, preferred_element_type=f32)
        pstart = pends - padded
        pstart_ref[...] = pstart
        s64 = _iota2((N_EXPERTS, 128), 0)
        l64 = _iota2((N_EXPERTS, 128), 1)
        diag = s64 == l64
        fill_off = jnp.sum(jnp.where(diag, pstart + cnt, 0.0), axis=0, keepdims=True)
        fill_n = jnp.sum(jnp.where(diag, padded - cnt, 0.0), axis=0, keepdims=True)
        nused = pends[N_EXPERTS - 1:N_EXPERTS, :] * (1.0 / MOE_BLOCK)
        m8 = _iota2((8, 128), 0)
        meta_ref[...] = jnp.where(m8 == 0, fill_off, jnp.where(m8 == 1, fill_n, jnp.where(m8 == 2, nused, 0.0))).astype(i32)
        bstart = (_iota2((N_EXPERTS, nb_pad), 1) * MOE_BLOCK).astype(f32)
        pe = jnp.concatenate([pends] * (nb_pad // 128), axis=1)
        be = jnp.sum((pe <= bstart).astype(f32), axis=0, keepdims=True)
        be = jnp.minimum(be, float(N_EXPERTS - 1))
        blk_ref[...] = jnp.broadcast_to(be, (8, nb_pad)).astype(i32)

    @pl.when(phase == 1)
    def _():
        ti = _iota2((tm, tm), 0)
        tj = _iota2((tm, tm), 1)
        before = (ti < tj).astype(bf16)
        prefix = jnp.dot(oh, before, preferred_element_type=f32)
        nxt = prefix + jnp.concatenate([pstart_ref[...]] * (tm // 128), axis=1)
        d1 = jnp.sum(jnp.where(is1, nxt, 0.0), axis=0, keepdims=True)
        d2 = jnp.sum(jnp.where(is2, nxt, 0.0), axis=0, keepdims=True)
        sub8 = _iota2((8, tm), 0)
        dest_ref[...] = jnp.where(sub8 == 0, d1, jnp.where(sub8 == 1, d2, 0.0)).astype(i32)
        pstart_ref[...] = pstart_ref[...] + counts


def _stage_route(ids, *, tm, nb_pad):
    n = ids.shape[1]
    return pl.pallas_call(
        functools.partial(_route_body, tm=tm, nb_pad=nb_pad),
        grid=(2, n // tm),
        in_specs=[pl.BlockSpec((8, tm), lambda p, i: (0, i))],
        out_specs=[pl.BlockSpec((8, tm), lambda p, i: (0, i * p)),
                   pl.BlockSpec((8, 128), lambda p, i: (0, 0)),
                   pl.BlockSpec((8, nb_pad), lambda p, i: (0, 0))],
        out_shape=[jax.ShapeDtypeStruct((8, n), i32), jax.ShapeDtypeStruct((8, 128), i32),
                   jax.ShapeDtypeStruct((8, nb_pad), i32)],
        scratch_shapes=[pltpu.VMEM((N_EXPERTS, 128), f32), pltpu.VMEM((N_EXPERTS, 128), f32)],
        compiler_params=_cparams(("arbitrary", "arbitrary")),
        name="moe_route",
    )(ids)


def _dispatch_body(fill_off_ref, fill_n_ref, nused_ref, dest_ref, h3_ref, xs_ref, zero_ref, sem, zsem, *, tm):
    def row_copy(t, d):
        return pltpu.make_async_copy(h3_ref.at[t], xs_ref.at[d], sem)

    def issue(t, carry):
        row_copy(t, dest_ref[0, 0, t]).start(priority=0)
        row_copy(t, dest_ref[0, 1, t]).start(priority=1)
        return carry

    lax.fori_loop(0, tm, issue, 0, unroll=8)

    @pl.when(pl.program_id(0) == 0)
    def _():
        zero_ref[...] = jnp.zeros(zero_ref.shape, bf16)

        def fill(start):
            def body(e, carry):
                off = fill_off_ref[e]
                npad = fill_n_ref[e]
                bit = MOE_BLOCK // 2
                while bit:
                    @pl.when((npad & bit) != 0)
                    def _(off=off, bit=bit):
                        cp = pltpu.make_async_copy(zero_ref.at[pl.ds(0, bit)], xs_ref.at[pl.ds(off, bit)], zsem)
                        cp.start() if start else cp.wait()
                    off = off + (npad & bit)
                    bit //= 2
                return carry
            return body

        lax.fori_loop(0, N_EXPERTS, fill(True), 0)
        lax.fori_loop(0, N_EXPERTS, fill(False), 0)

        def tail_copy(b):
            return pltpu.make_async_copy(zero_ref, xs_ref.at[pl.ds(b * MOE_BLOCK, MOE_BLOCK)], zsem)

        nblocks = xs_ref.shape[0] // MOE_BLOCK
        lax.fori_loop(nused_ref[0], nblocks, lambda b, c: (tail_copy(b).start(), c)[1], 0)
        lax.fori_loop(nused_ref[0], nblocks, lambda b, c: (tail_copy(0).wait(), c)[1], 0)

    for _ in range(2):
        pltpu.make_async_copy(h3_ref, xs_ref.at[pl.ds(0, tm)], sem).wait()


def _stage_dispatch(fill_off, fill_n, nused, dest3, hrow, p_rows, *, tm):
    n = hrow.shape[0]
    return pl.pallas_call(
        functools.partial(_dispatch_body, tm=tm),
        grid_spec=pltpu.PrefetchScalarGridSpec(
            num_scalar_prefetch=3,
            grid=(n // tm,),
            in_specs=[pl.BlockSpec((1, 2, tm), lambda i, fo, fn, nu: (i, 0, 0), memory_space=pltpu.SMEM),
                      pl.BlockSpec((tm, ROW_TILE, 128), lambda i, fo, fn, nu: (i, 0, 0))],
            out_specs=pl.BlockSpec(memory_space=pl.ANY),
            scratch_shapes=[pltpu.VMEM((MOE_BLOCK, ROW_TILE, 128), bf16),
                            pltpu.SemaphoreType.DMA, pltpu.SemaphoreType.DMA],
        ),
        out_shape=jax.ShapeDtypeStruct((p_rows, ROW_TILE, 128), bf16),
        compiler_params=_cparams(("arbitrary",)),
        name="moe_dispatch",
    )(fill_off, fill_n, nused, dest3, hrow)


def _experts_body(blk_ref, nused_ref, xs_ref, wg_hbm, wu_hbm, wd_hbm, ys_ref,
                  wg32_ref, wu32_ref, wd32_ref, wgu16_ref, wd16_ref, wsem):
    i = pl.program_id(0)
    nused = nused_ref[0]
    used = i < nused
    e = blk_ref[i]

    def weight_copies(ex):
        return (pltpu.make_async_copy(wg_hbm.at[ex], wg32_ref, wsem.at[0]),
                pltpu.make_async_copy(wu_hbm.at[ex], wu32_ref, wsem.at[1]),
                pltpu.make_async_copy(wd_hbm.at[ex], wd32_ref, wsem.at[2]))

    @pl.when((i == 0) & used)
    def _():
        for cp in weight_copies(e):
            cp.start()

    @pl.when(used & ((i == 0) | (e != blk_ref[jnp.maximum(i - 1, 0)])))
    def _():
        for cp in weight_copies(e):
            cp.wait()
        wgu16_ref[:, 0:D_EXPERT] = wg32_ref[...].astype(bf16)
        wgu16_ref[:, D_EXPERT:2 * D_EXPERT] = wu32_ref[...].astype(bf16)
        wd16_ref[...] = wd32_ref[...].astype(bf16)
        nxt = lax.while_loop(lambda j: (j < nused) & (blk_ref[jnp.minimum(j, nused - 1)] == e), lambda j: j + 1, i + 1)

        @pl.when(nxt < nused)
        def _():
            for cp in weight_copies(blk_ref[jnp.minimum(nxt, nused - 1)]):
                cp.start()

    @pl.when(used)
    def _():
        half = MOE_BLOCK // 2
        rows = [slice(p * half, (p + 1) * half) for p in range(2)]
        gu = [jnp.dot(xs_ref[r].reshape(half, D_MODEL), wgu16_ref[...], preferred_element_type=f32) for r in rows]
        hid = [(_silu(g[:, :D_EXPERT]) * g[:, D_EXPERT:]).astype(bf16) for g in gu]
        y = [jnp.dot(hd, wd16_ref[...], preferred_element_type=f32) for hd in hid]
        for r, yp in zip(rows, y):
            ys_ref[r] = yp.astype(bf16).reshape(half, ROW_TILE, 128)

    @pl.when(jnp.logical_not(used))
    def _():
        ys_ref[...] = jnp.zeros(ys_ref.shape, bf16)


def _stage_experts(blk_e, nused, xs, w_gate, w_up, w_down):
    p_rows = xs.shape[0]
    nb = p_rows // MOE_BLOCK

    def last_used(i, nu):
        return jnp.maximum(jnp.minimum(i, nu[0] - 1), 0)

    def x_map(i, blk, nu):
        return (last_used(i, nu), 0, 0)

    def row_map(i, blk, nu):
        return (i, 0, 0)

    return pl.pallas_call(
        _experts_body,
        grid_spec=pltpu.PrefetchScalarGridSpec(
            num_scalar_prefetch=2,
            grid=(nb,),
            in_specs=[pl.BlockSpec((MOE_BLOCK, ROW_TILE, 128), x_map),
                      pl.BlockSpec(memory_space=pl.ANY),
                      pl.BlockSpec(memory_space=pl.ANY),
                      pl.BlockSpec(memory_space=pl.ANY)],
            out_specs=pl.BlockSpec((MOE_BLOCK, ROW_TILE, 128), row_map),
            scratch_shapes=[pltpu.VMEM((D_MODEL, D_EXPERT), f32), pltpu.VMEM((D_MODEL, D_EXPERT), f32),
                            pltpu.VMEM((D_EXPERT, D_MODEL), f32),
                            pltpu.VMEM((D_MODEL, 2 * D_EXPERT), bf16), pltpu.VMEM((D_EXPERT, D_MODEL), bf16),
                            pltpu.SemaphoreType.DMA((3,))],
        ),
        out_shape=jax.ShapeDtypeStruct((p_rows, ROW_TILE, 128), bf16),
        compiler_params=_cparams(("arbitrary",)),
        name="moe_experts",
    )(blk_e, nused, xs, w_gate, w_up, w_down)


def _combine_body(dcur_ref, dnext_ref, h_ref, wts_ref, g2_ref, b2_ref, ys_ref, o_ref, ybuf_ref, sem, *, tm, nsteps):
    i = pl.program_id(0)
    slot = lax.rem(i, 2)

    def issue_tile(d_ref, s):
        def body(t, carry):
            for k in range(2):
                pltpu.make_async_copy(ys_ref.at[d_ref[0, k, t]], ybuf_ref.at[s, k, t], sem.at[s]).start(priority=k)
            return carry

        lax.fori_loop(0, tm, body, 0, unroll=8)

    @pl.when(i == 0)
    def _():
        issue_tile(dcur_ref, 0)

    @pl.when(i + 1 < nsteps)
    def _():
        issue_tile(dnext_ref, 1 - slot)

    for k in range(2):
        pltpu.make_async_copy(ys_ref.at[pl.ds(0, tm)], ybuf_ref.at[slot, k], sem.at[slot]).wait()

    pieces = []
    for c in range(tm // 128):
        ls = slice(c * 128, (c + 1) * 128)
        w1c = jnp.broadcast_to(wts_ref[0:1, ls], (128, 128)).T
        w2c = jnp.broadcast_to(wts_ref[1:2, ls], (128, 128)).T
        w1f = jnp.concatenate([w1c] * (D_MODEL // 128), axis=1)
        w2f = jnp.concatenate([w2c] * (D_MODEL // 128), axis=1)
        y1 = ybuf_ref[slot, 0, ls].reshape(128, D_MODEL).astype(f32)
        y2 = ybuf_ref[slot, 1, ls].reshape(128, D_MODEL).astype(f32)
        pieces.append(w1f * y1 + w2f * y2)
    ffn = jnp.concatenate(pieces, axis=0)
    hp = DEEPNORM_ALPHA * h_ref[...] + ffn
    mu = jnp.mean(hp, axis=-1, keepdims=True)
    hc = hp - mu
    var = jnp.mean(hc * hc, axis=-1, keepdims=True)
    o_ref[...] = hc * lax.rsqrt(var + LN_EPS) * g2_ref[...] + b2_ref[...]


def _stage_combine(dest3, h2, wts, g2, b2, ys, *, tm):
    n = h2.shape[0]
    nsteps = n // tm
    return pl.pallas_call(
        functools.partial(_combine_body, tm=tm, nsteps=nsteps),
        grid=(nsteps,),
        in_specs=[pl.BlockSpec((1, 2, tm), lambda i: (i, 0, 0), memory_space=pltpu.SMEM),
                  pl.BlockSpec((1, 2, tm), lambda i: (jnp.minimum(i + 1, nsteps - 1), 0, 0), memory_space=pltpu.SMEM),
                  pl.BlockSpec((tm, D_MODEL), lambda i: (i, 0)),
                  pl.BlockSpec((8, tm), lambda i: (0, i)),
                  pl.BlockSpec((1, D_MODEL), lambda i: (0, 0)),
                  pl.BlockSpec((1, D_MODEL), lambda i: (0, 0)),
                  pl.BlockSpec(memory_space=pl.ANY)],
        out_specs=pl.BlockSpec((tm, D_MODEL), lambda i: (i, 0)),
        out_shape=jax.ShapeDtypeStruct((n, D_MODEL), f32),
        scratch_shapes=[pltpu.VMEM((2, 2, tm, ROW_TILE, 128), bf16), pltpu.SemaphoreType.DMA((2,))],
        compiler_params=_cparams(("arbitrary",)),
        name="moe_combine",
    )(dest3, dest3, h2, wts, g2, b2, ys)


def _layer(h, w_in, conv_w, a_log, dt_bias, dn_norm_w, sgu_ln_g, sgu_ln_b, w_spatial, b_spatial, w_out,
           ln1_g, ln1_b, w_rg, b_rg, w_re, b_re, w_gate, w_up, w_down, ln2_g, ln2_b,
           *, tm_in, in_groups, dn_rows, dn_chunks, dn_groups, tm_mix, tm_rank, tm_disp, tm_comb):
    B, T, _ = h.shape
    n = B * T
    qkvz = 4 * DN_WIDTH
    ba0 = qkvz
    uv0 = qkvz + 2 * DN_HEADS
    w_cols = jnp.zeros((D_MODEL, IN_COLS_ALIGNED), bf16)
    w_cols = lax.dynamic_update_slice(w_cols, w_in[:, :qkvz].astype(bf16), (0, 0))
    w_cols = lax.dynamic_update_slice(w_cols, w_in[:, uv0:].astype(bf16), (0, qkvz))
    w_cols = lax.dynamic_update_slice(w_cols, w_in[:, ba0:uv0].astype(bf16), (0, qkvz + 2 * SGU_WIDTH))
    decay_prm = jnp.stack([a_log, dt_bias])
    pcol = jnp.pad(decay_prm, ((0, 6), (DN_HEADS, 128 - 2 * DN_HEADS)))
    prow = jnp.broadcast_to(jnp.pad(decay_prm, ((0, 0), (DN_HEADS, 8 - 2 * DN_HEADS)))[:, :, None], (2, 8, 128))

    q, k, v, z, u, vln, gcol, grow = _stage_inproj(
        h, w_cols, conv_w, pcol, prow, sgu_ln_g[None, :], sgu_ln_b[None, :], tm=tm_in, ngroups=in_groups)
    ydn = _stage_deltanet(q, k, v, z, gcol, grow, dn_norm_w[None, :], nbr=dn_rows, nch=dn_chunks, ngroups=dn_groups)

    bsp = jnp.broadcast_to(b_spatial.T[:, :, None], (SGU_CHUNK, SGU_GROUPS, SGU_CHUNK)).reshape(SGU_CHUNK, SGU_WIDTH)
    n_logit = MOE_GROUPS + N_EXPERTS
    wrt = jnp.pad(jnp.concatenate([w_rg, w_re], axis=1).T, ((0, 128 - n_logit), (0, 0))).astype(bf16)
    brt = jnp.broadcast_to(jnp.pad(jnp.concatenate([b_rg, b_re]), (0, 128 - n_logit))[:, None], (128, 128))
    h1, hrow, ids, wts = _stage_mixout(ydn, u, vln, h, w_spatial, bsp, w_out.astype(bf16), ln1_g[None, :],
                                       ln1_b[None, :], wrt, brt, tm=tm_mix)

    p_rows = (-(-(n * 2) // MOE_BLOCK)) * MOE_BLOCK + N_EXPERTS * MOE_BLOCK
    nb = p_rows // MOE_BLOCK
    nb_pad = (-(-nb // 128)) * 128
    dest, meta, blk = _stage_route(ids, tm=tm_rank, nb_pad=nb_pad)

    h2 = h1.reshape(n, D_MODEL)
    dest_d = dest[0:2].reshape(2, n // tm_disp, tm_disp).transpose(1, 0, 2)
    xs = _stage_dispatch(meta[0, :N_EXPERTS], meta[1, :N_EXPERTS], meta[2, 0:1], dest_d, hrow, p_rows, tm=tm_disp)
    ys = _stage_experts(blk[0, :nb], meta[2, 0:1], xs, w_gate, w_up, w_down)
    dest_c = dest[0:2].reshape(2, n // tm_comb, tm_comb).transpose(1, 0, 2)
    out = _stage_combine(dest_c, h2, wts, ln2_g[None, :], ln2_b[None, :], ys, tm=tm_comb)
    return out.reshape(B, T, D_MODEL)


def kernel(x, w_in, conv_w, a_log, dt_bias, dn_norm_w, sgu_ln_g, sgu_ln_b, w_spatial, b_spatial, w_out, ln1_g, ln1_b, w_router_group, b_router_group, w_router_expert, b_router_expert, w_gate, w_up, w_down, ln2_g, ln2_b):
    h = x
    for l in range(w_in.shape[0]):
        h = _layer(h, w_in[l], conv_w[l], a_log[l], dt_bias[l], dn_norm_w[l], sgu_ln_g[l], sgu_ln_b[l],
                   w_spatial[l], b_spatial[l], w_out[l], ln1_g[l], ln1_b[l],
                   w_router_group[l], b_router_group[l], w_router_expert[l], b_router_expert[l],
                   w_gate[l], w_up[l], w_down[l], ln2_g[l], ln2_b[l],
                   tm_in=512, in_groups=1, dn_rows=4, dn_chunks=2, dn_groups=2, tm_mix=512, tm_rank=1024, tm_disp=2048, tm_comb=512)
    return h
```

```python
import functools

import jax
import jax.numpy as jnp
from jax import lax
from jax.experimental import pallas as pl
from jax.experimental.pallas import tpu as pltpu

f32 = jnp.float32
bf16 = jnp.bfloat16
i32 = jnp.int32

D_MODEL = 1024
DN_WIDTH = 512
DN_HEADS = 4
HEAD_DIM = 128
CONV_K = 4
SGU_WIDTH = 512
SGU_GROUPS = 4
SGU_CHUNK = 128
DN_CHUNK = 128
MOE_GROUPS = 8
EXPERTS_PER_GROUP = 8
N_EXPERTS = 64
D_EXPERT = 512
MOE_BLOCK = 256
IN_COLS_ALIGNED = 4 * DN_WIDTH + 2 * SGU_WIDTH + 128
ROW_TILE = D_MODEL // 128
DEEPNORM_ALPHA = 2.0 ** 0.25
LN_EPS = 1e-5
RMS_EPS = 1e-6
HIGHEST = lax.Precision.HIGHEST
VMEM_LIMIT_BYTES = 56 * 1024 * 1024

NT_DIMS = (((1,), (1,)), ((), ()))


def _cparams(sem, flags=None):
    return pltpu.CompilerParams(dimension_semantics=sem, vmem_limit_bytes=VMEM_LIMIT_BYTES, flags=flags)


def _sigmoid(x):
    return 1.0 / (1.0 + jnp.exp(-x))


def _silu(x):
    h = 0.5 * x
    return h + h * jnp.tanh(h)


def _softplus(x):
    return jnp.maximum(x, 0.0) + jnp.log1p(jnp.exp(-jnp.abs(x)))


def _gelu_tanh(x):
    c = 0.7978845608028654
    return x * (0.5 * (1.0 + jnp.tanh(c * (x + 0.044715 * (x * x * x)))))


def _iota2(shape, axis):
    return lax.broadcasted_iota(i32, shape, axis)


def _inproj_body(x_ref, w_ref, convw_ref, pcol_ref, prow_ref, lng_ref, lnb_ref, ones_ref,
                 q_ref, k_ref, v_ref, z_ref, u_ref, vln_ref, gcol_ref, grow_ref, *ext_refs, tm, ngroups):
    W = DN_WIDTH
    gm = tm // ngroups
    ext = [ext_refs[3 * g:3 * g + 3] for g in range(ngroups)]

    @pl.when(pl.program_id(1) == 0)
    def _():
        for e_ref in ext[0]:
            e_ref[0:8, :] = jnp.zeros((8, W), f32)

    for g in range(ngroups):
        _inproj_rows(x_ref, w_ref, convw_ref, pcol_ref, prow_ref, lng_ref, lnb_ref, ones_ref,
                     q_ref, k_ref, v_ref, z_ref, u_ref, vln_ref, gcol_ref, grow_ref, ext[g],
                     ext[(g + 1) % ngroups], r0=g * gm, gm=gm)


def _inproj_rows(x_ref, w_ref, convw_ref, pcol_ref, prow_ref, lng_ref, lnb_ref, ones_ref,
                 q_ref, k_ref, v_ref, z_ref, u_ref, vln_ref, gcol_ref, grow_ref, ext, ext_next, *, r0, gm):
    W = DN_WIDTH
    rows = slice(r0, r0 + gm)
    xb = x_ref[0, rows, :].astype(bf16)
    for part, e_ref in enumerate(ext):
        e_ref[8:8 + gm, :] = jnp.dot(xb, w_ref[:, part * W:(part + 1) * W], preferred_element_type=f32)
    zc = 3 * W
    uc = zc + W
    vc = uc + SGU_WIDTH
    bc = vc + SGU_WIDTH
    pz = jnp.dot(xb, w_ref[:, zc:zc + W], preferred_element_type=f32)
    pu = jnp.dot(xb, w_ref[:, uc:uc + SGU_WIDTH], preferred_element_type=f32)
    pv = jnp.dot(xb, w_ref[:, vc:vc + SGU_WIDTH], preferred_element_type=f32)
    pba = jnp.dot(xb, w_ref[:, bc:bc + 128], preferred_element_type=f32)

    def group_sums(a):
        return jnp.dot(a.astype(bf16), ones_ref[...], preferred_element_type=f32)

    for part, (e_ref, n_ref, out_ref) in enumerate(zip(ext, ext_next, (q_ref, k_ref, v_ref))):
        cs = slice(part * W, (part + 1) * W)
        y = convw_ref[3:4, cs] * e_ref[8:8 + gm, :]
        for j in range(CONV_K - 1):
            y = y + convw_ref[j:j + 1, cs] * e_ref[5 + j:5 + j + gm, :]
        y = _silu(y)
        if part < 2:
            scale = HEAD_DIM ** -0.5 if part == 0 else 1.0
            y = y * (lax.rsqrt(group_sums(y * y) + RMS_EPS) * scale)
        out_ref[0, rows, :] = y.astype(bf16)
        n_ref[0:8, :] = e_ref[gm:gm + 8, :]

    z_ref[0, rows, :] = _silu(pz).astype(bf16)

    u_ref[0, rows, :] = _gelu_tanh(pu).astype(bf16)
    pv = _gelu_tanh(pv)
    for g in range(SGU_GROUPS):
        sl = slice(g * SGU_CHUNK, (g + 1) * SGU_CHUNK)
        vg = pv[:, sl]
        mu = jnp.mean(vg, axis=-1, keepdims=True)
        vcn = vg - mu
        var = jnp.mean(vcn * vcn, axis=-1, keepdims=True)
        vln_ref[0, rows, sl] = (vcn * lax.rsqrt(var + LN_EPS) * lng_ref[:, sl] + lnb_ref[:, sl]).astype(bf16)

    lane = _iota2((DN_CHUNK, 128), 1)
    is_g = (lane >= DN_HEADS) & (lane < 2 * DN_HEADS)
    beta = _sigmoid(pba)
    gval = -jnp.exp(pcol_ref[0:1, :]) * _softplus(pba + pcol_ref[1:2, :])
    ci = _iota2((DN_CHUNK, DN_CHUNK), 0)
    cj = _iota2((DN_CHUNK, DN_CHUNK), 1)
    ltri = (ci >= cj).astype(f32)
    utri = (ci <= cj).astype(f32)
    sub = _iota2((8, DN_CHUNK), 0)
    for c in range(gm // DN_CHUNK):
        rs = slice(c * DN_CHUNK, (c + 1) * DN_CHUNK)
        os_ = slice(r0 + c * DN_CHUNK, r0 + (c + 1) * DN_CHUNK)
        gc = jnp.dot(ltri, jnp.where(is_g, gval[rs], 0.0), precision=HIGHEST, preferred_element_type=f32)
        gcol_ref[0, os_, :] = jnp.where(lane < DN_HEADS, beta[rs], gc)
        pbat = pba[rs].T[0:8, :]
        gt = -jnp.exp(prow_ref[0]) * _softplus(pbat + prow_ref[1])
        gt = jnp.where(sub >= DN_HEADS, gt, 0.0)
        grow_ref[0, :, os_] = jnp.dot(gt, utri, precision=HIGHEST, preferred_element_type=f32)


def _stage_inproj(x, w_re, conv_w, pcol, prow, lng, lnb, *, tm, ngroups):
    B, T, _ = x.shape
    wcols = w_re.shape[1]
    grid = (B, T // tm)
    gi = lax.broadcasted_iota(i32, (DN_WIDTH, DN_WIDTH), 0) // 128
    gj = lax.broadcasted_iota(i32, (DN_WIDTH, DN_WIDTH), 1) // 128
    group_ones = (gi == gj).astype(bf16)
    act = lambda: jax.ShapeDtypeStruct((B, T, DN_WIDTH), bf16)
    act_spec = lambda: pl.BlockSpec((1, tm, DN_WIDTH), lambda b, t: (b, t, 0))
    const2 = lambda shp: pl.BlockSpec(shp, lambda b, t: (0, 0))
    return pl.pallas_call(
        functools.partial(_inproj_body, tm=tm, ngroups=ngroups),
        grid=grid,
        in_specs=[
            pl.BlockSpec((1, tm, D_MODEL), lambda b, t: (b, t, 0)),
            const2((D_MODEL, wcols)),
            const2((CONV_K, 3 * DN_WIDTH)),
            const2((8, 128)),
            pl.BlockSpec((2, 8, 128), lambda b, t: (0, 0, 0)),
            const2((1, SGU_WIDTH)),
            const2((1, SGU_WIDTH)),
            const2((DN_WIDTH, DN_WIDTH)),
        ],
        out_specs=[act_spec() for _ in range(6)] + [
            pl.BlockSpec((1, tm, 128), lambda b, t: (b, t, 0)),
            pl.BlockSpec((1, 8, tm), lambda b, t: (b, 0, t)),
        ],
        out_shape=[act() for _ in range(6)] + [
            jax.ShapeDtypeStruct((B, T, 128), f32),
            jax.ShapeDtypeStruct((B, 8, T), f32),
        ],
        scratch_shapes=[pltpu.VMEM((tm // ngroups + 8, DN_WIDTH), f32) for _ in range(3 * ngroups)],
        compiler_params=_cparams(("arbitrary", "arbitrary")),
        name="inproj",
    )(x, w_re, conv_w, pcol, prow, lng, lnb, group_ones)


def _mm(a, b):
    return jnp.dot(a.astype(bf16), b.astype(bf16), preferred_element_type=f32)


def _unit_lower_inverse(nmats, ii, jj):
    n = nmats[0].shape[0]
    eye = (ii == jj).astype(f32)
    leaf = jnp.right_shift(ii, 3) == jnp.right_shift(jj, 3)
    dblk = [jnp.where(leaf, m, 0.0) for m in nmats]
    s1 = [_mm(d, d) for d in dblk]
    r1 = [eye - d for d in dblk]
    both = [_mm(s, jnp.concatenate([s, r], axis=1)) for s, r in zip(s1, r1)]
    r2 = [r + bo[:, n:] for r, bo in zip(r1, both)]
    xs = [r + _mm(bo[:, :n], r) for r, bo in zip(r2, both)]
    size = 8
    while size < n:
        lows = [slice(r + size, r + 2 * size) for r in range(0, n, 2 * size)]
        ups = [slice(r, r + size) for r in range(0, n, 2 * size)]
        rsel = _iota2((n // 2, n), 0)
        ilow = rsel + size * (jnp.right_shift(rsel, size.bit_length() - 1) + 1)
        jlow = _iota2((n // 2, n), 1)
        in_pair_upper = (jlow >= ilow - (ilow & (2 * size - 1))) & (jlow < ilow - (ilow & (size - 1)))
        zeros = jnp.zeros((size, n), f32)
        new_xs = []
        ylows = [_mm(jnp.where(in_pair_upper, jnp.concatenate([m[s] for s in lows], axis=0), 0.0), x)
                 for m, x in zip(nmats, xs)]
        yfull = [jnp.concatenate([piece for k in range(len(lows)) for piece in (zeros, y[k * size:(k + 1) * size])], axis=0)
                 for y in ylows]
        corr = [_mm(jnp.concatenate([x[s] for s in lows], axis=0), yf) for x, yf in zip(xs, yfull)]
        for x, c in zip(xs, corr):
            pieces = []
            for k, (u, l) in enumerate(zip(ups, lows)):
                pieces += [x[u], x[l] - c[k * size:(k + 1) * size]]
            new_xs.append(jnp.concatenate(pieces, axis=0))
        xs = new_xs
        size *= 2
    return xs


def _deltanet_body(q_ref, k_ref, v_ref, z_ref, gcol_ref, grow_ref, nw_ref, y_ref, s_ref, *, nbr, nch, ngroups):
    @pl.when(pl.program_id(1) == 0)
    def _():
        s_ref[...] = jnp.zeros(s_ref.shape, f32)

    per = nbr // ngroups
    for g in range(ngroups):
        _deltanet_rows(q_ref, k_ref, v_ref, z_ref, gcol_ref, grow_ref, nw_ref, y_ref, s_ref,
                       rows=range(g * per, (g + 1) * per), nch=nch)


def _deltanet_rows(q_ref, k_ref, v_ref, z_ref, gcol_ref, grow_ref, nw_ref, y_ref, s_ref, *, rows, nch):
    C = DN_CHUNK
    S = [(b, h) for b in rows for h in range(DN_HEADS)]
    P = [(b, c, h) for c in range(nch) for b, h in S]

    ii = _iota2((C, C), 0)
    jj = _iota2((C, C), 1)
    causal = ii >= jj
    rs = [slice(c * C, (c + 1) * C) for c in range(nch)]
    hs = [slice(h * HEAD_DIM, (h + 1) * HEAD_DIM) for h in range(DN_HEADS)]
    gcol = {(b, c): gcol_ref[b, rs[c], :] for b in rows for c in range(nch)}
    qh = {(b, c, h): q_ref[b, rs[c], hs[h]] for b, c, h in P}
    kh = {(b, c, h): k_ref[b, rs[c], hs[h]] for b, c, h in P}
    vh = {(b, c, h): v_ref[b, rs[c], hs[h]] for b, c, h in P}
    gc_b = {(b, c, h): jnp.broadcast_to(gcol[b, c][:, DN_HEADS + h:DN_HEADS + h + 1], (C, HEAD_DIM)) for b, c, h in P}
    beta_b = {(b, c, h): jnp.broadcast_to(gcol[b, c][:, h:h + 1], (C, HEAD_DIM)) for b, c, h in P}
    gc_r = {(b, c, h): jnp.broadcast_to(grow_ref[b, DN_HEADS + h:DN_HEADS + h + 1, rs[c]], (C, C)) for b, c, h in P}
    decay = {p: jnp.exp(jnp.where(causal, gc_b[p] - gc_r[p], -1e30)) for p in P}

    kf = {p: kh[p].astype(f32) for p in P}
    kb = {p: kf[p] * beta_b[p] for p in P}
    kk = {p: lax.dot_general(kb[p].astype(bf16), kh[p], NT_DIMS, preferred_element_type=f32) for p in P}
    a_intra = {p: lax.dot_general(qh[p], kh[p], NT_DIMS, preferred_element_type=f32) * decay[p] for p in P}
    nmat = [jnp.where(ii > jj, kk[p] * decay[p], 0.0) for p in P]
    tinv = dict(zip(P, _unit_lower_inverse(nmat, ii, jj)))

    eg = {p: jnp.exp(gc_b[p]) for p in P}
    rhs = {p: jnp.concatenate([vh[p].astype(f32) * beta_b[p], kb[p] * eg[p]], axis=1) for p in P}
    sol = {p: _mm(tinv[p], rhs[p]) for p in P}

    q_dec = {p: qh[p].astype(f32) * eg[p] for p in P}
    g_last = {p: gc_b[p][C - 1:C, :] for p in P}
    kdt = {p: (kf[p] * jnp.exp(g_last[p] - gc_b[p])).T for p in P}

    state = {(b, h): s_ref[b * DN_HEADS + h] for b, h in S}
    for c in range(nch):
        m1 = {(b, h): _mm(jnp.concatenate([sol[b, c, h][:, HEAD_DIM:], q_dec[b, c, h]], axis=0), state[b, h]) for b, h in S}
        v_new = {(b, h): sol[b, c, h][:, :HEAD_DIM] - m1[b, h][:C] for b, h in S}
        m2 = {(b, h): _mm(jnp.concatenate([a_intra[b, c, h], kdt[b, c, h]], axis=0), v_new[b, h]) for b, h in S}
        state = {(b, h): state[b, h] * jnp.exp(g_last[b, c, h]) + m2[b, h][C:] for b, h in S}
        for b, h in S:
            o = m1[b, h][C:] + m2[b, h][:C]
            rms = lax.rsqrt(jnp.mean(o * o, axis=-1, keepdims=True) + RMS_EPS)
            y_ref[b, rs[c], hs[h]] = (o * rms * nw_ref[...] * z_ref[b, rs[c], hs[h]].astype(f32)).astype(bf16)
    for b, h in S:
        s_ref[b * DN_HEADS + h] = state[b, h]


def _stage_deltanet(q, k, v, z, gcol, grow, norm_w, *, nbr, nch, ngroups):
    B, T, _ = q.shape
    tt = nch * DN_CHUNK
    act_spec = lambda: pl.BlockSpec((nbr, tt, DN_WIDTH), lambda b, t: (b, t, 0))
    return pl.pallas_call(
        functools.partial(_deltanet_body, nbr=nbr, nch=nch, ngroups=ngroups),
        grid=(B // nbr, T // tt),
        in_specs=[act_spec(), act_spec(), act_spec(), act_spec(),
                  pl.BlockSpec((nbr, tt, 128), lambda b, t: (b, t, 0)),
                  pl.BlockSpec((nbr, 8, tt), lambda b, t: (b, 0, t)),
                  pl.BlockSpec((1, HEAD_DIM), lambda b, t: (0, 0))],
        out_specs=act_spec(),
        out_shape=jax.ShapeDtypeStruct((B, T, DN_WIDTH), bf16),
        scratch_shapes=[pltpu.VMEM((nbr * DN_HEADS, HEAD_DIM, HEAD_DIM), f32)],
        compiler_params=_cparams(("arbitrary", "arbitrary")),
        name="deltanet",
    )(q, k, v, z, gcol, grow, norm_w)


def _mixout_body(ydn_ref, u_ref, vln_ref, x_ref, ws_ref, bsp_ref, wout_ref, g1_ref, b1_ref, wrt_ref, brt_ref,
                 h_ref, hrow_ref, ids_ref, wts_ref, ycat_ref, *, tm):
    C = SGU_CHUNK
    ii = _iota2((C, C), 0)
    jj = _iota2((C, C), 1)
    ycat_ref[:, 0:DN_WIDTH] = ydn_ref[0]
    for g in range(SGU_GROUPS):
        gs = slice(g * C, (g + 1) * C)
        wsg = jnp.where(ii >= jj, ws_ref[g], 0.0).astype(bf16)
        for c in range(tm // C):
            rs = slice(c * C, (c + 1) * C)
            mixed = jnp.dot(wsg, vln_ref[0, rs, gs], preferred_element_type=f32) + bsp_ref[:, gs]
            ycat_ref[rs, DN_WIDTH + g * C:DN_WIDTH + (g + 1) * C] = (u_ref[0, rs, gs].astype(f32) * mixed).astype(bf16)

    RB = 128
    blocks = [slice(r, r + RB) for r in range(0, tm, RB)]
    mix = [jnp.dot(ycat_ref[rb, :], wout_ref[...], preferred_element_type=f32) for rb in blocks]
    h1s = []
    for rb, m in zip(blocks, mix):
        hp = DEEPNORM_ALPHA * x_ref[0, rb, :] + m
        mu = jnp.mean(hp, axis=-1, keepdims=True)
        hc = hp - mu
        var = jnp.mean(hc * hc, axis=-1, keepdims=True)
        h1 = hc * lax.rsqrt(var + LN_EPS) * g1_ref[...] + b1_ref[...]
        h_ref[0, rb, :] = h1
        h1b = h1.astype(bf16)
        hrow_ref[rb] = h1b.reshape(RB, ROW_TILE, 128)
        h1s.append(h1b)

    logit_blocks = [lax.dot_general(wrt_ref[...], hb, NT_DIMS, preferred_element_type=f32) + brt_ref[...] for hb in h1s]
    sub = _iota2((8, RB), 0)
    subf = sub.astype(f32)
    for rb, logits in zip(blocks, logit_blocks):
        gl = logits[0:8]
        gmax = jnp.max(gl, axis=0, keepdims=True)
        g_idx = jnp.min(jnp.where(gl == gmax, subf, float(MOE_GROUPS)), axis=0, keepdims=True)
        p_group = 1.0 / jnp.sum(jnp.exp(gl - gmax), axis=0, keepdims=True)
        within = jnp.zeros((8, RB), f32)
        for g in range(MOE_GROUPS):
            within = within + jnp.where(g_idx == float(g), logits[8 + 8 * g:16 + 8 * g], 0.0)
        m1 = jnp.max(within, axis=0, keepdims=True)
        i1 = jnp.min(jnp.where(within == m1, subf, float(EXPERTS_PER_GROUP)), axis=0, keepdims=True)
        rest = jnp.where(subf == i1, -jnp.inf, within)
        m2 = jnp.max(rest, axis=0, keepdims=True)
        i2 = jnp.min(jnp.where(rest == m2, subf, float(EXPERTS_PER_GROUP)), axis=0, keepdims=True)
        e = jnp.exp(m2 - m1)
        w1 = p_group / (1.0 + e)
        w2 = p_group * e / (1.0 + e)
        e1 = g_idx * float(EXPERTS_PER_GROUP) + i1
        e2 = g_idx * float(EXPERTS_PER_GROUP) + i2
        ids_ref[:, rb] = jnp.where(sub == 0, e1, jnp.where(sub == 1, e2, 0.0)).astype(i32)
        wts_ref[:, rb] = jnp.where(sub == 0, w1, jnp.where(sub == 1, w2, 0.0))


def _stage_mixout(ydn, u, vln, x, ws, bsp, wout, g1, b1, wrt, brt, *, tm):
    B, T, _ = x.shape
    nt = T // tm
    act_spec = lambda: pl.BlockSpec((1, tm, DN_WIDTH), lambda b, t: (b, t, 0))
    const2 = lambda shp: pl.BlockSpec(shp, lambda b, t: (0, 0))
    tok_spec = lambda: pl.BlockSpec((8, tm), lambda b, t: (0, b * nt + t))
    return pl.pallas_call(
        functools.partial(_mixout_body, tm=tm),
        grid=(B, nt),
        in_specs=[act_spec(), act_spec(), act_spec(),
                  pl.BlockSpec((1, tm, D_MODEL), lambda b, t: (b, t, 0)),
                  pl.BlockSpec((SGU_GROUPS, SGU_CHUNK, SGU_CHUNK), lambda b, t: (0, 0, 0)),
                  const2((SGU_CHUNK, SGU_WIDTH)),
                  const2((D_MODEL, D_MODEL)),
                  const2((1, D_MODEL)), const2((1, D_MODEL)),
                  const2((128, D_MODEL)), const2((128, 128))],
        out_specs=[pl.BlockSpec((1, tm, D_MODEL), lambda b, t: (b, t, 0)),
                   pl.BlockSpec((tm, ROW_TILE, 128), lambda b, t: (b * nt + t, 0, 0)), tok_spec(), tok_spec()],
        out_shape=[jax.ShapeDtypeStruct((B, T, D_MODEL), f32),
                   jax.ShapeDtypeStruct((B * T, ROW_TILE, 128), bf16),
                   jax.ShapeDtypeStruct((8, B * T), i32),
                   jax.ShapeDtypeStruct((8, B * T), f32)],
        scratch_shapes=[pltpu.VMEM((tm, D_MODEL), bf16)],
        compiler_params=_cparams(("arbitrary", "arbitrary")),
        name="mixout",
    )(ydn, u, vln, x, ws, bsp, wout, g1, b1, wrt, brt)


def _route_body(ids_ref, dest_ref, meta_ref, blk_ref, base_ref, pstart_ref, *, tm, nb_pad):
    phase = pl.program_id(0)
    i = pl.program_id(1)
    sub = _iota2((N_EXPERTS, tm), 0)
    is1 = sub == ids_ref[0:1, :]
    is2 = sub == ids_ref[1:2, :]
    oh = (is1.astype(f32) + is2.astype(f32)).astype(bf16)
    counts = jnp.dot(oh, jnp.ones((tm, 128), bf16), preferred_element_type=f32)

    @pl.when((phase == 0) & (i == 0))
    def _():
        base_ref[...] = jnp.zeros(base_ref.shape, f32)

    @pl.when(phase == 0)
    def _():
        base_ref[...] = base_ref[...] + counts

    @pl.when((phase == 1) & (i == 0))
    def _():
        cnt = base_ref[...]
        padded = jnp.floor((cnt + (MOE_BLOCK - 1)) * (1.0 / MOE_BLOCK)) * MOE_BLOCK
        ei = _iota2((N_EXPERTS, N_EXPERTS), 0)
        ej = _iota2((N_EXPERTS, N_EXPERTS), 1)
        pends = jnp.dot((ei >= ej).astype(f32), padded, precision=HIGHEST, preferred_element_type=f32)
        pstart = pends - padded
        pstart_ref[...] = pstart
        s64 = _iota2((N_EXPERTS, 128), 0)
        l64 = _iota2((N_EXPERTS, 128), 1)
        diag = s64 == l64
        fill_off = jnp.sum(jnp.where(diag, pstart + cnt, 0.0), axis=0, keepdims=True)
        fill_n = jnp.sum(jnp.where(diag, padded - cnt, 0.0), axis=0, keepdims=True)
        nused = pends[N_EXPERTS - 1:N_EXPERTS, :] * (1.0 / MOE_BLOCK)
        m8 = _iota2((8, 128), 0)
        meta_ref[...] = jnp.where(m8 == 0, fill_off, jnp.where(m8 == 1, fill_n, jnp.where(m8 == 2, nused, 0.0))).astype(i32)
        bstart = (_iota2((N_EXPERTS, nb_pad), 1) * MOE_BLOCK).astype(f32)
        pe = jnp.concatenate([pends] * (nb_pad // 128), axis=1)
        be = jnp.sum((pe <= bstart).astype(f32), axis=0, keepdims=True)
        be = jnp.minimum(be, float(N_EXPERTS - 1))
        blk_ref[...] = jnp.broadcast_to(be, (8, nb_pad)).astype(i32)

    @pl.when(phase == 1)
    def _():
        ti = _iota2((tm, tm), 0)
        tj = _iota2((tm, tm), 1)
        before = (ti < tj).astype(bf16)
        prefix = jnp.dot(oh, before, preferred_element_type=f32)
        nxt = prefix + jnp.concatenate([pstart_ref[...]] * (tm // 128), axis=1)
        d1 = jnp.sum(jnp.where(is1, nxt, 0.0), axis=0, keepdims=True)
        d2 = jnp.sum(jnp.where(is2, nxt, 0.0), axis=0, keepdims=True)
        sub8 = _iota2((8, tm), 0)
        dest_ref[...] = jnp.where(sub8 == 0, d1, jnp.where(sub8 == 1, d2, 0.0)).astype(i32)
        pstart_ref[...] = pstart_ref[...] + counts


def _stage_route(ids, *, tm, nb_pad):
    n = ids.shape[1]
    return pl.pallas_call(
        functools.partial(_route_body, tm=tm, nb_pad=nb_pad),
        grid=(2, n // tm),
        in_specs=[pl.BlockSpec((8, tm), lambda p, i: (0, i))],
        out_specs=[pl.BlockSpec((8, tm), lambda p, i: (0, i * p)),
                   pl.BlockSpec((8, 128), lambda p, i: (0, 0)),
                   pl.BlockSpec((8, nb_pad), lambda p, i: (0, 0))],
        out_shape=[jax.ShapeDtypeStruct((8, n), i32), jax.ShapeDtypeStruct((8, 128), i32),
                   jax.ShapeDtypeStruct((8, nb_pad), i32)],
        scratch_shapes=[pltpu.VMEM((N_EXPERTS, 128), f32), pltpu.VMEM((N_EXPERTS, 128), f32)],
        compiler_params=_cparams(("arbitrary", "arbitrary")),
        name="moe_route",
    )(ids)


def _dispatch_body(fill_off_ref, fill_n_ref, nused_ref, dest_ref, h3_ref, xs_ref, zero_ref, sem, zsem, *, tm):
    def row_copy(t, d):
        return pltpu.make_async_copy(h3_ref.at[t], xs_ref.at[d], sem)

    def issue(t, carry):
        row_copy(t, dest_ref[0, 0, t]).start(priority=0)
        row_copy(t, dest_ref[0, 1, t]).start(priority=1)
        return carry

    lax.fori_loop(0, tm, issue, 0, unroll=8)

    @pl.when(pl.program_id(0) == 0)
    def _():
        zero_ref[...] = jnp.zeros(zero_ref.shape, bf16)

        def fill(start):
            def body(e, carry):
                off = fill_off_ref[e]
                npad = fill_n_ref[e]
                bit = MOE_BLOCK // 2
                while bit:
                    @pl.when((npad & bit) != 0)
                    def _(off=off, bit=bit):
                        cp = pltpu.make_async_copy(zero_ref.at[pl.ds(0, bit)], xs_ref.at[pl.ds(off, bit)], zsem)
                        cp.start() if start else cp.wait()
                    off = off + (npad & bit)
                    bit //= 2
                return carry
            return body

        lax.fori_loop(0, N_EXPERTS, fill(True), 0)
        lax.fori_loop(0, N_EXPERTS, fill(False), 0)

        def tail_copy(b):
            return pltpu.make_async_copy(zero_ref, xs_ref.at[pl.ds(b * MOE_BLOCK, MOE_BLOCK)], zsem)

        nblocks = xs_ref.shape[0] // MOE_BLOCK
        lax.fori_loop(nused_ref[0], nblocks, lambda b, c: (tail_copy(b).start(), c)[1], 0)
        lax.fori_loop(nused_ref[0], nblocks, lambda b, c: (tail_copy(0).wait(), c)[1], 0)

    for _ in range(2):
        pltpu.make_async_copy(h3_ref, xs_ref.at[pl.ds(0, tm)], sem).wait()


def _stage_dispatch(fill_off, fill_n, nused, dest3, hrow, p_rows, *, tm):
    n = hrow.shape[0]
    return pl.pallas_call(
        functools.partial(_dispatch_body, tm=tm),
        grid_spec=pltpu.PrefetchScalarGridSpec(
            num_scalar_prefetch=3,
            grid=(n // tm,),
            in_specs=[pl.BlockSpec((1, 2, tm), lambda i, fo, fn, nu: (i, 0, 0), memory_space=pltpu.SMEM),
                      pl.BlockSpec((tm, ROW_TILE, 128), lambda i, fo, fn, nu: (i, 0, 0))],
            out_specs=pl.BlockSpec(memory_space=pl.ANY),
            scratch_shapes=[pltpu.VMEM((MOE_BLOCK, ROW_TILE, 128), bf16),
                            pltpu.SemaphoreType.DMA, pltpu.SemaphoreType.DMA],
        ),
        out_shape=jax.ShapeDtypeStruct((p_rows, ROW_TILE, 128), bf16),
        compiler_params=_cparams(("arbitrary",)),
        name="moe_dispatch",
    )(fill_off, fill_n, nused, dest3, hrow)


def _experts_body(blk_ref, nused_ref, xs_ref, wg_hbm, wu_hbm, wd_hbm, ys_ref,
                  wg32_ref, wu32_ref, wd32_ref, wgu16_ref, wd16_ref, wsem):
    i = pl.program_id(0)
    nused = nused_ref[0]
    used = i < nused
    e = blk_ref[i]

    def weight_copies(ex):
        return (pltpu.make_async_copy(wg_hbm.at[ex], wg32_ref, wsem.at[0]),
                pltpu.make_async_copy(wu_hbm.at[ex], wu32_ref, wsem.at[1]),
                pltpu.make_async_copy(wd_hbm.at[ex], wd32_ref, wsem.at[2]))

    @pl.when((i == 0) & used)
    def _():
        for cp in weight_copies(e):
            cp.start()

    @pl.when(used & ((i == 0) | (e != blk_ref[jnp.maximum(i - 1, 0)])))
    def _():
        for cp in weight_copies(e):
            cp.wait()
        wgu16_ref[:, 0:D_EXPERT] = wg32_ref[...].astype(bf16)
        wgu16_ref[:, D_EXPERT:2 * D_EXPERT] = wu32_ref[...].astype(bf16)
        wd16_ref[...] = wd32_ref[...].astype(bf16)
        nxt = lax.while_loop(lambda j: (j < nused) & (blk_ref[jnp.minimum(j, nused - 1)] == e), lambda j: j + 1, i + 1)

        @pl.when(nxt < nused)
        def _():
            for cp in weight_copies(blk_ref[jnp.minimum(nxt, nused - 1)]):
                cp.start(priority=1)

    @pl.when(used)
    def _():
        half = MOE_BLOCK // 2
        rows = [slice(p * half, (p + 1) * half) for p in range(2)]
        gu = [jnp.dot(xs_ref[r].reshape(half, D_MODEL), wgu16_ref[...], preferred_element_type=f32) for r in rows]
        hid = [(_silu(g[:, :D_EXPERT]) * g[:, D_EXPERT:]).astype(bf16) for g in gu]
        y = [jnp.dot(hd, wd16_ref[...], preferred_element_type=f32) for hd in hid]
        for r, yp in zip(rows, y):
            ys_ref[r] = yp.astype(bf16).reshape(half, ROW_TILE, 128)

    @pl.when(jnp.logical_not(used))
    def _():
        ys_ref[...] = jnp.zeros(ys_ref.shape, bf16)


def _stage_experts(blk_e, nused, xs, w_gate, w_up, w_down):
    p_rows = xs.shape[0]
    nb = p_rows // MOE_BLOCK

    def last_used(i, nu):
        return jnp.maximum(jnp.minimum(i, nu[0] - 1), 0)

    def x_map(i, blk, nu):
        return (last_used(i, nu), 0, 0)

    def row_map(i, blk, nu):
        return (i, 0, 0)

    return pl.pallas_call(
        _experts_body,
        grid_spec=pltpu.PrefetchScalarGridSpec(
            num_scalar_prefetch=2,
            grid=(nb,),
            in_specs=[pl.BlockSpec((MOE_BLOCK, ROW_TILE, 128), x_map),
                      pl.BlockSpec(memory_space=pl.ANY),
                      pl.BlockSpec(memory_space=pl.ANY),
                      pl.BlockSpec(memory_space=pl.ANY)],
            out_specs=pl.BlockSpec((MOE_BLOCK, ROW_TILE, 128), row_map),
            scratch_shapes=[pltpu.VMEM((D_MODEL, D_EXPERT), f32), pltpu.VMEM((D_MODEL, D_EXPERT), f32),
                            pltpu.VMEM((D_EXPERT, D_MODEL), f32),
                            pltpu.VMEM((D_MODEL, 2 * D_EXPERT), bf16), pltpu.VMEM((D_EXPERT, D_MODEL), bf16),
                            pltpu.SemaphoreType.DMA((3,))],
        ),
        out_shape=jax.ShapeDtypeStruct((p_rows, ROW_TILE, 128), bf16),
        compiler_params=_cparams(("arbitrary",)),
        name="moe_experts",
    )(blk_e, nused, xs, w_gate, w_up, w_down)


def _combine_body(dcur_ref, dnext_ref, h_ref, wts_ref, g2_ref, b2_ref, ys_ref, o_ref, ybuf_ref, sem, *, tm, nsteps):
    i = pl.program_id(0)
    slot = lax.rem(i, 2)

    def issue_tile(d_ref, s):
        def body(t, carry):
            for k in range(2):
                pltpu.make_async_copy(ys_ref.at[d_ref[0, k, t]], ybuf_ref.at[s, k, t], sem.at[s]).start(priority=k)
            return carry

        lax.fori_loop(0, tm, body, 0, unroll=8)

    @pl.when(i == 0)
    def _():
        issue_tile(dcur_ref, 0)

    @pl.when(i + 1 < nsteps)
    def _():
        issue_tile(dnext_ref, 1 - slot)

    for k in range(2):
        pltpu.make_async_copy(ys_ref.at[pl.ds(0, tm)], ybuf_ref.at[slot, k], sem.at[slot]).wait()

    pieces = []
    for c in range(tm // 128):
        ls = slice(c * 128, (c + 1) * 128)
        w1c = jnp.broadcast_to(wts_ref[0:1, ls], (128, 128)).T
        w2c = jnp.broadcast_to(wts_ref[1:2, ls], (128, 128)).T
        w1f = jnp.concatenate([w1c] * (D_MODEL // 128), axis=1)
        w2f = jnp.concatenate([w2c] * (D_MODEL // 128), axis=1)
        y1 = ybuf_ref[slot, 0, ls].reshape(128, D_MODEL).astype(f32)
        y2 = ybuf_ref[slot, 1, ls].reshape(128, D_MODEL).astype(f32)
        pieces.append(w1f * y1 + w2f * y2)
    ffn = jnp.concatenate(pieces, axis=0)
    hp = DEEPNORM_ALPHA * h_ref[...] + ffn
    mu = jnp.mean(hp, axis=-1, keepdims=True)
    hc = hp - mu
    var = jnp.mean(hc * hc, axis=-1, keepdims=True)
    o_ref[...] = hc * lax.rsqrt(var + LN_EPS) * g2_ref[...] + b2_ref[...]


def _stage_combine(dest3, h2, wts, g2, b2, ys, *, tm):
    n = h2.shape[0]
    nsteps = n // tm
    return pl.pallas_call(
        functools.partial(_combine_body, tm=tm, nsteps=nsteps),
        grid=(nsteps,),
        in_specs=[pl.BlockSpec((1, 2, tm), lambda i: (i, 0, 0), memory_space=pltpu.SMEM),
                  pl.BlockSpec((1, 2, tm), lambda i: (jnp.minimum(i + 1, nsteps - 1), 0, 0), memory_space=pltpu.SMEM),
                  pl.BlockSpec((tm, D_MODEL), lambda i: (i, 0)),
                  pl.BlockSpec((8, tm), lambda i: (0, i)),
                  pl.BlockSpec((1, D_MODEL), lambda i: (0, 0)),
                  pl.BlockSpec((1, D_MODEL), lambda i: (0, 0)),
                  pl.BlockSpec(memory_space=pl.ANY)],
        out_specs=pl.BlockSpec((tm, D_MODEL), lambda i: (i, 0)),
        out_shape=jax.ShapeDtypeStruct((n, D_MODEL), f32),
        scratch_shapes=[pltpu.VMEM((2, 2, tm, ROW_TILE, 128), bf16), pltpu.SemaphoreType.DMA((2,))],
        compiler_params=_cparams(("arbitrary",)),
        name="moe_combine",
    )(dest3, dest3, h2, wts, g2, b2, ys)


def _layer(h, w_in, conv_w, a_log, dt_bias, dn_norm_w, sgu_ln_g, sgu_ln_b, w_spatial, b_spatial, w_out,
           ln1_g, ln1_b, w_rg, b_rg, w_re, b_re, w_gate, w_up, w_down, ln2_g, ln2_b,
           *, tm_in, in_groups, dn_rows, dn_chunks, dn_groups, tm_mix, tm_rank, tm_disp, tm_comb):
    B, T, _ = h.shape
    n = B * T
    qkvz = 4 * DN_WIDTH
    ba0 = qkvz
    uv0 = qkvz + 2 * DN_HEADS
    w_cols = jnp.zeros((D_MODEL, IN_COLS_ALIGNED), bf16)
    w_cols = lax.dynamic_update_slice(w_cols, w_in[:, :qkvz].astype(bf16), (0, 0))
    w_cols = lax.dynamic_update_slice(w_cols, w_in[:, uv0:].astype(bf16), (0, qkvz))
    w_cols = lax.dynamic_update_slice(w_cols, w_in[:, ba0:uv0].astype(bf16), (0, qkvz + 2 * SGU_WIDTH))
    decay_prm = jnp.stack([a_log, dt_bias])
    pcol = jnp.pad(decay_prm, ((0, 6), (DN_HEADS, 128 - 2 * DN_HEADS)))
    prow = jnp.broadcast_to(jnp.pad(decay_prm, ((0, 0), (DN_HEADS, 8 - 2 * DN_HEADS)))[:, :, None], (2, 8, 128))

    q, k, v, z, u, vln, gcol, grow = _stage_inproj(
        h, w_cols, conv_w, pcol, prow, sgu_ln_g[None, :], sgu_ln_b[None, :], tm=tm_in, ngroups=in_groups)
    ydn = _stage_deltanet(q, k, v, z, gcol, grow, dn_norm_w[None, :], nbr=dn_rows, nch=dn_chunks, ngroups=dn_groups)

    bsp = jnp.broadcast_to(b_spatial.T[:, :, None], (SGU_CHUNK, SGU_GROUPS, SGU_CHUNK)).reshape(SGU_CHUNK, SGU_WIDTH)
    n_logit = MOE_GROUPS + N_EXPERTS
    wrt = jnp.pad(jnp.concatenate([w_rg, w_re], axis=1).T, ((0, 128 - n_logit), (0, 0))).astype(bf16)
    brt = jnp.broadcast_to(jnp.pad(jnp.concatenate([b_rg, b_re]), (0, 128 - n_logit))[:, None], (128, 128))
    h1, hrow, ids, wts = _stage_mixout(ydn, u, vln, h, w_spatial, bsp, w_out.astype(bf16), ln1_g[None, :],
                                       ln1_b[None, :], wrt, brt, tm=tm_mix)

    p_rows = (-(-(n * 2) // MOE_BLOCK)) * MOE_BLOCK + N_EXPERTS * MOE_BLOCK
    nb = p_rows // MOE_BLOCK
    nb_pad = (-(-nb // 128)) * 128
    dest, meta, blk = _stage_route(ids, tm=tm_rank, nb_pad=nb_pad)

    h2 = h1.reshape(n, D_MODEL)
    dest_d = dest[0:2].reshape(2, n // tm_disp, tm_disp).transpose(1, 0, 2)
    xs = _stage_dispatch(meta[0, :N_EXPERTS], meta[1, :N_EXPERTS], meta[2, 0:1], dest_d, hrow, p_rows, tm=tm_disp)
    ys = _stage_experts(blk[0, :nb], meta[2, 0:1], xs, w_gate, w_up, w_down)
    dest_c = dest[0:2].reshape(2, n // tm_comb, tm_comb).transpose(1, 0, 2)
    out = _stage_combine(dest_c, h2, wts, ln2_g[None, :], ln2_b[None, :], ys, tm=tm_comb)
    return out.reshape(B, T, D_MODEL)


def kernel(x, w_in, conv_w, a_log, dt_bias, dn_norm_w, sgu_ln_g, sgu_ln_b, w_spatial, b_spatial, w_out, ln1_g, ln1_b, w_router_group, b_router_group, w_router_expert, b_router_expert, w_gate, w_up, w_down, ln2_g, ln2_b):
    h = x
    for l in range(w_in.shape[0]):
        h = _layer(h, w_in[l], conv_w[l], a_log[l], dt_bias[l], dn_norm_w[l], sgu_ln_g[l], sgu_ln_b[l],
                   w_spatial[l], b_spatial[l], w_out[l], ln1_g[l], ln1_b[l],
                   w_router_group[l], b_router_group[l], w_router_expert[l], b_router_expert[l],
                   w_gate[l], w_up[l], w_down[l], ln2_g[l], ln2_b[l],
                   tm_in=512, in_groups=1, dn_rows=4, dn_chunks=2, dn_groups=2, tm_mix=512, tm_rank=1024, tm_disp=2048, tm_comb=512)
    return h
```

```python
import functools

import jax
import jax.numpy as jnp
from jax import lax
from jax.experimental import pallas as pl
from jax.experimental.pallas import tpu as pltpu

f32 = jnp.float32
bf16 = jnp.bfloat16
i32 = jnp.int32

D_MODEL = 1024
DN_WIDTH = 512
DN_HEADS = 4
HEAD_DIM = 128
CONV_K = 4
SGU_WIDTH = 512
SGU_GROUPS = 4
SGU_CHUNK = 128
DN_CHUNK = 128
MOE_GROUPS = 8
EXPERTS_PER_GROUP = 8
N_EXPERTS = 64
D_EXPERT = 512
MOE_BLOCK = 256
IN_COLS_ALIGNED = 4 * DN_WIDTH + 2 * SGU_WIDTH + 128
X_SLOTS = 3
ROW_TILE = D_MODEL // 128
DEEPNORM_ALPHA = 2.0 ** 0.25
LN_EPS = 1e-5
RMS_EPS = 1e-6
HIGHEST = lax.Precision.HIGHEST
VMEM_LIMIT_BYTES = 56 * 1024 * 1024

NT_DIMS = (((1,), (1,)), ((), ()))


def _cparams(sem, flags=None):
    return pltpu.CompilerParams(dimension_semantics=sem, vmem_limit_bytes=VMEM_LIMIT_BYTES, flags=flags)


def _sigmoid(x):
    return 1.0 / (1.0 + jnp.exp(-x))


def _silu(x):
    h = 0.5 * x
    return h + h * jnp.tanh(h)


def _softplus(x):
    return jnp.maximum(x, 0.0) + jnp.log1p(jnp.exp(-jnp.abs(x)))


def _gelu_tanh(x):
    c = 0.7978845608028654
    return x * (0.5 * (1.0 + jnp.tanh(c * (x + 0.044715 * (x * x * x)))))


def _iota2(shape, axis):
    return lax.broadcasted_iota(i32, shape, axis)


def _inproj_body(x_ref, w_ref, convw_ref, pcol_ref, prow_ref, lng_ref, lnb_ref, ones_ref,
                 q_ref, k_ref, v_ref, z_ref, u_ref, vln_ref, gcol_ref, grow_ref, *ext_refs, tm, ngroups):
    W = DN_WIDTH
    gm = tm // ngroups
    ext = [ext_refs[3 * g:3 * g + 3] for g in range(ngroups)]

    @pl.when(pl.program_id(1) == 0)
    def _():
        for e_ref in ext[0]:
            e_ref[0:8, :] = jnp.zeros((8, W), f32)

    for g in range(ngroups):
        _inproj_rows(x_ref, w_ref, convw_ref, pcol_ref, prow_ref, lng_ref, lnb_ref, ones_ref,
                     q_ref, k_ref, v_ref, z_ref, u_ref, vln_ref, gcol_ref, grow_ref, ext[g],
                     ext[(g + 1) % ngroups], r0=g * gm, gm=gm)


def _inproj_rows(x_ref, w_ref, convw_ref, pcol_ref, prow_ref, lng_ref, lnb_ref, ones_ref,
                 q_ref, k_ref, v_ref, z_ref, u_ref, vln_ref, gcol_ref, grow_ref, ext, ext_next, *, r0, gm):
    W = DN_WIDTH
    rows = slice(r0, r0 + gm)
    xb = x_ref[0, rows, :].astype(bf16)
    for part, e_ref in enumerate(ext):
        e_ref[8:8 + gm, :] = jnp.dot(xb, w_ref[:, part * W:(part + 1) * W], preferred_element_type=f32)
    zc = 3 * W
    uc = zc + W
    vc = uc + SGU_WIDTH
    bc = vc + SGU_WIDTH
    pz = jnp.dot(xb, w_ref[:, zc:zc + W], preferred_element_type=f32)
    pu = jnp.dot(xb, w_ref[:, uc:uc + SGU_WIDTH], preferred_element_type=f32)
    pv = jnp.dot(xb, w_ref[:, vc:vc + SGU_WIDTH], preferred_element_type=f32)
    pba = jnp.dot(xb, w_ref[:, bc:bc + 128], preferred_element_type=f32)

    def group_sums(a):
        return jnp.dot(a.astype(bf16), ones_ref[...], preferred_element_type=f32)

    for part, (e_ref, n_ref, out_ref) in enumerate(zip(ext, ext_next, (q_ref, k_ref, v_ref))):
        cs = slice(part * W, (part + 1) * W)
        y = convw_ref[3:4, cs] * e_ref[8:8 + gm, :]
        for j in range(CONV_K - 1):
            y = y + convw_ref[j:j + 1, cs] * e_ref[5 + j:5 + j + gm, :]
        y = _silu(y)
        if part < 2:
            scale = HEAD_DIM ** -0.5 if part == 0 else 1.0
            y = y * (lax.rsqrt(group_sums(y * y) + RMS_EPS) * scale)
        out_ref[0, rows, :] = y.astype(bf16)
        n_ref[0:8, :] = e_ref[gm:gm + 8, :]

    z_ref[0, rows, :] = _silu(pz).astype(bf16)

    u_ref[0, rows, :] = _gelu_tanh(pu).astype(bf16)
    pv = _gelu_tanh(pv)
    for g in range(SGU_GROUPS):
        sl = slice(g * SGU_CHUNK, (g + 1) * SGU_CHUNK)
        vg = pv[:, sl]
        mu = jnp.mean(vg, axis=-1, keepdims=True)
        vcn = vg - mu
        var = jnp.mean(vcn * vcn, axis=-1, keepdims=True)
        vln_ref[0, rows, sl] = (vcn * lax.rsqrt(var + LN_EPS) * lng_ref[:, sl] + lnb_ref[:, sl]).astype(bf16)

    lane = _iota2((DN_CHUNK, 128), 1)
    is_g = (lane >= DN_HEADS) & (lane < 2 * DN_HEADS)
    beta = _sigmoid(pba)
    gval = -jnp.exp(pcol_ref[0:1, :]) * _softplus(pba + pcol_ref[1:2, :])
    ci = _iota2((DN_CHUNK, DN_CHUNK), 0)
    cj = _iota2((DN_CHUNK, DN_CHUNK), 1)
    ltri = (ci >= cj).astype(f32)
    utri = (ci <= cj).astype(f32)
    sub = _iota2((8, DN_CHUNK), 0)
    for c in range(gm // DN_CHUNK):
        rs = slice(c * DN_CHUNK, (c + 1) * DN_CHUNK)
        os_ = slice(r0 + c * DN_CHUNK, r0 + (c + 1) * DN_CHUNK)
        gc = jnp.dot(ltri, jnp.where(is_g, gval[rs], 0.0), precision=HIGHEST, preferred_element_type=f32)
        gcol_ref[0, os_, :] = jnp.where(lane < DN_HEADS, beta[rs], gc)
        pbat = pba[rs].T[0:8, :]
        gt = -jnp.exp(prow_ref[0]) * _softplus(pbat + prow_ref[1])
        gt = jnp.where(sub >= DN_HEADS, gt, 0.0)
        grow_ref[0, :, os_] = jnp.dot(gt, utri, precision=HIGHEST, preferred_element_type=f32)


def _stage_inproj(x, w_re, conv_w, pcol, prow, lng, lnb, *, tm, ngroups):
    B, T, _ = x.shape
    wcols = w_re.shape[1]
    grid = (B, T // tm)
    gi = lax.broadcasted_iota(i32, (DN_WIDTH, DN_WIDTH), 0) // 128
    gj = lax.broadcasted_iota(i32, (DN_WIDTH, DN_WIDTH), 1) // 128
    group_ones = (gi == gj).astype(bf16)
    act = lambda: jax.ShapeDtypeStruct((B, T, DN_WIDTH), bf16)
    act_spec = lambda: pl.BlockSpec((1, tm, DN_WIDTH), lambda b, t: (b, t, 0))
    const2 = lambda shp: pl.BlockSpec(shp, lambda b, t: (0, 0))
    return pl.pallas_call(
        functools.partial(_inproj_body, tm=tm, ngroups=ngroups),
        grid=grid,
        in_specs=[
            pl.BlockSpec((1, tm, D_MODEL), lambda b, t: (b, t, 0)),
            const2((D_MODEL, wcols)),
            const2((CONV_K, 3 * DN_WIDTH)),
            const2((8, 128)),
            pl.BlockSpec((2, 8, 128), lambda b, t: (0, 0, 0)),
            const2((1, SGU_WIDTH)),
            const2((1, SGU_WIDTH)),
            const2((DN_WIDTH, DN_WIDTH)),
        ],
        out_specs=[act_spec() for _ in range(6)] + [
            pl.BlockSpec((1, tm, 128), lambda b, t: (b, t, 0)),
            pl.BlockSpec((1, 8, tm), lambda b, t: (b, 0, t)),
        ],
        out_shape=[act() for _ in range(6)] + [
            jax.ShapeDtypeStruct((B, T, 128), f32),
            jax.ShapeDtypeStruct((B, 8, T), f32),
        ],
        scratch_shapes=[pltpu.VMEM((tm // ngroups + 8, DN_WIDTH), f32) for _ in range(3 * ngroups)],
        compiler_params=_cparams(("arbitrary", "arbitrary")),
        name="inproj",
    )(x, w_re, conv_w, pcol, prow, lng, lnb, group_ones)


def _mm(a, b):
    return jnp.dot(a.astype(bf16), b.astype(bf16), preferred_element_type=f32)


def _unit_lower_inverse(nmats, ii, jj):
    n = nmats[0].shape[0]
    eye = (ii == jj).astype(f32)
    leaf = jnp.right_shift(ii, 3) == jnp.right_shift(jj, 3)
    dblk = [jnp.where(leaf, m, 0.0) for m in nmats]
    s1 = [_mm(d, d) for d in dblk]
    r1 = [eye - d for d in dblk]
    both = [_mm(s, jnp.concatenate([s, r], axis=1)) for s, r in zip(s1, r1)]
    r2 = [r + bo[:, n:] for r, bo in zip(r1, both)]
    xs = [r + _mm(bo[:, :n], r) for r, bo in zip(r2, both)]
    size = 8
    while size < n:
        lows = [slice(r + size, r + 2 * size) for r in range(0, n, 2 * size)]
        ups = [slice(r, r + size) for r in range(0, n, 2 * size)]
        rsel = _iota2((n // 2, n), 0)
        ilow = rsel + size * (jnp.right_shift(rsel, size.bit_length() - 1) + 1)
        jlow = _iota2((n // 2, n), 1)
        in_pair_upper = (jlow >= ilow - (ilow & (2 * size - 1))) & (jlow < ilow - (ilow & (size - 1)))
        zeros = jnp.zeros((size, n), f32)
        new_xs = []
        ylows = [_mm(jnp.where(in_pair_upper, jnp.concatenate([m[s] for s in lows], axis=0), 0.0), x)
                 for m, x in zip(nmats, xs)]
        yfull = [jnp.concatenate([piece for k in range(len(lows)) for piece in (zeros, y[k * size:(k + 1) * size])], axis=0)
                 for y in ylows]
        corr = [_mm(jnp.concatenate([x[s] for s in lows], axis=0), yf) for x, yf in zip(xs, yfull)]
        for x, c in zip(xs, corr):
            pieces = []
            for k, (u, l) in enumerate(zip(ups, lows)):
                pieces += [x[u], x[l] - c[k * size:(k + 1) * size]]
            new_xs.append(jnp.concatenate(pieces, axis=0))
        xs = new_xs
        size *= 2
    return xs


def _deltanet_body(q_ref, k_ref, v_ref, z_ref, gcol_ref, grow_ref, nw_ref, y_ref, s_ref, *, nbr, nch, ngroups):
    @pl.when(pl.program_id(1) == 0)
    def _():
        s_ref[...] = jnp.zeros(s_ref.shape, f32)

    per = nbr // ngroups
    for g in range(ngroups):
        _deltanet_rows(q_ref, k_ref, v_ref, z_ref, gcol_ref, grow_ref, nw_ref, y_ref, s_ref,
                       rows=range(g * per, (g + 1) * per), nch=nch)


def _deltanet_rows(q_ref, k_ref, v_ref, z_ref, gcol_ref, grow_ref, nw_ref, y_ref, s_ref, *, rows, nch):
    C = DN_CHUNK
    S = [(b, h) for b in rows for h in range(DN_HEADS)]
    P = [(b, c, h) for c in range(nch) for b, h in S]

    ii = _iota2((C, C), 0)
    jj = _iota2((C, C), 1)
    causal = ii >= jj
    rs = [slice(c * C, (c + 1) * C) for c in range(nch)]
    hs = [slice(h * HEAD_DIM, (h + 1) * HEAD_DIM) for h in range(DN_HEADS)]
    gcol = {(b, c): gcol_ref[b, rs[c], :] for b in rows for c in range(nch)}
    qh = {(b, c, h): q_ref[b, rs[c], hs[h]] for b, c, h in P}
    kh = {(b, c, h): k_ref[b, rs[c], hs[h]] for b, c, h in P}
    vh = {(b, c, h): v_ref[b, rs[c], hs[h]] for b, c, h in P}
    gc_b = {(b, c, h): jnp.broadcast_to(gcol[b, c][:, DN_HEADS + h:DN_HEADS + h + 1], (C, HEAD_DIM)) for b, c, h in P}
    beta_b = {(b, c, h): jnp.broadcast_to(gcol[b, c][:, h:h + 1], (C, HEAD_DIM)) for b, c, h in P}
    gc_r = {(b, c, h): jnp.broadcast_to(grow_ref[b, DN_HEADS + h:DN_HEADS + h + 1, rs[c]], (C, C)) for b, c, h in P}
    decay = {p: jnp.exp(jnp.where(causal, gc_b[p] - gc_r[p], -1e30)) for p in P}

    kf = {p: kh[p].astype(f32) for p in P}
    kb = {p: kf[p] * beta_b[p] for p in P}
    kk = {p: lax.dot_general(kb[p].astype(bf16), kh[p], NT_DIMS, preferred_element_type=f32) for p in P}
    a_intra = {p: lax.dot_general(qh[p], kh[p], NT_DIMS, preferred_element_type=f32) * decay[p] for p in P}
    nmat = [jnp.where(ii > jj, kk[p] * decay[p], 0.0) for p in P]
    tinv = dict(zip(P, _unit_lower_inverse(nmat, ii, jj)))

    eg = {p: jnp.exp(gc_b[p]) for p in P}
    rhs = {p: jnp.concatenate([vh[p].astype(f32) * beta_b[p], kb[p] * eg[p]], axis=1) for p in P}
    sol = {p: _mm(tinv[p], rhs[p]) for p in P}

    q_dec = {p: qh[p].astype(f32) * eg[p] for p in P}
    g_last = {p: gc_b[p][C - 1:C, :] for p in P}
    kdt = {p: (kf[p] * jnp.exp(g_last[p] - gc_b[p])).T for p in P}

    state = {(b, h): s_ref[b * DN_HEADS + h] for b, h in S}
    for c in range(nch):
        m1 = {(b, h): _mm(jnp.concatenate([sol[b, c, h][:, HEAD_DIM:], q_dec[b, c, h]], axis=0), state[b, h]) for b, h in S}
        v_new = {(b, h): sol[b, c, h][:, :HEAD_DIM] - m1[b, h][:C] for b, h in S}
        m2 = {(b, h): _mm(jnp.concatenate([a_intra[b, c, h], kdt[b, c, h]], axis=0), v_new[b, h]) for b, h in S}
        state = {(b, h): state[b, h] * jnp.exp(g_last[b, c, h]) + m2[b, h][C:] for b, h in S}
        for b, h in S:
            o = m1[b, h][C:] + m2[b, h][:C]
            rms = lax.rsqrt(jnp.mean(o * o, axis=-1, keepdims=True) + RMS_EPS)
            y_ref[b, rs[c], hs[h]] = (o * rms * nw_ref[...] * z_ref[b, rs[c], hs[h]].astype(f32)).astype(bf16)
    for b, h in S:
        s_ref[b * DN_HEADS + h] = state[b, h]


def _stage_deltanet(q, k, v, z, gcol, grow, norm_w, *, nbr, nch, ngroups):
    B, T, _ = q.shape
    tt = nch * DN_CHUNK
    act_spec = lambda: pl.BlockSpec((nbr, tt, DN_WIDTH), lambda b, t: (b, t, 0))
    return pl.pallas_call(
        functools.partial(_deltanet_body, nbr=nbr, nch=nch, ngroups=ngroups),
        grid=(B // nbr, T // tt),
        in_specs=[act_spec(), act_spec(), act_spec(), act_spec(),
                  pl.BlockSpec((nbr, tt, 128), lambda b, t: (b, t, 0)),
                  pl.BlockSpec((nbr, 8, tt), lambda b, t: (b, 0, t)),
                  pl.BlockSpec((1, HEAD_DIM), lambda b, t: (0, 0))],
        out_specs=act_spec(),
        out_shape=jax.ShapeDtypeStruct((B, T, DN_WIDTH), bf16),
        scratch_shapes=[pltpu.VMEM((nbr * DN_HEADS, HEAD_DIM, HEAD_DIM), f32)],
        compiler_params=_cparams(("arbitrary", "arbitrary")),
        name="deltanet",
    )(q, k, v, z, gcol, grow, norm_w)


def _mixout_body(ydn_ref, u_ref, vln_ref, x_ref, ws_ref, bsp_ref, wout_ref, g1_ref, b1_ref, wrt_ref, brt_ref,
                 h_ref, hrow_ref, ids_ref, wts_ref, ycat_ref, *, tm):
    C = SGU_CHUNK
    ii = _iota2((C, C), 0)
    jj = _iota2((C, C), 1)
    ycat_ref[:, 0:DN_WIDTH] = ydn_ref[0]
    for g in range(SGU_GROUPS):
        gs = slice(g * C, (g + 1) * C)
        wsg = jnp.where(ii >= jj, ws_ref[g], 0.0).astype(bf16)
        for c in range(tm // C):
            rs = slice(c * C, (c + 1) * C)
            mixed = jnp.dot(wsg, vln_ref[0, rs, gs], preferred_element_type=f32) + bsp_ref[:, gs]
            ycat_ref[rs, DN_WIDTH + g * C:DN_WIDTH + (g + 1) * C] = (u_ref[0, rs, gs].astype(f32) * mixed).astype(bf16)

    RB = 128
    blocks = [slice(r, r + RB) for r in range(0, tm, RB)]
    mix = [jnp.dot(ycat_ref[rb, :], wout_ref[...], preferred_element_type=f32) for rb in blocks]
    h1s = []
    for rb, m in zip(blocks, mix):
        hp = DEEPNORM_ALPHA * x_ref[0, rb, :] + m
        mu = jnp.mean(hp, axis=-1, keepdims=True)
        hc = hp - mu
        var = jnp.mean(hc * hc, axis=-1, keepdims=True)
        h1 = hc * lax.rsqrt(var + LN_EPS) * g1_ref[...] + b1_ref[...]
        h_ref[0, rb, :] = h1
        h1b = h1.astype(bf16)
        hrow_ref[rb] = h1b.reshape(RB, ROW_TILE, 128)
        h1s.append(h1b)

    logit_blocks = [lax.dot_general(wrt_ref[...], hb, NT_DIMS, preferred_element_type=f32) + brt_ref[...] for hb in h1s]
    sub = _iota2((8, RB), 0)
    subf = sub.astype(f32)
    for rb, logits in zip(blocks, logit_blocks):
        gl = logits[0:8]
        gmax = jnp.max(gl, axis=0, keepdims=True)
        g_idx = jnp.min(jnp.where(gl == gmax, subf, float(MOE_GROUPS)), axis=0, keepdims=True)
        p_group = 1.0 / jnp.sum(jnp.exp(gl - gmax), axis=0, keepdims=True)
        within = jnp.zeros((8, RB), f32)
        for g in range(MOE_GROUPS):
            within = within + jnp.where(g_idx == float(g), logits[8 + 8 * g:16 + 8 * g], 0.0)
        m1 = jnp.max(within, axis=0, keepdims=True)
        i1 = jnp.min(jnp.where(within == m1, subf, float(EXPERTS_PER_GROUP)), axis=0, keepdims=True)
        rest = jnp.where(subf == i1, -jnp.inf, within)
        m2 = jnp.max(rest, axis=0, keepdims=True)
        i2 = jnp.min(jnp.where(rest == m2, subf, float(EXPERTS_PER_GROUP)), axis=0, keepdims=True)
        e = jnp.exp(m2 - m1)
        w1 = p_group / (1.0 + e)
        w2 = p_group * e / (1.0 + e)
        e1 = g_idx * float(EXPERTS_PER_GROUP) + i1
        e2 = g_idx * float(EXPERTS_PER_GROUP) + i2
        ids_ref[:, rb] = jnp.where(sub == 0, e1, jnp.where(sub == 1, e2, 0.0)).astype(i32)
        wts_ref[:, rb] = jnp.where(sub == 0, w1, jnp.where(sub == 1, w2, 0.0))


def _stage_mixout(ydn, u, vln, x, ws, bsp, wout, g1, b1, wrt, brt, *, tm):
    B, T, _ = x.shape
    nt = T // tm
    act_spec = lambda: pl.BlockSpec((1, tm, DN_WIDTH), lambda b, t: (b, t, 0))
    const2 = lambda shp: pl.BlockSpec(shp, lambda b, t: (0, 0))
    tok_spec = lambda: pl.BlockSpec((8, tm), lambda b, t: (0, b * nt + t))
    return pl.pallas_call(
        functools.partial(_mixout_body, tm=tm),
        grid=(B, nt),
        in_specs=[act_spec(), act_spec(), act_spec(),
                  pl.BlockSpec((1, tm, D_MODEL), lambda b, t: (b, t, 0)),
                  pl.BlockSpec((SGU_GROUPS, SGU_CHUNK, SGU_CHUNK), lambda b, t: (0, 0, 0)),
                  const2((SGU_CHUNK, SGU_WIDTH)),
                  const2((D_MODEL, D_MODEL)),
                  const2((1, D_MODEL)), const2((1, D_MODEL)),
                  const2((128, D_MODEL)), const2((128, 128))],
        out_specs=[pl.BlockSpec((1, tm, D_MODEL), lambda b, t: (b, t, 0)),
                   pl.BlockSpec((tm, ROW_TILE, 128), lambda b, t: (b * nt + t, 0, 0)), tok_spec(), tok_spec()],
        out_shape=[jax.ShapeDtypeStruct((B, T, D_MODEL), f32),
                   jax.ShapeDtypeStruct((B * T, ROW_TILE, 128), bf16),
                   jax.ShapeDtypeStruct((8, B * T), i32),
                   jax.ShapeDtypeStruct((8, B * T), f32)],
        scratch_shapes=[pltpu.VMEM((tm, D_MODEL), bf16)],
        compiler_params=_cparams(("arbitrary", "arbitrary")),
        name="mixout",
    )(ydn, u, vln, x, ws, bsp, wout, g1, b1, wrt, brt)


def _route_body(ids_ref, dest_ref, meta_ref, blk_ref, base_ref, pstart_ref, *, tm, nb_pad):
    phase = pl.program_id(0)
    i = pl.program_id(1)
    sub = _iota2((N_EXPERTS, tm), 0)
    is1 = sub == ids_ref[0:1, :]
    is2 = sub == ids_ref[1:2, :]
    oh = (is1.astype(f32) + is2.astype(f32)).astype(bf16)
    counts = jnp.dot(oh, jnp.ones((tm, 128), bf16), preferred_element_type=f32)

    @pl.when((phase == 0) & (i == 0))
    def _():
        base_ref[...] = jnp.zeros(base_ref.shape, f32)

    @pl.when(phase == 0)
    def _():
        base_ref[...] = base_ref[...] + counts

    @pl.when((phase == 1) & (i == 0))
    def _():
        cnt = base_ref[...]
        padded = jnp.floor((cnt + (MOE_BLOCK - 1)) * (1.0 / MOE_BLOCK)) * MOE_BLOCK
        ei = _iota2((N_EXPERTS, N_EXPERTS), 0)
        ej = _iota2((N_EXPERTS, N_EXPERTS), 1)
        pends = jnp.dot((ei >= ej).astype(f32), padded, precision=HIGHEST, preferred_element_type=f32)
        pstart = pends - padded
        pstart_ref[...] = pstart
        s64 = _iota2((N_EXPERTS, 128), 0)
        l64 = _iota2((N_EXPERTS, 128), 1)
        diag = s64 == l64
        fill_off = jnp.sum(jnp.where(diag, pstart + cnt, 0.0), axis=0, keepdims=True)
        fill_n = jnp.sum(jnp.where(diag, padded - cnt, 0.0), axis=0, keepdims=True)
        nused = pends[N_EXPERTS - 1:N_EXPERTS, :] * (1.0 / MOE_BLOCK)
        m8 = _iota2((8, 128), 0)
        meta_ref[...] = jnp.where(m8 == 0, fill_off, jnp.where(m8 == 1, fill_n, jnp.where(m8 == 2, nused, 0.0))).astype(i32)
        bstart = (_iota2((N_EXPERTS, nb_pad), 1) * MOE_BLOCK).astype(f32)
        pe = jnp.concatenate([pends] * (nb_pad // 128), axis=1)
        be = jnp.sum((pe <= bstart).astype(f32), axis=0, keepdims=True)
        be = jnp.minimum(be, float(N_EXPERTS - 1))
        blk_ref[...] = jnp.broadcast_to(be, (8, nb_pad)).astype(i32)

    @pl.when(phase == 1)
    def _():
        ti = _iota2((tm, tm), 0)
        tj = _iota2((tm, tm), 1)
        before = (ti < tj).astype(bf16)
        prefix = jnp.dot(oh, before, preferred_element_type=f32)
        nxt = prefix + jnp.concatenate([pstart_ref[...]] * (tm // 128), axis=1)
        d1 = jnp.sum(jnp.where(is1, nxt, 0.0), axis=0, keepdims=True)
        d2 = jnp.sum(jnp.where(is2, nxt, 0.0), axis=0, keepdims=True)
        sub8 = _iota2((8, tm), 0)
        dest_ref[...] = jnp.where(sub8 == 0, d1, jnp.where(sub8 == 1, d2, 0.0)).astype(i32)
        pstart_ref[...] = pstart_ref[...] + counts


def _stage_route(ids, *, tm, nb_pad):
    n = ids.shape[1]
    return pl.pallas_call(
        functools.partial(_route_body, tm=tm, nb_pad=nb_pad),
        grid=(2, n // tm),
        in_specs=[pl.BlockSpec((8, tm), lambda p, i: (0, i))],
        out_specs=[pl.BlockSpec((8, tm), lambda p, i: (0, i * p)),
                   pl.BlockSpec((8, 128), lambda p, i: (0, 0)),
                   pl.BlockSpec((8, nb_pad), lambda p, i: (0, 0))],
        out_shape=[jax.ShapeDtypeStruct((8, n), i32), jax.ShapeDtypeStruct((8, 128), i32),
                   jax.ShapeDtypeStruct((8, nb_pad), i32)],
        scratch_shapes=[pltpu.VMEM((N_EXPERTS, 128), f32), pltpu.VMEM((N_EXPERTS, 128), f32)],
        compiler_params=_cparams(("arbitrary", "arbitrary")),
        name="moe_route",
    )(ids)


def _dispatch_body(fill_off_ref, fill_n_ref, nused_ref, dest_ref, h3_ref, xs_ref, zero_ref, sem, zsem, *, tm):
    def row_copy(t, d):
        return pltpu.make_async_copy(h3_ref.at[t], xs_ref.at[d], sem)

    def issue(t, carry):
        row_copy(t, dest_ref[0, 0, t]).start(priority=0)
        row_copy(t, dest_ref[0, 1, t]).start(priority=1)
        return carry

    lax.fori_loop(0, tm, issue, 0, unroll=8)

    @pl.when(pl.program_id(0) == 0)
    def _():
        zero_ref[...] = jnp.zeros(zero_ref.shape, bf16)

        def fill(start):
            def body(e, carry):
                off = fill_off_ref[e]
                npad = fill_n_ref[e]
                bit = MOE_BLOCK // 2
                while bit:
                    @pl.when((npad & bit) != 0)
                    def _(off=off, bit=bit):
                        cp = pltpu.make_async_copy(zero_ref.at[pl.ds(0, bit)], xs_ref.at[pl.ds(off, bit)], zsem)
                        cp.start() if start else cp.wait()
                    off = off + (npad & bit)
                    bit //= 2
                return carry
            return body

        lax.fori_loop(0, N_EXPERTS, fill(True), 0)
        lax.fori_loop(0, N_EXPERTS, fill(False), 0)

        def tail_copy(b):
            return pltpu.make_async_copy(zero_ref, xs_ref.at[pl.ds(b * MOE_BLOCK, MOE_BLOCK)], zsem)

        nblocks = xs_ref.shape[0] // MOE_BLOCK
        lax.fori_loop(nused_ref[0], nblocks, lambda b, c: (tail_copy(b).start(), c)[1], 0)
        lax.fori_loop(nused_ref[0], nblocks, lambda b, c: (tail_copy(0).wait(), c)[1], 0)

    for _ in range(2):
        pltpu.make_async_copy(h3_ref, xs_ref.at[pl.ds(0, tm)], sem).wait()


def _stage_dispatch(fill_off, fill_n, nused, dest3, hrow, p_rows, *, tm):
    n = hrow.shape[0]
    return pl.pallas_call(
        functools.partial(_dispatch_body, tm=tm),
        grid_spec=pltpu.PrefetchScalarGridSpec(
            num_scalar_prefetch=3,
            grid=(n // tm,),
            in_specs=[pl.BlockSpec((1, 2, tm), lambda i, fo, fn, nu: (i, 0, 0), memory_space=pltpu.SMEM),
                      pl.BlockSpec((tm, ROW_TILE, 128), lambda i, fo, fn, nu: (i, 0, 0))],
            out_specs=pl.BlockSpec(memory_space=pl.ANY),
            scratch_shapes=[pltpu.VMEM((MOE_BLOCK, ROW_TILE, 128), bf16),
                            pltpu.SemaphoreType.DMA, pltpu.SemaphoreType.DMA],
        ),
        out_shape=jax.ShapeDtypeStruct((p_rows, ROW_TILE, 128), bf16),
        compiler_params=_cparams(("arbitrary",)),
        name="moe_dispatch",
    )(fill_off, fill_n, nused, dest3, hrow)


def _experts_body(blk_ref, nused_ref, xs_hbm, wg_hbm, wu_hbm, wd_hbm, ys_ref,
                  xbuf_ref, wg32_ref, wu32_ref, wd32_ref, wgu16_ref, wd16_ref, xsem, wsem):
    i = pl.program_id(0)
    nused = nused_ref[0]
    used = i < nused
    e = blk_ref[i]
    slot = lax.rem(i, X_SLOTS)

    def weight_copies(ex):
        return (pltpu.make_async_copy(wg_hbm.at[ex], wg32_ref, wsem.at[0]),
                pltpu.make_async_copy(wu_hbm.at[ex], wu32_ref, wsem.at[1]),
                pltpu.make_async_copy(wd_hbm.at[ex], wd32_ref, wsem.at[2]))

    def x_copy(block, s):
        return pltpu.make_async_copy(xs_hbm.at[pl.ds(block * MOE_BLOCK, MOE_BLOCK)], xbuf_ref.at[s], xsem.at[s])

    @pl.when((i == 0) & used)
    def _():
        for cp in weight_copies(e):
            cp.start()
        for j in range(X_SLOTS - 1):
            @pl.when(j < nused)
            def _(j=j):
                x_copy(j, j).start()

    ahead = i + (X_SLOTS - 1)

    @pl.when(ahead < nused)
    def _():
        x_copy(ahead, lax.rem(ahead, X_SLOTS)).start()

    @pl.when(used & ((i == 0) | (e != blk_ref[jnp.maximum(i - 1, 0)])))
    def _():
        for cp in weight_copies(e):
            cp.wait()
        wgu16_ref[:, 0:D_EXPERT] = wg32_ref[...].astype(bf16)
        wgu16_ref[:, D_EXPERT:2 * D_EXPERT] = wu32_ref[...].astype(bf16)
        wd16_ref[...] = wd32_ref[...].astype(bf16)
        nxt = lax.while_loop(lambda j: (j < nused) & (blk_ref[jnp.minimum(j, nused - 1)] == e), lambda j: j + 1, i + 1)

        @pl.when(nxt < nused)
        def _():
            for cp in weight_copies(blk_ref[jnp.minimum(nxt, nused - 1)]):
                cp.start(priority=1)

    @pl.when(used)
    def _():
        x_copy(i, slot).wait()
        half = MOE_BLOCK // 2
        rows = [slice(p * half, (p + 1) * half) for p in range(2)]
        gu = [jnp.dot(xbuf_ref[slot, r].reshape(half, D_MODEL), wgu16_ref[...], preferred_element_type=f32)
              for r in rows]
        hid = [(_silu(g[:, :D_EXPERT]) * g[:, D_EXPERT:]).astype(bf16) for g in gu]
        y = [jnp.dot(hd, wd16_ref[...], preferred_element_type=f32) for hd in hid]
        for r, yp in zip(rows, y):
            ys_ref[r] = yp.astype(bf16).reshape(half, ROW_TILE, 128)

    @pl.when(jnp.logical_not(used))
    def _():
        ys_ref[...] = jnp.zeros(ys_ref.shape, bf16)


def _stage_experts(blk_e, nused, xs, w_gate, w_up, w_down):
    p_rows = xs.shape[0]
    nb = p_rows // MOE_BLOCK

    def row_map(i, blk, nu):
        return (i, 0, 0)

    return pl.pallas_call(
        _experts_body,
        grid_spec=pltpu.PrefetchScalarGridSpec(
            num_scalar_prefetch=2,
            grid=(nb,),
            in_specs=[pl.BlockSpec(memory_space=pl.ANY),
                      pl.BlockSpec(memory_space=pl.ANY),
                      pl.BlockSpec(memory_space=pl.ANY),
                      pl.BlockSpec(memory_space=pl.ANY)],
            out_specs=pl.BlockSpec((MOE_BLOCK, ROW_TILE, 128), row_map),
            scratch_shapes=[pltpu.VMEM((X_SLOTS, MOE_BLOCK, ROW_TILE, 128), bf16),
                            pltpu.VMEM((D_MODEL, D_EXPERT), f32), pltpu.VMEM((D_MODEL, D_EXPERT), f32),
                            pltpu.VMEM((D_EXPERT, D_MODEL), f32),
                            pltpu.VMEM((D_MODEL, 2 * D_EXPERT), bf16), pltpu.VMEM((D_EXPERT, D_MODEL), bf16),
                            pltpu.SemaphoreType.DMA((X_SLOTS,)), pltpu.SemaphoreType.DMA((3,))],
        ),
        out_shape=jax.ShapeDtypeStruct((p_rows, ROW_TILE, 128), bf16),
        compiler_params=_cparams(("arbitrary",)),
        name="moe_experts",
    )(blk_e, nused, xs, w_gate, w_up, w_down)


def _combine_body(dcur_ref, dnext_ref, h_ref, wts_ref, g2_ref, b2_ref, ys_ref, o_ref, ybuf_ref, sem, *, tm, nsteps):
    i = pl.program_id(0)
    slot = lax.rem(i, 2)

    def issue_tile(d_ref, s):
        def body(t, carry):
            for k in range(2):
                pltpu.make_async_copy(ys_ref.at[d_ref[0, k, t]], ybuf_ref.at[s, k, t], sem.at[s]).start(priority=k)
            return carry

        lax.fori_loop(0, tm, body, 0, unroll=8)

    @pl.when(i == 0)
    def _():
        issue_tile(dcur_ref, 0)

    @pl.when(i + 1 < nsteps)
    def _():
        issue_tile(dnext_ref, 1 - slot)

    for k in range(2):
        pltpu.make_async_copy(ys_ref.at[pl.ds(0, tm)], ybuf_ref.at[slot, k], sem.at[slot]).wait()

    pieces = []
    for c in range(tm // 128):
        ls = slice(c * 128, (c + 1) * 128)
        w1c = jnp.broadcast_to(wts_ref[0:1, ls], (128, 128)).T
        w2c = jnp.broadcast_to(wts_ref[1:2, ls], (128, 128)).T
        w1f = jnp.concatenate([w1c] * (D_MODEL // 128), axis=1)
        w2f = jnp.concatenate([w2c] * (D_MODEL // 128), axis=1)
        y1 = ybuf_ref[slot, 0, ls].reshape(128, D_MODEL).astype(f32)
        y2 = ybuf_ref[slot, 1, ls].reshape(128, D_MODEL).astype(f32)
        pieces.append(w1f * y1 + w2f * y2)
    ffn = jnp.concatenate(pieces, axis=0)
    hp = DEEPNORM_ALPHA * h_ref[...] + ffn
    mu = jnp.mean(hp, axis=-1, keepdims=True)
    hc = hp - mu
    var = jnp.mean(hc * hc, axis=-1, keepdims=True)
    o_ref[...] = hc * lax.rsqrt(var + LN_EPS) * g2_ref[...] + b2_ref[...]


def _stage_combine(dest3, h2, wts, g2, b2, ys, *, tm):
    n = h2.shape[0]
    nsteps = n // tm
    return pl.pallas_call(
        functools.partial(_combine_body, tm=tm, nsteps=nsteps),
        grid=(nsteps,),
        in_specs=[pl.BlockSpec((1, 2, tm), lambda i: (i, 0, 0), memory_space=pltpu.SMEM),
                  pl.BlockSpec((1, 2, tm), lambda i: (jnp.minimum(i + 1, nsteps - 1), 0, 0), memory_space=pltpu.SMEM),
                  pl.BlockSpec((tm, D_MODEL), lambda i: (i, 0)),
                  pl.BlockSpec((8, tm), lambda i: (0, i)),
                  pl.BlockSpec((1, D_MODEL), lambda i: (0, 0)),
                  pl.BlockSpec((1, D_MODEL), lambda i: (0, 0)),
                  pl.BlockSpec(memory_space=pl.ANY)],
        out_specs=pl.BlockSpec((tm, D_MODEL), lambda i: (i, 0)),
        out_shape=jax.ShapeDtypeStruct((n, D_MODEL), f32),
        scratch_shapes=[pltpu.VMEM((2, 2, tm, ROW_TILE, 128), bf16), pltpu.SemaphoreType.DMA((2,))],
        compiler_params=_cparams(("arbitrary",)),
        name="moe_combine",
    )(dest3, dest3, h2, wts, g2, b2, ys)


def _layer(h, w_in, conv_w, a_log, dt_bias, dn_norm_w, sgu_ln_g, sgu_ln_b, w_spatial, b_spatial, w_out,
           ln1_g, ln1_b, w_rg, b_rg, w_re, b_re, w_gate, w_up, w_down, ln2_g, ln2_b,
           *, tm_in, in_groups, dn_rows, dn_chunks, dn_groups, tm_mix, tm_rank, tm_disp, tm_comb):
    B, T, _ = h.shape
    n = B * T
    qkvz = 4 * DN_WIDTH
    ba0 = qkvz
    uv0 = qkvz + 2 * DN_HEADS
    w_cols = jnp.zeros((D_MODEL, IN_COLS_ALIGNED), bf16)
    w_cols = lax.dynamic_update_slice(w_cols, w_in[:, :qkvz].astype(bf16), (0, 0))
    w_cols = lax.dynamic_update_slice(w_cols, w_in[:, uv0:].astype(bf16), (0, qkvz))
    w_cols = lax.dynamic_update_slice(w_cols, w_in[:, ba0:uv0].astype(bf16), (0, qkvz + 2 * SGU_WIDTH))
    decay_prm = jnp.stack([a_log, dt_bias])
    pcol = jnp.pad(decay_prm, ((0, 6), (DN_HEADS, 128 - 2 * DN_HEADS)))
    prow = jnp.broadcast_to(jnp.pad(decay_prm, ((0, 0), (DN_HEADS, 8 - 2 * DN_HEADS)))[:, :, None], (2, 8, 128))

    q, k, v, z, u, vln, gcol, grow = _stage_inproj(
        h, w_cols, conv_w, pcol, prow, sgu_ln_g[None, :], sgu_ln_b[None, :], tm=tm_in, ngroups=in_groups)
    ydn = _stage_deltanet(q, k, v, z, gcol, grow, dn_norm_w[None, :], nbr=dn_rows, nch=dn_chunks, ngroups=dn_groups)

    bsp = jnp.broadcast_to(b_spatial.T[:, :, None], (SGU_CHUNK, SGU_GROUPS, SGU_CHUNK)).reshape(SGU_CHUNK, SGU_WIDTH)
    n_logit = MOE_GROUPS + N_EXPERTS
    wrt = jnp.pad(jnp.concatenate([w_rg, w_re], axis=1).T, ((0, 128 - n_logit), (0, 0))).astype(bf16)
    brt = jnp.broadcast_to(jnp.pad(jnp.concatenate([b_rg, b_re]), (0, 128 - n_logit))[:, None], (128, 128))
    h1, hrow, ids, wts = _stage_mixout(ydn, u, vln, h, w_spatial, bsp, w_out.astype(bf16), ln1_g[None, :],
                                       ln1_b[None, :], wrt, brt, tm=tm_mix)

    p_rows = (-(-(n * 2) // MOE_BLOCK)) * MOE_BLOCK + N_EXPERTS * MOE_BLOCK
    nb = p_rows // MOE_BLOCK
    nb_pad = (-(-nb // 128)) * 128
    dest, meta, blk = _stage_route(ids, tm=tm_rank, nb_pad=nb_pad)

    h2 = h1.reshape(n, D_MODEL)
    dest_d = dest[0:2].reshape(2, n // tm_disp, tm_disp).transpose(1, 0, 2)
    xs = _stage_dispatch(meta[0, :N_EXPERTS], meta[1, :N_EXPERTS], meta[2, 0:1], dest_d, hrow, p_rows, tm=tm_disp)
    ys = _stage_experts(blk[0, :nb], meta[2, 0:1], xs, w_gate, w_up, w_down)
    dest_c = dest[0:2].reshape(2, n // tm_comb, tm_comb).transpose(1, 0, 2)
    out = _stage_combine(dest_c, h2, wts, ln2_g[None, :], ln2_b[None, :], ys, tm=tm_comb)
    return out.reshape(B, T, D_MODEL)


def kernel(x, w_in, conv_w, a_log, dt_bias, dn_norm_w, sgu_ln_g, sgu_ln_b, w_spatial, b_spatial, w_out, ln1_g, ln1_b, w_router_group, b_router_group, w_router_expert, b_router_expert, w_gate, w_up, w_down, ln2_g, ln2_b):
    h = x
    for l in range(w_in.shape[0]):
        h = _layer(h, w_in[l], conv_w[l], a_log[l], dt_bias[l], dn_norm_w[l], sgu_ln_g[l], sgu_ln_b[l],
                   w_spatial[l], b_spatial[l], w_out[l], ln1_g[l], ln1_b[l],
                   w_router_group[l], b_router_group[l], w_router_expert[l], b_router_expert[l],
                   w_gate[l], w_up[l], w_down[l], ln2_g[l], ln2_b[l],
                   tm_in=512, in_groups=1, dn_rows=4, dn_chunks=2, dn_groups=2, tm_mix=512, tm_rank=1024, tm_disp=2048, tm_comb=512)
    return h
```

```python
import functools

import jax
import jax.numpy as jnp
from jax import lax
from jax.experimental import pallas as pl
from jax.experimental.pallas import tpu as pltpu

f32 = jnp.float32
bf16 = jnp.bfloat16
i32 = jnp.int32

D_MODEL = 1024
DN_WIDTH = 512
DN_HEADS = 4
HEAD_DIM = 128
CONV_K = 4
SGU_WIDTH = 512
SGU_GROUPS = 4
SGU_CHUNK = 128
DN_CHUNK = 128
MOE_GROUPS = 8
EXPERTS_PER_GROUP = 8
N_EXPERTS = 64
D_EXPERT = 512
MOE_BLOCK = 256
IN_COLS_ALIGNED = 4 * DN_WIDTH + 2 * SGU_WIDTH + 128
X_SLOTS = 3
ROW_TILE = D_MODEL // 128
DEEPNORM_ALPHA = 2.0 ** 0.25
LN_EPS = 1e-5
RMS_EPS = 1e-6
HIGHEST = lax.Precision.HIGHEST
VMEM_LIMIT_BYTES = 56 * 1024 * 1024

NT_DIMS = (((1,), (1,)), ((), ()))


def _cparams(sem, flags=None):
    return pltpu.CompilerParams(dimension_semantics=sem, vmem_limit_bytes=VMEM_LIMIT_BYTES, flags=flags)


def _sigmoid(x):
    return 1.0 / (1.0 + jnp.exp(-x))


def _silu(x):
    h = 0.5 * x
    return h + h * jnp.tanh(h)


def _softplus(x):
    return jnp.maximum(x, 0.0) + jnp.log1p(jnp.exp(-jnp.abs(x)))


def _gelu_tanh(x):
    c = 0.7978845608028654
    return x * (0.5 * (1.0 + jnp.tanh(c * (x + 0.044715 * (x * x * x)))))


def _iota2(shape, axis):
    return lax.broadcasted_iota(i32, shape, axis)


def _inproj_body(x_ref, w_ref, convw_ref, prow_ref, lng_ref, lnb_ref, ones_ref,
                 q_ref, k_ref, v_ref, z_ref, u_ref, vln_ref, gcol_ref, grow_ref, *ext_refs, tm, ngroups):
    W = DN_WIDTH
    gm = tm // ngroups
    ext = [ext_refs[3 * g:3 * g + 3] for g in range(ngroups)]

    @pl.when(pl.program_id(1) == 0)
    def _():
        for e_ref in ext[0]:
            e_ref[0:8, :] = jnp.zeros((8, W), f32)

    for g in range(ngroups):
        _inproj_rows(x_ref, w_ref, convw_ref, prow_ref, lng_ref, lnb_ref, ones_ref,
                     q_ref, k_ref, v_ref, z_ref, u_ref, vln_ref, gcol_ref, grow_ref, ext[g],
                     ext[(g + 1) % ngroups], r0=g * gm, gm=gm)


def _inproj_rows(x_ref, w_ref, convw_ref, prow_ref, lng_ref, lnb_ref, ones_ref,
                 q_ref, k_ref, v_ref, z_ref, u_ref, vln_ref, gcol_ref, grow_ref, ext, ext_next, *, r0, gm):
    W = DN_WIDTH
    rows = slice(r0, r0 + gm)
    xb = x_ref[0, rows, :].astype(bf16)
    for part, e_ref in enumerate(ext):
        e_ref[8:8 + gm, :] = jnp.dot(xb, w_ref[:, part * W:(part + 1) * W], preferred_element_type=f32)
    zc = 3 * W
    uc = zc + W
    vc = uc + SGU_WIDTH
    bc = vc + SGU_WIDTH
    pz = jnp.dot(xb, w_ref[:, zc:zc + W], preferred_element_type=f32)
    pu = jnp.dot(xb, w_ref[:, uc:uc + SGU_WIDTH], preferred_element_type=f32)
    pv = jnp.dot(xb, w_ref[:, vc:vc + SGU_WIDTH], preferred_element_type=f32)
    pba = jnp.dot(xb, w_ref[:, bc:bc + 128], preferred_element_type=f32)

    def group_sums(a):
        return jnp.dot(a.astype(bf16), ones_ref[...], preferred_element_type=f32)

    for part, (e_ref, n_ref, out_ref) in enumerate(zip(ext, ext_next, (q_ref, k_ref, v_ref))):
        cs = slice(part * W, (part + 1) * W)
        y = convw_ref[3:4, cs] * e_ref[8:8 + gm, :]
        for j in range(CONV_K - 1):
            y = y + convw_ref[j:j + 1, cs] * e_ref[5 + j:5 + j + gm, :]
        y = _silu(y)
        if part < 2:
            scale = HEAD_DIM ** -0.5 if part == 0 else 1.0
            y = y * (lax.rsqrt(group_sums(y * y) + RMS_EPS) * scale)
        out_ref[0, rows, :] = y.astype(bf16)
        n_ref[0:8, :] = e_ref[gm:gm + 8, :]

    z_ref[0, rows, :] = _silu(pz).astype(bf16)

    u_ref[0, rows, :] = _gelu_tanh(pu).astype(bf16)
    pv = _gelu_tanh(pv)
    for g in range(SGU_GROUPS):
        sl = slice(g * SGU_CHUNK, (g + 1) * SGU_CHUNK)
        vg = pv[:, sl]
        mu = jnp.mean(vg, axis=-1, keepdims=True)
        vcn = vg - mu
        var = jnp.mean(vcn * vcn, axis=-1, keepdims=True)
        vln_ref[0, rows, sl] = (vcn * lax.rsqrt(var + LN_EPS) * lng_ref[:, sl] + lnb_ref[:, sl]).astype(bf16)

    lane = _iota2((DN_CHUNK, 128), 1)
    beta = _sigmoid(pba)
    lane8 = _iota2((8, DN_CHUNK), 1)
    sub8 = _iota2((8, DN_CHUNK), 0)
    for c in range(gm // DN_CHUNK):
        rs = slice(c * DN_CHUNK, (c + 1) * DN_CHUNK)
        os_ = slice(r0 + c * DN_CHUNK, r0 + (c + 1) * DN_CHUNK)
        pbat = pba[rs].T[0:8, :]
        gt = -jnp.exp(prow_ref[0]) * _softplus(pbat + prow_ref[1])
        gc = jnp.where(sub8 >= DN_HEADS, gt, 0.0)
        shift = 1
        while shift < DN_CHUNK:
            gc = gc + jnp.where(lane8 >= shift, pltpu.roll(gc, shift, axis=1), 0.0)
            shift *= 2
        grow_ref[0, :, os_] = gc
        gc_col = jnp.concatenate([gc, jnp.zeros((DN_CHUNK - 8, DN_CHUNK), f32)], axis=0).T
        gcol_ref[0, os_, :] = jnp.where(lane < DN_HEADS, beta[rs], gc_col)


def _stage_inproj(x, w_re, conv_w, prow, lng, lnb, *, tm, ngroups):
    B, T, _ = x.shape
    wcols = w_re.shape[1]
    grid = (B, T // tm)
    gi = lax.broadcasted_iota(i32, (DN_WIDTH, DN_WIDTH), 0) // 128
    gj = lax.broadcasted_iota(i32, (DN_WIDTH, DN_WIDTH), 1) // 128
    group_ones = (gi == gj).astype(bf16)
    act = lambda: jax.ShapeDtypeStruct((B, T, DN_WIDTH), bf16)
    act_spec = lambda: pl.BlockSpec((1, tm, DN_WIDTH), lambda b, t: (b, t, 0))
    const2 = lambda shp: pl.BlockSpec(shp, lambda b, t: (0, 0))
    return pl.pallas_call(
        functools.partial(_inproj_body, tm=tm, ngroups=ngroups),
        grid=grid,
        in_specs=[
            pl.BlockSpec((1, tm, D_MODEL), lambda b, t: (b, t, 0)),
            const2((D_MODEL, wcols)),
            const2((CONV_K, 3 * DN_WIDTH)),
            pl.BlockSpec((2, 8, 128), lambda b, t: (0, 0, 0)),
            const2((1, SGU_WIDTH)),
            const2((1, SGU_WIDTH)),
            const2((DN_WIDTH, DN_WIDTH)),
        ],
        out_specs=[act_spec() for _ in range(6)] + [
            pl.BlockSpec((1, tm, 128), lambda b, t: (b, t, 0)),
            pl.BlockSpec((1, 8, tm), lambda b, t: (b, 0, t)),
        ],
        out_shape=[act() for _ in range(6)] + [
            jax.ShapeDtypeStruct((B, T, 128), f32),
            jax.ShapeDtypeStruct((B, 8, T), f32),
        ],
        scratch_shapes=[pltpu.VMEM((tm // ngroups + 8, DN_WIDTH), f32) for _ in range(3 * ngroups)],
        compiler_params=_cparams(("arbitrary", "arbitrary")),
        name="inproj",
    )(x, w_re, conv_w, prow, lng, lnb, group_ones)


def _mm(a, b):
    return jnp.dot(a.astype(bf16), b.astype(bf16), preferred_element_type=f32)


def _unit_lower_inverse(nmats, ii, jj):
    n = nmats[0].shape[0]
    eye = (ii == jj).astype(f32)
    leaf = jnp.right_shift(ii, 3) == jnp.right_shift(jj, 3)
    dblk = [jnp.where(leaf, m, 0.0) for m in nmats]
    s1 = [_mm(d, d) for d in dblk]
    r1 = [eye - d for d in dblk]
    both = [_mm(s, jnp.concatenate([s, r], axis=1)) for s, r in zip(s1, r1)]
    r2 = [r + bo[:, n:] for r, bo in zip(r1, both)]
    xs = [r + _mm(bo[:, :n], r) for r, bo in zip(r2, both)]
    size = 8
    while size < n:
        lows = [slice(r + size, r + 2 * size) for r in range(0, n, 2 * size)]
        ups = [slice(r, r + size) for r in range(0, n, 2 * size)]
        rsel = _iota2((n // 2, n), 0)
        ilow = rsel + size * (jnp.right_shift(rsel, size.bit_length() - 1) + 1)
        jlow = _iota2((n // 2, n), 1)
        in_pair_upper = (jlow >= ilow - (ilow & (2 * size - 1))) & (jlow < ilow - (ilow & (size - 1)))
        zeros = jnp.zeros((size, n), f32)
        new_xs = []
        ylows = [_mm(jnp.where(in_pair_upper, jnp.concatenate([m[s] for s in lows], axis=0), 0.0), x)
                 for m, x in zip(nmats, xs)]
        yfull = [jnp.concatenate([piece for k in range(len(lows)) for piece in (zeros, y[k * size:(k + 1) * size])], axis=0)
                 for y in ylows]
        corr = [_mm(jnp.concatenate([x[s] for s in lows], axis=0), yf) for x, yf in zip(xs, yfull)]
        for x, c in zip(xs, corr):
            pieces = []
            for k, (u, l) in enumerate(zip(ups, lows)):
                pieces += [x[u], x[l] - c[k * size:(k + 1) * size]]
            new_xs.append(jnp.concatenate(pieces, axis=0))
        xs = new_xs
        size *= 2
    return xs


def _deltanet_body(q_ref, k_ref, v_ref, z_ref, gcol_ref, grow_ref, nw_ref, y_ref, s_ref, *, nbr, nch, ngroups):
    @pl.when(pl.program_id(1) == 0)
    def _():
        s_ref[...] = jnp.zeros(s_ref.shape, f32)

    per = nbr // ngroups
    for g in range(ngroups):
        _deltanet_rows(q_ref, k_ref, v_ref, z_ref, gcol_ref, grow_ref, nw_ref, y_ref, s_ref,
                       rows=range(g * per, (g + 1) * per), nch=nch)


def _deltanet_rows(q_ref, k_ref, v_ref, z_ref, gcol_ref, grow_ref, nw_ref, y_ref, s_ref, *, rows, nch):
    C = DN_CHUNK
    S = [(b, h) for b in rows for h in range(DN_HEADS)]
    P = [(b, c, h) for c in range(nch) for b, h in S]

    ii = _iota2((C, C), 0)
    jj = _iota2((C, C), 1)
    causal = ii >= jj
    rs = [slice(c * C, (c + 1) * C) for c in range(nch)]
    hs = [slice(h * HEAD_DIM, (h + 1) * HEAD_DIM) for h in range(DN_HEADS)]
    gcol = {(b, c): gcol_ref[b, rs[c], :] for b in rows for c in range(nch)}
    qh = {(b, c, h): q_ref[b, rs[c], hs[h]] for b, c, h in P}
    kh = {(b, c, h): k_ref[b, rs[c], hs[h]] for b, c, h in P}
    vh = {(b, c, h): v_ref[b, rs[c], hs[h]] for b, c, h in P}
    gc_b = {(b, c, h): jnp.broadcast_to(gcol[b, c][:, DN_HEADS + h:DN_HEADS + h + 1], (C, HEAD_DIM)) for b, c, h in P}
    beta_b = {(b, c, h): jnp.broadcast_to(gcol[b, c][:, h:h + 1], (C, HEAD_DIM)) for b, c, h in P}
    gc_r = {(b, c, h): jnp.broadcast_to(grow_ref[b, DN_HEADS + h:DN_HEADS + h + 1, rs[c]], (C, C)) for b, c, h in P}
    decay = {p: jnp.exp(jnp.where(causal, gc_b[p] - gc_r[p], -1e30)) for p in P}

    kf = {p: kh[p].astype(f32) for p in P}
    kb = {p: kf[p] * beta_b[p] for p in P}
    kk = {p: lax.dot_general(kb[p].astype(bf16), kh[p], NT_DIMS, preferred_element_type=f32) for p in P}
    a_intra = {p: lax.dot_general(qh[p], kh[p], NT_DIMS, preferred_element_type=f32) * decay[p] for p in P}
    nmat = [jnp.where(ii > jj, kk[p] * decay[p], 0.0) for p in P]
    tinv = dict(zip(P, _unit_lower_inverse(nmat, ii, jj)))

    eg = {p: jnp.exp(gc_b[p]) for p in P}
    rhs = {p: jnp.concatenate([vh[p].astype(f32) * beta_b[p], kb[p] * eg[p]], axis=1) for p in P}
    sol = {p: _mm(tinv[p], rhs[p]) for p in P}

    q_dec = {p: qh[p].astype(f32) * eg[p] for p in P}
    g_last = {p: gc_b[p][C - 1:C, :] for p in P}
    kdt = {p: (kf[p] * jnp.exp(g_last[p] - gc_b[p])).T for p in P}

    state = {(b, h): s_ref[b * DN_HEADS + h] for b, h in S}
    for c in range(nch):
        m1 = {(b, h): _mm(jnp.concatenate([sol[b, c, h][:, HEAD_DIM:], q_dec[b, c, h]], axis=0), state[b, h]) for b, h in S}
        v_new = {(b, h): sol[b, c, h][:, :HEAD_DIM] - m1[b, h][:C] for b, h in S}
        m2 = {(b, h): _mm(jnp.concatenate([a_intra[b, c, h], kdt[b, c, h]], axis=0), v_new[b, h]) for b, h in S}
        state = {(b, h): state[b, h] * jnp.exp(g_last[b, c, h]) + m2[b, h][C:] for b, h in S}
        for b, h in S:
            o = m1[b, h][C:] + m2[b, h][:C]
            rms = lax.rsqrt(jnp.mean(o * o, axis=-1, keepdims=True) + RMS_EPS)
            y_ref[b, rs[c], hs[h]] = (o * rms * nw_ref[...] * z_ref[b, rs[c], hs[h]].astype(f32)).astype(bf16)
    for b, h in S:
        s_ref[b * DN_HEADS + h] = state[b, h]


def _stage_deltanet(q, k, v, z, gcol, grow, norm_w, *, nbr, nch, ngroups):
    B, T, _ = q.shape
    tt = nch * DN_CHUNK
    act_spec = lambda: pl.BlockSpec((nbr, tt, DN_WIDTH), lambda b, t: (b, t, 0))
    return pl.pallas_call(
        functools.partial(_deltanet_body, nbr=nbr, nch=nch, ngroups=ngroups),
        grid=(B // nbr, T // tt),
        in_specs=[act_spec(), act_spec(), act_spec(), act_spec(),
                  pl.BlockSpec((nbr, tt, 128), lambda b, t: (b, t, 0)),
                  pl.BlockSpec((nbr, 8, tt), lambda b, t: (b, 0, t)),
                  pl.BlockSpec((1, HEAD_DIM), lambda b, t: (0, 0))],
        out_specs=act_spec(),
        out_shape=jax.ShapeDtypeStruct((B, T, DN_WIDTH), bf16),
        scratch_shapes=[pltpu.VMEM((nbr * DN_HEADS, HEAD_DIM, HEAD_DIM), f32)],
        compiler_params=_cparams(("arbitrary", "arbitrary")),
        name="deltanet",
    )(q, k, v, z, gcol, grow, norm_w)


def _mixout_body(ydn_ref, u_ref, vln_ref, x_ref, ws_ref, bsp_ref, wout_ref, g1_ref, b1_ref, wrt_ref, brt_ref,
                 h_ref, hrow_ref, ids_ref, wts_ref, ycat_ref, *, tm):
    C = SGU_CHUNK
    ii = _iota2((C, C), 0)
    jj = _iota2((C, C), 1)
    ycat_ref[:, 0:DN_WIDTH] = ydn_ref[0]
    for g in range(SGU_GROUPS):
        gs = slice(g * C, (g + 1) * C)
        wsg = jnp.where(ii >= jj, ws_ref[g], 0.0).astype(bf16)
        for c in range(tm // C):
            rs = slice(c * C, (c + 1) * C)
            mixed = jnp.dot(wsg, vln_ref[0, rs, gs], preferred_element_type=f32) + bsp_ref[:, gs]
            ycat_ref[rs, DN_WIDTH + g * C:DN_WIDTH + (g + 1) * C] = (u_ref[0, rs, gs].astype(f32) * mixed).astype(bf16)

    RB = 128
    blocks = [slice(r, r + RB) for r in range(0, tm, RB)]
    mix = [jnp.dot(ycat_ref[rb, :], wout_ref[...], preferred_element_type=f32) for rb in blocks]
    h1s = []
    for rb, m in zip(blocks, mix):
        hp = DEEPNORM_ALPHA * x_ref[0, rb, :] + m
        mu = jnp.mean(hp, axis=-1, keepdims=True)
        hc = hp - mu
        var = jnp.mean(hc * hc, axis=-1, keepdims=True)
        h1 = hc * lax.rsqrt(var + LN_EPS) * g1_ref[...] + b1_ref[...]
        h_ref[0, rb, :] = h1
        h1b = h1.astype(bf16)
        hrow_ref[rb] = h1b.reshape(RB, ROW_TILE, 128)
        h1s.append(h1b)

    logit_blocks = [lax.dot_general(wrt_ref[...], hb, NT_DIMS, preferred_element_type=f32) + brt_ref[...] for hb in h1s]
    sub = _iota2((8, RB), 0)
    subf = sub.astype(f32)
    for rb, logits in zip(blocks, logit_blocks):
        gl = logits[0:8]
        gmax = jnp.max(gl, axis=0, keepdims=True)
        g_idx = jnp.min(jnp.where(gl == gmax, subf, float(MOE_GROUPS)), axis=0, keepdims=True)
        p_group = 1.0 / jnp.sum(jnp.exp(gl - gmax), axis=0, keepdims=True)
        within = jnp.zeros((8, RB), f32)
        for g in range(MOE_GROUPS):
            within = within + jnp.where(g_idx == float(g), logits[8 + 8 * g:16 + 8 * g], 0.0)
        m1 = jnp.max(within, axis=0, keepdims=True)
        i1 = jnp.min(jnp.where(within == m1, subf, float(EXPERTS_PER_GROUP)), axis=0, keepdims=True)
        rest = jnp.where(subf == i1, -jnp.inf, within)
        m2 = jnp.max(rest, axis=0, keepdims=True)
        i2 = jnp.min(jnp.where(rest == m2, subf, float(EXPERTS_PER_GROUP)), axis=0, keepdims=True)
        e = jnp.exp(m2 - m1)
        w1 = p_group / (1.0 + e)
        w2 = p_group * e / (1.0 + e)
        e1 = g_idx * float(EXPERTS_PER_GROUP) + i1
        e2 = g_idx * float(EXPERTS_PER_GROUP) + i2
        ids_ref[:, rb] = jnp.where(sub == 0, e1, jnp.where(sub == 1, e2, 0.0)).astype(i32)
        wts_ref[:, rb] = jnp.where(sub == 0, w1, jnp.where(sub == 1, w2, 0.0))


def _stage_mixout(ydn, u, vln, x, ws, bsp, wout, g1, b1, wrt, brt, *, tm):
    B, T, _ = x.shape
    nt = T // tm
    act_spec = lambda: pl.BlockSpec((1, tm, DN_WIDTH), lambda b, t: (b, t, 0))
    const2 = lambda shp: pl.BlockSpec(shp, lambda b, t: (0, 0))
    tok_spec = lambda: pl.BlockSpec((8, tm), lambda b, t: (0, b * nt + t))
    return pl.pallas_call(
        functools.partial(_mixout_body, tm=tm),
        grid=(B, nt),
        in_specs=[act_spec(), act_spec(), act_spec(),
                  pl.BlockSpec((1, tm, D_MODEL), lambda b, t: (b, t, 0)),
                  pl.BlockSpec((SGU_GROUPS, SGU_CHUNK, SGU_CHUNK), lambda b, t: (0, 0, 0)),
                  const2((SGU_CHUNK, SGU_WIDTH)),
                  const2((D_MODEL, D_MODEL)),
                  const2((1, D_MODEL)), const2((1, D_MODEL)),
                  const2((128, D_MODEL)), const2((128, 128))],
        out_specs=[pl.BlockSpec((1, tm, D_MODEL), lambda b, t: (b, t, 0)),
                   pl.BlockSpec((tm, ROW_TILE, 128), lambda b, t: (b * nt + t, 0, 0)), tok_spec(), tok_spec()],
        out_shape=[jax.ShapeDtypeStruct((B, T, D_MODEL), f32),
                   jax.ShapeDtypeStruct((B * T, ROW_TILE, 128), bf16),
                   jax.ShapeDtypeStruct((8, B * T), i32),
                   jax.ShapeDtypeStruct((8, B * T), f32)],
        scratch_shapes=[pltpu.VMEM((tm, D_MODEL), bf16)],
        compiler_params=_cparams(("arbitrary", "arbitrary")),
        name="mixout",
    )(ydn, u, vln, x, ws, bsp, wout, g1, b1, wrt, brt)


def _route_body(ids_ref, dest_ref, meta_ref, blk_ref, base_ref, pstart_ref, *, tm, nb_pad):
    phase = pl.program_id(0)
    i = pl.program_id(1)
    sub = _iota2((N_EXPERTS, tm), 0)
    is1 = sub == ids_ref[0:1, :]
    is2 = sub == ids_ref[1:2, :]
    oh = (is1.astype(f32) + is2.astype(f32)).astype(bf16)
    counts = jnp.dot(oh, jnp.ones((tm, 128), bf16), preferred_element_type=f32)

    @pl.when((phase == 0) & (i == 0))
    def _():
        base_ref[...] = jnp.zeros(base_ref.shape, f32)

    @pl.when(phase == 0)
    def _():
        base_ref[...] = base_ref[...] + counts

    @pl.when((phase == 1) & (i == 0))
    def _():
        cnt = base_ref[...]
        padded = jnp.floor((cnt + (MOE_BLOCK - 1)) * (1.0 / MOE_BLOCK)) * MOE_BLOCK
        ei = _iota2((N_EXPERTS, N_EXPERTS), 0)
        ej = _iota2((N_EXPERTS, N_EXPERTS), 1)
        pends = jnp.dot((ei >= ej).astype(f32), padded, precision=HIGHEST, preferred_element_type=f32)
        pstart = pends - padded
        pstart_ref[...] = pstart
        s64 = _iota2((N_EXPERTS, 128), 0)
        l64 = _iota2((N_EXPERTS, 128), 1)
        diag = s64 == l64
        fill_off = jnp.sum(jnp.where(diag, pstart + cnt, 0.0), axis=0, keepdims=True)
        fill_n = jnp.sum(jnp.where(diag, padded - cnt, 0.0), axis=0, keepdims=True)
        nused = pends[N_EXPERTS - 1:N_EXPERTS, :] * (1.0 / MOE_BLOCK)
        m8 = _iota2((8, 128), 0)
        meta_ref[...] = jnp.where(m8 == 0, fill_off, jnp.where(m8 == 1, fill_n, jnp.where(m8 == 2, nused, 0.0))).astype(i32)
        bstart = (_iota2((N_EXPERTS, nb_pad), 1) * MOE_BLOCK).astype(f32)
        pe = jnp.concatenate([pends] * (nb_pad // 128), axis=1)
        be = jnp.sum((pe <= bstart).astype(f32), axis=0, keepdims=True)
        be = jnp.minimum(be, float(N_EXPERTS - 1))
        blk_ref[...] = jnp.broadcast_to(be, (8, nb_pad)).astype(i32)

    @pl.when(phase == 1)
    def _():
        ti = _iota2((tm, tm), 0)
        tj = _iota2((tm, tm), 1)
        before = (ti < tj).astype(bf16)
        prefix = jnp.dot(oh, before, preferred_element_type=f32)
        nxt = prefix + jnp.concatenate([pstart_ref[...]] * (tm // 128), axis=1)
        d1 = jnp.sum(jnp.where(is1, nxt, 0.0), axis=0, keepdims=True)
        d2 = jnp.sum(jnp.where(is2, nxt, 0.0), axis=0, keepdims=True)
        sub8 = _iota2((8, tm), 0)
        dest_ref[...] = jnp.where(sub8 == 0, d1, jnp.where(sub8 == 1, d2, 0.0)).astype(i32)
        pstart_ref[...] = pstart_ref[...] + counts


def _stage_route(ids, *, tm, nb_pad):
    n = ids.shape[1]
    return pl.pallas_call(
        functools.partial(_route_body, tm=tm, nb_pad=nb_pad),
        grid=(2, n // tm),
        in_specs=[pl.BlockSpec((8, tm), lambda p, i: (0, i))],
        out_specs=[pl.BlockSpec((8, tm), lambda p, i: (0, i * p)),
                   pl.BlockSpec((8, 128), lambda p, i: (0, 0)),
                   pl.BlockSpec((8, nb_pad), lambda p, i: (0, 0))],
        out_shape=[jax.ShapeDtypeStruct((8, n), i32), jax.ShapeDtypeStruct((8, 128), i32),
                   jax.ShapeDtypeStruct((8, nb_pad), i32)],
        scratch_shapes=[pltpu.VMEM((N_EXPERTS, 128), f32), pltpu.VMEM((N_EXPERTS, 128), f32)],
        compiler_params=_cparams(("arbitrary", "arbitrary")),
        name="moe_route",
    )(ids)


def _dispatch_body(fill_off_ref, fill_n_ref, nused_ref, dest_ref, h3_ref, xs_ref, zero_ref, sem, zsem, *, tm):
    def row_copy(t, d):
        return pltpu.make_async_copy(h3_ref.at[t], xs_ref.at[d], sem)

    def issue(t, carry):
        row_copy(t, dest_ref[0, 0, t]).start(priority=0)
        row_copy(t, dest_ref[0, 1, t]).start(priority=1)
        return carry

    lax.fori_loop(0, tm, issue, 0, unroll=8)

    @pl.when(pl.program_id(0) == 0)
    def _():
        zero_ref[...] = jnp.zeros(zero_ref.shape, bf16)

        def fill(start):
            def body(e, carry):
                off = fill_off_ref[e]
                npad = fill_n_ref[e]
                bit = MOE_BLOCK // 2
                while bit:
                    @pl.when((npad & bit) != 0)
                    def _(off=off, bit=bit):
                        cp = pltpu.make_async_copy(zero_ref.at[pl.ds(0, bit)], xs_ref.at[pl.ds(off, bit)], zsem)
                        cp.start() if start else cp.wait()
                    off = off + (npad & bit)
                    bit //= 2
                return carry
            return body

        lax.fori_loop(0, N_EXPERTS, fill(True), 0)
        lax.fori_loop(0, N_EXPERTS, fill(False), 0)

        def tail_copy(b):
            return pltpu.make_async_copy(zero_ref, xs_ref.at[pl.ds(b * MOE_BLOCK, MOE_BLOCK)], zsem)

        nblocks = xs_ref.shape[0] // MOE_BLOCK
        lax.fori_loop(nused_ref[0], nblocks, lambda b, c: (tail_copy(b).start(), c)[1], 0)
        lax.fori_loop(nused_ref[0], nblocks, lambda b, c: (tail_copy(0).wait(), c)[1], 0)

    for _ in range(2):
        pltpu.make_async_copy(h3_ref, xs_ref.at[pl.ds(0, tm)], sem).wait()


def _stage_dispatch(fill_off, fill_n, nused, dest3, hrow, p_rows, *, tm):
    n = hrow.shape[0]
    return pl.pallas_call(
        functools.partial(_dispatch_body, tm=tm),
        grid_spec=pltpu.PrefetchScalarGridSpec(
            num_scalar_prefetch=3,
            grid=(n // tm,),
            in_specs=[pl.BlockSpec((1, 2, tm), lambda i, fo, fn, nu: (i, 0, 0), memory_space=pltpu.SMEM),
                      pl.BlockSpec((tm, ROW_TILE, 128), lambda i, fo, fn, nu: (i, 0, 0))],
            out_specs=pl.BlockSpec(memory_space=pl.ANY),
            scratch_shapes=[pltpu.VMEM((MOE_BLOCK, ROW_TILE, 128), bf16),
                            pltpu.SemaphoreType.DMA, pltpu.SemaphoreType.DMA],
        ),
        out_shape=jax.ShapeDtypeStruct((p_rows, ROW_TILE, 128), bf16),
        compiler_params=_cparams(("arbitrary",)),
        name="moe_dispatch",
    )(fill_off, fill_n, nused, dest3, hrow)


def _experts_body(blk_ref, nused_ref, xs_hbm, wg_hbm, wu_hbm, wd_hbm, ys_ref,
                  xbuf_ref, wg32_ref, wu32_ref, wd32_ref, wgu16_ref, wd16_ref, xsem, wsem):
    i = pl.program_id(0)
    nused = nused_ref[0]
    used = i < nused
    e = blk_ref[i]
    slot = lax.rem(i, X_SLOTS)

    def weight_copies(ex):
        return (pltpu.make_async_copy(wg_hbm.at[ex], wg32_ref, wsem.at[0]),
                pltpu.make_async_copy(wu_hbm.at[ex], wu32_ref, wsem.at[1]),
                pltpu.make_async_copy(wd_hbm.at[ex], wd32_ref, wsem.at[2]))

    def x_copy(block, s):
        return pltpu.make_async_copy(xs_hbm.at[pl.ds(block * MOE_BLOCK, MOE_BLOCK)], xbuf_ref.at[s], xsem.at[s])

    @pl.when((i == 0) & used)
    def _():
        for cp in weight_copies(e):
            cp.start()
        for j in range(X_SLOTS - 1):
            @pl.when(j < nused)
            def _(j=j):
                x_copy(j, j).start()

    ahead = i + (X_SLOTS - 1)

    @pl.when(ahead < nused)
    def _():
        x_copy(ahead, lax.rem(ahead, X_SLOTS)).start()

    @pl.when(used & ((i == 0) | (e != blk_ref[jnp.maximum(i - 1, 0)])))
    def _():
        for cp in weight_copies(e):
            cp.wait()
        wgu16_ref[:, 0:D_EXPERT] = wg32_ref[...].astype(bf16)
        wgu16_ref[:, D_EXPERT:2 * D_EXPERT] = wu32_ref[...].astype(bf16)
        wd16_ref[...] = wd32_ref[...].astype(bf16)
        nxt = lax.while_loop(lambda j: (j < nused) & (blk_ref[jnp.minimum(j, nused - 1)] == e), lambda j: j + 1, i + 1)

        @pl.when(nxt < nused)
        def _():
            for cp in weight_copies(blk_ref[jnp.minimum(nxt, nused - 1)]):
                cp.start(priority=1)

    @pl.when(used)
    def _():
        x_copy(i, slot).wait()
        half = MOE_BLOCK // 2
        rows = [slice(p * half, (p + 1) * half) for p in range(2)]
        gu = [jnp.dot(xbuf_ref[slot, r].reshape(half, D_MODEL), wgu16_ref[...], preferred_element_type=f32)
              for r in rows]
        hid = [(_silu(g[:, :D_EXPERT]) * g[:, D_EXPERT:]).astype(bf16) for g in gu]
        y = [jnp.dot(hd, wd16_ref[...], preferred_element_type=f32) for hd in hid]
        for r, yp in zip(rows, y):
            ys_ref[r] = yp.astype(bf16).reshape(half, ROW_TILE, 128)

    @pl.when(jnp.logical_not(used))
    def _():
        ys_ref[...] = jnp.zeros(ys_ref.shape, bf16)


def _stage_experts(blk_e, nused, xs, w_gate, w_up, w_down):
    p_rows = xs.shape[0]
    nb = p_rows // MOE_BLOCK

    def row_map(i, blk, nu):
        return (i, 0, 0)

    return pl.pallas_call(
        _experts_body,
        grid_spec=pltpu.PrefetchScalarGridSpec(
            num_scalar_prefetch=2,
            grid=(nb,),
            in_specs=[pl.BlockSpec(memory_space=pl.ANY),
                      pl.BlockSpec(memory_space=pl.ANY),
                      pl.BlockSpec(memory_space=pl.ANY),
                      pl.BlockSpec(memory_space=pl.ANY)],
            out_specs=pl.BlockSpec((MOE_BLOCK, ROW_TILE, 128), row_map),
            scratch_shapes=[pltpu.VMEM((X_SLOTS, MOE_BLOCK, ROW_TILE, 128), bf16),
                            pltpu.VMEM((D_MODEL, D_EXPERT), f32), pltpu.VMEM((D_MODEL, D_EXPERT), f32),
                            pltpu.VMEM((D_EXPERT, D_MODEL), f32),
                            pltpu.VMEM((D_MODEL, 2 * D_EXPERT), bf16), pltpu.VMEM((D_EXPERT, D_MODEL), bf16),
                            pltpu.SemaphoreType.DMA((X_SLOTS,)), pltpu.SemaphoreType.DMA((3,))],
        ),
        out_shape=jax.ShapeDtypeStruct((p_rows, ROW_TILE, 128), bf16),
        compiler_params=_cparams(("arbitrary",)),
        name="moe_experts",
    )(blk_e, nused, xs, w_gate, w_up, w_down)


def _combine_body(dcur_ref, dnext_ref, h_ref, wts_ref, g2_ref, b2_ref, ys_ref, o_ref, ybuf_ref, sem, *, tm, nsteps):
    i = pl.program_id(0)
    slot = lax.rem(i, 2)

    def issue_tile(d_ref, s):
        def body(t, carry):
            for k in range(2):
                pltpu.make_async_copy(ys_ref.at[d_ref[0, k, t]], ybuf_ref.at[s, k, t], sem.at[s]).start(priority=k)
            return carry

        lax.fori_loop(0, tm, body, 0, unroll=8)

    @pl.when(i == 0)
    def _():
        issue_tile(dcur_ref, 0)

    @pl.when(i + 1 < nsteps)
    def _():
        issue_tile(dnext_ref, 1 - slot)

    for k in range(2):
        pltpu.make_async_copy(ys_ref.at[pl.ds(0, tm)], ybuf_ref.at[slot, k], sem.at[slot]).wait()

    pieces = []
    for c in range(tm // 128):
        ls = slice(c * 128, (c + 1) * 128)
        w1c = jnp.broadcast_to(wts_ref[0:1, ls], (128, 128)).T
        w2c = jnp.broadcast_to(wts_ref[1:2, ls], (128, 128)).T
        w1f = jnp.concatenate([w1c] * (D_MODEL // 128), axis=1)
        w2f = jnp.concatenate([w2c] * (D_MODEL // 128), axis=1)
        y1 = ybuf_ref[slot, 0, ls].reshape(128, D_MODEL).astype(f32)
        y2 = ybuf_ref[slot, 1, ls].reshape(128, D_MODEL).astype(f32)
        pieces.append(w1f * y1 + w2f * y2)
    ffn = jnp.concatenate(pieces, axis=0)
    hp = DEEPNORM_ALPHA * h_ref[...] + ffn
    mu = jnp.mean(hp, axis=-1, keepdims=True)
    hc = hp - mu
    var = jnp.mean(hc * hc, axis=-1, keepdims=True)
    o_ref[...] = hc * lax.rsqrt(var + LN_EPS) * g2_ref[...] + b2_ref[...]


def _stage_combine(dest3, h2, wts, g2, b2, ys, *, tm):
    n = h2.shape[0]
    nsteps = n // tm
    return pl.pallas_call(
        functools.partial(_combine_body, tm=tm, nsteps=nsteps),
        grid=(nsteps,),
        in_specs=[pl.BlockSpec((1, 2, tm), lambda i: (i, 0, 0), memory_space=pltpu.SMEM),
                  pl.BlockSpec((1, 2, tm), lambda i: (jnp.minimum(i + 1, nsteps - 1), 0, 0), memory_space=pltpu.SMEM),
                  pl.BlockSpec((tm, D_MODEL), lambda i: (i, 0)),
                  pl.BlockSpec((8, tm), lambda i: (0, i)),
                  pl.BlockSpec((1, D_MODEL), lambda i: (0, 0)),
                  pl.BlockSpec((1, D_MODEL), lambda i: (0, 0)),
                  pl.BlockSpec(memory_space=pl.ANY)],
        out_specs=pl.BlockSpec((tm, D_MODEL), lambda i: (i, 0)),
        out_shape=jax.ShapeDtypeStruct((n, D_MODEL), f32),
        scratch_shapes=[pltpu.VMEM((2, 2, tm, ROW_TILE, 128), bf16), pltpu.SemaphoreType.DMA((2,))],
        compiler_params=_cparams(("arbitrary",)),
        name="moe_combine",
    )(dest3, dest3, h2, wts, g2, b2, ys)


def _layer(h, w_in, conv_w, a_log, dt_bias, dn_norm_w, sgu_ln_g, sgu_ln_b, w_spatial, b_spatial, w_out,
           ln1_g, ln1_b, w_rg, b_rg, w_re, b_re, w_gate, w_up, w_down, ln2_g, ln2_b,
           *, tm_in, in_groups, dn_rows, dn_chunks, dn_groups, tm_mix, tm_rank, tm_disp, tm_comb):
    B, T, _ = h.shape
    n = B * T
    qkvz = 4 * DN_WIDTH
    ba0 = qkvz
    uv0 = qkvz + 2 * DN_HEADS
    w_cols = jnp.zeros((D_MODEL, IN_COLS_ALIGNED), bf16)
    w_cols = lax.dynamic_update_slice(w_cols, w_in[:, :qkvz].astype(bf16), (0, 0))
    w_cols = lax.dynamic_update_slice(w_cols, w_in[:, uv0:].astype(bf16), (0, qkvz))
    w_cols = lax.dynamic_update_slice(w_cols, w_in[:, ba0:uv0].astype(bf16), (0, qkvz + 2 * SGU_WIDTH))
    decay_prm = jnp.stack([a_log, dt_bias])
    prow =jnp.broadcast_to(jnp.pad(decay_prm, ((0, 0), (DN_HEADS, 8 - 2 * DN_HEADS)))[:, :, None], (2, 8, 128))

    q, k, v, z, u, vln, gcol, grow = _stage_inproj(
        h, w_cols, conv_w, prow, sgu_ln_g[None, :], sgu_ln_b[None, :], tm=tm_in, ngroups=in_groups)
    ydn = _stage_deltanet(q, k, v, z, gcol, grow, dn_norm_w[None, :], nbr=dn_rows, nch=dn_chunks, ngroups=dn_groups)

    bsp = jnp.broadcast_to(b_spatial.T[:, :, None], (SGU_CHUNK, SGU_GROUPS, SGU_CHUNK)).reshape(SGU_CHUNK, SGU_WIDTH)
    n_logit = MOE_GROUPS + N_EXPERTS
    wrt = jnp.pad(jnp.concatenate([w_rg, w_re], axis=1).T, ((0, 128 - n_logit), (0, 0))).astype(bf16)
    brt = jnp.broadcast_to(jnp.pad(jnp.concatenate([b_rg, b_re]), (0, 128 - n_logit))[:, None], (128, 128))
    h1, hrow, ids, wts = _stage_mixout(ydn, u, vln, h, w_spatial, bsp, w_out.astype(bf16), ln1_g[None, :],
                                       ln1_b[None, :], wrt, brt, tm=tm_mix)

    p_rows = (-(-(n * 2) // MOE_BLOCK)) * MOE_BLOCK + N_EXPERTS * MOE_BLOCK
    nb = p_rows // MOE_BLOCK
    nb_pad = (-(-nb // 128)) * 128
    dest, meta, blk = _stage_route(ids, tm=tm_rank, nb_pad=nb_pad)

    h2 = h1.reshape(n, D_MODEL)
    dest_d = dest[0:2].reshape(2, n // tm_disp, tm_disp).transpose(1, 0, 2)
    xs = _stage_dispatch(meta[0, :N_EXPERTS], meta[1, :N_EXPERTS], meta[2, 0:1], dest_d, hrow, p_rows, tm=tm_disp)
    ys = _stage_experts(blk[0, :nb], meta[2, 0:1], xs, w_gate, w_up, w_down)
    dest_c = dest[0:2].reshape(2, n // tm_comb, tm_comb).transpose(1, 0, 2)
    out = _stage_combine(dest_c, h2, wts, ln2_g[None, :], ln2_b[None, :], ys, tm=tm_comb)
    return out.reshape(B, T, D_MODEL)


def kernel(x, w_in, conv_w, a_log, dt_bias, dn_norm_w, sgu_ln_g, sgu_ln_b, w_spatial, b_spatial, w_out, ln1_g, ln1_b, w_router_group, b_router_group, w_router_expert, b_router_expert, w_gate, w_up, w_down, ln2_g, ln2_b):
    h = x
    for l in range(w_in.shape[0]):
        h = _layer(h, w_in[l], conv_w[l], a_log[l], dt_bias[l], dn_norm_w[l], sgu_ln_g[l], sgu_ln_b[l],
                   w_spatial[l], b_spatial[l], w_out[l], ln1_g[l], ln1_b[l],
                   w_router_group[l], b_router_group[l], w_router_expert[l], b_router_expert[l],
                   w_gate[l], w_up[l], w_down[l], ln2_g[l], ln2_b[l],
                   tm_in=512, in_groups=1, dn_rows=4, dn_chunks=2, dn_groups=2, tm_mix=512, tm_rank=1024, tm_disp=2048, tm_comb=512)
    return h
```

```python
import functools

import jax
import jax.numpy as jnp
from jax import lax
from jax.experimental import pallas as pl
from jax.experimental.pallas import tpu as pltpu

f32 = jnp.float32
bf16 = jnp.bfloat16
i32 = jnp.int32

D_MODEL = 1024
DN_WIDTH = 512
DN_HEADS = 4
HEAD_DIM = 128
CONV_K = 4
SGU_WIDTH = 512
SGU_GROUPS = 4
SGU_CHUNK = 128
DN_CHUNK = 128
MOE_GROUPS = 8
EXPERTS_PER_GROUP = 8
N_EXPERTS = 64
D_EXPERT = 512
MOE_BLOCK = 256
IN_COLS_ALIGNED = 4 * DN_WIDTH + 2 * SGU_WIDTH + 128
X_SLOTS = 3
ROW_TILE = D_MODEL // 128
DEEPNORM_ALPHA = 2.0 ** 0.25
LN_EPS = 1e-5
RMS_EPS = 1e-6
HIGHEST = lax.Precision.HIGHEST
VMEM_LIMIT_BYTES = 56 * 1024 * 1024

NT_DIMS = (((1,), (1,)), ((), ()))


def _cparams(sem, flags=None):
    return pltpu.CompilerParams(dimension_semantics=sem, vmem_limit_bytes=VMEM_LIMIT_BYTES, flags=flags)


def _sigmoid(x):
    return 1.0 / (1.0 + jnp.exp(-x))


def _silu(x):
    h = 0.5 * x
    return h + h * jnp.tanh(h)


def _softplus(x):
    return jnp.maximum(x, 0.0) + jnp.log1p(jnp.exp(-jnp.abs(x)))


def _gelu_tanh(x):
    c = 0.7978845608028654
    return x * (0.5 * (1.0 + jnp.tanh(c * (x + 0.044715 * (x * x * x)))))


def _iota2(shape, axis):
    return lax.broadcasted_iota(i32, shape, axis)


def _inproj_body(x_ref, w_ref, convw_ref, prow_ref, lng_ref, lnb_ref, ones_ref,
                 q_ref, k_ref, v_ref, z_ref, u_ref, vln_ref, gcol_ref, grow_ref, *ext_refs, tm, ngroups):
    W = DN_WIDTH
    gm = tm // ngroups
    ext = [ext_refs[3 * g:3 * g + 3] for g in range(ngroups)]

    @pl.when(pl.program_id(1) == 0)
    def _():
        for e_ref in ext[0]:
            e_ref[0:8, :] = jnp.zeros((8, W), f32)

    for g in range(ngroups):
        _inproj_rows(x_ref, w_ref, convw_ref, prow_ref, lng_ref, lnb_ref, ones_ref,
                     q_ref, k_ref, v_ref, z_ref, u_ref, vln_ref, gcol_ref, grow_ref, ext[g],
                     ext[(g + 1) % ngroups], r0=g * gm, gm=gm)


def _inproj_rows(x_ref, w_ref, convw_ref, prow_ref, lng_ref, lnb_ref, ones_ref,
                 q_ref, k_ref, v_ref, z_ref, u_ref, vln_ref, gcol_ref, grow_ref, ext, ext_next, *, r0, gm):
    W = DN_WIDTH
    rows = slice(r0, r0 + gm)
    xb = x_ref[0, rows, :].astype(bf16)
    for part, e_ref in enumerate(ext):
        e_ref[8:8 + gm, :] = jnp.dot(xb, w_ref[:, part * W:(part + 1) * W], preferred_element_type=f32)
    zc = 3 * W
    uc = zc + W
    vc = uc + SGU_WIDTH
    bc = vc + SGU_WIDTH
    pz = jnp.dot(xb, w_ref[:, zc:zc + W], preferred_element_type=f32)
    pu = jnp.dot(xb, w_ref[:, uc:uc + SGU_WIDTH], preferred_element_type=f32)
    pv = jnp.dot(xb, w_ref[:, vc:vc + SGU_WIDTH], preferred_element_type=f32)
    pba = jnp.dot(xb, w_ref[:, bc:bc + 128], preferred_element_type=f32)

    def group_sums(a):
        return jnp.dot(a.astype(bf16), ones_ref[...], preferred_element_type=f32)

    for part, (e_ref, n_ref, out_ref) in enumerate(zip(ext, ext_next, (q_ref, k_ref, v_ref))):
        cs = slice(part * W, (part + 1) * W)
        y = convw_ref[3:4, cs] * e_ref[8:8 + gm, :]
        for j in range(CONV_K - 1):
            y = y + convw_ref[j:j + 1, cs] * e_ref[5 + j:5 + j + gm, :]
        y = _silu(y)
        if part < 2:
            scale = HEAD_DIM ** -0.5 if part == 0 else 1.0
            y = y * (lax.rsqrt(group_sums(y * y) + RMS_EPS) * scale)
        out_ref[0, rows, :] = y.astype(bf16)
        n_ref[0:8, :] = e_ref[gm:gm + 8, :]

    z_ref[0, rows, :] = _silu(pz).astype(bf16)

    u_ref[0, rows, :] = _gelu_tanh(pu).astype(bf16)
    pv = _gelu_tanh(pv)
    for g in range(SGU_GROUPS):
        sl = slice(g * SGU_CHUNK, (g + 1) * SGU_CHUNK)
        vg = pv[:, sl]
        mu = jnp.mean(vg, axis=-1, keepdims=True)
        vcn = vg - mu
        var = jnp.mean(vcn * vcn, axis=-1, keepdims=True)
        vln_ref[0, rows, sl] = (vcn * lax.rsqrt(var + LN_EPS) * lng_ref[:, sl] + lnb_ref[:, sl]).astype(bf16)

    lane = _iota2((DN_CHUNK, 128), 1)
    beta = _sigmoid(pba)
    lane8 = _iota2((8, DN_CHUNK), 1)
    sub8 = _iota2((8, DN_CHUNK), 0)
    for c in range(gm // DN_CHUNK):
        rs = slice(c * DN_CHUNK, (c + 1) * DN_CHUNK)
        os_ = slice(r0 + c * DN_CHUNK, r0 + (c + 1) * DN_CHUNK)
        pbat = pba[rs].T[0:8, :]
        gt = -jnp.exp(prow_ref[0]) * _softplus(pbat + prow_ref[1])
        gc = jnp.where(sub8 >= DN_HEADS, gt, 0.0)
        shift = 1
        while shift < DN_CHUNK:
            gc = gc + jnp.where(lane8 >= shift, pltpu.roll(gc, shift, axis=1), 0.0)
            shift *= 2
        grow_ref[0, :, os_] = gc
        gc_col = jnp.concatenate([gc, jnp.zeros((DN_CHUNK - 8, DN_CHUNK), f32)], axis=0).T
        gcol_ref[0, os_, :] = jnp.where(lane < DN_HEADS, beta[rs], gc_col)


def _stage_inproj(x, w_re, conv_w, prow, lng, lnb, *, tm, ngroups):
    B, T, _ = x.shape
    wcols = w_re.shape[1]
    grid = (B, T // tm)
    gi = lax.broadcasted_iota(i32, (DN_WIDTH, DN_WIDTH), 0) // 128
    gj = lax.broadcasted_iota(i32, (DN_WIDTH, DN_WIDTH), 1) // 128
    group_ones = (gi == gj).astype(bf16)
    act = lambda: jax.ShapeDtypeStruct((B, T, DN_WIDTH), bf16)
    act_spec = lambda: pl.BlockSpec((1, tm, DN_WIDTH), lambda b, t: (b, t, 0))
    const2 = lambda shp: pl.BlockSpec(shp, lambda b, t: (0, 0))
    return pl.pallas_call(
        functools.partial(_inproj_body, tm=tm, ngroups=ngroups),
        grid=grid,
        in_specs=[
            pl.BlockSpec((1, tm, D_MODEL), lambda b, t: (b, t, 0)),
            const2((D_MODEL, wcols)),
            const2((CONV_K, 3 * DN_WIDTH)),
            pl.BlockSpec((2, 8, 128), lambda b, t: (0, 0, 0)),
            const2((1, SGU_WIDTH)),
            const2((1, SGU_WIDTH)),
            const2((DN_WIDTH, DN_WIDTH)),
        ],
        out_specs=[act_spec() for _ in range(6)] + [
            pl.BlockSpec((1, tm, 128), lambda b, t: (b, t, 0)),
            pl.BlockSpec((1, 8, tm), lambda b, t: (b, 0, t)),
        ],
        out_shape=[act() for _ in range(6)] + [
            jax.ShapeDtypeStruct((B, T, 128), f32),
            jax.ShapeDtypeStruct((B, 8, T), f32),
        ],
        scratch_shapes=[pltpu.VMEM((tm // ngroups + 8, DN_WIDTH), f32) for _ in range(3 * ngroups)],
        compiler_params=_cparams(("arbitrary", "arbitrary")),
        name="inproj",
    )(x, w_re, conv_w, prow, lng, lnb, group_ones)


def _mm(a, b):
    return jnp.dot(a.astype(bf16), b.astype(bf16), preferred_element_type=f32)


def _unit_lower_inverse(nmats, ii, jj):
    n = nmats[0].shape[0]
    eye = (ii == jj).astype(f32)
    leaf = jnp.right_shift(ii, 3) == jnp.right_shift(jj, 3)
    dblk = [jnp.where(leaf, m, 0.0) for m in nmats]
    s1 = [_mm(d, d) for d in dblk]
    r1 = [eye - d for d in dblk]
    both = [_mm(s, jnp.concatenate([s, r], axis=1)) for s, r in zip(s1, r1)]
    r2 = [r + bo[:, n:] for r, bo in zip(r1, both)]
    xs = [r + _mm(bo[:, :n], r) for r, bo in zip(r2, both)]
    size = 8
    while size < n:
        lows = [slice(r + size, r + 2 * size) for r in range(0, n, 2 * size)]
        ups = [slice(r, r + size) for r in range(0, n, 2 * size)]
        rsel = _iota2((n // 2, n), 0)
        ilow = rsel + size * (jnp.right_shift(rsel, size.bit_length() - 1) + 1)
        jlow = _iota2((n // 2, n), 1)
        in_pair_upper = (jlow >= ilow - (ilow & (2 * size - 1))) & (jlow < ilow - (ilow & (size - 1)))
        zeros = jnp.zeros((size, n), f32)
        new_xs = []
        ylows = [_mm(jnp.where(in_pair_upper, jnp.concatenate([m[s] for s in lows], axis=0), 0.0), x)
                 for m, x in zip(nmats, xs)]
        yfull = [jnp.concatenate([piece for k in range(len(lows)) for piece in (zeros, y[k * size:(k + 1) * size])], axis=0)
                 for y in ylows]
        corr = [_mm(jnp.concatenate([x[s] for s in lows], axis=0), yf) for x, yf in zip(xs, yfull)]
        for x, c in zip(xs, corr):
            pieces = []
            for k, (u, l) in enumerate(zip(ups, lows)):
                pieces += [x[u], x[l] - c[k * size:(k + 1) * size]]
            new_xs.append(jnp.concatenate(pieces, axis=0))
        xs = new_xs
        size *= 2
    return xs


def _deltanet_body(q_ref, k_ref, v_ref, z_ref, gcol_ref, grow_ref, nw_ref, y_ref, s_ref, *, nbr, nch, ngroups):
    @pl.when(pl.program_id(1) == 0)
    def _():
        s_ref[...] = jnp.zeros(s_ref.shape, f32)

    per = nbr // ngroups
    for g in range(ngroups):
        _deltanet_rows(q_ref, k_ref, v_ref, z_ref, gcol_ref, grow_ref, nw_ref, y_ref, s_ref,
                       rows=range(g * per, (g + 1) * per), nch=nch)


def _deltanet_rows(q_ref, k_ref, v_ref, z_ref, gcol_ref, grow_ref, nw_ref, y_ref, s_ref, *, rows, nch):
    C = DN_CHUNK
    S = [(b, h) for b in rows for h in range(DN_HEADS)]
    P = [(b, c, h) for c in range(nch) for b, h in S]

    ii = _iota2((C, C), 0)
    jj = _iota2((C, C), 1)
    causal = ii >= jj
    rs = [slice(c * C, (c + 1) * C) for c in range(nch)]
    hs = [slice(h * HEAD_DIM, (h + 1) * HEAD_DIM) for h in range(DN_HEADS)]
    gcol = {(b, c): gcol_ref[b, rs[c], :] for b in rows for c in range(nch)}
    qh = {(b, c, h): q_ref[b, rs[c], hs[h]] for b, c, h in P}
    kh = {(b, c, h): k_ref[b, rs[c], hs[h]] for b, c, h in P}
    vh = {(b, c, h): v_ref[b, rs[c], hs[h]] for b, c, h in P}
    gc_b = {(b, c, h): jnp.broadcast_to(gcol[b, c][:, DN_HEADS + h:DN_HEADS + h + 1], (C, HEAD_DIM)) for b, c, h in P}
    beta_b = {(b, c, h): jnp.broadcast_to(gcol[b, c][:, h:h + 1], (C, HEAD_DIM)) for b, c, h in P}
    gc_r = {(b, c, h): jnp.broadcast_to(grow_ref[b, DN_HEADS + h:DN_HEADS + h + 1, rs[c]], (C, C)) for b, c, h in P}
    decay = {p: jnp.exp(jnp.where(causal, gc_b[p] - gc_r[p], -1e30)) for p in P}

    kf = {p: kh[p].astype(f32) for p in P}
    kb = {p: kf[p] * beta_b[p] for p in P}
    kk = {p: lax.dot_general(kb[p].astype(bf16), kh[p], NT_DIMS, preferred_element_type=f32) for p in P}
    a_intra = {p: lax.dot_general(qh[p], kh[p], NT_DIMS, preferred_element_type=f32) * decay[p] for p in P}
    nmat = [jnp.where(ii > jj, kk[p] * decay[p], 0.0) for p in P]
    tinv = dict(zip(P, _unit_lower_inverse(nmat, ii, jj)))

    eg = {p: jnp.exp(gc_b[p]) for p in P}
    rhs = {p: jnp.concatenate([vh[p].astype(f32) * beta_b[p], kb[p] * eg[p]], axis=1) for p in P}
    sol = {p: _mm(tinv[p], rhs[p]) for p in P}

    q_dec = {p: qh[p].astype(f32) * eg[p] for p in P}
    g_last = {p: gc_b[p][C - 1:C, :] for p in P}
    kdt = {p: (kf[p] * jnp.exp(g_last[p] - gc_b[p])).T for p in P}

    state = {(b, h): s_ref[b * DN_HEADS + h] for b, h in S}
    for c in range(nch):
        m1 = {(b, h): _mm(jnp.concatenate([sol[b, c, h][:, HEAD_DIM:], q_dec[b, c, h]], axis=0), state[b, h]) for b, h in S}
        v_new = {(b, h): sol[b, c, h][:, :HEAD_DIM] - m1[b, h][:C] for b, h in S}
        m2 = {(b, h): _mm(jnp.concatenate([a_intra[b, c, h], kdt[b, c, h]], axis=0), v_new[b, h]) for b, h in S}
        state = {(b, h): state[b, h] * jnp.exp(g_last[b, c, h]) + m2[b, h][C:] for b, h in S}
        for b, h in S:
            o = m1[b, h][C:] + m2[b, h][:C]
            rms = lax.rsqrt(jnp.mean(o * o, axis=-1, keepdims=True) + RMS_EPS)
            y_ref[b, rs[c], hs[h]] = (o * rms * nw_ref[...] * z_ref[b, rs[c], hs[h]].astype(f32)).astype(bf16)
    for b, h in S:
        s_ref[b * DN_HEADS + h] = state[b, h]


def _stage_deltanet(q, k, v, z, gcol, grow, norm_w, *, nbr, nch, ngroups):
    B, T, _ = q.shape
    tt = nch * DN_CHUNK
    act_spec = lambda: pl.BlockSpec((nbr, tt, DN_WIDTH), lambda b, t: (b, t, 0))
    return pl.pallas_call(
        functools.partial(_deltanet_body, nbr=nbr, nch=nch, ngroups=ngroups),
        grid=(B // nbr, T // tt),
        in_specs=[act_spec(), act_spec(), act_spec(), act_spec(),
                  pl.BlockSpec((nbr, tt, 128), lambda b, t: (b, t, 0)),
                  pl.BlockSpec((nbr, 8, tt), lambda b, t: (b, 0, t)),
                  pl.BlockSpec((1, HEAD_DIM), lambda b, t: (0, 0))],
        out_specs=act_spec(),
        out_shape=jax.ShapeDtypeStruct((B, T, DN_WIDTH), bf16),
        scratch_shapes=[pltpu.VMEM((nbr * DN_HEADS, HEAD_DIM, HEAD_DIM), f32)],
        compiler_params=_cparams(("arbitrary", "arbitrary")),
        name="deltanet",
    )(q, k, v, z, gcol, grow, norm_w)


def _mixout_body(ydn_ref, u_ref, vln_ref, x_hbm, ws_ref, bsp_ref, wout_ref, g1_ref, b1_ref, wrt_ref, brt_ref,
                 h_ref, hrow_ref, ids_ref, wts_ref, ycat_ref, xring_ref, xsem, *, tm, nsteps):
    C = SGU_CHUNK
    step = pl.program_id(0) * pl.num_programs(1) + pl.program_id(1)
    slot = lax.rem(step, X_SLOTS)

    def x_copy(s, sl):
        return pltpu.make_async_copy(x_hbm.at[s], xring_ref.at[sl], xsem.at[sl])

    @pl.when(step == 0)
    def _():
        for j in range(X_SLOTS - 1):
            x_copy(j, j).start()

    ahead = step + (X_SLOTS - 1)

    @pl.when(ahead < nsteps)
    def _():
        x_copy(ahead, lax.rem(ahead, X_SLOTS)).start()

    x_copy(step, slot).wait()
    ii = _iota2((C, C), 0)
    jj = _iota2((C, C), 1)
    ycat_ref[:, 0:DN_WIDTH] = ydn_ref[0]
    for g in range(SGU_GROUPS):
        gs = slice(g * C, (g + 1) * C)
        wsg = jnp.where(ii >= jj, ws_ref[g], 0.0).astype(bf16)
        for c in range(tm // C):
            rs = slice(c * C, (c + 1) * C)
            mixed = jnp.dot(wsg, vln_ref[0, rs, gs], preferred_element_type=f32) + bsp_ref[:, gs]
            ycat_ref[rs, DN_WIDTH + g * C:DN_WIDTH + (g + 1) * C] = (u_ref[0, rs, gs].astype(f32) * mixed).astype(bf16)

    RB = 128
    blocks = [slice(r, r + RB) for r in range(0, tm, RB)]
    mix = [jnp.dot(ycat_ref[rb, :], wout_ref[...], preferred_element_type=f32) for rb in blocks]
    h1s = []
    for rb, m in zip(blocks, mix):
        hp = DEEPNORM_ALPHA * xring_ref[slot, rb, :] + m
        mu = jnp.mean(hp, axis=-1, keepdims=True)
        hc = hp - mu
        var = jnp.mean(hc * hc, axis=-1, keepdims=True)
        h1 = hc * lax.rsqrt(var + LN_EPS) * g1_ref[...] + b1_ref[...]
        h_ref[0, rb, :] = h1
        h1b = h1.astype(bf16)
        hrow_ref[rb] = h1b.reshape(RB, ROW_TILE, 128)
        h1s.append(h1b)

    logit_blocks = [lax.dot_general(wrt_ref[...], hb, NT_DIMS, preferred_element_type=f32) + brt_ref[...] for hb in h1s]
    sub = _iota2((8, RB), 0)
    subf = sub.astype(f32)
    for rb, logits in zip(blocks, logit_blocks):
        gl = logits[0:8]
        gmax = jnp.max(gl, axis=0, keepdims=True)
        g_idx = jnp.min(jnp.where(gl == gmax, subf, float(MOE_GROUPS)), axis=0, keepdims=True)
        p_group = 1.0 / jnp.sum(jnp.exp(gl - gmax), axis=0, keepdims=True)
        within = jnp.zeros((8, RB), f32)
        for g in range(MOE_GROUPS):
            within = within + jnp.where(g_idx == float(g), logits[8 + 8 * g:16 + 8 * g], 0.0)
        m1 = jnp.max(within, axis=0, keepdims=True)
        i1 = jnp.min(jnp.where(within == m1, subf, float(EXPERTS_PER_GROUP)), axis=0, keepdims=True)
        rest = jnp.where(subf == i1, -jnp.inf, within)
        m2 = jnp.max(rest, axis=0, keepdims=True)
        i2 = jnp.min(jnp.where(rest == m2, subf, float(EXPERTS_PER_GROUP)), axis=0, keepdims=True)
        e = jnp.exp(m2 - m1)
        w1 = p_group / (1.0 + e)
        w2 = p_group * e / (1.0 + e)
        e1 = g_idx * float(EXPERTS_PER_GROUP) + i1
        e2 = g_idx * float(EXPERTS_PER_GROUP) + i2
        ids_ref[:, rb] = jnp.where(sub == 0, e1, jnp.where(sub == 1, e2, 0.0)).astype(i32)
        wts_ref[:, rb] = jnp.where(sub == 0, w1, jnp.where(sub == 1, w2, 0.0))


def _stage_mixout(ydn, u, vln, x, ws, bsp, wout, g1, b1, wrt, brt, *, tm):
    B, T, _ = x.shape
    nt = T // tm
    act_spec = lambda: pl.BlockSpec((1, tm, DN_WIDTH), lambda b, t: (b, t, 0))
    const2 = lambda shp: pl.BlockSpec(shp, lambda b, t: (0, 0))
    tok_spec = lambda: pl.BlockSpec((8, tm), lambda b, t: (0, b * nt + t))
    return pl.pallas_call(
        functools.partial(_mixout_body, tm=tm, nsteps=B * nt),
        grid=(B, nt),
        in_specs=[act_spec(), act_spec(), act_spec(),
                  pl.BlockSpec(memory_space=pl.ANY),
                  pl.BlockSpec((SGU_GROUPS, SGU_CHUNK, SGU_CHUNK), lambda b, t: (0, 0, 0)),
                  const2((SGU_CHUNK, SGU_WIDTH)),
                  const2((D_MODEL, D_MODEL)),
                  const2((1, D_MODEL)), const2((1, D_MODEL)),
                  const2((128, D_MODEL)), const2((128, 128))],
        out_specs=[pl.BlockSpec((1, tm, D_MODEL), lambda b, t: (b, t, 0)),
                   pl.BlockSpec((tm, ROW_TILE, 128), lambda b, t: (b * nt + t, 0, 0)), tok_spec(), tok_spec()],
        out_shape=[jax.ShapeDtypeStruct((B, T, D_MODEL), f32),
                   jax.ShapeDtypeStruct((B * T, ROW_TILE, 128), bf16),
                   jax.ShapeDtypeStruct((8, B * T), i32),
                   jax.ShapeDtypeStruct((8, B * T), f32)],
        scratch_shapes=[pltpu.VMEM((tm, D_MODEL), bf16), pltpu.VMEM((X_SLOTS, tm, D_MODEL), f32),
                        pltpu.SemaphoreType.DMA((X_SLOTS,))],
        compiler_params=_cparams(("arbitrary", "arbitrary")),
        name="mixout",
    )(ydn, u, vln, x.reshape(B * nt, tm, D_MODEL), ws, bsp, wout, g1, b1, wrt, brt)


def _route_body(ids_ref, dest_ref, meta_ref, blk_ref, base_ref, pstart_ref, *, tm, nb_pad):
    phase = pl.program_id(0)
    i = pl.program_id(1)
    sub = _iota2((N_EXPERTS, tm), 0)
    is1 = sub == ids_ref[0:1, :]
    is2 = sub == ids_ref[1:2, :]
    oh = (is1.astype(f32) + is2.astype(f32)).astype(bf16)
    counts = jnp.dot(oh, jnp.ones((tm, 128), bf16), preferred_element_type=f32)

    @pl.when((phase == 0) & (i == 0))
    def _():
        base_ref[...] = jnp.zeros(base_ref.shape, f32)

    @pl.when(phase == 0)
    def _():
        base_ref[...] = base_ref[...] + counts

    @pl.when((phase == 1) & (i == 0))
    def _():
        cnt = base_ref[...]
        padded = jnp.floor((cnt + (MOE_BLOCK - 1)) * (1.0 / MOE_BLOCK)) * MOE_BLOCK
        ei = _iota2((N_EXPERTS, N_EXPERTS), 0)
        ej = _iota2((N_EXPERTS, N_EXPERTS), 1)
        pends = jnp.dot((ei >= ej).astype(f32), padded, precision=HIGHEST, preferred_element_type=f32)
        pstart = pends - padded
        pstart_ref[...] = pstart
        s64 = _iota2((N_EXPERTS, 128), 0)
        l64 = _iota2((N_EXPERTS, 128), 1)
        diag = s64 == l64
        fill_off = jnp.sum(jnp.where(diag, pstart + cnt, 0.0), axis=0, keepdims=True)
        fill_n = jnp.sum(jnp.where(diag, padded - cnt, 0.0), axis=0, keepdims=True)
        nused = pends[N_EXPERTS - 1:N_EXPERTS, :] * (1.0 / MOE_BLOCK)
        m8 = _iota2((8, 128), 0)
        meta_ref[...] = jnp.where(m8 == 0, fill_off, jnp.where(m8 == 1, fill_n, jnp.where(m8 == 2, nused, 0.0))).astype(i32)
        bstart = (_iota2((N_EXPERTS, nb_pad), 1) * MOE_BLOCK).astype(f32)
        pe = jnp.concatenate([pends] * (nb_pad // 128), axis=1)
        be = jnp.sum((pe <= bstart).astype(f32), axis=0, keepdims=True)
        be = jnp.minimum(be, float(N_EXPERTS - 1))
        blk_ref[...] = jnp.broadcast_to(be, (8, nb_pad)).astype(i32)

    @pl.when(phase == 1)
    def _():
        ti = _iota2((tm, tm), 0)
        tj = _iota2((tm, tm), 1)
        before = (ti < tj).astype(bf16)
        prefix = jnp.dot(oh, before, preferred_element_type=f32)
        nxt = prefix + jnp.concatenate([pstart_ref[...]] * (tm // 128), axis=1)
        d1 = jnp.sum(jnp.where(is1, nxt, 0.0), axis=0, keepdims=True)
        d2 = jnp.sum(jnp.where(is2, nxt, 0.0), axis=0, keepdims=True)
        sub8 = _iota2((8, tm), 0)
        dest_ref[...] = jnp.where(sub8 == 0, d1, jnp.where(sub8 == 1, d2, 0.0)).astype(i32)
        pstart_ref[...] = pstart_ref[...] + counts


def _stage_route(ids, *, tm, nb_pad):
    n = ids.shape[1]
    return pl.pallas_call(
        functools.partial(_route_body, tm=tm, nb_pad=nb_pad),
        grid=(2, n // tm),
        in_specs=[pl.BlockSpec((8, tm), lambda p, i: (0, i))],
        out_specs=[pl.BlockSpec((8, tm), lambda p, i: (0, i * p)),
                   pl.BlockSpec((8, 128), lambda p, i: (0, 0)),
                   pl.BlockSpec((8, nb_pad), lambda p, i: (0, 0))],
        out_shape=[jax.ShapeDtypeStruct((8, n), i32), jax.ShapeDtypeStruct((8, 128), i32),
                   jax.ShapeDtypeStruct((8, nb_pad), i32)],
        scratch_shapes=[pltpu.VMEM((N_EXPERTS, 128), f32), pltpu.VMEM((N_EXPERTS, 128), f32)],
        compiler_params=_cparams(("arbitrary", "arbitrary")),
        name="moe_route",
    )(ids)


def _dispatch_body(fill_off_ref, fill_n_ref, nused_ref, dest_ref, h3_ref, xs_ref, zero_ref, sem, zsem, *, tm):
    def row_copy(t, d):
        return pltpu.make_async_copy(h3_ref.at[t], xs_ref.at[d], sem)

    def issue(t, carry):
        row_copy(t, dest_ref[0, 0, t]).start(priority=0)
        row_copy(t, dest_ref[0, 1, t]).start(priority=1)
        return carry

    lax.fori_loop(0, tm, issue, 0, unroll=8)

    @pl.when(pl.program_id(0) == 0)
    def _():
        zero_ref[...] = jnp.zeros(zero_ref.shape, bf16)

        def fill(start):
            def body(e, carry):
                off = fill_off_ref[e]
                npad = fill_n_ref[e]
                bit = MOE_BLOCK // 2
                while bit:
                    @pl.when((npad & bit) != 0)
                    def _(off=off, bit=bit):
                        cp = pltpu.make_async_copy(zero_ref.at[pl.ds(0, bit)], xs_ref.at[pl.ds(off, bit)], zsem)
                        cp.start() if start else cp.wait()
                    off = off + (npad & bit)
                    bit //= 2
                return carry
            return body

        lax.fori_loop(0, N_EXPERTS, fill(True), 0)
        lax.fori_loop(0, N_EXPERTS, fill(False), 0)

        def tail_copy(b):
            return pltpu.make_async_copy(zero_ref, xs_ref.at[pl.ds(b * MOE_BLOCK, MOE_BLOCK)], zsem)

        nblocks = xs_ref.shape[0] // MOE_BLOCK
        lax.fori_loop(nused_ref[0], nblocks, lambda b, c: (tail_copy(b).start(), c)[1], 0)
        lax.fori_loop(nused_ref[0], nblocks, lambda b, c: (tail_copy(0).wait(), c)[1], 0)

    for _ in range(2):
        pltpu.make_async_copy(h3_ref, xs_ref.at[pl.ds(0, tm)], sem).wait()


def _stage_dispatch(fill_off, fill_n, nused, dest3, hrow, p_rows, *, tm):
    n = hrow.shape[0]
    return pl.pallas_call(
        functools.partial(_dispatch_body, tm=tm),
        grid_spec=pltpu.PrefetchScalarGridSpec(
            num_scalar_prefetch=3,
            grid=(n // tm,),
            in_specs=[pl.BlockSpec((1, 2, tm), lambda i, fo, fn, nu: (i, 0, 0), memory_space=pltpu.SMEM),
                      pl.BlockSpec((tm, ROW_TILE, 128), lambda i, fo, fn, nu: (i, 0, 0))],
            out_specs=pl.BlockSpec(memory_space=pl.ANY),
            scratch_shapes=[pltpu.VMEM((MOE_BLOCK, ROW_TILE, 128), bf16),
                            pltpu.SemaphoreType.DMA, pltpu.SemaphoreType.DMA],
        ),
        out_shape=jax.ShapeDtypeStruct((p_rows, ROW_TILE, 128), bf16),
        compiler_params=_cparams(("arbitrary",)),
        name="moe_dispatch",
    )(fill_off, fill_n, nused, dest3, hrow)


def _experts_body(blk_ref, nused_ref, xs_hbm, wg_hbm, wu_hbm, wd_hbm, ys_ref,
                  xbuf_ref, wg32_ref, wu32_ref, wd32_ref, wgu16_ref, wd16_ref, xsem, wsem):
    i = pl.program_id(0)
    nused = nused_ref[0]
    used = i < nused
    e = blk_ref[i]
    slot = lax.rem(i, X_SLOTS)

    def weight_copies(ex):
        return (pltpu.make_async_copy(wg_hbm.at[ex], wg32_ref, wsem.at[0]),
                pltpu.make_async_copy(wu_hbm.at[ex], wu32_ref, wsem.at[1]),
                pltpu.make_async_copy(wd_hbm.at[ex], wd32_ref, wsem.at[2]))

    def x_copy(block, s):
        return pltpu.make_async_copy(xs_hbm.at[pl.ds(block * MOE_BLOCK, MOE_BLOCK)], xbuf_ref.at[s], xsem.at[s])

    @pl.when((i == 0) & used)
    def _():
        for cp in weight_copies(e):
            cp.start()
        for j in range(X_SLOTS - 1):
            @pl.when(j < nused)
            def _(j=j):
                x_copy(j, j).start()

    ahead = i + (X_SLOTS - 1)

    @pl.when(ahead < nused)
    def _():
        x_copy(ahead, lax.rem(ahead, X_SLOTS)).start()

    @pl.when(used & ((i == 0) | (e != blk_ref[jnp.maximum(i - 1, 0)])))
    def _():
        for cp in weight_copies(e):
            cp.wait()
        wgu16_ref[:, 0:D_EXPERT] = wg32_ref[...].astype(bf16)
        wgu16_ref[:, D_EXPERT:2 * D_EXPERT] = wu32_ref[...].astype(bf16)
        wd16_ref[...] = wd32_ref[...].astype(bf16)
        nxt = lax.while_loop(lambda j: (j < nused) & (blk_ref[jnp.minimum(j, nused - 1)] == e), lambda j: j + 1, i + 1)

        @pl.when(nxt < nused)
        def _():
            for cp in weight_copies(blk_ref[jnp.minimum(nxt, nused - 1)]):
                cp.start(priority=1)

    @pl.when(used)
    def _():
        x_copy(i, slot).wait()
        half = MOE_BLOCK // 2
        rows = [slice(p * half, (p + 1) * half) for p in range(2)]
        gu = [jnp.dot(xbuf_ref[slot, r].reshape(half, D_MODEL), wgu16_ref[...], preferred_element_type=f32)
              for r in rows]
        hid = [(_silu(g[:, :D_EXPERT]) * g[:, D_EXPERT:]).astype(bf16) for g in gu]
        y = [jnp.dot(hd, wd16_ref[...], preferred_element_type=f32) for hd in hid]
        for r, yp in zip(rows, y):
            ys_ref[r] = yp.astype(bf16).reshape(half, ROW_TILE, 128)

    @pl.when(jnp.logical_not(used))
    def _():
        ys_ref[...] = jnp.zeros(ys_ref.shape, bf16)


def _stage_experts(blk_e, nused, xs, w_gate, w_up, w_down):
    p_rows = xs.shape[0]
    nb = p_rows // MOE_BLOCK

    def row_map(i, blk, nu):
        return (i, 0, 0)

    return pl.pallas_call(
        _experts_body,
        grid_spec=pltpu.PrefetchScalarGridSpec(
            num_scalar_prefetch=2,
            grid=(nb,),
            in_specs=[pl.BlockSpec(memory_space=pl.ANY),
                      pl.BlockSpec(memory_space=pl.ANY),
                      pl.BlockSpec(memory_space=pl.ANY),
                      pl.BlockSpec(memory_space=pl.ANY)],
            out_specs=pl.BlockSpec((MOE_BLOCK, ROW_TILE, 128), row_map),
            scratch_shapes=[pltpu.VMEM((X_SLOTS, MOE_BLOCK, ROW_TILE, 128), bf16),
                            pltpu.VMEM((D_MODEL, D_EXPERT), f32), pltpu.VMEM((D_MODEL, D_EXPERT), f32),
                            pltpu.VMEM((D_EXPERT, D_MODEL), f32),
                            pltpu.VMEM((D_MODEL, 2 * D_EXPERT), bf16), pltpu.VMEM((D_EXPERT, D_MODEL), bf16),
                            pltpu.SemaphoreType.DMA((X_SLOTS,)), pltpu.SemaphoreType.DMA((3,))],
        ),
        out_shape=jax.ShapeDtypeStruct((p_rows, ROW_TILE, 128), bf16),
        compiler_params=_cparams(("arbitrary",)),
        name="moe_experts",
    )(blk_e, nused, xs, w_gate, w_up, w_down)


def _combine_body(dcur_ref, dnext_ref, h_ref, wts_ref, g2_ref, b2_ref, ys_ref, o_ref, ybuf_ref, sem, *, tm, nsteps):
    i = pl.program_id(0)
    slot = lax.rem(i, 2)

    def issue_tile(d_ref, s):
        def body(t, carry):
            for k in range(2):
                pltpu.make_async_copy(ys_ref.at[d_ref[0, k, t]], ybuf_ref.at[s, k, t], sem.at[s]).start(priority=k)
            return carry

        lax.fori_loop(0, tm, body, 0, unroll=8)

    @pl.when(i == 0)
    def _():
        issue_tile(dcur_ref, 0)

    @pl.when(i + 1 < nsteps)
    def _():
        issue_tile(dnext_ref, 1 - slot)

    for k in range(2):
        pltpu.make_async_copy(ys_ref.at[pl.ds(0, tm)], ybuf_ref.at[slot, k], sem.at[slot]).wait()

    pieces = []
    for c in range(tm // 128):
        ls = slice(c * 128, (c + 1) * 128)
        w1c = jnp.broadcast_to(wts_ref[0:1, ls], (128, 128)).T
        w2c = jnp.broadcast_to(wts_ref[1:2, ls], (128, 128)).T
        w1f = jnp.concatenate([w1c] * (D_MODEL // 128), axis=1)
        w2f = jnp.concatenate([w2c] * (D_MODEL // 128), axis=1)
        y1 = ybuf_ref[slot, 0, ls].reshape(128, D_MODEL).astype(f32)
        y2 = ybuf_ref[slot, 1, ls].reshape(128, D_MODEL).astype(f32)
        pieces.append(w1f * y1 + w2f * y2)
    ffn = jnp.concatenate(pieces, axis=0)
    hp = DEEPNORM_ALPHA * h_ref[...] + ffn
    mu = jnp.mean(hp, axis=-1, keepdims=True)
    hc = hp - mu
    var = jnp.mean(hc * hc, axis=-1, keepdims=True)
    o_ref[...] = hc * lax.rsqrt(var + LN_EPS) * g2_ref[...] + b2_ref[...]


def _stage_combine(dest3, h2, wts, g2, b2, ys, *, tm):
    n = h2.shape[0]
    nsteps = n // tm
    return pl.pallas_call(
        functools.partial(_combine_body, tm=tm, nsteps=nsteps),
        grid=(nsteps,),
        in_specs=[pl.BlockSpec((1, 2, tm), lambda i: (i, 0, 0), memory_space=pltpu.SMEM),
                  pl.BlockSpec((1, 2, tm), lambda i: (jnp.minimum(i + 1, nsteps - 1), 0, 0), memory_space=pltpu.SMEM),
                  pl.BlockSpec((tm, D_MODEL), lambda i: (i, 0)),
                  pl.BlockSpec((8, tm), lambda i: (0, i)),
                  pl.BlockSpec((1, D_MODEL), lambda i: (0, 0)),
                  pl.BlockSpec((1, D_MODEL), lambda i: (0, 0)),
                  pl.BlockSpec(memory_space=pl.ANY)],
        out_specs=pl.BlockSpec((tm, D_MODEL), lambda i: (i, 0)),
        out_shape=jax.ShapeDtypeStruct((n, D_MODEL), f32),
        scratch_shapes=[pltpu.VMEM((2, 2, tm, ROW_TILE, 128), bf16), pltpu.SemaphoreType.DMA((2,))],
        compiler_params=_cparams(("arbitrary",)),
        name="moe_combine",
    )(dest3, dest3, h2, wts, g2, b2, ys)


def _layer(h, w_in, conv_w, a_log, dt_bias, dn_norm_w, sgu_ln_g, sgu_ln_b, w_spatial, b_spatial, w_out,
           ln1_g, ln1_b, w_rg, b_rg, w_re, b_re, w_gate, w_up, w_down, ln2_g, ln2_b,
           *, tm_in, in_groups, dn_rows, dn_chunks, dn_groups, tm_mix, tm_rank, tm_disp, tm_comb):
    B, T, _ = h.shape
    n = B * T
    qkvz = 4 * DN_WIDTH
    ba0 = qkvz
    uv0 = qkvz + 2 * DN_HEADS
    w_cols = jnp.zeros((D_MODEL, IN_COLS_ALIGNED), bf16)
    w_cols = lax.dynamic_update_slice(w_cols, w_in[:, :qkvz].astype(bf16), (0, 0))
    w_cols = lax.dynamic_update_slice(w_cols, w_in[:, uv0:].astype(bf16), (0, qkvz))
    w_cols = lax.dynamic_update_slice(w_cols, w_in[:, ba0:uv0].astype(bf16), (0, qkvz + 2 * SGU_WIDTH))
    decay_prm = jnp.stack([a_log, dt_bias])
    prow =jnp.broadcast_to(jnp.pad(decay_prm, ((0, 0), (DN_HEADS, 8 - 2 * DN_HEADS)))[:, :, None], (2, 8, 128))

    q, k, v, z, u, vln, gcol, grow = _stage_inproj(
        h, w_cols, conv_w, prow, sgu_ln_g[None, :], sgu_ln_b[None, :], tm=tm_in, ngroups=in_groups)
    ydn = _stage_deltanet(q, k, v, z, gcol, grow, dn_norm_w[None, :], nbr=dn_rows, nch=dn_chunks, ngroups=dn_groups)

    bsp = jnp.broadcast_to(b_spatial.T[:, :, None], (SGU_CHUNK, SGU_GROUPS, SGU_CHUNK)).reshape(SGU_CHUNK, SGU_WIDTH)
    n_logit = MOE_GROUPS + N_EXPERTS
    wrt = jnp.pad(jnp.concatenate([w_rg, w_re], axis=1).T, ((0, 128 - n_logit), (0, 0))).astype(bf16)
    brt = jnp.broadcast_to(jnp.pad(jnp.concatenate([b_rg, b_re]), (0, 128 - n_logit))[:, None], (128, 128))
    h1, hrow, ids, wts = _stage_mixout(ydn, u, vln, h, w_spatial, bsp, w_out.astype(bf16), ln1_g[None, :],
                                       ln1_b[None, :], wrt, brt, tm=tm_mix)

    p_rows = (-(-(n * 2) // MOE_BLOCK)) * MOE_BLOCK + N_EXPERTS * MOE_BLOCK
    nb = p_rows // MOE_BLOCK
    nb_pad = (-(-nb // 128)) * 128
    dest, meta, blk = _stage_route(ids, tm=tm_rank, nb_pad=nb_pad)

    h2 = h1.reshape(n, D_MODEL)
    dest_d = dest[0:2].reshape(2, n // tm_disp, tm_disp).transpose(1, 0, 2)
    xs = _stage_dispatch(meta[0, :N_EXPERTS], meta[1, :N_EXPERTS], meta[2, 0:1], dest_d, hrow, p_rows, tm=tm_disp)
    ys = _stage_experts(blk[0, :nb], meta[2, 0:1], xs, w_gate, w_up, w_down)
    dest_c = dest[0:2].reshape(2, n // tm_comb, tm_comb).transpose(1, 0, 2)
    out = _stage_combine(dest_c, h2, wts, ln2_g[None, :], ln2_b[None, :], ys, tm=tm_comb)
    return out.reshape(B, T, D_MODEL)


def kernel(x, w_in, conv_w, a_log, dt_bias, dn_norm_w, sgu_ln_g, sgu_ln_b, w_spatial, b_spatial, w_out, ln1_g, ln1_b, w_router_group, b_router_group, w_router_expert, b_router_expert, w_gate, w_up, w_down, ln2_g, ln2_b):
    h = x
    for l in range(w_in.shape[0]):
        h = _layer(h, w_in[l], conv_w[l], a_log[l], dt_bias[l], dn_norm_w[l], sgu_ln_g[l], sgu_ln_b[l],
                   w_spatial[l], b_spatial[l], w_out[l], ln1_g[l], ln1_b[l],
                   w_router_group[l], b_router_group[l], w_router_expert[l], b_router_expert[l],
                   w_gate[l], w_up[l], w_down[l], ln2_g[l], ln2_b[l],
                   tm_in=512, in_groups=1, dn_rows=4, dn_chunks=2, dn_groups=2, tm_mix=512, tm_rank=1024, tm_disp=2048, tm_comb=512)
    return h
```

```python
import functools

import jax
import jax.numpy as jnp
from jax import lax
from jax.experimental import pallas as pl
from jax.experimental.pallas import tpu as pltpu

f32 = jnp.float32
bf16 = jnp.bfloat16
i32 = jnp.int32

D_MODEL = 1024
DN_WIDTH = 512
DN_HEADS = 4
HEAD_DIM = 128
CONV_K = 4
SGU_WIDTH = 512
SGU_GROUPS = 4
SGU_CHUNK = 128
DN_CHUNK = 128
MOE_GROUPS = 8
EXPERTS_PER_GROUP = 8
N_EXPERTS = 64
D_EXPERT = 512
MOE_BLOCK = 256
IN_COLS_ALIGNED = 4 * DN_WIDTH + 2 * SGU_WIDTH + 128
X_SLOTS = 3
COMBINE_GROUP = 16
ROW_TILE = D_MODEL // 128
DEEPNORM_ALPHA = 2.0 ** 0.25
LN_EPS = 1e-5
RMS_EPS = 1e-6
HIGHEST = lax.Precision.HIGHEST
VMEM_LIMIT_BYTES = 56 * 1024 * 1024

NT_DIMS = (((1,), (1,)), ((), ()))


def _cparams(sem, flags=None):
    return pltpu.CompilerParams(dimension_semantics=sem, vmem_limit_bytes=VMEM_LIMIT_BYTES, flags=flags)


def _sigmoid(x):
    return 1.0 / (1.0 + jnp.exp(-x))


def _silu(x):
    h = 0.5 * x
    return h + h * jnp.tanh(h)


def _softplus(x):
    return jnp.maximum(x, 0.0) + jnp.log1p(jnp.exp(-jnp.abs(x)))


def _gelu_tanh(x):
    c = 0.7978845608028654
    return x * (0.5 * (1.0 + jnp.tanh(c * (x + 0.044715 * (x * x * x)))))


def _iota2(shape, axis):
    return lax.broadcasted_iota(i32, shape, axis)


def _inproj_body(x_ref, w_ref, convw_ref, prow_ref, lng_ref, lnb_ref, ones_ref,
                 q_ref, k_ref, v_ref, z_ref, u_ref, vln_ref, gcol_ref, grow_ref, *ext_refs, tm, ngroups):
    W = DN_WIDTH
    gm = tm // ngroups
    ext = [ext_refs[3 * g:3 * g + 3] for g in range(ngroups)]

    @pl.when(pl.program_id(1) == 0)
    def _():
        for e_ref in ext[0]:
            e_ref[0:8, :] = jnp.zeros((8, W), f32)

    for g in range(ngroups):
        _inproj_rows(x_ref, w_ref, convw_ref, prow_ref, lng_ref, lnb_ref, ones_ref,
                     q_ref, k_ref, v_ref, z_ref, u_ref, vln_ref, gcol_ref, grow_ref, ext[g],
                     ext[(g + 1) % ngroups], r0=g * gm, gm=gm)


def _inproj_rows(x_ref, w_ref, convw_ref, prow_ref, lng_ref, lnb_ref, ones_ref,
                 q_ref, k_ref, v_ref, z_ref, u_ref, vln_ref, gcol_ref, grow_ref, ext, ext_next, *, r0, gm):
    W = DN_WIDTH
    rows = slice(r0, r0 + gm)
    xb = x_ref[0, rows, :].astype(bf16)
    for part, e_ref in enumerate(ext):
        e_ref[8:8 + gm, :] = jnp.dot(xb, w_ref[:, part * W:(part + 1) * W], preferred_element_type=f32)
    zc = 3 * W
    uc = zc + W
    vc = uc + SGU_WIDTH
    bc = vc + SGU_WIDTH
    pz = jnp.dot(xb, w_ref[:, zc:zc + W], preferred_element_type=f32)
    pu = jnp.dot(xb, w_ref[:, uc:uc + SGU_WIDTH], preferred_element_type=f32)
    pv = jnp.dot(xb, w_ref[:, vc:vc + SGU_WIDTH], preferred_element_type=f32)
    pba = jnp.dot(xb, w_ref[:, bc:bc + 128], preferred_element_type=f32)

    def group_sums(a):
        return jnp.dot(a.astype(bf16), ones_ref[...], preferred_element_type=f32)

    for part, (e_ref, n_ref, out_ref) in enumerate(zip(ext, ext_next, (q_ref, k_ref, v_ref))):
        cs = slice(part * W, (part + 1) * W)
        y = convw_ref[3:4, cs] * e_ref[8:8 + gm, :]
        for j in range(CONV_K - 1):
            y = y + convw_ref[j:j + 1, cs] * e_ref[5 + j:5 + j + gm, :]
        y = _silu(y)
        if part < 2:
            scale = HEAD_DIM ** -0.5 if part == 0 else 1.0
            y = y * (lax.rsqrt(group_sums(y * y) + RMS_EPS) * scale)
        out_ref[0, rows, :] = y.astype(bf16)
        n_ref[0:8, :] = e_ref[gm:gm + 8, :]

    z_ref[0, rows, :] = _silu(pz).astype(bf16)

    u_ref[0, rows, :] = _gelu_tanh(pu).astype(bf16)
    pv = _gelu_tanh(pv)
    for g in range(SGU_GROUPS):
        sl = slice(g * SGU_CHUNK, (g + 1) * SGU_CHUNK)
        vg = pv[:, sl]
        mu = jnp.mean(vg, axis=-1, keepdims=True)
        vcn = vg - mu
        var = jnp.mean(vcn * vcn, axis=-1, keepdims=True)
        vln_ref[0, rows, sl] = (vcn * lax.rsqrt(var + LN_EPS) * lng_ref[:, sl] + lnb_ref[:, sl]).astype(bf16)

    lane = _iota2((DN_CHUNK, 128), 1)
    beta = _sigmoid(pba)
    lane8 = _iota2((8, DN_CHUNK), 1)
    sub8 = _iota2((8, DN_CHUNK), 0)
    for c in range(gm // DN_CHUNK):
        rs = slice(c * DN_CHUNK, (c + 1) * DN_CHUNK)
        os_ = slice(r0 + c * DN_CHUNK, r0 + (c + 1) * DN_CHUNK)
        pbat = pba[rs].T[0:8, :]
        gt = -jnp.exp(prow_ref[0]) * _softplus(pbat + prow_ref[1])
        gc = jnp.where(sub8 >= DN_HEADS, gt, 0.0)
        shift = 1
        while shift < DN_CHUNK:
            gc = gc + jnp.where(lane8 >= shift, pltpu.roll(gc, shift, axis=1), 0.0)
            shift *= 2
        grow_ref[0, :, os_] = gc
        gc_col = jnp.concatenate([gc, jnp.zeros((DN_CHUNK - 8, DN_CHUNK), f32)], axis=0).T
        gcol_ref[0, os_, :] = jnp.where(lane < DN_HEADS, beta[rs], gc_col)


def _stage_inproj(x, w_re, conv_w, prow, lng, lnb, *, tm, ngroups):
    B, T, _ = x.shape
    wcols = w_re.shape[1]
    grid = (B, T // tm)
    gi = lax.broadcasted_iota(i32, (DN_WIDTH, DN_WIDTH), 0) // 128
    gj = lax.broadcasted_iota(i32, (DN_WIDTH, DN_WIDTH), 1) // 128
    group_ones = (gi == gj).astype(bf16)
    act = lambda: jax.ShapeDtypeStruct((B, T, DN_WIDTH), bf16)
    act_spec = lambda: pl.BlockSpec((1, tm, DN_WIDTH), lambda b, t: (b, t, 0))
    const2 = lambda shp: pl.BlockSpec(shp, lambda b, t: (0, 0))
    return pl.pallas_call(
        functools.partial(_inproj_body, tm=tm, ngroups=ngroups),
        grid=grid,
        in_specs=[
            pl.BlockSpec((1, tm, D_MODEL), lambda b, t: (b, t, 0)),
            const2((D_MODEL, wcols)),
            const2((CONV_K, 3 * DN_WIDTH)),
            pl.BlockSpec((2, 8, 128), lambda b, t: (0, 0, 0)),
            const2((1, SGU_WIDTH)),
            const2((1, SGU_WIDTH)),
            const2((DN_WIDTH, DN_WIDTH)),
        ],
        out_specs=[act_spec() for _ in range(6)] + [
            pl.BlockSpec((1, tm, 128), lambda b, t: (b, t, 0)),
            pl.BlockSpec((1, 8, tm), lambda b, t: (b, 0, t)),
        ],
        out_shape=[act() for _ in range(6)] + [
            jax.ShapeDtypeStruct((B, T, 128), f32),
            jax.ShapeDtypeStruct((B, 8, T), f32),
        ],
        scratch_shapes=[pltpu.VMEM((tm // ngroups + 8, DN_WIDTH), f32) for _ in range(3 * ngroups)],
        compiler_params=_cparams(("arbitrary", "arbitrary")),
        name="inproj",
    )(x, w_re, conv_w, prow, lng, lnb, group_ones)


def _mm(a, b):
    return jnp.dot(a.astype(bf16), b.astype(bf16), preferred_element_type=f32)


def _unit_lower_inverse(nmats, ii, jj):
    n = nmats[0].shape[0]
    eye = (ii == jj).astype(f32)
    leaf = jnp.right_shift(ii, 3) == jnp.right_shift(jj, 3)
    dblk = [jnp.where(leaf, m, 0.0) for m in nmats]
    s1 = [_mm(d, d) for d in dblk]
    r1 = [eye - d for d in dblk]
    both = [_mm(s, jnp.concatenate([s, r], axis=1)) for s, r in zip(s1, r1)]
    r2 = [r + bo[:, n:] for r, bo in zip(r1, both)]
    xs = [r + _mm(bo[:, :n], r) for r, bo in zip(r2, both)]
    size = 8
    while size < n:
        lows = [slice(r + size, r + 2 * size) for r in range(0, n, 2 * size)]
        ups = [slice(r, r + size) for r in range(0, n, 2 * size)]
        rsel = _iota2((n // 2, n), 0)
        ilow = rsel + size * (jnp.right_shift(rsel, size.bit_length() - 1) + 1)
        jlow = _iota2((n // 2, n), 1)
        in_pair_upper = (jlow >= ilow - (ilow & (2 * size - 1))) & (jlow < ilow - (ilow & (size - 1)))
        zeros = jnp.zeros((size, n), f32)
        new_xs = []
        ylows = [_mm(jnp.where(in_pair_upper, jnp.concatenate([m[s] for s in lows], axis=0), 0.0), x)
                 for m, x in zip(nmats, xs)]
        yfull = [jnp.concatenate([piece for k in range(len(lows)) for piece in (zeros, y[k * size:(k + 1) * size])], axis=0)
                 for y in ylows]
        corr = [_mm(jnp.concatenate([x[s] for s in lows], axis=0), yf) for x, yf in zip(xs, yfull)]
        for x, c in zip(xs, corr):
            pieces = []
            for k, (u, l) in enumerate(zip(ups, lows)):
                pieces += [x[u], x[l] - c[k * size:(k + 1) * size]]
            new_xs.append(jnp.concatenate(pieces, axis=0))
        xs = new_xs
        size *= 2
    return xs


def _deltanet_body(q_ref, k_ref, v_ref, z_ref, gcol_ref, grow_ref, nw_ref, y_ref, s_ref, *, nbr, nch, ngroups):
    @pl.when(pl.program_id(1) == 0)
    def _():
        s_ref[...] = jnp.zeros(s_ref.shape, f32)

    per = nbr // ngroups
    for g in range(ngroups):
        _deltanet_rows(q_ref, k_ref, v_ref, z_ref, gcol_ref, grow_ref, nw_ref, y_ref, s_ref,
                       rows=range(g * per, (g + 1) * per), nch=nch)


def _deltanet_rows(q_ref, k_ref, v_ref, z_ref, gcol_ref, grow_ref, nw_ref, y_ref, s_ref, *, rows, nch):
    C = DN_CHUNK
    S = [(b, h) for b in rows for h in range(DN_HEADS)]
    P = [(b, c, h) for c in range(nch) for b, h in S]

    ii = _iota2((C, C), 0)
    jj = _iota2((C, C), 1)
    causal = ii >= jj
    rs = [slice(c * C, (c + 1) * C) for c in range(nch)]
    hs = [slice(h * HEAD_DIM, (h + 1) * HEAD_DIM) for h in range(DN_HEADS)]
    gcol = {(b, c): gcol_ref[b, rs[c], :] for b in rows for c in range(nch)}
    qh = {(b, c, h): q_ref[b, rs[c], hs[h]] for b, c, h in P}
    kh = {(b, c, h): k_ref[b, rs[c], hs[h]] for b, c, h in P}
    vh = {(b, c, h): v_ref[b, rs[c], hs[h]] for b, c, h in P}
    gc_b = {(b, c, h): jnp.broadcast_to(gcol[b, c][:, DN_HEADS + h:DN_HEADS + h + 1], (C, HEAD_DIM)) for b, c, h in P}
    beta_b = {(b, c, h): jnp.broadcast_to(gcol[b, c][:, h:h + 1], (C, HEAD_DIM)) for b, c, h in P}
    gc_r = {(b, c, h): jnp.broadcast_to(grow_ref[b, DN_HEADS + h:DN_HEADS + h + 1, rs[c]], (C, C)) for b, c, h in P}
    decay = {p: jnp.exp(jnp.where(causal, gc_b[p] - gc_r[p], -1e30)) for p in P}

    kf = {p: kh[p].astype(f32) for p in P}
    kb = {p: kf[p] * beta_b[p] for p in P}
    kk = {p: lax.dot_general(kb[p].astype(bf16), kh[p], NT_DIMS, preferred_element_type=f32) for p in P}
    a_intra = {p: lax.dot_general(qh[p], kh[p], NT_DIMS, preferred_element_type=f32) * decay[p] for p in P}
    nmat = [jnp.where(ii > jj, kk[p] * decay[p], 0.0) for p in P]
    tinv = dict(zip(P, _unit_lower_inverse(nmat, ii, jj)))

    eg = {p: jnp.exp(gc_b[p]) for p in P}
    rhs = {p: jnp.concatenate([vh[p].astype(f32) * beta_b[p], kb[p] * eg[p]], axis=1) for p in P}
    sol = {p: _mm(tinv[p], rhs[p]) for p in P}

    q_dec = {p: qh[p].astype(f32) * eg[p] for p in P}
    g_last = {p: gc_b[p][C - 1:C, :] for p in P}
    kdt = {p: (kf[p] * jnp.exp(g_last[p] - gc_b[p])).T for p in P}

    state = {(b, h): s_ref[b * DN_HEADS + h] for b, h in S}
    for c in range(nch):
        m1 = {(b, h): _mm(jnp.concatenate([sol[b, c, h][:, HEAD_DIM:], q_dec[b, c, h]], axis=0), state[b, h]) for b, h in S}
        v_new = {(b, h): sol[b, c, h][:, :HEAD_DIM] - m1[b, h][:C] for b, h in S}
        m2 = {(b, h): _mm(jnp.concatenate([a_intra[b, c, h], kdt[b, c, h]], axis=0), v_new[b, h]) for b, h in S}
        state = {(b, h): state[b, h] * jnp.exp(g_last[b, c, h]) + m2[b, h][C:] for b, h in S}
        for b, h in S:
            o = m1[b, h][C:] + m2[b, h][:C]
            rms = lax.rsqrt(jnp.mean(o * o, axis=-1, keepdims=True) + RMS_EPS)
            y_ref[b, rs[c], hs[h]] = (o * rms * nw_ref[...] * z_ref[b, rs[c], hs[h]].astype(f32)).astype(bf16)
    for b, h in S:
        s_ref[b * DN_HEADS + h] = state[b, h]


def _stage_deltanet(q, k, v, z, gcol, grow, norm_w, *, nbr, nch, ngroups):
    B, T, _ = q.shape
    tt = nch * DN_CHUNK
    act_spec = lambda: pl.BlockSpec((nbr, tt, DN_WIDTH), lambda b, t: (b, t, 0))
    return pl.pallas_call(
        functools.partial(_deltanet_body, nbr=nbr, nch=nch, ngroups=ngroups),
        grid=(B // nbr, T // tt),
        in_specs=[act_spec(), act_spec(), act_spec(), act_spec(),
                  pl.BlockSpec((nbr, tt, 128), lambda b, t: (b, t, 0)),
                  pl.BlockSpec((nbr, 8, tt), lambda b, t: (b, 0, t)),
                  pl.BlockSpec((1, HEAD_DIM), lambda b, t: (0, 0))],
        out_specs=act_spec(),
        out_shape=jax.ShapeDtypeStruct((B, T, DN_WIDTH), bf16),
        scratch_shapes=[pltpu.VMEM((nbr * DN_HEADS, HEAD_DIM, HEAD_DIM), f32)],
        compiler_params=_cparams(("arbitrary", "arbitrary")),
        name="deltanet",
    )(q, k, v, z, gcol, grow, norm_w)


def _mixout_body(ydn_ref, u_ref, vln_ref, x_hbm, ws_ref, bsp_ref, wout_ref, g1_ref, b1_ref, wrt_ref, brt_ref,
                 h_ref, hrow_ref, ids_ref, wts_ref, ycat_ref, xring_ref, xsem, *, tm, nsteps):
    C = SGU_CHUNK
    step = pl.program_id(0) * pl.num_programs(1) + pl.program_id(1)
    slot = lax.rem(step, X_SLOTS)

    def x_copy(s, sl):
        return pltpu.make_async_copy(x_hbm.at[s], xring_ref.at[sl], xsem.at[sl])

    @pl.when(step == 0)
    def _():
        for j in range(X_SLOTS - 1):
            x_copy(j, j).start()

    ahead = step + (X_SLOTS - 1)

    @pl.when(ahead < nsteps)
    def _():
        x_copy(ahead, lax.rem(ahead, X_SLOTS)).start()

    x_copy(step, slot).wait()
    ii = _iota2((C, C), 0)
    jj = _iota2((C, C), 1)
    ycat_ref[:, 0:DN_WIDTH] = ydn_ref[0]
    for g in range(SGU_GROUPS):
        gs = slice(g * C, (g + 1) * C)
        wsg = jnp.where(ii >= jj, ws_ref[g], 0.0).astype(bf16)
        for c in range(tm // C):
            rs = slice(c * C, (c + 1) * C)
            mixed = jnp.dot(wsg, vln_ref[0, rs, gs], preferred_element_type=f32) + bsp_ref[:, gs]
            ycat_ref[rs, DN_WIDTH + g * C:DN_WIDTH + (g + 1) * C] = (u_ref[0, rs, gs].astype(f32) * mixed).astype(bf16)

    RB = 128
    blocks = [slice(r, r + RB) for r in range(0, tm, RB)]
    mix = [jnp.dot(ycat_ref[rb, :], wout_ref[...], preferred_element_type=f32) for rb in blocks]
    h1s = []
    for rb, m in zip(blocks, mix):
        hp = DEEPNORM_ALPHA * xring_ref[slot, rb, :] + m
        mu = jnp.mean(hp, axis=-1, keepdims=True)
        hc = hp - mu
        var = jnp.mean(hc * hc, axis=-1, keepdims=True)
        h1 = hc * lax.rsqrt(var + LN_EPS) * g1_ref[...] + b1_ref[...]
        h_ref[0, rb, :] = h1
        h1b = h1.astype(bf16)
        hrow_ref[rb] = h1b.reshape(RB, ROW_TILE, 128)
        h1s.append(h1b)

    logit_blocks = [lax.dot_general(wrt_ref[...], hb, NT_DIMS, preferred_element_type=f32) + brt_ref[...] for hb in h1s]
    sub = _iota2((8, RB), 0)
    subf = sub.astype(f32)
    for rb, logits in zip(blocks, logit_blocks):
        gl = logits[0:8]
        gmax = jnp.max(gl, axis=0, keepdims=True)
        g_idx = jnp.min(jnp.where(gl == gmax, subf, float(MOE_GROUPS)), axis=0, keepdims=True)
        p_group = 1.0 / jnp.sum(jnp.exp(gl - gmax), axis=0, keepdims=True)
        within = jnp.zeros((8, RB), f32)
        for g in range(MOE_GROUPS):
            within = within + jnp.where(g_idx == float(g), logits[8 + 8 * g:16 + 8 * g], 0.0)
        m1 = jnp.max(within, axis=0, keepdims=True)
        i1 = jnp.min(jnp.where(within == m1, subf, float(EXPERTS_PER_GROUP)), axis=0, keepdims=True)
        rest = jnp.where(subf == i1, -jnp.inf, within)
        m2 = jnp.max(rest, axis=0, keepdims=True)
        i2 = jnp.min(jnp.where(rest == m2, subf, float(EXPERTS_PER_GROUP)), axis=0, keepdims=True)
        e = jnp.exp(m2 - m1)
        w1 = p_group / (1.0 + e)
        w2 = p_group * e / (1.0 + e)
        e1 = g_idx * float(EXPERTS_PER_GROUP) + i1
        e2 = g_idx * float(EXPERTS_PER_GROUP) + i2
        ids_ref[:, rb] = jnp.where(sub == 0, e1, jnp.where(sub == 1, e2, 0.0)).astype(i32)
        wts_ref[:, rb] = jnp.where(sub == 0, w1, jnp.where(sub == 1, w2, 0.0))


def _stage_mixout(ydn, u, vln, x, ws, bsp, wout, g1, b1, wrt, brt, *, tm):
    B, T, _ = x.shape
    nt = T // tm
    act_spec = lambda: pl.BlockSpec((1, tm, DN_WIDTH), lambda b, t: (b, t, 0))
    const2 = lambda shp: pl.BlockSpec(shp, lambda b, t: (0, 0))
    tok_spec = lambda: pl.BlockSpec((8, tm), lambda b, t: (0, b * nt + t))
    return pl.pallas_call(
        functools.partial(_mixout_body, tm=tm, nsteps=B * nt),
        grid=(B, nt),
        in_specs=[act_spec(), act_spec(), act_spec(),
                  pl.BlockSpec(memory_space=pl.ANY),
                  pl.BlockSpec((SGU_GROUPS, SGU_CHUNK, SGU_CHUNK), lambda b, t: (0, 0, 0)),
                  const2((SGU_CHUNK, SGU_WIDTH)),
                  const2((D_MODEL, D_MODEL)),
                  const2((1, D_MODEL)), const2((1, D_MODEL)),
                  const2((128, D_MODEL)), const2((128, 128))],
        out_specs=[pl.BlockSpec((1, tm, D_MODEL), lambda b, t: (b, t, 0)),
                   pl.BlockSpec((tm, ROW_TILE, 128), lambda b, t: (b * nt + t, 0, 0)), tok_spec(), tok_spec()],
        out_shape=[jax.ShapeDtypeStruct((B, T, D_MODEL), f32),
                   jax.ShapeDtypeStruct((B * T, ROW_TILE, 128), bf16),
                   jax.ShapeDtypeStruct((8, B * T), i32),
                   jax.ShapeDtypeStruct((8, B * T), f32)],
        scratch_shapes=[pltpu.VMEM((tm, D_MODEL), bf16), pltpu.VMEM((X_SLOTS, tm, D_MODEL), f32),
                        pltpu.SemaphoreType.DMA((X_SLOTS,))],
        compiler_params=_cparams(("arbitrary", "arbitrary")),
        name="mixout",
    )(ydn, u, vln, x.reshape(B * nt, tm, D_MODEL), ws, bsp, wout, g1, b1, wrt, brt)


def _route_body(ids_ref, dest_ref, meta_ref, blk_ref, base_ref, pstart_ref, *, tm, nb_pad):
    phase = pl.program_id(0)
    i = pl.program_id(1)
    sub = _iota2((N_EXPERTS, tm), 0)
    is1 = sub == ids_ref[0:1, :]
    is2 = sub == ids_ref[1:2, :]
    oh = (is1.astype(f32) + is2.astype(f32)).astype(bf16)
    counts = jnp.dot(oh, jnp.ones((tm, 128), bf16), preferred_element_type=f32)

    @pl.when((phase == 0) & (i == 0))
    def _():
        base_ref[...] = jnp.zeros(base_ref.shape, f32)

    @pl.when(phase == 0)
    def _():
        base_ref[...] = base_ref[...] + counts

    @pl.when((phase == 1) & (i == 0))
    def _():
        cnt = base_ref[...]
        padded = jnp.floor((cnt + (MOE_BLOCK - 1)) * (1.0 / MOE_BLOCK)) * MOE_BLOCK
        ei = _iota2((N_EXPERTS, N_EXPERTS), 0)
        ej = _iota2((N_EXPERTS, N_EXPERTS), 1)
        pends = jnp.dot((ei >= ej).astype(f32), padded, precision=HIGHEST, preferred_element_type=f32)
        pstart = pends - padded
        pstart_ref[...] = pstart
        s64 = _iota2((N_EXPERTS, 128), 0)
        l64 = _iota2((N_EXPERTS, 128), 1)
        diag = s64 == l64
        fill_off = jnp.sum(jnp.where(diag, pstart + cnt, 0.0), axis=0, keepdims=True)
        fill_n = jnp.sum(jnp.where(diag, padded - cnt, 0.0), axis=0, keepdims=True)
        nused = pends[N_EXPERTS - 1:N_EXPERTS, :] * (1.0 / MOE_BLOCK)
        m8 = _iota2((8, 128), 0)
        meta_ref[...] = jnp.where(m8 == 0, fill_off, jnp.where(m8 == 1, fill_n, jnp.where(m8 == 2, nused, 0.0))).astype(i32)
        bstart = (_iota2((N_EXPERTS, nb_pad), 1) * MOE_BLOCK).astype(f32)
        pe = jnp.concatenate([pends] * (nb_pad // 128), axis=1)
        be = jnp.sum((pe <= bstart).astype(f32), axis=0, keepdims=True)
        be = jnp.minimum(be, float(N_EXPERTS - 1))
        blk_ref[...] = jnp.broadcast_to(be, (8, nb_pad)).astype(i32)

    @pl.when(phase == 1)
    def _():
        ti = _iota2((tm, tm), 0)
        tj = _iota2((tm, tm), 1)
        before = (ti < tj).astype(bf16)
        prefix = jnp.dot(oh, before, preferred_element_type=f32)
        nxt = prefix + jnp.concatenate([pstart_ref[...]] * (tm // 128), axis=1)
        d1 = jnp.sum(jnp.where(is1, nxt, 0.0), axis=0, keepdims=True)
        d2 = jnp.sum(jnp.where(is2, nxt, 0.0), axis=0, keepdims=True)
        sub8 = _iota2((8, tm), 0)
        dest_ref[...] = jnp.where(sub8 == 0, d1, jnp.where(sub8 == 1, d2, 0.0)).astype(i32)
        pstart_ref[...] = pstart_ref[...] + counts


def _stage_route(ids, *, tm, nb_pad):
    n = ids.shape[1]
    return pl.pallas_call(
        functools.partial(_route_body, tm=tm, nb_pad=nb_pad),
        grid=(2, n // tm),
        in_specs=[pl.BlockSpec((8, tm), lambda p, i: (0, i))],
        out_specs=[pl.BlockSpec((8, tm), lambda p, i: (0, i * p)),
                   pl.BlockSpec((8, 128), lambda p, i: (0, 0)),
                   pl.BlockSpec((8, nb_pad), lambda p, i: (0, 0))],
        out_shape=[jax.ShapeDtypeStruct((8, n), i32), jax.ShapeDtypeStruct((8, 128), i32),
                   jax.ShapeDtypeStruct((8, nb_pad), i32)],
        scratch_shapes=[pltpu.VMEM((N_EXPERTS, 128), f32), pltpu.VMEM((N_EXPERTS, 128), f32)],
        compiler_params=_cparams(("arbitrary", "arbitrary")),
        name="moe_route",
    )(ids)


def _dispatch_body(fill_off_ref, fill_n_ref, nused_ref, dest_ref, h3_ref, xs_ref, zero_ref, sem, zsem, *, tm):
    def row_copy(t, d):
        return pltpu.make_async_copy(h3_ref.at[t], xs_ref.at[d], sem)

    def issue(t, carry):
        row_copy(t, dest_ref[0, 0, t]).start(priority=0)
        row_copy(t, dest_ref[0, 1, t]).start(priority=1)
        return carry

    lax.fori_loop(0, tm, issue, 0, unroll=8)

    @pl.when(pl.program_id(0) == 0)
    def _():
        zero_ref[...] = jnp.zeros(zero_ref.shape, bf16)

        def fill(start):
            def body(e, carry):
                off = fill_off_ref[e]
                npad = fill_n_ref[e]
                bit = MOE_BLOCK // 2
                while bit:
                    @pl.when((npad & bit) != 0)
                    def _(off=off, bit=bit):
                        cp = pltpu.make_async_copy(zero_ref.at[pl.ds(0, bit)], xs_ref.at[pl.ds(off, bit)], zsem)
                        cp.start() if start else cp.wait()
                    off = off + (npad & bit)
                    bit //= 2
                return carry
            return body

        lax.fori_loop(0, N_EXPERTS, fill(True), 0)
        lax.fori_loop(0, N_EXPERTS, fill(False), 0)

        def tail_copy(b):
            return pltpu.make_async_copy(zero_ref, xs_ref.at[pl.ds(b * MOE_BLOCK, MOE_BLOCK)], zsem)

        nblocks = xs_ref.shape[0] // MOE_BLOCK
        lax.fori_loop(nused_ref[0], nblocks, lambda b, c: (tail_copy(b).start(), c)[1], 0)
        lax.fori_loop(nused_ref[0], nblocks, lambda b, c: (tail_copy(0).wait(), c)[1], 0)

    for _ in range(2):
        pltpu.make_async_copy(h3_ref, xs_ref.at[pl.ds(0, tm)], sem).wait()


def _stage_dispatch(fill_off, fill_n, nused, dest3, hrow, p_rows, *, tm):
    n = hrow.shape[0]
    return pl.pallas_call(
        functools.partial(_dispatch_body, tm=tm),
        grid_spec=pltpu.PrefetchScalarGridSpec(
            num_scalar_prefetch=3,
            grid=(n // tm,),
            in_specs=[pl.BlockSpec((1, 2, tm), lambda i, fo, fn, nu: (i, 0, 0), memory_space=pltpu.SMEM),
                      pl.BlockSpec((tm, ROW_TILE, 128), lambda i, fo, fn, nu: (i, 0, 0))],
            out_specs=pl.BlockSpec(memory_space=pl.ANY),
            scratch_shapes=[pltpu.VMEM((MOE_BLOCK, ROW_TILE, 128), bf16),
                            pltpu.SemaphoreType.DMA, pltpu.SemaphoreType.DMA],
        ),
        out_shape=jax.ShapeDtypeStruct((p_rows, ROW_TILE, 128), bf16),
        compiler_params=_cparams(("arbitrary",)),
        name="moe_dispatch",
    )(fill_off, fill_n, nused, dest3, hrow)


def _experts_body(blk_ref, nused_ref, xs_hbm, wg_hbm, wu_hbm, wd_hbm, ys_ref,
                  xbuf_ref, wg32_ref, wu32_ref, wd32_ref, wgu16_ref, wd16_ref, xsem, wsem):
    i = pl.program_id(0)
    nused = nused_ref[0]
    used = i < nused
    e = blk_ref[i]
    slot = lax.rem(i, X_SLOTS)

    def weight_copies(ex):
        return (pltpu.make_async_copy(wg_hbm.at[ex], wg32_ref, wsem.at[0]),
                pltpu.make_async_copy(wu_hbm.at[ex], wu32_ref, wsem.at[1]),
                pltpu.make_async_copy(wd_hbm.at[ex], wd32_ref, wsem.at[2]))

    def x_copy(block, s):
        return pltpu.make_async_copy(xs_hbm.at[pl.ds(block * MOE_BLOCK, MOE_BLOCK)], xbuf_ref.at[s], xsem.at[s])

    @pl.when((i == 0) & used)
    def _():
        for cp in weight_copies(e):
            cp.start()
        for j in range(X_SLOTS - 1):
            @pl.when(j < nused)
            def _(j=j):
                x_copy(j, j).start()

    ahead = i + (X_SLOTS - 1)

    @pl.when(ahead < nused)
    def _():
        x_copy(ahead, lax.rem(ahead, X_SLOTS)).start()

    @pl.when(used & ((i == 0) | (e != blk_ref[jnp.maximum(i - 1, 0)])))
    def _():
        for cp in weight_copies(e):
            cp.wait()
        wgu16_ref[:, 0:D_EXPERT] = wg32_ref[...].astype(bf16)
        wgu16_ref[:, D_EXPERT:2 * D_EXPERT] = wu32_ref[...].astype(bf16)
        wd16_ref[...] = wd32_ref[...].astype(bf16)
        nxt = lax.while_loop(lambda j: (j < nused) & (blk_ref[jnp.minimum(j, nused - 1)] == e), lambda j: j + 1, i + 1)

        @pl.when(nxt < nused)
        def _():
            for cp in weight_copies(blk_ref[jnp.minimum(nxt, nused - 1)]):
                cp.start(priority=1)

    @pl.when(used)
    def _():
        x_copy(i, slot).wait()
        half = MOE_BLOCK // 2
        rows = [slice(p * half, (p + 1) * half) for p in range(2)]
        gu = [jnp.dot(xbuf_ref[slot, r].reshape(half, D_MODEL), wgu16_ref[...], preferred_element_type=f32)
              for r in rows]
        hid = [(_silu(g[:, :D_EXPERT]) * g[:, D_EXPERT:]).astype(bf16) for g in gu]
        y = [jnp.dot(hd, wd16_ref[...], preferred_element_type=f32) for hd in hid]
        for r, yp in zip(rows, y):
            ys_ref[r] = yp.astype(bf16).reshape(half, ROW_TILE, 128)

    @pl.when(jnp.logical_not(used))
    def _():
        ys_ref[...] = jnp.zeros(ys_ref.shape, bf16)


def _stage_experts(blk_e, nused, xs, w_gate, w_up, w_down):
    p_rows = xs.shape[0]
    nb = p_rows // MOE_BLOCK

    def row_map(i, blk, nu):
        return (i, 0, 0)

    return pl.pallas_call(
        _experts_body,
        grid_spec=pltpu.PrefetchScalarGridSpec(
            num_scalar_prefetch=2,
            grid=(nb,),
            in_specs=[pl.BlockSpec(memory_space=pl.ANY),
                      pl.BlockSpec(memory_space=pl.ANY),
                      pl.BlockSpec(memory_space=pl.ANY),
                      pl.BlockSpec(memory_space=pl.ANY)],
            out_specs=pl.BlockSpec((MOE_BLOCK, ROW_TILE, 128), row_map),
            scratch_shapes=[pltpu.VMEM((X_SLOTS, MOE_BLOCK, ROW_TILE, 128), bf16),
                            pltpu.VMEM((D_MODEL, D_EXPERT), f32), pltpu.VMEM((D_MODEL, D_EXPERT), f32),
                            pltpu.VMEM((D_EXPERT, D_MODEL), f32),
                            pltpu.VMEM((D_MODEL, 2 * D_EXPERT), bf16), pltpu.VMEM((D_EXPERT, D_MODEL), bf16),
                            pltpu.SemaphoreType.DMA((X_SLOTS,)), pltpu.SemaphoreType.DMA((3,))],
        ),
        out_shape=jax.ShapeDtypeStruct((p_rows, ROW_TILE, 128), bf16),
        compiler_params=_cparams(("arbitrary",)),
        name="moe_experts",
    )(blk_e, nused, xs, w_gate, w_up, w_down)


def _combine_body(dcur_ref, dnext_ref, h_ref, wts_ref, g2_ref, b2_ref, ys_ref, o_ref, ybuf_ref, wcol_ref, sem,
                  *, tm, nsteps):
    i = pl.program_id(0)
    slot = lax.rem(i, 2)
    G = COMBINE_GROUP
    ngroups = tm // G

    def issue_group(d_ref, s, j):
        for r in range(G):
            t = j * G + r
            for k in range(2):
                pltpu.make_async_copy(ys_ref.at[d_ref[0, k, t]], ybuf_ref.at[s, k, t], sem.at[s]).start(priority=k)

    @pl.when(i == 0)
    def _():
        lax.fori_loop(0, ngroups, lambda j, c: (issue_group(dcur_ref, 0, j), c)[1], 0)

    for k in range(2):
        pltpu.make_async_copy(ys_ref.at[pl.ds(0, tm)], ybuf_ref.at[slot, k], sem.at[slot]).wait()

    for c in range(tm // 128):
        ls = slice(c * 128, (c + 1) * 128)
        for k in range(2):
            wcol_ref[k, ls, :] = jnp.broadcast_to(wts_ref[k:k + 1, ls], (128, 128)).T

    def combine_group(j):
        rows = pl.ds(pl.multiple_of(j * G, G), G)
        y1 = ybuf_ref[slot, 0, rows].reshape(G, D_MODEL).astype(f32)
        y2 = ybuf_ref[slot, 1, rows].reshape(G, D_MODEL).astype(f32)
        w1 = jnp.concatenate([wcol_ref[0, rows, :]] * (D_MODEL // 128), axis=1)
        w2 = jnp.concatenate([wcol_ref[1, rows, :]] * (D_MODEL // 128), axis=1)
        hp = DEEPNORM_ALPHA * h_ref[rows, :] + (w1 * y1 + w2 * y2)
        mu = jnp.mean(hp, axis=-1, keepdims=True)
        hc = hp - mu
        var = jnp.mean(hc * hc, axis=-1, keepdims=True)
        o_ref[rows, :] = hc * lax.rsqrt(var + LN_EPS) * g2_ref[...] + b2_ref[...]

    @pl.when(i + 1 < nsteps)
    def _():
        def body(j, c):
            combine_group(j)
            issue_group(dnext_ref, 1 - slot, j)
            return c

        lax.fori_loop(0, ngroups, body, 0, unroll=8)

    @pl.when(i + 1 >= nsteps)
    def _():
        lax.fori_loop(0, ngroups, lambda j, c: (combine_group(j), c)[1], 0, unroll=8)


def _stage_combine(dest3, h2, wts, g2, b2, ys, *, tm):
    n = h2.shape[0]
    nsteps = n // tm
    return pl.pallas_call(
        functools.partial(_combine_body, tm=tm, nsteps=nsteps),
        grid=(nsteps,),
        in_specs=[pl.BlockSpec((1, 2, tm), lambda i: (i, 0, 0), memory_space=pltpu.SMEM),
                  pl.BlockSpec((1, 2, tm), lambda i: (jnp.minimum(i + 1, nsteps - 1), 0, 0), memory_space=pltpu.SMEM),
                  pl.BlockSpec((tm, D_MODEL), lambda i: (i, 0)),
                  pl.BlockSpec((8, tm), lambda i: (0, i)),
                  pl.BlockSpec((1, D_MODEL), lambda i: (0, 0)),
                  pl.BlockSpec((1, D_MODEL), lambda i: (0, 0)),
                  pl.BlockSpec(memory_space=pl.ANY)],
        out_specs=pl.BlockSpec((tm, D_MODEL), lambda i: (i, 0)),
        out_shape=jax.ShapeDtypeStruct((n, D_MODEL), f32),
        scratch_shapes=[pltpu.VMEM((2, 2, tm, ROW_TILE, 128), bf16), pltpu.VMEM((2, tm, 128), f32),
                        pltpu.SemaphoreType.DMA((2,))],
        compiler_params=_cparams(("arbitrary",)),
        name="moe_combine",
    )(dest3, dest3, h2, wts, g2, b2, ys)


def _layer(h, w_in, conv_w, a_log, dt_bias, dn_norm_w, sgu_ln_g, sgu_ln_b, w_spatial, b_spatial, w_out,
           ln1_g, ln1_b, w_rg, b_rg, w_re, b_re, w_gate, w_up, w_down, ln2_g, ln2_b,
           *, tm_in, in_groups, dn_rows, dn_chunks, dn_groups, tm_mix, tm_rank, tm_disp, tm_comb):
    B, T, _ = h.shape
    n = B * T
    qkvz = 4 * DN_WIDTH
    ba0 = qkvz
    uv0 = qkvz + 2 * DN_HEADS
    w_cols = jnp.zeros((D_MODEL, IN_COLS_ALIGNED), bf16)
    w_cols = lax.dynamic_update_slice(w_cols, w_in[:, :qkvz].astype(bf16), (0, 0))
    w_cols = lax.dynamic_update_slice(w_cols, w_in[:, uv0:].astype(bf16), (0, qkvz))
    w_cols = lax.dynamic_update_slice(w_cols, w_in[:, ba0:uv0].astype(bf16), (0, qkvz + 2 * SGU_WIDTH))
    decay_prm = jnp.stack([a_log, dt_bias])
    prow =jnp.broadcast_to(jnp.pad(decay_prm, ((0, 0), (DN_HEADS, 8 - 2 * DN_HEADS)))[:, :, None], (2, 8, 128))

    q, k, v, z, u, vln, gcol, grow = _stage_inproj(
        h, w_cols, conv_w, prow, sgu_ln_g[None, :], sgu_ln_b[None, :], tm=tm_in, ngroups=in_groups)
    ydn = _stage_deltanet(q, k, v, z, gcol, grow, dn_norm_w[None, :], nbr=dn_rows, nch=dn_chunks, ngroups=dn_groups)

    bsp = jnp.broadcast_to(b_spatial.T[:, :, None], (SGU_CHUNK, SGU_GROUPS, SGU_CHUNK)).reshape(SGU_CHUNK, SGU_WIDTH)
    n_logit = MOE_GROUPS + N_EXPERTS
    wrt = jnp.pad(jnp.concatenate([w_rg, w_re], axis=1).T, ((0, 128 - n_logit), (0, 0))).astype(bf16)
    brt = jnp.broadcast_to(jnp.pad(jnp.concatenate([b_rg, b_re]), (0, 128 - n_logit))[:, None], (128, 128))
    h1, hrow, ids, wts = _stage_mixout(ydn, u, vln, h, w_spatial, bsp, w_out.astype(bf16), ln1_g[None, :],
                                       ln1_b[None, :], wrt, brt, tm=tm_mix)

    p_rows = (-(-(n * 2) // MOE_BLOCK)) * MOE_BLOCK + N_EXPERTS * MOE_BLOCK
    nb = p_rows // MOE_BLOCK
    nb_pad = (-(-nb // 128)) * 128
    dest, meta, blk = _stage_route(ids, tm=tm_rank, nb_pad=nb_pad)

    h2 = h1.reshape(n, D_MODEL)
    dest_d = dest[0:2].reshape(2, n // tm_disp, tm_disp).transpose(1, 0, 2)
    xs = _stage_dispatch(meta[0, :N_EXPERTS], meta[1, :N_EXPERTS], meta[2, 0:1], dest_d, hrow, p_rows, tm=tm_disp)
    ys = _stage_experts(blk[0, :nb], meta[2, 0:1], xs, w_gate, w_up, w_down)
    dest_c = dest[0:2].reshape(2, n // tm_comb, tm_comb).transpose(1, 0, 2)
    out = _stage_combine(dest_c, h2, wts, ln2_g[None, :], ln2_b[None, :], ys, tm=tm_comb)
    return out.reshape(B, T, D_MODEL)


def kernel(x, w_in, conv_w, a_log, dt_bias, dn_norm_w, sgu_ln_g, sgu_ln_b, w_spatial, b_spatial, w_out, ln1_g, ln1_b, w_router_group, b_router_group, w_router_expert, b_router_expert, w_gate, w_up, w_down, ln2_g, ln2_b):
    h = x
    for l in range(w_in.shape[0]):
        h = _layer(h, w_in[l], conv_w[l], a_log[l], dt_bias[l], dn_norm_w[l], sgu_ln_g[l], sgu_ln_b[l],
                   w_spatial[l], b_spatial[l], w_out[l], ln1_g[l], ln1_b[l],
                   w_router_group[l], b_router_group[l], w_router_expert[l], b_router_expert[l],
                   w_gate[l], w_up[l], w_down[l], ln2_g[l], ln2_b[l],
                   tm_in=512, in_groups=1, dn_rows=4, dn_chunks=2, dn_groups=2, tm_mix=512, tm_rank=1024, tm_disp=2048, tm_comb=512)
    return h
```

```python
import functools

import jax
import jax.numpy as jnp
from jax import lax
from jax.experimental import pallas as pl
from jax.experimental.pallas import tpu as pltpu

f32 = jnp.float32
bf16 = jnp.bfloat16
i32 = jnp.int32

D_MODEL = 1024
DN_WIDTH = 512
DN_HEADS = 4
HEAD_DIM = 128
CONV_K = 4
SGU_WIDTH = 512
SGU_GROUPS = 4
SGU_CHUNK = 128
DN_CHUNK = 128
MOE_GROUPS = 8
EXPERTS_PER_GROUP = 8
N_EXPERTS = 64
D_EXPERT = 512
MOE_BLOCK = 256
IN_COLS_ALIGNED = 4 * DN_WIDTH + 2 * SGU_WIDTH + 128
X_SLOTS = 4
ROW_TILE = D_MODEL // 128
DEEPNORM_ALPHA = 2.0 ** 0.25
LN_EPS = 1e-5
RMS_EPS = 1e-6
HIGHEST = lax.Precision.HIGHEST
VMEM_LIMIT_BYTES = 56 * 1024 * 1024

NT_DIMS = (((1,), (1,)), ((), ()))


def _cparams(sem, flags=None):
    return pltpu.CompilerParams(dimension_semantics=sem, vmem_limit_bytes=VMEM_LIMIT_BYTES, flags=flags)


def _sigmoid(x):
    return 1.0 / (1.0 + jnp.exp(-x))


def _silu(x):
    h = 0.5 * x
    return h + h * jnp.tanh(h)


def _softplus(x):
    return jnp.maximum(x, 0.0) + jnp.log1p(jnp.exp(-jnp.abs(x)))


def _gelu_tanh(x):
    c = 0.7978845608028654
    return x * (0.5 * (1.0 + jnp.tanh(c * (x + 0.044715 * (x * x * x)))))


def _iota2(shape, axis):
    return lax.broadcasted_iota(i32, shape, axis)


def _inproj_body(x_ref, w_ref, convw_ref, prow_ref, lng_ref, lnb_ref, ones_ref,
                 q_ref, k_ref, v_ref, z_ref, u_ref, vln_ref, gcol_ref, grow_ref, *ext_refs, tm, ngroups):
    W = DN_WIDTH
    gm = tm // ngroups
    ext = [ext_refs[3 * g:3 * g + 3] for g in range(ngroups)]

    @pl.when(pl.program_id(1) == 0)
    def _():
        for e_ref in ext[0]:
            e_ref[0:8, :] = jnp.zeros((8, W), f32)

    for g in range(ngroups):
        _inproj_rows(x_ref, w_ref, convw_ref, prow_ref, lng_ref, lnb_ref, ones_ref,
                     q_ref, k_ref, v_ref, z_ref, u_ref, vln_ref, gcol_ref, grow_ref, ext[g],
                     ext[(g + 1) % ngroups], r0=g * gm, gm=gm)


def _inproj_rows(x_ref, w_ref, convw_ref, prow_ref, lng_ref, lnb_ref, ones_ref,
                 q_ref, k_ref, v_ref, z_ref, u_ref, vln_ref, gcol_ref, grow_ref, ext, ext_next, *, r0, gm):
    W = DN_WIDTH
    rows = slice(r0, r0 + gm)
    xb = x_ref[0, rows, :].astype(bf16)
    for part, e_ref in enumerate(ext):
        e_ref[8:8 + gm, :] = jnp.dot(xb, w_ref[:, part * W:(part + 1) * W], preferred_element_type=f32)
    zc = 3 * W
    uc = zc + W
    vc = uc + SGU_WIDTH
    bc = vc + SGU_WIDTH
    pz = jnp.dot(xb, w_ref[:, zc:zc + W], preferred_element_type=f32)
    pu = jnp.dot(xb, w_ref[:, uc:uc + SGU_WIDTH], preferred_element_type=f32)
    pv = jnp.dot(xb, w_ref[:, vc:vc + SGU_WIDTH], preferred_element_type=f32)
    pba = jnp.dot(xb, w_ref[:, bc:bc + 128], preferred_element_type=f32)

    def group_sums(a):
        return jnp.dot(a.astype(bf16), ones_ref[...], preferred_element_type=f32)

    for part, (e_ref, n_ref, out_ref) in enumerate(zip(ext, ext_next, (q_ref, k_ref, v_ref))):
        cs = slice(part * W, (part + 1) * W)
        y = convw_ref[3:4, cs] * e_ref[8:8 + gm, :]
        for j in range(CONV_K - 1):
            y = y + convw_ref[j:j + 1, cs] * e_ref[5 + j:5 + j + gm, :]
        y = _silu(y)
        if part < 2:
            scale = HEAD_DIM ** -0.5 if part == 0 else 1.0
            y = y * (lax.rsqrt(group_sums(y * y) + RMS_EPS) * scale)
        out_ref[0, rows, :] = y.astype(bf16)
        n_ref[0:8, :] = e_ref[gm:gm + 8, :]

    z_ref[0, rows, :] = _silu(pz).astype(bf16)

    u_ref[0, rows, :] = _gelu_tanh(pu).astype(bf16)
    pv = _gelu_tanh(pv)
    for g in range(SGU_GROUPS):
        sl = slice(g * SGU_CHUNK, (g + 1) * SGU_CHUNK)
        vg = pv[:, sl]
        mu = jnp.mean(vg, axis=-1, keepdims=True)
        vcn = vg - mu
        var = jnp.mean(vcn * vcn, axis=-1, keepdims=True)
        vln_ref[0, rows, sl] = (vcn * lax.rsqrt(var + LN_EPS) * lng_ref[:, sl] + lnb_ref[:, sl]).astype(bf16)

    lane = _iota2((DN_CHUNK, 128), 1)
    beta = _sigmoid(pba)
    lane8 = _iota2((8, DN_CHUNK), 1)
    sub8 = _iota2((8, DN_CHUNK), 0)
    for c in range(gm // DN_CHUNK):
        rs = slice(c * DN_CHUNK, (c + 1) * DN_CHUNK)
        os_ = slice(r0 + c * DN_CHUNK, r0 + (c + 1) * DN_CHUNK)
        pbat = pba[rs].T[0:8, :]
        gt = -jnp.exp(prow_ref[0]) * _softplus(pbat + prow_ref[1])
        gc = jnp.where(sub8 >= DN_HEADS, gt, 0.0)
        shift = 1
        while shift < DN_CHUNK:
            gc = gc + jnp.where(lane8 >= shift, pltpu.roll(gc, shift, axis=1), 0.0)
            shift *= 2
        grow_ref[0, :, os_] = gc
        gc_col = jnp.concatenate([gc, jnp.zeros((DN_CHUNK - 8, DN_CHUNK), f32)], axis=0).T
        gcol_ref[0, os_, :] = jnp.where(lane < DN_HEADS, beta[rs], gc_col)


def _stage_inproj(x, w_re, conv_w, prow, lng, lnb, *, tm, ngroups):
    B, T, _ = x.shape
    wcols = w_re.shape[1]
    grid = (B, T // tm)
    gi = lax.broadcasted_iota(i32, (DN_WIDTH, DN_WIDTH), 0) // 128
    gj = lax.broadcasted_iota(i32, (DN_WIDTH, DN_WIDTH), 1) // 128
    group_ones = (gi == gj).astype(bf16)
    act = lambda: jax.ShapeDtypeStruct((B, T, DN_WIDTH), bf16)
    act_spec = lambda: pl.BlockSpec((1, tm, DN_WIDTH), lambda b, t: (b, t, 0))
    const2 = lambda shp: pl.BlockSpec(shp, lambda b, t: (0, 0))
    return pl.pallas_call(
        functools.partial(_inproj_body, tm=tm, ngroups=ngroups),
        grid=grid,
        in_specs=[
            pl.BlockSpec((1, tm, D_MODEL), lambda b, t: (b, t, 0)),
            const2((D_MODEL, wcols)),
            const2((CONV_K, 3 * DN_WIDTH)),
            pl.BlockSpec((2, 8, 128), lambda b, t: (0, 0, 0)),
            const2((1, SGU_WIDTH)),
            const2((1, SGU_WIDTH)),
            const2((DN_WIDTH, DN_WIDTH)),
        ],
        out_specs=[act_spec() for _ in range(6)] + [
            pl.BlockSpec((1, tm, 128), lambda b, t: (b, t, 0)),
            pl.BlockSpec((1, 8, tm), lambda b, t: (b, 0, t)),
        ],
        out_shape=[act() for _ in range(6)] + [
            jax.ShapeDtypeStruct((B, T, 128), f32),
            jax.ShapeDtypeStruct((B, 8, T), f32),
        ],
        scratch_shapes=[pltpu.VMEM((tm // ngroups + 8, DN_WIDTH), f32) for _ in range(3 * ngroups)],
        compiler_params=_cparams(("arbitrary", "arbitrary")),
        name="inproj",
    )(x, w_re, conv_w, prow, lng, lnb, group_ones)


def _mm(a, b):
    return jnp.dot(a.astype(bf16), b.astype(bf16), preferred_element_type=f32)


def _unit_lower_inverse(nmats, ii, jj):
    n = nmats[0].shape[0]
    eye = (ii == jj).astype(f32)
    leaf = jnp.right_shift(ii, 3) == jnp.right_shift(jj, 3)
    dblk = [jnp.where(leaf, m, 0.0) for m in nmats]
    s1 = [_mm(d, d) for d in dblk]
    r1 = [eye - d for d in dblk]
    both = [_mm(s, jnp.concatenate([s, r], axis=1)) for s, r in zip(s1, r1)]
    r2 = [r + bo[:, n:] for r, bo in zip(r1, both)]
    xs = [r + _mm(bo[:, :n], r) for r, bo in zip(r2, both)]
    size = 8
    while size < n:
        lows = [slice(r + size, r + 2 * size) for r in range(0, n, 2 * size)]
        ups = [slice(r, r + size) for r in range(0, n, 2 * size)]
        rsel = _iota2((n // 2, n), 0)
        ilow = rsel + size * (jnp.right_shift(rsel, size.bit_length() - 1) + 1)
        jlow = _iota2((n // 2, n), 1)
        in_pair_upper = (jlow >= ilow - (ilow & (2 * size - 1))) & (jlow < ilow - (ilow & (size - 1)))
        zeros = jnp.zeros((size, n), f32)
        new_xs = []
        ylows = [_mm(jnp.where(in_pair_upper, jnp.concatenate([m[s] for s in lows], axis=0), 0.0), x)
                 for m, x in zip(nmats, xs)]
        yfull = [jnp.concatenate([piece for k in range(len(lows)) for piece in (zeros, y[k * size:(k + 1) * size])], axis=0)
                 for y in ylows]
        corr = [_mm(jnp.concatenate([x[s] for s in lows], axis=0), yf) for x, yf in zip(xs, yfull)]
        for x, c in zip(xs, corr):
            pieces = []
            for k, (u, l) in enumerate(zip(ups, lows)):
                pieces += [x[u], x[l] - c[k * size:(k + 1) * size]]
            new_xs.append(jnp.concatenate(pieces, axis=0))
        xs = new_xs
        size *= 2
    return xs


def _deltanet_body(q_ref, k_ref, v_ref, z_ref, gcol_ref, grow_ref, nw_ref, y_ref, s_ref, *, nbr, nch, ngroups):
    @pl.when(pl.program_id(1) == 0)
    def _():
        s_ref[...] = jnp.zeros(s_ref.shape, f32)

    per = nbr // ngroups
    for g in range(ngroups):
        _deltanet_rows(q_ref, k_ref, v_ref, z_ref, gcol_ref, grow_ref, nw_ref, y_ref, s_ref,
                       rows=range(g * per, (g + 1) * per), nch=nch)


def _deltanet_rows(q_ref, k_ref, v_ref, z_ref, gcol_ref, grow_ref, nw_ref, y_ref, s_ref, *, rows, nch):
    C = DN_CHUNK
    S = [(b, h) for b in rows for h in range(DN_HEADS)]
    P = [(b, c, h) for c in range(nch) for b, h in S]

    ii = _iota2((C, C), 0)
    jj = _iota2((C, C), 1)
    causal = ii >= jj
    rs = [slice(c * C, (c + 1) * C) for c in range(nch)]
    hs = [slice(h * HEAD_DIM, (h + 1) * HEAD_DIM) for h in range(DN_HEADS)]
    gcol = {(b, c): gcol_ref[b, rs[c], :] for b in rows for c in range(nch)}
    qh = {(b, c, h): q_ref[b, rs[c], hs[h]] for b, c, h in P}
    kh = {(b, c, h): k_ref[b, rs[c], hs[h]] for b, c, h in P}
    vh = {(b, c, h): v_ref[b, rs[c], hs[h]] for b, c, h in P}
    gc_b = {(b, c, h): jnp.broadcast_to(gcol[b, c][:, DN_HEADS + h:DN_HEADS + h + 1], (C, HEAD_DIM)) for b, c, h in P}
    beta_b = {(b, c, h): jnp.broadcast_to(gcol[b, c][:, h:h + 1], (C, HEAD_DIM)) for b, c, h in P}
    gc_r = {(b, c, h): jnp.broadcast_to(grow_ref[b, DN_HEADS + h:DN_HEADS + h + 1, rs[c]], (C, C)) for b, c, h in P}
    decay = {p: jnp.exp(jnp.where(causal, gc_b[p] - gc_r[p], -1e30)) for p in P}

    kf = {p: kh[p].astype(f32) for p in P}
    kb = {p: kf[p] * beta_b[p] for p in P}
    kk = {p: lax.dot_general(kb[p].astype(bf16), kh[p], NT_DIMS, preferred_element_type=f32) for p in P}
    a_intra = {p: lax.dot_general(qh[p], kh[p], NT_DIMS, preferred_element_type=f32) * decay[p] for p in P}
    nmat = [jnp.where(ii > jj, kk[p] * decay[p], 0.0) for p in P]
    tinv = dict(zip(P, _unit_lower_inverse(nmat, ii, jj)))

    eg = {p: jnp.exp(gc_b[p]) for p in P}
    rhs = {p: jnp.concatenate([vh[p].astype(f32) * beta_b[p], kb[p] * eg[p]], axis=1) for p in P}
    sol = {p: _mm(tinv[p], rhs[p]) for p in P}

    q_dec = {p: qh[p].astype(f32) * eg[p] for p in P}
    g_last = {p: gc_b[p][C - 1:C, :] for p in P}
    kdt = {p: (kf[p] * jnp.exp(g_last[p] - gc_b[p])).T for p in P}

    state = {(b, h): s_ref[b * DN_HEADS + h] for b, h in S}
    for c in range(nch):
        m1 = {(b, h): _mm(jnp.concatenate([sol[b, c, h][:, HEAD_DIM:], q_dec[b, c, h]], axis=0), state[b, h]) for b, h in S}
        v_new = {(b, h): sol[b, c, h][:, :HEAD_DIM] - m1[b, h][:C] for b, h in S}
        m2 = {(b, h): _mm(jnp.concatenate([a_intra[b, c, h], kdt[b, c, h]], axis=0), v_new[b, h]) for b, h in S}
        state = {(b, h): state[b, h] * jnp.exp(g_last[b, c, h]) + m2[b, h][C:] for b, h in S}
        for b, h in S:
            o = m1[b, h][C:] + m2[b, h][:C]
            rms = lax.rsqrt(jnp.mean(o * o, axis=-1, keepdims=True) + RMS_EPS)
            y_ref[b, rs[c], hs[h]] = (o * rms * nw_ref[...] * z_ref[b, rs[c], hs[h]].astype(f32)).astype(bf16)
    for b, h in S:
        s_ref[b * DN_HEADS + h] = state[b, h]


def _stage_deltanet(q, k, v, z, gcol, grow, norm_w, *, nbr, nch, ngroups):
    B, T, _ = q.shape
    tt = nch * DN_CHUNK
    act_spec = lambda: pl.BlockSpec((nbr, tt, DN_WIDTH), lambda b, t: (b, t, 0))
    return pl.pallas_call(
        functools.partial(_deltanet_body, nbr=nbr, nch=nch, ngroups=ngroups),
        grid=(B // nbr, T // tt),
        in_specs=[act_spec(), act_spec(), act_spec(), act_spec(),
                  pl.BlockSpec((nbr, tt, 128), lambda b, t: (b, t, 0)),
                  pl.BlockSpec((nbr, 8, tt), lambda b, t: (b, 0, t)),
                  pl.BlockSpec((1, HEAD_DIM), lambda b, t: (0, 0))],
        out_specs=act_spec(),
        out_shape=jax.ShapeDtypeStruct((B, T, DN_WIDTH), bf16),
        scratch_shapes=[pltpu.VMEM((nbr * DN_HEADS, HEAD_DIM, HEAD_DIM), f32)],
        compiler_params=_cparams(("arbitrary", "arbitrary")),
        name="deltanet",
    )(q, k, v, z, gcol, grow, norm_w)


def _mixout_body(ydn_ref, u_ref, vln_ref, x_hbm, ws_ref, bsp_ref, wout_ref, g1_ref, b1_ref, wrt_ref, brt_ref,
                 h_ref, hrow_ref, ids_ref, wts_ref, ycat_ref, xring_ref, xsem, *, tm, nsteps):
    C = SGU_CHUNK
    step = pl.program_id(0) * pl.num_programs(1) + pl.program_id(1)
    slot = lax.rem(step, X_SLOTS)

    def x_copy(s, sl):
        return pltpu.make_async_copy(x_hbm.at[s], xring_ref.at[sl], xsem.at[sl])

    @pl.when(step == 0)
    def _():
        for j in range(X_SLOTS - 1):
            x_copy(j, j).start()

    ahead = step + (X_SLOTS - 1)

    @pl.when(ahead < nsteps)
    def _():
        x_copy(ahead, lax.rem(ahead, X_SLOTS)).start()

    x_copy(step, slot).wait()
    ii = _iota2((C, C), 0)
    jj = _iota2((C, C), 1)
    ycat_ref[:, 0:DN_WIDTH] = ydn_ref[0]
    for g in range(SGU_GROUPS):
        gs = slice(g * C, (g + 1) * C)
        wsg = jnp.where(ii >= jj, ws_ref[g], 0.0).astype(bf16)
        for c in range(tm // C):
            rs = slice(c * C, (c + 1) * C)
            mixed = jnp.dot(wsg, vln_ref[0, rs, gs], preferred_element_type=f32) + bsp_ref[:, gs]
            ycat_ref[rs, DN_WIDTH + g * C:DN_WIDTH + (g + 1) * C] = (u_ref[0, rs, gs].astype(f32) * mixed).astype(bf16)

    RB = 128
    blocks = [slice(r, r + RB) for r in range(0, tm, RB)]
    mix = [jnp.dot(ycat_ref[rb, :], wout_ref[...], preferred_element_type=f32) for rb in blocks]
    h1s = []
    for rb, m in zip(blocks, mix):
        hp = DEEPNORM_ALPHA * xring_ref[slot, rb, :] + m
        mu = jnp.mean(hp, axis=-1, keepdims=True)
        hc = hp - mu
        var = jnp.mean(hc * hc, axis=-1, keepdims=True)
        h1 = hc * lax.rsqrt(var + LN_EPS) * g1_ref[...] + b1_ref[...]
        h_ref[0, rb, :] = h1
        h1b = h1.astype(bf16)
        hrow_ref[rb] = h1b.reshape(RB, ROW_TILE, 128)
        h1s.append(h1b)

    logit_blocks = [lax.dot_general(wrt_ref[...], hb, NT_DIMS, preferred_element_type=f32) + brt_ref[...] for hb in h1s]
    sub = _iota2((8, RB), 0)
    subf = sub.astype(f32)
    for rb, logits in zip(blocks, logit_blocks):
        gl = logits[0:8]
        gmax = jnp.max(gl, axis=0, keepdims=True)
        g_idx = jnp.min(jnp.where(gl == gmax, subf, float(MOE_GROUPS)), axis=0, keepdims=True)
        p_group = 1.0 / jnp.sum(jnp.exp(gl - gmax), axis=0, keepdims=True)
        within = jnp.zeros((8, RB), f32)
        for g in range(MOE_GROUPS):
            within = within + jnp.where(g_idx == float(g), logits[8 + 8 * g:16 + 8 * g], 0.0)
        m1 = jnp.max(within, axis=0, keepdims=True)
        i1 = jnp.min(jnp.where(within == m1, subf, float(EXPERTS_PER_GROUP)), axis=0, keepdims=True)
        rest = jnp.where(subf == i1, -jnp.inf, within)
        m2 = jnp.max(rest, axis=0, keepdims=True)
        i2 = jnp.min(jnp.where(rest == m2, subf, float(EXPERTS_PER_GROUP)), axis=0, keepdims=True)
        e = jnp.exp(m2 - m1)
        w1 = p_group / (1.0 + e)
        w2 = p_group * e / (1.0 + e)
        e1 = g_idx * float(EXPERTS_PER_GROUP) + i1
        e2 = g_idx * float(EXPERTS_PER_GROUP) + i2
        ids_ref[:, rb] = jnp.where(sub == 0, e1, jnp.where(sub == 1, e2, 0.0)).astype(i32)
        wts_ref[:, rb] = jnp.where(sub == 0, w1, jnp.where(sub == 1, w2, 0.0))


def _stage_mixout(ydn, u, vln, x, ws, bsp, wout, g1, b1, wrt, brt, *, tm):
    B, T, _ = x.shape
    nt = T // tm
    act_spec = lambda: pl.BlockSpec((1, tm, DN_WIDTH), lambda b, t: (b, t, 0))
    const2 = lambda shp: pl.BlockSpec(shp, lambda b, t: (0, 0))
    tok_spec = lambda: pl.BlockSpec((8, tm), lambda b, t: (0, b * nt + t))
    return pl.pallas_call(
        functools.partial(_mixout_body, tm=tm, nsteps=B * nt),
        grid=(B, nt),
        in_specs=[act_spec(), act_spec(), act_spec(),
                  pl.BlockSpec(memory_space=pl.ANY),
                  pl.BlockSpec((SGU_GROUPS, SGU_CHUNK, SGU_CHUNK), lambda b, t: (0, 0, 0)),
                  const2((SGU_CHUNK, SGU_WIDTH)),
                  const2((D_MODEL, D_MODEL)),
                  const2((1, D_MODEL)), const2((1, D_MODEL)),
                  const2((128, D_MODEL)), const2((128, 128))],
        out_specs=[pl.BlockSpec((1, tm, D_MODEL), lambda b, t: (b, t, 0)),
                   pl.BlockSpec((tm, ROW_TILE, 128), lambda b, t: (b * nt + t, 0, 0)), tok_spec(), tok_spec()],
        out_shape=[jax.ShapeDtypeStruct((B, T, D_MODEL), f32),
                   jax.ShapeDtypeStruct((B * T, ROW_TILE, 128), bf16),
                   jax.ShapeDtypeStruct((8, B * T), i32),
                   jax.ShapeDtypeStruct((8, B * T), f32)],
        scratch_shapes=[pltpu.VMEM((tm, D_MODEL), bf16), pltpu.VMEM((X_SLOTS, tm, D_MODEL), f32),
                        pltpu.SemaphoreType.DMA((X_SLOTS,))],
        compiler_params=_cparams(("arbitrary", "arbitrary")),
        name="mixout",
    )(ydn, u, vln, x.reshape(B * nt, tm, D_MODEL), ws, bsp, wout, g1, b1, wrt, brt)


def _route_body(ids_ref, dest_ref, meta_ref, blk_ref, base_ref, pstart_ref, *, tm, nb_pad):
    phase = pl.program_id(0)
    i = pl.program_id(1)
    sub = _iota2((N_EXPERTS, tm), 0)
    is1 = sub == ids_ref[0:1, :]
    is2 = sub == ids_ref[1:2, :]
    oh = (is1.astype(f32) + is2.astype(f32)).astype(bf16)
    counts = jnp.dot(oh, jnp.ones((tm, 128), bf16), preferred_element_type=f32)

    @pl.when((phase == 0) & (i == 0))
    def _():
        base_ref[...] = jnp.zeros(base_ref.shape, f32)

    @pl.when(phase == 0)
    def _():
        base_ref[...] = base_ref[...] + counts

    @pl.when((phase == 1) & (i == 0))
    def _():
        cnt = base_ref[...]
        padded = jnp.floor((cnt + (MOE_BLOCK - 1)) * (1.0 / MOE_BLOCK)) * MOE_BLOCK
        ei = _iota2((N_EXPERTS, N_EXPERTS), 0)
        ej = _iota2((N_EXPERTS, N_EXPERTS), 1)
        pends = jnp.dot((ei >= ej).astype(f32), padded, precision=HIGHEST, preferred_element_type=f32)
        pstart = pends - padded
        pstart_ref[...] = pstart
        s64 = _iota2((N_EXPERTS, 128), 0)
        l64 = _iota2((N_EXPERTS, 128), 1)
        diag = s64 == l64
        fill_off = jnp.sum(jnp.where(diag, pstart + cnt, 0.0), axis=0, keepdims=True)
        fill_n = jnp.sum(jnp.where(diag, padded - cnt, 0.0), axis=0, keepdims=True)
        nused = pends[N_EXPERTS - 1:N_EXPERTS, :] * (1.0 / MOE_BLOCK)
        m8 = _iota2((8, 128), 0)
        meta_ref[...] = jnp.where(m8 == 0, fill_off, jnp.where(m8 == 1, fill_n, jnp.where(m8 == 2, nused, 0.0))).astype(i32)
        bstart = (_iota2((N_EXPERTS, nb_pad), 1) * MOE_BLOCK).astype(f32)
        pe = jnp.concatenate([pends] * (nb_pad // 128), axis=1)
        be = jnp.sum((pe <= bstart).astype(f32), axis=0, keepdims=True)
        be = jnp.minimum(be, float(N_EXPERTS - 1))
        blk_ref[...] = jnp.broadcast_to(be, (8, nb_pad)).astype(i32)

    @pl.when(phase == 1)
    def _():
        ti = _iota2((tm, tm), 0)
        tj = _iota2((tm, tm), 1)
        before = (ti < tj).astype(bf16)
        prefix = jnp.dot(oh, before, preferred_element_type=f32)
        nxt = prefix + jnp.concatenate([pstart_ref[...]] * (tm // 128), axis=1)
        d1 = jnp.sum(jnp.where(is1, nxt, 0.0), axis=0, keepdims=True)
        d2 = jnp.sum(jnp.where(is2, nxt, 0.0), axis=0, keepdims=True)
        sub8 = _iota2((8, tm), 0)
        dest_ref[...] = jnp.where(sub8 == 0, d1, jnp.where(sub8 == 1, d2, 0.0)).astype(i32)
        pstart_ref[...] = pstart_ref[...] + counts


def _stage_route(ids, *, tm, nb_pad):
    n = ids.shape[1]
    return pl.pallas_call(
        functools.partial(_route_body, tm=tm, nb_pad=nb_pad),
        grid=(2, n // tm),
        in_specs=[pl.BlockSpec((8, tm), lambda p, i: (0, i))],
        out_specs=[pl.BlockSpec((8, tm), lambda p, i: (0, i * p)),
                   pl.BlockSpec((8, 128), lambda p, i: (0, 0)),
                   pl.BlockSpec((8, nb_pad), lambda p, i: (0, 0))],
        out_shape=[jax.ShapeDtypeStruct((8, n), i32), jax.ShapeDtypeStruct((8, 128), i32),
                   jax.ShapeDtypeStruct((8, nb_pad), i32)],
        scratch_shapes=[pltpu.VMEM((N_EXPERTS, 128), f32), pltpu.VMEM((N_EXPERTS, 128), f32)],
        compiler_params=_cparams(("arbitrary", "arbitrary")),
        name="moe_route",
    )(ids)


def _dispatch_body(fill_off_ref, fill_n_ref, nused_ref, dest_ref, h3_ref, xs_ref, zero_ref, sem, zsem, *, tm):
    def row_copy(t, d):
        return pltpu.make_async_copy(h3_ref.at[t], xs_ref.at[d], sem)

    def issue(t, carry):
        row_copy(t, dest_ref[0, 0, t]).start(priority=0)
        row_copy(t, dest_ref[0, 1, t]).start(priority=1)
        return carry

    lax.fori_loop(0, tm, issue, 0, unroll=8)

    @pl.when(pl.program_id(0) == 0)
    def _():
        zero_ref[...] = jnp.zeros(zero_ref.shape, bf16)

        def fill(start):
            def body(e, carry):
                off = fill_off_ref[e]
                npad = fill_n_ref[e]
                bit = MOE_BLOCK // 2
                while bit:
                    @pl.when((npad & bit) != 0)
                    def _(off=off, bit=bit):
                        cp = pltpu.make_async_copy(zero_ref.at[pl.ds(0, bit)], xs_ref.at[pl.ds(off, bit)], zsem)
                        cp.start() if start else cp.wait()
                    off = off + (npad & bit)
                    bit //= 2
                return carry
            return body

        lax.fori_loop(0, N_EXPERTS, fill(True), 0)
        lax.fori_loop(0, N_EXPERTS, fill(False), 0)

        def tail_copy(b):
            return pltpu.make_async_copy(zero_ref, xs_ref.at[pl.ds(b * MOE_BLOCK, MOE_BLOCK)], zsem)

        nblocks = xs_ref.shape[0] // MOE_BLOCK
        lax.fori_loop(nused_ref[0], nblocks, lambda b, c: (tail_copy(b).start(), c)[1], 0)
        lax.fori_loop(nused_ref[0], nblocks, lambda b, c: (tail_copy(0).wait(), c)[1], 0)

    for _ in range(2):
        pltpu.make_async_copy(h3_ref, xs_ref.at[pl.ds(0, tm)], sem).wait()


def _stage_dispatch(fill_off, fill_n, nused, dest3, hrow, p_rows, *, tm):
    n = hrow.shape[0]
    return pl.pallas_call(
        functools.partial(_dispatch_body, tm=tm),
        grid_spec=pltpu.PrefetchScalarGridSpec(
            num_scalar_prefetch=3,
            grid=(n // tm,),
            in_specs=[pl.BlockSpec((1, 2, tm), lambda i, fo, fn, nu: (i, 0, 0), memory_space=pltpu.SMEM),
                      pl.BlockSpec((tm, ROW_TILE, 128), lambda i, fo, fn, nu: (i, 0, 0))],
            out_specs=pl.BlockSpec(memory_space=pl.ANY),
            scratch_shapes=[pltpu.VMEM((MOE_BLOCK, ROW_TILE, 128), bf16),
                            pltpu.SemaphoreType.DMA, pltpu.SemaphoreType.DMA],
        ),
        out_shape=jax.ShapeDtypeStruct((p_rows, ROW_TILE, 128), bf16),
        compiler_params=_cparams(("arbitrary",)),
        name="moe_dispatch",
    )(fill_off, fill_n, nused, dest3, hrow)


def _experts_body(blk_ref, nused_ref, xs_hbm, wg_hbm, wu_hbm, wd_hbm, ys_ref,
                  xbuf_ref, wg32_ref, wu32_ref, wd32_ref, wgu16_ref, wd16_ref, xsem, wsem):
    i = pl.program_id(0)
    nused = nused_ref[0]
    used = i < nused
    e = blk_ref[i]
    slot = lax.rem(i, X_SLOTS)

    def weight_copies(ex):
        return (pltpu.make_async_copy(wg_hbm.at[ex], wg32_ref, wsem.at[0]),
                pltpu.make_async_copy(wu_hbm.at[ex], wu32_ref, wsem.at[1]),
                pltpu.make_async_copy(wd_hbm.at[ex], wd32_ref, wsem.at[2]))

    def x_copy(block, s):
        return pltpu.make_async_copy(xs_hbm.at[pl.ds(block * MOE_BLOCK, MOE_BLOCK)], xbuf_ref.at[s], xsem.at[s])

    @pl.when((i == 0) & used)
    def _():
        for cp in weight_copies(e):
            cp.start()
        for j in range(X_SLOTS - 1):
            @pl.when(j < nused)
            def _(j=j):
                x_copy(j, j).start()

    ahead = i + (X_SLOTS - 1)

    @pl.when(ahead < nused)
    def _():
        x_copy(ahead, lax.rem(ahead, X_SLOTS)).start()

    @pl.when(used & ((i == 0) | (e != blk_ref[jnp.maximum(i - 1, 0)])))
    def _():
        for cp in weight_copies(e):
            cp.wait()
        wgu16_ref[:, 0:D_EXPERT] = wg32_ref[...].astype(bf16)
        wgu16_ref[:, D_EXPERT:2 * D_EXPERT] = wu32_ref[...].astype(bf16)
        wd16_ref[...] = wd32_ref[...].astype(bf16)
        nxt = lax.while_loop(lambda j: (j < nused) & (blk_ref[jnp.minimum(j, nused - 1)] == e), lambda j: j + 1, i + 1)

        @pl.when(nxt < nused)
        def _():
            for cp in weight_copies(blk_ref[jnp.minimum(nxt, nused - 1)]):
                cp.start(priority=1)

    @pl.when(used)
    def _():
        x_copy(i, slot).wait()
        half = MOE_BLOCK // 2
        rows = [slice(p * half, (p + 1) * half) for p in range(2)]
        gu = [jnp.dot(xbuf_ref[slot, r].reshape(half, D_MODEL), wgu16_ref[...], preferred_element_type=f32)
              for r in rows]
        hid = [(_silu(g[:, :D_EXPERT]) * g[:, D_EXPERT:]).astype(bf16) for g in gu]
        y = [jnp.dot(hd, wd16_ref[...], preferred_element_type=f32) for hd in hid]
        for r, yp in zip(rows, y):
            ys_ref[r] = yp.astype(bf16).reshape(half, ROW_TILE, 128)

    @pl.when(jnp.logical_not(used))
    def _():
        ys_ref[...] = jnp.zeros(ys_ref.shape, bf16)


def _stage_experts(blk_e, nused, xs, w_gate, w_up, w_down):
    p_rows = xs.shape[0]
    nb = p_rows // MOE_BLOCK

    def row_map(i, blk, nu):
        return (i, 0, 0)

    return pl.pallas_call(
        _experts_body,
        grid_spec=pltpu.PrefetchScalarGridSpec(
            num_scalar_prefetch=2,
            grid=(nb,),
            in_specs=[pl.BlockSpec(memory_space=pl.ANY),
                      pl.BlockSpec(memory_space=pl.ANY),
                      pl.BlockSpec(memory_space=pl.ANY),
                      pl.BlockSpec(memory_space=pl.ANY)],
            out_specs=pl.BlockSpec((MOE_BLOCK, ROW_TILE, 128), row_map),
            scratch_shapes=[pltpu.VMEM((X_SLOTS, MOE_BLOCK, ROW_TILE, 128), bf16),
                            pltpu.VMEM((D_MODEL, D_EXPERT), f32), pltpu.VMEM((D_MODEL, D_EXPERT), f32),
                            pltpu.VMEM((D_EXPERT, D_MODEL), f32),
                            pltpu.VMEM((D_MODEL, 2 * D_EXPERT), bf16), pltpu.VMEM((D_EXPERT, D_MODEL), bf16),
                            pltpu.SemaphoreType.DMA((X_SLOTS,)), pltpu.SemaphoreType.DMA((3,))],
        ),
        out_shape=jax.ShapeDtypeStruct((p_rows, ROW_TILE, 128), bf16),
        compiler_params=_cparams(("arbitrary",)),
        name="moe_experts",
    )(blk_e, nused, xs, w_gate, w_up, w_down)


def _combine_body(dcur_ref, dnext_ref, h_ref, wts_ref, g2_ref, b2_ref, ys_ref, o_ref, ybuf_ref, sem, *, tm, nsteps):
    i = pl.program_id(0)
    slot = lax.rem(i, 2)

    def issue_tile(d_ref, s):
        def body(t, carry):
            for k in range(2):
                pltpu.make_async_copy(ys_ref.at[d_ref[0, k, t]], ybuf_ref.at[s, k, t], sem.at[s]).start(priority=k)
            return carry

        lax.fori_loop(0, tm, body, 0, unroll=8)

    @pl.when(i == 0)
    def _():
        issue_tile(dcur_ref, 0)

    @pl.when(i + 1 < nsteps)
    def _():
        issue_tile(dnext_ref, 1 - slot)

    for k in range(2):
        pltpu.make_async_copy(ys_ref.at[pl.ds(0, tm)], ybuf_ref.at[slot, k], sem.at[slot]).wait()

    pieces = []
    for c in range(tm // 128):
        ls = slice(c * 128, (c + 1) * 128)
        w1c = jnp.broadcast_to(wts_ref[0:1, ls], (128, 128)).T
        w2c = jnp.broadcast_to(wts_ref[1:2, ls], (128, 128)).T
        w1f = jnp.concatenate([w1c] * (D_MODEL // 128), axis=1)
        w2f = jnp.concatenate([w2c] * (D_MODEL // 128), axis=1)
        y1 = ybuf_ref[slot, 0, ls].reshape(128, D_MODEL).astype(f32)
        y2 = ybuf_ref[slot, 1, ls].reshape(128, D_MODEL).astype(f32)
        pieces.append(w1f * y1 + w2f * y2)
    ffn = jnp.concatenate(pieces, axis=0)
    hp = DEEPNORM_ALPHA * h_ref[...] + ffn
    mu = jnp.mean(hp, axis=-1, keepdims=True)
    hc = hp - mu
    var = jnp.mean(hc * hc, axis=-1, keepdims=True)
    o_ref[...] = hc * lax.rsqrt(var + LN_EPS) * g2_ref[...] + b2_ref[...]


def _stage_combine(dest3, h2, wts, g2, b2, ys, *, tm):
    n = h2.shape[0]
    nsteps = n // tm
    return pl.pallas_call(
        functools.partial(_combine_body, tm=tm, nsteps=nsteps),
        grid=(nsteps,),
        in_specs=[pl.BlockSpec((1, 2, tm), lambda i: (i, 0, 0), memory_space=pltpu.SMEM),
                  pl.BlockSpec((1, 2, tm), lambda i: (jnp.minimum(i + 1, nsteps - 1), 0, 0), memory_space=pltpu.SMEM),
                  pl.BlockSpec((tm, D_MODEL), lambda i: (i, 0)),
                  pl.BlockSpec((8, tm), lambda i: (0, i)),
                  pl.BlockSpec((1, D_MODEL), lambda i: (0, 0)),
                  pl.BlockSpec((1, D_MODEL), lambda i: (0, 0)),
                  pl.BlockSpec(memory_space=pl.ANY)],
        out_specs=pl.BlockSpec((tm, D_MODEL), lambda i: (i, 0)),
        out_shape=jax.ShapeDtypeStruct((n, D_MODEL), f32),
        scratch_shapes=[pltpu.VMEM((2, 2, tm, ROW_TILE, 128), bf16), pltpu.SemaphoreType.DMA((2,))],
        compiler_params=_cparams(("arbitrary",)),
        name="moe_combine",
    )(dest3, dest3, h2, wts, g2, b2, ys)


def _layer(h, w_in, conv_w, a_log, dt_bias, dn_norm_w, sgu_ln_g, sgu_ln_b, w_spatial, b_spatial, w_out,
           ln1_g, ln1_b, w_rg, b_rg, w_re, b_re, w_gate, w_up, w_down, ln2_g, ln2_b,
           *, tm_in, in_groups, dn_rows, dn_chunks, dn_groups, tm_mix, tm_rank, tm_disp, tm_comb):
    B, T, _ = h.shape
    n = B * T
    qkvz = 4 * DN_WIDTH
    ba0 = qkvz
    uv0 = qkvz + 2 * DN_HEADS
    w_cols = jnp.zeros((D_MODEL, IN_COLS_ALIGNED), bf16)
    w_cols = lax.dynamic_update_slice(w_cols, w_in[:, :qkvz].astype(bf16), (0, 0))
    w_cols = lax.dynamic_update_slice(w_cols, w_in[:, uv0:].astype(bf16), (0, qkvz))
    w_cols = lax.dynamic_update_slice(w_cols, w_in[:, ba0:uv0].astype(bf16), (0, qkvz + 2 * SGU_WIDTH))
    decay_prm = jnp.stack([a_log, dt_bias])
    prow =jnp.broadcast_to(jnp.pad(decay_prm, ((0, 0), (DN_HEADS, 8 - 2 * DN_HEADS)))[:, :, None], (2, 8, 128))

    q, k, v, z, u, vln, gcol, grow = _stage_inproj(
        h, w_cols, conv_w, prow, sgu_ln_g[None, :], sgu_ln_b[None, :], tm=tm_in, ngroups=in_groups)
    ydn = _stage_deltanet(q, k, v, z, gcol, grow, dn_norm_w[None, :], nbr=dn_rows, nch=dn_chunks, ngroups=dn_groups)

    bsp = jnp.broadcast_to(b_spatial.T[:, :, None], (SGU_CHUNK, SGU_GROUPS, SGU_CHUNK)).reshape(SGU_CHUNK, SGU_WIDTH)
    n_logit = MOE_GROUPS + N_EXPERTS
    wrt = jnp.pad(jnp.concatenate([w_rg, w_re], axis=1).T, ((0, 128 - n_logit), (0, 0))).astype(bf16)
    brt = jnp.broadcast_to(jnp.pad(jnp.concatenate([b_rg, b_re]), (0, 128 - n_logit))[:, None], (128, 128))
    h1, hrow, ids, wts = _stage_mixout(ydn, u, vln, h, w_spatial, bsp, w_out.astype(bf16), ln1_g[None, :],
                                       ln1_b[None, :], wrt, brt, tm=tm_mix)

    p_rows = (-(-(n * 2) // MOE_BLOCK)) * MOE_BLOCK + N_EXPERTS * MOE_BLOCK
    nb = p_rows // MOE_BLOCK
    nb_pad = (-(-nb // 128)) * 128
    dest, meta, blk = _stage_route(ids, tm=tm_rank, nb_pad=nb_pad)

    h2 = h1.reshape(n, D_MODEL)
    dest_d = dest[0:2].reshape(2, n // tm_disp, tm_disp).transpose(1, 0, 2)
    xs = _stage_dispatch(meta[0, :N_EXPERTS], meta[1, :N_EXPERTS], meta[2, 0:1], dest_d, hrow, p_rows, tm=tm_disp)
    ys = _stage_experts(blk[0, :nb], meta[2, 0:1], xs, w_gate, w_up, w_down)
    dest_c = dest[0:2].reshape(2, n // tm_comb, tm_comb).transpose(1, 0, 2)
    out = _stage_combine(dest_c, h2, wts, ln2_g[None, :], ln2_b[None, :], ys, tm=tm_comb)
    return out.reshape(B, T, D_MODEL)


def kernel(x, w_in, conv_w, a_log, dt_bias, dn_norm_w, sgu_ln_g, sgu_ln_b, w_spatial, b_spatial, w_out, ln1_g, ln1_b, w_router_group, b_router_group, w_router_expert, b_router_expert, w_gate, w_up, w_down, ln2_g, ln2_b):
    h = x
    for l in range(w_in.shape[0]):
        h = _layer(h, w_in[l], conv_w[l], a_log[l], dt_bias[l], dn_norm_w[l], sgu_ln_g[l], sgu_ln_b[l],
                   w_spatial[l], b_spatial[l], w_out[l], ln1_g[l], ln1_b[l],
                   w_router_group[l], b_router_group[l], w_router_expert[l], b_router_expert[l],
                   w_gate[l], w_up[l], w_down[l], ln2_g[l], ln2_b[l],
                   tm_in=512, in_groups=1, dn_rows=8, dn_chunks=2, dn_groups=4, tm_mix=512, tm_rank=1024, tm_disp=2048, tm_comb=512)
    return h
```

```python
import functools

import jax
import jax.numpy as jnp
from jax import lax
from jax.experimental import pallas as pl
from jax.experimental.pallas import tpu as pltpu

f32 = jnp.float32
bf16 = jnp.bfloat16
i32 = jnp.int32

D_MODEL = 1024
DN_WIDTH = 512
DN_HEADS = 4
HEAD_DIM = 128
CONV_K = 4
SGU_WIDTH = 512
SGU_GROUPS = 4
SGU_CHUNK = 128
DN_CHUNK = 128
MOE_GROUPS = 8
EXPERTS_PER_GROUP = 8
N_EXPERTS = 64
D_EXPERT = 512
MOE_BLOCK = 256
IN_COLS_ALIGNED = 4 * DN_WIDTH + 2 * SGU_WIDTH + 128
X_SLOTS = 4
ROW_TILE = D_MODEL // 128
DEEPNORM_ALPHA = 2.0 ** 0.25
LN_EPS = 1e-5
RMS_EPS = 1e-6
HIGHEST = lax.Precision.HIGHEST
VMEM_LIMIT_BYTES = 56 * 1024 * 1024

NT_DIMS = (((1,), (1,)), ((), ()))


def _cparams(sem, flags=None):
    return pltpu.CompilerParams(dimension_semantics=sem, vmem_limit_bytes=VMEM_LIMIT_BYTES, flags=flags)


def _sigmoid(x):
    return 1.0 / (1.0 + jnp.exp(-x))


def _silu(x):
    h = 0.5 * x
    return h + h * jnp.tanh(h)


def _softplus(x):
    return jnp.maximum(x, 0.0) + jnp.log1p(jnp.exp(-jnp.abs(x)))


def _gelu_tanh(x):
    c = 0.7978845608028654
    return x * (0.5 * (1.0 + jnp.tanh(c * (x + 0.044715 * (x * x * x)))))


def _iota2(shape, axis):
    return lax.broadcasted_iota(i32, shape, axis)


def _inproj_body(x_ref, w_ref, convw_ref, prow_ref, lng_ref, lnb_ref, ones_ref,
                 q_ref, k_ref, v_ref, z_ref, u_ref, vln_ref, gcol_ref, grow_ref, *ext_refs, tm, ngroups):
    W = DN_WIDTH
    gm = tm // ngroups
    ext = [ext_refs[3 * g:3 * g + 3] for g in range(ngroups)]

    @pl.when(pl.program_id(1) == 0)
    def _():
        for e_ref in ext[0]:
            e_ref[0:8, :] = jnp.zeros((8, W), f32)

    for g in range(ngroups):
        _inproj_rows(x_ref, w_ref, convw_ref, prow_ref, lng_ref, lnb_ref, ones_ref,
                     q_ref, k_ref, v_ref, z_ref, u_ref, vln_ref, gcol_ref, grow_ref, ext[g],
                     ext[(g + 1) % ngroups], r0=g * gm, gm=gm)


def _inproj_rows(x_ref, w_ref, convw_ref, prow_ref, lng_ref, lnb_ref, ones_ref,
                 q_ref, k_ref, v_ref, z_ref, u_ref, vln_ref, gcol_ref, grow_ref, ext, ext_next, *, r0, gm):
    W = DN_WIDTH
    rows = slice(r0, r0 + gm)
    xb = x_ref[0, rows, :].astype(bf16)
    for part, e_ref in enumerate(ext):
        e_ref[8:8 + gm, :] = jnp.dot(xb, w_ref[:, part * W:(part + 1) * W], preferred_element_type=f32)
    zc = 3 * W
    uc = zc + W
    vc = uc + SGU_WIDTH
    bc = vc + SGU_WIDTH

    def conv_silu(part):
        e_ref = ext[part]
        cs = slice(part * W, (part + 1) * W)
        y = convw_ref[3:4, cs] * e_ref[8:8 + gm, :]
        for j in range(CONV_K - 1):
            y = y + convw_ref[j:j + 1, cs] * e_ref[5 + j:5 + j + gm, :]
        ext_next[part][0:8, :] = e_ref[gm:gm + 8, :]
        return _silu(y)

    def group_sums(a):
        return jnp.dot(a.astype(bf16), ones_ref[...], preferred_element_type=f32)

    yq = conv_silu(0)
    ssq = group_sums(yq * yq)
    pba = jnp.dot(xb, w_ref[:, bc:bc + 128], preferred_element_type=f32)
    pv = jnp.dot(xb, w_ref[:, vc:vc + SGU_WIDTH], preferred_element_type=f32)
    yk = conv_silu(1)
    ssk = group_sums(yk * yk)
    pu = jnp.dot(xb, w_ref[:, uc:uc + SGU_WIDTH], preferred_element_type=f32)
    pz = jnp.dot(xb, w_ref[:, zc:zc + W], preferred_element_type=f32)
    q_ref[0, rows, :] = (yq * (lax.rsqrt(ssq + RMS_EPS) * HEAD_DIM ** -0.5)).astype(bf16)
    k_ref[0, rows, :] = (yk * lax.rsqrt(ssk + RMS_EPS)).astype(bf16)
    v_ref[0, rows, :] = conv_silu(2).astype(bf16)

    z_ref[0, rows, :] = _silu(pz).astype(bf16)

    u_ref[0, rows, :] = _gelu_tanh(pu).astype(bf16)
    pv = _gelu_tanh(pv)
    for g in range(SGU_GROUPS):
        sl = slice(g * SGU_CHUNK, (g + 1) * SGU_CHUNK)
        vg = pv[:, sl]
        mu = jnp.mean(vg, axis=-1, keepdims=True)
        vcn = vg - mu
        var = jnp.mean(vcn * vcn, axis=-1, keepdims=True)
        vln_ref[0, rows, sl] = (vcn * lax.rsqrt(var + LN_EPS) * lng_ref[:, sl] + lnb_ref[:, sl]).astype(bf16)

    lane = _iota2((DN_CHUNK, 128), 1)
    beta = _sigmoid(pba)
    lane8 = _iota2((8, DN_CHUNK), 1)
    sub8 = _iota2((8, DN_CHUNK), 0)
    for c in range(gm // DN_CHUNK):
        rs = slice(c * DN_CHUNK, (c + 1) * DN_CHUNK)
        os_ = slice(r0 + c * DN_CHUNK, r0 + (c + 1) * DN_CHUNK)
        pbat = pba[rs].T[0:8, :]
        gt = -jnp.exp(prow_ref[0]) * _softplus(pbat + prow_ref[1])
        gc = jnp.where(sub8 >= DN_HEADS, gt, 0.0)
        shift = 1
        while shift < DN_CHUNK:
            gc = gc + jnp.where(lane8 >= shift, pltpu.roll(gc, shift, axis=1), 0.0)
            shift *= 2
        grow_ref[0, :, os_] = gc
        gc_col = jnp.concatenate([gc, jnp.zeros((DN_CHUNK - 8, DN_CHUNK), f32)], axis=0).T
        gcol_ref[0, os_, :] = jnp.where(lane < DN_HEADS, beta[rs], gc_col)


def _stage_inproj(x, w_re, conv_w, prow, lng, lnb, *, tm, ngroups):
    B, T, _ = x.shape
    wcols = w_re.shape[1]
    grid = (B, T // tm)
    gi = lax.broadcasted_iota(i32, (DN_WIDTH, DN_WIDTH), 0) // 128
    gj = lax.broadcasted_iota(i32, (DN_WIDTH, DN_WIDTH), 1) // 128
    group_ones = (gi == gj).astype(bf16)
    act = lambda: jax.ShapeDtypeStruct((B, T, DN_WIDTH), bf16)
    act_spec = lambda: pl.BlockSpec((1, tm, DN_WIDTH), lambda b, t: (b, t, 0))
    const2 = lambda shp: pl.BlockSpec(shp, lambda b, t: (0, 0))
    return pl.pallas_call(
        functools.partial(_inproj_body, tm=tm, ngroups=ngroups),
        grid=grid,
        in_specs=[
            pl.BlockSpec((1, tm, D_MODEL), lambda b, t: (b, t, 0)),
            const2((D_MODEL, wcols)),
            const2((CONV_K, 3 * DN_WIDTH)),
            pl.BlockSpec((2, 8, 128), lambda b, t: (0, 0, 0)),
            const2((1, SGU_WIDTH)),
            const2((1, SGU_WIDTH)),
            const2((DN_WIDTH, DN_WIDTH)),
        ],
        out_specs=[act_spec() for _ in range(6)] + [
            pl.BlockSpec((1, tm, 128), lambda b, t: (b, t, 0)),
            pl.BlockSpec((1, 8, tm), lambda b, t: (b, 0, t)),
        ],
        out_shape=[act() for _ in range(6)] + [
            jax.ShapeDtypeStruct((B, T, 128), f32),
            jax.ShapeDtypeStruct((B, 8, T), f32),
        ],
        scratch_shapes=[pltpu.VMEM((tm // ngroups + 8, DN_WIDTH), f32) for _ in range(3 * ngroups)],
        compiler_params=_cparams(("arbitrary", "arbitrary")),
        name="inproj",
    )(x, w_re, conv_w, prow, lng, lnb, group_ones)


def _mm(a, b):
    return jnp.dot(a.astype(bf16), b.astype(bf16), preferred_element_type=f32)


def _unit_lower_inverse(nmats, ii, jj):
    n = nmats[0].shape[0]
    eye = (ii == jj).astype(f32)
    leaf = jnp.right_shift(ii, 3) == jnp.right_shift(jj, 3)
    dblk = [jnp.where(leaf, m, 0.0) for m in nmats]
    s1 = [_mm(d, d) for d in dblk]
    r1 = [eye - d for d in dblk]
    both = [_mm(s, jnp.concatenate([s, r], axis=1)) for s, r in zip(s1, r1)]
    r2 = [r + bo[:, n:] for r, bo in zip(r1, both)]
    xs = [r + _mm(bo[:, :n], r) for r, bo in zip(r2, both)]
    size = 8
    while size < n:
        lows = [slice(r + size, r + 2 * size) for r in range(0, n, 2 * size)]
        ups = [slice(r, r + size) for r in range(0, n, 2 * size)]
        rsel = _iota2((n // 2, n), 0)
        ilow = rsel + size * (jnp.right_shift(rsel, size.bit_length() - 1) + 1)
        jlow = _iota2((n // 2, n), 1)
        in_pair_upper = (jlow >= ilow - (ilow & (2 * size - 1))) & (jlow < ilow - (ilow & (size - 1)))
        zeros = jnp.zeros((size, n), f32)
        new_xs = []
        ylows = [_mm(jnp.where(in_pair_upper, jnp.concatenate([m[s] for s in lows], axis=0), 0.0), x)
                 for m, x in zip(nmats, xs)]
        yfull = [jnp.concatenate([piece for k in range(len(lows)) for piece in (zeros, y[k * size:(k + 1) * size])], axis=0)
                 for y in ylows]
        corr = [_mm(jnp.concatenate([x[s] for s in lows], axis=0), yf) for x, yf in zip(xs, yfull)]
        for x, c in zip(xs, corr):
            pieces = []
            for k, (u, l) in enumerate(zip(ups, lows)):
                pieces += [x[u], x[l] - c[k * size:(k + 1) * size]]
            new_xs.append(jnp.concatenate(pieces, axis=0))
        xs = new_xs
        size *= 2
    return xs


def _deltanet_body(q_ref, k_ref, v_ref, z_ref, gcol_ref, grow_ref, nw_ref, y_ref, s_ref, *, nbr, nch, ngroups):
    @pl.when(pl.program_id(1) == 0)
    def _():
        s_ref[...] = jnp.zeros(s_ref.shape, f32)

    per = nbr // ngroups
    for g in range(ngroups):
        _deltanet_rows(q_ref, k_ref, v_ref, z_ref, gcol_ref, grow_ref, nw_ref, y_ref, s_ref,
                       rows=range(g * per, (g + 1) * per), nch=nch)


def _deltanet_rows(q_ref, k_ref, v_ref, z_ref, gcol_ref, grow_ref, nw_ref, y_ref, s_ref, *, rows, nch):
    C = DN_CHUNK
    S = [(b, h) for b in rows for h in range(DN_HEADS)]
    P = [(b, c, h) for c in range(nch) for b, h in S]

    ii = _iota2((C, C), 0)
    jj = _iota2((C, C), 1)
    causal = ii >= jj
    rs = [slice(c * C, (c + 1) * C) for c in range(nch)]
    hs = [slice(h * HEAD_DIM, (h + 1) * HEAD_DIM) for h in range(DN_HEADS)]
    gcol = {(b, c): gcol_ref[b, rs[c], :] for b in rows for c in range(nch)}
    qh = {(b, c, h): q_ref[b, rs[c], hs[h]] for b, c, h in P}
    kh = {(b, c, h): k_ref[b, rs[c], hs[h]] for b, c, h in P}
    vh = {(b, c, h): v_ref[b, rs[c], hs[h]] for b, c, h in P}
    gc_b = {(b, c, h): jnp.broadcast_to(gcol[b, c][:, DN_HEADS + h:DN_HEADS + h + 1], (C, HEAD_DIM)) for b, c, h in P}
    beta_b = {(b, c, h): jnp.broadcast_to(gcol[b, c][:, h:h + 1], (C, HEAD_DIM)) for b, c, h in P}
    gc_r = {(b, c, h): jnp.broadcast_to(grow_ref[b, DN_HEADS + h:DN_HEADS + h + 1, rs[c]], (C, C)) for b, c, h in P}
    decay = {p: jnp.exp(jnp.where(causal, gc_b[p] - gc_r[p], -1e30)) for p in P}

    kf = {p: kh[p].astype(f32) for p in P}
    kb = {p: kf[p] * beta_b[p] for p in P}
    kk = {p: lax.dot_general(kb[p].astype(bf16), kh[p], NT_DIMS, preferred_element_type=f32) for p in P}
    a_intra = {p: lax.dot_general(qh[p], kh[p], NT_DIMS, preferred_element_type=f32) * decay[p] for p in P}
    nmat = [jnp.where(ii > jj, kk[p] * decay[p], 0.0) for p in P]
    tinv = dict(zip(P, _unit_lower_inverse(nmat, ii, jj)))

    eg = {p: jnp.exp(gc_b[p]) for p in P}
    rhs = {p: jnp.concatenate([vh[p].astype(f32) * beta_b[p], kb[p] * eg[p]], axis=1) for p in P}
    sol = {p: _mm(tinv[p], rhs[p]) for p in P}

    q_dec = {p: qh[p].astype(f32) * eg[p] for p in P}
    g_last = {p: gc_b[p][C - 1:C, :] for p in P}
    kdt = {p: (kf[p] * jnp.exp(g_last[p] - gc_b[p])).T for p in P}

    state = {(b, h): s_ref[b * DN_HEADS + h] for b, h in S}
    for c in range(nch):
        m1 = {(b, h): _mm(jnp.concatenate([sol[b, c, h][:, HEAD_DIM:], q_dec[b, c, h]], axis=0), state[b, h]) for b, h in S}
        v_new = {(b, h): sol[b, c, h][:, :HEAD_DIM] - m1[b, h][:C] for b, h in S}
        m2 = {(b, h): _mm(jnp.concatenate([a_intra[b, c, h], kdt[b, c, h]], axis=0), v_new[b, h]) for b, h in S}
        state = {(b, h): state[b, h] * jnp.exp(g_last[b, c, h]) + m2[b, h][C:] for b, h in S}
        for b, h in S:
            o = m1[b, h][C:] + m2[b, h][:C]
            rms = lax.rsqrt(jnp.mean(o * o, axis=-1, keepdims=True) + RMS_EPS)
            y_ref[b, rs[c], hs[h]] = (o * rms * nw_ref[...] * z_ref[b, rs[c], hs[h]].astype(f32)).astype(bf16)
    for b, h in S:
        s_ref[b * DN_HEADS + h] = state[b, h]


def _stage_deltanet(q, k, v, z, gcol, grow, norm_w, *, nbr, nch, ngroups):
    B, T, _ = q.shape
    tt = nch * DN_CHUNK
    act_spec = lambda: pl.BlockSpec((nbr, tt, DN_WIDTH), lambda b, t: (b, t, 0))
    return pl.pallas_call(
        functools.partial(_deltanet_body, nbr=nbr, nch=nch, ngroups=ngroups),
        grid=(B // nbr, T // tt),
        in_specs=[act_spec(), act_spec(), act_spec(), act_spec(),
                  pl.BlockSpec((nbr, tt, 128), lambda b, t: (b, t, 0)),
                  pl.BlockSpec((nbr, 8, tt), lambda b, t: (b, 0, t)),
                  pl.BlockSpec((1, HEAD_DIM), lambda b, t: (0, 0))],
        out_specs=act_spec(),
        out_shape=jax.ShapeDtypeStruct((B, T, DN_WIDTH), bf16),
        scratch_shapes=[pltpu.VMEM((nbr * DN_HEADS, HEAD_DIM, HEAD_DIM), f32)],
        compiler_params=_cparams(("arbitrary", "arbitrary")),
        name="deltanet",
    )(q, k, v, z, gcol, grow, norm_w)


def _mixout_body(ydn_ref, u_ref, vln_ref, x_hbm, ws_ref, bsp_ref, wout_ref, g1_ref, b1_ref, wrt_ref, brt_ref,
                 h_ref, hrow_ref, ids_ref, wts_ref, ycat_ref, xring_ref, xsem, *, tm, nsteps):
    C = SGU_CHUNK
    step = pl.program_id(0) * pl.num_programs(1) + pl.program_id(1)
    slot = lax.rem(step, X_SLOTS)

    def x_copy(s, sl):
        return pltpu.make_async_copy(x_hbm.at[s], xring_ref.at[sl], xsem.at[sl])

    @pl.when(step == 0)
    def _():
        for j in range(X_SLOTS - 1):
            x_copy(j, j).start()

    ahead = step + (X_SLOTS - 1)

    @pl.when(ahead < nsteps)
    def _():
        x_copy(ahead, lax.rem(ahead, X_SLOTS)).start()

    x_copy(step, slot).wait()
    ii = _iota2((C, C), 0)
    jj = _iota2((C, C), 1)
    ycat_ref[:, 0:DN_WIDTH] = ydn_ref[0]
    for g in range(SGU_GROUPS):
        gs = slice(g * C, (g + 1) * C)
        wsg = jnp.where(ii >= jj, ws_ref[g], 0.0).astype(bf16)
        for c in range(tm // C):
            rs = slice(c * C, (c + 1) * C)
            mixed = jnp.dot(wsg, vln_ref[0, rs, gs], preferred_element_type=f32) + bsp_ref[:, gs]
            ycat_ref[rs, DN_WIDTH + g * C:DN_WIDTH + (g + 1) * C] = (u_ref[0, rs, gs].astype(f32) * mixed).astype(bf16)

    RB = 128
    blocks = [slice(r, r + RB) for r in range(0, tm, RB)]
    mix = [jnp.dot(ycat_ref[rb, :], wout_ref[...], preferred_element_type=f32) for rb in blocks]
    h1s = []
    for rb, m in zip(blocks, mix):
        hp = DEEPNORM_ALPHA * xring_ref[slot, rb, :] + m
        mu = jnp.mean(hp, axis=-1, keepdims=True)
        hc = hp - mu
        var = jnp.mean(hc * hc, axis=-1, keepdims=True)
        h1 = hc * lax.rsqrt(var + LN_EPS) * g1_ref[...] + b1_ref[...]
        h_ref[0, rb, :] = h1
        h1b = h1.astype(bf16)
        hrow_ref[rb] = h1b.reshape(RB, ROW_TILE, 128)
        h1s.append(h1b)

    logit_blocks = [lax.dot_general(wrt_ref[...], hb, NT_DIMS, preferred_element_type=f32) + brt_ref[...] for hb in h1s]
    sub = _iota2((8, RB), 0)
    subf = sub.astype(f32)
    for rb, logits in zip(blocks, logit_blocks):
        gl = logits[0:8]
        gmax = jnp.max(gl, axis=0, keepdims=True)
        g_idx = jnp.min(jnp.where(gl == gmax, subf, float(MOE_GROUPS)), axis=0, keepdims=True)
        p_group = 1.0 / jnp.sum(jnp.exp(gl - gmax), axis=0, keepdims=True)
        within = jnp.zeros((8, RB), f32)
        for g in range(MOE_GROUPS):
            within = within + jnp.where(g_idx == float(g), logits[8 + 8 * g:16 + 8 * g], 0.0)
        m1 = jnp.max(within, axis=0, keepdims=True)
        i1 = jnp.min(jnp.where(within == m1, subf, float(EXPERTS_PER_GROUP)), axis=0, keepdims=True)
        rest = jnp.where(subf == i1, -jnp.inf, within)
        m2 = jnp.max(rest, axis=0, keepdims=True)
        i2 = jnp.min(jnp.where(rest == m2, subf, float(EXPERTS_PER_GROUP)), axis=0, keepdims=True)
        e = jnp.exp(m2 - m1)
        w1 = p_group / (1.0 + e)
        w2 = p_group * e / (1.0 + e)
        e1 = g_idx * float(EXPERTS_PER_GROUP) + i1
        e2 = g_idx * float(EXPERTS_PER_GROUP) + i2
        ids_ref[:, rb] = jnp.where(sub == 0, e1, jnp.where(sub == 1, e2, 0.0)).astype(i32)
        wts_ref[:, rb] = jnp.where(sub == 0, w1, jnp.where(sub == 1, w2, 0.0))


def _stage_mixout(ydn, u, vln, x, ws, bsp, wout, g1, b1, wrt, brt, *, tm):
    B, T, _ = x.shape
    nt = T // tm
    act_spec = lambda: pl.BlockSpec((1, tm, DN_WIDTH), lambda b, t: (b, t, 0))
    const2 = lambda shp: pl.BlockSpec(shp, lambda b, t: (0, 0))
    tok_spec = lambda: pl.BlockSpec((8, tm), lambda b, t: (0, b * nt + t))
    return pl.pallas_call(
        functools.partial(_mixout_body, tm=tm, nsteps=B * nt),
        grid=(B, nt),
        in_specs=[act_spec(), act_spec(), act_spec(),
                  pl.BlockSpec(memory_space=pl.ANY),
                  pl.BlockSpec((SGU_GROUPS, SGU_CHUNK, SGU_CHUNK), lambda b, t: (0, 0, 0)),
                  const2((SGU_CHUNK, SGU_WIDTH)),
                  const2((D_MODEL, D_MODEL)),
                  const2((1, D_MODEL)), const2((1, D_MODEL)),
                  const2((128, D_MODEL)), const2((128, 128))],
        out_specs=[pl.BlockSpec((1, tm, D_MODEL), lambda b, t: (b, t, 0)),
                   pl.BlockSpec((tm, ROW_TILE, 128), lambda b, t: (b * nt + t, 0, 0)), tok_spec(), tok_spec()],
        out_shape=[jax.ShapeDtypeStruct((B, T, D_MODEL), f32),
                   jax.ShapeDtypeStruct((B * T, ROW_TILE, 128), bf16),
                   jax.ShapeDtypeStruct((8, B * T), i32),
                   jax.ShapeDtypeStruct((8, B * T), f32)],
        scratch_shapes=[pltpu.VMEM((tm, D_MODEL), bf16), pltpu.VMEM((X_SLOTS, tm, D_MODEL), f32),
                        pltpu.SemaphoreType.DMA((X_SLOTS,))],
        compiler_params=_cparams(("arbitrary", "arbitrary")),
        name="mixout",
    )(ydn, u, vln, x.reshape(B * nt, tm, D_MODEL), ws, bsp, wout, g1, b1, wrt, brt)


def _route_body(ids_ref, dest_ref, meta_ref, blk_ref, base_ref, pstart_ref, *, tm, nb_pad):
    phase = pl.program_id(0)
    i = pl.program_id(1)
    sub = _iota2((N_EXPERTS, tm), 0)
    is1 = sub == ids_ref[0:1, :]
    is2 = sub == ids_ref[1:2, :]
    oh = (is1.astype(f32) + is2.astype(f32)).astype(bf16)
    counts = jnp.dot(oh, jnp.ones((tm, 128), bf16), preferred_element_type=f32)

    @pl.when((phase == 0) & (i == 0))
    def _():
        base_ref[...] = jnp.zeros(base_ref.shape, f32)

    @pl.when(phase == 0)
    def _():
        base_ref[...] = base_ref[...] + counts

    @pl.when((phase == 1) & (i == 0))
    def _():
        cnt = base_ref[...]
        padded = jnp.floor((cnt + (MOE_BLOCK - 1)) * (1.0 / MOE_BLOCK)) * MOE_BLOCK
        ei = _iota2((N_EXPERTS, N_EXPERTS), 0)
        ej = _iota2((N_EXPERTS, N_EXPERTS), 1)
        pends = jnp.dot((ei >= ej).astype(f32), padded, precision=HIGHEST, preferred_element_type=f32)
        pstart = pends - padded
        pstart_ref[...] = pstart
        s64 = _iota2((N_EXPERTS, 128), 0)
        l64 = _iota2((N_EXPERTS, 128), 1)
        diag = s64 == l64
        fill_off = jnp.sum(jnp.where(diag, pstart + cnt, 0.0), axis=0, keepdims=True)
        fill_n = jnp.sum(jnp.where(diag, padded - cnt, 0.0), axis=0, keepdims=True)
        nused = pends[N_EXPERTS - 1:N_EXPERTS, :] * (1.0 / MOE_BLOCK)
        m8 = _iota2((8, 128), 0)
        meta_ref[...] = jnp.where(m8 == 0, fill_off, jnp.where(m8 == 1, fill_n, jnp.where(m8 == 2, nused, 0.0))).astype(i32)
        bstart = (_iota2((N_EXPERTS, nb_pad), 1) * MOE_BLOCK).astype(f32)
        pe = jnp.concatenate([pends] * (nb_pad // 128), axis=1)
        be = jnp.sum((pe <= bstart).astype(f32), axis=0, keepdims=True)
        be = jnp.minimum(be, float(N_EXPERTS - 1))
        blk_ref[...] = jnp.broadcast_to(be, (8, nb_pad)).astype(i32)

    @pl.when(phase == 1)
    def _():
        ti = _iota2((tm, tm), 0)
        tj = _iota2((tm, tm), 1)
        before = (ti < tj).astype(bf16)
        prefix = jnp.dot(oh, before, preferred_element_type=f32)
        nxt = prefix + jnp.concatenate([pstart_ref[...]] * (tm // 128), axis=1)
        d1 = jnp.sum(jnp.where(is1, nxt, 0.0), axis=0, keepdims=True)
        d2 = jnp.sum(jnp.where(is2, nxt, 0.0), axis=0, keepdims=True)
        sub8 = _iota2((8, tm), 0)
        dest_ref[...] = jnp.where(sub8 == 0, d1, jnp.where(sub8 == 1, d2, 0.0)).astype(i32)
        pstart_ref[...] = pstart_ref[...] + counts


def _stage_route(ids, *, tm, nb_pad):
    n = ids.shape[1]
    return pl.pallas_call(
        functools.partial(_route_body, tm=tm, nb_pad=nb_pad),
        grid=(2, n // tm),
        in_specs=[pl.BlockSpec((8, tm), lambda p, i: (0, i))],
        out_specs=[pl.BlockSpec((8, tm), lambda p, i: (0, i * p)),
                   pl.BlockSpec((8, 128), lambda p, i: (0, 0)),
                   pl.BlockSpec((8, nb_pad), lambda p, i: (0, 0))],
        out_shape=[jax.ShapeDtypeStruct((8, n), i32), jax.ShapeDtypeStruct((8, 128), i32),
                   jax.ShapeDtypeStruct((8, nb_pad), i32)],
        scratch_shapes=[pltpu.VMEM((N_EXPERTS, 128), f32), pltpu.VMEM((N_EXPERTS, 128), f32)],
        compiler_params=_cparams(("arbitrary", "arbitrary")),
        name="moe_route",
    )(ids)


def _dispatch_body(fill_off_ref, fill_n_ref, nused_ref, dest_ref, h3_ref, xs_ref, zero_ref, sem, zsem, *, tm):
    def row_copy(t, d):
        return pltpu.make_async_copy(h3_ref.at[t], xs_ref.at[d], sem)

    def issue(t, carry):
        row_copy(t, dest_ref[0, 0, t]).start(priority=0)
        row_copy(t, dest_ref[0, 1, t]).start(priority=1)
        return carry

    lax.fori_loop(0, tm, issue, 0, unroll=8)

    @pl.when(pl.program_id(0) == 0)
    def _():
        zero_ref[...] = jnp.zeros(zero_ref.shape, bf16)

        def fill(start):
            def body(e, carry):
                off = fill_off_ref[e]
                npad = fill_n_ref[e]
                bit = MOE_BLOCK // 2
                while bit:
                    @pl.when((npad & bit) != 0)
                    def _(off=off, bit=bit):
                        cp = pltpu.make_async_copy(zero_ref.at[pl.ds(0, bit)], xs_ref.at[pl.ds(off, bit)], zsem)
                        cp.start() if start else cp.wait()
                    off = off + (npad & bit)
                    bit //= 2
                return carry
            return body

        lax.fori_loop(0, N_EXPERTS, fill(True), 0)
        lax.fori_loop(0, N_EXPERTS, fill(False), 0)

        def tail_copy(b):
            return pltpu.make_async_copy(zero_ref, xs_ref.at[pl.ds(b * MOE_BLOCK, MOE_BLOCK)], zsem)

        nblocks = xs_ref.shape[0] // MOE_BLOCK
        lax.fori_loop(nused_ref[0], nblocks, lambda b, c: (tail_copy(b).start(), c)[1], 0)
        lax.fori_loop(nused_ref[0], nblocks, lambda b, c: (tail_copy(0).wait(), c)[1], 0)

    for _ in range(2):
        pltpu.make_async_copy(h3_ref, xs_ref.at[pl.ds(0, tm)], sem).wait()


def _stage_dispatch(fill_off, fill_n, nused, dest3, hrow, p_rows, *, tm):
    n = hrow.shape[0]
    return pl.pallas_call(
        functools.partial(_dispatch_body, tm=tm),
        grid_spec=pltpu.PrefetchScalarGridSpec(
            num_scalar_prefetch=3,
            grid=(n // tm,),
            in_specs=[pl.BlockSpec((1, 2, tm), lambda i, fo, fn, nu: (i, 0, 0), memory_space=pltpu.SMEM),
                      pl.BlockSpec((tm, ROW_TILE, 128), lambda i, fo, fn, nu: (i, 0, 0))],
            out_specs=pl.BlockSpec(memory_space=pl.ANY),
            scratch_shapes=[pltpu.VMEM((MOE_BLOCK, ROW_TILE, 128), bf16),
                            pltpu.SemaphoreType.DMA, pltpu.SemaphoreType.DMA],
        ),
        out_shape=jax.ShapeDtypeStruct((p_rows, ROW_TILE, 128), bf16),
        compiler_params=_cparams(("arbitrary",)),
        name="moe_dispatch",
    )(fill_off, fill_n, nused, dest3, hrow)


def _experts_body(blk_ref, nused_ref, xs_hbm, wg_hbm, wu_hbm, wd_hbm, ys_ref,
                  xbuf_ref, wg32_ref, wu32_ref, wd32_ref, wgu16_ref, wd16_ref, xsem, wsem):
    i = pl.program_id(0)
    nused = nused_ref[0]
    used = i < nused
    e = blk_ref[i]
    slot = lax.rem(i, X_SLOTS)

    def weight_copies(ex):
        return (pltpu.make_async_copy(wg_hbm.at[ex], wg32_ref, wsem.at[0]),
                pltpu.make_async_copy(wu_hbm.at[ex], wu32_ref, wsem.at[1]),
                pltpu.make_async_copy(wd_hbm.at[ex], wd32_ref, wsem.at[2]))

    def x_copy(block, s):
        return pltpu.make_async_copy(xs_hbm.at[pl.ds(block * MOE_BLOCK, MOE_BLOCK)], xbuf_ref.at[s], xsem.at[s])

    @pl.when((i == 0) & used)
    def _():
        for cp in weight_copies(e):
            cp.start()
        for j in range(X_SLOTS - 1):
            @pl.when(j < nused)
            def _(j=j):
                x_copy(j, j).start()

    ahead = i + (X_SLOTS - 1)

    @pl.when(ahead < nused)
    def _():
        x_copy(ahead, lax.rem(ahead, X_SLOTS)).start()

    @pl.when(used & ((i == 0) | (e != blk_ref[jnp.maximum(i - 1, 0)])))
    def _():
        for cp in weight_copies(e):
            cp.wait()
        wgu16_ref[:, 0:D_EXPERT] = wg32_ref[...].astype(bf16)
        wgu16_ref[:, D_EXPERT:2 * D_EXPERT] = wu32_ref[...].astype(bf16)
        wd16_ref[...] = wd32_ref[...].astype(bf16)
        nxt = lax.while_loop(lambda j: (j < nused) & (blk_ref[jnp.minimum(j, nused - 1)] == e), lambda j: j + 1, i + 1)

        @pl.when(nxt < nused)
        def _():
            for cp in weight_copies(blk_ref[jnp.minimum(nxt, nused - 1)]):
                cp.start(priority=1)

    @pl.when(used)
    def _():
        x_copy(i, slot).wait()
        half = MOE_BLOCK // 2
        rows = [slice(p * half, (p + 1) * half) for p in range(2)]
        gu = [jnp.dot(xbuf_ref[slot, r].reshape(half, D_MODEL), wgu16_ref[...], preferred_element_type=f32)
              for r in rows]
        hid = [(_silu(g[:, :D_EXPERT]) * g[:, D_EXPERT:]).astype(bf16) for g in gu]
        y = [jnp.dot(hd, wd16_ref[...], preferred_element_type=f32) for hd in hid]
        for r, yp in zip(rows, y):
            ys_ref[r] = yp.astype(bf16).reshape(half, ROW_TILE, 128)

    @pl.when(jnp.logical_not(used))
    def _():
        ys_ref[...] = jnp.zeros(ys_ref.shape, bf16)


def _stage_experts(blk_e, nused, xs, w_gate, w_up, w_down):
    p_rows = xs.shape[0]
    nb = p_rows // MOE_BLOCK

    def row_map(i, blk, nu):
        return (i, 0, 0)

    return pl.pallas_call(
        _experts_body,
        grid_spec=pltpu.PrefetchScalarGridSpec(
            num_scalar_prefetch=2,
            grid=(nb,),
            in_specs=[pl.BlockSpec(memory_space=pl.ANY),
                      pl.BlockSpec(memory_space=pl.ANY),
                      pl.BlockSpec(memory_space=pl.ANY),
                      pl.BlockSpec(memory_space=pl.ANY)],
            out_specs=pl.BlockSpec((MOE_BLOCK, ROW_TILE, 128), row_map),
            scratch_shapes=[pltpu.VMEM((X_SLOTS, MOE_BLOCK, ROW_TILE, 128), bf16),
                            pltpu.VMEM((D_MODEL, D_EXPERT), f32), pltpu.VMEM((D_MODEL, D_EXPERT), f32),
                            pltpu.VMEM((D_EXPERT, D_MODEL), f32),
                            pltpu.VMEM((D_MODEL, 2 * D_EXPERT), bf16), pltpu.VMEM((D_EXPERT, D_MODEL), bf16),
                            pltpu.SemaphoreType.DMA((X_SLOTS,)), pltpu.SemaphoreType.DMA((3,))],
        ),
        out_shape=jax.ShapeDtypeStruct((p_rows, ROW_TILE, 128), bf16),
        compiler_params=_cparams(("arbitrary",)),
        name="moe_experts",
    )(blk_e, nused, xs, w_gate, w_up, w_down)


def _combine_body(dcur_ref, dnext_ref, h_ref, wts_ref, g2_ref, b2_ref, ys_ref, o_ref, ybuf_ref, sem, *, tm, nsteps):
    i = pl.program_id(0)
    slot = lax.rem(i, 2)

    def issue_tile(d_ref, s):
        def body(t, carry):
            for k in range(2):
                pltpu.make_async_copy(ys_ref.at[d_ref[0, k, t]], ybuf_ref.at[s, k, t], sem.at[s]).start(priority=k)
            return carry

        lax.fori_loop(0, tm, body, 0, unroll=8)

    @pl.when(i == 0)
    def _():
        issue_tile(dcur_ref, 0)

    @pl.when(i + 1 < nsteps)
    def _():
        issue_tile(dnext_ref, 1 - slot)

    for k in range(2):
        pltpu.make_async_copy(ys_ref.at[pl.ds(0, tm)], ybuf_ref.at[slot, k], sem.at[slot]).wait()

    pieces = []
    for c in range(tm // 128):
        ls = slice(c * 128, (c + 1) * 128)
        w1c = jnp.broadcast_to(wts_ref[0:1, ls], (128, 128)).T
        w2c = jnp.broadcast_to(wts_ref[1:2, ls], (128, 128)).T
        w1f = jnp.concatenate([w1c] * (D_MODEL // 128), axis=1)
        w2f = jnp.concatenate([w2c] * (D_MODEL // 128), axis=1)
        y1 = ybuf_ref[slot, 0, ls].reshape(128, D_MODEL).astype(f32)
        y2 = ybuf_ref[slot, 1, ls].reshape(128, D_MODEL).astype(f32)
        pieces.append(w1f * y1 + w2f * y2)
    ffn = jnp.concatenate(pieces, axis=0)
    hp = DEEPNORM_ALPHA * h_ref[...] + ffn
    mu = jnp.mean(hp, axis=-1, keepdims=True)
    hc = hp - mu
    var = jnp.mean(hc * hc, axis=-1, keepdims=True)
    o_ref[...] = hc * lax.rsqrt(var + LN_EPS) * g2_ref[...] + b2_ref[...]


def _stage_combine(dest3, h2, wts, g2, b2, ys, *, tm):
    n = h2.shape[0]
    nsteps = n // tm
    return pl.pallas_call(
        functools.partial(_combine_body, tm=tm, nsteps=nsteps),
        grid=(nsteps,),
        in_specs=[pl.BlockSpec((1, 2, tm), lambda i: (i, 0, 0), memory_space=pltpu.SMEM),
                  pl.BlockSpec((1, 2, tm), lambda i: (jnp.minimum(i + 1, nsteps - 1), 0, 0), memory_space=pltpu.SMEM),
                  pl.BlockSpec((tm, D_MODEL), lambda i: (i, 0)),
                  pl.BlockSpec((8, tm), lambda i: (0, i)),
                  pl.BlockSpec((1, D_MODEL), lambda i: (0, 0)),
                  pl.BlockSpec((1, D_MODEL), lambda i: (0, 0)),
                  pl.BlockSpec(memory_space=pl.ANY)],
        out_specs=pl.BlockSpec((tm, D_MODEL), lambda i: (i, 0)),
        out_shape=jax.ShapeDtypeStruct((n, D_MODEL), f32),
        scratch_shapes=[pltpu.VMEM((2, 2, tm, ROW_TILE, 128), bf16), pltpu.SemaphoreType.DMA((2,))],
        compiler_params=_cparams(("arbitrary",)),
        name="moe_combine",
    )(dest3, dest3, h2, wts, g2, b2, ys)


def _layer(h, w_in, conv_w, a_log, dt_bias, dn_norm_w, sgu_ln_g, sgu_ln_b, w_spatial, b_spatial, w_out,
           ln1_g, ln1_b, w_rg, b_rg, w_re, b_re, w_gate, w_up, w_down, ln2_g, ln2_b,
           *, tm_in, in_groups, dn_rows, dn_chunks, dn_groups, tm_mix, tm_rank, tm_disp, tm_comb):
    B, T, _ = h.shape
    n = B * T
    qkvz = 4 * DN_WIDTH
    ba0 = qkvz
    uv0 = qkvz + 2 * DN_HEADS
    w_cols = jnp.zeros((D_MODEL, IN_COLS_ALIGNED), bf16)
    w_cols = lax.dynamic_update_slice(w_cols, w_in[:, :qkvz].astype(bf16), (0, 0))
    w_cols = lax.dynamic_update_slice(w_cols, w_in[:, uv0:].astype(bf16), (0, qkvz))
    w_cols = lax.dynamic_update_slice(w_cols, w_in[:, ba0:uv0].astype(bf16), (0, qkvz + 2 * SGU_WIDTH))
    decay_prm = jnp.stack([a_log, dt_bias])
    prow =jnp.broadcast_to(jnp.pad(decay_prm, ((0, 0), (DN_HEADS, 8 - 2 * DN_HEADS)))[:, :, None], (2, 8, 128))

    q, k, v, z, u, vln, gcol, grow = _stage_inproj(
        h, w_cols, conv_w, prow, sgu_ln_g[None, :], sgu_ln_b[None, :], tm=tm_in, ngroups=in_groups)
    ydn = _stage_deltanet(q, k, v, z, gcol, grow, dn_norm_w[None, :], nbr=dn_rows, nch=dn_chunks, ngroups=dn_groups)

    bsp = jnp.broadcast_to(b_spatial.T[:, :, None], (SGU_CHUNK, SGU_GROUPS, SGU_CHUNK)).reshape(SGU_CHUNK, SGU_WIDTH)
    n_logit = MOE_GROUPS + N_EXPERTS
    wrt = jnp.pad(jnp.concatenate([w_rg, w_re], axis=1).T, ((0, 128 - n_logit), (0, 0))).astype(bf16)
    brt = jnp.broadcast_to(jnp.pad(jnp.concatenate([b_rg, b_re]), (0, 128 - n_logit))[:, None], (128, 128))
    h1, hrow, ids, wts = _stage_mixout(ydn, u, vln, h, w_spatial, bsp, w_out.astype(bf16), ln1_g[None, :],
                                       ln1_b[None, :], wrt, brt, tm=tm_mix)

    p_rows = (-(-(n * 2) // MOE_BLOCK)) * MOE_BLOCK + N_EXPERTS * MOE_BLOCK
    nb = p_rows // MOE_BLOCK
    nb_pad = (-(-nb // 128)) * 128
    dest, meta, blk = _stage_route(ids, tm=tm_rank, nb_pad=nb_pad)

    h2 = h1.reshape(n, D_MODEL)
    dest_d = dest[0:2].reshape(2, n // tm_disp, tm_disp).transpose(1, 0, 2)
    xs = _stage_dispatch(meta[0, :N_EXPERTS], meta[1, :N_EXPERTS], meta[2, 0:1], dest_d, hrow, p_rows, tm=tm_disp)
    ys = _stage_experts(blk[0, :nb], meta[2, 0:1], xs, w_gate, w_up, w_down)
    dest_c = dest[0:2].reshape(2, n // tm_comb, tm_comb).transpose(1, 0, 2)
    out = _stage_combine(dest_c, h2, wts, ln2_g[None, :], ln2_b[None, :], ys, tm=tm_comb)
    return out.reshape(B, T, D_MODEL)


def kernel(x, w_in, conv_w, a_log, dt_bias, dn_norm_w, sgu_ln_g, sgu_ln_b, w_spatial, b_spatial, w_out, ln1_g, ln1_b, w_router_group, b_router_group, w_router_expert, b_router_expert, w_gate, w_up, w_down, ln2_g, ln2_b):
    h = x
    for l in range(w_in.shape[0]):
        h = _layer(h, w_in[l], conv_w[l], a_log[l], dt_bias[l], dn_norm_w[l], sgu_ln_g[l], sgu_ln_b[l],
                   w_spatial[l], b_spatial[l], w_out[l], ln1_g[l], ln1_b[l],
                   w_router_group[l], b_router_group[l], w_router_expert[l], b_router_expert[l],
                   w_gate[l], w_up[l], w_down[l], ln2_g[l], ln2_b[l],
                   tm_in=512, in_groups=1, dn_rows=8, dn_chunks=2, dn_groups=4, tm_mix=512, tm_rank=1024, tm_disp=2048, tm_comb=512)
    return h
```

```python
import functools

import jax
import jax.numpy as jnp
from jax import lax
from jax.experimental import pallas as pl
from jax.experimental.pallas import tpu as pltpu

f32 = jnp.float32
bf16 = jnp.bfloat16
i32 = jnp.int32

D_MODEL = 1024
DN_WIDTH = 512
DN_HEADS = 4
HEAD_DIM = 128
CONV_K = 4
SGU_WIDTH = 512
SGU_GROUPS = 4
SGU_CHUNK = 128
DN_CHUNK = 128
MOE_GROUPS = 8
EXPERTS_PER_GROUP = 8
N_EXPERTS = 64
D_EXPERT = 512
MOE_BLOCK = 256
IN_COLS_ALIGNED = 4 * DN_WIDTH + 2 * SGU_WIDTH + 128
X_SLOTS = 4
ROW_TILE = D_MODEL // 128
DEEPNORM_ALPHA = 2.0 ** 0.25
LN_EPS = 1e-5
RMS_EPS = 1e-6
HIGHEST = lax.Precision.HIGHEST
VMEM_LIMIT_BYTES = 56 * 1024 * 1024

NT_DIMS = (((1,), (1,)), ((), ()))


def _cparams(sem, flags=None):
    return pltpu.CompilerParams(dimension_semantics=sem, vmem_limit_bytes=VMEM_LIMIT_BYTES, flags=flags)


def _sigmoid(x):
    return 1.0 / (1.0 + jnp.exp(-x))


def _silu(x):
    h = 0.5 * x
    return h + h * jnp.tanh(h)


def _softplus(x):
    return jnp.maximum(x, 0.0) + jnp.log1p(jnp.exp(-jnp.abs(x)))


def _gelu_tanh(x):
    c = 0.7978845608028654
    return x * (0.5 * (1.0 + jnp.tanh(c * (x + 0.044715 * (x * x * x)))))


def _iota2(shape, axis):
    return lax.broadcasted_iota(i32, shape, axis)


def _inproj_body(x_ref, w_ref, convw_ref, prow_ref, lng_ref, lnb_ref, ones_ref,
                 q_ref, k_ref, v_ref, z_ref, u_ref, vln_ref, gcol_ref, grow_ref, *ext_refs, tm, ngroups):
    W = DN_WIDTH
    gm = tm // ngroups
    ext = [ext_refs[3 * g:3 * g + 3] for g in range(ngroups)]

    @pl.when(pl.program_id(1) == 0)
    def _():
        for e_ref in ext[0]:
            e_ref[0:8, :] = jnp.zeros((8, W), f32)

    for g in range(ngroups):
        _inproj_rows(x_ref, w_ref, convw_ref, prow_ref, lng_ref, lnb_ref, ones_ref,
                     q_ref, k_ref, v_ref, z_ref, u_ref, vln_ref, gcol_ref, grow_ref, ext[g],
                     ext[(g + 1) % ngroups], r0=g * gm, gm=gm)


def _inproj_rows(x_ref, w_ref, convw_ref, prow_ref, lng_ref, lnb_ref, ones_ref,
                 q_ref, k_ref, v_ref, z_ref, u_ref, vln_ref, gcol_ref, grow_ref, ext, ext_next, *, r0, gm):
    W = DN_WIDTH
    rows = slice(r0, r0 + gm)
    xb = x_ref[0, rows, :].astype(bf16)
    for part, e_ref in enumerate(ext):
        e_ref[8:8 + gm, :] = jnp.dot(xb, w_ref[:, part * W:(part + 1) * W], preferred_element_type=f32)
    zc = 3 * W
    uc = zc + W
    vc = uc + SGU_WIDTH
    bc = vc + SGU_WIDTH

    def conv_silu(part):
        e_ref = ext[part]
        cs = slice(part * W, (part + 1) * W)
        y = convw_ref[3:4, cs] * e_ref[8:8 + gm, :]
        for j in range(CONV_K - 1):
            y = y + convw_ref[j:j + 1, cs] * e_ref[5 + j:5 + j + gm, :]
        ext_next[part][0:8, :] = e_ref[gm:gm + 8, :]
        return _silu(y)

    def group_sums(a):
        return jnp.dot(a.astype(bf16), ones_ref[...], preferred_element_type=f32)

    yq = conv_silu(0)
    ssq = group_sums(yq * yq)
    pba = jnp.dot(xb, w_ref[:, bc:bc + 128], preferred_element_type=f32)
    pv = jnp.dot(xb, w_ref[:, vc:vc + SGU_WIDTH], preferred_element_type=f32)
    yk = conv_silu(1)
    ssk = group_sums(yk * yk)
    pu = jnp.dot(xb, w_ref[:, uc:uc + SGU_WIDTH], preferred_element_type=f32)
    pz = jnp.dot(xb, w_ref[:, zc:zc + W], preferred_element_type=f32)
    q_ref[0, rows, :] = (yq * (lax.rsqrt(ssq + RMS_EPS) * HEAD_DIM ** -0.5)).astype(bf16)
    k_ref[0, rows, :] = (yk * lax.rsqrt(ssk + RMS_EPS)).astype(bf16)
    v_ref[0, rows, :] = conv_silu(2).astype(bf16)

    z_ref[0, rows, :] = _silu(pz).astype(bf16)

    u_ref[0, rows, :] = _gelu_tanh(pu).astype(bf16)
    pv = _gelu_tanh(pv)
    for g in range(SGU_GROUPS):
        sl = slice(g * SGU_CHUNK, (g + 1) * SGU_CHUNK)
        vg = pv[:, sl]
        mu = jnp.mean(vg, axis=-1, keepdims=True)
        vcn = vg - mu
        var = jnp.mean(vcn * vcn, axis=-1, keepdims=True)
        vln_ref[0, rows, sl] = (vcn * lax.rsqrt(var + LN_EPS) * lng_ref[:, sl] + lnb_ref[:, sl]).astype(bf16)

    lane = _iota2((DN_CHUNK, 128), 1)
    beta = _sigmoid(pba)
    lane8 = _iota2((8, DN_CHUNK), 1)
    sub8 = _iota2((8, DN_CHUNK), 0)
    for c in range(gm // DN_CHUNK):
        rs = slice(c * DN_CHUNK, (c + 1) * DN_CHUNK)
        os_ = slice(r0 + c * DN_CHUNK, r0 + (c + 1) * DN_CHUNK)
        pbat = pba[rs].T[0:8, :]
        gt = -jnp.exp(prow_ref[0]) * _softplus(pbat + prow_ref[1])
        gc = jnp.where(sub8 >= DN_HEADS, gt, 0.0)
        shift = 1
        while shift < DN_CHUNK:
            gc = gc + jnp.where(lane8 >= shift, pltpu.roll(gc, shift, axis=1), 0.0)
            shift *= 2
        grow_ref[0, :, os_] = gc
        gc_col = jnp.concatenate([gc, jnp.zeros((DN_CHUNK - 8, DN_CHUNK), f32)], axis=0).T
        gcol_ref[0, os_, :] = jnp.where(lane < DN_HEADS, beta[rs], gc_col)


def _stage_inproj(x, w_re, conv_w, prow, lng, lnb, *, tm, ngroups):
    B, T, _ = x.shape
    wcols = w_re.shape[1]
    grid = (B, T // tm)
    gi = lax.broadcasted_iota(i32, (DN_WIDTH, DN_WIDTH), 0) // 128
    gj = lax.broadcasted_iota(i32, (DN_WIDTH, DN_WIDTH), 1) // 128
    group_ones = (gi == gj).astype(bf16)
    act = lambda: jax.ShapeDtypeStruct((B, T, DN_WIDTH), bf16)
    act_spec = lambda: pl.BlockSpec((1, tm, DN_WIDTH), lambda b, t: (b, t, 0))
    const2 = lambda shp: pl.BlockSpec(shp, lambda b, t: (0, 0))
    return pl.pallas_call(
        functools.partial(_inproj_body, tm=tm, ngroups=ngroups),
        grid=grid,
        in_specs=[
            pl.BlockSpec((1, tm, D_MODEL), lambda b, t: (b, t, 0)),
            const2((D_MODEL, wcols)),
            const2((CONV_K, 3 * DN_WIDTH)),
            pl.BlockSpec((2, 8, 128), lambda b, t: (0, 0, 0)),
            const2((1, SGU_WIDTH)),
            const2((1, SGU_WIDTH)),
            const2((DN_WIDTH, DN_WIDTH)),
        ],
        out_specs=[act_spec() for _ in range(6)] + [
            pl.BlockSpec((1, tm, 128), lambda b, t: (b, t, 0)),
            pl.BlockSpec((1, 8, tm), lambda b, t: (b, 0, t)),
        ],
        out_shape=[act() for _ in range(6)] + [
            jax.ShapeDtypeStruct((B, T, 128), f32),
            jax.ShapeDtypeStruct((B, 8, T), f32),
        ],
        scratch_shapes=[pltpu.VMEM((tm // ngroups + 8, DN_WIDTH), f32) for _ in range(3 * ngroups)],
        compiler_params=_cparams(("arbitrary", "arbitrary")),
        name="inproj",
    )(x, w_re, conv_w, prow, lng, lnb, group_ones)


def _mm(a, b):
    return jnp.dot(a.astype(bf16), b.astype(bf16), preferred_element_type=f32)


def _unit_lower_inverse(nmats, ii, jj):
    n = nmats[0].shape[0]
    eye = (ii == jj).astype(f32)
    leaf = jnp.right_shift(ii, 3) == jnp.right_shift(jj, 3)
    dblk = [jnp.where(leaf, m, 0.0) for m in nmats]
    s1 = [_mm(d, d) for d in dblk]
    r1 = [eye - d for d in dblk]
    both = [_mm(s, jnp.concatenate([s, r], axis=1)) for s, r in zip(s1, r1)]
    r2 = [r + bo[:, n:] for r, bo in zip(r1, both)]
    xs = [r + _mm(bo[:, :n], r) for r, bo in zip(r2, both)]
    size = 8
    while size < n:
        lows = [slice(r + size, r + 2 * size) for r in range(0, n, 2 * size)]
        ups = [slice(r, r + size) for r in range(0, n, 2 * size)]
        rsel = _iota2((n // 2, n), 0)
        ilow = rsel + size * (jnp.right_shift(rsel, size.bit_length() - 1) + 1)
        jlow = _iota2((n // 2, n), 1)
        in_pair_upper = (jlow >= ilow - (ilow & (2 * size - 1))) & (jlow < ilow - (ilow & (size - 1)))
        zeros = jnp.zeros((size, n), f32)
        new_xs = []
        ylows = [_mm(jnp.where(in_pair_upper, jnp.concatenate([m[s] for s in lows], axis=0), 0.0), x)
                 for m, x in zip(nmats, xs)]
        yfull = [jnp.concatenate([piece for k in range(len(lows)) for piece in (zeros, y[k * size:(k + 1) * size])], axis=0)
                 for y in ylows]
        corr = [_mm(jnp.concatenate([x[s] for s in lows], axis=0), yf) for x, yf in zip(xs, yfull)]
        for x, c in zip(xs, corr):
            pieces = []
            for k, (u, l) in enumerate(zip(ups, lows)):
                pieces += [x[u], x[l] - c[k * size:(k + 1) * size]]
            new_xs.append(jnp.concatenate(pieces, axis=0))
        xs = new_xs
        size *= 2
    return xs


def _deltanet_body(q_ref, k_ref, v_ref, z_ref, gcol_ref, grow_ref, nw_ref, y_ref, s_ref, *, nbr, nch, ngroups):
    @pl.when(pl.program_id(1) == 0)
    def _():
        s_ref[...] = jnp.zeros(s_ref.shape, f32)

    per = nbr // ngroups
    for g in range(ngroups):
        _deltanet_rows(q_ref, k_ref, v_ref, z_ref, gcol_ref, grow_ref, nw_ref, y_ref, s_ref,
                       rows=range(g * per, (g + 1) * per), nch=nch)


def _deltanet_rows(q_ref, k_ref, v_ref, z_ref, gcol_ref, grow_ref, nw_ref, y_ref, s_ref, *, rows, nch):
    C = DN_CHUNK
    S = [(b, h) for b in rows for h in range(DN_HEADS)]
    P = [(b, c, h) for c in range(nch) for b, h in S]

    ii = _iota2((C, C), 0)
    jj = _iota2((C, C), 1)
    causal = ii >= jj
    rs = [slice(c * C, (c + 1) * C) for c in range(nch)]
    hs = [slice(h * HEAD_DIM, (h + 1) * HEAD_DIM) for h in range(DN_HEADS)]
    gcol = {(b, c): gcol_ref[b, rs[c], :] for b in rows for c in range(nch)}
    qh = {(b, c, h): q_ref[b, rs[c], hs[h]] for b, c, h in P}
    kh = {(b, c, h): k_ref[b, rs[c], hs[h]] for b, c, h in P}
    vh = {(b, c, h): v_ref[b, rs[c], hs[h]] for b, c, h in P}
    gc_b = {(b, c, h): jnp.broadcast_to(gcol[b, c][:, DN_HEADS + h:DN_HEADS + h + 1], (C, HEAD_DIM)) for b, c, h in P}
    beta_b = {(b, c, h): jnp.broadcast_to(gcol[b, c][:, h:h + 1], (C, HEAD_DIM)) for b, c, h in P}
    gc_r = {(b, c, h): jnp.broadcast_to(grow_ref[b, DN_HEADS + h:DN_HEADS + h + 1, rs[c]], (C, C)) for b, c, h in P}
    decay = {p: jnp.exp(jnp.where(causal, gc_b[p] - gc_r[p], -1e30)) for p in P}

    kf = {p: kh[p].astype(f32) for p in P}
    kb = {p: kf[p] * beta_b[p] for p in P}
    kk = {p: lax.dot_general(kb[p].astype(bf16), kh[p], NT_DIMS, preferred_element_type=f32) for p in P}
    a_intra = {p: lax.dot_general(qh[p], kh[p], NT_DIMS, preferred_element_type=f32) * decay[p] for p in P}
    nmat = [jnp.where(ii > jj, kk[p] * decay[p], 0.0) for p in P]
    tinv = dict(zip(P, _unit_lower_inverse(nmat, ii, jj)))

    eg = {p: jnp.exp(gc_b[p]) for p in P}
    rhs = {p: jnp.concatenate([vh[p].astype(f32) * beta_b[p], kb[p] * eg[p]], axis=1) for p in P}
    sol = {p: _mm(tinv[p], rhs[p]) for p in P}

    q_dec = {p: qh[p].astype(f32) * eg[p] for p in P}
    g_last = {p: gc_b[p][C - 1:C, :] for p in P}
    kdt = {p: (kf[p] * jnp.exp(g_last[p] - gc_b[p])).T for p in P}

    state = {(b, h): s_ref[b * DN_HEADS + h] for b, h in S}
    for c in range(nch):
        m1 = {(b, h): _mm(jnp.concatenate([sol[b, c, h][:, HEAD_DIM:], q_dec[b, c, h]], axis=0), state[b, h]) for b, h in S}
        v_new = {(b, h): sol[b, c, h][:, :HEAD_DIM] - m1[b, h][:C] for b, h in S}
        m2 = {(b, h): _mm(jnp.concatenate([a_intra[b, c, h], kdt[b, c, h]], axis=0), v_new[b, h]) for b, h in S}
        state = {(b, h): state[b, h] * jnp.exp(g_last[b, c, h]) + m2[b, h][C:] for b, h in S}
        for b, h in S:
            o = m1[b, h][C:] + m2[b, h][:C]
            rms = lax.rsqrt(jnp.mean(o * o, axis=-1, keepdims=True) + RMS_EPS)
            y_ref[b, rs[c], hs[h]] = (o * rms * nw_ref[...] * z_ref[b, rs[c], hs[h]].astype(f32)).astype(bf16)
    for b, h in S:
        s_ref[b * DN_HEADS + h] = state[b, h]


def _stage_deltanet(q, k, v, z, gcol, grow, norm_w, *, nbr, nch, ngroups):
    B, T, _ = q.shape
    tt = nch * DN_CHUNK
    act_spec = lambda: pl.BlockSpec((nbr, tt, DN_WIDTH), lambda b, t: (b, t, 0))
    return pl.pallas_call(
        functools.partial(_deltanet_body, nbr=nbr, nch=nch, ngroups=ngroups),
        grid=(B // nbr, T // tt),
        in_specs=[act_spec(), act_spec(), act_spec(), act_spec(),
                  pl.BlockSpec((nbr, tt, 128), lambda b, t: (b, t, 0)),
                  pl.BlockSpec((nbr, 8, tt), lambda b, t: (b, 0, t)),
                  pl.BlockSpec((1, HEAD_DIM), lambda b, t: (0, 0))],
        out_specs=act_spec(),
        out_shape=jax.ShapeDtypeStruct((B, T, DN_WIDTH), bf16),
        scratch_shapes=[pltpu.VMEM((nbr * DN_HEADS, HEAD_DIM, HEAD_DIM), f32)],
        compiler_params=_cparams(("arbitrary", "arbitrary")),
        name="deltanet",
    )(q, k, v, z, gcol, grow, norm_w)


def _mixout_body(ydn_ref, u_ref, vln_ref, x_hbm, ws_ref, bsp_ref, wout_ref, g1_ref, b1_ref, wrt_ref, brt_ref,
                 h_ref, hrow_ref, ids_ref, wts_ref, cnt_ref, ycat_ref, xring_ref, xsem, *, tm, nsteps):
    C = SGU_CHUNK
    step = pl.program_id(0) * pl.num_programs(1) + pl.program_id(1)
    slot = lax.rem(step, X_SLOTS)

    def x_copy(s, sl):
        return pltpu.make_async_copy(x_hbm.at[s], xring_ref.at[sl], xsem.at[sl])

    @pl.when(step == 0)
    def _():
        cnt_ref[...] = jnp.zeros(cnt_ref.shape, f32)
        for j in range(X_SLOTS - 1):
            x_copy(j, j).start()

    ahead = step + (X_SLOTS - 1)

    @pl.when(ahead < nsteps)
    def _():
        x_copy(ahead, lax.rem(ahead, X_SLOTS)).start()

    x_copy(step, slot).wait()
    ii = _iota2((C, C), 0)
    jj = _iota2((C, C), 1)
    ycat_ref[:, 0:DN_WIDTH] = ydn_ref[0]
    for g in range(SGU_GROUPS):
        gs = slice(g * C, (g + 1) * C)
        wsg = jnp.where(ii >= jj, ws_ref[g], 0.0).astype(bf16)
        for c in range(tm // C):
            rs = slice(c * C, (c + 1) * C)
            mixed = jnp.dot(wsg, vln_ref[0, rs, gs], preferred_element_type=f32) + bsp_ref[:, gs]
            ycat_ref[rs, DN_WIDTH + g * C:DN_WIDTH + (g + 1) * C] = (u_ref[0, rs, gs].astype(f32) * mixed).astype(bf16)

    RB = 128
    blocks = [slice(r, r + RB) for r in range(0, tm, RB)]
    mix = [jnp.dot(ycat_ref[rb, :], wout_ref[...], preferred_element_type=f32) for rb in blocks]
    h1s = []
    for rb, m in zip(blocks, mix):
        hp = DEEPNORM_ALPHA * xring_ref[slot, rb, :] + m
        mu = jnp.mean(hp, axis=-1, keepdims=True)
        hc = hp - mu
        var = jnp.mean(hc * hc, axis=-1, keepdims=True)
        h1 = hc * lax.rsqrt(var + LN_EPS) * g1_ref[...] + b1_ref[...]
        h_ref[0, rb, :] = h1
        h1b = h1.astype(bf16)
        hrow_ref[rb] = h1b.reshape(RB, ROW_TILE, 128)
        h1s.append(h1b)

    logit_blocks = [lax.dot_general(wrt_ref[...], hb, NT_DIMS, preferred_element_type=f32) + brt_ref[...] for hb in h1s]
    sub = _iota2((8, RB), 0)
    subf = sub.astype(f32)
    sub_e = _iota2((N_EXPERTS, RB), 0).astype(f32)
    chosen = []
    for rb, logits in zip(blocks, logit_blocks):
        gl = logits[0:8]
        gmax = jnp.max(gl, axis=0, keepdims=True)
        g_idx = jnp.min(jnp.where(gl == gmax, subf, float(MOE_GROUPS)), axis=0, keepdims=True)
        p_group = 1.0 / jnp.sum(jnp.exp(gl - gmax), axis=0, keepdims=True)
        within = jnp.zeros((8, RB), f32)
        for g in range(MOE_GROUPS):
            within = within + jnp.where(g_idx == float(g), logits[8 + 8 * g:16 + 8 * g], 0.0)
        m1 = jnp.max(within, axis=0, keepdims=True)
        i1 = jnp.min(jnp.where(within == m1, subf, float(EXPERTS_PER_GROUP)), axis=0, keepdims=True)
        rest = jnp.where(subf == i1, -jnp.inf, within)
        m2 = jnp.max(rest, axis=0, keepdims=True)
        i2 = jnp.min(jnp.where(rest == m2, subf, float(EXPERTS_PER_GROUP)), axis=0, keepdims=True)
        e = jnp.exp(m2 - m1)
        w1 = p_group / (1.0 + e)
        w2 = p_group * e / (1.0 + e)
        e1 = g_idx * float(EXPERTS_PER_GROUP) + i1
        e2 = g_idx * float(EXPERTS_PER_GROUP) + i2
        ids_ref[:, rb] = jnp.where(sub == 0, e1, jnp.where(sub == 1, e2, 0.0)).astype(i32)
        wts_ref[:, rb] = jnp.where(sub == 0, w1, jnp.where(sub == 1, w2, 0.0))
        chosen.append(((sub_e == e1).astype(f32) + (sub_e == e2).astype(f32)).astype(bf16))
    ones = jnp.ones((RB, 128), bf16)
    cnt_ref[...] = cnt_ref[...] + sum(jnp.dot(oh, ones, preferred_element_type=f32) for oh in chosen)


def _stage_mixout(ydn, u, vln, x, ws, bsp, wout, g1, b1, wrt, brt, *, tm):
    B, T, _ = x.shape
    nt = T // tm
    act_spec = lambda: pl.BlockSpec((1, tm, DN_WIDTH), lambda b, t: (b, t, 0))
    const2 = lambda shp: pl.BlockSpec(shp, lambda b, t: (0, 0))
    tok_spec = lambda: pl.BlockSpec((8, tm), lambda b, t: (0, b * nt + t))
    return pl.pallas_call(
        functools.partial(_mixout_body, tm=tm, nsteps=B * nt),
        grid=(B, nt),
        in_specs=[act_spec(), act_spec(), act_spec(),
                  pl.BlockSpec(memory_space=pl.ANY),
                  pl.BlockSpec((SGU_GROUPS, SGU_CHUNK, SGU_CHUNK), lambda b, t: (0, 0, 0)),
                  const2((SGU_CHUNK, SGU_WIDTH)),
                  const2((D_MODEL, D_MODEL)),
                  const2((1, D_MODEL)), const2((1, D_MODEL)),
                  const2((128, D_MODEL)), const2((128, 128))],
        out_specs=[pl.BlockSpec((1, tm, D_MODEL), lambda b, t: (b, t, 0)),
                   pl.BlockSpec((tm, ROW_TILE, 128), lambda b, t: (b * nt + t, 0, 0)), tok_spec(), tok_spec(),
                   const2((N_EXPERTS, 128))],
        out_shape=[jax.ShapeDtypeStruct((B, T, D_MODEL), f32),
                   jax.ShapeDtypeStruct((B * T, ROW_TILE, 128), bf16),
                   jax.ShapeDtypeStruct((8, B * T), i32),
                   jax.ShapeDtypeStruct((8, B * T), f32),
                   jax.ShapeDtypeStruct((N_EXPERTS, 128), f32)],
        scratch_shapes=[pltpu.VMEM((tm, D_MODEL), bf16), pltpu.VMEM((X_SLOTS, tm, D_MODEL), f32),
                        pltpu.SemaphoreType.DMA((X_SLOTS,))],
        compiler_params=_cparams(("arbitrary", "arbitrary")),
        name="mixout",
    )(ydn, u, vln, x.reshape(B * nt, tm, D_MODEL), ws, bsp, wout, g1, b1, wrt, brt)


def _route_body(cnt_ref, ids_ref, dest_ref, meta_ref, blk_ref, pstart_ref, *, tm, nb_pad):
    i = pl.program_id(0)
    sub = _iota2((N_EXPERTS, tm), 0)
    is1 = sub == ids_ref[0:1, :]
    is2 = sub == ids_ref[1:2, :]
    oh = (is1.astype(f32) + is2.astype(f32)).astype(bf16)

    @pl.when(i == 0)
    def _():
        cnt = cnt_ref[...]
        padded = jnp.floor((cnt + (MOE_BLOCK - 1)) * (1.0 / MOE_BLOCK)) * MOE_BLOCK
        ei = _iota2((N_EXPERTS, N_EXPERTS), 0)
        ej = _iota2((N_EXPERTS, N_EXPERTS), 1)
        pends = jnp.dot((ei >= ej).astype(f32), padded, precision=HIGHEST, preferred_element_type=f32)
        pstart = pends - padded
        pstart_ref[...] = pstart
        s64 = _iota2((N_EXPERTS, 128), 0)
        l64 = _iota2((N_EXPERTS, 128), 1)
        diag = s64 == l64
        fill_off = jnp.sum(jnp.where(diag, pstart + cnt, 0.0), axis=0, keepdims=True)
        fill_n = jnp.sum(jnp.where(diag, padded - cnt, 0.0), axis=0, keepdims=True)
        nused = pends[N_EXPERTS - 1:N_EXPERTS, :] * (1.0 / MOE_BLOCK)
        m8 = _iota2((8, 128), 0)
        meta_ref[...] = jnp.where(m8 == 0, fill_off, jnp.where(m8 == 1, fill_n, jnp.where(m8 == 2, nused, 0.0))).astype(i32)
        bstart = (_iota2((N_EXPERTS, nb_pad), 1) * MOE_BLOCK).astype(f32)
        pe = jnp.concatenate([pends] * (nb_pad // 128), axis=1)
        be = jnp.sum((pe <= bstart).astype(f32), axis=0, keepdims=True)
        be = jnp.minimum(be, float(N_EXPERTS - 1))
        blk_ref[...] = jnp.broadcast_to(be, (8, nb_pad)).astype(i32)

    ti = _iota2((tm, tm), 0)
    tj = _iota2((tm, tm), 1)
    before = (ti < tj).astype(bf16)
    prefix = jnp.dot(oh, before, preferred_element_type=f32)
    nxt = prefix + jnp.concatenate([pstart_ref[...]] * (tm // 128), axis=1)
    d1 = jnp.sum(jnp.where(is1, nxt, 0.0), axis=0, keepdims=True)
    d2 = jnp.sum(jnp.where(is2, nxt, 0.0), axis=0, keepdims=True)
    sub8 = _iota2((8, tm), 0)
    dest_ref[...] = jnp.where(sub8 == 0, d1, jnp.where(sub8 == 1, d2, 0.0)).astype(i32)
    pstart_ref[...] = pstart_ref[...] + jnp.dot(oh, jnp.ones((tm, 128), bf16), preferred_element_type=f32)


def _stage_route(cnt, ids, *, tm, nb_pad):
    n = ids.shape[1]
    return pl.pallas_call(
        functools.partial(_route_body, tm=tm, nb_pad=nb_pad),
        grid=(n // tm,),
        in_specs=[pl.BlockSpec((N_EXPERTS, 128), lambda i: (0, 0)), pl.BlockSpec((8, tm), lambda i: (0, i))],
        out_specs=[pl.BlockSpec((8, tm), lambda i: (0, i)),
                   pl.BlockSpec((8, 128), lambda i: (0, 0)),
                   pl.BlockSpec((8, nb_pad), lambda i: (0, 0))],
        out_shape=[jax.ShapeDtypeStruct((8, n), i32), jax.ShapeDtypeStruct((8, 128), i32),
                   jax.ShapeDtypeStruct((8, nb_pad), i32)],
        scratch_shapes=[pltpu.VMEM((N_EXPERTS, 128), f32)],
        compiler_params=_cparams(("arbitrary",)),
        name="moe_route",
    )(cnt, ids)


def _dispatch_body(fill_off_ref, fill_n_ref, nused_ref, dest_ref, h3_ref, xs_ref, zero_ref, sem, zsem, *, tm):
    def row_copy(t, d):
        return pltpu.make_async_copy(h3_ref.at[t], xs_ref.at[d], sem)

    def issue(t, carry):
        row_copy(t, dest_ref[0, 0, t]).start(priority=0)
        row_copy(t, dest_ref[0, 1, t]).start(priority=1)
        return carry

    lax.fori_loop(0, tm, issue, 0, unroll=8)

    @pl.when(pl.program_id(0) == 0)
    def _():
        zero_ref[...] = jnp.zeros(zero_ref.shape, bf16)

        def fill(start):
            def body(e, carry):
                off = fill_off_ref[e]
                npad = fill_n_ref[e]
                bit = MOE_BLOCK // 2
                while bit:
                    @pl.when((npad & bit) != 0)
                    def _(off=off, bit=bit):
                        cp = pltpu.make_async_copy(zero_ref.at[pl.ds(0, bit)], xs_ref.at[pl.ds(off, bit)], zsem)
                        cp.start() if start else cp.wait()
                    off = off + (npad & bit)
                    bit //= 2
                return carry
            return body

        lax.fori_loop(0, N_EXPERTS, fill(True), 0)
        lax.fori_loop(0, N_EXPERTS, fill(False), 0)

        def tail_copy(b):
            return pltpu.make_async_copy(zero_ref, xs_ref.at[pl.ds(b * MOE_BLOCK, MOE_BLOCK)], zsem)

        nblocks = xs_ref.shape[0] // MOE_BLOCK
        lax.fori_loop(nused_ref[0], nblocks, lambda b, c: (tail_copy(b).start(), c)[1], 0)
        lax.fori_loop(nused_ref[0], nblocks, lambda b, c: (tail_copy(0).wait(), c)[1], 0)

    for _ in range(2):
        pltpu.make_async_copy(h3_ref, xs_ref.at[pl.ds(0, tm)], sem).wait()


def _stage_dispatch(fill_off, fill_n, nused, dest3, hrow, p_rows, *, tm):
    n = hrow.shape[0]
    return pl.pallas_call(
        functools.partial(_dispatch_body, tm=tm),
        grid_spec=pltpu.PrefetchScalarGridSpec(
            num_scalar_prefetch=3,
            grid=(n // tm,),
            in_specs=[pl.BlockSpec((1, 2, tm), lambda i, fo, fn, nu: (i, 0, 0), memory_space=pltpu.SMEM),
                      pl.BlockSpec((tm, ROW_TILE, 128), lambda i, fo, fn, nu: (i, 0, 0))],
            out_specs=pl.BlockSpec(memory_space=pl.ANY),
            scratch_shapes=[pltpu.VMEM((MOE_BLOCK, ROW_TILE, 128), bf16),
                            pltpu.SemaphoreType.DMA, pltpu.SemaphoreType.DMA],
        ),
        out_shape=jax.ShapeDtypeStruct((p_rows, ROW_TILE, 128), bf16),
        compiler_params=_cparams(("arbitrary",)),
        name="moe_dispatch",
    )(fill_off, fill_n, nused, dest3, hrow)


def _experts_body(blk_ref, nused_ref, xs_hbm, wg_hbm, wu_hbm, wd_hbm, ys_ref,
                  xbuf_ref, wg32_ref, wu32_ref, wd32_ref, wgu16_ref, wd16_ref, xsem, wsem):
    i = pl.program_id(0)
    nused = nused_ref[0]
    used = i < nused
    e = blk_ref[i]
    slot = lax.rem(i, X_SLOTS)

    def weight_copies(ex):
        return (pltpu.make_async_copy(wg_hbm.at[ex], wg32_ref, wsem.at[0]),
                pltpu.make_async_copy(wu_hbm.at[ex], wu32_ref, wsem.at[1]),
                pltpu.make_async_copy(wd_hbm.at[ex], wd32_ref, wsem.at[2]))

    def x_copy(block, s):
        return pltpu.make_async_copy(xs_hbm.at[pl.ds(block * MOE_BLOCK, MOE_BLOCK)], xbuf_ref.at[s], xsem.at[s])

    @pl.when((i == 0) & used)
    def _():
        for cp in weight_copies(e):
            cp.start()
        for j in range(X_SLOTS - 1):
            @pl.when(j < nused)
            def _(j=j):
                x_copy(j, j).start()

    ahead = i + (X_SLOTS - 1)

    @pl.when(ahead < nused)
    def _():
        x_copy(ahead, lax.rem(ahead, X_SLOTS)).start()

    @pl.when(used & ((i == 0) | (e != blk_ref[jnp.maximum(i - 1, 0)])))
    def _():
        for cp in weight_copies(e):
            cp.wait()
        wgu16_ref[:, 0:D_EXPERT] = wg32_ref[...].astype(bf16)
        wgu16_ref[:, D_EXPERT:2 * D_EXPERT] = wu32_ref[...].astype(bf16)
        wd16_ref[...] = wd32_ref[...].astype(bf16)
        nxt = lax.while_loop(lambda j: (j < nused) & (blk_ref[jnp.minimum(j, nused - 1)] == e), lambda j: j + 1, i + 1)

        @pl.when(nxt < nused)
        def _():
            for cp in weight_copies(blk_ref[jnp.minimum(nxt, nused - 1)]):
                cp.start(priority=1)

    @pl.when(used)
    def _():
        x_copy(i, slot).wait()
        half = MOE_BLOCK // 2
        rows = [slice(p * half, (p + 1) * half) for p in range(2)]
        gu = [jnp.dot(xbuf_ref[slot, r].reshape(half, D_MODEL), wgu16_ref[...], preferred_element_type=f32)
              for r in rows]
        hid = [(_silu(g[:, :D_EXPERT]) * g[:, D_EXPERT:]).astype(bf16) for g in gu]
        y = [jnp.dot(hd, wd16_ref[...], preferred_element_type=f32) for hd in hid]
        for r, yp in zip(rows, y):
            ys_ref[r] = yp.astype(bf16).reshape(half, ROW_TILE, 128)

    @pl.when(jnp.logical_not(used))
    def _():
        ys_ref[...] = jnp.zeros(ys_ref.shape, bf16)


def _stage_experts(blk_e, nused, xs, w_gate, w_up, w_down):
    p_rows = xs.shape[0]
    nb = p_rows // MOE_BLOCK

    def row_map(i, blk, nu):
        return (i, 0, 0)

    return pl.pallas_call(
        _experts_body,
        grid_spec=pltpu.PrefetchScalarGridSpec(
            num_scalar_prefetch=2,
            grid=(nb,),
            in_specs=[pl.BlockSpec(memory_space=pl.ANY),
                      pl.BlockSpec(memory_space=pl.ANY),
                      pl.BlockSpec(memory_space=pl.ANY),
                      pl.BlockSpec(memory_space=pl.ANY)],
            out_specs=pl.BlockSpec((MOE_BLOCK, ROW_TILE, 128), row_map),
            scratch_shapes=[pltpu.VMEM((X_SLOTS, MOE_BLOCK, ROW_TILE, 128), bf16),
                            pltpu.VMEM((D_MODEL, D_EXPERT), f32), pltpu.VMEM((D_MODEL, D_EXPERT), f32),
                            pltpu.VMEM((D_EXPERT, D_MODEL), f32),
                            pltpu.VMEM((D_MODEL, 2 * D_EXPERT), bf16), pltpu.VMEM((D_EXPERT, D_MODEL), bf16),
                            pltpu.SemaphoreType.DMA((X_SLOTS,)), pltpu.SemaphoreType.DMA((3,))],
        ),
        out_shape=jax.ShapeDtypeStruct((p_rows, ROW_TILE, 128), bf16),
        compiler_params=_cparams(("arbitrary",)),
        name="moe_experts",
    )(blk_e, nused, xs, w_gate, w_up, w_down)


def _combine_body(dcur_ref, dnext_ref, h_ref, wts_ref, g2_ref, b2_ref, ys_ref, o_ref, ybuf_ref, sem, *, tm, nsteps):
    i = pl.program_id(0)
    slot = lax.rem(i, 2)

    def issue_tile(d_ref, s):
        def body(t, carry):
            for k in range(2):
                pltpu.make_async_copy(ys_ref.at[d_ref[0, k, t]], ybuf_ref.at[s, k, t], sem.at[s]).start(priority=k)
            return carry

        lax.fori_loop(0, tm, body, 0, unroll=8)

    @pl.when(i == 0)
    def _():
        issue_tile(dcur_ref, 0)

    @pl.when(i + 1 < nsteps)
    def _():
        issue_tile(dnext_ref, 1 - slot)

    for k in range(2):
        pltpu.make_async_copy(ys_ref.at[pl.ds(0, tm)], ybuf_ref.at[slot, k], sem.at[slot]).wait()

    pieces = []
    for c in range(tm // 128):
        ls = slice(c * 128, (c + 1) * 128)
        w1c = jnp.broadcast_to(wts_ref[0:1, ls], (128, 128)).T
        w2c = jnp.broadcast_to(wts_ref[1:2, ls], (128, 128)).T
        w1f = jnp.concatenate([w1c] * (D_MODEL // 128), axis=1)
        w2f = jnp.concatenate([w2c] * (D_MODEL // 128), axis=1)
        y1 = ybuf_ref[slot, 0, ls].reshape(128, D_MODEL).astype(f32)
        y2 = ybuf_ref[slot, 1, ls].reshape(128, D_MODEL).astype(f32)
        pieces.append(w1f * y1 + w2f * y2)
    ffn = jnp.concatenate(pieces, axis=0)
    hp = DEEPNORM_ALPHA * h_ref[...] + ffn
    mu = jnp.mean(hp, axis=-1, keepdims=True)
    hc = hp - mu
    var = jnp.mean(hc * hc, axis=-1, keepdims=True)
    o_ref[...] = hc * lax.rsqrt(var + LN_EPS) * g2_ref[...] + b2_ref[...]


def _stage_combine(dest3, h2, wts, g2, b2, ys, *, tm):
    n = h2.shape[0]
    nsteps = n // tm
    return pl.pallas_call(
        functools.partial(_combine_body, tm=tm, nsteps=nsteps),
        grid=(nsteps,),
        in_specs=[pl.BlockSpec((1, 2, tm), lambda i: (i, 0, 0), memory_space=pltpu.SMEM),
                  pl.BlockSpec((1, 2, tm), lambda i: (jnp.minimum(i + 1, nsteps - 1), 0, 0), memory_space=pltpu.SMEM),
                  pl.BlockSpec((tm, D_MODEL), lambda i: (i, 0)),
                  pl.BlockSpec((8, tm), lambda i: (0, i)),
                  pl.BlockSpec((1, D_MODEL), lambda i: (0, 0)),
                  pl.BlockSpec((1, D_MODEL), lambda i: (0, 0)),
                  pl.BlockSpec(memory_space=pl.ANY)],
        out_specs=pl.BlockSpec((tm, D_MODEL), lambda i: (i, 0)),
        out_shape=jax.ShapeDtypeStruct((n, D_MODEL), f32),
        scratch_shapes=[pltpu.VMEM((2, 2, tm, ROW_TILE, 128), bf16), pltpu.SemaphoreType.DMA((2,))],
        compiler_params=_cparams(("arbitrary",)),
        name="moe_combine",
    )(dest3, dest3, h2, wts, g2, b2, ys)


def _layer(h, w_in, conv_w, a_log, dt_bias, dn_norm_w, sgu_ln_g, sgu_ln_b, w_spatial, b_spatial, w_out,
           ln1_g, ln1_b, w_rg, b_rg, w_re, b_re, w_gate, w_up, w_down, ln2_g, ln2_b,
           *, tm_in, in_groups, dn_rows, dn_chunks, dn_groups, tm_mix, tm_rank, tm_disp, tm_comb):
    B, T, _ = h.shape
    n = B * T
    qkvz = 4 * DN_WIDTH
    ba0 = qkvz
    uv0 = qkvz + 2 * DN_HEADS
    w_cols = jnp.zeros((D_MODEL, IN_COLS_ALIGNED), bf16)
    w_cols = lax.dynamic_update_slice(w_cols, w_in[:, :qkvz].astype(bf16), (0, 0))
    w_cols = lax.dynamic_update_slice(w_cols, w_in[:, uv0:].astype(bf16), (0, qkvz))
    w_cols = lax.dynamic_update_slice(w_cols, w_in[:, ba0:uv0].astype(bf16), (0, qkvz + 2 * SGU_WIDTH))
    decay_prm = jnp.stack([a_log, dt_bias])
    prow =jnp.broadcast_to(jnp.pad(decay_prm, ((0, 0), (DN_HEADS, 8 - 2 * DN_HEADS)))[:, :, None], (2, 8, 128))

    q, k, v, z, u, vln, gcol, grow = _stage_inproj(
        h, w_cols, conv_w, prow, sgu_ln_g[None, :], sgu_ln_b[None, :], tm=tm_in, ngroups=in_groups)
    ydn = _stage_deltanet(q, k, v, z, gcol, grow, dn_norm_w[None, :], nbr=dn_rows, nch=dn_chunks, ngroups=dn_groups)

    bsp = jnp.broadcast_to(b_spatial.T[:, :, None], (SGU_CHUNK, SGU_GROUPS, SGU_CHUNK)).reshape(SGU_CHUNK, SGU_WIDTH)
    n_logit = MOE_GROUPS + N_EXPERTS
    wrt = jnp.pad(jnp.concatenate([w_rg, w_re], axis=1).T, ((0, 128 - n_logit), (0, 0))).astype(bf16)
    brt = jnp.broadcast_to(jnp.pad(jnp.concatenate([b_rg, b_re]), (0, 128 - n_logit))[:, None], (128, 128))
    h1, hrow, ids, wts, cnt = _stage_mixout(ydn, u, vln, h, w_spatial, bsp, w_out.astype(bf16), ln1_g[None, :],
                                       ln1_b[None, :], wrt, brt, tm=tm_mix)

    p_rows = (-(-(n * 2) // MOE_BLOCK)) * MOE_BLOCK + N_EXPERTS * MOE_BLOCK
    nb = p_rows // MOE_BLOCK
    nb_pad = (-(-nb // 128)) * 128
    dest, meta, blk = _stage_route(cnt, ids, tm=tm_rank, nb_pad=nb_pad)

    h2 = h1.reshape(n, D_MODEL)
    dest_d = dest[0:2].reshape(2, n // tm_disp, tm_disp).transpose(1, 0, 2)
    xs = _stage_dispatch(meta[0, :N_EXPERTS], meta[1, :N_EXPERTS], meta[2, 0:1], dest_d, hrow, p_rows, tm=tm_disp)
    ys = _stage_experts(blk[0, :nb], meta[2, 0:1], xs, w_gate, w_up, w_down)
    dest_c = dest[0:2].reshape(2, n // tm_comb, tm_comb).transpose(1, 0, 2)
    out = _stage_combine(dest_c, h2, wts, ln2_g[None, :], ln2_b[None, :], ys, tm=tm_comb)
    return out.reshape(B, T, D_MODEL)


def kernel(x, w_in, conv_w, a_log, dt_bias, dn_norm_w, sgu_ln_g, sgu_ln_b, w_spatial, b_spatial, w_out, ln1_g, ln1_b, w_router_group, b_router_group, w_router_expert, b_router_expert, w_gate, w_up, w_down, ln2_g, ln2_b):
    h = x
    for l in range(w_in.shape[0]):
        h = _layer(h, w_in[l], conv_w[l], a_log[l], dt_bias[l], dn_norm_w[l], sgu_ln_g[l], sgu_ln_b[l],
                   w_spatial[l], b_spatial[l], w_out[l], ln1_g[l], ln1_b[l],
                   w_router_group[l], b_router_group[l], w_router_expert[l], b_router_expert[l],
                   w_gate[l], w_up[l], w_down[l], ln2_g[l], ln2_b[l],
                   tm_in=512, in_groups=1, dn_rows=8, dn_chunks=2, dn_groups=4, tm_mix=512, tm_rank=1024, tm_disp=2048, tm_comb=512)
    return h
```

```python
import functools

import jax
import jax.numpy as jnp
from jax import lax
from jax.experimental import pallas as pl
from jax.experimental.pallas import tpu as pltpu

f32 = jnp.float32
bf16 = jnp.bfloat16
i32 = jnp.int32

D_MODEL = 1024
DN_WIDTH = 512
DN_HEADS = 4
HEAD_DIM = 128
CONV_K = 4
SGU_WIDTH = 512
SGU_GROUPS = 4
SGU_CHUNK = 128
DN_CHUNK = 128
MOE_GROUPS = 8
EXPERTS_PER_GROUP = 8
N_EXPERTS = 64
D_EXPERT = 512
MOE_BLOCK = 256
IN_COLS_ALIGNED = 4 * DN_WIDTH + 2 * SGU_WIDTH + 128
X_SLOTS = 4
ROW_TILE = D_MODEL // 128
DEEPNORM_ALPHA = 2.0 ** 0.25
LN_EPS = 1e-5
RMS_EPS = 1e-6
HIGHEST = lax.Precision.HIGHEST
VMEM_LIMIT_BYTES = 56 * 1024 * 1024

NT_DIMS = (((1,), (1,)), ((), ()))


def _cparams(sem, flags=None):
    return pltpu.CompilerParams(dimension_semantics=sem, vmem_limit_bytes=VMEM_LIMIT_BYTES, flags=flags)


def _sigmoid(x):
    return 1.0 / (1.0 + jnp.exp(-x))


def _silu(x):
    h = 0.5 * x
    return h + h * jnp.tanh(h)


def _softplus(x):
    return jnp.maximum(x, 0.0) + jnp.log1p(jnp.exp(-jnp.abs(x)))


def _gelu_tanh(x):
    c = 0.7978845608028654
    return x * (0.5 * (1.0 + jnp.tanh(c * (x + 0.044715 * (x * x * x)))))


def _iota2(shape, axis):
    return lax.broadcasted_iota(i32, shape, axis)


def _weight_layout_body(w_ref, o_ref):
    qkvz = 4 * DN_WIDTH
    uv0 = qkvz + 2 * DN_HEADS
    rows = w_ref.shape[0]
    o_ref[:, 0:qkvz] = w_ref[:, 0:qkvz].astype(bf16)
    o_ref[:, qkvz:qkvz + 2 * SGU_WIDTH] = w_ref[:, uv0:uv0 + 2 * SGU_WIDTH].astype(bf16)
    ba = jnp.concatenate([w_ref[:, qkvz:uv0], jnp.zeros((rows, 128 - 2 * DN_HEADS), f32)], axis=1)
    o_ref[:, qkvz + 2 * SGU_WIDTH:IN_COLS_ALIGNED] = ba.astype(bf16)


def _stage_weight_layout(w_in):
    rows = 256
    return pl.pallas_call(
        _weight_layout_body,
        grid=(D_MODEL // rows,),
        in_specs=[pl.BlockSpec((rows, w_in.shape[1]), lambda i: (i, 0))],
        out_specs=pl.BlockSpec((rows, IN_COLS_ALIGNED), lambda i: (i, 0)),
        out_shape=jax.ShapeDtypeStruct((D_MODEL, IN_COLS_ALIGNED), bf16),
        compiler_params=_cparams(("arbitrary",)),
        name="weight_layout",
    )(w_in)


def _inproj_body(x_ref, w_ref, convw_ref, prow_ref, lng_ref, lnb_ref, ones_ref,
                 q_ref, k_ref, v_ref, z_ref, u_ref, vln_ref, gcol_ref, grow_ref, *ext_refs, tm, ngroups):
    W = DN_WIDTH
    gm = tm // ngroups
    ext = [ext_refs[3 * g:3 * g + 3] for g in range(ngroups)]

    @pl.when(pl.program_id(1) == 0)
    def _():
        for e_ref in ext[0]:
            e_ref[0:8, :] = jnp.zeros((8, W), f32)

    for g in range(ngroups):
        _inproj_rows(x_ref, w_ref, convw_ref, prow_ref, lng_ref, lnb_ref, ones_ref,
                     q_ref, k_ref, v_ref, z_ref, u_ref, vln_ref, gcol_ref, grow_ref, ext[g],
                     ext[(g + 1) % ngroups], r0=g * gm, gm=gm)


def _inproj_rows(x_ref, w_ref, convw_ref, prow_ref, lng_ref, lnb_ref, ones_ref,
                 q_ref, k_ref, v_ref, z_ref, u_ref, vln_ref, gcol_ref, grow_ref, ext, ext_next, *, r0, gm):
    W = DN_WIDTH
    rows = slice(r0, r0 + gm)
    xb = x_ref[0, rows, :].astype(bf16)
    for part, e_ref in enumerate(ext):
        e_ref[8:8 + gm, :] = jnp.dot(xb, w_ref[:, part * W:(part + 1) * W], preferred_element_type=f32)
    zc = 3 * W
    uc = zc + W
    vc = uc + SGU_WIDTH
    bc = vc + SGU_WIDTH

    def conv_silu(part):
        e_ref = ext[part]
        cs = slice(part * W, (part + 1) * W)
        y = convw_ref[3:4, cs] * e_ref[8:8 + gm, :]
        for j in range(CONV_K - 1):
            y = y + convw_ref[j:j + 1, cs] * e_ref[5 + j:5 + j + gm, :]
        ext_next[part][0:8, :] = e_ref[gm:gm + 8, :]
        return _silu(y)

    def group_sums(a):
        return jnp.dot(a.astype(bf16), ones_ref[...], preferred_element_type=f32)

    yq = conv_silu(0)
    ssq = group_sums(yq * yq)
    pba = jnp.dot(xb, w_ref[:, bc:bc + 128], preferred_element_type=f32)
    pv = jnp.dot(xb, w_ref[:, vc:vc + SGU_WIDTH], preferred_element_type=f32)
    yk = conv_silu(1)
    ssk = group_sums(yk * yk)
    pu = jnp.dot(xb, w_ref[:, uc:uc + SGU_WIDTH], preferred_element_type=f32)
    pz = jnp.dot(xb, w_ref[:, zc:zc + W], preferred_element_type=f32)
    q_ref[0, rows, :] = (yq * (lax.rsqrt(ssq + RMS_EPS) * HEAD_DIM ** -0.5)).astype(bf16)
    k_ref[0, rows, :] = (yk * lax.rsqrt(ssk + RMS_EPS)).astype(bf16)
    v_ref[0, rows, :] = conv_silu(2).astype(bf16)

    z_ref[0, rows, :] = _silu(pz).astype(bf16)

    u_ref[0, rows, :] = _gelu_tanh(pu).astype(bf16)
    pv = _gelu_tanh(pv)
    for g in range(SGU_GROUPS):
        sl = slice(g * SGU_CHUNK, (g + 1) * SGU_CHUNK)
        vg = pv[:, sl]
        mu = jnp.mean(vg, axis=-1, keepdims=True)
        vcn = vg - mu
        var = jnp.mean(vcn * vcn, axis=-1, keepdims=True)
        vln_ref[0, rows, sl] = (vcn * lax.rsqrt(var + LN_EPS) * lng_ref[:, sl] + lnb_ref[:, sl]).astype(bf16)

    lane = _iota2((DN_CHUNK, 128), 1)
    beta = _sigmoid(pba)
    lane8 = _iota2((8, DN_CHUNK), 1)
    sub8 = _iota2((8, DN_CHUNK), 0)
    for c in range(gm // DN_CHUNK):
        rs = slice(c * DN_CHUNK, (c + 1) * DN_CHUNK)
        os_ = slice(r0 + c * DN_CHUNK, r0 + (c + 1) * DN_CHUNK)
        pbat = pba[rs].T[0:8, :]
        gt = -jnp.exp(prow_ref[0]) * _softplus(pbat + prow_ref[1])
        gc = jnp.where(sub8 >= DN_HEADS, gt, 0.0)
        shift = 1
        while shift < DN_CHUNK:
            gc = gc + jnp.where(lane8 >= shift, pltpu.roll(gc, shift, axis=1), 0.0)
            shift *= 2
        grow_ref[0, :, os_] = gc
        gc_col = jnp.concatenate([gc, jnp.zeros((DN_CHUNK - 8, DN_CHUNK), f32)], axis=0).T
        gcol_ref[0, os_, :] = jnp.where(lane < DN_HEADS, beta[rs], gc_col)


def _stage_inproj(x, w_re, conv_w, prow, lng, lnb, *, tm, ngroups):
    B, T, _ = x.shape
    wcols = w_re.shape[1]
    grid = (B, T // tm)
    gi = lax.broadcasted_iota(i32, (DN_WIDTH, DN_WIDTH), 0) // 128
    gj = lax.broadcasted_iota(i32, (DN_WIDTH, DN_WIDTH), 1) // 128
    group_ones = (gi == gj).astype(bf16)
    act = lambda: jax.ShapeDtypeStruct((B, T, DN_WIDTH), bf16)
    act_spec = lambda: pl.BlockSpec((1, tm, DN_WIDTH), lambda b, t: (b, t, 0))
    const2 = lambda shp: pl.BlockSpec(shp, lambda b, t: (0, 0))
    return pl.pallas_call(
        functools.partial(_inproj_body, tm=tm, ngroups=ngroups),
        grid=grid,
        in_specs=[
            pl.BlockSpec((1, tm, D_MODEL), lambda b, t: (b, t, 0)),
            const2((D_MODEL, wcols)),
            const2((CONV_K, 3 * DN_WIDTH)),
            pl.BlockSpec((2, 8, 128), lambda b, t: (0, 0, 0)),
            const2((1, SGU_WIDTH)),
            const2((1, SGU_WIDTH)),
            const2((DN_WIDTH, DN_WIDTH)),
        ],
        out_specs=[act_spec() for _ in range(6)] + [
            pl.BlockSpec((1, tm, 128), lambda b, t: (b, t, 0)),
            pl.BlockSpec((1, 8, tm), lambda b, t: (b, 0, t)),
        ],
        out_shape=[act() for _ in range(6)] + [
            jax.ShapeDtypeStruct((B, T, 128), f32),
            jax.ShapeDtypeStruct((B, 8, T), f32),
        ],
        scratch_shapes=[pltpu.VMEM((tm // ngroups + 8, DN_WIDTH), f32) for _ in range(3 * ngroups)],
        compiler_params=_cparams(("arbitrary", "arbitrary")),
        name="inproj",
    )(x, w_re, conv_w, prow, lng, lnb, group_ones)


def _mm(a, b):
    return jnp.dot(a.astype(bf16), b.astype(bf16), preferred_element_type=f32)


def _unit_lower_inverse(nmats, ii, jj):
    n = nmats[0].shape[0]
    eye = (ii == jj).astype(f32)
    leaf = jnp.right_shift(ii, 3) == jnp.right_shift(jj, 3)
    dblk = [jnp.where(leaf, m, 0.0) for m in nmats]
    s1 = [_mm(d, d) for d in dblk]
    r1 = [eye - d for d in dblk]
    both = [_mm(s, jnp.concatenate([s, r], axis=1)) for s, r in zip(s1, r1)]
    r2 = [r + bo[:, n:] for r, bo in zip(r1, both)]
    xs = [r + _mm(bo[:, :n], r) for r, bo in zip(r2, both)]
    size = 8
    while size < n:
        lows = [slice(r + size, r + 2 * size) for r in range(0, n, 2 * size)]
        ups = [slice(r, r + size) for r in range(0, n, 2 * size)]
        rsel = _iota2((n // 2, n), 0)
        ilow = rsel + size * (jnp.right_shift(rsel, size.bit_length() - 1) + 1)
        jlow = _iota2((n // 2, n), 1)
        in_pair_upper = (jlow >= ilow - (ilow & (2 * size - 1))) & (jlow < ilow - (ilow & (size - 1)))
        zeros = jnp.zeros((size, n), f32)
        new_xs = []
        ylows = [_mm(jnp.where(in_pair_upper, jnp.concatenate([m[s] for s in lows], axis=0), 0.0), x)
                 for m, x in zip(nmats, xs)]
        yfull = [jnp.concatenate([piece for k in range(len(lows)) for piece in (zeros, y[k * size:(k + 1) * size])], axis=0)
                 for y in ylows]
        corr = [_mm(jnp.concatenate([x[s] for s in lows], axis=0), yf) for x, yf in zip(xs, yfull)]
        for x, c in zip(xs, corr):
            pieces = []
            for k, (u, l) in enumerate(zip(ups, lows)):
                pieces += [x[u], x[l] - c[k * size:(k + 1) * size]]
            new_xs.append(jnp.concatenate(pieces, axis=0))
        xs = new_xs
        size *= 2
    return xs


def _deltanet_body(q_ref, k_ref, v_ref, z_ref, gcol_ref, grow_ref, nw_ref, y_ref, s_ref, *, nbr, nch, ngroups):
    @pl.when(pl.program_id(1) == 0)
    def _():
        s_ref[...] = jnp.zeros(s_ref.shape, f32)

    per = nbr // ngroups
    for g in range(ngroups):
        _deltanet_rows(q_ref, k_ref, v_ref, z_ref, gcol_ref, grow_ref, nw_ref, y_ref, s_ref,
                       rows=range(g * per, (g + 1) * per), nch=nch)


def _deltanet_rows(q_ref, k_ref, v_ref, z_ref, gcol_ref, grow_ref, nw_ref, y_ref, s_ref, *, rows, nch):
    C = DN_CHUNK
    S = [(b, h) for b in rows for h in range(DN_HEADS)]
    P = [(b, c, h) for c in range(nch) for b, h in S]

    ii = _iota2((C, C), 0)
    jj = _iota2((C, C), 1)
    causal = ii >= jj
    rs = [slice(c * C, (c + 1) * C) for c in range(nch)]
    hs = [slice(h * HEAD_DIM, (h + 1) * HEAD_DIM) for h in range(DN_HEADS)]
    gcol = {(b, c): gcol_ref[b, rs[c], :] for b in rows for c in range(nch)}
    qh = {(b, c, h): q_ref[b, rs[c], hs[h]] for b, c, h in P}
    kh = {(b, c, h): k_ref[b, rs[c], hs[h]] for b, c, h in P}
    vh = {(b, c, h): v_ref[b, rs[c], hs[h]] for b, c, h in P}
    gc_b = {(b, c, h): jnp.broadcast_to(gcol[b, c][:, DN_HEADS + h:DN_HEADS + h + 1], (C, HEAD_DIM)) for b, c, h in P}
    beta_b = {(b, c, h): jnp.broadcast_to(gcol[b, c][:, h:h + 1], (C, HEAD_DIM)) for b, c, h in P}
    gc_r = {(b, c, h): jnp.broadcast_to(grow_ref[b, DN_HEADS + h:DN_HEADS + h + 1, rs[c]], (C, C)) for b, c, h in P}
    decay = {p: jnp.exp(jnp.where(causal, gc_b[p] - gc_r[p], -1e30)) for p in P}

    kf = {p: kh[p].astype(f32) for p in P}
    kb = {p: kf[p] * beta_b[p] for p in P}
    kk = {p: lax.dot_general(kb[p].astype(bf16), kh[p], NT_DIMS, preferred_element_type=f32) for p in P}
    a_intra = {p: lax.dot_general(qh[p], kh[p], NT_DIMS, preferred_element_type=f32) * decay[p] for p in P}
    nmat = [jnp.where(ii > jj, kk[p] * decay[p], 0.0) for p in P]
    tinv = dict(zip(P, _unit_lower_inverse(nmat, ii, jj)))

    eg = {p: jnp.exp(gc_b[p]) for p in P}
    rhs = {p: jnp.concatenate([vh[p].astype(f32) * beta_b[p], kb[p] * eg[p]], axis=1) for p in P}
    sol = {p: _mm(tinv[p], rhs[p]) for p in P}

    q_dec = {p: qh[p].astype(f32) * eg[p] for p in P}
    g_last = {p: gc_b[p][C - 1:C, :] for p in P}
    kdt = {p: (kf[p] * jnp.exp(g_last[p] - gc_b[p])).T for p in P}

    state = {(b, h): s_ref[b * DN_HEADS + h] for b, h in S}
    for c in range(nch):
        m1 = {(b, h): _mm(jnp.concatenate([sol[b, c, h][:, HEAD_DIM:], q_dec[b, c, h]], axis=0), state[b, h]) for b, h in S}
        v_new = {(b, h): sol[b, c, h][:, :HEAD_DIM] - m1[b, h][:C] for b, h in S}
        m2 = {(b, h): _mm(jnp.concatenate([a_intra[b, c, h], kdt[b, c, h]], axis=0), v_new[b, h]) for b, h in S}
        state = {(b, h): state[b, h] * jnp.exp(g_last[b, c, h]) + m2[b, h][C:] for b, h in S}
        for b, h in S:
            o = m1[b, h][C:] + m2[b, h][:C]
            rms = lax.rsqrt(jnp.mean(o * o, axis=-1, keepdims=True) + RMS_EPS)
            y_ref[b, rs[c], hs[h]] = (o * rms * nw_ref[...] * z_ref[b, rs[c], hs[h]].astype(f32)).astype(bf16)
    for b, h in S:
        s_ref[b * DN_HEADS + h] = state[b, h]


def _stage_deltanet(q, k, v, z, gcol, grow, norm_w, *, nbr, nch, ngroups):
    B, T, _ = q.shape
    tt = nch * DN_CHUNK
    act_spec = lambda: pl.BlockSpec((nbr, tt, DN_WIDTH), lambda b, t: (b, t, 0))
    return pl.pallas_call(
        functools.partial(_deltanet_body, nbr=nbr, nch=nch, ngroups=ngroups),
        grid=(B // nbr, T // tt),
        in_specs=[act_spec(), act_spec(), act_spec(), act_spec(),
                  pl.BlockSpec((nbr, tt, 128), lambda b, t: (b, t, 0)),
                  pl.BlockSpec((nbr, 8, tt), lambda b, t: (b, 0, t)),
                  pl.BlockSpec((1, HEAD_DIM), lambda b, t: (0, 0))],
        out_specs=act_spec(),
        out_shape=jax.ShapeDtypeStruct((B, T, DN_WIDTH), bf16),
        scratch_shapes=[pltpu.VMEM((nbr * DN_HEADS, HEAD_DIM, HEAD_DIM), f32)],
        compiler_params=_cparams(("arbitrary", "arbitrary")),
        name="deltanet",
    )(q, k, v, z, gcol, grow, norm_w)


def _mixout_body(ydn_ref, u_ref, vln_ref, x_hbm, ws_ref, bsp_ref, wout_ref, g1_ref, b1_ref, wrt_ref, brt_ref,
                 h_ref, hrow_ref, ids_ref, wts_ref, cnt_ref, ycat_ref, xring_ref, xsem, *, tm, nsteps):
    C = SGU_CHUNK
    step = pl.program_id(0) * pl.num_programs(1) + pl.program_id(1)
    slot = lax.rem(step, X_SLOTS)

    def x_copy(s, sl):
        return pltpu.make_async_copy(x_hbm.at[s], xring_ref.at[sl], xsem.at[sl])

    @pl.when(step == 0)
    def _():
        cnt_ref[...] = jnp.zeros(cnt_ref.shape, f32)
        for j in range(X_SLOTS - 1):
            x_copy(j, j).start()

    ahead = step + (X_SLOTS - 1)

    @pl.when(ahead < nsteps)
    def _():
        x_copy(ahead, lax.rem(ahead, X_SLOTS)).start()

    x_copy(step, slot).wait()
    ii = _iota2((C, C), 0)
    jj = _iota2((C, C), 1)
    ycat_ref[:, 0:DN_WIDTH] = ydn_ref[0]
    for g in range(SGU_GROUPS):
        gs = slice(g * C, (g + 1) * C)
        wsg = jnp.where(ii >= jj, ws_ref[g], 0.0).astype(bf16)
        for c in range(tm // C):
            rs = slice(c * C, (c + 1) * C)
            mixed = jnp.dot(wsg, vln_ref[0, rs, gs], preferred_element_type=f32) + bsp_ref[:, gs]
            ycat_ref[rs, DN_WIDTH + g * C:DN_WIDTH + (g + 1) * C] = (u_ref[0, rs, gs].astype(f32) * mixed).astype(bf16)

    RB = 128
    blocks = [slice(r, r + RB) for r in range(0, tm, RB)]
    mix = [jnp.dot(ycat_ref[rb, :], wout_ref[...], preferred_element_type=f32) for rb in blocks]
    h1s = []
    for rb, m in zip(blocks, mix):
        hp = DEEPNORM_ALPHA * xring_ref[slot, rb, :] + m
        mu = jnp.mean(hp, axis=-1, keepdims=True)
        hc = hp - mu
        var = jnp.mean(hc * hc, axis=-1, keepdims=True)
        h1 = hc * lax.rsqrt(var + LN_EPS) * g1_ref[...] + b1_ref[...]
        h_ref[0, rb, :] = h1
        h1b = h1.astype(bf16)
        hrow_ref[rb] = h1b.reshape(RB, ROW_TILE, 128)
        h1s.append(h1b)

    logit_blocks = [lax.dot_general(wrt_ref[...], hb, NT_DIMS, preferred_element_type=f32) + brt_ref[...] for hb in h1s]
    sub = _iota2((8, RB), 0)
    subf = sub.astype(f32)
    sub_e = _iota2((N_EXPERTS, RB), 0).astype(f32)
    chosen = []
    for rb, logits in zip(blocks, logit_blocks):
        gl = logits[0:8]
        gmax = jnp.max(gl, axis=0, keepdims=True)
        g_idx = jnp.min(jnp.where(gl == gmax, subf, float(MOE_GROUPS)), axis=0, keepdims=True)
        p_group = 1.0 / jnp.sum(jnp.exp(gl - gmax), axis=0, keepdims=True)
        within = jnp.zeros((8, RB), f32)
        for g in range(MOE_GROUPS):
            within = within + jnp.where(g_idx == float(g), logits[8 + 8 * g:16 + 8 * g], 0.0)
        m1 = jnp.max(within, axis=0, keepdims=True)
        i1 = jnp.min(jnp.where(within == m1, subf, float(EXPERTS_PER_GROUP)), axis=0, keepdims=True)
        rest = jnp.where(subf == i1, -jnp.inf, within)
        m2 = jnp.max(rest, axis=0, keepdims=True)
        i2 = jnp.min(jnp.where(rest == m2, subf, float(EXPERTS_PER_GROUP)), axis=0, keepdims=True)
        e = jnp.exp(m2 - m1)
        w1 = p_group / (1.0 + e)
        w2 = p_group * e / (1.0 + e)
        e1 = g_idx * float(EXPERTS_PER_GROUP) + i1
        e2 = g_idx * float(EXPERTS_PER_GROUP) + i2
        ids_ref[:, rb] = jnp.where(sub == 0, e1, jnp.where(sub == 1, e2, 0.0)).astype(i32)
        wts_ref[:, rb] = jnp.where(sub == 0, w1, jnp.where(sub == 1, w2, 0.0))
        chosen.append(((sub_e == e1).astype(f32) + (sub_e == e2).astype(f32)).astype(bf16))
    ones = jnp.ones((RB, 128), bf16)
    cnt_ref[...] = cnt_ref[...] + sum(jnp.dot(oh, ones, preferred_element_type=f32) for oh in chosen)


def _stage_mixout(ydn, u, vln, x, ws, bsp, wout, g1, b1, wrt, brt, *, tm):
    B, T, _ = x.shape
    nt = T // tm
    act_spec = lambda: pl.BlockSpec((1, tm, DN_WIDTH), lambda b, t: (b, t, 0))
    const2 = lambda shp: pl.BlockSpec(shp, lambda b, t: (0, 0))
    tok_spec = lambda: pl.BlockSpec((8, tm), lambda b, t: (0, b * nt + t))
    return pl.pallas_call(
        functools.partial(_mixout_body, tm=tm, nsteps=B * nt),
        grid=(B, nt),
        in_specs=[act_spec(), act_spec(), act_spec(),
                  pl.BlockSpec(memory_space=pl.ANY),
                  pl.BlockSpec((SGU_GROUPS, SGU_CHUNK, SGU_CHUNK), lambda b, t: (0, 0, 0)),
                  const2((SGU_CHUNK, SGU_WIDTH)),
                  const2((D_MODEL, D_MODEL)),
                  const2((1, D_MODEL)), const2((1, D_MODEL)),
                  const2((128, D_MODEL)), const2((128, 128))],
        out_specs=[pl.BlockSpec((1, tm, D_MODEL), lambda b, t: (b, t, 0)),
                   pl.BlockSpec((tm, ROW_TILE, 128), lambda b, t: (b * nt + t, 0, 0)), tok_spec(), tok_spec(),
                   const2((N_EXPERTS, 128))],
        out_shape=[jax.ShapeDtypeStruct((B, T, D_MODEL), f32),
                   jax.ShapeDtypeStruct((B * T, ROW_TILE, 128), bf16),
                   jax.ShapeDtypeStruct((8, B * T), i32),
                   jax.ShapeDtypeStruct((8, B * T), f32),
                   jax.ShapeDtypeStruct((N_EXPERTS, 128), f32)],
        scratch_shapes=[pltpu.VMEM((tm, D_MODEL), bf16), pltpu.VMEM((X_SLOTS, tm, D_MODEL), f32),
                        pltpu.SemaphoreType.DMA((X_SLOTS,))],
        compiler_params=_cparams(("arbitrary", "arbitrary")),
        name="mixout",
    )(ydn, u, vln, x.reshape(B * nt, tm, D_MODEL), ws, bsp, wout, g1, b1, wrt, brt)


def _route_body(cnt_ref, ids_ref, dest_ref, meta_ref, blk_ref, pstart_ref, *, tm, nb_pad):
    i = pl.program_id(0)
    sub = _iota2((N_EXPERTS, tm), 0)
    is1 = sub == ids_ref[0:1, :]
    is2 = sub == ids_ref[1:2, :]
    oh = (is1.astype(f32) + is2.astype(f32)).astype(bf16)

    @pl.when(i == 0)
    def _():
        cnt = cnt_ref[...]
        padded = jnp.floor((cnt + (MOE_BLOCK - 1)) * (1.0 / MOE_BLOCK)) * MOE_BLOCK
        ei = _iota2((N_EXPERTS, N_EXPERTS), 0)
        ej = _iota2((N_EXPERTS, N_EXPERTS), 1)
        pends = jnp.dot((ei >= ej).astype(f32), padded, precision=HIGHEST, preferred_element_type=f32)
        pstart = pends - padded
        pstart_ref[...] = pstart
        s64 = _iota2((N_EXPERTS, 128), 0)
        l64 = _iota2((N_EXPERTS, 128), 1)
        diag = s64 == l64
        fill_off = jnp.sum(jnp.where(diag, pstart + cnt, 0.0), axis=0, keepdims=True)
        fill_n = jnp.sum(jnp.where(diag, padded - cnt, 0.0), axis=0, keepdims=True)
        nused = pends[N_EXPERTS - 1:N_EXPERTS, :] * (1.0 / MOE_BLOCK)
        m8 = _iota2((8, 128), 0)
        meta_ref[...] = jnp.where(m8 == 0, fill_off, jnp.where(m8 == 1, fill_n, jnp.where(m8 == 2, nused, 0.0))).astype(i32)
        bstart = (_iota2((N_EXPERTS, nb_pad), 1) * MOE_BLOCK).astype(f32)
        pe = jnp.concatenate([pends] * (nb_pad // 128), axis=1)
        be = jnp.sum((pe <= bstart).astype(f32), axis=0, keepdims=True)
        be = jnp.minimum(be, float(N_EXPERTS - 1))
        blk_ref[...] = jnp.broadcast_to(be, (8, nb_pad)).astype(i32)

    ti = _iota2((tm, tm), 0)
    tj = _iota2((tm, tm), 1)
    before = (ti < tj).astype(bf16)
    prefix = jnp.dot(oh, before, preferred_element_type=f32)
    nxt = prefix + jnp.concatenate([pstart_ref[...]] * (tm // 128), axis=1)
    d1 = jnp.sum(jnp.where(is1, nxt, 0.0), axis=0, keepdims=True)
    d2 = jnp.sum(jnp.where(is2, nxt, 0.0), axis=0, keepdims=True)
    sub8 = _iota2((8, tm), 0)
    dest_ref[...] = jnp.where(sub8 == 0, d1, jnp.where(sub8 == 1, d2, 0.0)).astype(i32)
    pstart_ref[...] = pstart_ref[...] + jnp.dot(oh, jnp.ones((tm, 128), bf16), preferred_element_type=f32)


def _stage_route(cnt, ids, *, tm, nb_pad):
    n = ids.shape[1]
    return pl.pallas_call(
        functools.partial(_route_body, tm=tm, nb_pad=nb_pad),
        grid=(n // tm,),
        in_specs=[pl.BlockSpec((N_EXPERTS, 128), lambda i: (0, 0)), pl.BlockSpec((8, tm), lambda i: (0, i))],
        out_specs=[pl.BlockSpec((8, tm), lambda i: (0, i)),
                   pl.BlockSpec((8, 128), lambda i: (0, 0)),
                   pl.BlockSpec((8, nb_pad), lambda i: (0, 0))],
        out_shape=[jax.ShapeDtypeStruct((8, n), i32), jax.ShapeDtypeStruct((8, 128), i32),
                   jax.ShapeDtypeStruct((8, nb_pad), i32)],
        scratch_shapes=[pltpu.VMEM((N_EXPERTS, 128), f32)],
        compiler_params=_cparams(("arbitrary",)),
        name="moe_route",
    )(cnt, ids)


def _dispatch_body(fill_off_ref, fill_n_ref, nused_ref, dest_ref, h3_ref, xs_ref, zero_ref, sem, zsem, *, tm):
    def row_copy(t, d):
        return pltpu.make_async_copy(h3_ref.at[t], xs_ref.at[d], sem)

    def issue(t, carry):
        row_copy(t, dest_ref[0, 0, t]).start(priority=0)
        row_copy(t, dest_ref[0, 1, t]).start(priority=1)
        return carry

    lax.fori_loop(0, tm, issue, 0, unroll=8)

    @pl.when(pl.program_id(0) == 0)
    def _():
        zero_ref[...] = jnp.zeros(zero_ref.shape, bf16)

        def fill(start):
            def body(e, carry):
                off = fill_off_ref[e]
                npad = fill_n_ref[e]
                bit = MOE_BLOCK // 2
                while bit:
                    @pl.when((npad & bit) != 0)
                    def _(off=off, bit=bit):
                        cp = pltpu.make_async_copy(zero_ref.at[pl.ds(0, bit)], xs_ref.at[pl.ds(off, bit)], zsem)
                        cp.start() if start else cp.wait()
                    off = off + (npad & bit)
                    bit //= 2
                return carry
            return body

        lax.fori_loop(0, N_EXPERTS, fill(True), 0)
        lax.fori_loop(0, N_EXPERTS, fill(False), 0)

        def tail_copy(b):
            return pltpu.make_async_copy(zero_ref, xs_ref.at[pl.ds(b * MOE_BLOCK, MOE_BLOCK)], zsem)

        nblocks = xs_ref.shape[0] // MOE_BLOCK
        lax.fori_loop(nused_ref[0], nblocks, lambda b, c: (tail_copy(b).start(), c)[1], 0)
        lax.fori_loop(nused_ref[0], nblocks, lambda b, c: (tail_copy(0).wait(), c)[1], 0)

    for _ in range(2):
        pltpu.make_async_copy(h3_ref, xs_ref.at[pl.ds(0, tm)], sem).wait()


def _stage_dispatch(fill_off, fill_n, nused, dest3, hrow, p_rows, *, tm):
    n = hrow.shape[0]
    return pl.pallas_call(
        functools.partial(_dispatch_body, tm=tm),
        grid_spec=pltpu.PrefetchScalarGridSpec(
            num_scalar_prefetch=3,
            grid=(n // tm,),
            in_specs=[pl.BlockSpec((1, 2, tm), lambda i, fo, fn, nu: (i, 0, 0), memory_space=pltpu.SMEM),
                      pl.BlockSpec((tm, ROW_TILE, 128), lambda i, fo, fn, nu: (i, 0, 0))],
            out_specs=pl.BlockSpec(memory_space=pl.ANY),
            scratch_shapes=[pltpu.VMEM((MOE_BLOCK, ROW_TILE, 128), bf16),
                            pltpu.SemaphoreType.DMA, pltpu.SemaphoreType.DMA],
        ),
        out_shape=jax.ShapeDtypeStruct((p_rows, ROW_TILE, 128), bf16),
        compiler_params=_cparams(("arbitrary",)),
        name="moe_dispatch",
    )(fill_off, fill_n, nused, dest3, hrow)


def _experts_body(blk_ref, nused_ref, xs_hbm, wg_hbm, wu_hbm, wd_hbm, ys_ref,
                  xbuf_ref, wg32_ref, wu32_ref, wd32_ref, wgu16_ref, wd16_ref, xsem, wsem):
    i = pl.program_id(0)
    nused = nused_ref[0]
    used = i < nused
    e = blk_ref[i]
    slot = lax.rem(i, X_SLOTS)

    def weight_copies(ex):
        return (pltpu.make_async_copy(wg_hbm.at[ex], wg32_ref, wsem.at[0]),
                pltpu.make_async_copy(wu_hbm.at[ex], wu32_ref, wsem.at[1]),
                pltpu.make_async_copy(wd_hbm.at[ex], wd32_ref, wsem.at[2]))

    def x_copy(block, s):
        return pltpu.make_async_copy(xs_hbm.at[pl.ds(block * MOE_BLOCK, MOE_BLOCK)], xbuf_ref.at[s], xsem.at[s])

    @pl.when((i == 0) & used)
    def _():
        for cp in weight_copies(e):
            cp.start()
        for j in range(X_SLOTS - 1):
            @pl.when(j < nused)
            def _(j=j):
                x_copy(j, j).start()

    ahead = i + (X_SLOTS - 1)

    @pl.when(ahead < nused)
    def _():
        x_copy(ahead, lax.rem(ahead, X_SLOTS)).start()

    @pl.when(used & ((i == 0) | (e != blk_ref[jnp.maximum(i - 1, 0)])))
    def _():
        for cp in weight_copies(e):
            cp.wait()
        wgu16_ref[:, 0:D_EXPERT] = wg32_ref[...].astype(bf16)
        wgu16_ref[:, D_EXPERT:2 * D_EXPERT] = wu32_ref[...].astype(bf16)
        wd16_ref[...] = wd32_ref[...].astype(bf16)
        nxt = lax.while_loop(lambda j: (j < nused) & (blk_ref[jnp.minimum(j, nused - 1)] == e), lambda j: j + 1, i + 1)

        @pl.when(nxt < nused)
        def _():
            for cp in weight_copies(blk_ref[jnp.minimum(nxt, nused - 1)]):
                cp.start(priority=1)

    @pl.when(used)
    def _():
        x_copy(i, slot).wait()
        half = MOE_BLOCK // 2
        rows = [slice(p * half, (p + 1) * half) for p in range(2)]
        gu = [jnp.dot(xbuf_ref[slot, r].reshape(half, D_MODEL), wgu16_ref[...], preferred_element_type=f32)
              for r in rows]
        hid = [(_silu(g[:, :D_EXPERT]) * g[:, D_EXPERT:]).astype(bf16) for g in gu]
        y = [jnp.dot(hd, wd16_ref[...], preferred_element_type=f32) for hd in hid]
        for r, yp in zip(rows, y):
            ys_ref[r] = yp.astype(bf16).reshape(half, ROW_TILE, 128)

    @pl.when(jnp.logical_not(used))
    def _():
        ys_ref[...] = jnp.zeros(ys_ref.shape, bf16)


def _stage_experts(blk_e, nused, xs, w_gate, w_up, w_down):
    p_rows = xs.shape[0]
    nb = p_rows // MOE_BLOCK

    def row_map(i, blk, nu):
        return (i, 0, 0)

    return pl.pallas_call(
        _experts_body,
        grid_spec=pltpu.PrefetchScalarGridSpec(
            num_scalar_prefetch=2,
            grid=(nb,),
            in_specs=[pl.BlockSpec(memory_space=pl.ANY),
                      pl.BlockSpec(memory_space=pl.ANY),
                      pl.BlockSpec(memory_space=pl.ANY),
                      pl.BlockSpec(memory_space=pl.ANY)],
            out_specs=pl.BlockSpec((MOE_BLOCK, ROW_TILE, 128), row_map),
            scratch_shapes=[pltpu.VMEM((X_SLOTS, MOE_BLOCK, ROW_TILE, 128), bf16),
                            pltpu.VMEM((D_MODEL, D_EXPERT), f32), pltpu.VMEM((D_MODEL, D_EXPERT), f32),
                            pltpu.VMEM((D_EXPERT, D_MODEL), f32),
                            pltpu.VMEM((D_MODEL, 2 * D_EXPERT), bf16), pltpu.VMEM((D_EXPERT, D_MODEL), bf16),
                            pltpu.SemaphoreType.DMA((X_SLOTS,)), pltpu.SemaphoreType.DMA((3,))],
        ),
        out_shape=jax.ShapeDtypeStruct((p_rows, ROW_TILE, 128), bf16),
        compiler_params=_cparams(("arbitrary",)),
        name="moe_experts",
    )(blk_e, nused, xs, w_gate, w_up, w_down)


def _combine_body(dcur_ref, dnext_ref, h_ref, wts_ref, g2_ref, b2_ref, ys_ref, o_ref, ybuf_ref, sem, *, tm, nsteps):
    i = pl.program_id(0)
    slot = lax.rem(i, 2)

    def issue_tile(d_ref, s):
        def body(t, carry):
            for k in range(2):
                pltpu.make_async_copy(ys_ref.at[d_ref[0, k, t]], ybuf_ref.at[s, k, t], sem.at[s]).start(priority=k)
            return carry

        lax.fori_loop(0, tm, body, 0, unroll=8)

    @pl.when(i == 0)
    def _():
        issue_tile(dcur_ref, 0)

    @pl.when(i + 1 < nsteps)
    def _():
        issue_tile(dnext_ref, 1 - slot)

    for k in range(2):
        pltpu.make_async_copy(ys_ref.at[pl.ds(0, tm)], ybuf_ref.at[slot, k], sem.at[slot]).wait()

    pieces = []
    for c in range(tm // 128):
        ls = slice(c * 128, (c + 1) * 128)
        w1c = jnp.broadcast_to(wts_ref[0:1, ls], (128, 128)).T
        w2c = jnp.broadcast_to(wts_ref[1:2, ls], (128, 128)).T
        w1f = jnp.concatenate([w1c] * (D_MODEL // 128), axis=1)
        w2f = jnp.concatenate([w2c] * (D_MODEL // 128), axis=1)
        y1 = ybuf_ref[slot, 0, ls].reshape(128, D_MODEL).astype(f32)
        y2 = ybuf_ref[slot, 1, ls].reshape(128, D_MODEL).astype(f32)
        pieces.append(w1f * y1 + w2f * y2)
    ffn = jnp.concatenate(pieces, axis=0)
    hp = DEEPNORM_ALPHA * h_ref[...] + ffn
    mu = jnp.mean(hp, axis=-1, keepdims=True)
    hc = hp - mu
    var = jnp.mean(hc * hc, axis=-1, keepdims=True)
    o_ref[...] = hc * lax.rsqrt(var + LN_EPS) * g2_ref[...] + b2_ref[...]


def _stage_combine(dest3, h2, wts, g2, b2, ys, *, tm):
    n = h2.shape[0]
    nsteps = n // tm
    return pl.pallas_call(
        functools.partial(_combine_body, tm=tm, nsteps=nsteps),
        grid=(nsteps,),
        in_specs=[pl.BlockSpec((1, 2, tm), lambda i: (i, 0, 0), memory_space=pltpu.SMEM),
                  pl.BlockSpec((1, 2, tm), lambda i: (jnp.minimum(i + 1, nsteps - 1), 0, 0), memory_space=pltpu.SMEM),
                  pl.BlockSpec((tm, D_MODEL), lambda i: (i, 0)),
                  pl.BlockSpec((8, tm), lambda i: (0, i)),
                  pl.BlockSpec((1, D_MODEL), lambda i: (0, 0)),
                  pl.BlockSpec((1, D_MODEL), lambda i: (0, 0)),
                  pl.BlockSpec(memory_space=pl.ANY)],
        out_specs=pl.BlockSpec((tm, D_MODEL), lambda i: (i, 0)),
        out_shape=jax.ShapeDtypeStruct((n, D_MODEL), f32),
        scratch_shapes=[pltpu.VMEM((2, 2, tm, ROW_TILE, 128), bf16), pltpu.SemaphoreType.DMA((2,))],
        compiler_params=_cparams(("arbitrary",)),
        name="moe_combine",
    )(dest3, dest3, h2, wts, g2, b2, ys)


def _layer(h, w_in, conv_w, a_log, dt_bias, dn_norm_w, sgu_ln_g, sgu_ln_b, w_spatial, b_spatial, w_out,
           ln1_g, ln1_b, w_rg, b_rg, w_re, b_re, w_gate, w_up, w_down, ln2_g, ln2_b,
           *, tm_in, in_groups, dn_rows, dn_chunks, dn_groups, tm_mix, tm_rank, tm_disp, tm_comb):
    B, T, _ = h.shape
    n = B * T
    w_cols = _stage_weight_layout(w_in)
    decay_prm = jnp.stack([a_log, dt_bias])
    prow = jnp.broadcast_to(jnp.pad(decay_prm, ((0, 0), (DN_HEADS, 8 - 2 * DN_HEADS)))[:, :, None], (2, 8, 128))

    q, k, v, z, u, vln, gcol, grow = _stage_inproj(
        h, w_cols, conv_w, prow, sgu_ln_g[None, :], sgu_ln_b[None, :], tm=tm_in, ngroups=in_groups)
    ydn = _stage_deltanet(q, k, v, z, gcol, grow, dn_norm_w[None, :], nbr=dn_rows, nch=dn_chunks, ngroups=dn_groups)

    bsp = jnp.broadcast_to(b_spatial.T[:, :, None], (SGU_CHUNK, SGU_GROUPS, SGU_CHUNK)).reshape(SGU_CHUNK, SGU_WIDTH)
    n_logit = MOE_GROUPS + N_EXPERTS
    wrt = jnp.pad(jnp.concatenate([w_rg, w_re], axis=1).T, ((0, 128 - n_logit), (0, 0))).astype(bf16)
    brt = jnp.broadcast_to(jnp.pad(jnp.concatenate([b_rg, b_re]), (0, 128 - n_logit))[:, None], (128, 128))
    h1, hrow, ids, wts, cnt = _stage_mixout(ydn, u, vln, h, w_spatial, bsp, w_out.astype(bf16), ln1_g[None, :],
                                       ln1_b[None, :], wrt, brt, tm=tm_mix)

    p_rows = (-(-(n * 2) // MOE_BLOCK)) * MOE_BLOCK + N_EXPERTS * MOE_BLOCK
    nb = p_rows // MOE_BLOCK
    nb_pad = (-(-nb // 128)) * 128
    dest, meta, blk = _stage_route(cnt, ids, tm=tm_rank, nb_pad=nb_pad)

    h2 = h1.reshape(n, D_MODEL)
    dest_d = dest[0:2].reshape(2, n // tm_disp, tm_disp).transpose(1, 0, 2)
    xs = _stage_dispatch(meta[0, :N_EXPERTS], meta[1, :N_EXPERTS], meta[2, 0:1], dest_d, hrow, p_rows, tm=tm_disp)
    ys = _stage_experts(blk[0, :nb], meta[2, 0:1], xs, w_gate, w_up, w_down)
    dest_c = dest[0:2].reshape(2, n // tm_comb, tm_comb).transpose(1, 0, 2)
    out = _stage_combine(dest_c, h2, wts, ln2_g[None, :], ln2_b[None, :], ys, tm=tm_comb)
    return out.reshape(B, T, D_MODEL)


def kernel(x, w_in, conv_w, a_log, dt_bias, dn_norm_w, sgu_ln_g, sgu_ln_b, w_spatial, b_spatial, w_out, ln1_g, ln1_b, w_router_group, b_router_group, w_router_expert, b_router_expert, w_gate, w_up, w_down, ln2_g, ln2_b):
    h = x
    for l in range(w_in.shape[0]):
        h = _layer(h, w_in[l], conv_w[l], a_log[l], dt_bias[l], dn_norm_w[l], sgu_ln_g[l], sgu_ln_b[l],
                   w_spatial[l], b_spatial[l], w_out[l], ln1_g[l], ln1_b[l],
                   w_router_group[l], b_router_group[l], w_router_expert[l], b_router_expert[l],
                   w_gate[l], w_up[l], w_down[l], ln2_g[l], ln2_b[l],
                   tm_in=512, in_groups=1, dn_rows=8, dn_chunks=2, dn_groups=4, tm_mix=512, tm_rank=1024, tm_disp=2048, tm_comb=512)
    return h
```

```python
import functools

import jax
import jax.numpy as jnp
from jax import lax
from jax.experimental import pallas as pl
from jax.experimental.pallas import tpu as pltpu

f32 = jnp.float32
bf16 = jnp.bfloat16
i32 = jnp.int32

D_MODEL = 1024
DN_WIDTH = 512
DN_HEADS = 4
HEAD_DIM = 128
CONV_K = 4
SGU_WIDTH = 512
SGU_GROUPS = 4
SGU_CHUNK = 128
DN_CHUNK = 128
MOE_GROUPS = 8
EXPERTS_PER_GROUP = 8
N_EXPERTS = 64
D_EXPERT = 512
MOE_BLOCK = 256
IN_COLS_ALIGNED = 4 * DN_WIDTH + 2 * SGU_WIDTH + 128
X_SLOTS = 4
ROW_TILE = D_MODEL // 128
DEEPNORM_ALPHA = 2.0 ** 0.25
LN_EPS = 1e-5
RMS_EPS = 1e-6
HIGHEST = lax.Precision.HIGHEST
VMEM_LIMIT_BYTES = 56 * 1024 * 1024

NT_DIMS = (((1,), (1,)), ((), ()))


def _cparams(sem, flags=None):
    return pltpu.CompilerParams(dimension_semantics=sem, vmem_limit_bytes=VMEM_LIMIT_BYTES, flags=flags)


def _sigmoid(x):
    return 1.0 / (1.0 + jnp.exp(-x))


def _silu(x):
    h = 0.5 * x
    return h + h * jnp.tanh(h)


def _softplus(x):
    return jnp.maximum(x, 0.0) + jnp.log1p(jnp.exp(-jnp.abs(x)))


def _gelu_tanh(x):
    c = 0.7978845608028654
    return x * (0.5 * (1.0 + jnp.tanh(c * (x + 0.044715 * (x * x * x)))))


def _iota2(shape, axis):
    return lax.broadcasted_iota(i32, shape, axis)


def _weight_layout_body(w_ref, o_ref):
    qkvz = 4 * DN_WIDTH
    uv0 = qkvz + 2 * DN_HEADS
    rows = w_ref.shape[0]
    o_ref[:, 0:qkvz] = w_ref[:, 0:qkvz].astype(bf16)
    o_ref[:, qkvz:qkvz + 2 * SGU_WIDTH] = w_ref[:, uv0:uv0 + 2 * SGU_WIDTH].astype(bf16)
    ba = jnp.concatenate([w_ref[:, qkvz:uv0], jnp.zeros((rows, 128 - 2 * DN_HEADS), f32)], axis=1)
    o_ref[:, qkvz + 2 * SGU_WIDTH:IN_COLS_ALIGNED] = ba.astype(bf16)


def _stage_weight_layout(w_in):
    rows = 256
    return pl.pallas_call(
        _weight_layout_body,
        grid=(D_MODEL // rows,),
        in_specs=[pl.BlockSpec((rows, w_in.shape[1]), lambda i: (i, 0))],
        out_specs=pl.BlockSpec((rows, IN_COLS_ALIGNED), lambda i: (i, 0)),
        out_shape=jax.ShapeDtypeStruct((D_MODEL, IN_COLS_ALIGNED), bf16),
        compiler_params=_cparams(("arbitrary",)),
        name="weight_layout",
    )(w_in)


def _inproj_body(x_ref, w_ref, convw_ref, prow_ref, lng_ref, lnb_ref, ones_ref,
                 q_ref, k_ref, v_ref, z_ref, u_ref, vln_ref, gcol_ref, grow_ref, *ext_refs, tm, ngroups):
    W = DN_WIDTH
    gm = tm // ngroups
    ext = [ext_refs[3 * g:3 * g + 3] for g in range(ngroups)]

    @pl.when(pl.program_id(1) == 0)
    def _():
        for e_ref in ext[0]:
            e_ref[0:8, :] = jnp.zeros((8, W), f32)

    for g in range(ngroups):
        _inproj_rows(x_ref, w_ref, convw_ref, prow_ref, lng_ref, lnb_ref, ones_ref,
                     q_ref, k_ref, v_ref, z_ref, u_ref, vln_ref, gcol_ref, grow_ref, ext[g],
                     ext[(g + 1) % ngroups], r0=g * gm, gm=gm)


def _inproj_rows(x_ref, w_ref, convw_ref, prow_ref, lng_ref, lnb_ref, ones_ref,
                 q_ref, k_ref, v_ref, z_ref, u_ref, vln_ref, gcol_ref, grow_ref, ext, ext_next, *, r0, gm):
    W = DN_WIDTH
    rows = slice(r0, r0 + gm)
    xb = x_ref[0, rows, :].astype(bf16)
    for part, e_ref in enumerate(ext):
        e_ref[8:8 + gm, :] = jnp.dot(xb, w_ref[:, part * W:(part + 1) * W], preferred_element_type=f32)
    zc = 3 * W
    uc = zc + W
    vc = uc + SGU_WIDTH
    bc = vc + SGU_WIDTH

    def conv_silu(part):
        e_ref = ext[part]
        cs = slice(part * W, (part + 1) * W)
        y = convw_ref[3:4, cs] * e_ref[8:8 + gm, :]
        for j in range(CONV_K - 1):
            y = y + convw_ref[j:j + 1, cs] * e_ref[5 + j:5 + j + gm, :]
        ext_next[part][0:8, :] = e_ref[gm:gm + 8, :]
        return _silu(y)

    def group_sums(a):
        return jnp.dot(a.astype(bf16), ones_ref[...], preferred_element_type=f32)

    yq = conv_silu(0)
    ssq = group_sums(yq * yq)
    pba = jnp.dot(xb, w_ref[:, bc:bc + 128], preferred_element_type=f32)
    pv = jnp.dot(xb, w_ref[:, vc:vc + SGU_WIDTH], preferred_element_type=f32)
    yk = conv_silu(1)
    ssk = group_sums(yk * yk)
    pu = jnp.dot(xb, w_ref[:, uc:uc + SGU_WIDTH], preferred_element_type=f32)
    pz = jnp.dot(xb, w_ref[:, zc:zc + W], preferred_element_type=f32)
    q_ref[0, rows, :] = (yq * (lax.rsqrt(ssq + RMS_EPS) * HEAD_DIM ** -0.5)).astype(bf16)
    k_ref[0, rows, :] = (yk * lax.rsqrt(ssk + RMS_EPS)).astype(bf16)
    v_ref[0, rows, :] = conv_silu(2).astype(bf16)

    z_ref[0, rows, :] = _silu(pz).astype(bf16)

    u_ref[0, rows, :] = _gelu_tanh(pu).astype(bf16)
    pv = _gelu_tanh(pv)
    for g in range(SGU_GROUPS):
        sl = slice(g * SGU_CHUNK, (g + 1) * SGU_CHUNK)
        vg = pv[:, sl]
        mu = jnp.mean(vg, axis=-1, keepdims=True)
        vcn = vg - mu
        var = jnp.mean(vcn * vcn, axis=-1, keepdims=True)
        vln_ref[0, rows, sl] = (vcn * lax.rsqrt(var + LN_EPS) * lng_ref[:, sl] + lnb_ref[:, sl]).astype(bf16)

    lane = _iota2((DN_CHUNK, 128), 1)
    beta = _sigmoid(pba)
    lane8 = _iota2((8, DN_CHUNK), 1)
    sub8 = _iota2((8, DN_CHUNK), 0)
    for c in range(gm // DN_CHUNK):
        rs = slice(c * DN_CHUNK, (c + 1) * DN_CHUNK)
        os_ = slice(r0 + c * DN_CHUNK, r0 + (c + 1) * DN_CHUNK)
        pbat = pba[rs].T[0:8, :]
        gt = -jnp.exp(prow_ref[0]) * _softplus(pbat + prow_ref[1])
        gc = jnp.where(sub8 >= DN_HEADS, gt, 0.0)
        shift = 1
        while shift < DN_CHUNK:
            gc = gc + jnp.where(lane8 >= shift, pltpu.roll(gc, shift, axis=1), 0.0)
            shift *= 2
        grow_ref[0, :, os_] = gc
        gc_col = jnp.concatenate([gc, jnp.zeros((DN_CHUNK - 8, DN_CHUNK), f32)], axis=0).T
        gcol_ref[0, os_, :] = jnp.where(lane < DN_HEADS, beta[rs], gc_col)


def _stage_inproj(x, w_re, conv_w, prow, lng, lnb, *, tm, ngroups):
    B, T, _ = x.shape
    wcols = w_re.shape[1]
    grid = (B, T // tm)
    gi = lax.broadcasted_iota(i32, (DN_WIDTH, DN_WIDTH), 0) // 128
    gj = lax.broadcasted_iota(i32, (DN_WIDTH, DN_WIDTH), 1) // 128
    group_ones = (gi == gj).astype(bf16)
    act = lambda: jax.ShapeDtypeStruct((B, T, DN_WIDTH), bf16)
    act_spec = lambda: pl.BlockSpec((1, tm, DN_WIDTH), lambda b, t: (b, t, 0))
    const2 = lambda shp: pl.BlockSpec(shp, lambda b, t: (0, 0))
    return pl.pallas_call(
        functools.partial(_inproj_body, tm=tm, ngroups=ngroups),
        grid=grid,
        in_specs=[
            pl.BlockSpec((1, tm, D_MODEL), lambda b, t: (b, t, 0)),
            const2((D_MODEL, wcols)),
            const2((CONV_K, 3 * DN_WIDTH)),
            pl.BlockSpec((2, 8, 128), lambda b, t: (0, 0, 0)),
            const2((1, SGU_WIDTH)),
            const2((1, SGU_WIDTH)),
            const2((DN_WIDTH, DN_WIDTH)),
        ],
        out_specs=[act_spec() for _ in range(6)] + [
            pl.BlockSpec((1, tm, 128), lambda b, t: (b, t, 0)),
            pl.BlockSpec((1, 8, tm), lambda b, t: (b, 0, t)),
        ],
        out_shape=[act() for _ in range(6)] + [
            jax.ShapeDtypeStruct((B, T, 128), f32),
            jax.ShapeDtypeStruct((B, 8, T), f32),
        ],
        scratch_shapes=[pltpu.VMEM((tm // ngroups + 8, DN_WIDTH), f32) for _ in range(3 * ngroups)],
        compiler_params=_cparams(("arbitrary", "arbitrary")),
        name="inproj",
    )(x, w_re, conv_w, prow, lng, lnb, group_ones)


def _mm(a, b):
    return jnp.dot(a.astype(bf16), b.astype(bf16), preferred_element_type=f32)


def _unit_lower_inverse(nmats, ii, jj):
    n = nmats[0].shape[0]
    eye = (ii == jj).astype(f32)
    leaf = jnp.right_shift(ii, 3) == jnp.right_shift(jj, 3)
    dblk = [jnp.where(leaf, m, 0.0) for m in nmats]
    s1 = [_mm(d, d) for d in dblk]
    r1 = [eye - d for d in dblk]
    both = [_mm(s, jnp.concatenate([s, r], axis=1)) for s, r in zip(s1, r1)]
    r2 = [r + bo[:, n:] for r, bo in zip(r1, both)]
    xs = [r + _mm(bo[:, :n], r) for r, bo in zip(r2, both)]
    size = 8
    while size < n:
        lows = [slice(r + size, r + 2 * size) for r in range(0, n, 2 * size)]
        ups = [slice(r, r + size) for r in range(0, n, 2 * size)]
        rsel = _iota2((n // 2, n), 0)
        ilow = rsel + size * (jnp.right_shift(rsel, size.bit_length() - 1) + 1)
        jlow = _iota2((n // 2, n), 1)
        in_pair_upper = (jlow >= ilow - (ilow & (2 * size - 1))) & (jlow < ilow - (ilow & (size - 1)))
        zeros = jnp.zeros((size, n), f32)
        new_xs = []
        ylows = [_mm(jnp.where(in_pair_upper, jnp.concatenate([m[s] for s in lows], axis=0), 0.0), x)
                 for m, x in zip(nmats, xs)]
        yfull = [jnp.concatenate([piece for k in range(len(lows)) for piece in (zeros, y[k * size:(k + 1) * size])], axis=0)
                 for y in ylows]
        corr = [_mm(jnp.concatenate([x[s] for s in lows], axis=0), yf) for x, yf in zip(xs, yfull)]
        for x, c in zip(xs, corr):
            pieces = []
            for k, (u, l) in enumerate(zip(ups, lows)):
                pieces += [x[u], x[l] - c[k * size:(k + 1) * size]]
            new_xs.append(jnp.concatenate(pieces, axis=0))
        xs = new_xs
        size *= 2
    return xs


def _deltanet_body(q_ref, k_ref, v_ref, z_ref, gcol_ref, grow_ref, nw_ref, y_ref, s_ref, *, nbr, nch, ngroups):
    @pl.when(pl.program_id(1) == 0)
    def _():
        s_ref[...] = jnp.zeros(s_ref.shape, f32)

    per = nbr // ngroups
    for g in range(ngroups):
        _deltanet_rows(q_ref, k_ref, v_ref, z_ref, gcol_ref, grow_ref, nw_ref, y_ref, s_ref,
                       rows=range(g * per, (g + 1) * per), nch=nch)


def _deltanet_rows(q_ref, k_ref, v_ref, z_ref, gcol_ref, grow_ref, nw_ref, y_ref, s_ref, *, rows, nch):
    C = DN_CHUNK
    S = [(b, h) for b in rows for h in range(DN_HEADS)]
    P = [(b, c, h) for c in range(nch) for b, h in S]

    ii = _iota2((C, C), 0)
    jj = _iota2((C, C), 1)
    causal = ii >= jj
    rs = [slice(c * C, (c + 1) * C) for c in range(nch)]
    hs = [slice(h * HEAD_DIM, (h + 1) * HEAD_DIM) for h in range(DN_HEADS)]
    gcol = {(b, c): gcol_ref[b, rs[c], :] for b in rows for c in range(nch)}
    qh = {(b, c, h): q_ref[b, rs[c], hs[h]] for b, c, h in P}
    kh = {(b, c, h): k_ref[b, rs[c], hs[h]] for b, c, h in P}
    vh = {(b, c, h): v_ref[b, rs[c], hs[h]] for b, c, h in P}
    gc_b = {(b, c, h): jnp.broadcast_to(gcol[b, c][:, DN_HEADS + h:DN_HEADS + h + 1], (C, HEAD_DIM)) for b, c, h in P}
    beta_b = {(b, c, h): jnp.broadcast_to(gcol[b, c][:, h:h + 1], (C, HEAD_DIM)) for b, c, h in P}
    gc_r = {(b, c, h): jnp.broadcast_to(grow_ref[b, DN_HEADS + h:DN_HEADS + h + 1, rs[c]], (C, C)) for b, c, h in P}
    decay = {p: jnp.exp(jnp.where(causal, gc_b[p] - gc_r[p], -1e30)) for p in P}

    kf = {p: kh[p].astype(f32) for p in P}
    kb = {p: kf[p] * beta_b[p] for p in P}
    kk = {p: lax.dot_general(kb[p].astype(bf16), kh[p], NT_DIMS, preferred_element_type=f32) for p in P}
    a_intra = {p: lax.dot_general(qh[p], kh[p], NT_DIMS, preferred_element_type=f32) * decay[p] for p in P}
    nmat = [jnp.where(ii > jj, kk[p] * decay[p], 0.0) for p in P]
    tinv = dict(zip(P, _unit_lower_inverse(nmat, ii, jj)))

    eg = {p: jnp.exp(gc_b[p]) for p in P}
    rhs = {p: jnp.concatenate([vh[p].astype(f32) * beta_b[p], kb[p] * eg[p]], axis=1) for p in P}
    sol = {p: _mm(tinv[p], rhs[p]) for p in P}

    q_dec = {p: qh[p].astype(f32) * eg[p] for p in P}
    g_last = {p: gc_b[p][C - 1:C, :] for p in P}
    kdt = {p: (kf[p] * jnp.exp(g_last[p] - gc_b[p])).T for p in P}

    state = {(b, h): s_ref[b * DN_HEADS + h] for b, h in S}
    for c in range(nch):
        m1 = {(b, h): _mm(jnp.concatenate([sol[b, c, h][:, HEAD_DIM:], q_dec[b, c, h]], axis=0), state[b, h]) for b, h in S}
        v_new = {(b, h): sol[b, c, h][:, :HEAD_DIM] - m1[b, h][:C] for b, h in S}
        m2 = {(b, h): _mm(jnp.concatenate([a_intra[b, c, h], kdt[b, c, h]], axis=0), v_new[b, h]) for b, h in S}
        state = {(b, h): state[b, h] * jnp.exp(g_last[b, c, h]) + m2[b, h][C:] for b, h in S}
        for b, h in S:
            o = m1[b, h][C:] + m2[b, h][:C]
            rms = lax.rsqrt(jnp.mean(o * o, axis=-1, keepdims=True) + RMS_EPS)
            y_ref[b, rs[c], hs[h]] = (o * rms * nw_ref[...] * z_ref[b, rs[c], hs[h]].astype(f32)).astype(bf16)
    for b, h in S:
        s_ref[b * DN_HEADS + h] = state[b, h]


def _stage_deltanet(q, k, v, z, gcol, grow, norm_w, *, nbr, nch, ngroups):
    B, T, _ = q.shape
    tt = nch * DN_CHUNK
    act_spec = lambda: pl.BlockSpec((nbr, tt, DN_WIDTH), lambda b, t: (b, t, 0))
    return pl.pallas_call(
        functools.partial(_deltanet_body, nbr=nbr, nch=nch, ngroups=ngroups),
        grid=(B // nbr, T // tt),
        in_specs=[act_spec(), act_spec(), act_spec(), act_spec(),
                  pl.BlockSpec((nbr, tt, 128), lambda b, t: (b, t, 0)),
                  pl.BlockSpec((nbr, 8, tt), lambda b, t: (b, 0, t)),
                  pl.BlockSpec((1, HEAD_DIM), lambda b, t: (0, 0))],
        out_specs=act_spec(),
        out_shape=jax.ShapeDtypeStruct((B, T, DN_WIDTH), bf16),
        scratch_shapes=[pltpu.VMEM((nbr * DN_HEADS, HEAD_DIM, HEAD_DIM), f32)],
        compiler_params=_cparams(("arbitrary", "arbitrary")),
        name="deltanet",
    )(q, k, v, z, gcol, grow, norm_w)


def _mixout_body(ydn_ref, u_ref, vln_ref, x_hbm, ws_ref, bsp_ref, wout_ref, g1_ref, b1_ref, wrt_ref, brt_ref,
                 h_ref, hrow_ref, ids_ref, wts_ref, cnt_ref, ycat_ref, xring_ref, xsem, *, tm, nsteps):
    C = SGU_CHUNK
    step = pl.program_id(0) * pl.num_programs(1) + pl.program_id(1)
    slot = lax.rem(step, X_SLOTS)

    def x_copy(s, sl):
        return pltpu.make_async_copy(x_hbm.at[s], xring_ref.at[sl], xsem.at[sl])

    @pl.when(step == 0)
    def _():
        cnt_ref[...] = jnp.zeros(cnt_ref.shape, f32)
        for j in range(X_SLOTS - 1):
            x_copy(j, j).start()

    ahead = step + (X_SLOTS - 1)

    @pl.when(ahead < nsteps)
    def _():
        x_copy(ahead, lax.rem(ahead, X_SLOTS)).start()

    x_copy(step, slot).wait()
    ii = _iota2((C, C), 0)
    jj = _iota2((C, C), 1)
    ycat_ref[:, 0:DN_WIDTH] = ydn_ref[0]
    for g in range(SGU_GROUPS):
        gs = slice(g * C, (g + 1) * C)
        wsg = jnp.where(ii >= jj, ws_ref[g], 0.0).astype(bf16)
        for c in range(tm // C):
            rs = slice(c * C, (c + 1) * C)
            mixed = jnp.dot(wsg, vln_ref[0, rs, gs], preferred_element_type=f32) + bsp_ref[:, gs]
            ycat_ref[rs, DN_WIDTH + g * C:DN_WIDTH + (g + 1) * C] = (u_ref[0, rs, gs].astype(f32) * mixed).astype(bf16)

    RB = 128
    blocks = [slice(r, r + RB) for r in range(0, tm, RB)]
    mix = [jnp.dot(ycat_ref[rb, :], wout_ref[...], preferred_element_type=f32) for rb in blocks]
    h1s = []
    for rb, m in zip(blocks, mix):
        hp = DEEPNORM_ALPHA * xring_ref[slot, rb, :] + m
        mu = jnp.mean(hp, axis=-1, keepdims=True)
        hc = hp - mu
        var = jnp.mean(hc * hc, axis=-1, keepdims=True)
        h1 = hc * lax.rsqrt(var + LN_EPS) * g1_ref[...] + b1_ref[...]
        h_ref[0, rb, :] = h1
        h1b = h1.astype(bf16)
        hrow_ref[rb] = h1b.reshape(RB, ROW_TILE, 128)
        h1s.append(h1b)

    logit_blocks = [lax.dot_general(wrt_ref[...], hb, NT_DIMS, preferred_element_type=f32) + brt_ref[...] for hb in h1s]
    sub = _iota2((8, RB), 0)
    subf = sub.astype(f32)
    sub_e = _iota2((N_EXPERTS, RB), 0).astype(f32)
    chosen = []
    for rb, logits in zip(blocks, logit_blocks):
        gl = logits[0:8]
        gmax = jnp.max(gl, axis=0, keepdims=True)
        g_idx = jnp.min(jnp.where(gl == gmax, subf, float(MOE_GROUPS)), axis=0, keepdims=True)
        p_group = 1.0 / jnp.sum(jnp.exp(gl - gmax), axis=0, keepdims=True)
        within = jnp.zeros((8, RB), f32)
        for g in range(MOE_GROUPS):
            within = within + jnp.where(g_idx == float(g), logits[8 + 8 * g:16 + 8 * g], 0.0)
        m1 = jnp.max(within, axis=0, keepdims=True)
        i1 = jnp.min(jnp.where(within == m1, subf, float(EXPERTS_PER_GROUP)), axis=0, keepdims=True)
        rest = jnp.where(subf == i1, -jnp.inf, within)
        m2 = jnp.max(rest, axis=0, keepdims=True)
        i2 = jnp.min(jnp.where(rest == m2, subf, float(EXPERTS_PER_GROUP)), axis=0, keepdims=True)
        e = jnp.exp(m2 - m1)
        w1 = p_group / (1.0 + e)
        w2 = p_group * e / (1.0 + e)
        e1 = g_idx * float(EXPERTS_PER_GROUP) + i1
        e2 = g_idx * float(EXPERTS_PER_GROUP) + i2
        ids_ref[:, rb] = jnp.where(sub == 0, e1, jnp.where(sub == 1, e2, 0.0)).astype(i32)
        wts_ref[:, rb] = jnp.where(sub == 0, w1, jnp.where(sub == 1, w2, 0.0))
        chosen.append(((sub_e == e1).astype(f32) + (sub_e == e2).astype(f32)).astype(bf16))
    ones = jnp.ones((RB, 128), bf16)
    cnt_ref[...] = cnt_ref[...] + sum(jnp.dot(oh, ones, preferred_element_type=f32) for oh in chosen)


def _stage_mixout(ydn, u, vln, x, ws, bsp, wout, g1, b1, wrt, brt, *, tm):
    B, T, _ = x.shape
    nt = T // tm
    act_spec = lambda: pl.BlockSpec((1, tm, DN_WIDTH), lambda b, t: (b, t, 0))
    const2 = lambda shp: pl.BlockSpec(shp, lambda b, t: (0, 0))
    tok_spec = lambda: pl.BlockSpec((8, tm), lambda b, t: (0, b * nt + t))
    return pl.pallas_call(
        functools.partial(_mixout_body, tm=tm, nsteps=B * nt),
        grid=(B, nt),
        in_specs=[act_spec(), act_spec(), act_spec(),
                  pl.BlockSpec(memory_space=pl.ANY),
                  pl.BlockSpec((SGU_GROUPS, SGU_CHUNK, SGU_CHUNK), lambda b, t: (0, 0, 0)),
                  const2((SGU_CHUNK, SGU_WIDTH)),
                  const2((D_MODEL, D_MODEL)),
                  const2((1, D_MODEL)), const2((1, D_MODEL)),
                  const2((128, D_MODEL)), const2((128, 128))],
        out_specs=[pl.BlockSpec((1, tm, D_MODEL), lambda b, t: (b, t, 0)),
                   pl.BlockSpec((tm, ROW_TILE, 128), lambda b, t: (b * nt + t, 0, 0)), tok_spec(), tok_spec(),
                   const2((N_EXPERTS, 128))],
        out_shape=[jax.ShapeDtypeStruct((B, T, D_MODEL), f32),
                   jax.ShapeDtypeStruct((B * T, ROW_TILE, 128), bf16),
                   jax.ShapeDtypeStruct((8, B * T), i32),
                   jax.ShapeDtypeStruct((8, B * T), f32),
                   jax.ShapeDtypeStruct((N_EXPERTS, 128), f32)],
        scratch_shapes=[pltpu.VMEM((tm, D_MODEL), bf16), pltpu.VMEM((X_SLOTS, tm, D_MODEL), f32),
                        pltpu.SemaphoreType.DMA((X_SLOTS,))],
        compiler_params=_cparams(("arbitrary", "arbitrary")),
        name="mixout",
    )(ydn, u, vln, x.reshape(B * nt, tm, D_MODEL), ws, bsp, wout, g1, b1, wrt, brt)


def _route_body(cnt_ref, ids_ref, dest_ref, meta_ref, blk_ref, pstart_ref, *, tm, nb_pad):
    i = pl.program_id(0)
    sub = _iota2((N_EXPERTS, tm), 0)
    is1 = sub == ids_ref[0:1, :]
    is2 = sub == ids_ref[1:2, :]
    oh = (is1.astype(f32) + is2.astype(f32)).astype(bf16)

    @pl.when(i == 0)
    def _():
        cnt = cnt_ref[...]
        padded = jnp.floor((cnt + (MOE_BLOCK - 1)) * (1.0 / MOE_BLOCK)) * MOE_BLOCK
        ei = _iota2((N_EXPERTS, N_EXPERTS), 0)
        ej = _iota2((N_EXPERTS, N_EXPERTS), 1)
        pends = jnp.dot((ei >= ej).astype(f32), padded, precision=HIGHEST, preferred_element_type=f32)
        pstart = pends - padded
        pstart_ref[...] = pstart
        s64 = _iota2((N_EXPERTS, 128), 0)
        l64 = _iota2((N_EXPERTS, 128), 1)
        diag = s64 == l64
        fill_off = jnp.sum(jnp.where(diag, pstart + cnt, 0.0), axis=0, keepdims=True)
        fill_n = jnp.sum(jnp.where(diag, padded - cnt, 0.0), axis=0, keepdims=True)
        nused = pends[N_EXPERTS - 1:N_EXPERTS, :] * (1.0 / MOE_BLOCK)
        m8 = _iota2((8, 128), 0)
        meta_ref[...] = jnp.where(m8 == 0, fill_off, jnp.where(m8 == 1, fill_n, jnp.where(m8 == 2, nused, 0.0))).astype(i32)
        bstart = (_iota2((N_EXPERTS, nb_pad), 1) * MOE_BLOCK).astype(f32)
        pe = jnp.concatenate([pends] * (nb_pad // 128), axis=1)
        be = jnp.sum((pe <= bstart).astype(f32), axis=0, keepdims=True)
        be = jnp.minimum(be, float(N_EXPERTS - 1))
        blk_ref[...] = jnp.broadcast_to(be, (8, nb_pad)).astype(i32)

    ti = _iota2((tm, tm), 0)
    tj = _iota2((tm, tm), 1)
    before = (ti < tj).astype(bf16)
    prefix = jnp.dot(oh, before, preferred_element_type=f32)
    nxt = prefix + jnp.concatenate([pstart_ref[...]] * (tm // 128), axis=1)
    d1 = jnp.sum(jnp.where(is1, nxt, 0.0), axis=0, keepdims=True)
    d2 = jnp.sum(jnp.where(is2, nxt, 0.0), axis=0, keepdims=True)
    sub8 = _iota2((8, tm), 0)
    dest_ref[...] = jnp.where(sub8 == 0, d1, jnp.where(sub8 == 1, d2, 0.0)).astype(i32)
    pstart_ref[...] = pstart_ref[...] + jnp.dot(oh, jnp.ones((tm, 128), bf16), preferred_element_type=f32)


def _stage_route(cnt, ids, *, tm, nb_pad):
    n = ids.shape[1]
    return pl.pallas_call(
        functools.partial(_route_body, tm=tm, nb_pad=nb_pad),
        grid=(n // tm,),
        in_specs=[pl.BlockSpec((N_EXPERTS, 128), lambda i: (0, 0)), pl.BlockSpec((8, tm), lambda i: (0, i))],
        out_specs=[pl.BlockSpec((8, tm), lambda i: (0, i)),
                   pl.BlockSpec((8, 128), lambda i: (0, 0)),
                   pl.BlockSpec((8, nb_pad), lambda i: (0, 0))],
        out_shape=[jax.ShapeDtypeStruct((8, n), i32), jax.ShapeDtypeStruct((8, 128), i32),
                   jax.ShapeDtypeStruct((8, nb_pad), i32)],
        scratch_shapes=[pltpu.VMEM((N_EXPERTS, 128), f32)],
        compiler_params=_cparams(("arbitrary",)),
        name="moe_route",
    )(cnt, ids)


def _dispatch_body(fill_off_ref, fill_n_ref, nused_ref, dest_ref, h_hbm, xs_ref, src_ref, zero_ref, isem, sem, zsem,
                   *, tm, nsteps):
    i = pl.program_id(0)
    slot = lax.rem(i, X_SLOTS)

    def in_copy(step, s):
        return pltpu.make_async_copy(h_hbm.at[pl.ds(step * tm, tm)], src_ref.at[s], isem.at[s])

    def drain(s):
        for _ in range(2):
            pltpu.make_async_copy(src_ref.at[s], xs_ref.at[pl.ds(0, tm)], sem.at[s]).wait()

    @pl.when(i == 0)
    def _():
        for j in range(min(X_SLOTS - 1, nsteps)):
            in_copy(j, j).start()

    in_copy(i, slot).wait()

    def row_copy(t, d):
        return pltpu.make_async_copy(src_ref.at[slot, t], xs_ref.at[d], sem.at[slot])

    def issue(t, carry):
        row_copy(t, dest_ref[0, 0, t]).start(priority=0)
        row_copy(t, dest_ref[0, 1, t]).start(priority=1)
        return carry

    lax.fori_loop(0, tm, issue, 0, unroll=8)

    @pl.when(i == 0)
    def _():
        zero_ref[...] = jnp.zeros(zero_ref.shape, bf16)

        def fill(start):
            def body(e, carry):
                off = fill_off_ref[e]
                npad = fill_n_ref[e]
                bit = MOE_BLOCK // 2
                while bit:
                    @pl.when((npad & bit) != 0)
                    def _(off=off, bit=bit):
                        cp = pltpu.make_async_copy(zero_ref.at[pl.ds(0, bit)], xs_ref.at[pl.ds(off, bit)], zsem)
                        cp.start() if start else cp.wait()
                    off = off + (npad & bit)
                    bit //= 2
                return carry
            return body

        lax.fori_loop(0, N_EXPERTS, fill(True), 0)
        lax.fori_loop(0, N_EXPERTS, fill(False), 0)

        def tail_copy(b):
            return pltpu.make_async_copy(zero_ref, xs_ref.at[pl.ds(b * MOE_BLOCK, MOE_BLOCK)], zsem)

        nblocks = xs_ref.shape[0] // MOE_BLOCK
        lax.fori_loop(nused_ref[0], nblocks, lambda b, c: (tail_copy(b).start(), c)[1], 0)
        lax.fori_loop(nused_ref[0], nblocks, lambda b, c: (tail_copy(0).wait(), c)[1], 0)

    @pl.when(i >= 1)
    def _():
        drain(lax.rem(i - 1, X_SLOTS))

    ahead = i + (X_SLOTS - 1)

    @pl.when(ahead < nsteps)
    def _():
        in_copy(ahead, lax.rem(ahead, X_SLOTS)).start()

    @pl.when(i == nsteps - 1)
    def _():
        drain(slot)


def _stage_dispatch(fill_off, fill_n, nused, dest3, hrow, p_rows, *, tm):
    n = hrow.shape[0]
    nsteps = n // tm
    return pl.pallas_call(
        functools.partial(_dispatch_body, tm=tm, nsteps=nsteps),
        grid_spec=pltpu.PrefetchScalarGridSpec(
            num_scalar_prefetch=3,
            grid=(nsteps,),
            in_specs=[pl.BlockSpec((1, 2, tm), lambda i, fo, fn, nu: (i, 0, 0), memory_space=pltpu.SMEM),
                      pl.BlockSpec(memory_space=pl.ANY)],
            out_specs=pl.BlockSpec(memory_space=pl.ANY),
            scratch_shapes=[pltpu.VMEM((X_SLOTS, tm, ROW_TILE, 128), bf16),
                            pltpu.VMEM((MOE_BLOCK, ROW_TILE, 128), bf16),
                            pltpu.SemaphoreType.DMA((X_SLOTS,)), pltpu.SemaphoreType.DMA((X_SLOTS,)),
                            pltpu.SemaphoreType.DMA],
        ),
        out_shape=jax.ShapeDtypeStruct((p_rows, ROW_TILE, 128), bf16),
        compiler_params=_cparams(("arbitrary",)),
        name="moe_dispatch",
    )(fill_off, fill_n, nused, dest3, hrow)


def _experts_body(blk_ref, nused_ref, xs_hbm, wg_hbm, wu_hbm, wd_hbm, ys_ref,
                  xbuf_ref, wg32_ref, wu32_ref, wd32_ref, wgu16_ref, wd16_ref, xsem, wsem):
    i = pl.program_id(0)
    nused = nused_ref[0]
    used = i < nused
    e = blk_ref[i]
    slot = lax.rem(i, X_SLOTS)

    def weight_copies(ex):
        return (pltpu.make_async_copy(wg_hbm.at[ex], wg32_ref, wsem.at[0]),
                pltpu.make_async_copy(wu_hbm.at[ex], wu32_ref, wsem.at[1]),
                pltpu.make_async_copy(wd_hbm.at[ex], wd32_ref, wsem.at[2]))

    def x_copy(block, s):
        return pltpu.make_async_copy(xs_hbm.at[pl.ds(block * MOE_BLOCK, MOE_BLOCK)], xbuf_ref.at[s], xsem.at[s])

    @pl.when((i == 0) & used)
    def _():
        for cp in weight_copies(e):
            cp.start()
        for j in range(X_SLOTS - 1):
            @pl.when(j < nused)
            def _(j=j):
                x_copy(j, j).start()

    ahead = i + (X_SLOTS - 1)

    @pl.when(ahead < nused)
    def _():
        x_copy(ahead, lax.rem(ahead, X_SLOTS)).start()

    @pl.when(used & ((i == 0) | (e != blk_ref[jnp.maximum(i - 1, 0)])))
    def _():
        for cp in weight_copies(e):
            cp.wait()
        wgu16_ref[:, 0:D_EXPERT] = wg32_ref[...].astype(bf16)
        wgu16_ref[:, D_EXPERT:2 * D_EXPERT] = wu32_ref[...].astype(bf16)
        wd16_ref[...] = wd32_ref[...].astype(bf16)
        nxt = lax.while_loop(lambda j: (j < nused) & (blk_ref[jnp.minimum(j, nused - 1)] == e), lambda j: j + 1, i + 1)

        @pl.when(nxt < nused)
        def _():
            for cp in weight_copies(blk_ref[jnp.minimum(nxt, nused - 1)]):
                cp.start(priority=1)

    @pl.when(used)
    def _():
        x_copy(i, slot).wait()
        half = MOE_BLOCK // 2
        rows = [slice(p * half, (p + 1) * half) for p in range(2)]
        gu = [jnp.dot(xbuf_ref[slot, r].reshape(half, D_MODEL), wgu16_ref[...], preferred_element_type=f32)
              for r in rows]
        hid = [(_silu(g[:, :D_EXPERT]) * g[:, D_EXPERT:]).astype(bf16) for g in gu]
        y = [jnp.dot(hd, wd16_ref[...], preferred_element_type=f32) for hd in hid]
        for r, yp in zip(rows, y):
            ys_ref[r] = yp.astype(bf16).reshape(half, ROW_TILE, 128)

    @pl.when(jnp.logical_not(used))
    def _():
        ys_ref[...] = jnp.zeros(ys_ref.shape, bf16)


def _stage_experts(blk_e, nused, xs, w_gate, w_up, w_down):
    p_rows = xs.shape[0]
    nb = p_rows // MOE_BLOCK

    def row_map(i, blk, nu):
        return (i, 0, 0)

    return pl.pallas_call(
        _experts_body,
        grid_spec=pltpu.PrefetchScalarGridSpec(
            num_scalar_prefetch=2,
            grid=(nb,),
            in_specs=[pl.BlockSpec(memory_space=pl.ANY),
                      pl.BlockSpec(memory_space=pl.ANY),
                      pl.BlockSpec(memory_space=pl.ANY),
                      pl.BlockSpec(memory_space=pl.ANY)],
            out_specs=pl.BlockSpec((MOE_BLOCK, ROW_TILE, 128), row_map),
            scratch_shapes=[pltpu.VMEM((X_SLOTS, MOE_BLOCK, ROW_TILE, 128), bf16),
                            pltpu.VMEM((D_MODEL, D_EXPERT), f32), pltpu.VMEM((D_MODEL, D_EXPERT), f32),
                            pltpu.VMEM((D_EXPERT, D_MODEL), f32),
                            pltpu.VMEM((D_MODEL, 2 * D_EXPERT), bf16), pltpu.VMEM((D_EXPERT, D_MODEL), bf16),
                            pltpu.SemaphoreType.DMA((X_SLOTS,)), pltpu.SemaphoreType.DMA((3,))],
        ),
        out_shape=jax.ShapeDtypeStruct((p_rows, ROW_TILE, 128), bf16),
        compiler_params=_cparams(("arbitrary",)),
        name="moe_experts",
    )(blk_e, nused, xs, w_gate, w_up, w_down)


def _combine_body(dcur_ref, dnext_ref, h_ref, wts_ref, g2_ref, b2_ref, ys_ref, o_ref, ybuf_ref, sem, *, tm, nsteps):
    i = pl.program_id(0)
    slot = lax.rem(i, 2)

    def issue_tile(d_ref, s):
        def body(t, carry):
            for k in range(2):
                pltpu.make_async_copy(ys_ref.at[d_ref[0, k, t]], ybuf_ref.at[s, k, t], sem.at[s]).start(priority=k)
            return carry

        lax.fori_loop(0, tm, body, 0, unroll=8)

    @pl.when(i == 0)
    def _():
        issue_tile(dcur_ref, 0)

    @pl.when(i + 1 < nsteps)
    def _():
        issue_tile(dnext_ref, 1 - slot)

    for k in range(2):
        pltpu.make_async_copy(ys_ref.at[pl.ds(0, tm)], ybuf_ref.at[slot, k], sem.at[slot]).wait()

    pieces = []
    for c in range(tm // 128):
        ls = slice(c * 128, (c + 1) * 128)
        w1c = jnp.broadcast_to(wts_ref[0:1, ls], (128, 128)).T
        w2c = jnp.broadcast_to(wts_ref[1:2, ls], (128, 128)).T
        w1f = jnp.concatenate([w1c] * (D_MODEL // 128), axis=1)
        w2f = jnp.concatenate([w2c] * (D_MODEL // 128), axis=1)
        y1 = ybuf_ref[slot, 0, ls].reshape(128, D_MODEL).astype(f32)
        y2 = ybuf_ref[slot, 1, ls].reshape(128, D_MODEL).astype(f32)
        pieces.append(w1f * y1 + w2f * y2)
    ffn = jnp.concatenate(pieces, axis=0)
    hp = DEEPNORM_ALPHA * h_ref[...] + ffn
    mu = jnp.mean(hp, axis=-1, keepdims=True)
    hc = hp - mu
    var = jnp.mean(hc * hc, axis=-1, keepdims=True)
    o_ref[...] = hc * lax.rsqrt(var + LN_EPS) * g2_ref[...] + b2_ref[...]


def _stage_combine(dest3, h2, wts, g2, b2, ys, *, tm):
    n = h2.shape[0]
    nsteps = n // tm
    return pl.pallas_call(
        functools.partial(_combine_body, tm=tm, nsteps=nsteps),
        grid=(nsteps,),
        in_specs=[pl.BlockSpec((1, 2, tm), lambda i: (i, 0, 0), memory_space=pltpu.SMEM),
                  pl.BlockSpec((1, 2, tm), lambda i: (jnp.minimum(i + 1, nsteps - 1), 0, 0), memory_space=pltpu.SMEM),
                  pl.BlockSpec((tm, D_MODEL), lambda i: (i, 0)),
                  pl.BlockSpec((8, tm), lambda i: (0, i)),
                  pl.BlockSpec((1, D_MODEL), lambda i: (0, 0)),
                  pl.BlockSpec((1, D_MODEL), lambda i: (0, 0)),
                  pl.BlockSpec(memory_space=pl.ANY)],
        out_specs=pl.BlockSpec((tm, D_MODEL), lambda i: (i, 0)),
        out_shape=jax.ShapeDtypeStruct((n, D_MODEL), f32),
        scratch_shapes=[pltpu.VMEM((2, 2, tm, ROW_TILE, 128), bf16), pltpu.SemaphoreType.DMA((2,))],
        compiler_params=_cparams(("arbitrary",)),
        name="moe_combine",
    )(dest3, dest3, h2, wts, g2, b2, ys)


def _layer(h, w_in, conv_w, a_log, dt_bias, dn_norm_w, sgu_ln_g, sgu_ln_b, w_spatial, b_spatial, w_out,
           ln1_g, ln1_b, w_rg, b_rg, w_re, b_re, w_gate, w_up, w_down, ln2_g, ln2_b,
           *, tm_in, in_groups, dn_rows, dn_chunks, dn_groups, tm_mix, tm_rank, tm_disp, tm_comb):
    B, T, _ = h.shape
    n = B * T
    w_cols = _stage_weight_layout(w_in)
    decay_prm = jnp.stack([a_log, dt_bias])
    prow = jnp.broadcast_to(jnp.pad(decay_prm, ((0, 0), (DN_HEADS, 8 - 2 * DN_HEADS)))[:, :, None], (2, 8, 128))

    q, k, v, z, u, vln, gcol, grow = _stage_inproj(
        h, w_cols, conv_w, prow, sgu_ln_g[None, :], sgu_ln_b[None, :], tm=tm_in, ngroups=in_groups)
    ydn = _stage_deltanet(q, k, v, z, gcol, grow, dn_norm_w[None, :], nbr=dn_rows, nch=dn_chunks, ngroups=dn_groups)

    bsp = jnp.broadcast_to(b_spatial.T[:, :, None], (SGU_CHUNK, SGU_GROUPS, SGU_CHUNK)).reshape(SGU_CHUNK, SGU_WIDTH)
    n_logit = MOE_GROUPS + N_EXPERTS
    wrt = jnp.pad(jnp.concatenate([w_rg, w_re], axis=1).T, ((0, 128 - n_logit), (0, 0))).astype(bf16)
    brt = jnp.broadcast_to(jnp.pad(jnp.concatenate([b_rg, b_re]), (0, 128 - n_logit))[:, None], (128, 128))
    h1, hrow, ids, wts, cnt = _stage_mixout(ydn, u, vln, h, w_spatial, bsp, w_out.astype(bf16), ln1_g[None, :],
                                       ln1_b[None, :], wrt, brt, tm=tm_mix)

    p_rows = (-(-(n * 2) // MOE_BLOCK)) * MOE_BLOCK + N_EXPERTS * MOE_BLOCK
    nb = p_rows // MOE_BLOCK
    nb_pad = (-(-nb // 128)) * 128
    dest, meta, blk = _stage_route(cnt, ids, tm=tm_rank, nb_pad=nb_pad)

    h2 = h1.reshape(n, D_MODEL)
    dest_d = dest[0:2].reshape(2, n // tm_disp, tm_disp).transpose(1, 0, 2)
    xs = _stage_dispatch(meta[0, :N_EXPERTS], meta[1, :N_EXPERTS], meta[2, 0:1], dest_d, hrow, p_rows, tm=tm_disp)
    ys = _stage_experts(blk[0, :nb], meta[2, 0:1], xs, w_gate, w_up, w_down)
    dest_c = dest[0:2].reshape(2, n // tm_comb, tm_comb).transpose(1, 0, 2)
    out = _stage_combine(dest_c, h2, wts, ln2_g[None, :], ln2_b[None, :], ys, tm=tm_comb)
    return out.reshape(B, T, D_MODEL)


def kernel(x, w_in, conv_w, a_log, dt_bias, dn_norm_w, sgu_ln_g, sgu_ln_b, w_spatial, b_spatial, w_out, ln1_g, ln1_b, w_router_group, b_router_group, w_router_expert, b_router_expert, w_gate, w_up, w_down, ln2_g, ln2_b):
    h = x
    for l in range(w_in.shape[0]):
        h = _layer(h, w_in[l], conv_w[l], a_log[l], dt_bias[l], dn_norm_w[l], sgu_ln_g[l], sgu_ln_b[l],
                   w_spatial[l], b_spatial[l], w_out[l], ln1_g[l], ln1_b[l],
                   w_router_group[l], b_router_group[l], w_router_expert[l], b_router_expert[l],
                   w_gate[l], w_up[l], w_down[l], ln2_g[l], ln2_b[l],
                   tm_in=512, in_groups=1, dn_rows=8, dn_chunks=2, dn_groups=4, tm_mix=512, tm_rank=1024, tm_disp=2048, tm_comb=512)
    return h
```

```python
import functools

import jax
import jax.numpy as jnp
from jax import lax
from jax.experimental import pallas as pl
from jax.experimental.pallas import tpu as pltpu

f32 = jnp.float32
bf16 = jnp.bfloat16
i32 = jnp.int32

D_MODEL = 1024
DN_WIDTH = 512
DN_HEADS = 4
HEAD_DIM = 128
CONV_K = 4
SGU_WIDTH = 512
SGU_GROUPS = 4
SGU_CHUNK = 128
DN_CHUNK = 128
MOE_GROUPS = 8
EXPERTS_PER_GROUP = 8
N_EXPERTS = 64
D_EXPERT = 512
MOE_BLOCK = 256
IN_COLS_ALIGNED = 4 * DN_WIDTH + 2 * SGU_WIDTH + 128
X_SLOTS = 4
ROW_TILE = D_MODEL // 128
DEEPNORM_ALPHA = 2.0 ** 0.25
LN_EPS = 1e-5
RMS_EPS = 1e-6
HIGHEST = lax.Precision.HIGHEST
VMEM_LIMIT_BYTES = 56 * 1024 * 1024

NT_DIMS = (((1,), (1,)), ((), ()))


def _cparams(sem, flags=None):
    return pltpu.CompilerParams(dimension_semantics=sem, vmem_limit_bytes=VMEM_LIMIT_BYTES, flags=flags)


def _sigmoid(x):
    return 1.0 / (1.0 + jnp.exp(-x))


def _silu(x):
    h = 0.5 * x
    return h + h * jnp.tanh(h)


def _softplus(x):
    return jnp.maximum(x, 0.0) + jnp.log1p(jnp.exp(-jnp.abs(x)))


def _gelu_tanh(x):
    c = 0.7978845608028654
    return x * (0.5 * (1.0 + jnp.tanh(c * (x + 0.044715 * (x * x * x)))))


def _iota2(shape, axis):
    return lax.broadcasted_iota(i32, shape, axis)


def _weight_layout_body(w_ref, o_ref):
    qkvz = 4 * DN_WIDTH
    uv0 = qkvz + 2 * DN_HEADS
    rows = w_ref.shape[0]
    o_ref[:, 0:qkvz] = w_ref[:, 0:qkvz].astype(bf16)
    o_ref[:, qkvz:qkvz + 2 * SGU_WIDTH] = w_ref[:, uv0:uv0 + 2 * SGU_WIDTH].astype(bf16)
    ba = jnp.concatenate([w_ref[:, qkvz:uv0], jnp.zeros((rows, 128 - 2 * DN_HEADS), f32)], axis=1)
    o_ref[:, qkvz + 2 * SGU_WIDTH:IN_COLS_ALIGNED] = ba.astype(bf16)


def _stage_weight_layout(w_in):
    rows = 256
    return pl.pallas_call(
        _weight_layout_body,
        grid=(D_MODEL // rows,),
        in_specs=[pl.BlockSpec((rows, w_in.shape[1]), lambda i: (i, 0))],
        out_specs=pl.BlockSpec((rows, IN_COLS_ALIGNED), lambda i: (i, 0)),
        out_shape=jax.ShapeDtypeStruct((D_MODEL, IN_COLS_ALIGNED), bf16),
        compiler_params=_cparams(("arbitrary",)),
        name="weight_layout",
    )(w_in)


def _inproj_body(x_ref, w_ref, convw_ref, prow_ref, lng_ref, lnb_ref, ones_ref,
                 q_ref, k_ref, v_ref, z_ref, u_ref, vln_ref, gcol_ref, grow_ref, *ext_refs, tm, ngroups):
    W = DN_WIDTH
    gm = tm // ngroups
    ext = [ext_refs[3 * g:3 * g + 3] for g in range(ngroups)]

    @pl.when(pl.program_id(1) == 0)
    def _():
        for e_ref in ext[0]:
            e_ref[0:8, :] = jnp.zeros((8, W), f32)

    for g in range(ngroups):
        _inproj_rows(x_ref, w_ref, convw_ref, prow_ref, lng_ref, lnb_ref, ones_ref,
                     q_ref, k_ref, v_ref, z_ref, u_ref, vln_ref, gcol_ref, grow_ref, ext[g],
                     ext[(g + 1) % ngroups], r0=g * gm, gm=gm)


def _inproj_rows(x_ref, w_ref, convw_ref, prow_ref, lng_ref, lnb_ref, ones_ref,
                 q_ref, k_ref, v_ref, z_ref, u_ref, vln_ref, gcol_ref, grow_ref, ext, ext_next, *, r0, gm):
    W = DN_WIDTH
    rows = slice(r0, r0 + gm)
    xb = x_ref[0, rows, :].astype(bf16)
    for part, e_ref in enumerate(ext):
        e_ref[8:8 + gm, :] = jnp.dot(xb, w_ref[:, part * W:(part + 1) * W], preferred_element_type=f32)
    zc = 3 * W
    uc = zc + W
    vc = uc + SGU_WIDTH
    bc = vc + SGU_WIDTH

    def conv_silu(part):
        e_ref = ext[part]
        cs = slice(part * W, (part + 1) * W)
        y = convw_ref[3:4, cs] * e_ref[8:8 + gm, :]
        for j in range(CONV_K - 1):
            y = y + convw_ref[j:j + 1, cs] * e_ref[5 + j:5 + j + gm, :]
        ext_next[part][0:8, :] = e_ref[gm:gm + 8, :]
        return _silu(y)

    def group_sums(a):
        return jnp.dot(a.astype(bf16), ones_ref[...], preferred_element_type=f32)

    yq = conv_silu(0)
    ssq = group_sums(yq * yq)
    pba = jnp.dot(xb, w_ref[:, bc:bc + 128], preferred_element_type=f32)
    pv = jnp.dot(xb, w_ref[:, vc:vc + SGU_WIDTH], preferred_element_type=f32)
    yk = conv_silu(1)
    ssk = group_sums(yk * yk)
    pu = jnp.dot(xb, w_ref[:, uc:uc + SGU_WIDTH], preferred_element_type=f32)
    pz = jnp.dot(xb, w_ref[:, zc:zc + W], preferred_element_type=f32)
    q_ref[0, rows, :] = (yq * (lax.rsqrt(ssq + RMS_EPS) * HEAD_DIM ** -0.5)).astype(bf16)
    k_ref[0, rows, :] = (yk * lax.rsqrt(ssk + RMS_EPS)).astype(bf16)
    v_ref[0, rows, :] = conv_silu(2).astype(bf16)

    z_ref[0, rows, :] = _silu(pz).astype(bf16)

    u_ref[0, rows, :] = _gelu_tanh(pu).astype(bf16)
    pv = _gelu_tanh(pv)
    for g in range(SGU_GROUPS):
        sl = slice(g * SGU_CHUNK, (g + 1) * SGU_CHUNK)
        vg = pv[:, sl]
        mu = jnp.mean(vg, axis=-1, keepdims=True)
        vcn = vg - mu
        var = jnp.mean(vcn * vcn, axis=-1, keepdims=True)
        vln_ref[0, rows, sl] = (vcn * lax.rsqrt(var + LN_EPS) * lng_ref[:, sl] + lnb_ref[:, sl]).astype(bf16)

    lane = _iota2((DN_CHUNK, 128), 1)
    beta = _sigmoid(pba)
    lane8 = _iota2((8, DN_CHUNK), 1)
    sub8 = _iota2((8, DN_CHUNK), 0)
    for c in range(gm // DN_CHUNK):
        rs = slice(c * DN_CHUNK, (c + 1) * DN_CHUNK)
        os_ = slice(r0 + c * DN_CHUNK, r0 + (c + 1) * DN_CHUNK)
        pbat = pba[rs].T[0:8, :]
        gt = -jnp.exp(prow_ref[0]) * _softplus(pbat + prow_ref[1])
        gc = jnp.where(sub8 >= DN_HEADS, gt, 0.0)
        shift = 1
        while shift < DN_CHUNK:
            gc = gc + jnp.where(lane8 >= shift, pltpu.roll(gc, shift, axis=1), 0.0)
            shift *= 2
        grow_ref[0, :, os_] = gc
        gc_col = jnp.concatenate([gc, jnp.zeros((DN_CHUNK - 8, DN_CHUNK), f32)], axis=0).T
        gcol_ref[0, os_, :] = jnp.where(lane < DN_HEADS, beta[rs], gc_col)


def _stage_inproj(x, w_re, conv_w, prow, lng, lnb, *, tm, ngroups):
    B, T, _ = x.shape
    wcols = w_re.shape[1]
    grid = (B, T // tm)
    gi = lax.broadcasted_iota(i32, (DN_WIDTH, DN_WIDTH), 0) // 128
    gj = lax.broadcasted_iota(i32, (DN_WIDTH, DN_WIDTH), 1) // 128
    group_ones = (gi == gj).astype(bf16)
    act = lambda: jax.ShapeDtypeStruct((B, T, DN_WIDTH), bf16)
    act_spec = lambda: pl.BlockSpec((1, tm, DN_WIDTH), lambda b, t: (b, t, 0))
    const2 = lambda shp: pl.BlockSpec(shp, lambda b, t: (0, 0))
    return pl.pallas_call(
        functools.partial(_inproj_body, tm=tm, ngroups=ngroups),
        grid=grid,
        in_specs=[
            pl.BlockSpec((1, tm, D_MODEL), lambda b, t: (b, t, 0)),
            const2((D_MODEL, wcols)),
            const2((CONV_K, 3 * DN_WIDTH)),
            pl.BlockSpec((2, 8, 128), lambda b, t: (0, 0, 0)),
            const2((1, SGU_WIDTH)),
            const2((1, SGU_WIDTH)),
            const2((DN_WIDTH, DN_WIDTH)),
        ],
        out_specs=[act_spec() for _ in range(6)] + [
            pl.BlockSpec((1, tm, 128), lambda b, t: (b, t, 0)),
            pl.BlockSpec((1, 8, tm), lambda b, t: (b, 0, t)),
        ],
        out_shape=[act() for _ in range(6)] + [
            jax.ShapeDtypeStruct((B, T, 128), f32),
            jax.ShapeDtypeStruct((B, 8, T), f32),
        ],
        scratch_shapes=[pltpu.VMEM((tm // ngroups + 8, DN_WIDTH), f32) for _ in range(3 * ngroups)],
        compiler_params=_cparams(("arbitrary", "arbitrary")),
        name="inproj",
    )(x, w_re, conv_w, prow, lng, lnb, group_ones)


def _mm(a, b):
    return jnp.dot(a.astype(bf16), b.astype(bf16), preferred_element_type=f32)


def _unit_lower_inverse(nmats, ii, jj):
    n = nmats[0].shape[0]
    eye = (ii == jj).astype(f32)
    leaf = jnp.right_shift(ii, 3) == jnp.right_shift(jj, 3)
    dblk = [jnp.where(leaf, m, 0.0) for m in nmats]
    s1 = [_mm(d, d) for d in dblk]
    r1 = [eye - d for d in dblk]
    both = [_mm(s, jnp.concatenate([s, r], axis=1)) for s, r in zip(s1, r1)]
    r2 = [r + bo[:, n:] for r, bo in zip(r1, both)]
    xs = [r + _mm(bo[:, :n], r) for r, bo in zip(r2, both)]
    size = 8
    while size < n:
        lows = [slice(r + size, r + 2 * size) for r in range(0, n, 2 * size)]
        ups = [slice(r, r + size) for r in range(0, n, 2 * size)]
        rsel = _iota2((n // 2, n), 0)
        ilow = rsel + size * (jnp.right_shift(rsel, size.bit_length() - 1) + 1)
        jlow = _iota2((n // 2, n), 1)
        in_pair_upper = (jlow >= ilow - (ilow & (2 * size - 1))) & (jlow < ilow - (ilow & (size - 1)))
        zeros = jnp.zeros((size, n), f32)
        new_xs = []
        ylows = [_mm(jnp.where(in_pair_upper, jnp.concatenate([m[s] for s in lows], axis=0), 0.0), x)
                 for m, x in zip(nmats, xs)]
        yfull = [jnp.concatenate([piece for k in range(len(lows)) for piece in (zeros, y[k * size:(k + 1) * size])], axis=0)
                 for y in ylows]
        corr = [_mm(jnp.concatenate([x[s] for s in lows], axis=0), yf) for x, yf in zip(xs, yfull)]
        for x, c in zip(xs, corr):
            pieces = []
            for k, (u, l) in enumerate(zip(ups, lows)):
                pieces += [x[u], x[l] - c[k * size:(k + 1) * size]]
            new_xs.append(jnp.concatenate(pieces, axis=0))
        xs = new_xs
        size *= 2
    return xs


def _deltanet_body(q_ref, k_ref, v_ref, z_ref, gcol_ref, grow_ref, nw_ref, y_ref, s_ref, *, nbr, nch, ngroups):
    @pl.when(pl.program_id(1) == 0)
    def _():
        s_ref[...] = jnp.zeros(s_ref.shape, f32)

    per = nbr // ngroups
    for g in range(ngroups):
        _deltanet_rows(q_ref, k_ref, v_ref, z_ref, gcol_ref, grow_ref, nw_ref, y_ref, s_ref,
                       rows=range(g * per, (g + 1) * per), nch=nch)


def _deltanet_rows(q_ref, k_ref, v_ref, z_ref, gcol_ref, grow_ref, nw_ref, y_ref, s_ref, *, rows, nch):
    C = DN_CHUNK
    S = [(b, h) for b in rows for h in range(DN_HEADS)]
    P = [(b, c, h) for c in range(nch) for b, h in S]

    ii = _iota2((C, C), 0)
    jj = _iota2((C, C), 1)
    causal = ii >= jj
    rs = [slice(c * C, (c + 1) * C) for c in range(nch)]
    hs = [slice(h * HEAD_DIM, (h + 1) * HEAD_DIM) for h in range(DN_HEADS)]
    gcol = {(b, c): gcol_ref[b, rs[c], :] for b in rows for c in range(nch)}
    qh = {(b, c, h): q_ref[b, rs[c], hs[h]] for b, c, h in P}
    kh = {(b, c, h): k_ref[b, rs[c], hs[h]] for b, c, h in P}
    vh = {(b, c, h): v_ref[b, rs[c], hs[h]] for b, c, h in P}
    gc_b = {(b, c, h): jnp.broadcast_to(gcol[b, c][:, DN_HEADS + h:DN_HEADS + h + 1], (C, HEAD_DIM)) for b, c, h in P}
    beta_b = {(b, c, h): jnp.broadcast_to(gcol[b, c][:, h:h + 1], (C, HEAD_DIM)) for b, c, h in P}
    gc_r = {(b, c, h): jnp.broadcast_to(grow_ref[b, DN_HEADS + h:DN_HEADS + h + 1, rs[c]], (C, C)) for b, c, h in P}
    decay = {p: jnp.exp(jnp.where(causal, gc_b[p] - gc_r[p], -1e30)) for p in P}

    kf = {p: kh[p].astype(f32) for p in P}
    kb = {p: kf[p] * beta_b[p] for p in P}
    kk = {p: lax.dot_general(kb[p].astype(bf16), kh[p], NT_DIMS, preferred_element_type=f32) for p in P}
    a_intra = {p: lax.dot_general(qh[p], kh[p], NT_DIMS, preferred_element_type=f32) * decay[p] for p in P}
    nmat = [jnp.where(ii > jj, kk[p] * decay[p], 0.0) for p in P]
    tinv = dict(zip(P, _unit_lower_inverse(nmat, ii, jj)))

    eg = {p: jnp.exp(gc_b[p]) for p in P}
    rhs = {p: jnp.concatenate([vh[p].astype(f32) * beta_b[p], kb[p] * eg[p]], axis=1) for p in P}
    sol = {p: _mm(tinv[p], rhs[p]) for p in P}

    q_dec = {p: qh[p].astype(f32) * eg[p] for p in P}
    g_last = {p: gc_b[p][C - 1:C, :] for p in P}
    kdt = {p: (kf[p] * jnp.exp(g_last[p] - gc_b[p])).T for p in P}

    state = {(b, h): s_ref[b * DN_HEADS + h] for b, h in S}
    for c in range(nch):
        m1 = {(b, h): _mm(jnp.concatenate([sol[b, c, h][:, HEAD_DIM:], q_dec[b, c, h]], axis=0), state[b, h]) for b, h in S}
        v_new = {(b, h): sol[b, c, h][:, :HEAD_DIM] - m1[b, h][:C] for b, h in S}
        m2 = {(b, h): _mm(jnp.concatenate([a_intra[b, c, h], kdt[b, c, h]], axis=0), v_new[b, h]) for b, h in S}
        state = {(b, h): state[b, h] * jnp.exp(g_last[b, c, h]) + m2[b, h][C:] for b, h in S}
        for b, h in S:
            o = m1[b, h][C:] + m2[b, h][:C]
            rms = lax.rsqrt(jnp.mean(o * o, axis=-1, keepdims=True) + RMS_EPS)
            y_ref[b, rs[c], hs[h]] = (o * rms * nw_ref[...] * z_ref[b, rs[c], hs[h]].astype(f32)).astype(bf16)
    for b, h in S:
        s_ref[b * DN_HEADS + h] = state[b, h]


def _stage_deltanet(q, k, v, z, gcol, grow, norm_w, *, nbr, nch, ngroups):
    B, T, _ = q.shape
    tt = nch * DN_CHUNK
    act_spec = lambda: pl.BlockSpec((nbr, tt, DN_WIDTH), lambda b, t: (b, t, 0))
    return pl.pallas_call(
        functools.partial(_deltanet_body, nbr=nbr, nch=nch, ngroups=ngroups),
        grid=(B // nbr, T // tt),
        in_specs=[act_spec(), act_spec(), act_spec(), act_spec(),
                  pl.BlockSpec((nbr, tt, 128), lambda b, t: (b, t, 0)),
                  pl.BlockSpec((nbr, 8, tt), lambda b, t: (b, 0, t)),
                  pl.BlockSpec((1, HEAD_DIM), lambda b, t: (0, 0))],
        out_specs=act_spec(),
        out_shape=jax.ShapeDtypeStruct((B, T, DN_WIDTH), bf16),
        scratch_shapes=[pltpu.VMEM((nbr * DN_HEADS, HEAD_DIM, HEAD_DIM), f32)],
        compiler_params=_cparams(("arbitrary", "arbitrary")),
        name="deltanet",
    )(q, k, v, z, gcol, grow, norm_w)


def _mixout_body(ydn_ref, u_ref, vln_ref, x_hbm, ws_ref, bsp_ref, wout_ref, g1_ref, b1_ref, wrt_ref, brt_ref,
                 h_ref, hrow_ref, ids_ref, wts_ref, cnt_ref, ycat_ref, xring_ref, xsem, *, tm, nsteps):
    C = SGU_CHUNK
    step = pl.program_id(0) * pl.num_programs(1) + pl.program_id(1)
    slot = lax.rem(step, X_SLOTS)

    def x_copy(s, sl):
        return pltpu.make_async_copy(x_hbm.at[s], xring_ref.at[sl], xsem.at[sl])

    @pl.when(step == 0)
    def _():
        cnt_ref[...] = jnp.zeros(cnt_ref.shape, f32)
        for j in range(X_SLOTS - 1):
            x_copy(j, j).start()

    ahead = step + (X_SLOTS - 1)

    @pl.when(ahead < nsteps)
    def _():
        x_copy(ahead, lax.rem(ahead, X_SLOTS)).start()

    x_copy(step, slot).wait()
    ii = _iota2((C, C), 0)
    jj = _iota2((C, C), 1)
    ycat_ref[:, 0:DN_WIDTH] = ydn_ref[0]
    for g in range(SGU_GROUPS):
        gs = slice(g * C, (g + 1) * C)
        wsg = jnp.where(ii >= jj, ws_ref[g], 0.0).astype(bf16)
        for c in range(tm // C):
            rs = slice(c * C, (c + 1) * C)
            mixed = jnp.dot(wsg, vln_ref[0, rs, gs], preferred_element_type=f32) + bsp_ref[:, gs]
            ycat_ref[rs, DN_WIDTH + g * C:DN_WIDTH + (g + 1) * C] = (u_ref[0, rs, gs].astype(f32) * mixed).astype(bf16)

    RB = 128
    blocks = [slice(r, r + RB) for r in range(0, tm, RB)]
    mix = [jnp.dot(ycat_ref[rb, :], wout_ref[...], preferred_element_type=f32) for rb in blocks]
    h1s = []
    for rb, m in zip(blocks, mix):
        hp = DEEPNORM_ALPHA * xring_ref[slot, rb, :] + m
        mu = jnp.mean(hp, axis=-1, keepdims=True)
        hc = hp - mu
        var = jnp.mean(hc * hc, axis=-1, keepdims=True)
        h1 = hc * lax.rsqrt(var + LN_EPS) * g1_ref[...] + b1_ref[...]
        h_ref[0, rb, :] = h1
        h1b = h1.astype(bf16)
        hrow_ref[rb] = h1b.reshape(RB, ROW_TILE, 128)
        h1s.append(h1b)

    logit_blocks = [lax.dot_general(wrt_ref[...], hb, NT_DIMS, preferred_element_type=f32) + brt_ref[...] for hb in h1s]
    sub = _iota2((8, RB), 0)
    subf = sub.astype(f32)
    sub_e = _iota2((N_EXPERTS, RB), 0).astype(f32)
    chosen = []
    for rb, logits in zip(blocks, logit_blocks):
        gl = logits[0:8]
        gmax = jnp.max(gl, axis=0, keepdims=True)
        g_idx = jnp.min(jnp.where(gl == gmax, subf, float(MOE_GROUPS)), axis=0, keepdims=True)
        p_group = 1.0 / jnp.sum(jnp.exp(gl - gmax), axis=0, keepdims=True)
        within = jnp.zeros((8, RB), f32)
        for g in range(MOE_GROUPS):
            within = within + jnp.where(g_idx == float(g), logits[8 + 8 * g:16 + 8 * g], 0.0)
        m1 = jnp.max(within, axis=0, keepdims=True)
        i1 = jnp.min(jnp.where(within == m1, subf, float(EXPERTS_PER_GROUP)), axis=0, keepdims=True)
        rest = jnp.where(subf == i1, -jnp.inf, within)
        m2 = jnp.max(rest, axis=0, keepdims=True)
        i2 = jnp.min(jnp.where(rest == m2, subf, float(EXPERTS_PER_GROUP)), axis=0, keepdims=True)
        e = jnp.exp(m2 - m1)
        w1 = p_group / (1.0 + e)
        w2 = p_group * e / (1.0 + e)
        e1 = g_idx * float(EXPERTS_PER_GROUP) + i1
        e2 = g_idx * float(EXPERTS_PER_GROUP) + i2
        ids_ref[:, rb] = jnp.where(sub == 0, e1, jnp.where(sub == 1, e2, 0.0)).astype(i32)
        wts_ref[:, rb] = jnp.where(sub == 0, w1, jnp.where(sub == 1, w2, 0.0))
        chosen.append(((sub_e == e1).astype(f32) + (sub_e == e2).astype(f32)).astype(bf16))
    ones = jnp.ones((RB, 128), bf16)
    cnt_ref[...] = cnt_ref[...] + sum(jnp.dot(oh, ones, preferred_element_type=f32) for oh in chosen)


def _stage_mixout(ydn, u, vln, x, ws, bsp, wout, g1, b1, wrt, brt, *, tm):
    B, T, _ = x.shape
    nt = T // tm
    act_spec = lambda: pl.BlockSpec((1, tm, DN_WIDTH), lambda b, t: (b, t, 0))
    const2 = lambda shp: pl.BlockSpec(shp, lambda b, t: (0, 0))
    tok_spec = lambda: pl.BlockSpec((8, tm), lambda b, t: (0, b * nt + t))
    return pl.pallas_call(
        functools.partial(_mixout_body, tm=tm, nsteps=B * nt),
        grid=(B, nt),
        in_specs=[act_spec(), act_spec(), act_spec(),
                  pl.BlockSpec(memory_space=pl.ANY),
                  pl.BlockSpec((SGU_GROUPS, SGU_CHUNK, SGU_CHUNK), lambda b, t: (0, 0, 0)),
                  const2((SGU_CHUNK, SGU_WIDTH)),
                  const2((D_MODEL, D_MODEL)),
                  const2((1, D_MODEL)), const2((1, D_MODEL)),
                  const2((128, D_MODEL)), const2((128, 128))],
        out_specs=[pl.BlockSpec((1, tm, D_MODEL), lambda b, t: (b, t, 0)),
                   pl.BlockSpec((tm, ROW_TILE, 128), lambda b, t: (b * nt + t, 0, 0)), tok_spec(), tok_spec(),
                   const2((N_EXPERTS, 128))],
        out_shape=[jax.ShapeDtypeStruct((B, T, D_MODEL), f32),
                   jax.ShapeDtypeStruct((B * T, ROW_TILE, 128), bf16),
                   jax.ShapeDtypeStruct((8, B * T), i32),
                   jax.ShapeDtypeStruct((8, B * T), f32),
                   jax.ShapeDtypeStruct((N_EXPERTS, 128), f32)],
        scratch_shapes=[pltpu.VMEM((tm, D_MODEL), bf16), pltpu.VMEM((X_SLOTS, tm, D_MODEL), f32),
                        pltpu.SemaphoreType.DMA((X_SLOTS,))],
        compiler_params=_cparams(("arbitrary", "arbitrary")),
        name="mixout",
    )(ydn, u, vln, x.reshape(B * nt, tm, D_MODEL), ws, bsp, wout, g1, b1, wrt, brt)


def _route_body(cnt_ref, ids_ref, dest_ref, meta_ref, blk_ref, pstart_ref, *, tm, nb_pad):
    i = pl.program_id(0)
    sub = _iota2((N_EXPERTS, tm), 0)
    is1 = sub == ids_ref[0:1, :]
    is2 = sub == ids_ref[1:2, :]
    oh = (is1.astype(f32) + is2.astype(f32)).astype(bf16)

    @pl.when(i == 0)
    def _():
        cnt = cnt_ref[...]
        padded = jnp.floor((cnt + (MOE_BLOCK - 1)) * (1.0 / MOE_BLOCK)) * MOE_BLOCK
        ei = _iota2((N_EXPERTS, N_EXPERTS), 0)
        ej = _iota2((N_EXPERTS, N_EXPERTS), 1)
        pends = jnp.dot((ei >= ej).astype(f32), padded, precision=HIGHEST, preferred_element_type=f32)
        pstart = pends - padded
        pstart_ref[...] = pstart
        s64 = _iota2((N_EXPERTS, 128), 0)
        l64 = _iota2((N_EXPERTS, 128), 1)
        diag = s64 == l64
        fill_off = jnp.sum(jnp.where(diag, pstart + cnt, 0.0), axis=0, keepdims=True)
        fill_n = jnp.sum(jnp.where(diag, padded - cnt, 0.0), axis=0, keepdims=True)
        nused = pends[N_EXPERTS - 1:N_EXPERTS, :] * (1.0 / MOE_BLOCK)
        m8 = _iota2((8, 128), 0)
        meta_ref[...] = jnp.where(m8 == 0, fill_off, jnp.where(m8 == 1, fill_n, jnp.where(m8 == 2, nused, 0.0))).astype(i32)
        bstart = (_iota2((N_EXPERTS, nb_pad), 1) * MOE_BLOCK).astype(f32)
        pe = jnp.concatenate([pends] * (nb_pad // 128), axis=1)
        be = jnp.sum((pe <= bstart).astype(f32), axis=0, keepdims=True)
        be = jnp.minimum(be, float(N_EXPERTS - 1))
        blk_ref[...] = jnp.broadcast_to(be, (8, nb_pad)).astype(i32)

    ti = _iota2((tm, tm), 0)
    tj = _iota2((tm, tm), 1)
    before = (ti < tj).astype(bf16)
    prefix = jnp.dot(oh, before, preferred_element_type=f32)
    nxt = prefix + jnp.concatenate([pstart_ref[...]] * (tm // 128), axis=1)
    d1 = jnp.sum(jnp.where(is1, nxt, 0.0), axis=0, keepdims=True)
    d2 = jnp.sum(jnp.where(is2, nxt, 0.0), axis=0, keepdims=True)
    sub8 = _iota2((8, tm), 0)
    dest_ref[...] = jnp.where(sub8 == 0, d1, jnp.where(sub8 == 1, d2, 0.0)).astype(i32)
    pstart_ref[...] = pstart_ref[...] + jnp.dot(oh, jnp.ones((tm, 128), bf16), preferred_element_type=f32)


def _stage_route(cnt, ids, *, tm, nb_pad):
    n = ids.shape[1]
    return pl.pallas_call(
        functools.partial(_route_body, tm=tm, nb_pad=nb_pad),
        grid=(n // tm,),
        in_specs=[pl.BlockSpec((N_EXPERTS, 128), lambda i: (0, 0)), pl.BlockSpec((8, tm), lambda i: (0, i))],
        out_specs=[pl.BlockSpec((8, tm), lambda i: (0, i)),
                   pl.BlockSpec((8, 128), lambda i: (0, 0)),
                   pl.BlockSpec((8, nb_pad), lambda i: (0, 0))],
        out_shape=[jax.ShapeDtypeStruct((8, n), i32), jax.ShapeDtypeStruct((8, 128), i32),
                   jax.ShapeDtypeStruct((8, nb_pad), i32)],
        scratch_shapes=[pltpu.VMEM((N_EXPERTS, 128), f32)],
        compiler_params=_cparams(("arbitrary",)),
        name="moe_route",
    )(cnt, ids)


def _dispatch_body(fill_off_ref, fill_n_ref, nused_ref, dest_ref, h_hbm, xs_ref, src_ref, zero_ref, isem, sem, zsem,
                   *, tm, nsteps):
    i = pl.program_id(0)
    slot = lax.rem(i, X_SLOTS)

    def in_copy(step, s):
        return pltpu.make_async_copy(h_hbm.at[pl.ds(step * tm, tm)], src_ref.at[s], isem.at[s])

    def drain(s):
        for _ in range(2):
            pltpu.make_async_copy(src_ref.at[s], xs_ref.at[pl.ds(0, tm)], sem.at[s]).wait()

    @pl.when(i == 0)
    def _():
        for j in range(min(X_SLOTS - 1, nsteps)):
            in_copy(j, j).start()

    in_copy(i, slot).wait()

    def row_copy(t, d):
        return pltpu.make_async_copy(src_ref.at[slot, t], xs_ref.at[d], sem.at[slot])

    def issue(t, carry):
        row_copy(t, dest_ref[0, 0, t]).start(priority=0)
        row_copy(t, dest_ref[0, 1, t]).start(priority=1)
        return carry

    lax.fori_loop(0, tm, issue, 0, unroll=8)

    @pl.when(i == 0)
    def _():
        zero_ref[...] = jnp.zeros(zero_ref.shape, bf16)

        def fill(start):
            def body(e, carry):
                off = fill_off_ref[e]
                npad = fill_n_ref[e]
                bit = MOE_BLOCK // 2
                while bit:
                    @pl.when((npad & bit) != 0)
                    def _(off=off, bit=bit):
                        cp = pltpu.make_async_copy(zero_ref.at[pl.ds(0, bit)], xs_ref.at[pl.ds(off, bit)], zsem)
                        cp.start() if start else cp.wait()
                    off = off + (npad & bit)
                    bit //= 2
                return carry
            return body

        lax.fori_loop(0, N_EXPERTS, fill(True), 0)
        lax.fori_loop(0, N_EXPERTS, fill(False), 0)

        def tail_copy(b):
            return pltpu.make_async_copy(zero_ref, xs_ref.at[pl.ds(b * MOE_BLOCK, MOE_BLOCK)], zsem)

        nblocks = xs_ref.shape[0] // MOE_BLOCK
        lax.fori_loop(nused_ref[0], nblocks, lambda b, c: (tail_copy(b).start(), c)[1], 0)
        lax.fori_loop(nused_ref[0], nblocks, lambda b, c: (tail_copy(0).wait(), c)[1], 0)

    @pl.when(i >= 1)
    def _():
        drain(lax.rem(i - 1, X_SLOTS))

    ahead = i + (X_SLOTS - 1)

    @pl.when(ahead < nsteps)
    def _():
        in_copy(ahead, lax.rem(ahead, X_SLOTS)).start()

    @pl.when(i == nsteps - 1)
    def _():
        drain(slot)


def _stage_dispatch(fill_off, fill_n, nused, dest3, hrow, p_rows, *, tm):
    n = hrow.shape[0]
    nsteps = n // tm
    return pl.pallas_call(
        functools.partial(_dispatch_body, tm=tm, nsteps=nsteps),
        grid_spec=pltpu.PrefetchScalarGridSpec(
            num_scalar_prefetch=3,
            grid=(nsteps,),
            in_specs=[pl.BlockSpec((1, 2, tm), lambda i, fo, fn, nu: (i, 0, 0), memory_space=pltpu.SMEM),
                      pl.BlockSpec(memory_space=pl.ANY)],
            out_specs=pl.BlockSpec(memory_space=pl.ANY),
            scratch_shapes=[pltpu.VMEM((X_SLOTS, tm, ROW_TILE, 128), bf16),
                            pltpu.VMEM((MOE_BLOCK, ROW_TILE, 128), bf16),
                            pltpu.SemaphoreType.DMA((X_SLOTS,)), pltpu.SemaphoreType.DMA((X_SLOTS,)),
                            pltpu.SemaphoreType.DMA],
        ),
        out_shape=jax.ShapeDtypeStruct((p_rows, ROW_TILE, 128), bf16),
        compiler_params=_cparams(("arbitrary",)),
        name="moe_dispatch",
    )(fill_off, fill_n, nused, dest3, hrow)


def _experts_body(blk_ref, nused_ref, xs_hbm, wg_hbm, wu_hbm, wd_hbm, ys_ref,
                  xbuf_ref, wg32_ref, wu32_ref, wd32_ref, wgu16_ref, wd16_ref, par_ref, xsem, wsem):
    i = pl.program_id(0)
    nused = nused_ref[0]
    used = i < nused
    e = blk_ref[i]
    slot = lax.rem(i, X_SLOTS)

    def weight_copies(ex, q):
        return (pltpu.make_async_copy(wg_hbm.at[ex], wg32_ref.at[q], wsem.at[q, 0]),
                pltpu.make_async_copy(wu_hbm.at[ex], wu32_ref.at[q], wsem.at[q, 1]),
                pltpu.make_async_copy(wd_hbm.at[ex], wd32_ref.at[q], wsem.at[q, 2]))

    def x_copy(block, s):
        return pltpu.make_async_copy(xs_hbm.at[pl.ds(block * MOE_BLOCK, MOE_BLOCK)], xbuf_ref.at[s], xsem.at[s])

    def expert_of(j):
        return blk_ref[jnp.clip(j, 0, jnp.maximum(nused - 1, 0))]

    def end_of_run(j0):
        ex = expert_of(j0 - 1)
        return lax.while_loop(lambda j: (j < nused) & (expert_of(j) == ex), lambda j: j + 1, j0)

    def start_weights(j, q):
        @pl.when(j < nused)
        def _():
            for cp in weight_copies(expert_of(j), q):
                cp.start(priority=1)

    @pl.when((i == 0) & used)
    def _():
        for cp in weight_copies(e, 0):
            cp.start()
        for j in range(X_SLOTS - 1):
            @pl.when(j < nused)
            def _(j=j):
                x_copy(j, j).start()
        par_ref[0] = 1
        start_weights(end_of_run(1), 1)

    ahead = i + (X_SLOTS - 1)

    @pl.when(ahead < nused)
    def _():
        x_copy(ahead, lax.rem(ahead, X_SLOTS)).start()

    @pl.when(used & ((i == 0) | (e != expert_of(i - 1))))
    def _():
        q = 1 - par_ref[0]
        par_ref[0] = q
        for cp in weight_copies(e, q):
            cp.wait()
        wgu16_ref[:, 0:D_EXPERT] = wg32_ref[q].astype(bf16)
        wgu16_ref[:, D_EXPERT:2 * D_EXPERT] = wu32_ref[q].astype(bf16)
        wd16_ref[...] = wd32_ref[q].astype(bf16)
        nxt = end_of_run(i + 1)

        @pl.when(nxt < nused)
        def _():
            start_weights(end_of_run(nxt + 1), q)

    @pl.when(used)
    def _():
        x_copy(i, slot).wait()
        half = MOE_BLOCK // 2
        rows = [slice(p * half, (p + 1) * half) for p in range(2)]
        gu = [jnp.dot(xbuf_ref[slot, r].reshape(half, D_MODEL), wgu16_ref[...], preferred_element_type=f32)
              for r in rows]
        hid = [(_silu(g[:, :D_EXPERT]) * g[:, D_EXPERT:]).astype(bf16) for g in gu]
        y = [jnp.dot(hd, wd16_ref[...], preferred_element_type=f32) for hd in hid]
        for r, yp in zip(rows, y):
            ys_ref[r] = yp.astype(bf16).reshape(half, ROW_TILE, 128)

    @pl.when(jnp.logical_not(used))
    def _():
        ys_ref[...] = jnp.zeros(ys_ref.shape, bf16)


def _stage_experts(blk_e, nused, xs, w_gate, w_up, w_down):
    p_rows = xs.shape[0]
    nb = p_rows // MOE_BLOCK

    def row_map(i, blk, nu):
        return (i, 0, 0)

    return pl.pallas_call(
        _experts_body,
        grid_spec=pltpu.PrefetchScalarGridSpec(
            num_scalar_prefetch=2,
            grid=(nb,),
            in_specs=[pl.BlockSpec(memory_space=pl.ANY),
                      pl.BlockSpec(memory_space=pl.ANY),
                      pl.BlockSpec(memory_space=pl.ANY),
                      pl.BlockSpec(memory_space=pl.ANY)],
            out_specs=pl.BlockSpec((MOE_BLOCK, ROW_TILE, 128), row_map),
            scratch_shapes=[pltpu.VMEM((X_SLOTS, MOE_BLOCK, ROW_TILE, 128), bf16),
                            pltpu.VMEM((2, D_MODEL, D_EXPERT), f32), pltpu.VMEM((2, D_MODEL, D_EXPERT), f32),
                            pltpu.VMEM((2, D_EXPERT, D_MODEL), f32),
                            pltpu.VMEM((D_MODEL, 2 * D_EXPERT), bf16), pltpu.VMEM((D_EXPERT, D_MODEL), bf16),
                            pltpu.SMEM((1,), jnp.int32),
                            pltpu.SemaphoreType.DMA((X_SLOTS,)), pltpu.SemaphoreType.DMA((2, 3))],
        ),
        out_shape=jax.ShapeDtypeStruct((p_rows, ROW_TILE, 128), bf16),
        compiler_params=_cparams(("arbitrary",)),
        name="moe_experts",
    )(blk_e, nused, xs, w_gate, w_up, w_down)


def _combine_body(dcur_ref, dnext_ref, h_ref, wts_ref, g2_ref, b2_ref, ys_ref, o_ref, ybuf_ref, sem, *, tm, nsteps):
    i = pl.program_id(0)
    slot = lax.rem(i, 2)

    def issue_tile(d_ref, s):
        def body(t, carry):
            for k in range(2):
                pltpu.make_async_copy(ys_ref.at[d_ref[0, k, t]], ybuf_ref.at[s, k, t], sem.at[s]).start(priority=k)
            return carry

        lax.fori_loop(0, tm, body, 0, unroll=8)

    @pl.when(i == 0)
    def _():
        issue_tile(dcur_ref, 0)

    @pl.when(i + 1 < nsteps)
    def _():
        issue_tile(dnext_ref, 1 - slot)

    for k in range(2):
        pltpu.make_async_copy(ys_ref.at[pl.ds(0, tm)], ybuf_ref.at[slot, k], sem.at[slot]).wait()

    pieces = []
    for c in range(tm // 128):
        ls = slice(c * 128, (c + 1) * 128)
        w1c = jnp.broadcast_to(wts_ref[0:1, ls], (128, 128)).T
        w2c = jnp.broadcast_to(wts_ref[1:2, ls], (128, 128)).T
        w1f = jnp.concatenate([w1c] * (D_MODEL // 128), axis=1)
        w2f = jnp.concatenate([w2c] * (D_MODEL // 128), axis=1)
        y1 = ybuf_ref[slot, 0, ls].reshape(128, D_MODEL).astype(f32)
        y2 = ybuf_ref[slot, 1, ls].reshape(128, D_MODEL).astype(f32)
        pieces.append(w1f * y1 + w2f * y2)
    ffn = jnp.concatenate(pieces, axis=0)
    hp = DEEPNORM_ALPHA * h_ref[...] + ffn
    mu = jnp.mean(hp, axis=-1, keepdims=True)
    hc = hp - mu
    var = jnp.mean(hc * hc, axis=-1, keepdims=True)
    o_ref[...] = hc * lax.rsqrt(var + LN_EPS) * g2_ref[...] + b2_ref[...]


def _stage_combine(dest3, h2, wts, g2, b2, ys, *, tm):
    n = h2.shape[0]
    nsteps = n // tm
    return pl.pallas_call(
        functools.partial(_combine_body, tm=tm, nsteps=nsteps),
        grid=(nsteps,),
        in_specs=[pl.BlockSpec((1, 2, tm), lambda i: (i, 0, 0), memory_space=pltpu.SMEM),
                  pl.BlockSpec((1, 2, tm), lambda i: (jnp.minimum(i + 1, nsteps - 1), 0, 0), memory_space=pltpu.SMEM),
                  pl.BlockSpec((tm, D_MODEL), lambda i: (i, 0)),
                  pl.BlockSpec((8, tm), lambda i: (0, i)),
                  pl.BlockSpec((1, D_MODEL), lambda i: (0, 0)),
                  pl.BlockSpec((1, D_MODEL), lambda i: (0, 0)),
                  pl.BlockSpec(memory_space=pl.ANY)],
        out_specs=pl.BlockSpec((tm, D_MODEL), lambda i: (i, 0)),
        out_shape=jax.ShapeDtypeStruct((n, D_MODEL), f32),
        scratch_shapes=[pltpu.VMEM((2, 2, tm, ROW_TILE, 128), bf16), pltpu.SemaphoreType.DMA((2,))],
        compiler_params=_cparams(("arbitrary",)),
        name="moe_combine",
    )(dest3, dest3, h2, wts, g2, b2, ys)


def _layer(h, w_in, conv_w, a_log, dt_bias, dn_norm_w, sgu_ln_g, sgu_ln_b, w_spatial, b_spatial, w_out,
           ln1_g, ln1_b, w_rg, b_rg, w_re, b_re, w_gate, w_up, w_down, ln2_g, ln2_b,
           *, tm_in, in_groups, dn_rows, dn_chunks, dn_groups, tm_mix, tm_rank, tm_disp, tm_comb):
    B, T, _ = h.shape
    n = B * T
    w_cols = _stage_weight_layout(w_in)
    decay_prm = jnp.stack([a_log, dt_bias])
    prow = jnp.broadcast_to(jnp.pad(decay_prm, ((0, 0), (DN_HEADS, 8 - 2 * DN_HEADS)))[:, :, None], (2, 8, 128))

    q, k, v, z, u, vln, gcol, grow = _stage_inproj(
        h, w_cols, conv_w, prow, sgu_ln_g[None, :], sgu_ln_b[None, :], tm=tm_in, ngroups=in_groups)
    ydn = _stage_deltanet(q, k, v, z, gcol, grow, dn_norm_w[None, :], nbr=dn_rows, nch=dn_chunks, ngroups=dn_groups)

    bsp = jnp.broadcast_to(b_spatial.T[:, :, None], (SGU_CHUNK, SGU_GROUPS, SGU_CHUNK)).reshape(SGU_CHUNK, SGU_WIDTH)
    n_logit = MOE_GROUPS + N_EXPERTS
    wrt = jnp.pad(jnp.concatenate([w_rg, w_re], axis=1).T, ((0, 128 - n_logit), (0, 0))).astype(bf16)
    brt = jnp.broadcast_to(jnp.pad(jnp.concatenate([b_rg, b_re]), (0, 128 - n_logit))[:, None], (128, 128))
    h1, hrow, ids, wts, cnt = _stage_mixout(ydn, u, vln, h, w_spatial, bsp, w_out.astype(bf16), ln1_g[None, :],
                                       ln1_b[None, :], wrt, brt, tm=tm_mix)

    p_rows = (-(-(n * 2) // MOE_BLOCK)) * MOE_BLOCK + N_EXPERTS * MOE_BLOCK
    nb = p_rows // MOE_BLOCK
    nb_pad = (-(-nb // 128)) * 128
    dest, meta, blk = _stage_route(cnt, ids, tm=tm_rank, nb_pad=nb_pad)

    h2 = h1.reshape(n, D_MODEL)
    dest_d = dest[0:2].reshape(2, n // tm_disp, tm_disp).transpose(1, 0, 2)
    xs = _stage_dispatch(meta[0, :N_EXPERTS], meta[1, :N_EXPERTS], meta[2, 0:1], dest_d, hrow, p_rows, tm=tm_disp)
    ys = _stage_experts(blk[0, :nb], meta[2, 0:1], xs, w_gate, w_up, w_down)
    dest_c = dest[0:2].reshape(2, n // tm_comb, tm_comb).transpose(1, 0, 2)
    out = _stage_combine(dest_c, h2, wts, ln2_g[None, :], ln2_b[None, :], ys, tm=tm_comb)
    return out.reshape(B, T, D_MODEL)


def kernel(x, w_in, conv_w, a_log, dt_bias, dn_norm_w, sgu_ln_g, sgu_ln_b, w_spatial, b_spatial, w_out, ln1_g, ln1_b, w_router_group, b_router_group, w_router_expert, b_router_expert, w_gate, w_up, w_down, ln2_g, ln2_b):
    h = x
    for l in range(w_in.shape[0]):
        h = _layer(h, w_in[l], conv_w[l], a_log[l], dt_bias[l], dn_norm_w[l], sgu_ln_g[l], sgu_ln_b[l],
                   w_spatial[l], b_spatial[l], w_out[l], ln1_g[l], ln1_b[l],
                   w_router_group[l], b_router_group[l], w_router_expert[l], b_router_expert[l],
                   w_gate[l], w_up[l], w_down[l], ln2_g[l], ln2_b[l],
                   tm_in=512, in_groups=1, dn_rows=8, dn_chunks=2, dn_groups=4, tm_mix=512, tm_rank=1024, tm_disp=2048, tm_comb=512)
    return h
```

```python
import functools

import jax
import jax.numpy as jnp
from jax import lax
from jax.experimental import pallas as pl
from jax.experimental.pallas import tpu as pltpu

f32 = jnp.float32
bf16 = jnp.bfloat16
i32 = jnp.int32

D_MODEL = 1024
DN_WIDTH = 512
DN_HEADS = 4
HEAD_DIM = 128
CONV_K = 4
SGU_WIDTH = 512
SGU_GROUPS = 4
SGU_CHUNK = 128
DN_CHUNK = 128
MOE_GROUPS = 8
EXPERTS_PER_GROUP = 8
N_EXPERTS = 64
D_EXPERT = 512
MOE_BLOCK = 256
IN_COLS_ALIGNED = 4 * DN_WIDTH + 2 * SGU_WIDTH + 128
X_SLOTS = 4
ROW_TILE = D_MODEL // 128
DEEPNORM_ALPHA = 2.0 ** 0.25
LN_EPS = 1e-5
RMS_EPS = 1e-6
HIGHEST = lax.Precision.HIGHEST
VMEM_LIMIT_BYTES = 56 * 1024 * 1024

NT_DIMS = (((1,), (1,)), ((), ()))


def _cparams(sem, flags=None):
    return pltpu.CompilerParams(dimension_semantics=sem, vmem_limit_bytes=VMEM_LIMIT_BYTES, flags=flags)


def _sigmoid(x):
    return 1.0 / (1.0 + jnp.exp(-x))


def _silu(x):
    h = 0.5 * x
    return h + h * jnp.tanh(h)


def _softplus(x):
    return jnp.maximum(x, 0.0) + jnp.log1p(jnp.exp(-jnp.abs(x)))


def _gelu_tanh(x):
    c = 0.7978845608028654
    return x * (0.5 * (1.0 + jnp.tanh(c * (x + 0.044715 * (x * x * x)))))


def _iota2(shape, axis):
    return lax.broadcasted_iota(i32, shape, axis)


def _weight_layout_body(w_ref, o_ref):
    qkvz = 4 * DN_WIDTH
    uv0 = qkvz + 2 * DN_HEADS
    rows = w_ref.shape[0]
    o_ref[:, 0:qkvz] = w_ref[:, 0:qkvz].astype(bf16)
    o_ref[:, qkvz:qkvz + 2 * SGU_WIDTH] = w_ref[:, uv0:uv0 + 2 * SGU_WIDTH].astype(bf16)
    ba = jnp.concatenate([w_ref[:, qkvz:uv0], jnp.zeros((rows, 128 - 2 * DN_HEADS), f32)], axis=1)
    o_ref[:, qkvz + 2 * SGU_WIDTH:IN_COLS_ALIGNED] = ba.astype(bf16)


def _stage_weight_layout(w_in):
    rows = 256
    return pl.pallas_call(
        _weight_layout_body,
        grid=(D_MODEL // rows,),
        in_specs=[pl.BlockSpec((rows, w_in.shape[1]), lambda i: (i, 0))],
        out_specs=pl.BlockSpec((rows, IN_COLS_ALIGNED), lambda i: (i, 0)),
        out_shape=jax.ShapeDtypeStruct((D_MODEL, IN_COLS_ALIGNED), bf16),
        compiler_params=_cparams(("arbitrary",)),
        name="weight_layout",
    )(w_in)


def _inproj_body(x_ref, w_ref, convw_ref, prow_ref, lng_ref, lnb_ref, ones_ref,
                 q_ref, k_ref, v_ref, z_ref, u_ref, vln_ref, gcol_ref, grow_ref, *ext_refs, tm, ngroups):
    W = DN_WIDTH
    gm = tm // ngroups
    ext = [ext_refs[3 * g:3 * g + 3] for g in range(ngroups)]

    @pl.when(pl.program_id(1) == 0)
    def _():
        for e_ref in ext[0]:
            e_ref[0:8, :] = jnp.zeros((8, W), f32)

    for g in range(ngroups):
        _inproj_rows(x_ref, w_ref, convw_ref, prow_ref, lng_ref, lnb_ref, ones_ref,
                     q_ref, k_ref, v_ref, z_ref, u_ref, vln_ref, gcol_ref, grow_ref, ext[g],
                     ext[(g + 1) % ngroups], r0=g * gm, gm=gm)


def _inproj_rows(x_ref, w_ref, convw_ref, prow_ref, lng_ref, lnb_ref, ones_ref,
                 q_ref, k_ref, v_ref, z_ref, u_ref, vln_ref, gcol_ref, grow_ref, ext, ext_next, *, r0, gm):
    W = DN_WIDTH
    rows = slice(r0, r0 + gm)
    xb = x_ref[0, rows, :].astype(bf16)
    for part, e_ref in enumerate(ext):
        e_ref[8:8 + gm, :] = jnp.dot(xb, w_ref[:, part * W:(part + 1) * W], preferred_element_type=f32)
    zc = 3 * W
    uc = zc + W
    vc = uc + SGU_WIDTH
    bc = vc + SGU_WIDTH

    def conv_silu(part):
        e_ref = ext[part]
        cs = slice(part * W, (part + 1) * W)
        y = convw_ref[3:4, cs] * e_ref[8:8 + gm, :]
        for j in range(CONV_K - 1):
            y = y + convw_ref[j:j + 1, cs] * e_ref[5 + j:5 + j + gm, :]
        ext_next[part][0:8, :] = e_ref[gm:gm + 8, :]
        return _silu(y)

    def group_sums(a):
        return jnp.dot(a.astype(bf16), ones_ref[...], preferred_element_type=f32)

    yq = conv_silu(0)
    ssq = group_sums(yq * yq)
    pba = jnp.dot(xb, w_ref[:, bc:bc + 128], preferred_element_type=f32)
    pv = jnp.dot(xb, w_ref[:, vc:vc + SGU_WIDTH], preferred_element_type=f32)
    yk = conv_silu(1)
    ssk = group_sums(yk * yk)
    pu = jnp.dot(xb, w_ref[:, uc:uc + SGU_WIDTH], preferred_element_type=f32)
    pz = jnp.dot(xb, w_ref[:, zc:zc + W], preferred_element_type=f32)
    q_ref[0, rows, :] = (yq * (lax.rsqrt(ssq + RMS_EPS) * HEAD_DIM ** -0.5)).astype(bf16)
    k_ref[0, rows, :] = (yk * lax.rsqrt(ssk + RMS_EPS)).astype(bf16)
    v_ref[0, rows, :] = conv_silu(2).astype(bf16)

    z_ref[0, rows, :] = _silu(pz).astype(bf16)

    u_ref[0, rows, :] = _gelu_tanh(pu).astype(bf16)
    pv = _gelu_tanh(pv)
    for g in range(SGU_GROUPS):
        sl = slice(g * SGU_CHUNK, (g + 1) * SGU_CHUNK)
        vg = pv[:, sl]
        mu = jnp.mean(vg, axis=-1, keepdims=True)
        vcn = vg - mu
        var = jnp.mean(vcn * vcn, axis=-1, keepdims=True)
        vln_ref[0, rows, sl] = (vcn * lax.rsqrt(var + LN_EPS) * lng_ref[:, sl] + lnb_ref[:, sl]).astype(bf16)

    lane = _iota2((DN_CHUNK, 128), 1)
    beta = _sigmoid(pba)
    lane8 = _iota2((8, DN_CHUNK), 1)
    sub8 = _iota2((8, DN_CHUNK), 0)
    for c in range(gm // DN_CHUNK):
        rs = slice(c * DN_CHUNK, (c + 1) * DN_CHUNK)
        os_ = slice(r0 + c * DN_CHUNK, r0 + (c + 1) * DN_CHUNK)
        pbat = pba[rs].T[0:8, :]
        gt = -jnp.exp(prow_ref[0]) * _softplus(pbat + prow_ref[1])
        gc = jnp.where(sub8 >= DN_HEADS, gt, 0.0)
        shift = 1
        while shift < DN_CHUNK:
            gc = gc + jnp.where(lane8 >= shift, pltpu.roll(gc, shift, axis=1), 0.0)
            shift *= 2
        grow_ref[0, :, os_] = gc
        gc_col = jnp.concatenate([gc, jnp.zeros((DN_CHUNK - 8, DN_CHUNK), f32)], axis=0).T
        gcol_ref[0, os_, :] = jnp.where(lane < DN_HEADS, beta[rs], gc_col)


def _stage_inproj(x, w_re, conv_w, prow, lng, lnb, *, tm, ngroups):
    B, T, _ = x.shape
    wcols = w_re.shape[1]
    grid = (B, T // tm)
    gi = lax.broadcasted_iota(i32, (DN_WIDTH, DN_WIDTH), 0) // 128
    gj = lax.broadcasted_iota(i32, (DN_WIDTH, DN_WIDTH), 1) // 128
    group_ones = (gi == gj).astype(bf16)
    act = lambda: jax.ShapeDtypeStruct((B, T, DN_WIDTH), bf16)
    act_spec = lambda: pl.BlockSpec((1, tm, DN_WIDTH), lambda b, t: (b, t, 0))
    const2 = lambda shp: pl.BlockSpec(shp, lambda b, t: (0, 0))
    return pl.pallas_call(
        functools.partial(_inproj_body, tm=tm, ngroups=ngroups),
        grid=grid,
        in_specs=[
            pl.BlockSpec((1, tm, D_MODEL), lambda b, t: (b, t, 0)),
            const2((D_MODEL, wcols)),
            const2((CONV_K, 3 * DN_WIDTH)),
            pl.BlockSpec((2, 8, 128), lambda b, t: (0, 0, 0)),
            const2((1, SGU_WIDTH)),
            const2((1, SGU_WIDTH)),
            const2((DN_WIDTH, DN_WIDTH)),
        ],
        out_specs=[act_spec() for _ in range(6)] + [
            pl.BlockSpec((1, tm, 128), lambda b, t: (b, t, 0)),
            pl.BlockSpec((1, 8, tm), lambda b, t: (b, 0, t)),
        ],
        out_shape=[act() for _ in range(6)] + [
            jax.ShapeDtypeStruct((B, T, 128), f32),
            jax.ShapeDtypeStruct((B, 8, T), f32),
        ],
        scratch_shapes=[pltpu.VMEM((tm // ngroups + 8, DN_WIDTH), f32) for _ in range(3 * ngroups)],
        compiler_params=_cparams(("arbitrary", "arbitrary")),
        name="inproj",
    )(x, w_re, conv_w, prow, lng, lnb, group_ones)


def _mm(a, b):
    return jnp.dot(a.astype(bf16), b.astype(bf16), preferred_element_type=f32)


def _unit_lower_inverse(nmats, ii, jj):
    n = nmats[0].shape[0]
    eye = (ii == jj).astype(f32)
    leaf = jnp.right_shift(ii, 3) == jnp.right_shift(jj, 3)
    dblk = [jnp.where(leaf, m, 0.0) for m in nmats]
    s1 = [_mm(d, d) for d in dblk]
    r1 = [eye - d for d in dblk]
    both = [_mm(s, jnp.concatenate([s, r], axis=1)) for s, r in zip(s1, r1)]
    r2 = [r + bo[:, n:] for r, bo in zip(r1, both)]
    xs = [r + _mm(bo[:, :n], r) for r, bo in zip(r2, both)]
    size = 8
    while size < n:
        lows = [slice(r + size, r + 2 * size) for r in range(0, n, 2 * size)]
        ups = [slice(r, r + size) for r in range(0, n, 2 * size)]
        rsel = _iota2((n // 2, n), 0)
        ilow = rsel + size * (jnp.right_shift(rsel, size.bit_length() - 1) + 1)
        jlow = _iota2((n // 2, n), 1)
        in_pair_upper = (jlow >= ilow - (ilow & (2 * size - 1))) & (jlow < ilow - (ilow & (size - 1)))
        zeros = jnp.zeros((size, n), f32)
        new_xs = []
        ylows = [_mm(jnp.where(in_pair_upper, jnp.concatenate([m[s] for s in lows], axis=0), 0.0), x)
                 for m, x in zip(nmats, xs)]
        yfull = [jnp.concatenate([piece for k in range(len(lows)) for piece in (zeros, y[k * size:(k + 1) * size])], axis=0)
                 for y in ylows]
        corr = [_mm(jnp.concatenate([x[s] for s in lows], axis=0), yf) for x, yf in zip(xs, yfull)]
        for x, c in zip(xs, corr):
            pieces = []
            for k, (u, l) in enumerate(zip(ups, lows)):
                pieces += [x[u], x[l] - c[k * size:(k + 1) * size]]
            new_xs.append(jnp.concatenate(pieces, axis=0))
        xs = new_xs
        size *= 2
    return xs


def _deltanet_body(q_ref, k_ref, v_ref, z_ref, gcol_ref, grow_ref, nw_ref, y_ref, s_ref, *, nbr, nch, ngroups):
    @pl.when(pl.program_id(1) == 0)
    def _():
        s_ref[...] = jnp.zeros(s_ref.shape, f32)

    per = nbr // ngroups
    for g in range(ngroups):
        _deltanet_rows(q_ref, k_ref, v_ref, z_ref, gcol_ref, grow_ref, nw_ref, y_ref, s_ref,
                       rows=range(g * per, (g + 1) * per), nch=nch)


def _deltanet_rows(q_ref, k_ref, v_ref, z_ref, gcol_ref, grow_ref, nw_ref, y_ref, s_ref, *, rows, nch):
    C = DN_CHUNK
    S = [(b, h) for b in rows for h in range(DN_HEADS)]
    P = [(b, c, h) for c in range(nch) for b, h in S]

    ii = _iota2((C, C), 0)
    jj = _iota2((C, C), 1)
    causal = ii >= jj
    rs = [slice(c * C, (c + 1) * C) for c in range(nch)]
    hs = [slice(h * HEAD_DIM, (h + 1) * HEAD_DIM) for h in range(DN_HEADS)]
    gcol = {(b, c): gcol_ref[b, rs[c], :] for b in rows for c in range(nch)}
    qh = {(b, c, h): q_ref[b, rs[c], hs[h]] for b, c, h in P}
    kh = {(b, c, h): k_ref[b, rs[c], hs[h]] for b, c, h in P}
    vh = {(b, c, h): v_ref[b, rs[c], hs[h]] for b, c, h in P}
    gc_b = {(b, c, h): jnp.broadcast_to(gcol[b, c][:, DN_HEADS + h:DN_HEADS + h + 1], (C, HEAD_DIM)) for b, c, h in P}
    beta_b = {(b, c, h): jnp.broadcast_to(gcol[b, c][:, h:h + 1], (C, HEAD_DIM)) for b, c, h in P}
    gc_r = {(b, c, h): jnp.broadcast_to(grow_ref[b, DN_HEADS + h:DN_HEADS + h + 1, rs[c]], (C, C)) for b, c, h in P}
    decay = {p: jnp.exp(jnp.where(causal, gc_b[p] - gc_r[p], -1e30)) for p in P}

    kf = {p: kh[p].astype(f32) for p in P}
    kb = {p: kf[p] * beta_b[p] for p in P}
    kk = {p: lax.dot_general(kb[p].astype(bf16), kh[p], NT_DIMS, preferred_element_type=f32) for p in P}
    a_intra = {p: lax.dot_general(qh[p], kh[p], NT_DIMS, preferred_element_type=f32) * decay[p] for p in P}
    nmat = [jnp.where(ii > jj, kk[p] * decay[p], 0.0) for p in P]
    tinv = dict(zip(P, _unit_lower_inverse(nmat, ii, jj)))

    eg = {p: jnp.exp(gc_b[p]) for p in P}
    rhs = {p: jnp.concatenate([vh[p].astype(f32) * beta_b[p], kb[p] * eg[p]], axis=1) for p in P}
    sol = {p: _mm(tinv[p], rhs[p]) for p in P}

    q_dec = {p: qh[p].astype(f32) * eg[p] for p in P}
    g_last = {p: gc_b[p][C - 1:C, :] for p in P}
    kdt = {p: (kf[p] * jnp.exp(g_last[p] - gc_b[p])).T for p in P}

    state = {(b, h): s_ref[b * DN_HEADS + h] for b, h in S}
    for c in range(nch):
        m1 = {(b, h): _mm(jnp.concatenate([sol[b, c, h][:, HEAD_DIM:], q_dec[b, c, h]], axis=0), state[b, h]) for b, h in S}
        v_new = {(b, h): sol[b, c, h][:, :HEAD_DIM] - m1[b, h][:C] for b, h in S}
        m2 = {(b, h): _mm(jnp.concatenate([a_intra[b, c, h], kdt[b, c, h]], axis=0), v_new[b, h]) for b, h in S}
        state = {(b, h): state[b, h] * jnp.exp(g_last[b, c, h]) + m2[b, h][C:] for b, h in S}
        for b, h in S:
            o = m1[b, h][C:] + m2[b, h][:C]
            rms = lax.rsqrt(jnp.mean(o * o, axis=-1, keepdims=True) + RMS_EPS)
            y_ref[b, rs[c], hs[h]] = (o * rms * nw_ref[...] * z_ref[b, rs[c], hs[h]].astype(f32)).astype(bf16)
    for b, h in S:
        s_ref[b * DN_HEADS + h] = state[b, h]


def _stage_deltanet(q, k, v, z, gcol, grow, norm_w, *, nbr, nch, ngroups):
    B, T, _ = q.shape
    tt = nch * DN_CHUNK
    act_spec = lambda: pl.BlockSpec((nbr, tt, DN_WIDTH), lambda b, t: (b, t, 0))
    return pl.pallas_call(
        functools.partial(_deltanet_body, nbr=nbr, nch=nch, ngroups=ngroups),
        grid=(B // nbr, T // tt),
        in_specs=[act_spec(), act_spec(), act_spec(), act_spec(),
                  pl.BlockSpec((nbr, tt, 128), lambda b, t: (b, t, 0)),
                  pl.BlockSpec((nbr, 8, tt), lambda b, t: (b, 0, t)),
                  pl.BlockSpec((1, HEAD_DIM), lambda b, t: (0, 0))],
        out_specs=act_spec(),
        out_shape=jax.ShapeDtypeStruct((B, T, DN_WIDTH), bf16),
        scratch_shapes=[pltpu.VMEM((nbr * DN_HEADS, HEAD_DIM, HEAD_DIM), f32)],
        compiler_params=_cparams(("arbitrary", "arbitrary")),
        name="deltanet",
    )(q, k, v, z, gcol, grow, norm_w)


def _mixout_body(ydn_ref, u_ref, vln_ref, x_hbm, ws_ref, bsp_ref, wout_ref, g1_ref, b1_ref, wrt_ref, brt_ref,
                 h_ref, hrow_ref, ids_ref, wts_ref, cnt_ref, ycat_ref, xring_ref, xsem, *, tm, nsteps):
    C = SGU_CHUNK
    step = pl.program_id(0) * pl.num_programs(1) + pl.program_id(1)
    slot = lax.rem(step, X_SLOTS)

    def x_copy(s, sl):
        return pltpu.make_async_copy(x_hbm.at[s], xring_ref.at[sl], xsem.at[sl])

    @pl.when(step == 0)
    def _():
        cnt_ref[...] = jnp.zeros(cnt_ref.shape, f32)
        for j in range(X_SLOTS - 1):
            x_copy(j, j).start()

    ahead = step + (X_SLOTS - 1)

    @pl.when(ahead < nsteps)
    def _():
        x_copy(ahead, lax.rem(ahead, X_SLOTS)).start()

    x_copy(step, slot).wait()
    ii = _iota2((C, C), 0)
    jj = _iota2((C, C), 1)
    ycat_ref[:, 0:DN_WIDTH] = ydn_ref[0]
    for g in range(SGU_GROUPS):
        gs = slice(g * C, (g + 1) * C)
        wsg = jnp.where(ii >= jj, ws_ref[g], 0.0).astype(bf16)
        for c in range(tm // C):
            rs = slice(c * C, (c + 1) * C)
            mixed = jnp.dot(wsg, vln_ref[0, rs, gs], preferred_element_type=f32) + bsp_ref[:, gs]
            ycat_ref[rs, DN_WIDTH + g * C:DN_WIDTH + (g + 1) * C] = (u_ref[0, rs, gs].astype(f32) * mixed).astype(bf16)

    RB = 128
    blocks = [slice(r, r + RB) for r in range(0, tm, RB)]
    mix = [jnp.dot(ycat_ref[rb, :], wout_ref[...], preferred_element_type=f32) for rb in blocks]
    h1s = []
    for rb, m in zip(blocks, mix):
        hp = DEEPNORM_ALPHA * xring_ref[slot, rb, :] + m
        mu = jnp.mean(hp, axis=-1, keepdims=True)
        hc = hp - mu
        var = jnp.mean(hc * hc, axis=-1, keepdims=True)
        h1 = hc * lax.rsqrt(var + LN_EPS) * g1_ref[...] + b1_ref[...]
        h_ref[0, rb, :] = h1
        h1b = h1.astype(bf16)
        hrow_ref[rb] = h1b.reshape(RB, ROW_TILE, 128)
        h1s.append(h1b)

    logit_blocks = [lax.dot_general(wrt_ref[...], hb, NT_DIMS, preferred_element_type=f32) + brt_ref[...] for hb in h1s]
    sub = _iota2((8, RB), 0)
    subf = sub.astype(f32)
    sub_e = _iota2((N_EXPERTS, RB), 0).astype(f32)
    chosen = []
    for rb, logits in zip(blocks, logit_blocks):
        gl = logits[0:8]
        gmax = jnp.max(gl, axis=0, keepdims=True)
        g_idx = jnp.min(jnp.where(gl == gmax, subf, float(MOE_GROUPS)), axis=0, keepdims=True)
        p_group = 1.0 / jnp.sum(jnp.exp(gl - gmax), axis=0, keepdims=True)
        within = jnp.zeros((8, RB), f32)
        for g in range(MOE_GROUPS):
            within = within + jnp.where(g_idx == float(g), logits[8 + 8 * g:16 + 8 * g], 0.0)
        m1 = jnp.max(within, axis=0, keepdims=True)
        i1 = jnp.min(jnp.where(within == m1, subf, float(EXPERTS_PER_GROUP)), axis=0, keepdims=True)
        rest = jnp.where(subf == i1, -jnp.inf, within)
        m2 = jnp.max(rest, axis=0, keepdims=True)
        i2 = jnp.min(jnp.where(rest == m2, subf, float(EXPERTS_PER_GROUP)), axis=0, keepdims=True)
        e = jnp.exp(m2 - m1)
        w1 = p_group / (1.0 + e)
        w2 = p_group * e / (1.0 + e)
        e1 = g_idx * float(EXPERTS_PER_GROUP) + i1
        e2 = g_idx * float(EXPERTS_PER_GROUP) + i2
        ids_ref[:, rb] = jnp.where(sub == 0, e1, jnp.where(sub == 1, e2, 0.0)).astype(i32)
        wts_ref[:, rb] = jnp.where(sub == 0, w1, jnp.where(sub == 1, w2, 0.0))
        chosen.append(((sub_e == e1).astype(f32) + (sub_e == e2).astype(f32)).astype(bf16))
    ones = jnp.ones((RB, 128), bf16)
    cnt_ref[...] = cnt_ref[...] + sum(jnp.dot(oh, ones, preferred_element_type=f32) for oh in chosen)


def _stage_mixout(ydn, u, vln, x, ws, bsp, wout, g1, b1, wrt, brt, *, tm):
    B, T, _ = x.shape
    nt = T // tm
    act_spec = lambda: pl.BlockSpec((1, tm, DN_WIDTH), lambda b, t: (b, t, 0))
    const2 = lambda shp: pl.BlockSpec(shp, lambda b, t: (0, 0))
    tok_spec = lambda: pl.BlockSpec((8, tm), lambda b, t: (0, b * nt + t))
    return pl.pallas_call(
        functools.partial(_mixout_body, tm=tm, nsteps=B * nt),
        grid=(B, nt),
        in_specs=[act_spec(), act_spec(), act_spec(),
                  pl.BlockSpec(memory_space=pl.ANY),
                  pl.BlockSpec((SGU_GROUPS, SGU_CHUNK, SGU_CHUNK), lambda b, t: (0, 0, 0)),
                  const2((SGU_CHUNK, SGU_WIDTH)),
                  const2((D_MODEL, D_MODEL)),
                  const2((1, D_MODEL)), const2((1, D_MODEL)),
                  const2((128, D_MODEL)), const2((128, 128))],
        out_specs=[pl.BlockSpec((1, tm, D_MODEL), lambda b, t: (b, t, 0)),
                   pl.BlockSpec((tm, ROW_TILE, 128), lambda b, t: (b * nt + t, 0, 0)), tok_spec(), tok_spec(),
                   const2((N_EXPERTS, 128))],
        out_shape=[jax.ShapeDtypeStruct((B, T, D_MODEL), f32),
                   jax.ShapeDtypeStruct((B * T, ROW_TILE, 128), bf16),
                   jax.ShapeDtypeStruct((8, B * T), i32),
                   jax.ShapeDtypeStruct((8, B * T), f32),
                   jax.ShapeDtypeStruct((N_EXPERTS, 128), f32)],
        scratch_shapes=[pltpu.VMEM((tm, D_MODEL), bf16), pltpu.VMEM((X_SLOTS, tm, D_MODEL), f32),
                        pltpu.SemaphoreType.DMA((X_SLOTS,))],
        compiler_params=_cparams(("arbitrary", "arbitrary")),
        name="mixout",
    )(ydn, u, vln, x.reshape(B * nt, tm, D_MODEL), ws, bsp, wout, g1, b1, wrt, brt)


def _route_body(cnt_ref, ids_ref, dest_ref, meta_ref, blk_ref, pstart_ref, *, tm, nb_pad):
    i = pl.program_id(0)
    sub = _iota2((N_EXPERTS, tm), 0)
    is1 = sub == ids_ref[0:1, :]
    is2 = sub == ids_ref[1:2, :]
    oh = (is1.astype(f32) + is2.astype(f32)).astype(bf16)

    @pl.when(i == 0)
    def _():
        cnt = cnt_ref[...]
        padded = jnp.floor((cnt + (MOE_BLOCK - 1)) * (1.0 / MOE_BLOCK)) * MOE_BLOCK
        ei = _iota2((N_EXPERTS, N_EXPERTS), 0)
        ej = _iota2((N_EXPERTS, N_EXPERTS), 1)
        pends = jnp.dot((ei >= ej).astype(f32), padded, precision=HIGHEST, preferred_element_type=f32)
        pstart = pends - padded
        pstart_ref[...] = pstart
        s64 = _iota2((N_EXPERTS, 128), 0)
        l64 = _iota2((N_EXPERTS, 128), 1)
        diag = s64 == l64
        fill_off = jnp.sum(jnp.where(diag, pstart + cnt, 0.0), axis=0, keepdims=True)
        fill_n = jnp.sum(jnp.where(diag, padded - cnt, 0.0), axis=0, keepdims=True)
        nused = pends[N_EXPERTS - 1:N_EXPERTS, :] * (1.0 / MOE_BLOCK)
        m8 = _iota2((8, 128), 0)
        meta_ref[...] = jnp.where(m8 == 0, fill_off, jnp.where(m8 == 1, fill_n, jnp.where(m8 == 2, nused, 0.0))).astype(i32)
        bstart = (_iota2((N_EXPERTS, nb_pad), 1) * MOE_BLOCK).astype(f32)
        pe = jnp.concatenate([pends] * (nb_pad // 128), axis=1)
        be = jnp.sum((pe <= bstart).astype(f32), axis=0, keepdims=True)
        be = jnp.minimum(be, float(N_EXPERTS - 1))
        blk_ref[...] = jnp.broadcast_to(be, (8, nb_pad)).astype(i32)

    ti = _iota2((tm, tm), 0)
    tj = _iota2((tm, tm), 1)
    before = (ti < tj).astype(bf16)
    prefix = jnp.dot(oh, before, preferred_element_type=f32)
    nxt = prefix + jnp.concatenate([pstart_ref[...]] * (tm // 128), axis=1)
    d1 = jnp.sum(jnp.where(is1, nxt, 0.0), axis=0, keepdims=True)
    d2 = jnp.sum(jnp.where(is2, nxt, 0.0), axis=0, keepdims=True)
    sub8 = _iota2((8, tm), 0)
    dest_ref[...] = jnp.where(sub8 == 0, d1, jnp.where(sub8 == 1, d2, 0.0)).astype(i32)
    pstart_ref[...] = pstart_ref[...] + jnp.dot(oh, jnp.ones((tm, 128), bf16), preferred_element_type=f32)


def _stage_route(cnt, ids, *, tm, nb_pad):
    n = ids.shape[1]
    return pl.pallas_call(
        functools.partial(_route_body, tm=tm, nb_pad=nb_pad),
        grid=(n // tm,),
        in_specs=[pl.BlockSpec((N_EXPERTS, 128), lambda i: (0, 0)), pl.BlockSpec((8, tm), lambda i: (0, i))],
        out_specs=[pl.BlockSpec((8, tm), lambda i: (0, i)),
                   pl.BlockSpec((8, 128), lambda i: (0, 0)),
                   pl.BlockSpec((8, nb_pad), lambda i: (0, 0))],
        out_shape=[jax.ShapeDtypeStruct((8, n), i32), jax.ShapeDtypeStruct((8, 128), i32),
                   jax.ShapeDtypeStruct((8, nb_pad), i32)],
        scratch_shapes=[pltpu.VMEM((N_EXPERTS, 128), f32)],
        compiler_params=_cparams(("arbitrary",)),
        name="moe_route",
    )(cnt, ids)


def _dispatch_body(fill_off_ref, fill_n_ref, nused_ref, dest_ref, h_hbm, xs_ref, src_ref, zero_ref, isem, sem, zsem,
                   *, tm, nsteps):
    i = pl.program_id(0)
    slot = lax.rem(i, X_SLOTS)

    def in_copy(step, s):
        return pltpu.make_async_copy(h_hbm.at[pl.ds(step * tm, tm)], src_ref.at[s], isem.at[s])

    def drain(s):
        for _ in range(2):
            pltpu.make_async_copy(src_ref.at[s], xs_ref.at[pl.ds(0, tm)], sem.at[s]).wait()

    @pl.when(i == 0)
    def _():
        for j in range(min(X_SLOTS - 1, nsteps)):
            in_copy(j, j).start()

    in_copy(i, slot).wait()

    def row_copy(t, d):
        return pltpu.make_async_copy(src_ref.at[slot, t], xs_ref.at[d], sem.at[slot])

    def issue(t, carry):
        row_copy(t, dest_ref[0, 0, t]).start(priority=0)
        row_copy(t, dest_ref[0, 1, t]).start(priority=1)
        return carry

    lax.fori_loop(0, tm, issue, 0, unroll=8)

    @pl.when(i == 0)
    def _():
        zero_ref[...] = jnp.zeros(zero_ref.shape, bf16)

        def fill(start):
            def body(e, carry):
                off = fill_off_ref[e]
                npad = fill_n_ref[e]
                bit = MOE_BLOCK // 2
                while bit:
                    @pl.when((npad & bit) != 0)
                    def _(off=off, bit=bit):
                        cp = pltpu.make_async_copy(zero_ref.at[pl.ds(0, bit)], xs_ref.at[pl.ds(off, bit)], zsem)
                        cp.start() if start else cp.wait()
                    off = off + (npad & bit)
                    bit //= 2
                return carry
            return body

        lax.fori_loop(0, N_EXPERTS, fill(True), 0)
        lax.fori_loop(0, N_EXPERTS, fill(False), 0)

        def tail_copy(b):
            return pltpu.make_async_copy(zero_ref, xs_ref.at[pl.ds(b * MOE_BLOCK, MOE_BLOCK)], zsem)

        nblocks = xs_ref.shape[0] // MOE_BLOCK
        lax.fori_loop(nused_ref[0], nblocks, lambda b, c: (tail_copy(b).start(), c)[1], 0)
        lax.fori_loop(nused_ref[0], nblocks, lambda b, c: (tail_copy(0).wait(), c)[1], 0)

    @pl.when(i >= 1)
    def _():
        drain(lax.rem(i - 1, X_SLOTS))

    ahead = i + (X_SLOTS - 1)

    @pl.when(ahead < nsteps)
    def _():
        in_copy(ahead, lax.rem(ahead, X_SLOTS)).start()

    @pl.when(i == nsteps - 1)
    def _():
        drain(slot)


def _stage_dispatch(fill_off, fill_n, nused, dest3, hrow, p_rows, *, tm):
    n = hrow.shape[0]
    nsteps = n // tm
    return pl.pallas_call(
        functools.partial(_dispatch_body, tm=tm, nsteps=nsteps),
        grid_spec=pltpu.PrefetchScalarGridSpec(
            num_scalar_prefetch=3,
            grid=(nsteps,),
            in_specs=[pl.BlockSpec((1, 2, tm), lambda i, fo, fn, nu: (i, 0, 0), memory_space=pltpu.SMEM),
                      pl.BlockSpec(memory_space=pl.ANY)],
            out_specs=pl.BlockSpec(memory_space=pl.ANY),
            scratch_shapes=[pltpu.VMEM((X_SLOTS, tm, ROW_TILE, 128), bf16),
                            pltpu.VMEM((MOE_BLOCK, ROW_TILE, 128), bf16),
                            pltpu.SemaphoreType.DMA((X_SLOTS,)), pltpu.SemaphoreType.DMA((X_SLOTS,)),
                            pltpu.SemaphoreType.DMA],
        ),
        out_shape=jax.ShapeDtypeStruct((p_rows, ROW_TILE, 128), bf16),
        compiler_params=_cparams(("arbitrary",)),
        name="moe_dispatch",
    )(fill_off, fill_n, nused, dest3, hrow)


def _experts_body(blk_ref, nused_ref, xs_hbm, wg_hbm, wu_hbm, wd_hbm, ys_hbm,
                  xbuf_ref, wg32_ref, wu32_ref, wd32_ref, wgu16_ref, wd16_ref, obuf_ref, xsem, wsem, osem,
                  *, nsteps):
    i = pl.program_id(0)
    nused = nused_ref[0]
    used = i < nused
    e = blk_ref[i]
    slot = lax.rem(i, X_SLOTS)

    def weight_copies(ex):
        return (pltpu.make_async_copy(wg_hbm.at[ex], wg32_ref, wsem.at[0]),
                pltpu.make_async_copy(wu_hbm.at[ex], wu32_ref, wsem.at[1]),
                pltpu.make_async_copy(wd_hbm.at[ex], wd32_ref, wsem.at[2]))

    def x_copy(block, s):
        return pltpu.make_async_copy(xs_hbm.at[pl.ds(block * MOE_BLOCK, MOE_BLOCK)], xbuf_ref.at[s], xsem.at[s])

    @pl.when((i == 0) & used)
    def _():
        for cp in weight_copies(e):
            cp.start()
        for j in range(X_SLOTS - 1):
            @pl.when(j < nused)
            def _(j=j):
                x_copy(j, j).start()

    ahead = i + (X_SLOTS - 1)

    @pl.when(ahead < nused)
    def _():
        x_copy(ahead, lax.rem(ahead, X_SLOTS)).start()

    @pl.when(used & ((i == 0) | (e != blk_ref[jnp.maximum(i - 1, 0)])))
    def _():
        for cp in weight_copies(e):
            cp.wait()
        wgu16_ref[:, 0:D_EXPERT] = wg32_ref[...].astype(bf16)
        wgu16_ref[:, D_EXPERT:2 * D_EXPERT] = wu32_ref[...].astype(bf16)
        wd16_ref[...] = wd32_ref[...].astype(bf16)
        nxt = lax.while_loop(lambda j: (j < nused) & (blk_ref[jnp.minimum(j, nused - 1)] == e), lambda j: j + 1, i + 1)

        @pl.when(nxt < nused)
        def _():
            for cp in weight_copies(blk_ref[jnp.minimum(nxt, nused - 1)]):
                cp.start(priority=1)

    def out_copy(block, s):
        return pltpu.make_async_copy(obuf_ref.at[s], ys_hbm.at[pl.ds(block * MOE_BLOCK, MOE_BLOCK)], osem.at[s])

    @pl.when(i >= X_SLOTS)
    def _():
        out_copy(i - X_SLOTS, slot).wait()

    @pl.when(used)
    def _():
        x_copy(i, slot).wait()
        half = MOE_BLOCK // 2
        rows = [slice(p * half, (p + 1) * half) for p in range(2)]
        gu = [jnp.dot(xbuf_ref[slot, r].reshape(half, D_MODEL), wgu16_ref[...], preferred_element_type=f32)
              for r in rows]
        hid = [(_silu(g[:, :D_EXPERT]) * g[:, D_EXPERT:]).astype(bf16) for g in gu]
        y = [jnp.dot(hd, wd16_ref[...], preferred_element_type=f32) for hd in hid]
        for r, yp in zip(rows, y):
            obuf_ref[slot, r] = yp.astype(bf16).reshape(half, ROW_TILE, 128)

    @pl.when(jnp.logical_not(used))
    def _():
        obuf_ref[slot] = jnp.zeros(obuf_ref.shape[1:], bf16)

    out_copy(i, slot).start()

    @pl.when(i == nsteps - 1)
    def _():
        for step in range(nsteps - X_SLOTS, nsteps):
            out_copy(step, step % X_SLOTS).wait()


def _stage_experts(blk_e, nused, xs, w_gate, w_up, w_down):
    p_rows = xs.shape[0]
    nb = p_rows // MOE_BLOCK
    assert nb >= X_SLOTS

    return pl.pallas_call(
        functools.partial(_experts_body, nsteps=nb),
        grid_spec=pltpu.PrefetchScalarGridSpec(
            num_scalar_prefetch=2,
            grid=(nb,),
            in_specs=[pl.BlockSpec(memory_space=pl.ANY),
                      pl.BlockSpec(memory_space=pl.ANY),
                      pl.BlockSpec(memory_space=pl.ANY),
                      pl.BlockSpec(memory_space=pl.ANY)],
            out_specs=pl.BlockSpec(memory_space=pl.ANY),
            scratch_shapes=[pltpu.VMEM((X_SLOTS, MOE_BLOCK, ROW_TILE, 128), bf16),
                            pltpu.VMEM((D_MODEL, D_EXPERT), f32), pltpu.VMEM((D_MODEL, D_EXPERT), f32),
                            pltpu.VMEM((D_EXPERT, D_MODEL), f32),
                            pltpu.VMEM((D_MODEL, 2 * D_EXPERT), bf16), pltpu.VMEM((D_EXPERT, D_MODEL), bf16),
                            pltpu.VMEM((X_SLOTS, MOE_BLOCK, ROW_TILE, 128), bf16),
                            pltpu.SemaphoreType.DMA((X_SLOTS,)), pltpu.SemaphoreType.DMA((3,)),
                            pltpu.SemaphoreType.DMA((X_SLOTS,))],
        ),
        out_shape=jax.ShapeDtypeStruct((p_rows, ROW_TILE, 128), bf16),
        compiler_params=_cparams(("arbitrary",)),
        name="moe_experts",
    )(blk_e, nused, xs, w_gate, w_up, w_down)


def _combine_body(dcur_ref, dnext_ref, h_ref, wts_ref, g2_ref, b2_ref, ys_ref, o_ref, ybuf_ref, sem, *, tm, nsteps):
    i = pl.program_id(0)
    slot = lax.rem(i, 2)

    def issue_tile(d_ref, s):
        def body(t, carry):
            for k in range(2):
                pltpu.make_async_copy(ys_ref.at[d_ref[0, k, t]], ybuf_ref.at[s, k, t], sem.at[s]).start(priority=k)
            return carry

        lax.fori_loop(0, tm, body, 0, unroll=8)

    @pl.when(i == 0)
    def _():
        issue_tile(dcur_ref, 0)

    @pl.when(i + 1 < nsteps)
    def _():
        issue_tile(dnext_ref, 1 - slot)

    for k in range(2):
        pltpu.make_async_copy(ys_ref.at[pl.ds(0, tm)], ybuf_ref.at[slot, k], sem.at[slot]).wait()

    pieces = []
    for c in range(tm // 128):
        ls = slice(c * 128, (c + 1) * 128)
        w1c = jnp.broadcast_to(wts_ref[0:1, ls], (128, 128)).T
        w2c = jnp.broadcast_to(wts_ref[1:2, ls], (128, 128)).T
        w1f = jnp.concatenate([w1c] * (D_MODEL // 128), axis=1)
        w2f = jnp.concatenate([w2c] * (D_MODEL // 128), axis=1)
        y1 = ybuf_ref[slot, 0, ls].reshape(128, D_MODEL).astype(f32)
        y2 = ybuf_ref[slot, 1, ls].reshape(128, D_MODEL).astype(f32)
        pieces.append(w1f * y1 + w2f * y2)
    ffn = jnp.concatenate(pieces, axis=0)
    hp = DEEPNORM_ALPHA * h_ref[...] + ffn
    mu = jnp.mean(hp, axis=-1, keepdims=True)
    hc = hp - mu
    var = jnp.mean(hc * hc, axis=-1, keepdims=True)
    o_ref[...] = hc * lax.rsqrt(var + LN_EPS) * g2_ref[...] + b2_ref[...]


def _stage_combine(dest3, h2, wts, g2, b2, ys, *, tm):
    n = h2.shape[0]
    nsteps = n // tm
    return pl.pallas_call(
        functools.partial(_combine_body, tm=tm, nsteps=nsteps),
        grid=(nsteps,),
        in_specs=[pl.BlockSpec((1, 2, tm), lambda i: (i, 0, 0), memory_space=pltpu.SMEM),
                  pl.BlockSpec((1, 2, tm), lambda i: (jnp.minimum(i + 1, nsteps - 1), 0, 0), memory_space=pltpu.SMEM),
                  pl.BlockSpec((tm, D_MODEL), lambda i: (i, 0)),
                  pl.BlockSpec((8, tm), lambda i: (0, i)),
                  pl.BlockSpec((1, D_MODEL), lambda i: (0, 0)),
                  pl.BlockSpec((1, D_MODEL), lambda i: (0, 0)),
                  pl.BlockSpec(memory_space=pl.ANY)],
        out_specs=pl.BlockSpec((tm, D_MODEL), lambda i: (i, 0)),
        out_shape=jax.ShapeDtypeStruct((n, D_MODEL), f32),
        scratch_shapes=[pltpu.VMEM((2, 2, tm, ROW_TILE, 128), bf16), pltpu.SemaphoreType.DMA((2,))],
        compiler_params=_cparams(("arbitrary",)),
        name="moe_combine",
    )(dest3, dest3, h2, wts, g2, b2, ys)


def _layer(h, w_in, conv_w, a_log, dt_bias, dn_norm_w, sgu_ln_g, sgu_ln_b, w_spatial, b_spatial, w_out,
           ln1_g, ln1_b, w_rg, b_rg, w_re, b_re, w_gate, w_up, w_down, ln2_g, ln2_b,
           *, tm_in, in_groups, dn_rows, dn_chunks, dn_groups, tm_mix, tm_rank, tm_disp, tm_comb):
    B, T, _ = h.shape
    n = B * T
    w_cols = _stage_weight_layout(w_in)
    decay_prm = jnp.stack([a_log, dt_bias])
    prow = jnp.broadcast_to(jnp.pad(decay_prm, ((0, 0), (DN_HEADS, 8 - 2 * DN_HEADS)))[:, :, None], (2, 8, 128))

    q, k, v, z, u, vln, gcol, grow = _stage_inproj(
        h, w_cols, conv_w, prow, sgu_ln_g[None, :], sgu_ln_b[None, :], tm=tm_in, ngroups=in_groups)
    ydn = _stage_deltanet(q, k, v, z, gcol, grow, dn_norm_w[None, :], nbr=dn_rows, nch=dn_chunks, ngroups=dn_groups)

    bsp = jnp.broadcast_to(b_spatial.T[:, :, None], (SGU_CHUNK, SGU_GROUPS, SGU_CHUNK)).reshape(SGU_CHUNK, SGU_WIDTH)
    n_logit = MOE_GROUPS + N_EXPERTS
    wrt = jnp.pad(jnp.concatenate([w_rg, w_re], axis=1).T, ((0, 128 - n_logit), (0, 0))).astype(bf16)
    brt = jnp.broadcast_to(jnp.pad(jnp.concatenate([b_rg, b_re]), (0, 128 - n_logit))[:, None], (128, 128))
    h1, hrow, ids, wts, cnt = _stage_mixout(ydn, u, vln, h, w_spatial, bsp, w_out.astype(bf16), ln1_g[None, :],
                                       ln1_b[None, :], wrt, brt, tm=tm_mix)

    p_rows = (-(-(n * 2) // MOE_BLOCK)) * MOE_BLOCK + N_EXPERTS * MOE_BLOCK
    nb = p_rows // MOE_BLOCK
    nb_pad = (-(-nb // 128)) * 128
    dest, meta, blk = _stage_route(cnt, ids, tm=tm_rank, nb_pad=nb_pad)

    h2 = h1.reshape(n, D_MODEL)
    dest_d = dest[0:2].reshape(2, n // tm_disp, tm_disp).transpose(1, 0, 2)
    xs = _stage_dispatch(meta[0, :N_EXPERTS], meta[1, :N_EXPERTS], meta[2, 0:1], dest_d, hrow, p_rows, tm=tm_disp)
    ys = _stage_experts(blk[0, :nb], meta[2, 0:1], xs, w_gate, w_up, w_down)
    dest_c = dest[0:2].reshape(2, n // tm_comb, tm_comb).transpose(1, 0, 2)
    out = _stage_combine(dest_c, h2, wts, ln2_g[None, :], ln2_b[None, :], ys, tm=tm_comb)
    return out.reshape(B, T, D_MODEL)


def kernel(x, w_in, conv_w, a_log, dt_bias, dn_norm_w, sgu_ln_g, sgu_ln_b, w_spatial, b_spatial, w_out, ln1_g, ln1_b, w_router_group, b_router_group, w_router_expert, b_router_expert, w_gate, w_up, w_down, ln2_g, ln2_b):
    h = x
    for l in range(w_in.shape[0]):
        h = _layer(h, w_in[l], conv_w[l], a_log[l], dt_bias[l], dn_norm_w[l], sgu_ln_g[l], sgu_ln_b[l],
                   w_spatial[l], b_spatial[l], w_out[l], ln1_g[l], ln1_b[l],
                   w_router_group[l], b_router_group[l], w_router_expert[l], b_router_expert[l],
                   w_gate[l], w_up[l], w_down[l], ln2_g[l], ln2_b[l],
                   tm_in=512, in_groups=1, dn_rows=8, dn_chunks=2, dn_groups=4, tm_mix=512, tm_rank=1024, tm_disp=2048, tm_comb=512)
    return h
```

```python
import functools

import jax
import jax.numpy as jnp
from jax import lax
from jax.experimental import pallas as pl
from jax.experimental.pallas import tpu as pltpu

f32 = jnp.float32
bf16 = jnp.bfloat16
i32 = jnp.int32

D_MODEL = 1024
DN_WIDTH = 512
DN_HEADS = 4
HEAD_DIM = 128
CONV_K = 4
SGU_WIDTH = 512
SGU_GROUPS = 4
SGU_CHUNK = 128
DN_CHUNK = 128
MOE_GROUPS = 8
EXPERTS_PER_GROUP = 8
N_EXPERTS = 64
D_EXPERT = 512
MOE_BLOCK = 256
IN_COLS_ALIGNED = 4 * DN_WIDTH + 2 * SGU_WIDTH + 128
X_SLOTS = 4
ROW_TILE = D_MODEL // 128
DEEPNORM_ALPHA = 2.0 ** 0.25
LN_EPS = 1e-5
RMS_EPS = 1e-6
HIGHEST = lax.Precision.HIGHEST
VMEM_LIMIT_BYTES = 56 * 1024 * 1024

NT_DIMS = (((1,), (1,)), ((), ()))


def _cparams(sem, flags=None):
    return pltpu.CompilerParams(dimension_semantics=sem, vmem_limit_bytes=VMEM_LIMIT_BYTES, flags=flags)


def _sigmoid(x):
    return 1.0 / (1.0 + jnp.exp(-x))


def _silu(x):
    h = 0.5 * x
    return h + h * jnp.tanh(h)


def _softplus(x):
    return jnp.maximum(x, 0.0) + jnp.log1p(jnp.exp(-jnp.abs(x)))


def _gelu_tanh(x):
    c = 0.7978845608028654
    return x * (0.5 * (1.0 + jnp.tanh(c * (x + 0.044715 * (x * x * x)))))


def _iota2(shape, axis):
    return lax.broadcasted_iota(i32, shape, axis)


def _weight_layout_body(w_ref, o_ref):
    qkvz = 4 * DN_WIDTH
    uv0 = qkvz + 2 * DN_HEADS
    rows = w_ref.shape[0]
    o_ref[:, 0:qkvz] = w_ref[:, 0:qkvz].astype(bf16)
    o_ref[:, qkvz:qkvz + 2 * SGU_WIDTH] = w_ref[:, uv0:uv0 + 2 * SGU_WIDTH].astype(bf16)
    ba = jnp.concatenate([w_ref[:, qkvz:uv0], jnp.zeros((rows, 128 - 2 * DN_HEADS), f32)], axis=1)
    o_ref[:, qkvz + 2 * SGU_WIDTH:IN_COLS_ALIGNED] = ba.astype(bf16)


def _stage_weight_layout(w_in):
    rows = 256
    return pl.pallas_call(
        _weight_layout_body,
        grid=(D_MODEL // rows,),
        in_specs=[pl.BlockSpec((rows, w_in.shape[1]), lambda i: (i, 0))],
        out_specs=pl.BlockSpec((rows, IN_COLS_ALIGNED), lambda i: (i, 0)),
        out_shape=jax.ShapeDtypeStruct((D_MODEL, IN_COLS_ALIGNED), bf16),
        compiler_params=_cparams(("arbitrary",)),
        name="weight_layout",
    )(w_in)


def _inproj_body(x_ref, w_ref, convw_ref, prow_ref, lng_ref, lnb_ref, ones_ref,
                 q_ref, k_ref, v_ref, z_ref, u_ref, vln_ref, gcol_ref, grow_ref, *ext_refs, tm, ngroups):
    W = DN_WIDTH
    gm = tm // ngroups
    ext = [ext_refs[3 * g:3 * g + 3] for g in range(ngroups)]

    @pl.when(pl.program_id(1) == 0)
    def _():
        for e_ref in ext[0]:
            e_ref[0:8, :] = jnp.zeros((8, W), f32)

    for g in range(ngroups):
        _inproj_rows(x_ref, w_ref, convw_ref, prow_ref, lng_ref, lnb_ref, ones_ref,
                     q_ref, k_ref, v_ref, z_ref, u_ref, vln_ref, gcol_ref, grow_ref, ext[g],
                     ext[(g + 1) % ngroups], r0=g * gm, gm=gm)


def _inproj_rows(x_ref, w_ref, convw_ref, prow_ref, lng_ref, lnb_ref, ones_ref,
                 q_ref, k_ref, v_ref, z_ref, u_ref, vln_ref, gcol_ref, grow_ref, ext, ext_next, *, r0, gm):
    W = DN_WIDTH
    rows = slice(r0, r0 + gm)
    xb = x_ref[0, rows, :].astype(bf16)
    for part, e_ref in enumerate(ext):
        e_ref[8:8 + gm, :] = jnp.dot(xb, w_ref[:, part * W:(part + 1) * W], preferred_element_type=f32)
    zc = 3 * W
    uc = zc + W
    vc = uc + SGU_WIDTH
    bc = vc + SGU_WIDTH

    def conv_silu(part):
        e_ref = ext[part]
        cs = slice(part * W, (part + 1) * W)
        y = convw_ref[3:4, cs] * e_ref[8:8 + gm, :]
        for j in range(CONV_K - 1):
            y = y + convw_ref[j:j + 1, cs] * e_ref[5 + j:5 + j + gm, :]
        ext_next[part][0:8, :] = e_ref[gm:gm + 8, :]
        return _silu(y)

    def group_sums(a):
        return jnp.dot(a.astype(bf16), ones_ref[...], preferred_element_type=f32)

    yq = conv_silu(0)
    ssq = group_sums(yq * yq)
    pba = jnp.dot(xb, w_ref[:, bc:bc + 128], preferred_element_type=f32)
    pv = jnp.dot(xb, w_ref[:, vc:vc + SGU_WIDTH], preferred_element_type=f32)
    yk = conv_silu(1)
    ssk = group_sums(yk * yk)
    pu = jnp.dot(xb, w_ref[:, uc:uc + SGU_WIDTH], preferred_element_type=f32)
    pz = jnp.dot(xb, w_ref[:, zc:zc + W], preferred_element_type=f32)
    q_ref[0, rows, :] = (yq * (lax.rsqrt(ssq + RMS_EPS) * HEAD_DIM ** -0.5)).astype(bf16)
    k_ref[0, rows, :] = (yk * lax.rsqrt(ssk + RMS_EPS)).astype(bf16)
    v_ref[0, rows, :] = conv_silu(2).astype(bf16)

    z_ref[0, rows, :] = _silu(pz).astype(bf16)

    u_ref[0, rows, :] = _gelu_tanh(pu).astype(bf16)
    pv = _gelu_tanh(pv)
    for g in range(SGU_GROUPS):
        sl = slice(g * SGU_CHUNK, (g + 1) * SGU_CHUNK)
        vg = pv[:, sl]
        mu = jnp.mean(vg, axis=-1, keepdims=True)
        vcn = vg - mu
        var = jnp.mean(vcn * vcn, axis=-1, keepdims=True)
        vln_ref[0, rows, sl] = (vcn * lax.rsqrt(var + LN_EPS) * lng_ref[:, sl] + lnb_ref[:, sl]).astype(bf16)

    lane = _iota2((DN_CHUNK, 128), 1)
    beta = _sigmoid(pba)
    lane8 = _iota2((8, DN_CHUNK), 1)
    sub8 = _iota2((8, DN_CHUNK), 0)
    for c in range(gm // DN_CHUNK):
        rs = slice(c * DN_CHUNK, (c + 1) * DN_CHUNK)
        os_ = slice(r0 + c * DN_CHUNK, r0 + (c + 1) * DN_CHUNK)
        pbat = pba[rs].T[0:8, :]
        gt = -jnp.exp(prow_ref[0]) * _softplus(pbat + prow_ref[1])
        gc = jnp.where(sub8 >= DN_HEADS, gt, 0.0)
        shift = 1
        while shift < DN_CHUNK:
            gc = gc + jnp.where(lane8 >= shift, pltpu.roll(gc, shift, axis=1), 0.0)
            shift *= 2
        grow_ref[0, :, os_] = gc
        gc_col = jnp.concatenate([gc, jnp.zeros((DN_CHUNK - 8, DN_CHUNK), f32)], axis=0).T
        gcol_ref[0, os_, :] = jnp.where(lane < DN_HEADS, beta[rs], gc_col)


def _stage_inproj(x, w_re, conv_w, prow, lng, lnb, *, tm, ngroups):
    B, T, _ = x.shape
    wcols = w_re.shape[1]
    grid = (B, T // tm)
    gi = lax.broadcasted_iota(i32, (DN_WIDTH, DN_WIDTH), 0) // 128
    gj = lax.broadcasted_iota(i32, (DN_WIDTH, DN_WIDTH), 1) // 128
    group_ones = (gi == gj).astype(bf16)
    act = lambda: jax.ShapeDtypeStruct((B, T, DN_WIDTH), bf16)
    act_spec = lambda: pl.BlockSpec((1, tm, DN_WIDTH), lambda b, t: (b, t, 0))
    const2 = lambda shp: pl.BlockSpec(shp, lambda b, t: (0, 0))
    return pl.pallas_call(
        functools.partial(_inproj_body, tm=tm, ngroups=ngroups),
        grid=grid,
        in_specs=[
            pl.BlockSpec((1, tm, D_MODEL), lambda b, t: (b, t, 0)),
            const2((D_MODEL, wcols)),
            const2((CONV_K, 3 * DN_WIDTH)),
            pl.BlockSpec((2, 8, 128), lambda b, t: (0, 0, 0)),
            const2((1, SGU_WIDTH)),
            const2((1, SGU_WIDTH)),
            const2((DN_WIDTH, DN_WIDTH)),
        ],
        out_specs=[act_spec() for _ in range(6)] + [
            pl.BlockSpec((1, tm, 128), lambda b, t: (b, t, 0)),
            pl.BlockSpec((1, 8, tm), lambda b, t: (b, 0, t)),
        ],
        out_shape=[act() for _ in range(6)] + [
            jax.ShapeDtypeStruct((B, T, 128), f32),
            jax.ShapeDtypeStruct((B, 8, T), f32),
        ],
        scratch_shapes=[pltpu.VMEM((tm // ngroups + 8, DN_WIDTH), f32) for _ in range(3 * ngroups)],
        compiler_params=_cparams(("arbitrary", "arbitrary")),
        name="inproj",
    )(x, w_re, conv_w, prow, lng, lnb, group_ones)


def _mm(a, b):
    return jnp.dot(a.astype(bf16), b.astype(bf16), preferred_element_type=f32)


def _unit_lower_inverse(nmats, ii, jj):
    n = nmats[0].shape[0]
    eye = (ii == jj).astype(f32)
    leaf = jnp.right_shift(ii, 3) == jnp.right_shift(jj, 3)
    dblk = [jnp.where(leaf, m, 0.0) for m in nmats]
    s1 = [_mm(d, d) for d in dblk]
    r1 = [eye - d for d in dblk]
    both = [_mm(s, jnp.concatenate([s, r], axis=1)) for s, r in zip(s1, r1)]
    r2 = [r + bo[:, n:] for r, bo in zip(r1, both)]
    xs = [r + _mm(bo[:, :n], r) for r, bo in zip(r2, both)]
    size = 8
    while size < n:
        lows = [slice(r + size, r + 2 * size) for r in range(0, n, 2 * size)]
        ups = [slice(r, r + size) for r in range(0, n, 2 * size)]
        rsel = _iota2((n // 2, n), 0)
        ilow = rsel + size * (jnp.right_shift(rsel, size.bit_length() - 1) + 1)
        jlow = _iota2((n // 2, n), 1)
        in_pair_upper = (jlow >= ilow - (ilow & (2 * size - 1))) & (jlow < ilow - (ilow & (size - 1)))
        zeros = jnp.zeros((size, n), f32)
        new_xs = []
        ylows = [_mm(jnp.where(in_pair_upper, jnp.concatenate([m[s] for s in lows], axis=0), 0.0), x)
                 for m, x in zip(nmats, xs)]
        yfull = [jnp.concatenate([piece for k in range(len(lows)) for piece in (zeros, y[k * size:(k + 1) * size])], axis=0)
                 for y in ylows]
        corr = [_mm(jnp.concatenate([x[s] for s in lows], axis=0), yf) for x, yf in zip(xs, yfull)]
        for x, c in zip(xs, corr):
            pieces = []
            for k, (u, l) in enumerate(zip(ups, lows)):
                pieces += [x[u], x[l] - c[k * size:(k + 1) * size]]
            new_xs.append(jnp.concatenate(pieces, axis=0))
        xs = new_xs
        size *= 2
    return xs


def _deltanet_body(q_ref, k_ref, v_ref, z_ref, gcol_ref, grow_ref, nw_ref, y_ref, s_ref, *, nbr, nch, ngroups):
    @pl.when(pl.program_id(1) == 0)
    def _():
        s_ref[...] = jnp.zeros(s_ref.shape, f32)

    per = nbr // ngroups
    for g in range(ngroups):
        _deltanet_rows(q_ref, k_ref, v_ref, z_ref, gcol_ref, grow_ref, nw_ref, y_ref, s_ref,
                       rows=range(g * per, (g + 1) * per), nch=nch)


def _deltanet_rows(q_ref, k_ref, v_ref, z_ref, gcol_ref, grow_ref, nw_ref, y_ref, s_ref, *, rows, nch):
    C = DN_CHUNK
    S = [(b, h) for b in rows for h in range(DN_HEADS)]
    P = [(b, c, h) for c in range(nch) for b, h in S]

    ii = _iota2((C, C), 0)
    jj = _iota2((C, C), 1)
    causal = ii >= jj
    rs = [slice(c * C, (c + 1) * C) for c in range(nch)]
    hs = [slice(h * HEAD_DIM, (h + 1) * HEAD_DIM) for h in range(DN_HEADS)]
    gcol = {(b, c): gcol_ref[b, rs[c], :] for b in rows for c in range(nch)}
    qh = {(b, c, h): q_ref[b, rs[c], hs[h]] for b, c, h in P}
    kh = {(b, c, h): k_ref[b, rs[c], hs[h]] for b, c, h in P}
    vh = {(b, c, h): v_ref[b, rs[c], hs[h]] for b, c, h in P}
    gc_b = {(b, c, h): jnp.broadcast_to(gcol[b, c][:, DN_HEADS + h:DN_HEADS + h + 1], (C, HEAD_DIM)) for b, c, h in P}
    beta_b = {(b, c, h): jnp.broadcast_to(gcol[b, c][:, h:h + 1], (C, HEAD_DIM)) for b, c, h in P}
    gc_r = {(b, c, h): jnp.broadcast_to(grow_ref[b, DN_HEADS + h:DN_HEADS + h + 1, rs[c]], (C, C)) for b, c, h in P}
    decay = {p: jnp.exp(jnp.where(causal, gc_b[p] - gc_r[p], -1e30)) for p in P}

    kf = {p: kh[p].astype(f32) for p in P}
    kb = {p: kf[p] * beta_b[p] for p in P}
    kk = {p: lax.dot_general(kb[p].astype(bf16), kh[p], NT_DIMS, preferred_element_type=f32) for p in P}
    a_intra = {p: lax.dot_general(qh[p], kh[p], NT_DIMS, preferred_element_type=f32) * decay[p] for p in P}
    nmat = [jnp.where(ii > jj, kk[p] * decay[p], 0.0) for p in P]
    tinv = dict(zip(P, _unit_lower_inverse(nmat, ii, jj)))

    eg = {p: jnp.exp(gc_b[p]) for p in P}
    rhs = {p: jnp.concatenate([vh[p].astype(f32) * beta_b[p], kb[p] * eg[p]], axis=1) for p in P}
    sol = {p: _mm(tinv[p], rhs[p]) for p in P}

    q_dec = {p: qh[p].astype(f32) * eg[p] for p in P}
    g_last = {p: gc_b[p][C - 1:C, :] for p in P}
    kdt = {p: (kf[p] * jnp.exp(g_last[p] - gc_b[p])).T for p in P}

    state = {(b, h): s_ref[b * DN_HEADS + h] for b, h in S}
    for c in range(nch):
        m1 = {(b, h): _mm(jnp.concatenate([sol[b, c, h][:, HEAD_DIM:], q_dec[b, c, h]], axis=0), state[b, h]) for b, h in S}
        v_new = {(b, h): sol[b, c, h][:, :HEAD_DIM] - m1[b, h][:C] for b, h in S}
        m2 = {(b, h): _mm(jnp.concatenate([a_intra[b, c, h], kdt[b, c, h]], axis=0), v_new[b, h]) for b, h in S}
        state = {(b, h): state[b, h] * jnp.exp(g_last[b, c, h]) + m2[b, h][C:] for b, h in S}
        for b, h in S:
            o = m1[b, h][C:] + m2[b, h][:C]
            rms = lax.rsqrt(jnp.mean(o * o, axis=-1, keepdims=True) + RMS_EPS)
            y_ref[b, rs[c], hs[h]] = (o * rms * nw_ref[...] * z_ref[b, rs[c], hs[h]].astype(f32)).astype(bf16)
    for b, h in S:
        s_ref[b * DN_HEADS + h] = state[b, h]


def _stage_deltanet(q, k, v, z, gcol, grow, norm_w, *, nbr, nch, ngroups):
    B, T, _ = q.shape
    tt = nch * DN_CHUNK
    act_spec = lambda: pl.BlockSpec((nbr, tt, DN_WIDTH), lambda b, t: (b, t, 0))
    return pl.pallas_call(
        functools.partial(_deltanet_body, nbr=nbr, nch=nch, ngroups=ngroups),
        grid=(B // nbr, T // tt),
        in_specs=[act_spec(), act_spec(), act_spec(), act_spec(),
                  pl.BlockSpec((nbr, tt, 128), lambda b, t: (b, t, 0)),
                  pl.BlockSpec((nbr, 8, tt), lambda b, t: (b, 0, t)),
                  pl.BlockSpec((1, HEAD_DIM), lambda b, t: (0, 0))],
        out_specs=act_spec(),
        out_shape=jax.ShapeDtypeStruct((B, T, DN_WIDTH), bf16),
        scratch_shapes=[pltpu.VMEM((nbr * DN_HEADS, HEAD_DIM, HEAD_DIM), f32)],
        compiler_params=_cparams(("arbitrary", "arbitrary")),
        name="deltanet",
    )(q, k, v, z, gcol, grow, norm_w)


def _mixout_body(ydn_ref, u_ref, vln_ref, x_hbm, ws_ref, bsp_ref, wout_ref, g1_ref, b1_ref, wrt_ref, brt_ref,
                 h_ref, hrow_ref, ids_ref, wts_ref, cnt_ref, ycat_ref, xring_ref, xsem, *, tm, nsteps):
    C = SGU_CHUNK
    step = pl.program_id(0) * pl.num_programs(1) + pl.program_id(1)
    slot = lax.rem(step, X_SLOTS)

    def x_copy(s, sl):
        return pltpu.make_async_copy(x_hbm.at[s], xring_ref.at[sl], xsem.at[sl])

    @pl.when(step == 0)
    def _():
        cnt_ref[...] = jnp.zeros(cnt_ref.shape, f32)
        for j in range(X_SLOTS - 1):
            x_copy(j, j).start()

    ahead = step + (X_SLOTS - 1)

    @pl.when(ahead < nsteps)
    def _():
        x_copy(ahead, lax.rem(ahead, X_SLOTS)).start()

    x_copy(step, slot).wait()
    ii = _iota2((C, C), 0)
    jj = _iota2((C, C), 1)
    ycat_ref[:, 0:DN_WIDTH] = ydn_ref[0]
    for g in range(SGU_GROUPS):
        gs = slice(g * C, (g + 1) * C)
        wsg = jnp.where(ii >= jj, ws_ref[g], 0.0).astype(bf16)
        for c in range(tm // C):
            rs = slice(c * C, (c + 1) * C)
            mixed = jnp.dot(wsg, vln_ref[0, rs, gs], preferred_element_type=f32) + bsp_ref[:, gs]
            ycat_ref[rs, DN_WIDTH + g * C:DN_WIDTH + (g + 1) * C] = (u_ref[0, rs, gs].astype(f32) * mixed).astype(bf16)

    RB = 128
    blocks = [slice(r, r + RB) for r in range(0, tm, RB)]
    mix = [jnp.dot(ycat_ref[rb, :], wout_ref[...], preferred_element_type=f32) for rb in blocks]
    h1s = []
    for rb, m in zip(blocks, mix):
        hp = DEEPNORM_ALPHA * xring_ref[slot, rb, :] + m
        mu = jnp.mean(hp, axis=-1, keepdims=True)
        hc = hp - mu
        var = jnp.mean(hc * hc, axis=-1, keepdims=True)
        h1 = hc * lax.rsqrt(var + LN_EPS) * g1_ref[...] + b1_ref[...]
        h_ref[0, rb, :] = h1
        h1b = h1.astype(bf16)
        hrow_ref[rb] = h1b.reshape(RB, ROW_TILE, 128)
        h1s.append(h1b)

    logit_blocks = [lax.dot_general(wrt_ref[...], hb, NT_DIMS, preferred_element_type=f32) + brt_ref[...] for hb in h1s]
    sub = _iota2((8, RB), 0)
    subf = sub.astype(f32)
    sub_e = _iota2((N_EXPERTS, RB), 0).astype(f32)
    chosen = []
    for rb, logits in zip(blocks, logit_blocks):
        gl = logits[0:8]
        gmax = jnp.max(gl, axis=0, keepdims=True)
        g_idx = jnp.min(jnp.where(gl == gmax, subf, float(MOE_GROUPS)), axis=0, keepdims=True)
        p_group = 1.0 / jnp.sum(jnp.exp(gl - gmax), axis=0, keepdims=True)
        within = jnp.zeros((8, RB), f32)
        for g in range(MOE_GROUPS):
            within = within + jnp.where(g_idx == float(g), logits[8 + 8 * g:16 + 8 * g], 0.0)
        m1 = jnp.max(within, axis=0, keepdims=True)
        i1 = jnp.min(jnp.where(within == m1, subf, float(EXPERTS_PER_GROUP)), axis=0, keepdims=True)
        rest = jnp.where(subf == i1, -jnp.inf, within)
        m2 = jnp.max(rest, axis=0, keepdims=True)
        i2 = jnp.min(jnp.where(rest == m2, subf, float(EXPERTS_PER_GROUP)), axis=0, keepdims=True)
        e = jnp.exp(m2 - m1)
        w1 = p_group / (1.0 + e)
        w2 = p_group * e / (1.0 + e)
        e1 = g_idx * float(EXPERTS_PER_GROUP) + i1
        e2 = g_idx * float(EXPERTS_PER_GROUP) + i2
        ids_ref[:, rb] = jnp.where(sub == 0, e1, jnp.where(sub == 1, e2, 0.0)).astype(i32)
        wts_ref[:, rb] = jnp.where(sub == 0, w1, jnp.where(sub == 1, w2, 0.0))
        chosen.append(((sub_e == e1).astype(f32) + (sub_e == e2).astype(f32)).astype(bf16))
    ones = jnp.ones((RB, 128), bf16)
    cnt_ref[...] = cnt_ref[...] + sum(jnp.dot(oh, ones, preferred_element_type=f32) for oh in chosen)


def _stage_mixout(ydn, u, vln, x, ws, bsp, wout, g1, b1, wrt, brt, *, tm):
    B, T, _ = x.shape
    nt = T // tm
    act_spec = lambda: pl.BlockSpec((1, tm, DN_WIDTH), lambda b, t: (b, t, 0))
    const2 = lambda shp: pl.BlockSpec(shp, lambda b, t: (0, 0))
    tok_spec = lambda: pl.BlockSpec((8, tm), lambda b, t: (0, b * nt + t))
    return pl.pallas_call(
        functools.partial(_mixout_body, tm=tm, nsteps=B * nt),
        grid=(B, nt),
        in_specs=[act_spec(), act_spec(), act_spec(),
                  pl.BlockSpec(memory_space=pl.ANY),
                  pl.BlockSpec((SGU_GROUPS, SGU_CHUNK, SGU_CHUNK), lambda b, t: (0, 0, 0)),
                  const2((SGU_CHUNK, SGU_WIDTH)),
                  const2((D_MODEL, D_MODEL)),
                  const2((1, D_MODEL)), const2((1, D_MODEL)),
                  const2((128, D_MODEL)), const2((128, 128))],
        out_specs=[pl.BlockSpec((1, tm, D_MODEL), lambda b, t: (b, t, 0)),
                   pl.BlockSpec((tm, ROW_TILE, 128), lambda b, t: (b * nt + t, 0, 0)), tok_spec(), tok_spec(),
                   const2((N_EXPERTS, 128))],
        out_shape=[jax.ShapeDtypeStruct((B, T, D_MODEL), f32),
                   jax.ShapeDtypeStruct((B * T, ROW_TILE, 128), bf16),
                   jax.ShapeDtypeStruct((8, B * T), i32),
                   jax.ShapeDtypeStruct((8, B * T), f32),
                   jax.ShapeDtypeStruct((N_EXPERTS, 128), f32)],
        scratch_shapes=[pltpu.VMEM((tm, D_MODEL), bf16), pltpu.VMEM((X_SLOTS, tm, D_MODEL), f32),
                        pltpu.SemaphoreType.DMA((X_SLOTS,))],
        compiler_params=_cparams(("arbitrary", "arbitrary")),
        name="mixout",
    )(ydn, u, vln, x.reshape(B * nt, tm, D_MODEL), ws, bsp, wout, g1, b1, wrt, brt)


def _route_body(cnt_ref, ids_ref, dest_ref, meta_ref, blk_ref, pstart_ref, *, tm, nb_pad):
    i = pl.program_id(0)
    sub = _iota2((N_EXPERTS, tm), 0)
    is1 = sub == ids_ref[0:1, :]
    is2 = sub == ids_ref[1:2, :]
    oh = (is1.astype(f32) + is2.astype(f32)).astype(bf16)

    @pl.when(i == 0)
    def _():
        cnt = cnt_ref[...]
        padded = jnp.floor((cnt + (MOE_BLOCK - 1)) * (1.0 / MOE_BLOCK)) * MOE_BLOCK
        ei = _iota2((N_EXPERTS, N_EXPERTS), 0)
        ej = _iota2((N_EXPERTS, N_EXPERTS), 1)
        pends = jnp.dot((ei >= ej).astype(f32), padded, precision=HIGHEST, preferred_element_type=f32)
        pstart = pends - padded
        pstart_ref[...] = pstart
        s64 = _iota2((N_EXPERTS, 128), 0)
        l64 = _iota2((N_EXPERTS, 128), 1)
        diag = s64 == l64
        fill_off = jnp.sum(jnp.where(diag, pstart + cnt, 0.0), axis=0, keepdims=True)
        fill_n = jnp.sum(jnp.where(diag, padded - cnt, 0.0), axis=0, keepdims=True)
        nused = pends[N_EXPERTS - 1:N_EXPERTS, :] * (1.0 / MOE_BLOCK)
        m8 = _iota2((8, 128), 0)
        meta_ref[...] = jnp.where(m8 == 0, fill_off, jnp.where(m8 == 1, fill_n, jnp.where(m8 == 2, nused, 0.0))).astype(i32)
        bstart = (_iota2((N_EXPERTS, nb_pad), 1) * MOE_BLOCK).astype(f32)
        pe = jnp.concatenate([pends] * (nb_pad // 128), axis=1)
        be = jnp.sum((pe <= bstart).astype(f32), axis=0, keepdims=True)
        be = jnp.minimum(be, float(N_EXPERTS - 1))
        blk_ref[...] = jnp.broadcast_to(be, (8, nb_pad)).astype(i32)

    ti = _iota2((tm, tm), 0)
    tj = _iota2((tm, tm), 1)
    before = (ti < tj).astype(bf16)
    prefix = jnp.dot(oh, before, preferred_element_type=f32)
    nxt = prefix + jnp.concatenate([pstart_ref[...]] * (tm // 128), axis=1)
    d1 = jnp.sum(jnp.where(is1, nxt, 0.0), axis=0, keepdims=True)
    d2 = jnp.sum(jnp.where(is2, nxt, 0.0), axis=0, keepdims=True)
    sub8 = _iota2((8, tm), 0)
    dest_ref[...] = jnp.where(sub8 == 0, d1, jnp.where(sub8 == 1, d2, 0.0)).astype(i32)
    pstart_ref[...] = pstart_ref[...] + jnp.dot(oh, jnp.ones((tm, 128), bf16), preferred_element_type=f32)


def _stage_route(cnt, ids, *, tm, nb_pad):
    n = ids.shape[1]
    return pl.pallas_call(
        functools.partial(_route_body, tm=tm, nb_pad=nb_pad),
        grid=(n // tm,),
        in_specs=[pl.BlockSpec((N_EXPERTS, 128), lambda i: (0, 0)), pl.BlockSpec((8, tm), lambda i: (0, i))],
        out_specs=[pl.BlockSpec((8, tm), lambda i: (0, i)),
                   pl.BlockSpec((8, 128), lambda i: (0, 0)),
                   pl.BlockSpec((8, nb_pad), lambda i: (0, 0))],
        out_shape=[jax.ShapeDtypeStruct((8, n), i32), jax.ShapeDtypeStruct((8, 128), i32),
                   jax.ShapeDtypeStruct((8, nb_pad), i32)],
        scratch_shapes=[pltpu.VMEM((N_EXPERTS, 128), f32)],
        compiler_params=_cparams(("arbitrary",)),
        name="moe_route",
    )(cnt, ids)


def _dispatch_body(fill_off_ref, fill_n_ref, nused_ref, dest_ref, h_hbm, xs_ref, src_ref, zero_ref, isem, sem, zsem,
                   *, tm, nsteps):
    i = pl.program_id(0)
    slot = lax.rem(i, X_SLOTS)

    def in_copy(step, s):
        return pltpu.make_async_copy(h_hbm.at[pl.ds(step * tm, tm)], src_ref.at[s], isem.at[s])

    def drain(s):
        for _ in range(2):
            pltpu.make_async_copy(src_ref.at[s], xs_ref.at[pl.ds(0, tm)], sem.at[s]).wait()

    @pl.when(i == 0)
    def _():
        for j in range(min(X_SLOTS - 1, nsteps)):
            in_copy(j, j).start()

    in_copy(i, slot).wait()

    def row_copy(t, d):
        return pltpu.make_async_copy(src_ref.at[slot, t], xs_ref.at[d], sem.at[slot])

    def issue(t, carry):
        row_copy(t, dest_ref[0, 0, t]).start(priority=0)
        row_copy(t, dest_ref[0, 1, t]).start(priority=1)
        return carry

    lax.fori_loop(0, tm, issue, 0, unroll=8)

    @pl.when(i == 0)
    def _():
        zero_ref[...] = jnp.zeros(zero_ref.shape, bf16)

        def fill(start):
            def body(e, carry):
                off = fill_off_ref[e]
                npad = fill_n_ref[e]
                bit = MOE_BLOCK // 2
                while bit:
                    @pl.when((npad & bit) != 0)
                    def _(off=off, bit=bit):
                        cp = pltpu.make_async_copy(zero_ref.at[pl.ds(0, bit)], xs_ref.at[pl.ds(off, bit)], zsem)
                        cp.start() if start else cp.wait()
                    off = off + (npad & bit)
                    bit //= 2
                return carry
            return body

        lax.fori_loop(0, N_EXPERTS, fill(True), 0)
        lax.fori_loop(0, N_EXPERTS, fill(False), 0)

        def tail_copy(b):
            return pltpu.make_async_copy(zero_ref, xs_ref.at[pl.ds(b * MOE_BLOCK, MOE_BLOCK)], zsem)

        nblocks = xs_ref.shape[0] // MOE_BLOCK
        lax.fori_loop(nused_ref[0], nblocks, lambda b, c: (tail_copy(b).start(), c)[1], 0)
        lax.fori_loop(nused_ref[0], nblocks, lambda b, c: (tail_copy(0).wait(), c)[1], 0)

    @pl.when(i >= 1)
    def _():
        drain(lax.rem(i - 1, X_SLOTS))

    ahead = i + (X_SLOTS - 1)

    @pl.when(ahead < nsteps)
    def _():
        in_copy(ahead, lax.rem(ahead, X_SLOTS)).start()

    @pl.when(i == nsteps - 1)
    def _():
        drain(slot)


def _stage_dispatch(fill_off, fill_n, nused, dest3, hrow, p_rows, *, tm):
    n = hrow.shape[0]
    nsteps = n // tm
    return pl.pallas_call(
        functools.partial(_dispatch_body, tm=tm, nsteps=nsteps),
        grid_spec=pltpu.PrefetchScalarGridSpec(
            num_scalar_prefetch=3,
            grid=(nsteps,),
            in_specs=[pl.BlockSpec((1, 2, tm), lambda i, fo, fn, nu: (i, 0, 0), memory_space=pltpu.SMEM),
                      pl.BlockSpec(memory_space=pl.ANY)],
            out_specs=pl.BlockSpec(memory_space=pl.ANY),
            scratch_shapes=[pltpu.VMEM((X_SLOTS, tm, ROW_TILE, 128), bf16),
                            pltpu.VMEM((MOE_BLOCK, ROW_TILE, 128), bf16),
                            pltpu.SemaphoreType.DMA((X_SLOTS,)), pltpu.SemaphoreType.DMA((X_SLOTS,)),
                            pltpu.SemaphoreType.DMA],
        ),
        out_shape=jax.ShapeDtypeStruct((p_rows, ROW_TILE, 128), bf16),
        compiler_params=_cparams(("arbitrary",)),
        name="moe_dispatch",
    )(fill_off, fill_n, nused, dest3, hrow)


def _experts_body(blk_ref, nused_ref, xs_hbm, wg_hbm, wu_hbm, wd_hbm, ys_hbm,
                  xbuf_ref, wg32_ref, wu32_ref, wd32_ref, wgu16_ref, wd16_ref, obuf_ref, xsem, wsem, osem,
                  *, nsteps):
    i = pl.program_id(0)
    nused = nused_ref[0]
    used = i < nused
    e = blk_ref[i]
    slot = lax.rem(i, X_SLOTS)

    def weight_copies(ex):
        return (pltpu.make_async_copy(wg_hbm.at[ex], wg32_ref, wsem.at[0]),
                pltpu.make_async_copy(wu_hbm.at[ex], wu32_ref, wsem.at[1]),
                pltpu.make_async_copy(wd_hbm.at[ex], wd32_ref, wsem.at[2]))

    def x_copy(block, s):
        return pltpu.make_async_copy(xs_hbm.at[pl.ds(block * MOE_BLOCK, MOE_BLOCK)], xbuf_ref.at[s], xsem.at[s])

    @pl.when((i == 0) & used)
    def _():
        for cp in weight_copies(e):
            cp.start()
        for j in range(X_SLOTS - 1):
            @pl.when(j < nused)
            def _(j=j):
                x_copy(j, j).start()

    ahead = i + (X_SLOTS - 1)

    @pl.when(ahead < nused)
    def _():
        x_copy(ahead, lax.rem(ahead, X_SLOTS)).start()

    @pl.when(used & ((i == 0) | (e != blk_ref[jnp.maximum(i - 1, 0)])))
    def _():
        for cp in weight_copies(e):
            cp.wait()
        wgu16_ref[:, 0:D_EXPERT] = wg32_ref[...].astype(bf16)
        wgu16_ref[:, D_EXPERT:2 * D_EXPERT] = wu32_ref[...].astype(bf16)
        wd16_ref[...] = wd32_ref[...].astype(bf16)
        nxt = lax.while_loop(lambda j: (j < nused) & (blk_ref[jnp.minimum(j, nused - 1)] == e), lambda j: j + 1, i + 1)

        @pl.when(nxt < nused)
        def _():
            for cp in weight_copies(blk_ref[jnp.minimum(nxt, nused - 1)]):
                cp.start(priority=1)

    def out_copy(block, s):
        return pltpu.make_async_copy(obuf_ref.at[s], ys_hbm.at[pl.ds(block * MOE_BLOCK, MOE_BLOCK)], osem.at[s])

    @pl.when(i >= X_SLOTS)
    def _():
        out_copy(i - X_SLOTS, slot).wait()

    @pl.when(used)
    def _():
        x_copy(i, slot).wait()
        half = MOE_BLOCK // 2
        rows = [slice(p * half, (p + 1) * half) for p in range(2)]
        gu = [jnp.dot(xbuf_ref[slot, r].reshape(half, D_MODEL), wgu16_ref[...], preferred_element_type=f32)
              for r in rows]
        hid = [(_silu(g[:, :D_EXPERT]) * g[:, D_EXPERT:]).astype(bf16) for g in gu]
        y = [jnp.dot(hd, wd16_ref[...], preferred_element_type=f32) for hd in hid]
        for r, yp in zip(rows, y):
            obuf_ref[slot, r] = yp.astype(bf16).reshape(half, ROW_TILE, 128)

    @pl.when(jnp.logical_not(used))
    def _():
        obuf_ref[slot] = jnp.zeros(obuf_ref.shape[1:], bf16)

    out_copy(i, slot).start()

    @pl.when(i == nsteps - 1)
    def _():
        for step in range(nsteps - X_SLOTS, nsteps):
            out_copy(step, step % X_SLOTS).wait()


def _stage_experts(blk_e, nused, xs, w_gate, w_up, w_down):
    p_rows = xs.shape[0]
    nb = p_rows // MOE_BLOCK
    assert nb >= X_SLOTS

    return pl.pallas_call(
        functools.partial(_experts_body, nsteps=nb),
        grid_spec=pltpu.PrefetchScalarGridSpec(
            num_scalar_prefetch=2,
            grid=(nb,),
            in_specs=[pl.BlockSpec(memory_space=pl.ANY),
                      pl.BlockSpec(memory_space=pl.ANY),
                      pl.BlockSpec(memory_space=pl.ANY),
                      pl.BlockSpec(memory_space=pl.ANY)],
            out_specs=pl.BlockSpec(memory_space=pl.ANY),
            scratch_shapes=[pltpu.VMEM((X_SLOTS, MOE_BLOCK, ROW_TILE, 128), bf16),
                            pltpu.VMEM((D_MODEL, D_EXPERT), f32), pltpu.VMEM((D_MODEL, D_EXPERT), f32),
                            pltpu.VMEM((D_EXPERT, D_MODEL), f32),
                            pltpu.VMEM((D_MODEL, 2 * D_EXPERT), bf16), pltpu.VMEM((D_EXPERT, D_MODEL), bf16),
                            pltpu.VMEM((X_SLOTS, MOE_BLOCK, ROW_TILE, 128), bf16),
                            pltpu.SemaphoreType.DMA((X_SLOTS,)), pltpu.SemaphoreType.DMA((3,)),
                            pltpu.SemaphoreType.DMA((X_SLOTS,))],
        ),
        out_shape=jax.ShapeDtypeStruct((p_rows, ROW_TILE, 128), bf16),
        compiler_params=_cparams(("arbitrary",)),
        name="moe_experts",
    )(blk_e, nused, xs, w_gate, w_up, w_down)


def _combine_body(dcur_ref, dnext_ref, h_ref, wts_ref, g2_ref, b2_ref, ys_ref, o_ref, ybuf_ref, sem, *, tm, nsteps):
    i = pl.program_id(0)
    slot = lax.rem(i, 2)

    def issue_tile(d_ref, s):
        def body(t, carry):
            for k in range(2):
                pltpu.make_async_copy(ys_ref.at[d_ref[0, k, t]], ybuf_ref.at[s, k, t], sem.at[s]).start(priority=k)
            return carry

        lax.fori_loop(0, tm, body, 0, unroll=8)

    @pl.when(i == 0)
    def _():
        issue_tile(dcur_ref, 0)

    @pl.when(i + 1 < nsteps)
    def _():
        issue_tile(dnext_ref, 1 - slot)

    for k in range(2):
        pltpu.make_async_copy(ys_ref.at[pl.ds(0, tm)], ybuf_ref.at[slot, k], sem.at[slot]).wait()

    pieces = []
    for c in range(tm // 128):
        ls = slice(c * 128, (c + 1) * 128)
        w1c = jnp.broadcast_to(wts_ref[0:1, ls], (128, 128)).T
        w2c = jnp.broadcast_to(wts_ref[1:2, ls], (128, 128)).T
        w1f = jnp.concatenate([w1c] * (D_MODEL // 128), axis=1)
        w2f = jnp.concatenate([w2c] * (D_MODEL // 128), axis=1)
        y1 = ybuf_ref[slot, 0, ls].reshape(128, D_MODEL).astype(f32)
        y2 = ybuf_ref[slot, 1, ls].reshape(128, D_MODEL).astype(f32)
        pieces.append(w1f * y1 + w2f * y2)
    ffn = jnp.concatenate(pieces, axis=0)
    hp = DEEPNORM_ALPHA * h_ref[...] + ffn
    mu = jnp.mean(hp, axis=-1, keepdims=True)
    hc = hp - mu
    var = jnp.mean(hc * hc, axis=-1, keepdims=True)
    o_ref[...] = hc * lax.rsqrt(var + LN_EPS) * g2_ref[...] + b2_ref[...]


def _stage_combine(dest3, h2, wts, g2, b2, ys, *, tm):
    n = h2.shape[0]
    nsteps = n // tm
    return pl.pallas_call(
        functools.partial(_combine_body, tm=tm, nsteps=nsteps),
        grid=(nsteps,),
        in_specs=[pl.BlockSpec((1, 2, tm), lambda i: (i, 0, 0), memory_space=pltpu.SMEM),
                  pl.BlockSpec((1, 2, tm), lambda i: (jnp.minimum(i + 1, nsteps - 1), 0, 0), memory_space=pltpu.SMEM),
                  pl.BlockSpec((tm, D_MODEL), lambda i: (i, 0)),
                  pl.BlockSpec((8, tm), lambda i: (0, i)),
                  pl.BlockSpec((1, D_MODEL), lambda i: (0, 0)),
                  pl.BlockSpec((1, D_MODEL), lambda i: (0, 0)),
                  pl.BlockSpec(memory_space=pl.ANY)],
        out_specs=pl.BlockSpec((tm, D_MODEL), lambda i: (i, 0)),
        out_shape=jax.ShapeDtypeStruct((n, D_MODEL), f32),
        scratch_shapes=[pltpu.VMEM((2, 2, tm, ROW_TILE, 128), bf16), pltpu.SemaphoreType.DMA((2,))],
        compiler_params=_cparams(("arbitrary",)),
        name="moe_combine",
    )(dest3, dest3, h2, wts, g2, b2, ys)


def _layer(h, w_in, conv_w, a_log, dt_bias, dn_norm_w, sgu_ln_g, sgu_ln_b, w_spatial, b_spatial, w_out,
           ln1_g, ln1_b, w_rg, b_rg, w_re, b_re, w_gate, w_up, w_down, ln2_g, ln2_b,
           *, tm_in, in_groups, dn_rows, dn_chunks, dn_groups, tm_mix, tm_rank, tm_disp, tm_comb):
    B, T, _ = h.shape
    n = B * T
    w_cols = _stage_weight_layout(w_in)
    decay_prm = jnp.stack([a_log, dt_bias])
    prow = jnp.broadcast_to(jnp.pad(decay_prm, ((0, 0), (DN_HEADS, 8 - 2 * DN_HEADS)))[:, :, None], (2, 8, 128))

    q, k, v, z, u, vln, gcol, grow = _stage_inproj(
        h, w_cols, conv_w, prow, sgu_ln_g[None, :], sgu_ln_b[None, :], tm=tm_in, ngroups=in_groups)
    ydn = _stage_deltanet(q, k, v, z, gcol, grow, dn_norm_w[None, :], nbr=dn_rows, nch=dn_chunks, ngroups=dn_groups)

    bsp = jnp.broadcast_to(b_spatial.T[:, :, None], (SGU_CHUNK, SGU_GROUPS, SGU_CHUNK)).reshape(SGU_CHUNK, SGU_WIDTH)
    n_logit = MOE_GROUPS + N_EXPERTS
    wrt = jnp.pad(jnp.concatenate([w_rg, w_re], axis=1).T, ((0, 128 - n_logit), (0, 0))).astype(bf16)
    brt = jnp.broadcast_to(jnp.pad(jnp.concatenate([b_rg, b_re]), (0, 128 - n_logit))[:, None], (128, 128))
    h1, hrow, ids, wts, cnt = _stage_mixout(ydn, u, vln, h, w_spatial, bsp, w_out.astype(bf16), ln1_g[None, :],
                                       ln1_b[None, :], wrt, brt, tm=tm_mix)

    p_rows = (-(-(n * 2) // MOE_BLOCK)) * MOE_BLOCK + N_EXPERTS * MOE_BLOCK
    nb = p_rows // MOE_BLOCK
    nb_pad = (-(-nb // 128)) * 128
    dest, meta, blk = _stage_route(cnt, ids, tm=tm_rank, nb_pad=nb_pad)

    h2 = h1.reshape(n, D_MODEL)
    dest_d = dest[0:2].reshape(2, n // tm_disp, tm_disp).transpose(1, 0, 2)
    xs = _stage_dispatch(meta[0, :N_EXPERTS], meta[1, :N_EXPERTS], meta[2, 0:1], dest_d, hrow, p_rows, tm=tm_disp)
    ys = _stage_experts(blk[0, :nb], meta[2, 0:1], xs, w_gate, w_up, w_down)
    dest_c = dest[0:2].reshape(2, n // tm_comb, tm_comb).transpose(1, 0, 2)
    out = _stage_combine(dest_c, h2, wts, ln2_g[None, :], ln2_b[None, :], ys, tm=tm_comb)
    return out.reshape(B, T, D_MODEL)


def kernel(x, w_in, conv_w, a_log, dt_bias, dn_norm_w, sgu_ln_g, sgu_ln_b, w_spatial, b_spatial, w_out, ln1_g, ln1_b, w_router_group, b_router_group, w_router_expert, b_router_expert, w_gate, w_up, w_down, ln2_g, ln2_b):
    h = x
    for l in range(w_in.shape[0]):
        h = _layer(h, w_in[l], conv_w[l], a_log[l], dt_bias[l], dn_norm_w[l], sgu_ln_g[l], sgu_ln_b[l],
                   w_spatial[l], b_spatial[l], w_out[l], ln1_g[l], ln1_b[l],
                   w_router_group[l], b_router_group[l], w_router_expert[l], b_router_expert[l],
                   w_gate[l], w_up[l], w_down[l], ln2_g[l], ln2_b[l],
                   tm_in=512, in_groups=1, dn_rows=8, dn_chunks=2, dn_groups=4, tm_mix=1024, tm_rank=1024, tm_disp=2048, tm_comb=512)
    return h
```

```python
import functools

import jax
import jax.numpy as jnp
from jax import lax
from jax.experimental import pallas as pl
from jax.experimental.pallas import tpu as pltpu

f32 = jnp.float32
bf16 = jnp.bfloat16
i32 = jnp.int32

D_MODEL = 1024
DN_WIDTH = 512
DN_HEADS = 4
HEAD_DIM = 128
CONV_K = 4
SGU_WIDTH = 512
SGU_GROUPS = 4
SGU_CHUNK = 128
DN_CHUNK = 128
MOE_GROUPS = 8
EXPERTS_PER_GROUP = 8
N_EXPERTS = 64
D_EXPERT = 512
MOE_BLOCK = 256
IN_COLS_ALIGNED = 4 * DN_WIDTH + 2 * SGU_WIDTH + 128
X_SLOTS = 4
ROW_TILE = D_MODEL // 128
DEEPNORM_ALPHA = 2.0 ** 0.25
LN_EPS = 1e-5
RMS_EPS = 1e-6
HIGHEST = lax.Precision.HIGHEST
VMEM_LIMIT_BYTES = 56 * 1024 * 1024

NT_DIMS = (((1,), (1,)), ((), ()))


def _cparams(sem, flags=None):
    return pltpu.CompilerParams(dimension_semantics=sem, vmem_limit_bytes=VMEM_LIMIT_BYTES, flags=flags)


def _sigmoid(x):
    return 1.0 / (1.0 + jnp.exp(-x))


def _silu(x):
    h = 0.5 * x
    return h + h * jnp.tanh(h)


def _softplus(x):
    return jnp.maximum(x, 0.0) + jnp.log1p(jnp.exp(-jnp.abs(x)))


def _gelu_tanh(x):
    c = 0.7978845608028654
    return x * (0.5 * (1.0 + jnp.tanh(c * (x + 0.044715 * (x * x * x)))))


def _iota2(shape, axis):
    return lax.broadcasted_iota(i32, shape, axis)


def _weight_layout_body(w_ref, o_ref):
    qkvz = 4 * DN_WIDTH
    uv0 = qkvz + 2 * DN_HEADS
    rows = w_ref.shape[0]
    o_ref[:, 0:qkvz] = w_ref[:, 0:qkvz].astype(bf16)
    o_ref[:, qkvz:qkvz + 2 * SGU_WIDTH] = w_ref[:, uv0:uv0 + 2 * SGU_WIDTH].astype(bf16)
    ba = jnp.concatenate([w_ref[:, qkvz:uv0], jnp.zeros((rows, 128 - 2 * DN_HEADS), f32)], axis=1)
    o_ref[:, qkvz + 2 * SGU_WIDTH:IN_COLS_ALIGNED] = ba.astype(bf16)


def _stage_weight_layout(w_in):
    rows = 256
    return pl.pallas_call(
        _weight_layout_body,
        grid=(D_MODEL // rows,),
        in_specs=[pl.BlockSpec((rows, w_in.shape[1]), lambda i: (i, 0))],
        out_specs=pl.BlockSpec((rows, IN_COLS_ALIGNED), lambda i: (i, 0)),
        out_shape=jax.ShapeDtypeStruct((D_MODEL, IN_COLS_ALIGNED), bf16),
        compiler_params=_cparams(("arbitrary",)),
        name="weight_layout",
    )(w_in)


def _inproj_body(x_ref, w_ref, convw_ref, prow_ref, lng_ref, lnb_ref, ones_ref,
                 q_ref, k_ref, v_ref, z_ref, u_ref, vln_ref, gcol_ref, grow_ref, *ext_refs, tm, ngroups):
    W = DN_WIDTH
    gm = tm // ngroups
    ext = [ext_refs[3 * g:3 * g + 3] for g in range(ngroups)]

    @pl.when(pl.program_id(1) == 0)
    def _():
        for e_ref in ext[0]:
            e_ref[0:8, :] = jnp.zeros((8, W), f32)

    for g in range(ngroups):
        _inproj_rows(x_ref, w_ref, convw_ref, prow_ref, lng_ref, lnb_ref, ones_ref,
                     q_ref, k_ref, v_ref, z_ref, u_ref, vln_ref, gcol_ref, grow_ref, ext[g],
                     ext[(g + 1) % ngroups], r0=g * gm, gm=gm)


def _inproj_rows(x_ref, w_ref, convw_ref, prow_ref, lng_ref, lnb_ref, ones_ref,
                 q_ref, k_ref, v_ref, z_ref, u_ref, vln_ref, gcol_ref, grow_ref, ext, ext_next, *, r0, gm):
    W = DN_WIDTH
    rows = slice(r0, r0 + gm)
    xb = x_ref[0, rows, :].astype(bf16)
    for part, e_ref in enumerate(ext):
        e_ref[8:8 + gm, :] = jnp.dot(xb, w_ref[:, part * W:(part + 1) * W], preferred_element_type=f32)
    zc = 3 * W
    uc = zc + W
    vc = uc + SGU_WIDTH
    bc = vc + SGU_WIDTH

    def conv_silu(part):
        e_ref = ext[part]
        cs = slice(part * W, (part + 1) * W)
        y = convw_ref[3:4, cs] * e_ref[8:8 + gm, :]
        for j in range(CONV_K - 1):
            y = y + convw_ref[j:j + 1, cs] * e_ref[5 + j:5 + j + gm, :]
        ext_next[part][0:8, :] = e_ref[gm:gm + 8, :]
        return _silu(y)

    def group_sums(a):
        return jnp.dot(a.astype(bf16), ones_ref[...], preferred_element_type=f32)

    yq = conv_silu(0)
    ssq = group_sums(yq * yq)
    pba = jnp.dot(xb, w_ref[:, bc:bc + 128], preferred_element_type=f32)
    pv = jnp.dot(xb, w_ref[:, vc:vc + SGU_WIDTH], preferred_element_type=f32)
    yk = conv_silu(1)
    ssk = group_sums(yk * yk)
    pu = jnp.dot(xb, w_ref[:, uc:uc + SGU_WIDTH], preferred_element_type=f32)
    pz = jnp.dot(xb, w_ref[:, zc:zc + W], preferred_element_type=f32)
    q_ref[0, rows, :] = (yq * (lax.rsqrt(ssq + RMS_EPS) * HEAD_DIM ** -0.5)).astype(bf16)
    k_ref[0, rows, :] = (yk * lax.rsqrt(ssk + RMS_EPS)).astype(bf16)
    v_ref[0, rows, :] = conv_silu(2).astype(bf16)

    z_ref[0, rows, :] = _silu(pz).astype(bf16)

    u_ref[0, rows, :] = _gelu_tanh(pu).astype(bf16)
    pv = _gelu_tanh(pv)
    for g in range(SGU_GROUPS):
        sl = slice(g * SGU_CHUNK, (g + 1) * SGU_CHUNK)
        vg = pv[:, sl]
        mu = jnp.mean(vg, axis=-1, keepdims=True)
        vcn = vg - mu
        var = jnp.mean(vcn * vcn, axis=-1, keepdims=True)
        vln_ref[0, rows, sl] = (vcn * lax.rsqrt(var + LN_EPS) * lng_ref[:, sl] + lnb_ref[:, sl]).astype(bf16)

    lane = _iota2((DN_CHUNK, 128), 1)
    beta = _sigmoid(pba)
    lane8 = _iota2((8, DN_CHUNK), 1)
    sub8 = _iota2((8, DN_CHUNK), 0)
    for c in range(gm // DN_CHUNK):
        rs = slice(c * DN_CHUNK, (c + 1) * DN_CHUNK)
        os_ = slice(r0 + c * DN_CHUNK, r0 + (c + 1) * DN_CHUNK)
        pbat = pba[rs].T[0:8, :]
        gt = -jnp.exp(prow_ref[0]) * _softplus(pbat + prow_ref[1])
        gc = jnp.where(sub8 >= DN_HEADS, gt, 0.0)
        shift = 1
        while shift < DN_CHUNK:
            gc = gc + jnp.where(lane8 >= shift, pltpu.roll(gc, shift, axis=1), 0.0)
            shift *= 2
        grow_ref[0, :, os_] = gc
        gc_col = jnp.concatenate([gc, jnp.zeros((DN_CHUNK - 8, DN_CHUNK), f32)], axis=0).T
        gcol_ref[0, os_, :] = jnp.where(lane < DN_HEADS, beta[rs], gc_col)


def _stage_inproj(x, w_re, conv_w, prow, lng, lnb, *, tm, ngroups):
    B, T, _ = x.shape
    wcols = w_re.shape[1]
    grid = (B, T // tm)
    gi = lax.broadcasted_iota(i32, (DN_WIDTH, DN_WIDTH), 0) // 128
    gj = lax.broadcasted_iota(i32, (DN_WIDTH, DN_WIDTH), 1) // 128
    group_ones = (gi == gj).astype(bf16)
    act = lambda: jax.ShapeDtypeStruct((B, T, DN_WIDTH), bf16)
    act_spec = lambda: pl.BlockSpec((1, tm, DN_WIDTH), lambda b, t: (b, t, 0))
    const2 = lambda shp: pl.BlockSpec(shp, lambda b, t: (0, 0))
    return pl.pallas_call(
        functools.partial(_inproj_body, tm=tm, ngroups=ngroups),
        grid=grid,
        in_specs=[
            pl.BlockSpec((1, tm, D_MODEL), lambda b, t: (b, t, 0)),
            const2((D_MODEL, wcols)),
            const2((CONV_K, 3 * DN_WIDTH)),
            pl.BlockSpec((2, 8, 128), lambda b, t: (0, 0, 0)),
            const2((1, SGU_WIDTH)),
            const2((1, SGU_WIDTH)),
            const2((DN_WIDTH, DN_WIDTH)),
        ],
        out_specs=[act_spec() for _ in range(6)] + [
            pl.BlockSpec((1, tm, 128), lambda b, t: (b, t, 0)),
            pl.BlockSpec((1, 8, tm), lambda b, t: (b, 0, t)),
        ],
        out_shape=[act() for _ in range(6)] + [
            jax.ShapeDtypeStruct((B, T, 128), f32),
            jax.ShapeDtypeStruct((B, 8, T), f32),
        ],
        scratch_shapes=[pltpu.VMEM((tm // ngroups + 8, DN_WIDTH), f32) for _ in range(3 * ngroups)],
        compiler_params=_cparams(("arbitrary", "arbitrary")),
        name="inproj",
    )(x, w_re, conv_w, prow, lng, lnb, group_ones)


def _mm(a, b):
    return jnp.dot(a.astype(bf16), b.astype(bf16), preferred_element_type=f32)


def _unit_lower_inverse(nmats, ii, jj):
    n = nmats[0].shape[0]
    eye = (ii == jj).astype(f32)
    leaf = jnp.right_shift(ii, 3) == jnp.right_shift(jj, 3)
    dblk = [jnp.where(leaf, m, 0.0) for m in nmats]
    s1 = [_mm(d, d) for d in dblk]
    r1 = [eye - d for d in dblk]
    both = [_mm(s, jnp.concatenate([s, r], axis=1)) for s, r in zip(s1, r1)]
    r2 = [r + bo[:, n:] for r, bo in zip(r1, both)]
    xs = [r + _mm(bo[:, :n], r) for r, bo in zip(r2, both)]
    size = 8
    while size < n:
        lows = [slice(r + size, r + 2 * size) for r in range(0, n, 2 * size)]
        ups = [slice(r, r + size) for r in range(0, n, 2 * size)]
        rsel = _iota2((n // 2, n), 0)
        ilow = rsel + size * (jnp.right_shift(rsel, size.bit_length() - 1) + 1)
        jlow = _iota2((n // 2, n), 1)
        in_pair_upper = (jlow >= ilow - (ilow & (2 * size - 1))) & (jlow < ilow - (ilow & (size - 1)))
        zeros = jnp.zeros((size, n), f32)
        new_xs = []
        ylows = [_mm(jnp.where(in_pair_upper, jnp.concatenate([m[s] for s in lows], axis=0), 0.0), x)
                 for m, x in zip(nmats, xs)]
        yfull = [jnp.concatenate([piece for k in range(len(lows)) for piece in (zeros, y[k * size:(k + 1) * size])], axis=0)
                 for y in ylows]
        corr = [_mm(jnp.concatenate([x[s] for s in lows], axis=0), yf) for x, yf in zip(xs, yfull)]
        for x, c in zip(xs, corr):
            pieces = []
            for k, (u, l) in enumerate(zip(ups, lows)):
                pieces += [x[u], x[l] - c[k * size:(k + 1) * size]]
            new_xs.append(jnp.concatenate(pieces, axis=0))
        xs = new_xs
        size *= 2
    return xs


def _deltanet_body(q_ref, k_ref, v_ref, z_ref, gcol_ref, grow_ref, nw_ref, y_ref, s_ref, *, nbr, nch, ngroups):
    @pl.when(pl.program_id(1) == 0)
    def _():
        s_ref[...] = jnp.zeros(s_ref.shape, f32)

    per = nbr // ngroups
    for g in range(ngroups):
        _deltanet_rows(q_ref, k_ref, v_ref, z_ref, gcol_ref, grow_ref, nw_ref, y_ref, s_ref,
                       rows=range(g * per, (g + 1) * per), nch=nch)


def _deltanet_rows(q_ref, k_ref, v_ref, z_ref, gcol_ref, grow_ref, nw_ref, y_ref, s_ref, *, rows, nch):
    C = DN_CHUNK
    S = [(b, h) for b in rows for h in range(DN_HEADS)]
    P = [(b, c, h) for c in range(nch) for b, h in S]

    ii = _iota2((C, C), 0)
    jj = _iota2((C, C), 1)
    causal = ii >= jj
    rs = [slice(c * C, (c + 1) * C) for c in range(nch)]
    hs = [slice(h * HEAD_DIM, (h + 1) * HEAD_DIM) for h in range(DN_HEADS)]
    gcol = {(b, c): gcol_ref[b, rs[c], :] for b in rows for c in range(nch)}
    qh = {(b, c, h): q_ref[b, rs[c], hs[h]] for b, c, h in P}
    kh = {(b, c, h): k_ref[b, rs[c], hs[h]] for b, c, h in P}
    vh = {(b, c, h): v_ref[b, rs[c], hs[h]] for b, c, h in P}
    gc_b = {(b, c, h): jnp.broadcast_to(gcol[b, c][:, DN_HEADS + h:DN_HEADS + h + 1], (C, HEAD_DIM)) for b, c, h in P}
    beta_b = {(b, c, h): jnp.broadcast_to(gcol[b, c][:, h:h + 1], (C, HEAD_DIM)) for b, c, h in P}
    gc_r = {(b, c, h): jnp.broadcast_to(grow_ref[b, DN_HEADS + h:DN_HEADS + h + 1, rs[c]], (C, C)) for b, c, h in P}
    decay = {p: jnp.exp(jnp.where(causal, gc_b[p] - gc_r[p], -1e30)) for p in P}

    kf = {p: kh[p].astype(f32) for p in P}
    kb = {p: kf[p] * beta_b[p] for p in P}
    kk = {p: lax.dot_general(kb[p].astype(bf16), kh[p], NT_DIMS, preferred_element_type=f32) for p in P}
    a_intra = {p: lax.dot_general(qh[p], kh[p], NT_DIMS, preferred_element_type=f32) * decay[p] for p in P}
    nmat = [jnp.where(ii > jj, kk[p] * decay[p], 0.0) for p in P]
    tinv = dict(zip(P, _unit_lower_inverse(nmat, ii, jj)))

    eg = {p: jnp.exp(gc_b[p]) for p in P}
    rhs = {p: jnp.concatenate([vh[p].astype(f32) * beta_b[p], kb[p] * eg[p]], axis=1) for p in P}
    sol = {p: _mm(tinv[p], rhs[p]) for p in P}

    q_dec = {p: qh[p].astype(f32) * eg[p] for p in P}
    g_last = {p: gc_b[p][C - 1:C, :] for p in P}
    kdt = {p: (kf[p] * jnp.exp(g_last[p] - gc_b[p])).T for p in P}

    state = {(b, h): s_ref[b * DN_HEADS + h] for b, h in S}
    for c in range(nch):
        m1 = {(b, h): _mm(jnp.concatenate([sol[b, c, h][:, HEAD_DIM:], q_dec[b, c, h]], axis=0), state[b, h]) for b, h in S}
        v_new = {(b, h): sol[b, c, h][:, :HEAD_DIM] - m1[b, h][:C] for b, h in S}
        m2 = {(b, h): _mm(jnp.concatenate([a_intra[b, c, h], kdt[b, c, h]], axis=0), v_new[b, h]) for b, h in S}
        state = {(b, h): state[b, h] * jnp.exp(g_last[b, c, h]) + m2[b, h][C:] for b, h in S}
        for b, h in S:
            o = m1[b, h][C:] + m2[b, h][:C]
            rms = lax.rsqrt(jnp.mean(o * o, axis=-1, keepdims=True) + RMS_EPS)
            y_ref[b, rs[c], hs[h]] = (o * rms * nw_ref[...] * z_ref[b, rs[c], hs[h]].astype(f32)).astype(bf16)
    for b, h in S:
        s_ref[b * DN_HEADS + h] = state[b, h]


def _stage_deltanet(q, k, v, z, gcol, grow, norm_w, *, nbr, nch, ngroups):
    B, T, _ = q.shape
    tt = nch * DN_CHUNK
    act_spec = lambda: pl.BlockSpec((nbr, tt, DN_WIDTH), lambda b, t: (b, t, 0))
    return pl.pallas_call(
        functools.partial(_deltanet_body, nbr=nbr, nch=nch, ngroups=ngroups),
        grid=(B // nbr, T // tt),
        in_specs=[act_spec(), act_spec(), act_spec(), act_spec(),
                  pl.BlockSpec((nbr, tt, 128), lambda b, t: (b, t, 0)),
                  pl.BlockSpec((nbr, 8, tt), lambda b, t: (b, 0, t)),
                  pl.BlockSpec((1, HEAD_DIM), lambda b, t: (0, 0))],
        out_specs=act_spec(),
        out_shape=jax.ShapeDtypeStruct((B, T, DN_WIDTH), bf16),
        scratch_shapes=[pltpu.VMEM((nbr * DN_HEADS, HEAD_DIM, HEAD_DIM), f32)],
        compiler_params=_cparams(("arbitrary", "arbitrary")),
        name="deltanet",
    )(q, k, v, z, gcol, grow, norm_w)


def _mixout_body(ydn_ref, u_ref, vln_ref, x_hbm, ws_ref, bsp_ref, wout_ref, g1_ref, b1_ref, wrt_ref, brt_ref,
                 h_ref, hrow_ref, ids_ref, wts_ref, cnt_ref, ycat_ref, xring_ref, xsem, *, tm, nsteps):
    C = SGU_CHUNK
    step = pl.program_id(0) * pl.num_programs(1) + pl.program_id(1)
    slot = lax.rem(step, X_SLOTS)

    def x_copy(s, sl):
        return pltpu.make_async_copy(x_hbm.at[s], xring_ref.at[sl], xsem.at[sl])

    @pl.when(step == 0)
    def _():
        cnt_ref[...] = jnp.zeros(cnt_ref.shape, f32)
        for j in range(X_SLOTS - 1):
            x_copy(j, j).start()

    ahead = step + (X_SLOTS - 1)

    @pl.when(ahead < nsteps)
    def _():
        x_copy(ahead, lax.rem(ahead, X_SLOTS)).start()

    x_copy(step, slot).wait()
    ii = _iota2((C, C), 0)
    jj = _iota2((C, C), 1)
    ycat_ref[:, 0:DN_WIDTH] = ydn_ref[0]
    for g in range(SGU_GROUPS):
        gs = slice(g * C, (g + 1) * C)
        wsg = jnp.where(ii >= jj, ws_ref[g], 0.0).astype(bf16)
        for c in range(tm // C):
            rs = slice(c * C, (c + 1) * C)
            mixed = jnp.dot(wsg, vln_ref[0, rs, gs], preferred_element_type=f32) + bsp_ref[:, gs]
            ycat_ref[rs, DN_WIDTH + g * C:DN_WIDTH + (g + 1) * C] = (u_ref[0, rs, gs].astype(f32) * mixed).astype(bf16)

    RB = 128
    blocks = [slice(r, r + RB) for r in range(0, tm, RB)]
    mix = [jnp.dot(ycat_ref[rb, :], wout_ref[...], preferred_element_type=f32) for rb in blocks]
    h1s = []
    for rb, m in zip(blocks, mix):
        hp = DEEPNORM_ALPHA * xring_ref[slot, rb, :] + m
        mu = jnp.mean(hp, axis=-1, keepdims=True)
        hc = hp - mu
        var = jnp.mean(hc * hc, axis=-1, keepdims=True)
        h1 = hc * lax.rsqrt(var + LN_EPS) * g1_ref[...] + b1_ref[...]
        h_ref[0, rb, :] = h1
        h1b = h1.astype(bf16)
        hrow_ref[rb] = h1b.reshape(RB, ROW_TILE, 128)
        h1s.append(h1b)

    logit_blocks = [lax.dot_general(wrt_ref[...], hb, NT_DIMS, preferred_element_type=f32) + brt_ref[...] for hb in h1s]
    sub = _iota2((8, RB), 0)
    subf = sub.astype(f32)
    sub_e = _iota2((N_EXPERTS, RB), 0).astype(f32)
    chosen = []
    for rb, logits in zip(blocks, logit_blocks):
        gl = logits[0:8]
        gmax = jnp.max(gl, axis=0, keepdims=True)
        g_idx = jnp.min(jnp.where(gl == gmax, subf, float(MOE_GROUPS)), axis=0, keepdims=True)
        p_group = 1.0 / jnp.sum(jnp.exp(gl - gmax), axis=0, keepdims=True)
        within = jnp.zeros((8, RB), f32)
        for g in range(MOE_GROUPS):
            within = within + jnp.where(g_idx == float(g), logits[8 + 8 * g:16 + 8 * g], 0.0)
        m1 = jnp.max(within, axis=0, keepdims=True)
        i1 = jnp.min(jnp.where(within == m1, subf, float(EXPERTS_PER_GROUP)), axis=0, keepdims=True)
        rest = jnp.where(subf == i1, -jnp.inf, within)
        m2 = jnp.max(rest, axis=0, keepdims=True)
        i2 = jnp.min(jnp.where(rest == m2, subf, float(EXPERTS_PER_GROUP)), axis=0, keepdims=True)
        e = jnp.exp(m2 - m1)
        w1 = p_group / (1.0 + e)
        w2 = p_group * e / (1.0 + e)
        e1 = g_idx * float(EXPERTS_PER_GROUP) + i1
        e2 = g_idx * float(EXPERTS_PER_GROUP) + i2
        ids_ref[:, rb] = jnp.where(sub == 0, e1, jnp.where(sub == 1, e2, 0.0)).astype(i32)
        wts_ref[:, rb] = jnp.where(sub == 0, w1, jnp.where(sub == 1, w2, 0.0))
        chosen.append(((sub_e == e1).astype(f32) + (sub_e == e2).astype(f32)).astype(bf16))
    ones = jnp.ones((RB, 128), bf16)
    cnt_ref[...] = cnt_ref[...] + sum(jnp.dot(oh, ones, preferred_element_type=f32) for oh in chosen)


def _stage_mixout(ydn, u, vln, x, ws, bsp, wout, g1, b1, wrt, brt, *, tm):
    B, T, _ = x.shape
    nt = T // tm
    act_spec = lambda: pl.BlockSpec((1, tm, DN_WIDTH), lambda b, t: (b, t, 0))
    const2 = lambda shp: pl.BlockSpec(shp, lambda b, t: (0, 0))
    tok_spec = lambda: pl.BlockSpec((8, tm), lambda b, t: (0, b * nt + t))
    return pl.pallas_call(
        functools.partial(_mixout_body, tm=tm, nsteps=B * nt),
        grid=(B, nt),
        in_specs=[act_spec(), act_spec(), act_spec(),
                  pl.BlockSpec(memory_space=pl.ANY),
                  pl.BlockSpec((SGU_GROUPS, SGU_CHUNK, SGU_CHUNK), lambda b, t: (0, 0, 0)),
                  const2((SGU_CHUNK, SGU_WIDTH)),
                  const2((D_MODEL, D_MODEL)),
                  const2((1, D_MODEL)), const2((1, D_MODEL)),
                  const2((128, D_MODEL)), const2((128, 128))],
        out_specs=[pl.BlockSpec((1, tm, D_MODEL), lambda b, t: (b, t, 0)),
                   pl.BlockSpec((tm, ROW_TILE, 128), lambda b, t: (b * nt + t, 0, 0)), tok_spec(), tok_spec(),
                   const2((N_EXPERTS, 128))],
        out_shape=[jax.ShapeDtypeStruct((B, T, D_MODEL), f32),
                   jax.ShapeDtypeStruct((B * T, ROW_TILE, 128), bf16),
                   jax.ShapeDtypeStruct((8, B * T), i32),
                   jax.ShapeDtypeStruct((8, B * T), f32),
                   jax.ShapeDtypeStruct((N_EXPERTS, 128), f32)],
        scratch_shapes=[pltpu.VMEM((tm, D_MODEL), bf16), pltpu.VMEM((X_SLOTS, tm, D_MODEL), f32),
                        pltpu.SemaphoreType.DMA((X_SLOTS,))],
        compiler_params=_cparams(("arbitrary", "arbitrary")),
        name="mixout",
    )(ydn, u, vln, x.reshape(B * nt, tm, D_MODEL), ws, bsp, wout, g1, b1, wrt, brt)


def _route_body(cnt_ref, ids_ref, dest_ref, meta_ref, blk_ref, pstart_ref, *, tm, nb_pad):
    i = pl.program_id(0)
    sub = _iota2((N_EXPERTS, tm), 0)
    is1 = sub == ids_ref[0:1, :]
    is2 = sub == ids_ref[1:2, :]
    oh = (is1.astype(f32) + is2.astype(f32)).astype(bf16)

    @pl.when(i == 0)
    def _():
        cnt = cnt_ref[...]
        padded = jnp.floor((cnt + (MOE_BLOCK - 1)) * (1.0 / MOE_BLOCK)) * MOE_BLOCK
        ei = _iota2((N_EXPERTS, N_EXPERTS), 0)
        ej = _iota2((N_EXPERTS, N_EXPERTS), 1)
        pends = jnp.dot((ei >= ej).astype(f32), padded, precision=HIGHEST, preferred_element_type=f32)
        pstart = pends - padded
        pstart_ref[...] = pstart
        s64 = _iota2((N_EXPERTS, 128), 0)
        l64 = _iota2((N_EXPERTS, 128), 1)
        diag = s64 == l64
        fill_off = jnp.sum(jnp.where(diag, pstart + cnt, 0.0), axis=0, keepdims=True)
        fill_n = jnp.sum(jnp.where(diag, padded - cnt, 0.0), axis=0, keepdims=True)
        nused = pends[N_EXPERTS - 1:N_EXPERTS, :] * (1.0 / MOE_BLOCK)
        m8 = _iota2((8, 128), 0)
        meta_ref[...] = jnp.where(m8 == 0, fill_off, jnp.where(m8 == 1, fill_n, jnp.where(m8 == 2, nused, 0.0))).astype(i32)
        bstart = (_iota2((N_EXPERTS, nb_pad), 1) * MOE_BLOCK).astype(f32)
        pe = jnp.concatenate([pends] * (nb_pad // 128), axis=1)
        be = jnp.sum((pe <= bstart).astype(f32), axis=0, keepdims=True)
        be = jnp.minimum(be, float(N_EXPERTS - 1))
        blk_ref[...] = jnp.broadcast_to(be, (8, nb_pad)).astype(i32)

    ti = _iota2((tm, tm), 0)
    tj = _iota2((tm, tm), 1)
    before = (ti < tj).astype(bf16)
    prefix = jnp.dot(oh, before, preferred_element_type=f32)
    nxt = prefix + jnp.concatenate([pstart_ref[...]] * (tm // 128), axis=1)
    d1 = jnp.sum(jnp.where(is1, nxt, 0.0), axis=0, keepdims=True)
    d2 = jnp.sum(jnp.where(is2, nxt, 0.0), axis=0, keepdims=True)
    sub8 = _iota2((8, tm), 0)
    dest_ref[...] = jnp.where(sub8 == 0, d1, jnp.where(sub8 == 1, d2, 0.0)).astype(i32)
    pstart_ref[...] = pstart_ref[...] + jnp.dot(oh, jnp.ones((tm, 128), bf16), preferred_element_type=f32)


def _stage_route(cnt, ids, *, tm, nb_pad):
    n = ids.shape[1]
    return pl.pallas_call(
        functools.partial(_route_body, tm=tm, nb_pad=nb_pad),
        grid=(n // tm,),
        in_specs=[pl.BlockSpec((N_EXPERTS, 128), lambda i: (0, 0)), pl.BlockSpec((8, tm), lambda i: (0, i))],
        out_specs=[pl.BlockSpec((8, tm), lambda i: (0, i)),
                   pl.BlockSpec((8, 128), lambda i: (0, 0)),
                   pl.BlockSpec((8, nb_pad), lambda i: (0, 0))],
        out_shape=[jax.ShapeDtypeStruct((8, n), i32), jax.ShapeDtypeStruct((8, 128), i32),
                   jax.ShapeDtypeStruct((8, nb_pad), i32)],
        scratch_shapes=[pltpu.VMEM((N_EXPERTS, 128), f32)],
        compiler_params=_cparams(("arbitrary",)),
        name="moe_route",
    )(cnt, ids)


def _dispatch_body(fill_off_ref, fill_n_ref, nused_ref, dest_ref, h_hbm, xs_ref, src_ref, zero_ref, isem, sem, zsem,
                   *, tm, nsteps):
    i = pl.program_id(0)
    slot = lax.rem(i, X_SLOTS)

    def in_copy(step, s):
        return pltpu.make_async_copy(h_hbm.at[pl.ds(step * tm, tm)], src_ref.at[s], isem.at[s])

    def drain(s):
        for _ in range(2):
            pltpu.make_async_copy(src_ref.at[s], xs_ref.at[pl.ds(0, tm)], sem.at[s]).wait()

    @pl.when(i == 0)
    def _():
        for j in range(min(X_SLOTS - 1, nsteps)):
            in_copy(j, j).start()

    in_copy(i, slot).wait()

    def row_copy(t, d):
        return pltpu.make_async_copy(src_ref.at[slot, t], xs_ref.at[d], sem.at[slot])

    def issue(t, carry):
        row_copy(t, dest_ref[0, 0, t]).start(priority=0)
        row_copy(t, dest_ref[0, 1, t]).start(priority=1)
        return carry

    lax.fori_loop(0, tm, issue, 0, unroll=8)

    @pl.when(i == 0)
    def _():
        zero_ref[...] = jnp.zeros(zero_ref.shape, bf16)

        def fill(start):
            def body(e, carry):
                off = fill_off_ref[e]
                npad = fill_n_ref[e]
                bit = MOE_BLOCK // 2
                while bit:
                    @pl.when((npad & bit) != 0)
                    def _(off=off, bit=bit):
                        cp = pltpu.make_async_copy(zero_ref.at[pl.ds(0, bit)], xs_ref.at[pl.ds(off, bit)], zsem)
                        cp.start() if start else cp.wait()
                    off = off + (npad & bit)
                    bit //= 2
                return carry
            return body

        lax.fori_loop(0, N_EXPERTS, fill(True), 0)
        lax.fori_loop(0, N_EXPERTS, fill(False), 0)

        def tail_copy(b):
            return pltpu.make_async_copy(zero_ref, xs_ref.at[pl.ds(b * MOE_BLOCK, MOE_BLOCK)], zsem)

        nblocks = xs_ref.shape[0] // MOE_BLOCK
        lax.fori_loop(nused_ref[0], nblocks, lambda b, c: (tail_copy(b).start(), c)[1], 0)
        lax.fori_loop(nused_ref[0], nblocks, lambda b, c: (tail_copy(0).wait(), c)[1], 0)

    @pl.when(i >= 1)
    def _():
        drain(lax.rem(i - 1, X_SLOTS))

    ahead = i + (X_SLOTS - 1)

    @pl.when(ahead < nsteps)
    def _():
        in_copy(ahead, lax.rem(ahead, X_SLOTS)).start()

    @pl.when(i == nsteps - 1)
    def _():
        drain(slot)


def _stage_dispatch(fill_off, fill_n, nused, dest3, hrow, p_rows, *, tm):
    n = hrow.shape[0]
    nsteps = n // tm
    return pl.pallas_call(
        functools.partial(_dispatch_body, tm=tm, nsteps=nsteps),
        grid_spec=pltpu.PrefetchScalarGridSpec(
            num_scalar_prefetch=3,
            grid=(nsteps,),
            in_specs=[pl.BlockSpec((1, 2, tm), lambda i, fo, fn, nu: (i, 0, 0), memory_space=pltpu.SMEM),
                      pl.BlockSpec(memory_space=pl.ANY)],
            out_specs=pl.BlockSpec(memory_space=pl.ANY),
            scratch_shapes=[pltpu.VMEM((X_SLOTS, tm, ROW_TILE, 128), bf16),
                            pltpu.VMEM((MOE_BLOCK, ROW_TILE, 128), bf16),
                            pltpu.SemaphoreType.DMA((X_SLOTS,)), pltpu.SemaphoreType.DMA((X_SLOTS,)),
                            pltpu.SemaphoreType.DMA],
        ),
        out_shape=jax.ShapeDtypeStruct((p_rows, ROW_TILE, 128), bf16),
        compiler_params=_cparams(("arbitrary",)),
        name="moe_dispatch",
    )(fill_off, fill_n, nused, dest3, hrow)


def _experts_body(blk_ref, nused_ref, xs_hbm, wg_hbm, wu_hbm, wd_hbm, ys_hbm,
                  xbuf_ref, wg32_ref, wu32_ref, wd32_ref, wgu16_ref, wd16_ref, obuf_ref, xsem, wsem, osem,
                  *, nsteps):
    i = pl.program_id(0)
    nused = nused_ref[0]
    used = i < nused
    e = blk_ref[i]
    slot = lax.rem(i, X_SLOTS)

    def weight_copies(ex):
        return (pltpu.make_async_copy(wg_hbm.at[ex], wg32_ref, wsem.at[0]),
                pltpu.make_async_copy(wu_hbm.at[ex], wu32_ref, wsem.at[1]),
                pltpu.make_async_copy(wd_hbm.at[ex], wd32_ref, wsem.at[2]))

    def x_copy(block, s):
        return pltpu.make_async_copy(xs_hbm.at[pl.ds(block * MOE_BLOCK, MOE_BLOCK)], xbuf_ref.at[s], xsem.at[s])

    @pl.when((i == 0) & used)
    def _():
        for cp in weight_copies(e):
            cp.start()
        for j in range(X_SLOTS - 1):
            @pl.when(j < nused)
            def _(j=j):
                x_copy(j, j).start()

    ahead = i + (X_SLOTS - 1)

    @pl.when(ahead < nused)
    def _():
        x_copy(ahead, lax.rem(ahead, X_SLOTS)).start()

    @pl.when(used & ((i == 0) | (e != blk_ref[jnp.maximum(i - 1, 0)])))
    def _():
        for cp in weight_copies(e):
            cp.wait()
        wgu16_ref[:, 0:D_EXPERT] = wg32_ref[...].astype(bf16)
        wgu16_ref[:, D_EXPERT:2 * D_EXPERT] = wu32_ref[...].astype(bf16)
        wd16_ref[...] = wd32_ref[...].astype(bf16)
        nxt = lax.while_loop(lambda j: (j < nused) & (blk_ref[jnp.minimum(j, nused - 1)] == e), lambda j: j + 1, i + 1)

        @pl.when(nxt < nused)
        def _():
            for cp in weight_copies(blk_ref[jnp.minimum(nxt, nused - 1)]):
                cp.start(priority=1)

    def out_copy(block, s):
        return pltpu.make_async_copy(obuf_ref.at[s], ys_hbm.at[pl.ds(block * MOE_BLOCK, MOE_BLOCK)], osem.at[s])

    @pl.when(i >= X_SLOTS)
    def _():
        out_copy(i - X_SLOTS, slot).wait()

    @pl.when(used)
    def _():
        x_copy(i, slot).wait()
        half = MOE_BLOCK // 2
        rows = [slice(p * half, (p + 1) * half) for p in range(2)]
        gu = [jnp.dot(xbuf_ref[slot, r].reshape(half, D_MODEL), wgu16_ref[...], preferred_element_type=f32)
              for r in rows]
        hid = [(_silu(g[:, :D_EXPERT]) * g[:, D_EXPERT:]).astype(bf16) for g in gu]
        y = [jnp.dot(hd, wd16_ref[...], preferred_element_type=f32) for hd in hid]
        for r, yp in zip(rows, y):
            obuf_ref[slot, r] = yp.astype(bf16).reshape(half, ROW_TILE, 128)

    @pl.when(jnp.logical_not(used))
    def _():
        obuf_ref[slot] = jnp.zeros(obuf_ref.shape[1:], bf16)

    out_copy(i, slot).start()

    @pl.when(i == nsteps - 1)
    def _():
        for step in range(nsteps - X_SLOTS, nsteps):
            out_copy(step, step % X_SLOTS).wait()


def _stage_experts(blk_e, nused, xs, w_gate, w_up, w_down):
    p_rows = xs.shape[0]
    nb = p_rows // MOE_BLOCK
    assert nb >= X_SLOTS

    return pl.pallas_call(
        functools.partial(_experts_body, nsteps=nb),
        grid_spec=pltpu.PrefetchScalarGridSpec(
            num_scalar_prefetch=2,
            grid=(nb,),
            in_specs=[pl.BlockSpec(memory_space=pl.ANY),
                      pl.BlockSpec(memory_space=pl.ANY),
                      pl.BlockSpec(memory_space=pl.ANY),
                      pl.BlockSpec(memory_space=pl.ANY)],
            out_specs=pl.BlockSpec(memory_space=pl.ANY),
            scratch_shapes=[pltpu.VMEM((X_SLOTS, MOE_BLOCK, ROW_TILE, 128), bf16),
                            pltpu.VMEM((D_MODEL, D_EXPERT), f32), pltpu.VMEM((D_MODEL, D_EXPERT), f32),
                            pltpu.VMEM((D_EXPERT, D_MODEL), f32),
                            pltpu.VMEM((D_MODEL, 2 * D_EXPERT), bf16), pltpu.VMEM((D_EXPERT, D_MODEL), bf16),
                            pltpu.VMEM((X_SLOTS, MOE_BLOCK, ROW_TILE, 128), bf16),
                            pltpu.SemaphoreType.DMA((X_SLOTS,)), pltpu.SemaphoreType.DMA((3,)),
                            pltpu.SemaphoreType.DMA((X_SLOTS,))],
        ),
        out_shape=jax.ShapeDtypeStruct((p_rows, ROW_TILE, 128), bf16),
        compiler_params=_cparams(("arbitrary",)),
        name="moe_experts",
    )(blk_e, nused, xs, w_gate, w_up, w_down)


def _combine_body(dcur_ref, dnext_ref, h_ref, wts_ref, g2_ref, b2_ref, ys_ref, o_ref, ybuf_ref, sem, *, tm, nsteps):
    i = pl.program_id(0)
    slot = lax.rem(i, 2)

    def issue_tile(d_ref, s):
        def body(t, carry):
            for k in range(2):
                pltpu.make_async_copy(ys_ref.at[d_ref[0, k, t]], ybuf_ref.at[s, k, t], sem.at[s]).start(priority=k)
            return carry

        lax.fori_loop(0, tm, body, 0, unroll=8)

    @pl.when(i == 0)
    def _():
        issue_tile(dcur_ref, 0)

    @pl.when(i + 1 < nsteps)
    def _():
        issue_tile(dnext_ref, 1 - slot)

    for k in range(2):
        pltpu.make_async_copy(ys_ref.at[pl.ds(0, tm)], ybuf_ref.at[slot, k], sem.at[slot]).wait()

    pieces = []
    for c in range(tm // 128):
        ls = slice(c * 128, (c + 1) * 128)
        w1c = jnp.broadcast_to(wts_ref[0:1, ls], (128, 128)).T
        w2c = jnp.broadcast_to(wts_ref[1:2, ls], (128, 128)).T
        w1f = jnp.concatenate([w1c] * (D_MODEL // 128), axis=1)
        w2f = jnp.concatenate([w2c] * (D_MODEL // 128), axis=1)
        y1 = ybuf_ref[slot, 0, ls].reshape(128, D_MODEL).astype(f32)
        y2 = ybuf_ref[slot, 1, ls].reshape(128, D_MODEL).astype(f32)
        pieces.append(w1f * y1 + w2f * y2)
    ffn = jnp.concatenate(pieces, axis=0)
    hp = DEEPNORM_ALPHA * h_ref[...] + ffn
    mu = jnp.mean(hp, axis=-1, keepdims=True)
    hc = hp - mu
    var = jnp.mean(hc * hc, axis=-1, keepdims=True)
    o_ref[...] = hc * lax.rsqrt(var + LN_EPS) * g2_ref[...] + b2_ref[...]


def _stage_combine(dest3, h2, wts, g2, b2, ys, *, tm):
    n = h2.shape[0]
    nsteps = n // tm
    return pl.pallas_call(
        functools.partial(_combine_body, tm=tm, nsteps=nsteps),
        grid=(nsteps,),
        in_specs=[pl.BlockSpec((1, 2, tm), lambda i: (i, 0, 0), memory_space=pltpu.SMEM),
                  pl.BlockSpec((1, 2, tm), lambda i: (jnp.minimum(i + 1, nsteps - 1), 0, 0), memory_space=pltpu.SMEM),
                  pl.BlockSpec((tm, D_MODEL), lambda i: (i, 0)),
                  pl.BlockSpec((8, tm), lambda i: (0, i)),
                  pl.BlockSpec((1, D_MODEL), lambda i: (0, 0)),
                  pl.BlockSpec((1, D_MODEL), lambda i: (0, 0)),
                  pl.BlockSpec(memory_space=pl.ANY)],
        out_specs=pl.BlockSpec((tm, D_MODEL), lambda i: (i, 0)),
        out_shape=jax.ShapeDtypeStruct((n, D_MODEL), f32),
        scratch_shapes=[pltpu.VMEM((2, 2, tm, ROW_TILE, 128), bf16), pltpu.SemaphoreType.DMA((2,))],
        compiler_params=_cparams(("arbitrary",)),
        name="moe_combine",
    )(dest3, dest3, h2, wts, g2, b2, ys)


def _layer(h, w_in, conv_w, a_log, dt_bias, dn_norm_w, sgu_ln_g, sgu_ln_b, w_spatial, b_spatial, w_out,
           ln1_g, ln1_b, w_rg, b_rg, w_re, b_re, w_gate, w_up, w_down, ln2_g, ln2_b,
           *, tm_in, in_groups, dn_rows, dn_chunks, dn_groups, tm_mix, tm_rank, tm_disp, tm_comb):
    B, T, _ = h.shape
    n = B * T
    w_cols = _stage_weight_layout(w_in)
    decay_prm = jnp.stack([a_log, dt_bias])
    prow = jnp.broadcast_to(jnp.pad(decay_prm, ((0, 0), (DN_HEADS, 8 - 2 * DN_HEADS)))[:, :, None], (2, 8, 128))

    q, k, v, z, u, vln, gcol, grow = _stage_inproj(
        h, w_cols, conv_w, prow, sgu_ln_g[None, :], sgu_ln_b[None, :], tm=tm_in, ngroups=in_groups)
    ydn = _stage_deltanet(q, k, v, z, gcol, grow, dn_norm_w[None, :], nbr=dn_rows, nch=dn_chunks, ngroups=dn_groups)

    bsp = jnp.broadcast_to(b_spatial.T[:, :, None], (SGU_CHUNK, SGU_GROUPS, SGU_CHUNK)).reshape(SGU_CHUNK, SGU_WIDTH)
    n_logit = MOE_GROUPS + N_EXPERTS
    wrt = jnp.pad(jnp.concatenate([w_rg, w_re], axis=1).T, ((0, 128 - n_logit), (0, 0))).astype(bf16)
    brt = jnp.broadcast_to(jnp.pad(jnp.concatenate([b_rg, b_re]), (0, 128 - n_logit))[:, None], (128, 128))
    h1, hrow, ids, wts, cnt = _stage_mixout(ydn, u, vln, h, w_spatial, bsp, w_out.astype(bf16), ln1_g[None, :],
                                       ln1_b[None, :], wrt, brt, tm=tm_mix)

    p_rows = (-(-(n * 2) // MOE_BLOCK)) * MOE_BLOCK + N_EXPERTS * MOE_BLOCK
    nb = p_rows // MOE_BLOCK
    nb_pad = (-(-nb // 128)) * 128
    dest, meta, blk = _stage_route(cnt, ids, tm=tm_rank, nb_pad=nb_pad)

    h2 = h1.reshape(n, D_MODEL)
    dest_d = dest[0:2].reshape(2, n // tm_disp, tm_disp).transpose(1, 0, 2)
    xs = _stage_dispatch(meta[0, :N_EXPERTS], meta[1, :N_EXPERTS], meta[2, 0:1], dest_d, hrow, p_rows, tm=tm_disp)
    ys = _stage_experts(blk[0, :nb], meta[2, 0:1], xs, w_gate, w_up, w_down)
    dest_c = dest[0:2].reshape(2, n // tm_comb, tm_comb).transpose(1, 0, 2)
    out = _stage_combine(dest_c, h2, wts, ln2_g[None, :], ln2_b[None, :], ys, tm=tm_comb)
    return out.reshape(B, T, D_MODEL)


def kernel(x, w_in, conv_w, a_log, dt_bias, dn_norm_w, sgu_ln_g, sgu_ln_b, w_spatial, b_spatial, w_out, ln1_g, ln1_b, w_router_group, b_router_group, w_router_expert, b_router_expert, w_gate, w_up, w_down, ln2_g, ln2_b):
    h = x
    for l in range(w_in.shape[0]):
        h = _layer(h, w_in[l], conv_w[l], a_log[l], dt_bias[l], dn_norm_w[l], sgu_ln_g[l], sgu_ln_b[l],
                   w_spatial[l], b_spatial[l], w_out[l], ln1_g[l], ln1_b[l],
                   w_router_group[l], b_router_group[l], w_router_expert[l], b_router_expert[l],
                   w_gate[l], w_up[l], w_down[l], ln2_g[l], ln2_b[l],
                   tm_in=512, in_groups=1, dn_rows=8, dn_chunks=2, dn_groups=4, tm_mix=1024, tm_rank=1024, tm_disp=2048, tm_comb=1024)
    return h
```

```python
import functools

import jax
import jax.numpy as jnp
from jax import lax
from jax.experimental import pallas as pl
from jax.experimental.pallas import tpu as pltpu

f32 = jnp.float32
bf16 = jnp.bfloat16
i32 = jnp.int32

D_MODEL = 1024
DN_WIDTH = 512
DN_HEADS = 4
HEAD_DIM = 128
CONV_K = 4
SGU_WIDTH = 512
SGU_GROUPS = 4
SGU_CHUNK = 128
DN_CHUNK = 128
MOE_GROUPS = 8
EXPERTS_PER_GROUP = 8
N_EXPERTS = 64
D_EXPERT = 512
MOE_BLOCK = 256
IN_COLS_ALIGNED = 4 * DN_WIDTH + 2 * SGU_WIDTH + 128
X_SLOTS = 4
ROW_TILE = D_MODEL // 128
DEEPNORM_ALPHA = 2.0 ** 0.25
LN_EPS = 1e-5
RMS_EPS = 1e-6
HIGHEST = lax.Precision.HIGHEST
VMEM_LIMIT_BYTES = 56 * 1024 * 1024

NT_DIMS = (((1,), (1,)), ((), ()))


def _cparams(sem, flags=None):
    return pltpu.CompilerParams(dimension_semantics=sem, vmem_limit_bytes=VMEM_LIMIT_BYTES, flags=flags)


def _sigmoid(x):
    return 1.0 / (1.0 + jnp.exp(-x))


def _silu(x):
    h = 0.5 * x
    return h + h * jnp.tanh(h)


def _softplus(x):
    return jnp.maximum(x, 0.0) + jnp.log1p(jnp.exp(-jnp.abs(x)))


def _gelu_tanh(x):
    c = 0.7978845608028654
    return x * (0.5 * (1.0 + jnp.tanh(c * (x + 0.044715 * (x * x * x)))))


def _iota2(shape, axis):
    return lax.broadcasted_iota(i32, shape, axis)


def _weight_layout_body(w_ref, o_ref):
    qkvz = 4 * DN_WIDTH
    uv0 = qkvz + 2 * DN_HEADS
    rows = w_ref.shape[0]
    o_ref[:, 0:qkvz] = w_ref[:, 0:qkvz].astype(bf16)
    o_ref[:, qkvz:qkvz + 2 * SGU_WIDTH] = w_ref[:, uv0:uv0 + 2 * SGU_WIDTH].astype(bf16)
    ba = jnp.concatenate([w_ref[:, qkvz:uv0], jnp.zeros((rows, 128 - 2 * DN_HEADS), f32)], axis=1)
    o_ref[:, qkvz + 2 * SGU_WIDTH:IN_COLS_ALIGNED] = ba.astype(bf16)


def _stage_weight_layout(w_in):
    rows = 256
    return pl.pallas_call(
        _weight_layout_body,
        grid=(D_MODEL // rows,),
        in_specs=[pl.BlockSpec((rows, w_in.shape[1]), lambda i: (i, 0))],
        out_specs=pl.BlockSpec((rows, IN_COLS_ALIGNED), lambda i: (i, 0)),
        out_shape=jax.ShapeDtypeStruct((D_MODEL, IN_COLS_ALIGNED), bf16),
        compiler_params=_cparams(("arbitrary",)),
        name="weight_layout",
    )(w_in)


def _inproj_body(x_ref, w_ref, convw_ref, prow_ref, lng_ref, lnb_ref, ones_ref,
                 q_ref, k_ref, v_ref, z_ref, u_ref, vln_ref, gcol_ref, grow_ref, *ext_refs, tm, ngroups):
    W = DN_WIDTH
    gm = tm // ngroups
    ext = [ext_refs[3 * g:3 * g + 3] for g in range(ngroups)]

    @pl.when(pl.program_id(1) == 0)
    def _():
        for e_ref in ext[0]:
            e_ref[0:8, :] = jnp.zeros((8, W), f32)

    for g in range(ngroups):
        _inproj_rows(x_ref, w_ref, convw_ref, prow_ref, lng_ref, lnb_ref, ones_ref,
                     q_ref, k_ref, v_ref, z_ref, u_ref, vln_ref, gcol_ref, grow_ref, ext[g],
                     ext[(g + 1) % ngroups], r0=g * gm, gm=gm)


def _inproj_rows(x_ref, w_ref, convw_ref, prow_ref, lng_ref, lnb_ref, ones_ref,
                 q_ref, k_ref, v_ref, z_ref, u_ref, vln_ref, gcol_ref, grow_ref, ext, ext_next, *, r0, gm):
    W = DN_WIDTH
    rows = slice(r0, r0 + gm)
    xb = x_ref[0, rows, :].astype(bf16)
    for part, e_ref in enumerate(ext):
        e_ref[8:8 + gm, :] = jnp.dot(xb, w_ref[:, part * W:(part + 1) * W], preferred_element_type=f32)
    zc = 3 * W
    uc = zc + W
    vc = uc + SGU_WIDTH
    bc = vc + SGU_WIDTH

    def conv_silu(part):
        e_ref = ext[part]
        cs = slice(part * W, (part + 1) * W)
        y = convw_ref[3:4, cs] * e_ref[8:8 + gm, :]
        for j in range(CONV_K - 1):
            y = y + convw_ref[j:j + 1, cs] * e_ref[5 + j:5 + j + gm, :]
        ext_next[part][0:8, :] = e_ref[gm:gm + 8, :]
        return _silu(y)

    def group_sums(a):
        return jnp.dot(a.astype(bf16), ones_ref[...], preferred_element_type=f32)

    yq = conv_silu(0)
    ssq = group_sums(yq * yq)
    pba = jnp.dot(xb, w_ref[:, bc:bc + 128], preferred_element_type=f32)
    pv = jnp.dot(xb, w_ref[:, vc:vc + SGU_WIDTH], preferred_element_type=f32)
    yk = conv_silu(1)
    ssk = group_sums(yk * yk)
    pu = jnp.dot(xb, w_ref[:, uc:uc + SGU_WIDTH], preferred_element_type=f32)
    pz = jnp.dot(xb, w_ref[:, zc:zc + W], preferred_element_type=f32)
    q_ref[0, rows, :] = (yq * (lax.rsqrt(ssq + RMS_EPS) * HEAD_DIM ** -0.5)).astype(bf16)
    k_ref[0, rows, :] = (yk * lax.rsqrt(ssk + RMS_EPS)).astype(bf16)
    v_ref[0, rows, :] = conv_silu(2).astype(bf16)

    z_ref[0, rows, :] = _silu(pz).astype(bf16)

    u_ref[0, rows, :] = _gelu_tanh(pu).astype(bf16)
    pv = _gelu_tanh(pv)
    for g in range(SGU_GROUPS):
        sl = slice(g * SGU_CHUNK, (g + 1) * SGU_CHUNK)
        vg = pv[:, sl]
        mu = jnp.mean(vg, axis=-1, keepdims=True)
        vcn = vg - mu
        var = jnp.mean(vcn * vcn, axis=-1, keepdims=True)
        vln_ref[0, rows, sl] = (vcn * lax.rsqrt(var + LN_EPS) * lng_ref[:, sl] + lnb_ref[:, sl]).astype(bf16)

    lane = _iota2((DN_CHUNK, 128), 1)
    beta = _sigmoid(pba)
    lane8 = _iota2((8, DN_CHUNK), 1)
    sub8 = _iota2((8, DN_CHUNK), 0)
    for c in range(gm // DN_CHUNK):
        rs = slice(c * DN_CHUNK, (c + 1) * DN_CHUNK)
        os_ = slice(r0 + c * DN_CHUNK, r0 + (c + 1) * DN_CHUNK)
        pbat = pba[rs].T[0:8, :]
        gt = -jnp.exp(prow_ref[0]) * _softplus(pbat + prow_ref[1])
        gc = jnp.where(sub8 >= DN_HEADS, gt, 0.0)
        shift = 1
        while shift < DN_CHUNK:
            gc = gc + jnp.where(lane8 >= shift, pltpu.roll(gc, shift, axis=1), 0.0)
            shift *= 2
        grow_ref[0, :, os_] = gc
        gc_col = jnp.concatenate([gc, jnp.zeros((DN_CHUNK - 8, DN_CHUNK), f32)], axis=0).T
        gcol_ref[0, os_, :] = jnp.where(lane < DN_HEADS, beta[rs], gc_col)


def _stage_inproj(x, w_re, conv_w, prow, lng, lnb, *, tm, ngroups):
    B, T, _ = x.shape
    wcols = w_re.shape[1]
    grid = (B, T // tm)
    gi = lax.broadcasted_iota(i32, (DN_WIDTH, DN_WIDTH), 0) // 128
    gj = lax.broadcasted_iota(i32, (DN_WIDTH, DN_WIDTH), 1) // 128
    group_ones = (gi == gj).astype(bf16)
    act = lambda: jax.ShapeDtypeStruct((B, T, DN_WIDTH), bf16)
    act_spec = lambda: pl.BlockSpec((1, tm, DN_WIDTH), lambda b, t: (b, t, 0))
    const2 = lambda shp: pl.BlockSpec(shp, lambda b, t: (0, 0))
    return pl.pallas_call(
        functools.partial(_inproj_body, tm=tm, ngroups=ngroups),
        grid=grid,
        in_specs=[
            pl.BlockSpec((1, tm, D_MODEL), lambda b, t: (b, t, 0)),
            const2((D_MODEL, wcols)),
            const2((CONV_K, 3 * DN_WIDTH)),
            pl.BlockSpec((2, 8, 128), lambda b, t: (0, 0, 0)),
            const2((1, SGU_WIDTH)),
            const2((1, SGU_WIDTH)),
            const2((DN_WIDTH, DN_WIDTH)),
        ],
        out_specs=[act_spec() for _ in range(6)] + [
            pl.BlockSpec((1, tm, 128), lambda b, t: (b, t, 0)),
            pl.BlockSpec((1, 8, tm), lambda b, t: (b, 0, t)),
        ],
        out_shape=[act() for _ in range(6)] + [
            jax.ShapeDtypeStruct((B, T, 128), f32),
            jax.ShapeDtypeStruct((B, 8, T), f32),
        ],
        scratch_shapes=[pltpu.VMEM((tm // ngroups + 8, DN_WIDTH), f32) for _ in range(3 * ngroups)],
        compiler_params=_cparams(("arbitrary", "arbitrary")),
        name="inproj",
    )(x, w_re, conv_w, prow, lng, lnb, group_ones)


def _mm(a, b):
    return jnp.dot(a.astype(bf16), b.astype(bf16), preferred_element_type=f32)


def _unit_lower_inverse(nmats, ii, jj):
    n = nmats[0].shape[0]
    eye = (ii == jj).astype(f32)
    leaf = jnp.right_shift(ii, 3) == jnp.right_shift(jj, 3)
    dblk = [jnp.where(leaf, m, 0.0) for m in nmats]
    s1 = [_mm(d, d) for d in dblk]
    r1 = [eye - d for d in dblk]
    both = [_mm(s, jnp.concatenate([s, r], axis=1)) for s, r in zip(s1, r1)]
    r2 = [r + bo[:, n:] for r, bo in zip(r1, both)]
    xs = [r + _mm(bo[:, :n], r) for r, bo in zip(r2, both)]
    size = 8
    while size < n:
        lows = [slice(r + size, r + 2 * size) for r in range(0, n, 2 * size)]
        ups = [slice(r, r + size) for r in range(0, n, 2 * size)]
        rsel = _iota2((n // 2, n), 0)
        ilow = rsel + size * (jnp.right_shift(rsel, size.bit_length() - 1) + 1)
        jlow = _iota2((n // 2, n), 1)
        in_pair_upper = (jlow >= ilow - (ilow & (2 * size - 1))) & (jlow < ilow - (ilow & (size - 1)))
        zeros = jnp.zeros((size, n), f32)
        new_xs = []
        ylows = [_mm(jnp.where(in_pair_upper, jnp.concatenate([m[s] for s in lows], axis=0), 0.0), x)
                 for m, x in zip(nmats, xs)]
        yfull = [jnp.concatenate([piece for k in range(len(lows)) for piece in (zeros, y[k * size:(k + 1) * size])], axis=0)
                 for y in ylows]
        corr = [_mm(jnp.concatenate([x[s] for s in lows], axis=0), yf) for x, yf in zip(xs, yfull)]
        for x, c in zip(xs, corr):
            pieces = []
            for k, (u, l) in enumerate(zip(ups, lows)):
                pieces += [x[u], x[l] - c[k * size:(k + 1) * size]]
            new_xs.append(jnp.concatenate(pieces, axis=0))
        xs = new_xs
        size *= 2
    return xs


def _deltanet_body(q_ref, k_ref, v_ref, z_ref, gcol_ref, grow_ref, nw_ref, y_ref, s_ref, *, nbr, nch, ngroups):
    @pl.when(pl.program_id(1) == 0)
    def _():
        s_ref[...] = jnp.zeros(s_ref.shape, f32)

    per = nbr // ngroups
    for g in range(ngroups):
        _deltanet_rows(q_ref, k_ref, v_ref, z_ref, gcol_ref, grow_ref, nw_ref, y_ref, s_ref,
                       rows=range(g * per, (g + 1) * per), nch=nch)


def _deltanet_rows(q_ref, k_ref, v_ref, z_ref, gcol_ref, grow_ref, nw_ref, y_ref, s_ref, *, rows, nch):
    C = DN_CHUNK
    S = [(b, h) for b in rows for h in range(DN_HEADS)]
    P = [(b, c, h) for c in range(nch) for b, h in S]

    ii = _iota2((C, C), 0)
    jj = _iota2((C, C), 1)
    causal = ii >= jj
    rs = [slice(c * C, (c + 1) * C) for c in range(nch)]
    hs = [slice(h * HEAD_DIM, (h + 1) * HEAD_DIM) for h in range(DN_HEADS)]
    gcol = {(b, c): gcol_ref[b, rs[c], :] for b in rows for c in range(nch)}
    qh = {(b, c, h): q_ref[b, rs[c], hs[h]] for b, c, h in P}
    kh = {(b, c, h): k_ref[b, rs[c], hs[h]] for b, c, h in P}
    vh = {(b, c, h): v_ref[b, rs[c], hs[h]] for b, c, h in P}
    gc_b = {(b, c, h): jnp.broadcast_to(gcol[b, c][:, DN_HEADS + h:DN_HEADS + h + 1], (C, HEAD_DIM)) for b, c, h in P}
    beta_b = {(b, c, h): jnp.broadcast_to(gcol[b, c][:, h:h + 1], (C, HEAD_DIM)) for b, c, h in P}
    gc_r = {(b, c, h): jnp.broadcast_to(grow_ref[b, DN_HEADS + h:DN_HEADS + h + 1, rs[c]], (C, C)) for b, c, h in P}
    decay = {p: jnp.exp(jnp.where(causal, gc_b[p] - gc_r[p], -1e30)) for p in P}

    kf = {p: kh[p].astype(f32) for p in P}
    kb = {p: kf[p] * beta_b[p] for p in P}
    kk = {p: lax.dot_general(kb[p].astype(bf16), kh[p], NT_DIMS, preferred_element_type=f32) for p in P}
    a_intra = {p: lax.dot_general(qh[p], kh[p], NT_DIMS, preferred_element_type=f32) * decay[p] for p in P}
    nmat = [jnp.where(ii > jj, kk[p] * decay[p], 0.0) for p in P]
    tinv = dict(zip(P, _unit_lower_inverse(nmat, ii, jj)))

    eg = {p: jnp.exp(gc_b[p]) for p in P}
    rhs = {p: jnp.concatenate([vh[p].astype(f32) * beta_b[p], kb[p] * eg[p]], axis=1) for p in P}
    sol = {p: _mm(tinv[p], rhs[p]) for p in P}

    q_dec = {p: qh[p].astype(f32) * eg[p] for p in P}
    g_last = {p: gc_b[p][C - 1:C, :] for p in P}
    kdt = {p: (kf[p] * jnp.exp(g_last[p] - gc_b[p])).T for p in P}

    state = {(b, h): s_ref[b * DN_HEADS + h] for b, h in S}
    for c in range(nch):
        m1 = {(b, h): _mm(jnp.concatenate([sol[b, c, h][:, HEAD_DIM:], q_dec[b, c, h]], axis=0), state[b, h]) for b, h in S}
        v_new = {(b, h): sol[b, c, h][:, :HEAD_DIM] - m1[b, h][:C] for b, h in S}
        m2 = {(b, h): _mm(jnp.concatenate([a_intra[b, c, h], kdt[b, c, h]], axis=0), v_new[b, h]) for b, h in S}
        state = {(b, h): state[b, h] * jnp.exp(g_last[b, c, h]) + m2[b, h][C:] for b, h in S}
        for b, h in S:
            o = m1[b, h][C:] + m2[b, h][:C]
            rms = lax.rsqrt(jnp.mean(o * o, axis=-1, keepdims=True) + RMS_EPS)
            y_ref[b, rs[c], hs[h]] = (o * rms * nw_ref[...] * z_ref[b, rs[c], hs[h]].astype(f32)).astype(bf16)
    for b, h in S:
        s_ref[b * DN_HEADS + h] = state[b, h]


def _stage_deltanet(q, k, v, z, gcol, grow, norm_w, *, nbr, nch, ngroups):
    B, T, _ = q.shape
    tt = nch * DN_CHUNK
    act_spec = lambda: pl.BlockSpec((nbr, tt, DN_WIDTH), lambda b, t: (b, t, 0))
    return pl.pallas_call(
        functools.partial(_deltanet_body, nbr=nbr, nch=nch, ngroups=ngroups),
        grid=(B // nbr, T // tt),
        in_specs=[act_spec(), act_spec(), act_spec(), act_spec(),
                  pl.BlockSpec((nbr, tt, 128), lambda b, t: (b, t, 0)),
                  pl.BlockSpec((nbr, 8, tt), lambda b, t: (b, 0, t)),
                  pl.BlockSpec((1, HEAD_DIM), lambda b, t: (0, 0))],
        out_specs=act_spec(),
        out_shape=jax.ShapeDtypeStruct((B, T, DN_WIDTH), bf16),
        scratch_shapes=[pltpu.VMEM((nbr * DN_HEADS, HEAD_DIM, HEAD_DIM), f32)],
        compiler_params=_cparams(("arbitrary", "arbitrary")),
        name="deltanet",
    )(q, k, v, z, gcol, grow, norm_w)


def _mixout_body(ydn_ref, u_ref, vln_ref, x_hbm, ws_ref, bsp_ref, wout_ref, g1_ref, b1_ref, wrt_ref, brt_ref,
                 h_ref, hrow_ref, ids_ref, wts_ref, cnt_ref, ycat_ref, xring_ref, xsem, *, tm, nsteps):
    C = SGU_CHUNK
    step = pl.program_id(0) * pl.num_programs(1) + pl.program_id(1)
    slot = lax.rem(step, X_SLOTS)

    def x_copy(s, sl):
        return pltpu.make_async_copy(x_hbm.at[s], xring_ref.at[sl], xsem.at[sl])

    @pl.when(step == 0)
    def _():
        cnt_ref[...] = jnp.zeros(cnt_ref.shape, f32)
        for j in range(X_SLOTS - 1):
            x_copy(j, j).start()

    ahead = step + (X_SLOTS - 1)

    @pl.when(ahead < nsteps)
    def _():
        x_copy(ahead, lax.rem(ahead, X_SLOTS)).start()

    x_copy(step, slot).wait()
    ii = _iota2((C, C), 0)
    jj = _iota2((C, C), 1)
    ycat_ref[:, 0:DN_WIDTH] = ydn_ref[0]
    for g in range(SGU_GROUPS):
        gs = slice(g * C, (g + 1) * C)
        wsg = jnp.where(ii >= jj, ws_ref[g], 0.0).astype(bf16)
        for c in range(tm // C):
            rs = slice(c * C, (c + 1) * C)
            mixed = jnp.dot(wsg, vln_ref[0, rs, gs], preferred_element_type=f32) + bsp_ref[:, gs]
            ycat_ref[rs, DN_WIDTH + g * C:DN_WIDTH + (g + 1) * C] = (u_ref[0, rs, gs].astype(f32) * mixed).astype(bf16)

    RB = 128
    blocks = [slice(r, r + RB) for r in range(0, tm, RB)]
    mix = [jnp.dot(ycat_ref[rb, :], wout_ref[...], preferred_element_type=f32) for rb in blocks]
    h1s = []
    for rb, m in zip(blocks, mix):
        hp = DEEPNORM_ALPHA * xring_ref[slot, rb, :] + m
        mu = jnp.mean(hp, axis=-1, keepdims=True)
        hc = hp - mu
        var = jnp.mean(hc * hc, axis=-1, keepdims=True)
        h1 = hc * lax.rsqrt(var + LN_EPS) * g1_ref[...] + b1_ref[...]
        h_ref[0, rb, :] = h1
        h1b = h1.astype(bf16)
        hrow_ref[rb] = h1b.reshape(RB, ROW_TILE, 128)
        h1s.append(h1b)

    logit_blocks = [lax.dot_general(wrt_ref[...], hb, NT_DIMS, preferred_element_type=f32) + brt_ref[...] for hb in h1s]
    sub = _iota2((8, RB), 0)
    subf = sub.astype(f32)
    sub_e = _iota2((N_EXPERTS, RB), 0).astype(f32)
    chosen = []
    for rb, logits in zip(blocks, logit_blocks):
        gl = logits[0:8]
        gmax = jnp.max(gl, axis=0, keepdims=True)
        g_idx = jnp.min(jnp.where(gl == gmax, subf, float(MOE_GROUPS)), axis=0, keepdims=True)
        p_group = 1.0 / jnp.sum(jnp.exp(gl - gmax), axis=0, keepdims=True)
        within = jnp.zeros((8, RB), f32)
        for g in range(MOE_GROUPS):
            within = within + jnp.where(g_idx == float(g), logits[8 + 8 * g:16 + 8 * g], 0.0)
        m1 = jnp.max(within, axis=0, keepdims=True)
        i1 = jnp.min(jnp.where(within == m1, subf, float(EXPERTS_PER_GROUP)), axis=0, keepdims=True)
        rest = jnp.where(subf == i1, -jnp.inf, within)
        m2 = jnp.max(rest, axis=0, keepdims=True)
        i2 = jnp.min(jnp.where(rest == m2, subf, float(EXPERTS_PER_GROUP)), axis=0, keepdims=True)
        e = jnp.exp(m2 - m1)
        w1 = p_group / (1.0 + e)
        w2 = p_group * e / (1.0 + e)
        e1 = g_idx * float(EXPERTS_PER_GROUP) + i1
        e2 = g_idx * float(EXPERTS_PER_GROUP) + i2
        ids_ref[:, rb] = jnp.where(sub == 0, e1, jnp.where(sub == 1, e2, 0.0)).astype(i32)
        wts_ref[:, rb] = jnp.where(sub == 0, w1, jnp.where(sub == 1, w2, 0.0))
        chosen.append(((sub_e == e1).astype(f32) + (sub_e == e2).astype(f32)).astype(bf16))
    ones = jnp.ones((RB, 128), bf16)
    cnt_ref[...] = cnt_ref[...] + sum(jnp.dot(oh, ones, preferred_element_type=f32) for oh in chosen)


def _stage_mixout(ydn, u, vln, x, ws, bsp, wout, g1, b1, wrt, brt, *, tm):
    B, T, _ = x.shape
    nt = T // tm
    act_spec = lambda: pl.BlockSpec((1, tm, DN_WIDTH), lambda b, t: (b, t, 0))
    const2 = lambda shp: pl.BlockSpec(shp, lambda b, t: (0, 0))
    tok_spec = lambda: pl.BlockSpec((8, tm), lambda b, t: (0, b * nt + t))
    return pl.pallas_call(
        functools.partial(_mixout_body, tm=tm, nsteps=B * nt),
        grid=(B, nt),
        in_specs=[act_spec(), act_spec(), act_spec(),
                  pl.BlockSpec(memory_space=pl.ANY),
                  pl.BlockSpec((SGU_GROUPS, SGU_CHUNK, SGU_CHUNK), lambda b, t: (0, 0, 0)),
                  const2((SGU_CHUNK, SGU_WIDTH)),
                  const2((D_MODEL, D_MODEL)),
                  const2((1, D_MODEL)), const2((1, D_MODEL)),
                  const2((128, D_MODEL)), const2((128, 128))],
        out_specs=[pl.BlockSpec((1, tm, D_MODEL), lambda b, t: (b, t, 0)),
                   pl.BlockSpec((tm, ROW_TILE, 128), lambda b, t: (b * nt + t, 0, 0)), tok_spec(), tok_spec(),
                   const2((N_EXPERTS, 128))],
        out_shape=[jax.ShapeDtypeStruct((B, T, D_MODEL), f32),
                   jax.ShapeDtypeStruct((B * T, ROW_TILE, 128), bf16),
                   jax.ShapeDtypeStruct((8, B * T), i32),
                   jax.ShapeDtypeStruct((8, B * T), f32),
                   jax.ShapeDtypeStruct((N_EXPERTS, 128), f32)],
        scratch_shapes=[pltpu.VMEM((tm, D_MODEL), bf16), pltpu.VMEM((X_SLOTS, tm, D_MODEL), f32),
                        pltpu.SemaphoreType.DMA((X_SLOTS,))],
        compiler_params=_cparams(("arbitrary", "arbitrary")),
        name="mixout",
    )(ydn, u, vln, x.reshape(B * nt, tm, D_MODEL), ws, bsp, wout, g1, b1, wrt, brt)


def _route_body(cnt_ref, ids_ref, dest_ref, meta_ref, blk_ref, pstart_ref, *, tm, nb_pad):
    i = pl.program_id(0)
    sub = _iota2((N_EXPERTS, tm), 0)
    is1 = sub == ids_ref[0:1, :]
    is2 = sub == ids_ref[1:2, :]
    oh = (is1.astype(f32) + is2.astype(f32)).astype(bf16)

    @pl.when(i == 0)
    def _():
        cnt = cnt_ref[...]
        padded = jnp.floor((cnt + (MOE_BLOCK - 1)) * (1.0 / MOE_BLOCK)) * MOE_BLOCK
        ei = _iota2((N_EXPERTS, N_EXPERTS), 0)
        ej = _iota2((N_EXPERTS, N_EXPERTS), 1)
        pends = jnp.dot((ei >= ej).astype(f32), padded, precision=HIGHEST, preferred_element_type=f32)
        pstart = pends - padded
        pstart_ref[...] = pstart
        s64 = _iota2((N_EXPERTS, 128), 0)
        l64 = _iota2((N_EXPERTS, 128), 1)
        diag = s64 == l64
        fill_off = jnp.sum(jnp.where(diag, pstart + cnt, 0.0), axis=0, keepdims=True)
        fill_n = jnp.sum(jnp.where(diag, padded - cnt, 0.0), axis=0, keepdims=True)
        nused = pends[N_EXPERTS - 1:N_EXPERTS, :] * (1.0 / MOE_BLOCK)
        m8 = _iota2((8, 128), 0)
        meta_ref[...] = jnp.where(m8 == 0, fill_off, jnp.where(m8 == 1, fill_n, jnp.where(m8 == 2, nused, 0.0))).astype(i32)
        bstart = (_iota2((N_EXPERTS, nb_pad), 1) * MOE_BLOCK).astype(f32)
        pe = jnp.concatenate([pends] * (nb_pad // 128), axis=1)
        be = jnp.sum((pe <= bstart).astype(f32), axis=0, keepdims=True)
        be = jnp.minimum(be, float(N_EXPERTS - 1))
        blk_ref[...] = jnp.broadcast_to(be, (8, nb_pad)).astype(i32)

    ti = _iota2((tm, tm), 0)
    tj = _iota2((tm, tm), 1)
    before = (ti < tj).astype(bf16)
    prefix = jnp.dot(oh, before, preferred_element_type=f32)
    nxt = prefix + jnp.concatenate([pstart_ref[...]] * (tm // 128), axis=1)
    d1 = jnp.sum(jnp.where(is1, nxt, 0.0), axis=0, keepdims=True)
    d2 = jnp.sum(jnp.where(is2, nxt, 0.0), axis=0, keepdims=True)
    sub8 = _iota2((8, tm), 0)
    dest_ref[...] = jnp.where(sub8 == 0, d1, jnp.where(sub8 == 1, d2, 0.0)).astype(i32)
    pstart_ref[...] = pstart_ref[...] + jnp.dot(oh, jnp.ones((tm, 128), bf16), preferred_element_type=f32)


def _stage_route(cnt, ids, *, tm, nb_pad):
    n = ids.shape[1]
    return pl.pallas_call(
        functools.partial(_route_body, tm=tm, nb_pad=nb_pad),
        grid=(n // tm,),
        in_specs=[pl.BlockSpec((N_EXPERTS, 128), lambda i: (0, 0)), pl.BlockSpec((8, tm), lambda i: (0, i))],
        out_specs=[pl.BlockSpec((8, tm), lambda i: (0, i)),
                   pl.BlockSpec((8, 128), lambda i: (0, 0)),
                   pl.BlockSpec((8, nb_pad), lambda i: (0, 0))],
        out_shape=[jax.ShapeDtypeStruct((8, n), i32), jax.ShapeDtypeStruct((8, 128), i32),
                   jax.ShapeDtypeStruct((8, nb_pad), i32)],
        scratch_shapes=[pltpu.VMEM((N_EXPERTS, 128), f32)],
        compiler_params=_cparams(("arbitrary",)),
        name="moe_route",
    )(cnt, ids)


def _dispatch_body(fill_off_ref, fill_n_ref, nused_ref, dest_ref, h_hbm, xs_ref, src_ref, zero_ref, isem, sem, zsem,
                   *, tm, nsteps):
    i = pl.program_id(0)
    slot = lax.rem(i, X_SLOTS)

    def in_copy(step, s):
        return pltpu.make_async_copy(h_hbm.at[pl.ds(step * tm, tm)], src_ref.at[s], isem.at[s])

    def drain(s):
        for _ in range(2):
            pltpu.make_async_copy(src_ref.at[s], xs_ref.at[pl.ds(0, tm)], sem.at[s]).wait()

    @pl.when(i == 0)
    def _():
        for j in range(min(X_SLOTS - 1, nsteps)):
            in_copy(j, j).start()

    in_copy(i, slot).wait()

    def row_copy(t, d):
        return pltpu.make_async_copy(src_ref.at[slot, t], xs_ref.at[d], sem.at[slot])

    def issue(t, carry):
        row_copy(t, dest_ref[0, 0, t]).start(priority=0)
        row_copy(t, dest_ref[0, 1, t]).start(priority=1)
        return carry

    lax.fori_loop(0, tm, issue, 0, unroll=8)

    @pl.when(i == 0)
    def _():
        zero_ref[...] = jnp.zeros(zero_ref.shape, bf16)

        def fill(start):
            def body(e, carry):
                off = fill_off_ref[e]
                npad = fill_n_ref[e]
                bit = MOE_BLOCK // 2
                while bit:
                    @pl.when((npad & bit) != 0)
                    def _(off=off, bit=bit):
                        cp = pltpu.make_async_copy(zero_ref.at[pl.ds(0, bit)], xs_ref.at[pl.ds(off, bit)], zsem)
                        cp.start() if start else cp.wait()
                    off = off + (npad & bit)
                    bit //= 2
                return carry
            return body

        lax.fori_loop(0, N_EXPERTS, fill(True), 0)
        lax.fori_loop(0, N_EXPERTS, fill(False), 0)

        def tail_copy(b):
            return pltpu.make_async_copy(zero_ref, xs_ref.at[pl.ds(b * MOE_BLOCK, MOE_BLOCK)], zsem)

        nblocks = xs_ref.shape[0] // MOE_BLOCK
        lax.fori_loop(nused_ref[0], nblocks, lambda b, c: (tail_copy(b).start(), c)[1], 0)
        lax.fori_loop(nused_ref[0], nblocks, lambda b, c: (tail_copy(0).wait(), c)[1], 0)

    @pl.when(i >= 1)
    def _():
        drain(lax.rem(i - 1, X_SLOTS))

    ahead = i + (X_SLOTS - 1)

    @pl.when(ahead < nsteps)
    def _():
        in_copy(ahead, lax.rem(ahead, X_SLOTS)).start()

    @pl.when(i == nsteps - 1)
    def _():
        drain(slot)


def _stage_dispatch(fill_off, fill_n, nused, dest3, hrow, p_rows, *, tm):
    n = hrow.shape[0]
    nsteps = n // tm
    return pl.pallas_call(
        functools.partial(_dispatch_body, tm=tm, nsteps=nsteps),
        grid_spec=pltpu.PrefetchScalarGridSpec(
            num_scalar_prefetch=3,
            grid=(nsteps,),
            in_specs=[pl.BlockSpec((1, 2, tm), lambda i, fo, fn, nu: (i, 0, 0), memory_space=pltpu.SMEM),
                      pl.BlockSpec(memory_space=pl.ANY)],
            out_specs=pl.BlockSpec(memory_space=pl.ANY),
            scratch_shapes=[pltpu.VMEM((X_SLOTS, tm, ROW_TILE, 128), bf16),
                            pltpu.VMEM((MOE_BLOCK, ROW_TILE, 128), bf16),
                            pltpu.SemaphoreType.DMA((X_SLOTS,)), pltpu.SemaphoreType.DMA((X_SLOTS,)),
                            pltpu.SemaphoreType.DMA],
        ),
        out_shape=jax.ShapeDtypeStruct((p_rows, ROW_TILE, 128), bf16),
        compiler_params=_cparams(("arbitrary",)),
        name="moe_dispatch",
    )(fill_off, fill_n, nused, dest3, hrow)


def _experts_body(blk_ref, nused_ref, xs_hbm, wg_hbm, wu_hbm, wd_hbm, ys_hbm,
                  xbuf_ref, wg32_ref, wu32_ref, wd32_ref, wgu16_ref, wd16_ref, obuf_ref, zero_ref, xsem, wsem, osem, zsem,
                  *, nblocks):
    nused = nused_ref[0]

    def weight_copies(ex):
        return (pltpu.make_async_copy(wg_hbm.at[ex], wg32_ref, wsem.at[0]),
                pltpu.make_async_copy(wu_hbm.at[ex], wu32_ref, wsem.at[1]),
                pltpu.make_async_copy(wd_hbm.at[ex], wd32_ref, wsem.at[2]))

    def x_copy(block, s):
        return pltpu.make_async_copy(xs_hbm.at[pl.ds(block * MOE_BLOCK, MOE_BLOCK)], xbuf_ref.at[s], xsem.at[s])

    def out_copy(block, s):
        return pltpu.make_async_copy(obuf_ref.at[s], ys_hbm.at[pl.ds(block * MOE_BLOCK, MOE_BLOCK)], osem.at[s])

    def tail_copy(block):
        return pltpu.make_async_copy(zero_ref, ys_hbm.at[pl.ds(block * MOE_BLOCK, MOE_BLOCK)], zsem)

    zero_ref[...] = jnp.zeros(zero_ref.shape, bf16)
    lax.fori_loop(nused, nblocks, lambda b, c: (tail_copy(b).start(), c)[1], 0)

    @pl.when(nused > 0)
    def _():
        for cp in weight_copies(blk_ref[0]):
            cp.start()
        for j in range(X_SLOTS - 1):
            @pl.when(j < nused)
            def _(j=j):
                x_copy(j, j).start()

    def step(i, carry):
        e = blk_ref[i]
        slot = lax.rem(i, X_SLOTS)
        ahead = i + (X_SLOTS - 1)

        @pl.when(ahead < nused)
        def _():
            x_copy(ahead, lax.rem(ahead, X_SLOTS)).start()

        @pl.when((i == 0) | (e != blk_ref[jnp.maximum(i - 1, 0)]))
        def _():
            for cp in weight_copies(e):
                cp.wait()
            wgu16_ref[:, 0:D_EXPERT] = wg32_ref[...].astype(bf16)
            wgu16_ref[:, D_EXPERT:2 * D_EXPERT] = wu32_ref[...].astype(bf16)
            wd16_ref[...] = wd32_ref[...].astype(bf16)
            nxt = lax.while_loop(lambda j: (j < nused) & (blk_ref[jnp.minimum(j, nused - 1)] == e), lambda j: j + 1,
                                 i + 1)

            @pl.when(nxt < nused)
            def _():
                for cp in weight_copies(blk_ref[jnp.minimum(nxt, nused - 1)]):
                    cp.start(priority=1)

        @pl.when(i >= X_SLOTS)
        def _():
            out_copy(i - X_SLOTS, slot).wait()

        x_copy(i, slot).wait()
        half = MOE_BLOCK // 2
        rows = [slice(p * half, (p + 1) * half) for p in range(2)]
        gu = [jnp.dot(xbuf_ref[slot, r].reshape(half, D_MODEL), wgu16_ref[...], preferred_element_type=f32)
              for r in rows]
        hid = [(_silu(g[:, :D_EXPERT]) * g[:, D_EXPERT:]).astype(bf16) for g in gu]
        y = [jnp.dot(hd, wd16_ref[...], preferred_element_type=f32) for hd in hid]
        for r, yp in zip(rows, y):
            obuf_ref[slot, r] = yp.astype(bf16).reshape(half, ROW_TILE, 128)
        out_copy(i, slot).start()
        return carry

    lax.fori_loop(0, nused, step, 0)

    for j in range(X_SLOTS):
        last = nused - 1 - j

        @pl.when(last >= 0)
        def _(last=last):
            out_copy(last, lax.rem(last, X_SLOTS)).wait()

    lax.fori_loop(nused, nblocks, lambda b, c: (tail_copy(b).wait(), c)[1], 0)


def _stage_experts(blk_e, nused, xs, w_gate, w_up, w_down):
    p_rows = xs.shape[0]
    nb = p_rows // MOE_BLOCK

    return pl.pallas_call(
        functools.partial(_experts_body, nblocks=nb),
        grid_spec=pltpu.PrefetchScalarGridSpec(
            num_scalar_prefetch=2,
            grid=(1,),
            in_specs=[pl.BlockSpec(memory_space=pl.ANY),
                      pl.BlockSpec(memory_space=pl.ANY),
                      pl.BlockSpec(memory_space=pl.ANY),
                      pl.BlockSpec(memory_space=pl.ANY)],
            out_specs=pl.BlockSpec(memory_space=pl.ANY),
            scratch_shapes=[pltpu.VMEM((X_SLOTS, MOE_BLOCK, ROW_TILE, 128), bf16),
                            pltpu.VMEM((D_MODEL, D_EXPERT), f32), pltpu.VMEM((D_MODEL, D_EXPERT), f32),
                            pltpu.VMEM((D_EXPERT, D_MODEL), f32),
                            pltpu.VMEM((D_MODEL, 2 * D_EXPERT), bf16), pltpu.VMEM((D_EXPERT, D_MODEL), bf16),
                            pltpu.VMEM((X_SLOTS, MOE_BLOCK, ROW_TILE, 128), bf16),
                            pltpu.VMEM((MOE_BLOCK, ROW_TILE, 128), bf16),
                            pltpu.SemaphoreType.DMA((X_SLOTS,)), pltpu.SemaphoreType.DMA((3,)),
                            pltpu.SemaphoreType.DMA((X_SLOTS,)), pltpu.SemaphoreType.DMA],
        ),
        out_shape=jax.ShapeDtypeStruct((p_rows, ROW_TILE, 128), bf16),
        compiler_params=_cparams(("arbitrary",)),
        name="moe_experts",
    )(blk_e, nused, xs, w_gate, w_up, w_down)


def _combine_body(dcur_ref, dnext_ref, h_ref, wts_ref, g2_ref, b2_ref, ys_ref, o_ref, ybuf_ref, sem, *, tm, nsteps):
    i = pl.program_id(0)
    slot = lax.rem(i, 2)

    def issue_tile(d_ref, s):
        def body(t, carry):
            for k in range(2):
                pltpu.make_async_copy(ys_ref.at[d_ref[0, k, t]], ybuf_ref.at[s, k, t], sem.at[s]).start(priority=k)
            return carry

        lax.fori_loop(0, tm, body, 0, unroll=8)

    @pl.when(i == 0)
    def _():
        issue_tile(dcur_ref, 0)

    @pl.when(i + 1 < nsteps)
    def _():
        issue_tile(dnext_ref, 1 - slot)

    for k in range(2):
        pltpu.make_async_copy(ys_ref.at[pl.ds(0, tm)], ybuf_ref.at[slot, k], sem.at[slot]).wait()

    pieces = []
    for c in range(tm // 128):
        ls = slice(c * 128, (c + 1) * 128)
        w1c = jnp.broadcast_to(wts_ref[0:1, ls], (128, 128)).T
        w2c = jnp.broadcast_to(wts_ref[1:2, ls], (128, 128)).T
        w1f = jnp.concatenate([w1c] * (D_MODEL // 128), axis=1)
        w2f = jnp.concatenate([w2c] * (D_MODEL // 128), axis=1)
        y1 = ybuf_ref[slot, 0, ls].reshape(128, D_MODEL).astype(f32)
        y2 = ybuf_ref[slot, 1, ls].reshape(128, D_MODEL).astype(f32)
        pieces.append(w1f * y1 + w2f * y2)
    ffn = jnp.concatenate(pieces, axis=0)
    hp = DEEPNORM_ALPHA * h_ref[...] + ffn
    mu = jnp.mean(hp, axis=-1, keepdims=True)
    hc = hp - mu
    var = jnp.mean(hc * hc, axis=-1, keepdims=True)
    o_ref[...] = hc * lax.rsqrt(var + LN_EPS) * g2_ref[...] + b2_ref[...]


def _stage_combine(dest3, h2, wts, g2, b2, ys, *, tm):
    n = h2.shape[0]
    nsteps = n // tm
    return pl.pallas_call(
        functools.partial(_combine_body, tm=tm, nsteps=nsteps),
        grid=(nsteps,),
        in_specs=[pl.BlockSpec((1, 2, tm), lambda i: (i, 0, 0), memory_space=pltpu.SMEM),
                  pl.BlockSpec((1, 2, tm), lambda i: (jnp.minimum(i + 1, nsteps - 1), 0, 0), memory_space=pltpu.SMEM),
                  pl.BlockSpec((tm, D_MODEL), lambda i: (i, 0)),
                  pl.BlockSpec((8, tm), lambda i: (0, i)),
                  pl.BlockSpec((1, D_MODEL), lambda i: (0, 0)),
                  pl.BlockSpec((1, D_MODEL), lambda i: (0, 0)),
                  pl.BlockSpec(memory_space=pl.ANY)],
        out_specs=pl.BlockSpec((tm, D_MODEL), lambda i: (i, 0)),
        out_shape=jax.ShapeDtypeStruct((n, D_MODEL), f32),
        scratch_shapes=[pltpu.VMEM((2, 2, tm, ROW_TILE, 128), bf16), pltpu.SemaphoreType.DMA((2,))],
        compiler_params=_cparams(("arbitrary",)),
        name="moe_combine",
    )(dest3, dest3, h2, wts, g2, b2, ys)


def _layer(h, w_in, conv_w, a_log, dt_bias, dn_norm_w, sgu_ln_g, sgu_ln_b, w_spatial, b_spatial, w_out,
           ln1_g, ln1_b, w_rg, b_rg, w_re, b_re, w_gate, w_up, w_down, ln2_g, ln2_b,
           *, tm_in, in_groups, dn_rows, dn_chunks, dn_groups, tm_mix, tm_rank, tm_disp, tm_comb):
    B, T, _ = h.shape
    n = B * T
    w_cols = _stage_weight_layout(w_in)
    decay_prm = jnp.stack([a_log, dt_bias])
    prow = jnp.broadcast_to(jnp.pad(decay_prm, ((0, 0), (DN_HEADS, 8 - 2 * DN_HEADS)))[:, :, None], (2, 8, 128))

    q, k, v, z, u, vln, gcol, grow = _stage_inproj(
        h, w_cols, conv_w, prow, sgu_ln_g[None, :], sgu_ln_b[None, :], tm=tm_in, ngroups=in_groups)
    ydn = _stage_deltanet(q, k, v, z, gcol, grow, dn_norm_w[None, :], nbr=dn_rows, nch=dn_chunks, ngroups=dn_groups)

    bsp = jnp.broadcast_to(b_spatial.T[:, :, None], (SGU_CHUNK, SGU_GROUPS, SGU_CHUNK)).reshape(SGU_CHUNK, SGU_WIDTH)
    n_logit = MOE_GROUPS + N_EXPERTS
    wrt = jnp.pad(jnp.concatenate([w_rg, w_re], axis=1).T, ((0, 128 - n_logit), (0, 0))).astype(bf16)
    brt = jnp.broadcast_to(jnp.pad(jnp.concatenate([b_rg, b_re]), (0, 128 - n_logit))[:, None], (128, 128))
    h1, hrow, ids, wts, cnt = _stage_mixout(ydn, u, vln, h, w_spatial, bsp, w_out.astype(bf16), ln1_g[None, :],
                                       ln1_b[None, :], wrt, brt, tm=tm_mix)

    p_rows = (-(-(n * 2) // MOE_BLOCK)) * MOE_BLOCK + N_EXPERTS * MOE_BLOCK
    nb = p_rows // MOE_BLOCK
    nb_pad = (-(-nb // 128)) * 128
    dest, meta, blk = _stage_route(cnt, ids, tm=tm_rank, nb_pad=nb_pad)

    h2 = h1.reshape(n, D_MODEL)
    dest_d = dest[0:2].reshape(2, n // tm_disp, tm_disp).transpose(1, 0, 2)
    xs = _stage_dispatch(meta[0, :N_EXPERTS], meta[1, :N_EXPERTS], meta[2, 0:1], dest_d, hrow, p_rows, tm=tm_disp)
    ys = _stage_experts(blk[0, :nb], meta[2, 0:1], xs, w_gate, w_up, w_down)
    dest_c = dest[0:2].reshape(2, n // tm_comb, tm_comb).transpose(1, 0, 2)
    out = _stage_combine(dest_c, h2, wts, ln2_g[None, :], ln2_b[None, :], ys, tm=tm_comb)
    return out.reshape(B, T, D_MODEL)


def kernel(x, w_in, conv_w, a_log, dt_bias, dn_norm_w, sgu_ln_g, sgu_ln_b, w_spatial, b_spatial, w_out, ln1_g, ln1_b, w_router_group, b_router_group, w_router_expert, b_router_expert, w_gate, w_up, w_down, ln2_g, ln2_b):
    h = x
    for l in range(w_in.shape[0]):
        h = _layer(h, w_in[l], conv_w[l], a_log[l], dt_bias[l], dn_norm_w[l], sgu_ln_g[l], sgu_ln_b[l],
                   w_spatial[l], b_spatial[l], w_out[l], ln1_g[l], ln1_b[l],
                   w_router_group[l], b_router_group[l], w_router_expert[l], b_router_expert[l],
                   w_gate[l], w_up[l], w_down[l], ln2_g[l], ln2_b[l],
                   tm_in=512, in_groups=1, dn_rows=8, dn_chunks=2, dn_groups=4, tm_mix=1024, tm_rank=1024, tm_disp=2048, tm_comb=512)
    return h
```

```python
import functools

import jax
import jax.numpy as jnp
from jax import lax
from jax.experimental import pallas as pl
from jax.experimental.pallas import tpu as pltpu

f32 = jnp.float32
bf16 = jnp.bfloat16
i32 = jnp.int32

D_MODEL = 1024
DN_WIDTH = 512
DN_HEADS = 4
HEAD_DIM = 128
CONV_K = 4
SGU_WIDTH = 512
SGU_GROUPS = 4
SGU_CHUNK = 128
DN_CHUNK = 128
MOE_GROUPS = 8
EXPERTS_PER_GROUP = 8
N_EXPERTS = 64
D_EXPERT = 512
MOE_BLOCK = 256
IN_COLS_ALIGNED = 4 * DN_WIDTH + 2 * SGU_WIDTH + 128
X_SLOTS = 4
ROW_TILE = D_MODEL // 128
DEEPNORM_ALPHA = 2.0 ** 0.25
LN_EPS = 1e-5
RMS_EPS = 1e-6
HIGHEST = lax.Precision.HIGHEST
VMEM_LIMIT_BYTES = 56 * 1024 * 1024

NT_DIMS = (((1,), (1,)), ((), ()))


def _cparams(sem, flags=None):
    return pltpu.CompilerParams(dimension_semantics=sem, vmem_limit_bytes=VMEM_LIMIT_BYTES, flags=flags)


def _sigmoid(x):
    return 1.0 / (1.0 + jnp.exp(-x))


def _silu(x):
    h = 0.5 * x
    return h + h * jnp.tanh(h)


def _softplus(x):
    return jnp.maximum(x, 0.0) + jnp.log1p(jnp.exp(-jnp.abs(x)))


def _gelu_tanh(x):
    c = 0.7978845608028654
    return x * (0.5 * (1.0 + jnp.tanh(c * (x + 0.044715 * (x * x * x)))))


def _iota2(shape, axis):
    return lax.broadcasted_iota(i32, shape, axis)


def _weight_layout_body(w_ref, o_ref):
    qkvz = 4 * DN_WIDTH
    uv0 = qkvz + 2 * DN_HEADS
    rows = w_ref.shape[0]
    o_ref[:, 0:qkvz] = w_ref[:, 0:qkvz].astype(bf16)
    o_ref[:, qkvz:qkvz + 2 * SGU_WIDTH] = w_ref[:, uv0:uv0 + 2 * SGU_WIDTH].astype(bf16)
    ba = jnp.concatenate([w_ref[:, qkvz:uv0], jnp.zeros((rows, 128 - 2 * DN_HEADS), f32)], axis=1)
    o_ref[:, qkvz + 2 * SGU_WIDTH:IN_COLS_ALIGNED] = ba.astype(bf16)


def _stage_weight_layout(w_in):
    rows = 256
    return pl.pallas_call(
        _weight_layout_body,
        grid=(D_MODEL // rows,),
        in_specs=[pl.BlockSpec((rows, w_in.shape[1]), lambda i: (i, 0))],
        out_specs=pl.BlockSpec((rows, IN_COLS_ALIGNED), lambda i: (i, 0)),
        out_shape=jax.ShapeDtypeStruct((D_MODEL, IN_COLS_ALIGNED), bf16),
        compiler_params=_cparams(("arbitrary",)),
        name="weight_layout",
    )(w_in)


def _inproj_body(x_ref, w_ref, convw_ref, prow_ref, lng_ref, lnb_ref, ones_ref,
                 q_ref, k_ref, v_ref, z_ref, u_ref, vln_ref, gcol_ref, grow_ref, *ext_refs, tm, ngroups):
    W = DN_WIDTH
    gm = tm // ngroups
    ext = [ext_refs[3 * g:3 * g + 3] for g in range(ngroups)]

    @pl.when(pl.program_id(1) == 0)
    def _():
        for e_ref in ext[0]:
            e_ref[0:8, :] = jnp.zeros((8, W), f32)

    for g in range(ngroups):
        _inproj_rows(x_ref, w_ref, convw_ref, prow_ref, lng_ref, lnb_ref, ones_ref,
                     q_ref, k_ref, v_ref, z_ref, u_ref, vln_ref, gcol_ref, grow_ref, ext[g],
                     ext[(g + 1) % ngroups], r0=g * gm, gm=gm)


def _inproj_rows(x_ref, w_ref, convw_ref, prow_ref, lng_ref, lnb_ref, ones_ref,
                 q_ref, k_ref, v_ref, z_ref, u_ref, vln_ref, gcol_ref, grow_ref, ext, ext_next, *, r0, gm):
    W = DN_WIDTH
    rows = slice(r0, r0 + gm)
    xb = x_ref[0, rows, :].astype(bf16)
    for part, e_ref in enumerate(ext):
        e_ref[8:8 + gm, :] = jnp.dot(xb, w_ref[:, part * W:(part + 1) * W], preferred_element_type=f32)
    zc = 3 * W
    uc = zc + W
    vc = uc + SGU_WIDTH
    bc = vc + SGU_WIDTH

    def conv_silu(part):
        e_ref = ext[part]
        cs = slice(part * W, (part + 1) * W)
        y = convw_ref[3:4, cs] * e_ref[8:8 + gm, :]
        for j in range(CONV_K - 1):
            y = y + convw_ref[j:j + 1, cs] * e_ref[5 + j:5 + j + gm, :]
        ext_next[part][0:8, :] = e_ref[gm:gm + 8, :]
        return _silu(y)

    def group_sums(a):
        return jnp.dot(a.astype(bf16), ones_ref[...], preferred_element_type=f32)

    yq = conv_silu(0)
    ssq = group_sums(yq * yq)
    pba = jnp.dot(xb, w_ref[:, bc:bc + 128], preferred_element_type=f32)
    pv = jnp.dot(xb, w_ref[:, vc:vc + SGU_WIDTH], preferred_element_type=f32)
    yk = conv_silu(1)
    ssk = group_sums(yk * yk)
    pu = jnp.dot(xb, w_ref[:, uc:uc + SGU_WIDTH], preferred_element_type=f32)
    pz = jnp.dot(xb, w_ref[:, zc:zc + W], preferred_element_type=f32)
    q_ref[0, rows, :] = (yq * (lax.rsqrt(ssq + RMS_EPS) * HEAD_DIM ** -0.5)).astype(bf16)
    k_ref[0, rows, :] = (yk * lax.rsqrt(ssk + RMS_EPS)).astype(bf16)
    v_ref[0, rows, :] = conv_silu(2).astype(bf16)

    z_ref[0, rows, :] = _silu(pz).astype(bf16)

    u_ref[0, rows, :] = _gelu_tanh(pu).astype(bf16)
    pv = _gelu_tanh(pv)
    for g in range(SGU_GROUPS):
        sl = slice(g * SGU_CHUNK, (g + 1) * SGU_CHUNK)
        vg = pv[:, sl]
        mu = jnp.mean(vg, axis=-1, keepdims=True)
        vcn = vg - mu
        var = jnp.mean(vcn * vcn, axis=-1, keepdims=True)
        vln_ref[0, rows, sl] = (vcn * lax.rsqrt(var + LN_EPS) * lng_ref[:, sl] + lnb_ref[:, sl]).astype(bf16)

    lane = _iota2((DN_CHUNK, 128), 1)
    beta = _sigmoid(pba)
    lane8 = _iota2((8, DN_CHUNK), 1)
    sub8 = _iota2((8, DN_CHUNK), 0)
    for c in range(gm // DN_CHUNK):
        rs = slice(c * DN_CHUNK, (c + 1) * DN_CHUNK)
        os_ = slice(r0 + c * DN_CHUNK, r0 + (c + 1) * DN_CHUNK)
        pbat = pba[rs].T[0:8, :]
        gt = -jnp.exp(prow_ref[0]) * _softplus(pbat + prow_ref[1])
        gc = jnp.where(sub8 >= DN_HEADS, gt, 0.0)
        shift = 1
        while shift < DN_CHUNK:
            gc = gc + jnp.where(lane8 >= shift, pltpu.roll(gc, shift, axis=1), 0.0)
            shift *= 2
        grow_ref[0, :, os_] = gc
        gc_col = jnp.concatenate([gc, jnp.zeros((DN_CHUNK - 8, DN_CHUNK), f32)], axis=0).T
        gcol_ref[0, os_, :] = jnp.where(lane < DN_HEADS, beta[rs], gc_col)


def _stage_inproj(x, w_re, conv_w, prow, lng, lnb, *, tm, ngroups):
    B, T, _ = x.shape
    wcols = w_re.shape[1]
    grid = (B, T // tm)
    gi = lax.broadcasted_iota(i32, (DN_WIDTH, DN_WIDTH), 0) // 128
    gj = lax.broadcasted_iota(i32, (DN_WIDTH, DN_WIDTH), 1) // 128
    group_ones = (gi == gj).astype(bf16)
    act = lambda: jax.ShapeDtypeStruct((B, T, DN_WIDTH), bf16)
    act_spec = lambda: pl.BlockSpec((1, tm, DN_WIDTH), lambda b, t: (b, t, 0))
    const2 = lambda shp: pl.BlockSpec(shp, lambda b, t: (0, 0))
    return pl.pallas_call(
        functools.partial(_inproj_body, tm=tm, ngroups=ngroups),
        grid=grid,
        in_specs=[
            pl.BlockSpec((1, tm, D_MODEL), lambda b, t: (b, t, 0)),
            const2((D_MODEL, wcols)),
            const2((CONV_K, 3 * DN_WIDTH)),
            pl.BlockSpec((2, 8, 128), lambda b, t: (0, 0, 0)),
            const2((1, SGU_WIDTH)),
            const2((1, SGU_WIDTH)),
            const2((DN_WIDTH, DN_WIDTH)),
        ],
        out_specs=[act_spec() for _ in range(6)] + [
            pl.BlockSpec((1, tm, 128), lambda b, t: (b, t, 0)),
            pl.BlockSpec((1, 8, tm), lambda b, t: (b, 0, t)),
        ],
        out_shape=[act() for _ in range(6)] + [
            jax.ShapeDtypeStruct((B, T, 128), f32),
            jax.ShapeDtypeStruct((B, 8, T), f32),
        ],
        scratch_shapes=[pltpu.VMEM((tm // ngroups + 8, DN_WIDTH), f32) for _ in range(3 * ngroups)],
        compiler_params=_cparams(("arbitrary", "arbitrary")),
        name="inproj",
    )(x, w_re, conv_w, prow, lng, lnb, group_ones)


def _mm(a, b):
    return jnp.dot(a.astype(bf16), b.astype(bf16), preferred_element_type=f32)


def _unit_lower_inverse(nmats, ii, jj):
    n = nmats[0].shape[0]
    eye = (ii == jj).astype(f32)
    leaf = jnp.right_shift(ii, 3) == jnp.right_shift(jj, 3)
    dblk = [jnp.where(leaf, m, 0.0) for m in nmats]
    s1 = [_mm(d, d) for d in dblk]
    r1 = [eye - d for d in dblk]
    both = [_mm(s, jnp.concatenate([s, r], axis=1)) for s, r in zip(s1, r1)]
    r2 = [r + bo[:, n:] for r, bo in zip(r1, both)]
    xs = [r + _mm(bo[:, :n], r) for r, bo in zip(r2, both)]
    size = 8
    while size < n:
        lows = [slice(r + size, r + 2 * size) for r in range(0, n, 2 * size)]
        ups = [slice(r, r + size) for r in range(0, n, 2 * size)]
        rsel = _iota2((n // 2, n), 0)
        ilow = rsel + size * (jnp.right_shift(rsel, size.bit_length() - 1) + 1)
        jlow = _iota2((n // 2, n), 1)
        in_pair_upper = (jlow >= ilow - (ilow & (2 * size - 1))) & (jlow < ilow - (ilow & (size - 1)))
        zeros = jnp.zeros((size, n), f32)
        new_xs = []
        ylows = [_mm(jnp.where(in_pair_upper, jnp.concatenate([m[s] for s in lows], axis=0), 0.0), x)
                 for m, x in zip(nmats, xs)]
        yfull = [jnp.concatenate([piece for k in range(len(lows)) for piece in (zeros, y[k * size:(k + 1) * size])], axis=0)
                 for y in ylows]
        corr = [_mm(jnp.concatenate([x[s] for s in lows], axis=0), yf) for x, yf in zip(xs, yfull)]
        for x, c in zip(xs, corr):
            pieces = []
            for k, (u, l) in enumerate(zip(ups, lows)):
                pieces += [x[u], x[l] - c[k * size:(k + 1) * size]]
            new_xs.append(jnp.concatenate(pieces, axis=0))
        xs = new_xs
        size *= 2
    return xs


def _deltanet_body(q_ref, k_ref, v_ref, z_ref, gcol_ref, grow_ref, nw_ref, y_ref, s_ref, *, nbr, nch, ngroups):
    @pl.when(pl.program_id(1) == 0)
    def _():
        s_ref[...] = jnp.zeros(s_ref.shape, f32)

    per = nbr // ngroups
    for g in range(ngroups):
        _deltanet_rows(q_ref, k_ref, v_ref, z_ref, gcol_ref, grow_ref, nw_ref, y_ref, s_ref,
                       rows=range(g * per, (g + 1) * per), nch=nch)


def _deltanet_rows(q_ref, k_ref, v_ref, z_ref, gcol_ref, grow_ref, nw_ref, y_ref, s_ref, *, rows, nch):
    C = DN_CHUNK
    S = [(b, h) for b in rows for h in range(DN_HEADS)]
    P = [(b, c, h) for c in range(nch) for b, h in S]

    ii = _iota2((C, C), 0)
    jj = _iota2((C, C), 1)
    causal = ii >= jj
    rs = [slice(c * C, (c + 1) * C) for c in range(nch)]
    hs = [slice(h * HEAD_DIM, (h + 1) * HEAD_DIM) for h in range(DN_HEADS)]
    gcol = {(b, c): gcol_ref[b, rs[c], :] for b in rows for c in range(nch)}
    qh = {(b, c, h): q_ref[b, rs[c], hs[h]] for b, c, h in P}
    kh = {(b, c, h): k_ref[b, rs[c], hs[h]] for b, c, h in P}
    vh = {(b, c, h): v_ref[b, rs[c], hs[h]] for b, c, h in P}
    gc_b = {(b, c, h): jnp.broadcast_to(gcol[b, c][:, DN_HEADS + h:DN_HEADS + h + 1], (C, HEAD_DIM)) for b, c, h in P}
    beta_b = {(b, c, h): jnp.broadcast_to(gcol[b, c][:, h:h + 1], (C, HEAD_DIM)) for b, c, h in P}
    gc_r = {(b, c, h): jnp.broadcast_to(grow_ref[b, DN_HEADS + h:DN_HEADS + h + 1, rs[c]], (C, C)) for b, c, h in P}
    decay = {p: jnp.exp(jnp.where(causal, gc_b[p] - gc_r[p], -1e30)) for p in P}

    kf = {p: kh[p].astype(f32) for p in P}
    kb = {p: kf[p] * beta_b[p] for p in P}
    kk = {p: lax.dot_general(kb[p].astype(bf16), kh[p], NT_DIMS, preferred_element_type=f32) for p in P}
    a_intra = {p: lax.dot_general(qh[p], kh[p], NT_DIMS, preferred_element_type=f32) * decay[p] for p in P}
    nmat = [jnp.where(ii > jj, kk[p] * decay[p], 0.0) for p in P]
    tinv = dict(zip(P, _unit_lower_inverse(nmat, ii, jj)))

    eg = {p: jnp.exp(gc_b[p]) for p in P}
    rhs = {p: jnp.concatenate([vh[p].astype(f32) * beta_b[p], kb[p] * eg[p]], axis=1) for p in P}
    sol = {p: _mm(tinv[p], rhs[p]) for p in P}

    q_dec = {p: qh[p].astype(f32) * eg[p] for p in P}
    g_last = {p: gc_b[p][C - 1:C, :] for p in P}
    kdt = {p: (kf[p] * jnp.exp(g_last[p] - gc_b[p])).T for p in P}

    state = {(b, h): s_ref[b * DN_HEADS + h] for b, h in S}
    for c in range(nch):
        m1 = {(b, h): _mm(jnp.concatenate([sol[b, c, h][:, HEAD_DIM:], q_dec[b, c, h]], axis=0), state[b, h]) for b, h in S}
        v_new = {(b, h): sol[b, c, h][:, :HEAD_DIM] - m1[b, h][:C] for b, h in S}
        m2 = {(b, h): _mm(jnp.concatenate([a_intra[b, c, h], kdt[b, c, h]], axis=0), v_new[b, h]) for b, h in S}
        state = {(b, h): state[b, h] * jnp.exp(g_last[b, c, h]) + m2[b, h][C:] for b, h in S}
        for b, h in S:
            o = m1[b, h][C:] + m2[b, h][:C]
            rms = lax.rsqrt(jnp.mean(o * o, axis=-1, keepdims=True) + RMS_EPS)
            y_ref[b, rs[c], hs[h]] = (o * rms * nw_ref[...] * z_ref[b, rs[c], hs[h]].astype(f32)).astype(bf16)
    for b, h in S:
        s_ref[b * DN_HEADS + h] = state[b, h]


def _stage_deltanet(q, k, v, z, gcol, grow, norm_w, *, nbr, nch, ngroups):
    B, T, _ = q.shape
    tt = nch * DN_CHUNK
    act_spec = lambda: pl.BlockSpec((nbr, tt, DN_WIDTH), lambda b, t: (b, t, 0))
    return pl.pallas_call(
        functools.partial(_deltanet_body, nbr=nbr, nch=nch, ngroups=ngroups),
        grid=(B // nbr, T // tt),
        in_specs=[act_spec(), act_spec(), act_spec(), act_spec(),
                  pl.BlockSpec((nbr, tt, 128), lambda b, t: (b, t, 0)),
                  pl.BlockSpec((nbr, 8, tt), lambda b, t: (b, 0, t)),
                  pl.BlockSpec((1, HEAD_DIM), lambda b, t: (0, 0))],
        out_specs=act_spec(),
        out_shape=jax.ShapeDtypeStruct((B, T, DN_WIDTH), bf16),
        scratch_shapes=[pltpu.VMEM((nbr * DN_HEADS, HEAD_DIM, HEAD_DIM), f32)],
        compiler_params=_cparams(("arbitrary", "arbitrary")),
        name="deltanet",
    )(q, k, v, z, gcol, grow, norm_w)


def _mixout_body(ydn_ref, u_ref, vln_ref, x_hbm, ws_ref, bsp_ref, wout_ref, g1_ref, b1_ref, wrt_ref, brt_ref,
                 h_ref, hrow_ref, ids_ref, wts_ref, cnt_ref, ycat_ref, xring_ref, xsem, *, tm, nsteps):
    C = SGU_CHUNK
    step = pl.program_id(0) * pl.num_programs(1) + pl.program_id(1)
    slot = lax.rem(step, X_SLOTS)

    def x_copy(s, sl):
        return pltpu.make_async_copy(x_hbm.at[s], xring_ref.at[sl], xsem.at[sl])

    @pl.when(step == 0)
    def _():
        cnt_ref[...] = jnp.zeros(cnt_ref.shape, f32)
        for j in range(X_SLOTS - 1):
            x_copy(j, j).start()

    ahead = step + (X_SLOTS - 1)

    @pl.when(ahead < nsteps)
    def _():
        x_copy(ahead, lax.rem(ahead, X_SLOTS)).start()

    x_copy(step, slot).wait()
    ii = _iota2((C, C), 0)
    jj = _iota2((C, C), 1)
    ycat_ref[:, 0:DN_WIDTH] = ydn_ref[0]
    for g in range(SGU_GROUPS):
        gs = slice(g * C, (g + 1) * C)
        wsg = jnp.where(ii >= jj, ws_ref[g], 0.0).astype(bf16)
        for c in range(tm // C):
            rs = slice(c * C, (c + 1) * C)
            mixed = jnp.dot(wsg, vln_ref[0, rs, gs], preferred_element_type=f32) + bsp_ref[:, gs]
            ycat_ref[rs, DN_WIDTH + g * C:DN_WIDTH + (g + 1) * C] = (u_ref[0, rs, gs].astype(f32) * mixed).astype(bf16)

    RB = 128
    blocks = [slice(r, r + RB) for r in range(0, tm, RB)]
    mix = [jnp.dot(ycat_ref[rb, :], wout_ref[...], preferred_element_type=f32) for rb in blocks]
    h1s = []
    for rb, m in zip(blocks, mix):
        hp = DEEPNORM_ALPHA * xring_ref[slot, rb, :] + m
        mu = jnp.mean(hp, axis=-1, keepdims=True)
        hc = hp - mu
        var = jnp.mean(hc * hc, axis=-1, keepdims=True)
        h1 = hc * lax.rsqrt(var + LN_EPS) * g1_ref[...] + b1_ref[...]
        h_ref[0, rb, :] = h1
        h1b = h1.astype(bf16)
        hrow_ref[rb] = h1b.reshape(RB, ROW_TILE, 128)
        h1s.append(h1b)

    logit_blocks = [lax.dot_general(wrt_ref[...], hb, NT_DIMS, preferred_element_type=f32) + brt_ref[...] for hb in h1s]
    sub = _iota2((8, RB), 0)
    subf = sub.astype(f32)
    sub_e = _iota2((N_EXPERTS, RB), 0).astype(f32)
    chosen = []
    for rb, logits in zip(blocks, logit_blocks):
        gl = logits[0:8]
        gmax = jnp.max(gl, axis=0, keepdims=True)
        g_idx = jnp.min(jnp.where(gl == gmax, subf, float(MOE_GROUPS)), axis=0, keepdims=True)
        p_group = 1.0 / jnp.sum(jnp.exp(gl - gmax), axis=0, keepdims=True)
        within = jnp.zeros((8, RB), f32)
        for g in range(MOE_GROUPS):
            within = within + jnp.where(g_idx == float(g), logits[8 + 8 * g:16 + 8 * g], 0.0)
        m1 = jnp.max(within, axis=0, keepdims=True)
        i1 = jnp.min(jnp.where(within == m1, subf, float(EXPERTS_PER_GROUP)), axis=0, keepdims=True)
        rest = jnp.where(subf == i1, -jnp.inf, within)
        m2 = jnp.max(rest, axis=0, keepdims=True)
        i2 = jnp.min(jnp.where(rest == m2, subf, float(EXPERTS_PER_GROUP)), axis=0, keepdims=True)
        e = jnp.exp(m2 - m1)
        w1 = p_group / (1.0 + e)
        w2 = p_group * e / (1.0 + e)
        e1 = g_idx * float(EXPERTS_PER_GROUP) + i1
        e2 = g_idx * float(EXPERTS_PER_GROUP) + i2
        ids_ref[:, rb] = jnp.where(sub == 0, e1, jnp.where(sub == 1, e2, 0.0)).astype(i32)
        wts_ref[:, rb] = jnp.where(sub == 0, w1, jnp.where(sub == 1, w2, 0.0))
        chosen.append(((sub_e == e1).astype(f32) + (sub_e == e2).astype(f32)).astype(bf16))
    ones = jnp.ones((RB, 128), bf16)
    cnt_ref[...] = cnt_ref[...] + sum(jnp.dot(oh, ones, preferred_element_type=f32) for oh in chosen)


def _stage_mixout(ydn, u, vln, x, ws, bsp, wout, g1, b1, wrt, brt, *, tm):
    B, T, _ = x.shape
    nt = T // tm
    act_spec = lambda: pl.BlockSpec((1, tm, DN_WIDTH), lambda b, t: (b, t, 0))
    const2 = lambda shp: pl.BlockSpec(shp, lambda b, t: (0, 0))
    tok_spec = lambda: pl.BlockSpec((8, tm), lambda b, t: (0, b * nt + t))
    return pl.pallas_call(
        functools.partial(_mixout_body, tm=tm, nsteps=B * nt),
        grid=(B, nt),
        in_specs=[act_spec(), act_spec(), act_spec(),
                  pl.BlockSpec(memory_space=pl.ANY),
                  pl.BlockSpec((SGU_GROUPS, SGU_CHUNK, SGU_CHUNK), lambda b, t: (0, 0, 0)),
                  const2((SGU_CHUNK, SGU_WIDTH)),
                  const2((D_MODEL, D_MODEL)),
                  const2((1, D_MODEL)), const2((1, D_MODEL)),
                  const2((128, D_MODEL)), const2((128, 128))],
        out_specs=[pl.BlockSpec((1, tm, D_MODEL), lambda b, t: (b, t, 0)),
                   pl.BlockSpec((tm, ROW_TILE, 128), lambda b, t: (b * nt + t, 0, 0)), tok_spec(), tok_spec(),
                   const2((N_EXPERTS, 128))],
        out_shape=[jax.ShapeDtypeStruct((B, T, D_MODEL), f32),
                   jax.ShapeDtypeStruct((B * T, ROW_TILE, 128), bf16),
                   jax.ShapeDtypeStruct((8, B * T), i32),
                   jax.ShapeDtypeStruct((8, B * T), f32),
                   jax.ShapeDtypeStruct((N_EXPERTS, 128), f32)],
        scratch_shapes=[pltpu.VMEM((tm, D_MODEL), bf16), pltpu.VMEM((X_SLOTS, tm, D_MODEL), f32),
                        pltpu.SemaphoreType.DMA((X_SLOTS,))],
        compiler_params=_cparams(("arbitrary", "arbitrary")),
        name="mixout",
    )(ydn, u, vln, x.reshape(B * nt, tm, D_MODEL), ws, bsp, wout, g1, b1, wrt, brt)


def _route_body(cnt_ref, ids_ref, dest_ref, meta_ref, blk_ref, pstart_ref, *, tm, nb_pad):
    def ranges_and_block_table():
        cnt = cnt_ref[...]
        padded = jnp.floor((cnt + (MOE_BLOCK - 1)) * (1.0 / MOE_BLOCK)) * MOE_BLOCK
        ei = _iota2((N_EXPERTS, N_EXPERTS), 0)
        ej = _iota2((N_EXPERTS, N_EXPERTS), 1)
        pends = jnp.dot((ei >= ej).astype(f32), padded, precision=HIGHEST, preferred_element_type=f32)
        pstart = pends - padded
        pstart_ref[...] = pstart
        s64 = _iota2((N_EXPERTS, 128), 0)
        l64 = _iota2((N_EXPERTS, 128), 1)
        diag = s64 == l64
        fill_off = jnp.sum(jnp.where(diag, pstart + cnt, 0.0), axis=0, keepdims=True)
        fill_n = jnp.sum(jnp.where(diag, padded - cnt, 0.0), axis=0, keepdims=True)
        nused = pends[N_EXPERTS - 1:N_EXPERTS, :] * (1.0 / MOE_BLOCK)
        m8 = _iota2((8, 128), 0)
        meta_ref[...] = jnp.where(m8 == 0, fill_off, jnp.where(m8 == 1, fill_n, jnp.where(m8 == 2, nused, 0.0))).astype(i32)
        bstart = (_iota2((N_EXPERTS, nb_pad), 1) * MOE_BLOCK).astype(f32)
        pe = jnp.concatenate([pends] * (nb_pad // 128), axis=1)
        be = jnp.sum((pe <= bstart).astype(f32), axis=0, keepdims=True)
        be = jnp.minimum(be, float(N_EXPERTS - 1))
        blk_ref[...] = jnp.broadcast_to(be, (8, nb_pad)).astype(i32)

    ranges_and_block_table()

    C = 128
    earlier = (_iota2((C, C), 0) < _iota2((C, C), 1)).astype(bf16)
    w2 = jnp.concatenate([earlier, jnp.ones((C, C), bf16)], axis=1)
    sub = _iota2((N_EXPERTS, C), 0)
    sub8 = _iota2((8, C), 0)

    def tile(t, carry):
        run = pstart_ref[...]
        for k in range(tm // C):
            ls = pl.ds(pl.multiple_of(t * tm + k * C, C), C)
            is1 = sub == ids_ref[0:1, ls]
            is2 = sub == ids_ref[1:2, ls]
            oh = (is1.astype(f32) + is2.astype(f32)).astype(bf16)
            r = jnp.dot(oh, w2, preferred_element_type=f32)
            nxt = r[:, :C] + run
            d1 = jnp.sum(jnp.where(is1, nxt, 0.0), axis=0, keepdims=True)
            d2 = jnp.sum(jnp.where(is2, nxt, 0.0), axis=0, keepdims=True)
            dest_ref[:, ls] = jnp.where(sub8 == 0, d1, jnp.where(sub8 == 1, d2, 0.0)).astype(i32)
            run = run + r[:, C:]
        pstart_ref[...] = run
        return carry

    lax.fori_loop(0, ids_ref.shape[1] // tm, tile, 0)


def _stage_route(cnt, ids, *, tm, nb_pad):
    n = ids.shape[1]
    return pl.pallas_call(
        functools.partial(_route_body, tm=tm, nb_pad=nb_pad),
        grid=(1,),
        in_specs=[pl.BlockSpec((N_EXPERTS, 128), lambda i: (0, 0)), pl.BlockSpec((8, n), lambda i: (0, 0))],
        out_specs=[pl.BlockSpec((8, n), lambda i: (0, 0)),
                   pl.BlockSpec((8, 128), lambda i: (0, 0)),
                   pl.BlockSpec((8, nb_pad), lambda i: (0, 0))],
        out_shape=[jax.ShapeDtypeStruct((8, n), i32), jax.ShapeDtypeStruct((8, 128), i32),
                   jax.ShapeDtypeStruct((8, nb_pad), i32)],
        scratch_shapes=[pltpu.VMEM((N_EXPERTS, 128), f32)],
        compiler_params=_cparams(("arbitrary",)),
        name="moe_route",
    )(cnt, ids)


def _dispatch_body(fill_off_ref, fill_n_ref, nused_ref, dest_ref, h_hbm, xs_ref, src_ref, zero_ref, isem, sem, zsem,
                   *, tm, nsteps):
    i = pl.program_id(0)
    slot = lax.rem(i, X_SLOTS)

    def in_copy(step, s):
        return pltpu.make_async_copy(h_hbm.at[pl.ds(step * tm, tm)], src_ref.at[s], isem.at[s])

    def drain(s):
        for _ in range(2):
            pltpu.make_async_copy(src_ref.at[s], xs_ref.at[pl.ds(0, tm)], sem.at[s]).wait()

    @pl.when(i == 0)
    def _():
        for j in range(min(X_SLOTS - 1, nsteps)):
            in_copy(j, j).start()

    in_copy(i, slot).wait()

    def row_copy(t, d):
        return pltpu.make_async_copy(src_ref.at[slot, t], xs_ref.at[d], sem.at[slot])

    def issue(t, carry):
        row_copy(t, dest_ref[0, 0, t]).start(priority=0)
        row_copy(t, dest_ref[0, 1, t]).start(priority=1)
        return carry

    lax.fori_loop(0, tm, issue, 0, unroll=8)

    @pl.when(i == 0)
    def _():
        zero_ref[...] = jnp.zeros(zero_ref.shape, bf16)

        def fill(start):
            def body(e, carry):
                off = fill_off_ref[e]
                npad = fill_n_ref[e]
                bit = MOE_BLOCK // 2
                while bit:
                    @pl.when((npad & bit) != 0)
                    def _(off=off, bit=bit):
                        cp = pltpu.make_async_copy(zero_ref.at[pl.ds(0, bit)], xs_ref.at[pl.ds(off, bit)], zsem)
                        cp.start() if start else cp.wait()
                    off = off + (npad & bit)
                    bit //= 2
                return carry
            return body

        lax.fori_loop(0, N_EXPERTS, fill(True), 0)
        lax.fori_loop(0, N_EXPERTS, fill(False), 0)

        def tail_copy(b):
            return pltpu.make_async_copy(zero_ref, xs_ref.at[pl.ds(b * MOE_BLOCK, MOE_BLOCK)], zsem)

        nblocks = xs_ref.shape[0] // MOE_BLOCK
        lax.fori_loop(nused_ref[0], nblocks, lambda b, c: (tail_copy(b).start(), c)[1], 0)
        lax.fori_loop(nused_ref[0], nblocks, lambda b, c: (tail_copy(0).wait(), c)[1], 0)

    @pl.when(i >= 1)
    def _():
        drain(lax.rem(i - 1, X_SLOTS))

    ahead = i + (X_SLOTS - 1)

    @pl.when(ahead < nsteps)
    def _():
        in_copy(ahead, lax.rem(ahead, X_SLOTS)).start()

    @pl.when(i == nsteps - 1)
    def _():
        drain(slot)


def _stage_dispatch(fill_off, fill_n, nused, dest3, hrow, p_rows, *, tm):
    n = hrow.shape[0]
    nsteps = n // tm
    return pl.pallas_call(
        functools.partial(_dispatch_body, tm=tm, nsteps=nsteps),
        grid_spec=pltpu.PrefetchScalarGridSpec(
            num_scalar_prefetch=3,
            grid=(nsteps,),
            in_specs=[pl.BlockSpec((1, 2, tm), lambda i, fo, fn, nu: (i, 0, 0), memory_space=pltpu.SMEM),
                      pl.BlockSpec(memory_space=pl.ANY)],
            out_specs=pl.BlockSpec(memory_space=pl.ANY),
            scratch_shapes=[pltpu.VMEM((X_SLOTS, tm, ROW_TILE, 128), bf16),
                            pltpu.VMEM((MOE_BLOCK, ROW_TILE, 128), bf16),
                            pltpu.SemaphoreType.DMA((X_SLOTS,)), pltpu.SemaphoreType.DMA((X_SLOTS,)),
                            pltpu.SemaphoreType.DMA],
        ),
        out_shape=jax.ShapeDtypeStruct((p_rows, ROW_TILE, 128), bf16),
        compiler_params=_cparams(("arbitrary",)),
        name="moe_dispatch",
    )(fill_off, fill_n, nused, dest3, hrow)


def _experts_body(blk_ref, nused_ref, xs_hbm, wg_hbm, wu_hbm, wd_hbm, ys_hbm,
                  xbuf_ref, wg32_ref, wu32_ref, wd32_ref, wgu16_ref, wd16_ref, obuf_ref, zero_ref, xsem, wsem, osem, zsem,
                  *, nblocks):
    nused = nused_ref[0]

    def weight_copies(ex):
        return (pltpu.make_async_copy(wg_hbm.at[ex], wg32_ref, wsem.at[0]),
                pltpu.make_async_copy(wu_hbm.at[ex], wu32_ref, wsem.at[1]),
                pltpu.make_async_copy(wd_hbm.at[ex], wd32_ref, wsem.at[2]))

    def x_copy(block, s):
        return pltpu.make_async_copy(xs_hbm.at[pl.ds(block * MOE_BLOCK, MOE_BLOCK)], xbuf_ref.at[s], xsem.at[s])

    def out_copy(block, s):
        return pltpu.make_async_copy(obuf_ref.at[s], ys_hbm.at[pl.ds(block * MOE_BLOCK, MOE_BLOCK)], osem.at[s])

    def tail_copy(block):
        return pltpu.make_async_copy(zero_ref, ys_hbm.at[pl.ds(block * MOE_BLOCK, MOE_BLOCK)], zsem)

    zero_ref[...] = jnp.zeros(zero_ref.shape, bf16)
    lax.fori_loop(nused, nblocks, lambda b, c: (tail_copy(b).start(), c)[1], 0)

    @pl.when(nused > 0)
    def _():
        for cp in weight_copies(blk_ref[0]):
            cp.start()
        for j in range(X_SLOTS - 1):
            @pl.when(j < nused)
            def _(j=j):
                x_copy(j, j).start()

    def step(i, carry):
        e = blk_ref[i]
        slot = lax.rem(i, X_SLOTS)
        ahead = i + (X_SLOTS - 1)

        @pl.when(ahead < nused)
        def _():
            x_copy(ahead, lax.rem(ahead, X_SLOTS)).start()

        @pl.when((i == 0) | (e != blk_ref[jnp.maximum(i - 1, 0)]))
        def _():
            for cp in weight_copies(e):
                cp.wait()
            wgu16_ref[:, 0:D_EXPERT] = wg32_ref[...].astype(bf16)
            wgu16_ref[:, D_EXPERT:2 * D_EXPERT] = wu32_ref[...].astype(bf16)
            wd16_ref[...] = wd32_ref[...].astype(bf16)
            nxt = lax.while_loop(lambda j: (j < nused) & (blk_ref[jnp.minimum(j, nused - 1)] == e), lambda j: j + 1,
                                 i + 1)

            @pl.when(nxt < nused)
            def _():
                for cp in weight_copies(blk_ref[jnp.minimum(nxt, nused - 1)]):
                    cp.start(priority=1)

        @pl.when(i >= X_SLOTS)
        def _():
            out_copy(i - X_SLOTS, slot).wait()

        x_copy(i, slot).wait()
        half = MOE_BLOCK // 2
        rows = [slice(p * half, (p + 1) * half) for p in range(2)]
        gu = [jnp.dot(xbuf_ref[slot, r].reshape(half, D_MODEL), wgu16_ref[...], preferred_element_type=f32)
              for r in rows]
        hid = [(_silu(g[:, :D_EXPERT]) * g[:, D_EXPERT:]).astype(bf16) for g in gu]
        y = [jnp.dot(hd, wd16_ref[...], preferred_element_type=f32) for hd in hid]
        for r, yp in zip(rows, y):
            obuf_ref[slot, r] = yp.astype(bf16).reshape(half, ROW_TILE, 128)
        out_copy(i, slot).start()
        return carry

    lax.fori_loop(0, nused, step, 0)

    for j in range(X_SLOTS):
        last = nused - 1 - j

        @pl.when(last >= 0)
        def _(last=last):
            out_copy(last, lax.rem(last, X_SLOTS)).wait()

    lax.fori_loop(nused, nblocks, lambda b, c: (tail_copy(b).wait(), c)[1], 0)


def _stage_experts(blk_e, nused, xs, w_gate, w_up, w_down):
    p_rows = xs.shape[0]
    nb = p_rows // MOE_BLOCK

    return pl.pallas_call(
        functools.partial(_experts_body, nblocks=nb),
        grid_spec=pltpu.PrefetchScalarGridSpec(
            num_scalar_prefetch=2,
            grid=(1,),
            in_specs=[pl.BlockSpec(memory_space=pl.ANY),
                      pl.BlockSpec(memory_space=pl.ANY),
                      pl.BlockSpec(memory_space=pl.ANY),
                      pl.BlockSpec(memory_space=pl.ANY)],
            out_specs=pl.BlockSpec(memory_space=pl.ANY),
            scratch_shapes=[pltpu.VMEM((X_SLOTS, MOE_BLOCK, ROW_TILE, 128), bf16),
                            pltpu.VMEM((D_MODEL, D_EXPERT), f32), pltpu.VMEM((D_MODEL, D_EXPERT), f32),
                            pltpu.VMEM((D_EXPERT, D_MODEL), f32),
                            pltpu.VMEM((D_MODEL, 2 * D_EXPERT), bf16), pltpu.VMEM((D_EXPERT, D_MODEL), bf16),
                            pltpu.VMEM((X_SLOTS, MOE_BLOCK, ROW_TILE, 128), bf16),
                            pltpu.VMEM((MOE_BLOCK, ROW_TILE, 128), bf16),
                            pltpu.SemaphoreType.DMA((X_SLOTS,)), pltpu.SemaphoreType.DMA((3,)),
                            pltpu.SemaphoreType.DMA((X_SLOTS,)), pltpu.SemaphoreType.DMA],
        ),
        out_shape=jax.ShapeDtypeStruct((p_rows, ROW_TILE, 128), bf16),
        compiler_params=_cparams(("arbitrary",)),
        name="moe_experts",
    )(blk_e, nused, xs, w_gate, w_up, w_down)


def _combine_body(dcur_ref, dnext_ref, h_ref, wts_ref, g2_ref, b2_ref, ys_ref, o_ref, ybuf_ref, sem, *, tm, nsteps):
    i = pl.program_id(0)
    slot = lax.rem(i, 2)

    def issue_tile(d_ref, s):
        def body(t, carry):
            for k in range(2):
                pltpu.make_async_copy(ys_ref.at[d_ref[0, k, t]], ybuf_ref.at[s, k, t], sem.at[s]).start(priority=k)
            return carry

        lax.fori_loop(0, tm, body, 0, unroll=8)

    @pl.when(i == 0)
    def _():
        issue_tile(dcur_ref, 0)

    @pl.when(i + 1 < nsteps)
    def _():
        issue_tile(dnext_ref, 1 - slot)

    for k in range(2):
        pltpu.make_async_copy(ys_ref.at[pl.ds(0, tm)], ybuf_ref.at[slot, k], sem.at[slot]).wait()

    pieces = []
    for c in range(tm // 128):
        ls = slice(c * 128, (c + 1) * 128)
        w1c = jnp.broadcast_to(wts_ref[0:1, ls], (128, 128)).T
        w2c = jnp.broadcast_to(wts_ref[1:2, ls], (128, 128)).T
        w1f = jnp.concatenate([w1c] * (D_MODEL // 128), axis=1)
        w2f = jnp.concatenate([w2c] * (D_MODEL // 128), axis=1)
        y1 = ybuf_ref[slot, 0, ls].reshape(128, D_MODEL).astype(f32)
        y2 = ybuf_ref[slot, 1, ls].reshape(128, D_MODEL).astype(f32)
        pieces.append(w1f * y1 + w2f * y2)
    ffn = jnp.concatenate(pieces, axis=0)
    hp = DEEPNORM_ALPHA * h_ref[...] + ffn
    mu = jnp.mean(hp, axis=-1, keepdims=True)
    hc = hp - mu
    var = jnp.mean(hc * hc, axis=-1, keepdims=True)
    o_ref[...] = hc * lax.rsqrt(var + LN_EPS) * g2_ref[...] + b2_ref[...]


def _stage_combine(dest3, h2, wts, g2, b2, ys, *, tm):
    n = h2.shape[0]
    nsteps = n // tm
    return pl.pallas_call(
        functools.partial(_combine_body, tm=tm, nsteps=nsteps),
        grid=(nsteps,),
        in_specs=[pl.BlockSpec((1, 2, tm), lambda i: (i, 0, 0), memory_space=pltpu.SMEM),
                  pl.BlockSpec((1, 2, tm), lambda i: (jnp.minimum(i + 1, nsteps - 1), 0, 0), memory_space=pltpu.SMEM),
                  pl.BlockSpec((tm, D_MODEL), lambda i: (i, 0)),
                  pl.BlockSpec((8, tm), lambda i: (0, i)),
                  pl.BlockSpec((1, D_MODEL), lambda i: (0, 0)),
                  pl.BlockSpec((1, D_MODEL), lambda i: (0, 0)),
                  pl.BlockSpec(memory_space=pl.ANY)],
        out_specs=pl.BlockSpec((tm, D_MODEL), lambda i: (i, 0)),
        out_shape=jax.ShapeDtypeStruct((n, D_MODEL), f32),
        scratch_shapes=[pltpu.VMEM((2, 2, tm, ROW_TILE, 128), bf16), pltpu.SemaphoreType.DMA((2,))],
        compiler_params=_cparams(("arbitrary",)),
        name="moe_combine",
    )(dest3, dest3, h2, wts, g2, b2, ys)


def _layer(h, w_in, conv_w, a_log, dt_bias, dn_norm_w, sgu_ln_g, sgu_ln_b, w_spatial, b_spatial, w_out,
           ln1_g, ln1_b, w_rg, b_rg, w_re, b_re, w_gate, w_up, w_down, ln2_g, ln2_b,
           *, tm_in, in_groups, dn_rows, dn_chunks, dn_groups, tm_mix, tm_rank, tm_disp, tm_comb):
    B, T, _ = h.shape
    n = B * T
    w_cols = _stage_weight_layout(w_in)
    decay_prm = jnp.stack([a_log, dt_bias])
    prow = jnp.broadcast_to(jnp.pad(decay_prm, ((0, 0), (DN_HEADS, 8 - 2 * DN_HEADS)))[:, :, None], (2, 8, 128))

    q, k, v, z, u, vln, gcol, grow = _stage_inproj(
        h, w_cols, conv_w, prow, sgu_ln_g[None, :], sgu_ln_b[None, :], tm=tm_in, ngroups=in_groups)
    ydn = _stage_deltanet(q, k, v, z, gcol, grow, dn_norm_w[None, :], nbr=dn_rows, nch=dn_chunks, ngroups=dn_groups)

    bsp = jnp.broadcast_to(b_spatial.T[:, :, None], (SGU_CHUNK, SGU_GROUPS, SGU_CHUNK)).reshape(SGU_CHUNK, SGU_WIDTH)
    n_logit = MOE_GROUPS + N_EXPERTS
    wrt = jnp.pad(jnp.concatenate([w_rg, w_re], axis=1).T, ((0, 128 - n_logit), (0, 0))).astype(bf16)
    brt = jnp.broadcast_to(jnp.pad(jnp.concatenate([b_rg, b_re]), (0, 128 - n_logit))[:, None], (128, 128))
    h1, hrow, ids, wts, cnt = _stage_mixout(ydn, u, vln, h, w_spatial, bsp, w_out.astype(bf16), ln1_g[None, :],
                                       ln1_b[None, :], wrt, brt, tm=tm_mix)

    p_rows = (-(-(n * 2) // MOE_BLOCK)) * MOE_BLOCK + N_EXPERTS * MOE_BLOCK
    nb = p_rows // MOE_BLOCK
    nb_pad = (-(-nb // 128)) * 128
    dest, meta, blk = _stage_route(cnt, ids, tm=tm_rank, nb_pad=nb_pad)

    h2 = h1.reshape(n, D_MODEL)
    dest_d = dest[0:2].reshape(2, n // tm_disp, tm_disp).transpose(1, 0, 2)
    xs = _stage_dispatch(meta[0, :N_EXPERTS], meta[1, :N_EXPERTS], meta[2, 0:1], dest_d, hrow, p_rows, tm=tm_disp)
    ys = _stage_experts(blk[0, :nb], meta[2, 0:1], xs, w_gate, w_up, w_down)
    dest_c = dest[0:2].reshape(2, n // tm_comb, tm_comb).transpose(1, 0, 2)
    out = _stage_combine(dest_c, h2, wts, ln2_g[None, :], ln2_b[None, :], ys, tm=tm_comb)
    return out.reshape(B, T, D_MODEL)


def kernel(x, w_in, conv_w, a_log, dt_bias, dn_norm_w, sgu_ln_g, sgu_ln_b, w_spatial, b_spatial, w_out, ln1_g, ln1_b, w_router_group, b_router_group, w_router_expert, b_router_expert, w_gate, w_up, w_down, ln2_g, ln2_b):
    h = x
    for l in range(w_in.shape[0]):
        h = _layer(h, w_in[l], conv_w[l], a_log[l], dt_bias[l], dn_norm_w[l], sgu_ln_g[l], sgu_ln_b[l],
                   w_spatial[l], b_spatial[l], w_out[l], ln1_g[l], ln1_b[l],
                   w_router_group[l], b_router_group[l], w_router_expert[l], b_router_expert[l],
                   w_gate[l], w_up[l], w_down[l], ln2_g[l], ln2_b[l],
                   tm_in=512, in_groups=1, dn_rows=8, dn_chunks=2, dn_groups=4, tm_mix=1024, tm_rank=1024, tm_disp=2048, tm_comb=512)
    return h
```

```python
import functools

import jax
import jax.numpy as jnp
from jax import lax
from jax.experimental import pallas as pl
from jax.experimental.pallas import tpu as pltpu

f32 = jnp.float32
bf16 = jnp.bfloat16
i32 = jnp.int32

D_MODEL = 1024
DN_WIDTH = 512
DN_HEADS = 4
HEAD_DIM = 128
CONV_K = 4
SGU_WIDTH = 512
SGU_GROUPS = 4
SGU_CHUNK = 128
DN_CHUNK = 128
MOE_GROUPS = 8
EXPERTS_PER_GROUP = 8
N_EXPERTS = 64
D_EXPERT = 512
MOE_BLOCK = 256
IN_COLS_ALIGNED = 4 * DN_WIDTH + 2 * SGU_WIDTH + 128
X_SLOTS = 4
ROW_TILE = D_MODEL // 128
DEEPNORM_ALPHA = 2.0 ** 0.25
LN_EPS = 1e-5
RMS_EPS = 1e-6
HIGHEST = lax.Precision.HIGHEST
VMEM_LIMIT_BYTES = 56 * 1024 * 1024

NT_DIMS = (((1,), (1,)), ((), ()))


def _cparams(sem, flags=None):
    return pltpu.CompilerParams(dimension_semantics=sem, vmem_limit_bytes=VMEM_LIMIT_BYTES, flags=flags)


def _sigmoid(x):
    return 1.0 / (1.0 + jnp.exp(-x))


def _silu(x):
    h = 0.5 * x
    return h + h * jnp.tanh(h)


def _softplus(x):
    return jnp.maximum(x, 0.0) + jnp.log1p(jnp.exp(-jnp.abs(x)))


def _gelu_tanh(x):
    c = 0.7978845608028654
    return x * (0.5 * (1.0 + jnp.tanh(c * (x + 0.044715 * (x * x * x)))))


def _iota2(shape, axis):
    return lax.broadcasted_iota(i32, shape, axis)


def _weight_layout_body(wt_ref, o_ref):
    qkvz = 4 * DN_WIDTH
    uv0 = qkvz + 2 * DN_HEADS
    cols = wt_ref.shape[1]
    o_ref[:, 0:qkvz] = wt_ref[0:qkvz, :].T.astype(bf16)
    o_ref[:, qkvz:qkvz + 2 * SGU_WIDTH] = wt_ref[uv0:uv0 + 2 * SGU_WIDTH, :].T.astype(bf16)
    ba = jnp.concatenate([wt_ref[qkvz:uv0, :], jnp.zeros((128 - 2 * DN_HEADS, cols), f32)], axis=0)
    o_ref[:, qkvz + 2 * SGU_WIDTH:IN_COLS_ALIGNED] = ba.T.astype(bf16)


def _stage_weight_layout(w_in):
    cols = 256
    wt = w_in.T
    return pl.pallas_call(
        _weight_layout_body,
        grid=(D_MODEL // cols,),
        in_specs=[pl.BlockSpec((wt.shape[0], cols), lambda i: (0, i))],
        out_specs=pl.BlockSpec((cols, IN_COLS_ALIGNED), lambda i: (i, 0)),
        out_shape=jax.ShapeDtypeStruct((D_MODEL, IN_COLS_ALIGNED), bf16),
        compiler_params=_cparams(("arbitrary",)),
        name="weight_layout",
    )(wt)


def _inproj_body(x_ref, w_ref, convw_ref, prow_ref, lng_ref, lnb_ref, ones_ref,
                 q_ref, k_ref, v_ref, z_ref, u_ref, vln_ref, gcol_ref, grow_ref, *ext_refs, tm, ngroups):
    W = DN_WIDTH
    gm = tm // ngroups
    ext = [ext_refs[3 * g:3 * g + 3] for g in range(ngroups)]

    @pl.when(pl.program_id(1) == 0)
    def _():
        for e_ref in ext[0]:
            e_ref[0:8, :] = jnp.zeros((8, W), f32)

    for g in range(ngroups):
        _inproj_rows(x_ref, w_ref, convw_ref, prow_ref, lng_ref, lnb_ref, ones_ref,
                     q_ref, k_ref, v_ref, z_ref, u_ref, vln_ref, gcol_ref, grow_ref, ext[g],
                     ext[(g + 1) % ngroups], r0=g * gm, gm=gm)


def _inproj_rows(x_ref, w_ref, convw_ref, prow_ref, lng_ref, lnb_ref, ones_ref,
                 q_ref, k_ref, v_ref, z_ref, u_ref, vln_ref, gcol_ref, grow_ref, ext, ext_next, *, r0, gm):
    W = DN_WIDTH
    rows = slice(r0, r0 + gm)
    xb = x_ref[0, rows, :].astype(bf16)
    for part, e_ref in enumerate(ext):
        e_ref[8:8 + gm, :] = jnp.dot(xb, w_ref[:, part * W:(part + 1) * W], preferred_element_type=f32)
    zc = 3 * W
    uc = zc + W
    vc = uc + SGU_WIDTH
    bc = vc + SGU_WIDTH

    def conv_silu(part):
        e_ref = ext[part]
        cs = slice(part * W, (part + 1) * W)
        y = convw_ref[3:4, cs] * e_ref[8:8 + gm, :]
        for j in range(CONV_K - 1):
            y = y + convw_ref[j:j + 1, cs] * e_ref[5 + j:5 + j + gm, :]
        ext_next[part][0:8, :] = e_ref[gm:gm + 8, :]
        return _silu(y)

    def group_sums(a):
        return jnp.dot(a.astype(bf16), ones_ref[...], preferred_element_type=f32)

    yq = conv_silu(0)
    ssq = group_sums(yq * yq)
    pba = jnp.dot(xb, w_ref[:, bc:bc + 128], preferred_element_type=f32)
    pv = jnp.dot(xb, w_ref[:, vc:vc + SGU_WIDTH], preferred_element_type=f32)
    yk = conv_silu(1)
    ssk = group_sums(yk * yk)
    pu = jnp.dot(xb, w_ref[:, uc:uc + SGU_WIDTH], preferred_element_type=f32)
    pz = jnp.dot(xb, w_ref[:, zc:zc + W], preferred_element_type=f32)
    q_ref[0, rows, :] = (yq * (lax.rsqrt(ssq + RMS_EPS) * HEAD_DIM ** -0.5)).astype(bf16)
    k_ref[0, rows, :] = (yk * lax.rsqrt(ssk + RMS_EPS)).astype(bf16)
    v_ref[0, rows, :] = conv_silu(2).astype(bf16)

    z_ref[0, rows, :] = _silu(pz).astype(bf16)

    u_ref[0, rows, :] = _gelu_tanh(pu).astype(bf16)
    pv = _gelu_tanh(pv)
    for g in range(SGU_GROUPS):
        sl = slice(g * SGU_CHUNK, (g + 1) * SGU_CHUNK)
        vg = pv[:, sl]
        mu = jnp.mean(vg, axis=-1, keepdims=True)
        vcn = vg - mu
        var = jnp.mean(vcn * vcn, axis=-1, keepdims=True)
        vln_ref[0, rows, sl] = (vcn * lax.rsqrt(var + LN_EPS) * lng_ref[:, sl] + lnb_ref[:, sl]).astype(bf16)

    lane = _iota2((DN_CHUNK, 128), 1)
    beta = _sigmoid(pba)
    lane8 = _iota2((8, DN_CHUNK), 1)
    sub8 = _iota2((8, DN_CHUNK), 0)
    for c in range(gm // DN_CHUNK):
        rs = slice(c * DN_CHUNK, (c + 1) * DN_CHUNK)
        os_ = slice(r0 + c * DN_CHUNK, r0 + (c + 1) * DN_CHUNK)
        pbat = pba[rs].T[0:8, :]
        gt = -jnp.exp(prow_ref[0]) * _softplus(pbat + prow_ref[1])
        gc = jnp.where(sub8 >= DN_HEADS, gt, 0.0)
        shift = 1
        while shift < DN_CHUNK:
            gc = gc + jnp.where(lane8 >= shift, pltpu.roll(gc, shift, axis=1), 0.0)
            shift *= 2
        grow_ref[0, :, os_] = gc
        gc_col = jnp.concatenate([gc, jnp.zeros((DN_CHUNK - 8, DN_CHUNK), f32)], axis=0).T
        gcol_ref[0, os_, :] = jnp.where(lane < DN_HEADS, beta[rs], gc_col)


def _stage_inproj(x, w_re, conv_w, prow, lng, lnb, *, tm, ngroups):
    B, T, _ = x.shape
    wcols = w_re.shape[1]
    grid = (B, T // tm)
    gi = lax.broadcasted_iota(i32, (DN_WIDTH, DN_WIDTH), 0) // 128
    gj = lax.broadcasted_iota(i32, (DN_WIDTH, DN_WIDTH), 1) // 128
    group_ones = (gi == gj).astype(bf16)
    act = lambda: jax.ShapeDtypeStruct((B, T, DN_WIDTH), bf16)
    act_spec = lambda: pl.BlockSpec((1, tm, DN_WIDTH), lambda b, t: (b, t, 0))
    const2 = lambda shp: pl.BlockSpec(shp, lambda b, t: (0, 0))
    return pl.pallas_call(
        functools.partial(_inproj_body, tm=tm, ngroups=ngroups),
        grid=grid,
        in_specs=[
            pl.BlockSpec((1, tm, D_MODEL), lambda b, t: (b, t, 0)),
            const2((D_MODEL, wcols)),
            const2((CONV_K, 3 * DN_WIDTH)),
            pl.BlockSpec((2, 8, 128), lambda b, t: (0, 0, 0)),
            const2((1, SGU_WIDTH)),
            const2((1, SGU_WIDTH)),
            const2((DN_WIDTH, DN_WIDTH)),
        ],
        out_specs=[act_spec() for _ in range(6)] + [
            pl.BlockSpec((1, tm, 128), lambda b, t: (b, t, 0)),
            pl.BlockSpec((1, 8, tm), lambda b, t: (b, 0, t)),
        ],
        out_shape=[act() for _ in range(6)] + [
            jax.ShapeDtypeStruct((B, T, 128), f32),
            jax.ShapeDtypeStruct((B, 8, T), f32),
        ],
        scratch_shapes=[pltpu.VMEM((tm // ngroups + 8, DN_WIDTH), f32) for _ in range(3 * ngroups)],
        compiler_params=_cparams(("arbitrary", "arbitrary")),
        name="inproj",
    )(x, w_re, conv_w, prow, lng, lnb, group_ones)


def _mm(a, b):
    return jnp.dot(a.astype(bf16), b.astype(bf16), preferred_element_type=f32)


def _unit_lower_inverse(nmats, ii, jj):
    n = nmats[0].shape[0]
    eye = (ii == jj).astype(f32)
    leaf = jnp.right_shift(ii, 3) == jnp.right_shift(jj, 3)
    dblk = [jnp.where(leaf, m, 0.0) for m in nmats]
    s1 = [_mm(d, d) for d in dblk]
    r1 = [eye - d for d in dblk]
    both = [_mm(s, jnp.concatenate([s, r], axis=1)) for s, r in zip(s1, r1)]
    r2 = [r + bo[:, n:] for r, bo in zip(r1, both)]
    xs = [r + _mm(bo[:, :n], r) for r, bo in zip(r2, both)]
    size = 8
    while size < n:
        lows = [slice(r + size, r + 2 * size) for r in range(0, n, 2 * size)]
        ups = [slice(r, r + size) for r in range(0, n, 2 * size)]
        rsel = _iota2((n // 2, n), 0)
        ilow = rsel + size * (jnp.right_shift(rsel, size.bit_length() - 1) + 1)
        jlow = _iota2((n // 2, n), 1)
        in_pair_upper = (jlow >= ilow - (ilow & (2 * size - 1))) & (jlow < ilow - (ilow & (size - 1)))
        zeros = jnp.zeros((size, n), f32)
        new_xs = []
        ylows = [_mm(jnp.where(in_pair_upper, jnp.concatenate([m[s] for s in lows], axis=0), 0.0), x)
                 for m, x in zip(nmats, xs)]
        yfull = [jnp.concatenate([piece for k in range(len(lows)) for piece in (zeros, y[k * size:(k + 1) * size])], axis=0)
                 for y in ylows]
        corr = [_mm(jnp.concatenate([x[s] for s in lows], axis=0), yf) for x, yf in zip(xs, yfull)]
        for x, c in zip(xs, corr):
            pieces = []
            for k, (u, l) in enumerate(zip(ups, lows)):
                pieces += [x[u], x[l] - c[k * size:(k + 1) * size]]
            new_xs.append(jnp.concatenate(pieces, axis=0))
        xs = new_xs
        size *= 2
    return xs


def _deltanet_body(q_ref, k_ref, v_ref, z_ref, gcol_ref, grow_ref, nw_ref, y_ref, s_ref, *, nbr, nch, ngroups):
    @pl.when(pl.program_id(1) == 0)
    def _():
        s_ref[...] = jnp.zeros(s_ref.shape, f32)

    per = nbr // ngroups
    for g in range(ngroups):
        _deltanet_rows(q_ref, k_ref, v_ref, z_ref, gcol_ref, grow_ref, nw_ref, y_ref, s_ref,
                       rows=range(g * per, (g + 1) * per), nch=nch)


def _deltanet_rows(q_ref, k_ref, v_ref, z_ref, gcol_ref, grow_ref, nw_ref, y_ref, s_ref, *, rows, nch):
    C = DN_CHUNK
    S = [(b, h) for b in rows for h in range(DN_HEADS)]
    P = [(b, c, h) for c in range(nch) for b, h in S]

    ii = _iota2((C, C), 0)
    jj = _iota2((C, C), 1)
    causal = ii >= jj
    rs = [slice(c * C, (c + 1) * C) for c in range(nch)]
    hs = [slice(h * HEAD_DIM, (h + 1) * HEAD_DIM) for h in range(DN_HEADS)]
    gcol = {(b, c): gcol_ref[b, rs[c], :] for b in rows for c in range(nch)}
    qh = {(b, c, h): q_ref[b, rs[c], hs[h]] for b, c, h in P}
    kh = {(b, c, h): k_ref[b, rs[c], hs[h]] for b, c, h in P}
    vh = {(b, c, h): v_ref[b, rs[c], hs[h]] for b, c, h in P}
    gc_b = {(b, c, h): jnp.broadcast_to(gcol[b, c][:, DN_HEADS + h:DN_HEADS + h + 1], (C, HEAD_DIM)) for b, c, h in P}
    beta_b = {(b, c, h): jnp.broadcast_to(gcol[b, c][:, h:h + 1], (C, HEAD_DIM)) for b, c, h in P}
    gc_r = {(b, c, h): jnp.broadcast_to(grow_ref[b, DN_HEADS + h:DN_HEADS + h + 1, rs[c]], (C, C)) for b, c, h in P}
    decay = {p: jnp.exp(jnp.where(causal, gc_b[p] - gc_r[p], -1e30)) for p in P}

    kf = {p: kh[p].astype(f32) for p in P}
    kb = {p: kf[p] * beta_b[p] for p in P}
    kk = {p: lax.dot_general(kb[p].astype(bf16), kh[p], NT_DIMS, preferred_element_type=f32) for p in P}
    a_intra = {p: lax.dot_general(qh[p], kh[p], NT_DIMS, preferred_element_type=f32) * decay[p] for p in P}
    nmat = [jnp.where(ii > jj, kk[p] * decay[p], 0.0) for p in P]
    tinv = dict(zip(P, _unit_lower_inverse(nmat, ii, jj)))

    eg = {p: jnp.exp(gc_b[p]) for p in P}
    rhs = {p: jnp.concatenate([vh[p].astype(f32) * beta_b[p], kb[p] * eg[p]], axis=1) for p in P}
    sol = {p: _mm(tinv[p], rhs[p]) for p in P}

    q_dec = {p: qh[p].astype(f32) * eg[p] for p in P}
    g_last = {p: gc_b[p][C - 1:C, :] for p in P}
    kdt = {p: (kf[p] * jnp.exp(g_last[p] - gc_b[p])).T for p in P}

    state = {(b, h): s_ref[b * DN_HEADS + h] for b, h in S}
    for c in range(nch):
        m1 = {(b, h): _mm(jnp.concatenate([sol[b, c, h][:, HEAD_DIM:], q_dec[b, c, h]], axis=0), state[b, h]) for b, h in S}
        v_new = {(b, h): sol[b, c, h][:, :HEAD_DIM] - m1[b, h][:C] for b, h in S}
        m2 = {(b, h): _mm(jnp.concatenate([a_intra[b, c, h], kdt[b, c, h]], axis=0), v_new[b, h]) for b, h in S}
        state = {(b, h): state[b, h] * jnp.exp(g_last[b, c, h]) + m2[b, h][C:] for b, h in S}
        for b, h in S:
            o = m1[b, h][C:] + m2[b, h][:C]
            rms = lax.rsqrt(jnp.mean(o * o, axis=-1, keepdims=True) + RMS_EPS)
            y_ref[b, rs[c], hs[h]] = (o * rms * nw_ref[...] * z_ref[b, rs[c], hs[h]].astype(f32)).astype(bf16)
    for b, h in S:
        s_ref[b * DN_HEADS + h] = state[b, h]


def _stage_deltanet(q, k, v, z, gcol, grow, norm_w, *, nbr, nch, ngroups):
    B, T, _ = q.shape
    tt = nch * DN_CHUNK
    act_spec = lambda: pl.BlockSpec((nbr, tt, DN_WIDTH), lambda b, t: (b, t, 0))
    return pl.pallas_call(
        functools.partial(_deltanet_body, nbr=nbr, nch=nch, ngroups=ngroups),
        grid=(B // nbr, T // tt),
        in_specs=[act_spec(), act_spec(), act_spec(), act_spec(),
                  pl.BlockSpec((nbr, tt, 128), lambda b, t: (b, t, 0)),
                  pl.BlockSpec((nbr, 8, tt), lambda b, t: (b, 0, t)),
                  pl.BlockSpec((1, HEAD_DIM), lambda b, t: (0, 0))],
        out_specs=act_spec(),
        out_shape=jax.ShapeDtypeStruct((B, T, DN_WIDTH), bf16),
        scratch_shapes=[pltpu.VMEM((nbr * DN_HEADS, HEAD_DIM, HEAD_DIM), f32)],
        compiler_params=_cparams(("arbitrary", "arbitrary")),
        name="deltanet",
    )(q, k, v, z, gcol, grow, norm_w)


def _mixout_body(ydn_ref, u_ref, vln_ref, x_hbm, ws_ref, bsp_ref, wout_ref, g1_ref, b1_ref, wrt_ref, brt_ref,
                 h_ref, hrow_ref, ids_ref, wts_ref, cnt_ref, ycat_ref, xring_ref, xsem, *, tm, nsteps):
    C = SGU_CHUNK
    step = pl.program_id(0) * pl.num_programs(1) + pl.program_id(1)
    slot = lax.rem(step, X_SLOTS)

    def x_copy(s, sl):
        return pltpu.make_async_copy(x_hbm.at[s], xring_ref.at[sl], xsem.at[sl])

    @pl.when(step == 0)
    def _():
        cnt_ref[...] = jnp.zeros(cnt_ref.shape, f32)
        for j in range(X_SLOTS - 1):
            x_copy(j, j).start()

    ahead = step + (X_SLOTS - 1)

    @pl.when(ahead < nsteps)
    def _():
        x_copy(ahead, lax.rem(ahead, X_SLOTS)).start()

    x_copy(step, slot).wait()
    ii = _iota2((C, C), 0)
    jj = _iota2((C, C), 1)
    ycat_ref[:, 0:DN_WIDTH] = ydn_ref[0]
    for g in range(SGU_GROUPS):
        gs = slice(g * C, (g + 1) * C)
        wsg = jnp.where(ii >= jj, ws_ref[g], 0.0).astype(bf16)
        for c in range(tm // C):
            rs = slice(c * C, (c + 1) * C)
            mixed = jnp.dot(wsg, vln_ref[0, rs, gs], preferred_element_type=f32) + bsp_ref[:, gs]
            ycat_ref[rs, DN_WIDTH + g * C:DN_WIDTH + (g + 1) * C] = (u_ref[0, rs, gs].astype(f32) * mixed).astype(bf16)

    RB = 128
    blocks = [slice(r, r + RB) for r in range(0, tm, RB)]
    mix = [jnp.dot(ycat_ref[rb, :], wout_ref[...], preferred_element_type=f32) for rb in blocks]
    h1s = []
    for rb, m in zip(blocks, mix):
        hp = DEEPNORM_ALPHA * xring_ref[slot, rb, :] + m
        mu = jnp.mean(hp, axis=-1, keepdims=True)
        hc = hp - mu
        var = jnp.mean(hc * hc, axis=-1, keepdims=True)
        h1 = hc * lax.rsqrt(var + LN_EPS) * g1_ref[...] + b1_ref[...]
        h_ref[0, rb, :] = h1
        h1b = h1.astype(bf16)
        hrow_ref[rb] = h1b.reshape(RB, ROW_TILE, 128)
        h1s.append(h1b)

    logit_blocks = [lax.dot_general(wrt_ref[...], hb, NT_DIMS, preferred_element_type=f32) + brt_ref[...] for hb in h1s]
    sub = _iota2((8, RB), 0)
    subf = sub.astype(f32)
    sub_e = _iota2((N_EXPERTS, RB), 0).astype(f32)
    chosen = []
    for rb, logits in zip(blocks, logit_blocks):
        gl = logits[0:8]
        gmax = jnp.max(gl, axis=0, keepdims=True)
        g_idx = jnp.min(jnp.where(gl == gmax, subf, float(MOE_GROUPS)), axis=0, keepdims=True)
        p_group = 1.0 / jnp.sum(jnp.exp(gl - gmax), axis=0, keepdims=True)
        within = jnp.zeros((8, RB), f32)
        for g in range(MOE_GROUPS):
            within = within + jnp.where(g_idx == float(g), logits[8 + 8 * g:16 + 8 * g], 0.0)
        m1 = jnp.max(within, axis=0, keepdims=True)
        i1 = jnp.min(jnp.where(within == m1, subf, float(EXPERTS_PER_GROUP)), axis=0, keepdims=True)
        rest = jnp.where(subf == i1, -jnp.inf, within)
        m2 = jnp.max(rest, axis=0, keepdims=True)
        i2 = jnp.min(jnp.where(rest == m2, subf, float(EXPERTS_PER_GROUP)), axis=0, keepdims=True)
        e = jnp.exp(m2 - m1)
        w1 = p_group / (1.0 + e)
        w2 = p_group * e / (1.0 + e)
        e1 = g_idx * float(EXPERTS_PER_GROUP) + i1
        e2 = g_idx * float(EXPERTS_PER_GROUP) + i2
        ids_ref[:, rb] = jnp.where(sub == 0, e1, jnp.where(sub == 1, e2, 0.0)).astype(i32)
        wts_ref[:, rb] = jnp.where(sub == 0, w1, jnp.where(sub == 1, w2, 0.0))
        chosen.append(((sub_e == e1).astype(f32) + (sub_e == e2).astype(f32)).astype(bf16))
    ones = jnp.ones((RB, 128), bf16)
    cnt_ref[...] = cnt_ref[...] + sum(jnp.dot(oh, ones, preferred_element_type=f32) for oh in chosen)


def _stage_mixout(ydn, u, vln, x, ws, bsp, wout, g1, b1, wrt, brt, *, tm):
    B, T, _ = x.shape
    nt = T // tm
    act_spec = lambda: pl.BlockSpec((1, tm, DN_WIDTH), lambda b, t: (b, t, 0))
    const2 = lambda shp: pl.BlockSpec(shp, lambda b, t: (0, 0))
    tok_spec = lambda: pl.BlockSpec((8, tm), lambda b, t: (0, b * nt + t))
    return pl.pallas_call(
        functools.partial(_mixout_body, tm=tm, nsteps=B * nt),
        grid=(B, nt),
        in_specs=[act_spec(), act_spec(), act_spec(),
                  pl.BlockSpec(memory_space=pl.ANY),
                  pl.BlockSpec((SGU_GROUPS, SGU_CHUNK, SGU_CHUNK), lambda b, t: (0, 0, 0)),
                  const2((SGU_CHUNK, SGU_WIDTH)),
                  const2((D_MODEL, D_MODEL)),
                  const2((1, D_MODEL)), const2((1, D_MODEL)),
                  const2((128, D_MODEL)), const2((128, 128))],
        out_specs=[pl.BlockSpec((1, tm, D_MODEL), lambda b, t: (b, t, 0)),
                   pl.BlockSpec((tm, ROW_TILE, 128), lambda b, t: (b * nt + t, 0, 0)), tok_spec(), tok_spec(),
                   const2((N_EXPERTS, 128))],
        out_shape=[jax.ShapeDtypeStruct((B, T, D_MODEL), f32),
                   jax.ShapeDtypeStruct((B * T, ROW_TILE, 128), bf16),
                   jax.ShapeDtypeStruct((8, B * T), i32),
                   jax.ShapeDtypeStruct((8, B * T), f32),
                   jax.ShapeDtypeStruct((N_EXPERTS, 128), f32)],
        scratch_shapes=[pltpu.VMEM((tm, D_MODEL), bf16), pltpu.VMEM((X_SLOTS, tm, D_MODEL), f32),
                        pltpu.SemaphoreType.DMA((X_SLOTS,))],
        compiler_params=_cparams(("arbitrary", "arbitrary")),
        name="mixout",
    )(ydn, u, vln, x.reshape(B * nt, tm, D_MODEL), ws, bsp, wout, g1, b1, wrt, brt)


def _route_body(cnt_ref, ids_ref, dest_ref, meta_ref, blk_ref, pstart_ref, *, tm, nb_pad):
    def ranges_and_block_table():
        cnt = cnt_ref[...]
        padded = jnp.floor((cnt + (MOE_BLOCK - 1)) * (1.0 / MOE_BLOCK)) * MOE_BLOCK
        ei = _iota2((N_EXPERTS, N_EXPERTS), 0)
        ej = _iota2((N_EXPERTS, N_EXPERTS), 1)
        pends = jnp.dot((ei >= ej).astype(f32), padded, precision=HIGHEST, preferred_element_type=f32)
        pstart = pends - padded
        pstart_ref[...] = pstart
        s64 = _iota2((N_EXPERTS, 128), 0)
        l64 = _iota2((N_EXPERTS, 128), 1)
        diag = s64 == l64
        fill_off = jnp.sum(jnp.where(diag, pstart + cnt, 0.0), axis=0, keepdims=True)
        fill_n = jnp.sum(jnp.where(diag, padded - cnt, 0.0), axis=0, keepdims=True)
        nused = pends[N_EXPERTS - 1:N_EXPERTS, :] * (1.0 / MOE_BLOCK)
        m8 = _iota2((8, 128), 0)
        meta_ref[...] = jnp.where(m8 == 0, fill_off, jnp.where(m8 == 1, fill_n, jnp.where(m8 == 2, nused, 0.0))).astype(i32)
        bstart = (_iota2((N_EXPERTS, nb_pad), 1) * MOE_BLOCK).astype(f32)
        pe = jnp.concatenate([pends] * (nb_pad // 128), axis=1)
        be = jnp.sum((pe <= bstart).astype(f32), axis=0, keepdims=True)
        be = jnp.minimum(be, float(N_EXPERTS - 1))
        blk_ref[...] = jnp.broadcast_to(be, (8, nb_pad)).astype(i32)

    ranges_and_block_table()

    C = 128
    earlier = (_iota2((C, C), 0) < _iota2((C, C), 1)).astype(bf16)
    w2 = jnp.concatenate([earlier, jnp.ones((C, C), bf16)], axis=1)
    sub = _iota2((N_EXPERTS, C), 0)
    sub8 = _iota2((8, C), 0)

    def tile(t, carry):
        run = pstart_ref[...]
        for k in range(tm // C):
            ls = pl.ds(pl.multiple_of(t * tm + k * C, C), C)
            is1 = sub == ids_ref[0:1, ls]
            is2 = sub == ids_ref[1:2, ls]
            oh = (is1.astype(f32) + is2.astype(f32)).astype(bf16)
            r = jnp.dot(oh, w2, preferred_element_type=f32)
            nxt = r[:, :C] + run
            d1 = jnp.sum(jnp.where(is1, nxt, 0.0), axis=0, keepdims=True)
            d2 = jnp.sum(jnp.where(is2, nxt, 0.0), axis=0, keepdims=True)
            dest_ref[:, ls] = jnp.where(sub8 == 0, d1, jnp.where(sub8 == 1, d2, 0.0)).astype(i32)
            run = run + r[:, C:]
        pstart_ref[...] = run
        return carry

    lax.fori_loop(0, ids_ref.shape[1] // tm, tile, 0)


def _stage_route(cnt, ids, *, tm, nb_pad):
    n = ids.shape[1]
    return pl.pallas_call(
        functools.partial(_route_body, tm=tm, nb_pad=nb_pad),
        grid=(1,),
        in_specs=[pl.BlockSpec((N_EXPERTS, 128), lambda i: (0, 0)), pl.BlockSpec((8, n), lambda i: (0, 0))],
        out_specs=[pl.BlockSpec((8, n), lambda i: (0, 0)),
                   pl.BlockSpec((8, 128), lambda i: (0, 0)),
                   pl.BlockSpec((8, nb_pad), lambda i: (0, 0))],
        out_shape=[jax.ShapeDtypeStruct((8, n), i32), jax.ShapeDtypeStruct((8, 128), i32),
                   jax.ShapeDtypeStruct((8, nb_pad), i32)],
        scratch_shapes=[pltpu.VMEM((N_EXPERTS, 128), f32)],
        compiler_params=_cparams(("arbitrary",)),
        name="moe_route",
    )(cnt, ids)


def _dispatch_body(fill_off_ref, fill_n_ref, nused_ref, dest_ref, h_hbm, xs_ref, src_ref, zero_ref, isem, sem, zsem,
                   *, tm, nsteps):
    i = pl.program_id(0)
    slot = lax.rem(i, X_SLOTS)

    def in_copy(step, s):
        return pltpu.make_async_copy(h_hbm.at[pl.ds(step * tm, tm)], src_ref.at[s], isem.at[s])

    def drain(s):
        for _ in range(2):
            pltpu.make_async_copy(src_ref.at[s], xs_ref.at[pl.ds(0, tm)], sem.at[s]).wait()

    @pl.when(i == 0)
    def _():
        for j in range(min(X_SLOTS - 1, nsteps)):
            in_copy(j, j).start()

    in_copy(i, slot).wait()

    def row_copy(t, d):
        return pltpu.make_async_copy(src_ref.at[slot, t], xs_ref.at[d], sem.at[slot])

    def issue(t, carry):
        row_copy(t, dest_ref[0, 0, t]).start(priority=0)
        row_copy(t, dest_ref[0, 1, t]).start(priority=1)
        return carry

    lax.fori_loop(0, tm, issue, 0, unroll=8)

    @pl.when(i == 0)
    def _():
        zero_ref[...] = jnp.zeros(zero_ref.shape, bf16)

        def fill(start):
            def body(e, carry):
                off = fill_off_ref[e]
                npad = fill_n_ref[e]
                bit = MOE_BLOCK // 2
                while bit:
                    @pl.when((npad & bit) != 0)
                    def _(off=off, bit=bit):
                        cp = pltpu.make_async_copy(zero_ref.at[pl.ds(0, bit)], xs_ref.at[pl.ds(off, bit)], zsem)
                        cp.start() if start else cp.wait()
                    off = off + (npad & bit)
                    bit //= 2
                return carry
            return body

        lax.fori_loop(0, N_EXPERTS, fill(True), 0)
        lax.fori_loop(0, N_EXPERTS, fill(False), 0)

        def tail_copy(b):
            return pltpu.make_async_copy(zero_ref, xs_ref.at[pl.ds(b * MOE_BLOCK, MOE_BLOCK)], zsem)

        nblocks = xs_ref.shape[0] // MOE_BLOCK
        lax.fori_loop(nused_ref[0], nblocks, lambda b, c: (tail_copy(b).start(), c)[1], 0)
        lax.fori_loop(nused_ref[0], nblocks, lambda b, c: (tail_copy(0).wait(), c)[1], 0)

    @pl.when(i >= 1)
    def _():
        drain(lax.rem(i - 1, X_SLOTS))

    ahead = i + (X_SLOTS - 1)

    @pl.when(ahead < nsteps)
    def _():
        in_copy(ahead, lax.rem(ahead, X_SLOTS)).start()

    @pl.when(i == nsteps - 1)
    def _():
        drain(slot)


def _stage_dispatch(fill_off, fill_n, nused, dest3, hrow, p_rows, *, tm):
    n = hrow.shape[0]
    nsteps = n // tm
    return pl.pallas_call(
        functools.partial(_dispatch_body, tm=tm, nsteps=nsteps),
        grid_spec=pltpu.PrefetchScalarGridSpec(
            num_scalar_prefetch=3,
            grid=(nsteps,),
            in_specs=[pl.BlockSpec((1, 2, tm), lambda i, fo, fn, nu: (i, 0, 0), memory_space=pltpu.SMEM),
                      pl.BlockSpec(memory_space=pl.ANY)],
            out_specs=pl.BlockSpec(memory_space=pl.ANY),
            scratch_shapes=[pltpu.VMEM((X_SLOTS, tm, ROW_TILE, 128), bf16),
                            pltpu.VMEM((MOE_BLOCK, ROW_TILE, 128), bf16),
                            pltpu.SemaphoreType.DMA((X_SLOTS,)), pltpu.SemaphoreType.DMA((X_SLOTS,)),
                            pltpu.SemaphoreType.DMA],
        ),
        out_shape=jax.ShapeDtypeStruct((p_rows, ROW_TILE, 128), bf16),
        compiler_params=_cparams(("arbitrary",)),
        name="moe_dispatch",
    )(fill_off, fill_n, nused, dest3, hrow)


def _experts_body(blk_ref, nused_ref, xs_hbm, wg_hbm, wu_hbm, wd_hbm, ys_hbm,
                  xbuf_ref, wg32_ref, wu32_ref, wd32_ref, wgu16_ref, wd16_ref, obuf_ref, zero_ref, xsem, wsem, osem, zsem,
                  *, nblocks):
    nused = nused_ref[0]

    def weight_copies(ex):
        return (pltpu.make_async_copy(wg_hbm.at[ex], wg32_ref, wsem.at[0]),
                pltpu.make_async_copy(wu_hbm.at[ex], wu32_ref, wsem.at[1]),
                pltpu.make_async_copy(wd_hbm.at[ex], wd32_ref, wsem.at[2]))

    def x_copy(block, s):
        return pltpu.make_async_copy(xs_hbm.at[pl.ds(block * MOE_BLOCK, MOE_BLOCK)], xbuf_ref.at[s], xsem.at[s])

    def out_copy(block, s):
        return pltpu.make_async_copy(obuf_ref.at[s], ys_hbm.at[pl.ds(block * MOE_BLOCK, MOE_BLOCK)], osem.at[s])

    def tail_copy(block):
        return pltpu.make_async_copy(zero_ref, ys_hbm.at[pl.ds(block * MOE_BLOCK, MOE_BLOCK)], zsem)

    zero_ref[...] = jnp.zeros(zero_ref.shape, bf16)
    lax.fori_loop(nused, nblocks, lambda b, c: (tail_copy(b).start(), c)[1], 0)

    @pl.when(nused > 0)
    def _():
        for cp in weight_copies(blk_ref[0]):
            cp.start()
        for j in range(X_SLOTS - 1):
            @pl.when(j < nused)
            def _(j=j):
                x_copy(j, j).start()

    def step(i, carry):
        e = blk_ref[i]
        slot = lax.rem(i, X_SLOTS)
        ahead = i + (X_SLOTS - 1)

        @pl.when(ahead < nused)
        def _():
            x_copy(ahead, lax.rem(ahead, X_SLOTS)).start()

        @pl.when((i == 0) | (e != blk_ref[jnp.maximum(i - 1, 0)]))
        def _():
            for cp in weight_copies(e):
                cp.wait()
            wgu16_ref[:, 0:D_EXPERT] = wg32_ref[...].astype(bf16)
            wgu16_ref[:, D_EXPERT:2 * D_EXPERT] = wu32_ref[...].astype(bf16)
            wd16_ref[...] = wd32_ref[...].astype(bf16)
            nxt = lax.while_loop(lambda j: (j < nused) & (blk_ref[jnp.minimum(j, nused - 1)] == e), lambda j: j + 1,
                                 i + 1)

            @pl.when(nxt < nused)
            def _():
                for cp in weight_copies(blk_ref[jnp.minimum(nxt, nused - 1)]):
                    cp.start(priority=1)

        @pl.when(i >= X_SLOTS)
        def _():
            out_copy(i - X_SLOTS, slot).wait()

        x_copy(i, slot).wait()
        half = MOE_BLOCK // 2
        rows = [slice(p * half, (p + 1) * half) for p in range(2)]
        gu = [jnp.dot(xbuf_ref[slot, r].reshape(half, D_MODEL), wgu16_ref[...], preferred_element_type=f32)
              for r in rows]
        hid = [(_silu(g[:, :D_EXPERT]) * g[:, D_EXPERT:]).astype(bf16) for g in gu]
        y = [jnp.dot(hd, wd16_ref[...], preferred_element_type=f32) for hd in hid]
        for r, yp in zip(rows, y):
            obuf_ref[slot, r] = yp.astype(bf16).reshape(half, ROW_TILE, 128)
        out_copy(i, slot).start()
        return carry

    lax.fori_loop(0, nused, step, 0)

    for j in range(X_SLOTS):
        last = nused - 1 - j

        @pl.when(last >= 0)
        def _(last=last):
            out_copy(last, lax.rem(last, X_SLOTS)).wait()

    lax.fori_loop(nused, nblocks, lambda b, c: (tail_copy(b).wait(), c)[1], 0)


def _stage_experts(blk_e, nused, xs, w_gate, w_up, w_down):
    p_rows = xs.shape[0]
    nb = p_rows // MOE_BLOCK

    return pl.pallas_call(
        functools.partial(_experts_body, nblocks=nb),
        grid_spec=pltpu.PrefetchScalarGridSpec(
            num_scalar_prefetch=2,
            grid=(1,),
            in_specs=[pl.BlockSpec(memory_space=pl.ANY),
                      pl.BlockSpec(memory_space=pl.ANY),
                      pl.BlockSpec(memory_space=pl.ANY),
                      pl.BlockSpec(memory_space=pl.ANY)],
            out_specs=pl.BlockSpec(memory_space=pl.ANY),
            scratch_shapes=[pltpu.VMEM((X_SLOTS, MOE_BLOCK, ROW_TILE, 128), bf16),
                            pltpu.VMEM((D_MODEL, D_EXPERT), f32), pltpu.VMEM((D_MODEL, D_EXPERT), f32),
                            pltpu.VMEM((D_EXPERT, D_MODEL), f32),
                            pltpu.VMEM((D_MODEL, 2 * D_EXPERT), bf16), pltpu.VMEM((D_EXPERT, D_MODEL), bf16),
                            pltpu.VMEM((X_SLOTS, MOE_BLOCK, ROW_TILE, 128), bf16),
                            pltpu.VMEM((MOE_BLOCK, ROW_TILE, 128), bf16),
                            pltpu.SemaphoreType.DMA((X_SLOTS,)), pltpu.SemaphoreType.DMA((3,)),
                            pltpu.SemaphoreType.DMA((X_SLOTS,)), pltpu.SemaphoreType.DMA],
        ),
        out_shape=jax.ShapeDtypeStruct((p_rows, ROW_TILE, 128), bf16),
        compiler_params=_cparams(("arbitrary",)),
        name="moe_experts",
    )(blk_e, nused, xs, w_gate, w_up, w_down)


def _combine_body(dcur_ref, dnext_ref, h_ref, wts_ref, g2_ref, b2_ref, ys_ref, o_ref, ybuf_ref, sem, *, tm, nsteps):
    i = pl.program_id(0)
    slot = lax.rem(i, 2)

    def issue_tile(d_ref, s):
        def body(t, carry):
            for k in range(2):
                pltpu.make_async_copy(ys_ref.at[d_ref[0, k, t]], ybuf_ref.at[s, k, t], sem.at[s]).start(priority=k)
            return carry

        lax.fori_loop(0, tm, body, 0, unroll=8)

    @pl.when(i == 0)
    def _():
        issue_tile(dcur_ref, 0)

    @pl.when(i + 1 < nsteps)
    def _():
        issue_tile(dnext_ref, 1 - slot)

    for k in range(2):
        pltpu.make_async_copy(ys_ref.at[pl.ds(0, tm)], ybuf_ref.at[slot, k], sem.at[slot]).wait()

    pieces = []
    for c in range(tm // 128):
        ls = slice(c * 128, (c + 1) * 128)
        w1c = jnp.broadcast_to(wts_ref[0:1, ls], (128, 128)).T
        w2c = jnp.broadcast_to(wts_ref[1:2, ls], (128, 128)).T
        w1f = jnp.concatenate([w1c] * (D_MODEL // 128), axis=1)
        w2f = jnp.concatenate([w2c] * (D_MODEL // 128), axis=1)
        y1 = ybuf_ref[slot, 0, ls].reshape(128, D_MODEL).astype(f32)
        y2 = ybuf_ref[slot, 1, ls].reshape(128, D_MODEL).astype(f32)
        pieces.append(w1f * y1 + w2f * y2)
    ffn = jnp.concatenate(pieces, axis=0)
    hp = DEEPNORM_ALPHA * h_ref[...] + ffn
    mu = jnp.mean(hp, axis=-1, keepdims=True)
    hc = hp - mu
    var = jnp.mean(hc * hc, axis=-1, keepdims=True)
    o_ref[...] = hc * lax.rsqrt(var + LN_EPS) * g2_ref[...] + b2_ref[...]


def _stage_combine(dest3, h2, wts, g2, b2, ys, *, tm):
    n = h2.shape[0]
    nsteps = n // tm
    return pl.pallas_call(
        functools.partial(_combine_body, tm=tm, nsteps=nsteps),
        grid=(nsteps,),
        in_specs=[pl.BlockSpec((1, 2, tm), lambda i: (i, 0, 0), memory_space=pltpu.SMEM),
                  pl.BlockSpec((1, 2, tm), lambda i: (jnp.minimum(i + 1, nsteps - 1), 0, 0), memory_space=pltpu.SMEM),
                  pl.BlockSpec((tm, D_MODEL), lambda i: (i, 0)),
                  pl.BlockSpec((8, tm), lambda i: (0, i)),
                  pl.BlockSpec((1, D_MODEL), lambda i: (0, 0)),
                  pl.BlockSpec((1, D_MODEL), lambda i: (0, 0)),
                  pl.BlockSpec(memory_space=pl.ANY)],
        out_specs=pl.BlockSpec((tm, D_MODEL), lambda i: (i, 0)),
        out_shape=jax.ShapeDtypeStruct((n, D_MODEL), f32),
        scratch_shapes=[pltpu.VMEM((2, 2, tm, ROW_TILE, 128), bf16), pltpu.SemaphoreType.DMA((2,))],
        compiler_params=_cparams(("arbitrary",)),
        name="moe_combine",
    )(dest3, dest3, h2, wts, g2, b2, ys)


def _layer(h, w_in, conv_w, a_log, dt_bias, dn_norm_w, sgu_ln_g, sgu_ln_b, w_spatial, b_spatial, w_out,
           ln1_g, ln1_b, w_rg, b_rg, w_re, b_re, w_gate, w_up, w_down, ln2_g, ln2_b,
           *, tm_in, in_groups, dn_rows, dn_chunks, dn_groups, tm_mix, tm_rank, tm_disp, tm_comb):
    B, T, _ = h.shape
    n = B * T
    w_cols = _stage_weight_layout(w_in)
    decay_prm = jnp.stack([a_log, dt_bias])
    prow = jnp.broadcast_to(jnp.pad(decay_prm, ((0, 0), (DN_HEADS, 8 - 2 * DN_HEADS)))[:, :, None], (2, 8, 128))

    q, k, v, z, u, vln, gcol, grow = _stage_inproj(
        h, w_cols, conv_w, prow, sgu_ln_g[None, :], sgu_ln_b[None, :], tm=tm_in, ngroups=in_groups)
    ydn = _stage_deltanet(q, k, v, z, gcol, grow, dn_norm_w[None, :], nbr=dn_rows, nch=dn_chunks, ngroups=dn_groups)

    bsp = jnp.broadcast_to(b_spatial.T[:, :, None], (SGU_CHUNK, SGU_GROUPS, SGU_CHUNK)).reshape(SGU_CHUNK, SGU_WIDTH)
    n_logit = MOE_GROUPS + N_EXPERTS
    wrt = jnp.pad(jnp.concatenate([w_rg, w_re], axis=1).T, ((0, 128 - n_logit), (0, 0))).astype(bf16)
    brt = jnp.broadcast_to(jnp.pad(jnp.concatenate([b_rg, b_re]), (0, 128 - n_logit))[:, None], (128, 128))
    h1, hrow, ids, wts, cnt = _stage_mixout(ydn, u, vln, h, w_spatial, bsp, w_out.astype(bf16), ln1_g[None, :],
                                       ln1_b[None, :], wrt, brt, tm=tm_mix)

    p_rows = (-(-(n * 2) // MOE_BLOCK)) * MOE_BLOCK + N_EXPERTS * MOE_BLOCK
    nb = p_rows // MOE_BLOCK
    nb_pad = (-(-nb // 128)) * 128
    dest, meta, blk = _stage_route(cnt, ids, tm=tm_rank, nb_pad=nb_pad)

    h2 = h1.reshape(n, D_MODEL)
    dest_d = dest[0:2].reshape(2, n // tm_disp, tm_disp).transpose(1, 0, 2)
    xs = _stage_dispatch(meta[0, :N_EXPERTS], meta[1, :N_EXPERTS], meta[2, 0:1], dest_d, hrow, p_rows, tm=tm_disp)
    ys = _stage_experts(blk[0, :nb], meta[2, 0:1], xs, w_gate, w_up, w_down)
    dest_c = dest[0:2].reshape(2, n // tm_comb, tm_comb).transpose(1, 0, 2)
    out = _stage_combine(dest_c, h2, wts, ln2_g[None, :], ln2_b[None, :], ys, tm=tm_comb)
    return out.reshape(B, T, D_MODEL)


def kernel(x, w_in, conv_w, a_log, dt_bias, dn_norm_w, sgu_ln_g, sgu_ln_b, w_spatial, b_spatial, w_out, ln1_g, ln1_b, w_router_group, b_router_group, w_router_expert, b_router_expert, w_gate, w_up, w_down, ln2_g, ln2_b):
    h = x
    for l in range(w_in.shape[0]):
        h = _layer(h, w_in[l], conv_w[l], a_log[l], dt_bias[l], dn_norm_w[l], sgu_ln_g[l], sgu_ln_b[l],
                   w_spatial[l], b_spatial[l], w_out[l], ln1_g[l], ln1_b[l],
                   w_router_group[l], b_router_group[l], w_router_expert[l], b_router_expert[l],
                   w_gate[l], w_up[l], w_down[l], ln2_g[l], ln2_b[l],
                   tm_in=512, in_groups=1, dn_rows=8, dn_chunks=2, dn_groups=4, tm_mix=1024, tm_rank=1024, tm_disp=2048, tm_comb=512)
    return h
```
